```python
import math
import jax, jax.numpy as jnp
from jax import lax
import numpy as np

D_MODEL = 1024
BATCH = 4
SEQ = 4096
DEPTH = 1

CTX_LEN = 256
GRID_W = 64

MLA_HEADS = 8
QK_NOPE = 64
QK_ROPE = 32
V_DIM = 64
Q_LORA = 256
KV_LORA = 128
MLA_W = MLA_HEADS * V_DIM
ROPE_BASE = 10000.0
ROPE_FREQS_PER_AXIS = QK_ROPE // 4
Q_BLOCK = 128

POOL_WINDOWS = (2, 4, 8, 16)
POOL_GROUPS = 4
POOL_CH = 128
POOL_W = POOL_GROUPS * POOL_CH

IN_W = Q_LORA + KV_LORA + QK_ROPE + POOL_W
MIX_W = MLA_W + POOL_W

N_EXPERTS = 32
TOP_K = 4
D_EXPERT = 1024
SWIGLU_LIMIT = 7.0
SWIGLU_ALPHA = 1.702
MOE_BLOCK = 256

N_MOD = 6
EPS = 1e-6

kernel_name = "hybrid_mla_pool_moe_dit_block"


def _rmsnorm(x, g):
    xf = x.astype(jnp.float32)
    y = xf * lax.rsqrt(jnp.mean(xf * xf, axis=-1, keepdims=True) + EPS)
    return (y * g.astype(jnp.float32)).astype(x.dtype)


def _modulate(x, g, shift, scale):
    return _rmsnorm(x, g) * (1 + scale) + shift


def _axial_rope_angles(rows):
    row = jnp.repeat(jnp.arange(rows, dtype=jnp.float32), GRID_W)
    col = jnp.tile(jnp.arange(GRID_W, dtype=jnp.float32), rows)
    freqs = ROPE_BASE ** (-jnp.arange(ROPE_FREQS_PER_AXIS, dtype=jnp.float32) / ROPE_FREQS_PER_AXIS)
    ang = jnp.concatenate([row[:, None] * freqs, col[:, None] * freqs], axis=-1)
    return jnp.cos(ang), jnp.sin(ang)


def _apply_rope(x, cos, sin):
    half = QK_ROPE // 2
    xf = x.astype(jnp.float32)
    x1, x2 = xf[..., :half], xf[..., half:]
    return jnp.concatenate([x1 * cos - x2 * sin, x2 * cos + x1 * sin], axis=-1).astype(x.dtype)


def _queries(cq, q_norm_g, w_uq):
    B, L, _ = cq.shape
    q = (_rmsnorm(cq, q_norm_g) @ w_uq).reshape(B, L, MLA_HEADS, QK_NOPE + QK_ROPE)
    return q[..., :QK_NOPE], q[..., QK_NOPE:]


def _keys_values(ckv, k_rope, kv_norm_g, w_ukv):
    B, L, _ = ckv.shape
    kv = (_rmsnorm(ckv, kv_norm_g) @ w_ukv).reshape(B, L, MLA_HEADS, QK_NOPE + V_DIM)
    return kv[..., :QK_NOPE], k_rope, kv[..., QK_NOPE:]


def _attend(q_nope, q_rope, k_nope, k_rope, v):
    scale = 1.0 / math.sqrt(QK_NOPE + QK_ROPE)
    s = (jnp.einsum('bqhd,bkhd->bhqk', q_nope, k_nope)
         + jnp.einsum('bqhr,bkr->bhqk', q_rope, k_rope))
    p = jax.nn.softmax(s.astype(jnp.float32) * scale, axis=-1).astype(v.dtype)
    return jnp.einsum('bhqk,bkhd->bqhd', p, v)


def _blocked_attention(q_nope, q_rope, k_nope, k_rope, v):
    B, N = q_nope.shape[:2]
    nblk = N // Q_BLOCK

    def to_blocks(a):
        return a.reshape(B, nblk, Q_BLOCK, *a.shape[2:]).swapaxes(0, 1)

    def one_block(blk):
        qn, qr = blk
        return _attend(qn, qr, k_nope, k_rope, v)

    o = lax.map(one_block, (to_blocks(q_nope), to_blocks(q_rope)))
    return o.swapaxes(0, 1).reshape(B, N, MLA_W)


def _multiscale_pool(u, w_pool, pool_scale):
    B, L, _ = u.shape
    uf = u.astype(jnp.float32).reshape(B, L, POOL_GROUPS, POOL_CH)
    cs = jnp.concatenate([jnp.zeros((B, 1, POOL_GROUPS, POOL_CH), jnp.float32),
                          jnp.cumsum(uf, axis=1)], axis=1)
    t = jnp.arange(L, dtype=jnp.int32)[:, None]
    half = jnp.array(POOL_WINDOWS, dtype=jnp.int32)[None, :] // 2
    lo = jnp.clip(t - half, 0, L)
    hi = jnp.clip(t + half, 0, L)
    gidx = jnp.arange(POOL_GROUPS, dtype=jnp.int32)[None, :]
    window_sum = cs[:, hi, gidx] - cs[:, lo, gidx]
    count = (hi - lo).astype(jnp.float32)[None, :, :, None]
    mix = (window_sum / count - uf).astype(u.dtype)
    y = jnp.einsum('blgc,gcd->blgd', mix, w_pool).reshape(B, L, POOL_W)
    return y * pool_scale


def _moe(h, router_w, router_b, w_gate_up, b_gate_up, w_down, b_down):
    B, L, D = h.shape
    hf = h.reshape(B * L, D)
    T = hf.shape[0]
    logits = hf.astype(jnp.float32) @ router_w.astype(jnp.float32) + router_b.astype(jnp.float32)
    top_val, top_idx = lax.top_k(logits, TOP_K)
    gates = jax.nn.softmax(top_val, axis=-1)
    A = T * TOP_K
    expert = top_idx.reshape(A).astype(jnp.int32)
    token = jnp.repeat(jnp.arange(T, dtype=jnp.int32), TOP_K)
    weight = gates.reshape(A)
    order = jnp.argsort(expert)
    e_s, tok_s, w_s = expert[order], token[order], weight[order]
    counts = jnp.zeros((N_EXPERTS,), jnp.int32).at[expert].add(1)
    padded = (counts + MOE_BLOCK - 1) // MOE_BLOCK * MOE_BLOCK
    starts = jnp.cumsum(counts) - counts
    pad_ends = jnp.cumsum(padded)
    pad_starts = pad_ends - padded
    dest = pad_starts[e_s] + (jnp.arange(A, dtype=jnp.int32) - starts[e_s])
    n_blocks = -(-A // MOE_BLOCK) + N_EXPERTS
    P = n_blocks * MOE_BLOCK
    slot_tok = jnp.zeros((P,), jnp.int32).at[dest].set(tok_s)
    slot_w = jnp.zeros((P,), jnp.float32).at[dest].set(w_s)
    block_expert = jnp.minimum(
        jnp.searchsorted(pad_ends, jnp.arange(n_blocks, dtype=jnp.int32) * MOE_BLOCK, side='right'),
        N_EXPERTS - 1)

    def expert_block(args):
        tok, e = args
        xb = hf[tok]
        gu = xb @ w_gate_up[e] + b_gate_up[e]
        g, lin = gu[..., :D_EXPERT], gu[..., D_EXPERT:]
        g = jnp.minimum(g, SWIGLU_LIMIT)
        lin = jnp.clip(lin, -SWIGLU_LIMIT, SWIGLU_LIMIT)
        act = g * jax.nn.sigmoid(SWIGLU_ALPHA * g) * (lin + 1)
        return act @ w_down[e] + b_down[e]

    y = lax.map(expert_block, (slot_tok.reshape(n_blocks, MOE_BLOCK), block_expert))
    y = y.reshape(P, D) * slot_w[:, None].astype(y.dtype)
    out = jnp.zeros((T, D), h.dtype).at[slot_tok].add(y)
    return out.reshape(B, L, D)


def setup_inputs(seed: int = 0) -> dict:
    key = jax.random.key(seed)
    ks = jax.random.split(key, 24)
    f32 = jnp.float32
    D = D_MODEL

    def nrm(k, shape, scale):
        return jax.random.normal(k, shape, f32) * scale

    return {
        "x": nrm(ks[0], (BATCH, SEQ, D), 1.0),
        "c": nrm(ks[1], (BATCH, D), 1.0),
        "ctx": nrm(ks[2], (BATCH, CTX_LEN, D), 1.0),
        "c_ctx": nrm(ks[3], (D,), 1.0),
        "w_mod": nrm(ks[4], (DEPTH, D, N_MOD * D), 0.5 * D ** -0.5),
        "b_mod": nrm(ks[5], (DEPTH, N_MOD * D), 0.02),
        "norm1_g": 1.0 + nrm(ks[6], (DEPTH, D), 0.05),
        "w_in": nrm(ks[7], (DEPTH, D, IN_W), D ** -0.5),
        "q_norm_g": 1.0 + nrm(ks[8], (DEPTH, Q_LORA), 0.05),
        "kv_norm_g": 1.0 + nrm(ks[9], (DEPTH, KV_LORA), 0.05),
        "w_uq": nrm(ks[10], (DEPTH, Q_LORA, MLA_HEADS * (QK_NOPE + QK_ROPE)), Q_LORA ** -0.5),
        "w_ukv": nrm(ks[11], (DEPTH, KV_LORA, MLA_HEADS * (QK_NOPE + V_DIM)), KV_LORA ** -0.5),
        "w_pool": nrm(ks[12], (DEPTH, POOL_GROUPS, POOL_CH, POOL_CH), POOL_CH ** -0.5),
        "pool_scale": 1.0 + nrm(ks[13], (DEPTH, POOL_W), 0.1),
        "w_out": nrm(ks[14], (DEPTH, MIX_W, D), MIX_W ** -0.5),
        "norm2_g": 1.0 + nrm(ks[15], (DEPTH, D), 0.05),
        "router_w": nrm(ks[16], (DEPTH, D, N_EXPERTS), D ** -0.5),
        "router_b": nrm(ks[17], (DEPTH, N_EXPERTS), 0.01),
        "w_gate_up": nrm(ks[18], (DEPTH, N_EXPERTS, D, 2 * D_EXPERT), D ** -0.5),
        "b_gate_up": nrm(ks[19], (DEPTH, N_EXPERTS, 2 * D_EXPERT), 0.02),
        "w_down": nrm(ks[20], (DEPTH, N_EXPERTS, D_EXPERT, D), D_EXPERT ** -0.5),
        "b_down": nrm(ks[21], (DEPTH, N_EXPERTS, D), 0.02),
        "final_g": 1.0 + nrm(ks[22], (D,), 0.05),
    }


def reference(x, c, ctx, c_ctx, w_mod, b_mod, norm1_g, w_in, q_norm_g, kv_norm_g, w_uq, w_ukv,
              w_pool, pool_scale, w_out, norm2_g, router_w, router_b, w_gate_up, b_gate_up,
              w_down, b_down, final_g):
    n_lat = x.shape[1]
    rows = n_lat // GRID_W
    cos, sin = _axial_rope_angles(rows)
    kv_lo, kv_hi = Q_LORA, Q_LORA + KV_LORA + QK_ROPE

    for l in range(DEPTH):
        mod_lat = (jax.nn.silu(c) @ w_mod[l] + b_mod[l])[:, None, :]
        mod_ctx = (jax.nn.silu(c_ctx) @ w_mod[l] + b_mod[l])[None, None, :]
        sh1, sc1, g1, sh2, sc2, g2 = jnp.split(mod_lat, N_MOD, axis=-1)
        csh1, csc1, cg1, csh2, csc2, cg2 = jnp.split(mod_ctx, N_MOD, axis=-1)

        h_lat = _modulate(x, norm1_g[l], sh1, sc1)
        h_ctx = _modulate(ctx, norm1_g[l], csh1, csc1)

        z = h_lat @ w_in[l]
        cq, ckv, kr_lat, u_lat = jnp.split(z, [Q_LORA, Q_LORA + KV_LORA, kv_hi], axis=-1)
        qn_lat, qr_lat = _queries(cq, q_norm_g[l], w_uq[l])
        qr_lat = _apply_rope(qr_lat, cos[:, None, :], sin[:, None, :])
        kn_lat, kr_lat, v_lat = _keys_values(ckv, _apply_rope(kr_lat, cos, sin), kv_norm_g[l], w_ukv[l])

        z_ckv = h_ctx @ w_in[l][:, kv_lo:kv_hi]
        kn_ctx, kr_ctx, v_ctx = _keys_values(z_ckv[..., :KV_LORA], z_ckv[..., KV_LORA:],
                                             kv_norm_g[l], w_ukv[l])

        k_nope = jnp.concatenate([kn_ctx, kn_lat], axis=1)
        k_rope = jnp.concatenate([kr_ctx, kr_lat], axis=1)
        v_all = jnp.concatenate([v_ctx, v_lat], axis=1)
        att_lat = _blocked_attention(qn_lat, qr_lat, k_nope, k_rope, v_all)
        pool_lat = _multiscale_pool(u_lat, w_pool[l], pool_scale[l])
        mix_lat = jnp.concatenate([att_lat, pool_lat], axis=-1) @ w_out[l]
        x = x + g1 * mix_lat

        h2 = _modulate(x, norm2_g[l], sh2, sc2)
        x = x + g2 * _moe(h2, router_w[l], router_b[l], w_gate_up[l], b_gate_up[l], w_down[l], b_down[l])

        if l < DEPTH - 1:
            qn_ctx, qr_ctx = _queries(h_ctx @ w_in[l][:, :Q_LORA], q_norm_g[l], w_uq[l])
            att_ctx = _attend(qn_ctx, qr_ctx, kn_ctx, kr_ctx, v_ctx).reshape(ctx.shape[0], CTX_LEN, MLA_W)
            pool_ctx = _multiscale_pool(h_ctx @ w_in[l][:, kv_hi:], w_pool[l], pool_scale[l])
            ctx = ctx + cg1 * (jnp.concatenate([att_ctx, pool_ctx], axis=-1) @ w_out[l])
            h2c = _modulate(ctx, norm2_g[l], csh2, csc2)
            ctx = ctx + cg2 * _moe(h2c, router_w[l], router_b[l], w_gate_up[l], b_gate_up[l],
                                   w_down[l], b_down[l])

    return _rmsnorm(x, final_g)
```

```python
import functools
import math

import jax
import jax.numpy as jnp
from jax import lax
from jax.experimental import pallas as pl
from jax.experimental.pallas import tpu as pltpu

F32 = jnp.float32
BF16 = jnp.bfloat16
HIGHEST = lax.Precision.HIGHEST

N_HEADS = 8
QK_NOPE = 64
QK_ROPE = 32
V_DIM = 64
Q_LORA = 256
KV_LORA = 128
GRID_W = 64
ROPE_BASE = 10000.0
POOL_WINDOWS = (2, 4, 8, 16)
POOL_CH = 128
N_EXPERTS = 32
TOP_K = 4
SWIGLU_LIMIT = 7.0
SWIGLU_ALPHA = 1.702
MOE_BLOCK = 256
N_MOD = 6
EPS = 1e-6

LANES = 128
POOL_HALO = 8
VMEM_LIMIT = 56 * 1024 * 1024

FRONT_TS = 512
ATTN_TQ = 512
ATTN_KC = 512
MIX_TS = 512
COMBINE_TT = 256


def _dot(a, b, **kw):
    return jnp.dot(a, b, preferred_element_type=F32, **kw)


def _dot_nt(a, b):
    return lax.dot_general(a, b, (((1,), (1,)), ((), ())), preferred_element_type=F32)


def _rms(x):
    return x * lax.rsqrt(jnp.mean(x * x, axis=-1, keepdims=True) + EPS)


def _mod_kernel(c_ref, w_ref, b_ref, o_ref):
    c = c_ref[...]
    a = c / (1.0 + jnp.exp(-c))
    o_ref[...] = _dot(a, w_ref[...], precision=HIGHEST) + b_ref[...]


def _mod_call(cc, w_mod, b_mod):
    d = w_mod.shape[0]
    return pl.pallas_call(
        _mod_kernel,
        grid=(N_MOD,),
        in_specs=[
            pl.BlockSpec((8, d), lambda j: (0, 0)),
            pl.BlockSpec((d, d), lambda j: (0, j)),
            pl.BlockSpec((1, d), lambda j: (0, j)),
        ],
        out_specs=pl.BlockSpec((8, d), lambda j: (0, j)),
        out_shape=jax.ShapeDtypeStruct((8, N_MOD * d), F32),
        compiler_params=pltpu.CompilerParams(
            dimension_semantics=("arbitrary",), vmem_limit_bytes=VMEM_LIMIT),
        name="mod",
    )(cc, w_mod, b_mod)


def _front_kernel(x_ref, mod_ref, g1_ref, win_ref, qg_ref, kvg_ref, wuq_ref, wukv_ref,
                  cs_ref, sn_ref, *out_refs, is_ctx, scale):
    x = x_ref[0]
    m = mod_ref[0]
    h = _rms(x) * g1_ref[...] * (1.0 + m[1:2]) + m[0:1]
    z = _dot(h.astype(BF16), win_ref[...])
    if is_ctx:
        k_ref, v_ref = out_refs
        zkv = z
    else:
        q_ref, k_ref, v_ref, u_ref = out_refs
        zkv = z[:, Q_LORA:Q_LORA + 3 * LANES]
        u_ref[0] = z[:, Q_LORA + 3 * LANES:]
    cs = cs_ref[...]
    sn = sn_ref[...]

    ckv = _rms(zkv[:, :KV_LORA]) * kvg_ref[...]
    kv = _dot(ckv.astype(BF16), wukv_ref[...])
    kr = zkv[:, LANES:2 * LANES] * cs + zkv[:, 2 * LANES:3 * LANES] * sn
    for hd in range(N_HEADS):
        k_ref[0, hd] = (kv[:, hd * LANES:(hd + 1) * LANES] + kr).astype(BF16)
    voff = N_HEADS * LANES
    for j in range(N_HEADS // 2):
        v_ref[0, j] = kv[:, voff + j * LANES: voff + (j + 1) * LANES].astype(BF16)

    if not is_ctx:
        cq = _rms(z[:, :Q_LORA]) * qg_ref[...]
        qq = _dot(cq.astype(BF16), wuq_ref[...])
        boff = N_HEADS * LANES
        for hd in range(N_HEADS):
            qa = qq[:, hd * LANES:(hd + 1) * LANES]
            qb = qq[:, boff + hd * LANES: boff + (hd + 1) * LANES]
            q_ref[0, hd] = ((qa * cs + qb * sn) * scale).astype(BF16)


def _front_call(xs, mod, mod_per_batch, g1, win, qg, kvg, wuq, wukv, cs, sn, *, is_ctx, ts):
    bsz, n, d = xs.shape
    nt = n // ts
    scale = 1.0 / math.sqrt(QK_NOPE + QK_ROPE)
    const = lambda b, i: (0, 0)
    mod_map = (lambda b, i: (b, 0, 0)) if mod_per_batch else (lambda b, i: (0, 0, 0))
    in_specs = [
        pl.BlockSpec((1, ts, d), lambda b, i: (b, i, 0)),
        pl.BlockSpec((1, N_MOD, d), mod_map),
        pl.BlockSpec((1, d), const),
        pl.BlockSpec(win.shape, const),
        pl.BlockSpec((1, Q_LORA), const),
        pl.BlockSpec((1, KV_LORA), const),
        pl.BlockSpec(wuq.shape, const),
        pl.BlockSpec(wukv.shape, const),
        pl.BlockSpec((ts, LANES), lambda b, i: (i, 0)),
        pl.BlockSpec((ts, LANES), lambda b, i: (i, 0)),
    ]
    k_spec = pl.BlockSpec((1, N_HEADS, ts, LANES), lambda b, i: (b, 0, i, 0))
    v_spec = pl.BlockSpec((1, N_HEADS // 2, ts, LANES), lambda b, i: (b, 0, i, 0))
    k_shape = jax.ShapeDtypeStruct((bsz, N_HEADS, n, LANES), BF16)
    v_shape = jax.ShapeDtypeStruct((bsz, N_HEADS // 2, n, LANES), BF16)
    if is_ctx:
        out_specs = [k_spec, v_spec]
        out_shape = [k_shape, v_shape]
    else:
        pool_w = win.shape[1] - Q_LORA - 3 * LANES
        out_specs = [k_spec, k_spec, v_spec, pl.BlockSpec((1, ts, pool_w), lambda b, i: (b, i, 0))]
        out_shape = [k_shape, k_shape, v_shape, jax.ShapeDtypeStruct((bsz, n, pool_w), F32)]
    return pl.pallas_call(
        functools.partial(_front_kernel, is_ctx=is_ctx, scale=scale),
        grid=(bsz, nt),
        in_specs=in_specs,
        out_specs=out_specs,
        out_shape=out_shape,
        compiler_params=pltpu.CompilerParams(
            dimension_semantics=("arbitrary", "arbitrary"), vmem_limit_bytes=VMEM_LIMIT),
        name="front_ctx" if is_ctx else "front",
    )(xs, mod, g1, win, qg, kvg, wuq, wukv, cs, sn)


def _attn_kernel(q_ref, k_ref, v_ref, kc_ref, vc_ref, o_ref, s_ref, ml_ref, ll_ref, acc_ref):
    n_ctx = kc_ref.shape[2]
    n_lat = k_ref.shape[2]
    n_chunks = n_lat // ATTN_KC
    groups = ATTN_KC // LANES

    def one_head(hd, j):
        q = q_ref[0, hd]
        s = _dot_nt(q, kc_ref[0, hd])
        s_ref[:, 0:n_ctx] = s
        ml = s[:, 0:LANES]
        for g in range(1, n_ctx // LANES):
            ml = jnp.maximum(ml, s[:, g * LANES:(g + 1) * LANES])
        ml_ref[...] = ml

        def scores(c, carry):
            off = pl.multiple_of(c * ATTN_KC, ATTN_KC)
            s = _dot_nt(q, k_ref[0, hd, pl.ds(off, ATTN_KC), :])
            s_ref[:, pl.ds(pl.multiple_of(n_ctx + off, LANES), ATTN_KC)] = s
            ml = ml_ref[...]
            for g in range(groups):
                ml = jnp.maximum(ml, s[:, g * LANES:(g + 1) * LANES])
            ml_ref[...] = ml
            return carry

        lax.fori_loop(0, n_chunks, scores, 0)
        mx = jnp.max(ml_ref[...], axis=-1, keepdims=True)

        p = jnp.exp(s_ref[:, 0:n_ctx] - mx)
        ll = p[:, 0:LANES]
        for g in range(1, n_ctx // LANES):
            ll = ll + p[:, g * LANES:(g + 1) * LANES]
        ll_ref[...] = ll
        acc_ref[...] = _dot(p.astype(BF16), vc_ref[0, j])

        def weighted(c, carry):
            off = pl.multiple_of(c * ATTN_KC, ATTN_KC)
            p = jnp.exp(s_ref[:, pl.ds(pl.multiple_of(n_ctx + off, LANES), ATTN_KC)] - mx)
            ll = ll_ref[...]
            for g in range(groups):
                ll = ll + p[:, g * LANES:(g + 1) * LANES]
            ll_ref[...] = ll
            acc_ref[...] += _dot(p.astype(BF16), v_ref[0, j, pl.ds(off, ATTN_KC), :])
            return carry

        lax.fori_loop(0, n_chunks, weighted, 0)
        denom = jnp.sum(ll_ref[...], axis=-1, keepdims=True)
        return acc_ref[...] / denom

    def head_pair(j, carry):
        o0 = one_head(2 * j, j)
        o1 = one_head(2 * j + 1, j)
        lane = lax.broadcasted_iota(jnp.int32, o0.shape, 1)
        o_ref[0, j] = jnp.where(lane < V_DIM, o0, o1).astype(BF16)
        return carry

    lax.fori_loop(0, N_HEADS // 2, head_pair, 0)


def _attn_call(q, k, v, kc, vc):
    bsz, _, n, _ = q.shape
    n_ctx = kc.shape[2]
    tq = ATTN_TQ
    return pl.pallas_call(
        _attn_kernel,
        grid=(bsz, n // tq),
        in_specs=[
            pl.BlockSpec((1, N_HEADS, tq, LANES), lambda b, i: (b, 0, i, 0)),
            pl.BlockSpec((1, N_HEADS, n, LANES), lambda b, i: (b, 0, 0, 0)),
            pl.BlockSpec((1, N_HEADS // 2, n, LANES), lambda b, i: (b, 0, 0, 0)),
            pl.BlockSpec((1, N_HEADS, n_ctx, LANES), lambda b, i: (b, 0, 0, 0)),
            pl.BlockSpec((1, N_HEADS // 2, n_ctx, LANES), lambda b, i: (b, 0, 0, 0)),
        ],
        out_specs=pl.BlockSpec((1, N_HEADS // 2, tq, LANES), lambda b, i: (b, 0, i, 0)),
        out_shape=jax.ShapeDtypeStruct((bsz, N_HEADS // 2, n, LANES), BF16),
        scratch_shapes=[
            pltpu.VMEM((tq, n_ctx + n), F32),
            pltpu.VMEM((tq, LANES), F32),
            pltpu.VMEM((tq, LANES), F32),
            pltpu.VMEM((tq, LANES), F32),
        ],
        compiler_params=pltpu.CompilerParams(
            dimension_semantics=("arbitrary", "arbitrary"), vmem_limit_bytes=VMEM_LIMIT),
        name="attn",
    )(q, k, v, kc, vc)


def _mix_kernel(att_ref, u_ref, up_ref, un_ref, x_ref, mod_ref, wpool_ref, pscale_ref, wout_ref,
                g2_ref, rw_ref, rb_ref,
                x1_ref, h2_ref, gate_ref, ridx_ref, cnt_ref, ue_ref, base_ref, *, n_seq):
    b = pl.program_id(0)
    i = pl.program_id(1)
    n_tiles = pl.num_programs(1)
    ts = u_ref.shape[1]

    @pl.when(jnp.logical_and(b == 0, i == 0))
    def _():
        base_ref[...] = jnp.zeros_like(base_ref)

    u = u_ref[0]
    ue_ref[0:POOL_HALO] = jnp.where(i > 0, up_ref[0], 0.0)
    ue_ref[POOL_HALO:POOL_HALO + ts] = u
    ue_ref[POOL_HALO + ts:2 * POOL_HALO + ts] = jnp.where(i < n_tiles - 1, un_ref[0], 0.0)
    t = i * ts + lax.broadcasted_iota(jnp.int32, (ts, 1), 0)
    ys = []
    for g, w in enumerate(POOL_WINDOWS):
        half = w // 2
        lanes = slice(g * POOL_CH, (g + 1) * POOL_CH)
        ws = ue_ref[POOL_HALO - half:POOL_HALO - half + ts, lanes]
        for jj in range(-half + 1, half):
            ws = ws + ue_ref[POOL_HALO + jj:POOL_HALO + jj + ts, lanes]
        count = (jnp.minimum(t + half, n_seq) - jnp.maximum(t - half, 0)).astype(F32)
        mixed = (ws / count - u[:, lanes]).astype(BF16)
        ys.append(_dot(mixed, wpool_ref[g]))
    pool = jnp.concatenate(ys, axis=-1) * pscale_ref[...]

    cat = jnp.concatenate([att_ref[0, j] for j in range(N_HEADS // 2)] + [pool.astype(BF16)], axis=-1)
    m = mod_ref[0]
    x1 = x_ref[0] + m[2:3] * _dot(cat, wout_ref[...])
    x1_ref[0] = x1
    h2 = _rms(x1) * g2_ref[...] * (1.0 + m[4:5]) + m[3:4]
    h2_ref[0] = h2

    logits = _dot(h2, rw_ref[...], precision=HIGHEST) + rb_ref[...]
    lane = lax.broadcasted_iota(jnp.int32, logits.shape, 1).astype(F32)
    vals, idxs = [], []
    for _k in range(TOP_K):
        mv = jnp.max(logits, axis=-1, keepdims=True)
        ix = jnp.min(jnp.where(logits == mv, lane, float(LANES)), axis=-1, keepdims=True)
        vals.append(mv)
        idxs.append(ix)
        logits = jnp.where(lane == ix, -jnp.inf, logits)
    es = [jnp.exp(v - vals[0]) for v in vals]
    den = es[0] + es[1] + es[2] + es[3]

    onehot = jnp.zeros(lane.shape, F32)
    for ix in idxs:
        onehot = onehot + jnp.where(lane == ix, 1.0, 0.0)
    row = lax.broadcasted_iota(jnp.int32, (ts, ts), 0)
    col = lax.broadcasted_iota(jnp.int32, (ts, ts), 1)
    tri = jnp.where(col < row, 1.0, 0.0).astype(BF16)
    before = _dot(tri, onehot.astype(BF16)) + base_ref[0:1, :]
    base_new = base_ref[0:1, :] + jnp.sum(onehot, axis=0, keepdims=True)
    base_ref[...] = jnp.broadcast_to(base_new, base_ref.shape)
    cnt_ref[...] = jnp.broadcast_to(base_new, cnt_ref.shape)

    gate_out = jnp.zeros(lane.shape, F32)
    ridx_out = jnp.zeros(lane.shape, F32)
    for kk in range(TOP_K):
        rank = jnp.sum(jnp.where(lane == idxs[kk], before, 0.0), axis=-1, keepdims=True)
        gate_out = jnp.where(lane == float(kk), es[kk] / den, gate_out)
        ridx_out = jnp.where(lane == float(kk), idxs[kk], ridx_out)
        ridx_out = jnp.where(lane == float(TOP_K + kk), rank, ridx_out)
    gate_ref[0] = gate_out
    ridx_ref[0] = ridx_out.astype(jnp.int32)


def _mix_call(att, u, x, mod, wpool, pscale, wout, g2, rw, rb):
    bsz, n, d = x.shape
    ts = MIX_TS
    nt = n // ts
    pool_w = u.shape[2]
    hb = ts // POOL_HALO
    n_halo_blocks = n // POOL_HALO
    const2 = lambda b, i: (0, 0)
    tok = lambda b, i: (b, i, 0)
    return pl.pallas_call(
        functools.partial(_mix_kernel, n_seq=n),
        grid=(bsz, nt),
        in_specs=[
            pl.BlockSpec((1, N_HEADS // 2, ts, LANES), lambda b, i: (b, 0, i, 0)),
            pl.BlockSpec((1, ts, pool_w), tok),
            pl.BlockSpec((1, POOL_HALO, pool_w), lambda b, i: (b, jnp.maximum(i * hb - 1, 0), 0)),
            pl.BlockSpec((1, POOL_HALO, pool_w),
                         lambda b, i: (b, jnp.minimum((i + 1) * hb, n_halo_blocks - 1), 0)),
            pl.BlockSpec((1, ts, d), tok),
            pl.BlockSpec((1, N_MOD, d), lambda b, i: (b, 0, 0)),
            pl.BlockSpec(wpool.shape, lambda b, i: (0, 0, 0)),
            pl.BlockSpec((1, pool_w), const2),
            pl.BlockSpec(wout.shape, const2),
            pl.BlockSpec((1, d), const2),
            pl.BlockSpec(rw.shape, const2),
            pl.BlockSpec((1, LANES), const2),
        ],
        out_specs=[
            pl.BlockSpec((1, ts, d), tok),
            pl.BlockSpec((1, ts, d), tok),
            pl.BlockSpec((1, ts, LANES), tok),
            pl.BlockSpec((1, ts, LANES), tok),
            pl.BlockSpec((8, LANES), const2),
        ],
        out_shape=[
            jax.ShapeDtypeStruct((bsz, n, d), F32),
            jax.ShapeDtypeStruct((bsz, n, d), F32),
            jax.ShapeDtypeStruct((bsz, n, LANES), F32),
            jax.ShapeDtypeStruct((bsz, n, LANES), jnp.int32),
            jax.ShapeDtypeStruct((8, LANES), F32),
        ],
        scratch_shapes=[
            pltpu.VMEM((ts + 2 * POOL_HALO, pool_w), F32),
            pltpu.VMEM((8, LANES), F32),
        ],
        compiler_params=pltpu.CompilerParams(
            dimension_semantics=("arbitrary", "arbitrary"), vmem_limit_bytes=VMEM_LIMIT),
        name="mix",
    )(att, u, u, u, x, mod, wpool, pscale, wout, g2, rw, rb)


def _row_copy(src_hbm, src_row, dst_vmem, dst_row, sem):
    return pltpu.make_async_copy(src_hbm.at[pl.ds(src_row, 1), :], dst_vmem.at[pl.ds(dst_row, 1), :], sem)


def _dispatch_kernel(slot_tok_ref, valid_ref, h2_hbm, xs_ref, buf_ref, sem):
    i = pl.program_id(0)

    @pl.when(valid_ref[i] == 0)
    def _():
        xs_ref[...] = jnp.zeros_like(xs_ref)

    @pl.when(valid_ref[i] == 1)
    def _():
        base = i * MOE_BLOCK

        def start(r, carry):
            _row_copy(h2_hbm, slot_tok_ref[base + r], buf_ref, r, sem).start()
            return carry

        lax.fori_loop(0, MOE_BLOCK, start, 0)

        def wait(r, carry):
            _row_copy(h2_hbm, 0, buf_ref, r, sem).wait()
            return carry

        lax.fori_loop(0, MOE_BLOCK, wait, 0)
        xs_ref[...] = buf_ref[...].astype(BF16)


def _dispatch_call(slot_tok, valid, h2):
    d = h2.shape[1]
    n_blocks = valid.shape[0]
    return pl.pallas_call(
        _dispatch_kernel,
        grid_spec=pltpu.PrefetchScalarGridSpec(
            num_scalar_prefetch=2,
            grid=(n_blocks,),
            in_specs=[pl.BlockSpec(memory_space=pl.ANY)],
            out_specs=pl.BlockSpec((MOE_BLOCK, d), lambda i, st, va: (i, 0)),
            scratch_shapes=[pltpu.VMEM((MOE_BLOCK, d), F32), pltpu.SemaphoreType.DMA],
        ),
        out_shape=jax.ShapeDtypeStruct((n_blocks * MOE_BLOCK, d), BF16),
        compiler_params=pltpu.CompilerParams(
            dimension_semantics=("arbitrary",), vmem_limit_bytes=VMEM_LIMIT),
        name="dispatch",
    )(slot_tok, valid, h2)


def _moe_kernel(be_ref, valid_ref, first_ref, xs_ref, wgu_ref, bgu_ref, wd_ref, bd_ref,
                ys_ref, wgu_bf, wd_bf):
    i = pl.program_id(0)

    @pl.when(valid_ref[i] == 0)
    def _():
        ys_ref[...] = jnp.zeros_like(ys_ref)

    @pl.when(first_ref[i] == 1)
    def _():
        wgu_bf[...] = wgu_ref[0].astype(BF16)
        wd_bf[...] = wd_ref[0].astype(BF16)

    @pl.when(valid_ref[i] == 1)
    def _():
        de = wd_bf.shape[0]
        gu = _dot(xs_ref[...], wgu_bf[...]) + bgu_ref[0]
        g = jnp.minimum(gu[:, :de], SWIGLU_LIMIT)
        lin = jnp.clip(gu[:, de:], -SWIGLU_LIMIT, SWIGLU_LIMIT)
        act = g / (1.0 + jnp.exp(-SWIGLU_ALPHA * g)) * (lin + 1.0)
        ys_ref[...] = _dot(act.astype(BF16), wd_bf[...]) + bd_ref[0]


def _moe_call(be, valid, first, xs, wgu, bgu, wd, bd):
    n_blocks = be.shape[0]
    _, d, de2 = wgu.shape
    de = de2 // 2
    wmap = lambda i, be_, va, fi: (be_[i], 0, 0)
    xmap = lambda i, be_, va, fi: (i, 0)
    return pl.pallas_call(
        _moe_kernel,
        grid_spec=pltpu.PrefetchScalarGridSpec(
            num_scalar_prefetch=3,
            grid=(n_blocks,),
            in_specs=[
                pl.BlockSpec((MOE_BLOCK, d), xmap),
                pl.BlockSpec((1, d, de2), wmap),
                pl.BlockSpec((1, 1, de2), wmap),
                pl.BlockSpec((1, de, d), wmap),
                pl.BlockSpec((1, 1, d), wmap),
            ],
            out_specs=pl.BlockSpec((MOE_BLOCK, d), xmap),
            scratch_shapes=[pltpu.VMEM((d, de2), BF16), pltpu.VMEM((de, d), BF16)],
        ),
        out_shape=jax.ShapeDtypeStruct((n_blocks * MOE_BLOCK, d), F32),
        compiler_params=pltpu.CompilerParams(
            dimension_semantics=("arbitrary",), vmem_limit_bytes=VMEM_LIMIT),
        name="moe",
    )(be, valid, first, xs, wgu, bgu, wd, bd)


def _combine_kernel(dest_ref, ys_hbm, x1_ref, gate_ref, mod_ref, fg_ref, o_ref, buf_ref, sem):
    i = pl.program_id(0)
    tt = x1_ref.shape[0]
    base = i * tt * TOP_K

    def start(r, carry):
        for kk in range(TOP_K):
            _row_copy(ys_hbm, dest_ref[base + r * TOP_K + kk], buf_ref, kk * tt + r, sem).start()
        return carry

    lax.fori_loop(0, tt, start, 0)

    def wait(r, carry):
        _row_copy(ys_hbm, 0, buf_ref, r, sem).wait()
        return carry

    lax.fori_loop(0, tt * TOP_K, wait, 0)

    gates = gate_ref[...]
    y = gates[:, 0:1] * buf_ref[0:tt]
    for kk in range(1, TOP_K):
        y = y + gates[:, kk:kk + 1] * buf_ref[kk * tt:(kk + 1) * tt]
    m = mod_ref[0]
    x2 = x1_ref[...] + m[5:6] * y
    o_ref[...] = _rms(x2) * fg_ref[...]


def _combine_call(dest, ys, x1, gates, mod, fg, n_seq):
    t, d = x1.shape
    tt = COMBINE_TT
    tiles_per_seq = n_seq // tt
    tok = lambda i, de: (i, 0)
    return pl.pallas_call(
        _combine_kernel,
        grid_spec=pltpu.PrefetchScalarGridSpec(
            num_scalar_prefetch=1,
            grid=(t // tt,),
            in_specs=[
                pl.BlockSpec(memory_space=pl.ANY),
                pl.BlockSpec((tt, d), tok),
                pl.BlockSpec((tt, LANES), tok),
                pl.BlockSpec((1, N_MOD, d), lambda i, de: (i // tiles_per_seq, 0, 0)),
                pl.BlockSpec((1, d), lambda i, de: (0, 0)),
            ],
            out_specs=pl.BlockSpec((tt, d), tok),
            scratch_shapes=[pltpu.VMEM((TOP_K * tt, d), F32), pltpu.SemaphoreType.DMA],
        ),
        out_shape=jax.ShapeDtypeStruct((t, d), F32),
        compiler_params=pltpu.CompilerParams(
            dimension_semantics=("arbitrary",), vmem_limit_bytes=VMEM_LIMIT),
        name="combine",
    )(dest, ys, x1, gates, mod, fg)


def _pad_cols(a, width):
    return jnp.pad(a, ((0, 0), (0, width - a.shape[1])))


def _rope_slab(w_rope, swapped):
    half = QK_ROPE // 2
    if swapped:
        w_rope = jnp.concatenate([w_rope[:, half:], w_rope[:, :half]], axis=1)
    zeros = jnp.zeros((w_rope.shape[0], QK_NOPE), w_rope.dtype)
    return _pad_cols(jnp.concatenate([zeros, w_rope], axis=1), LANES)


def _prep_w_in(w_in):
    kr0 = Q_LORA + KV_LORA
    w_kr = w_in[:, kr0:kr0 + QK_ROPE]
    return jnp.concatenate(
        [w_in[:, :kr0], _rope_slab(w_kr, False), _rope_slab(w_kr, True), w_in[:, kr0 + QK_ROPE:]],
        axis=1).astype(BF16)


def _prep_w_uq(w_uq):
    per = QK_NOPE + QK_ROPE
    slabs_a, slabs_b = [], []
    for hd in range(N_HEADS):
        w = w_uq[:, hd * per:(hd + 1) * per]
        slabs_a.append(_pad_cols(w, LANES))
        slabs_b.append(_rope_slab(w[:, QK_NOPE:], True))
    return jnp.concatenate(slabs_a + slabs_b, axis=1).astype(BF16)


def _prep_w_ukv(w_ukv):
    per = QK_NOPE + V_DIM
    ks, vs = [], []
    for hd in range(N_HEADS):
        w = w_ukv[:, hd * per:(hd + 1) * per]
        ks.append(_pad_cols(w[:, :QK_NOPE], LANES))
        vs.append(w[:, QK_NOPE:])
    return jnp.concatenate(ks + vs, axis=1).astype(BF16)


def _rope_tables(n_lat):
    rows = n_lat // GRID_W
    nf = QK_ROPE // 4
    row = jnp.repeat(jnp.arange(rows, dtype=F32), GRID_W)
    col = jnp.tile(jnp.arange(GRID_W, dtype=F32), rows)
    freqs = ROPE_BASE ** (-jnp.arange(nf, dtype=F32) / nf)
    ang = jnp.concatenate([row[:, None] * freqs, col[:, None] * freqs], axis=-1)
    cos, sin = jnp.cos(ang), jnp.sin(ang)
    ones = jnp.ones((n_lat, QK_NOPE), F32)
    zeros = jnp.zeros((n_lat, QK_NOPE), F32)
    cs = _pad_cols(jnp.concatenate([ones, cos, cos], axis=1), LANES)
    sn = _pad_cols(jnp.concatenate([zeros, -sin, sin], axis=1), LANES)
    return cs, sn


def kernel(x, c, ctx, c_ctx, w_mod, b_mod, norm1_g, w_in, q_norm_g, kv_norm_g, w_uq, w_ukv, w_pool,
           pool_scale, w_out, norm2_g, router_w, router_b, w_gate_up, b_gate_up, w_down, b_down,
           final_g):
    bsz, n, d = x.shape
    n_ctx = ctx.shape[1]
    t = bsz * n
    l = 0

    cc = jnp.concatenate([c, c_ctx[None, :], jnp.zeros((8 - bsz - 1, d), F32)], axis=0)
    mod = _mod_call(cc, w_mod[l], b_mod[l][None, :]).reshape(8, N_MOD, d)
    mod_lat, mod_ctx = mod[:bsz], mod[bsz:bsz + 1]

    win = _prep_w_in(w_in[l])
    wuq = _prep_w_uq(w_uq[l])
    wukv = _prep_w_ukv(w_ukv[l])
    cs, sn = _rope_tables(n)
    cs_ctx = jnp.broadcast_to((jnp.arange(LANES) < QK_NOPE + QK_ROPE).astype(F32), (n_ctx, LANES))
    sn_ctx = jnp.zeros((n_ctx, LANES), F32)
    g1 = norm1_g[l][None, :]
    qg = q_norm_g[l][None, :]
    kvg = kv_norm_g[l][None, :]

    q, k, v, u = _front_call(x, mod_lat, True, g1, win, qg, kvg, wuq, wukv, cs, sn,
                             is_ctx=False, ts=FRONT_TS)
    kc, vc = _front_call(ctx, mod_ctx, False, g1, win[:, Q_LORA:Q_LORA + 3 * LANES], qg, kvg, wuq,
                         wukv, cs_ctx, sn_ctx, is_ctx=True, ts=n_ctx)
    att = _attn_call(q, k, v, kc, vc)

    rw = _pad_cols(router_w[l], LANES)
    rb = jnp.concatenate([router_b[l], jnp.full((LANES - N_EXPERTS,), -jnp.inf, F32)])[None, :]
    x1, h2, gates, ridx, cnt = _mix_call(
        att, u, x, mod_lat, w_pool[l].astype(BF16), pool_scale[l][None, :], w_out[l].astype(BF16),
        norm2_g[l][None, :], rw, rb)

    n_blocks = t * TOP_K // MOE_BLOCK + N_EXPERTS
    counts = cnt[0, :N_EXPERTS].astype(jnp.int32)
    padded = (counts + MOE_BLOCK - 1) // MOE_BLOCK * MOE_BLOCK
    pad_ends = jnp.cumsum(padded)
    pad_starts = pad_ends - padded
    ridx = ridx.reshape(t, LANES)
    dest = (pad_starts[ridx[:, :TOP_K]] + ridx[:, TOP_K:2 * TOP_K]).reshape(-1)
    n_used = pad_ends[-1] // MOE_BLOCK
    blk_iota = jnp.arange(n_blocks, dtype=jnp.int32)
    blk = jnp.minimum(blk_iota, n_used - 1)
    be = jnp.minimum(jnp.searchsorted(pad_ends, blk * MOE_BLOCK, side="right"),
                     N_EXPERTS - 1).astype(jnp.int32)
    valid = (blk_iota < n_used).astype(jnp.int32)
    first = (valid * ((blk_iota == 0) | (be != jnp.roll(be, 1))).astype(jnp.int32))
    slot_tok = jnp.zeros((n_blocks * MOE_BLOCK,), jnp.int32).at[dest].set(
        jnp.repeat(jnp.arange(t, dtype=jnp.int32), TOP_K))

    xs = _dispatch_call(slot_tok, valid, h2.reshape(t, d))
    ys = _moe_call(be, valid, first, xs, w_gate_up[l], b_gate_up[l][:, None, :], w_down[l],
                   b_down[l][:, None, :])
    out = _combine_call(dest, ys, x1.reshape(t, d), gates.reshape(t, LANES), mod_lat,
                        final_g[None, :], n)
    return out.reshape(bsz, n, d)
```

```python
import functools
import math

import jax
import jax.numpy as jnp
from jax import lax
from jax.experimental import pallas as pl
from jax.experimental.pallas import tpu as pltpu
from jax.experimental.pallas import tpu_sc as plsc

F32 = jnp.float32
BF16 = jnp.bfloat16
HIGHEST = lax.Precision.HIGHEST

N_HEADS = 8
QK_NOPE = 64
QK_ROPE = 32
V_DIM = 64
Q_LORA = 256
KV_LORA = 128
GRID_W = 64
ROPE_BASE = 10000.0
POOL_WINDOWS = (2, 4, 8, 16)
POOL_CH = 128
N_EXPERTS = 32
TOP_K = 4
SWIGLU_LIMIT = 7.0
SWIGLU_ALPHA = 1.702
MOE_BLOCK = 256
N_MOD = 6
EPS = 1e-6

LANES = 128
POOL_HALO = 8
VMEM_LIMIT = 56 * 1024 * 1024

FRONT_TS = 512
ATTN_TQ = 512
ATTN_KC = 512
MIX_TS = 512
COMBINE_TT = 256
SC_ROWS = 64


def _dot(a, b, **kw):
    return jnp.dot(a, b, preferred_element_type=F32, **kw)


def _dot_nt(a, b):
    return lax.dot_general(a, b, (((1,), (1,)), ((), ())), preferred_element_type=F32)


def _rms(x):
    return x * lax.rsqrt(jnp.mean(x * x, axis=-1, keepdims=True) + EPS)


def _mod_kernel(c_ref, w_ref, b_ref, o_ref):
    c = c_ref[...]
    a = c / (1.0 + jnp.exp(-c))
    o_ref[...] = _dot(a, w_ref[...], precision=HIGHEST) + b_ref[...]


def _mod_call(cc, w_mod, b_mod):
    d = w_mod.shape[0]
    return pl.pallas_call(
        _mod_kernel,
        grid=(N_MOD,),
        in_specs=[
            pl.BlockSpec((8, d), lambda j: (0, 0)),
            pl.BlockSpec((d, d), lambda j: (0, j)),
            pl.BlockSpec((1, d), lambda j: (0, j)),
        ],
        out_specs=pl.BlockSpec((8, d), lambda j: (0, j)),
        out_shape=jax.ShapeDtypeStruct((8, N_MOD * d), F32),
        compiler_params=pltpu.CompilerParams(
            dimension_semantics=("arbitrary",), vmem_limit_bytes=VMEM_LIMIT),
        name="mod",
    )(cc, w_mod, b_mod)


def _front_kernel(x_ref, mod_ref, g1_ref, win_ref, qg_ref, kvg_ref, wuq_ref, wukv_ref,
                  cs_ref, sn_ref, *out_refs, is_ctx, scale):
    x = x_ref[0]
    m = mod_ref[0]
    h = _rms(x) * g1_ref[...] * (1.0 + m[1:2]) + m[0:1]
    z = _dot(h.astype(BF16), win_ref[...])
    if is_ctx:
        k_ref, v_ref = out_refs
        zkv = z
    else:
        q_ref, k_ref, v_ref, u_ref = out_refs
        zkv = z[:, Q_LORA:Q_LORA + 3 * LANES]
        u_ref[0] = z[:, Q_LORA + 3 * LANES:]
    cs = cs_ref[...]
    sn = sn_ref[...]

    ckv = _rms(zkv[:, :KV_LORA]) * kvg_ref[...]
    kv = _dot(ckv.astype(BF16), wukv_ref[...])
    kr = zkv[:, LANES:2 * LANES] * cs + zkv[:, 2 * LANES:3 * LANES] * sn
    for hd in range(N_HEADS):
        k_ref[0, hd] = (kv[:, hd * LANES:(hd + 1) * LANES] + kr).astype(BF16)
    voff = N_HEADS * LANES
    for j in range(N_HEADS // 2):
        v_ref[0, j] = kv[:, voff + j * LANES: voff + (j + 1) * LANES].astype(BF16)

    if not is_ctx:
        cq = _rms(z[:, :Q_LORA]) * qg_ref[...]
        qq = _dot(cq.astype(BF16), wuq_ref[...])
        boff = N_HEADS * LANES
        for hd in range(N_HEADS):
            qa = qq[:, hd * LANES:(hd + 1) * LANES]
            qb = qq[:, boff + hd * LANES: boff + (hd + 1) * LANES]
            q_ref[0, hd] = ((qa * cs + qb * sn) * scale).astype(BF16)


def _front_call(xs, mod, mod_per_batch, g1, win, qg, kvg, wuq, wukv, cs, sn, *, is_ctx, ts):
    bsz, n, d = xs.shape
    nt = n // ts
    scale = 1.0 / math.sqrt(QK_NOPE + QK_ROPE)
    const = lambda b, i: (0, 0)
    mod_map = (lambda b, i: (b, 0, 0)) if mod_per_batch else (lambda b, i: (0, 0, 0))
    in_specs = [
        pl.BlockSpec((1, ts, d), lambda b, i: (b, i, 0)),
        pl.BlockSpec((1, N_MOD, d), mod_map),
        pl.BlockSpec((1, d), const),
        pl.BlockSpec(win.shape, const),
        pl.BlockSpec((1, Q_LORA), const),
        pl.BlockSpec((1, KV_LORA), const),
        pl.BlockSpec(wuq.shape, const),
        pl.BlockSpec(wukv.shape, const),
        pl.BlockSpec((ts, LANES), lambda b, i: (i, 0)),
        pl.BlockSpec((ts, LANES), lambda b, i: (i, 0)),
    ]
    k_spec = pl.BlockSpec((1, N_HEADS, ts, LANES), lambda b, i: (b, 0, i, 0))
    v_spec = pl.BlockSpec((1, N_HEADS // 2, ts, LANES), lambda b, i: (b, 0, i, 0))
    k_shape = jax.ShapeDtypeStruct((bsz, N_HEADS, n, LANES), BF16)
    v_shape = jax.ShapeDtypeStruct((bsz, N_HEADS // 2, n, LANES), BF16)
    if is_ctx:
        out_specs = [k_spec, v_spec]
        out_shape = [k_shape, v_shape]
    else:
        pool_w = win.shape[1] - Q_LORA - 3 * LANES
        out_specs = [k_spec, k_spec, v_spec, pl.BlockSpec((1, ts, pool_w), lambda b, i: (b, i, 0))]
        out_shape = [k_shape, k_shape, v_shape, jax.ShapeDtypeStruct((bsz, n, pool_w), F32)]
    return pl.pallas_call(
        functools.partial(_front_kernel, is_ctx=is_ctx, scale=scale),
        grid=(bsz, nt),
        in_specs=in_specs,
        out_specs=out_specs,
        out_shape=out_shape,
        compiler_params=pltpu.CompilerParams(
            dimension_semantics=("arbitrary", "arbitrary"), vmem_limit_bytes=VMEM_LIMIT),
        name="front_ctx" if is_ctx else "front",
    )(xs, mod, g1, win, qg, kvg, wuq, wukv, cs, sn)


def _attn_kernel(q_ref, k_ref, v_ref, kc_ref, vc_ref, o_ref, s_ref, ml_ref, ll_ref, acc_ref):
    n_ctx = kc_ref.shape[2]
    n_lat = k_ref.shape[2]
    n_chunks = n_lat // ATTN_KC
    groups = ATTN_KC // LANES

    def one_head(hd, j):
        q = q_ref[0, hd]
        s = _dot_nt(q, kc_ref[0, hd])
        s_ref[:, 0:n_ctx] = s
        ml = s[:, 0:LANES]
        for g in range(1, n_ctx // LANES):
            ml = jnp.maximum(ml, s[:, g * LANES:(g + 1) * LANES])
        ml_ref[...] = ml

        def scores(c, carry):
            off = pl.multiple_of(c * ATTN_KC, ATTN_KC)
            s = _dot_nt(q, k_ref[0, hd, pl.ds(off, ATTN_KC), :])
            s_ref[:, pl.ds(pl.multiple_of(n_ctx + off, LANES), ATTN_KC)] = s
            ml = ml_ref[...]
            for g in range(groups):
                ml = jnp.maximum(ml, s[:, g * LANES:(g + 1) * LANES])
            ml_ref[...] = ml
            return carry

        lax.fori_loop(0, n_chunks, scores, 0)
        mx = jnp.max(ml_ref[...], axis=-1, keepdims=True)

        p = jnp.exp(s_ref[:, 0:n_ctx] - mx)
        ll = p[:, 0:LANES]
        for g in range(1, n_ctx // LANES):
            ll = ll + p[:, g * LANES:(g + 1) * LANES]
        ll_ref[...] = ll
        acc_ref[...] = _dot(p.astype(BF16), vc_ref[0, j])

        def weighted(c, carry):
            off = pl.multiple_of(c * ATTN_KC, ATTN_KC)
            p = jnp.exp(s_ref[:, pl.ds(pl.multiple_of(n_ctx + off, LANES), ATTN_KC)] - mx)
            ll = ll_ref[...]
            for g in range(groups):
                ll = ll + p[:, g * LANES:(g + 1) * LANES]
            ll_ref[...] = ll
            acc_ref[...] += _dot(p.astype(BF16), v_ref[0, j, pl.ds(off, ATTN_KC), :])
            return carry

        lax.fori_loop(0, n_chunks, weighted, 0)
        denom = jnp.sum(ll_ref[...], axis=-1, keepdims=True)
        return acc_ref[...] / denom

    def head_pair(j, carry):
        o0 = one_head(2 * j, j)
        o1 = one_head(2 * j + 1, j)
        lane = lax.broadcasted_iota(jnp.int32, o0.shape, 1)
        o_ref[0, j] = jnp.where(lane < V_DIM, o0, o1).astype(BF16)
        return carry

    lax.fori_loop(0, N_HEADS // 2, head_pair, 0)


def _attn_call(q, k, v, kc, vc):
    bsz, _, n, _ = q.shape
    n_ctx = kc.shape[2]
    tq = ATTN_TQ
    return pl.pallas_call(
        _attn_kernel,
        grid=(bsz, n // tq),
        in_specs=[
            pl.BlockSpec((1, N_HEADS, tq, LANES), lambda b, i: (b, 0, i, 0)),
            pl.BlockSpec((1, N_HEADS, n, LANES), lambda b, i: (b, 0, 0, 0)),
            pl.BlockSpec((1, N_HEADS // 2, n, LANES), lambda b, i: (b, 0, 0, 0)),
            pl.BlockSpec((1, N_HEADS, n_ctx, LANES), lambda b, i: (b, 0, 0, 0)),
            pl.BlockSpec((1, N_HEADS // 2, n_ctx, LANES), lambda b, i: (b, 0, 0, 0)),
        ],
        out_specs=pl.BlockSpec((1, N_HEADS // 2, tq, LANES), lambda b, i: (b, 0, i, 0)),
        out_shape=jax.ShapeDtypeStruct((bsz, N_HEADS // 2, n, LANES), BF16),
        scratch_shapes=[
            pltpu.VMEM((tq, n_ctx + n), F32),
            pltpu.VMEM((tq, LANES), F32),
            pltpu.VMEM((tq, LANES), F32),
            pltpu.VMEM((tq, LANES), F32),
        ],
        compiler_params=pltpu.CompilerParams(
            dimension_semantics=("arbitrary", "arbitrary"), vmem_limit_bytes=VMEM_LIMIT),
        name="attn",
    )(q, k, v, kc, vc)


def _mix_kernel(att_ref, u_ref, up_ref, un_ref, x_ref, mod_ref, wpool_ref, pscale_ref, wout_ref,
                g2_ref, rw_ref, rb_ref,
                x1_ref, h2_ref, gate_ref, ridx_ref, cnt_ref, ue_ref, base_ref, *, n_seq):
    b = pl.program_id(0)
    i = pl.program_id(1)
    n_tiles = pl.num_programs(1)
    ts = u_ref.shape[1]

    @pl.when(jnp.logical_and(b == 0, i == 0))
    def _():
        base_ref[...] = jnp.zeros_like(base_ref)

    u = u_ref[0]
    ue_ref[0:POOL_HALO] = jnp.where(i > 0, up_ref[0], 0.0)
    ue_ref[POOL_HALO:POOL_HALO + ts] = u
    ue_ref[POOL_HALO + ts:2 * POOL_HALO + ts] = jnp.where(i < n_tiles - 1, un_ref[0], 0.0)
    t = i * ts + lax.broadcasted_iota(jnp.int32, (ts, 1), 0)
    ys = []
    for g, w in enumerate(POOL_WINDOWS):
        half = w // 2
        lanes = slice(g * POOL_CH, (g + 1) * POOL_CH)
        ws = ue_ref[POOL_HALO - half:POOL_HALO - half + ts, lanes]
        for jj in range(-half + 1, half):
            ws = ws + ue_ref[POOL_HALO + jj:POOL_HALO + jj + ts, lanes]
        count = (jnp.minimum(t + half, n_seq) - jnp.maximum(t - half, 0)).astype(F32)
        mixed = (ws / count - u[:, lanes]).astype(BF16)
        ys.append(_dot(mixed, wpool_ref[g]))
    pool = jnp.concatenate(ys, axis=-1) * pscale_ref[...]

    cat = jnp.concatenate([att_ref[0, j] for j in range(N_HEADS // 2)] + [pool.astype(BF16)], axis=-1)
    m = mod_ref[0]
    x1 = x_ref[0] + m[2:3] * _dot(cat, wout_ref[...])
    x1_ref[0] = x1
    h2 = _rms(x1) * g2_ref[...] * (1.0 + m[4:5]) + m[3:4]
    h2_ref[0] = h2

    logits = _dot(h2, rw_ref[...], precision=HIGHEST) + rb_ref[...]
    lane = lax.broadcasted_iota(jnp.int32, logits.shape, 1).astype(F32)
    vals, idxs = [], []
    for _k in range(TOP_K):
        mv = jnp.max(logits, axis=-1, keepdims=True)
        ix = jnp.min(jnp.where(logits == mv, lane, float(LANES)), axis=-1, keepdims=True)
        vals.append(mv)
        idxs.append(ix)
        logits = jnp.where(lane == ix, -jnp.inf, logits)
    es = [jnp.exp(v - vals[0]) for v in vals]
    den = es[0] + es[1] + es[2] + es[3]

    onehot = jnp.zeros(lane.shape, F32)
    for ix in idxs:
        onehot = onehot + jnp.where(lane == ix, 1.0, 0.0)
    row = lax.broadcasted_iota(jnp.int32, (ts, ts), 0)
    col = lax.broadcasted_iota(jnp.int32, (ts, ts), 1)
    tri = jnp.where(col < row, 1.0, 0.0).astype(BF16)
    before = _dot(tri, onehot.astype(BF16)) + base_ref[0:1, :]
    base_new = base_ref[0:1, :] + jnp.sum(onehot, axis=0, keepdims=True)
    base_ref[...] = jnp.broadcast_to(base_new, base_ref.shape)
    cnt_ref[...] = jnp.broadcast_to(base_new, cnt_ref.shape)

    gate_out = jnp.zeros(lane.shape, F32)
    ridx_out = jnp.zeros(lane.shape, F32)
    for kk in range(TOP_K):
        rank = jnp.sum(jnp.where(lane == idxs[kk], before, 0.0), axis=-1, keepdims=True)
        gate_out = jnp.where(lane == float(kk), es[kk] / den, gate_out)
        ridx_out = jnp.where(lane == float(kk), idxs[kk], ridx_out)
        ridx_out = jnp.where(lane == float(TOP_K + kk), rank, ridx_out)
    gate_ref[0] = gate_out
    ridx_ref[0] = ridx_out.astype(jnp.int32)


def _mix_call(att, u, x, mod, wpool, pscale, wout, g2, rw, rb):
    bsz, n, d = x.shape
    ts = MIX_TS
    nt = n // ts
    pool_w = u.shape[2]
    hb = ts // POOL_HALO
    n_halo_blocks = n // POOL_HALO
    const2 = lambda b, i: (0, 0)
    tok = lambda b, i: (b, i, 0)
    return pl.pallas_call(
        functools.partial(_mix_kernel, n_seq=n),
        grid=(bsz, nt),
        in_specs=[
            pl.BlockSpec((1, N_HEADS // 2, ts, LANES), lambda b, i: (b, 0, i, 0)),
            pl.BlockSpec((1, ts, pool_w), tok),
            pl.BlockSpec((1, POOL_HALO, pool_w), lambda b, i: (b, jnp.maximum(i * hb - 1, 0), 0)),
            pl.BlockSpec((1, POOL_HALO, pool_w),
                         lambda b, i: (b, jnp.minimum((i + 1) * hb, n_halo_blocks - 1), 0)),
            pl.BlockSpec((1, ts, d), tok),
            pl.BlockSpec((1, N_MOD, d), lambda b, i: (b, 0, 0)),
            pl.BlockSpec(wpool.shape, lambda b, i: (0, 0, 0)),
            pl.BlockSpec((1, pool_w), const2),
            pl.BlockSpec(wout.shape, const2),
            pl.BlockSpec((1, d), const2),
            pl.BlockSpec(rw.shape, const2),
            pl.BlockSpec((1, LANES), const2),
        ],
        out_specs=[
            pl.BlockSpec((1, ts, d), tok),
            pl.BlockSpec((1, ts, d), tok),
            pl.BlockSpec((1, ts, LANES), tok),
            pl.BlockSpec((1, ts, LANES), tok),
            pl.BlockSpec((8, LANES), const2),
        ],
        out_shape=[
            jax.ShapeDtypeStruct((bsz, n, d), F32),
            jax.ShapeDtypeStruct((bsz, n, d), F32),
            jax.ShapeDtypeStruct((bsz, n, LANES), F32),
            jax.ShapeDtypeStruct((bsz, n, LANES), jnp.int32),
            jax.ShapeDtypeStruct((8, LANES), F32),
        ],
        scratch_shapes=[
            pltpu.VMEM((ts + 2 * POOL_HALO, pool_w), F32),
            pltpu.VMEM((8, LANES), F32),
        ],
        compiler_params=pltpu.CompilerParams(
            dimension_semantics=("arbitrary", "arbitrary"), vmem_limit_bytes=VMEM_LIMIT),
        name="mix",
    )(att, u, u, u, x, mod, wpool, pscale, wout, g2, rw, rb)


def _sc_mesh():
    return plsc.VectorSubcoreMesh(core_axis_name="core", subcore_axis_name="subcore")


def _sc_worker_id():
    info = plsc.get_sparse_core_info()
    return lax.axis_index("subcore") * info.num_cores + lax.axis_index("core")


def _sc_num_workers():
    info = plsc.get_sparse_core_info()
    return info.num_cores * info.num_subcores


def _scatter_rows(x, dest3, n_out):
    t, d = x.shape
    top_k, n_chunks, rows = dest3.shape
    per_worker = n_chunks // _sc_num_workers()

    @functools.partial(
        pl.kernel,
        out_type=jax.ShapeDtypeStruct((n_out, d), x.dtype),
        mesh=_sc_mesh(),
        scratch_types=[pltpu.VMEM((top_k, rows), jnp.int32), pltpu.VMEM((rows, d), x.dtype)],
        name="sc_scatter",
    )
    def scatter(x_hbm, i_hbm, o_hbm, idx_v, rows_v):
        wid = _sc_worker_id()

        @pl.loop(0, per_worker)
        def _(c):
            chunk = wid * per_worker + c
            for kk in range(top_k):
                pltpu.sync_copy(i_hbm.at[kk, chunk], idx_v.at[kk])
            pltpu.sync_copy(x_hbm.at[pl.ds(pl.multiple_of(chunk * rows, 8), rows)], rows_v)
            for kk in range(top_k):
                pltpu.sync_copy(rows_v, o_hbm.at[idx_v.at[kk]])

    return scatter(x, dest3)


def _gather_rows(y, idx):
    n = idx.shape[0]
    d = y.shape[1]
    rows = SC_ROWS
    per_worker = n // _sc_num_workers()
    n_chunks = per_worker // rows

    @functools.partial(
        pl.kernel,
        out_type=jax.ShapeDtypeStruct((n, d), y.dtype),
        mesh=_sc_mesh(),
        scratch_types=[pltpu.VMEM((rows,), jnp.int32), pltpu.VMEM((rows, d), y.dtype)],
        name="sc_gather",
    )
    def gather(y_hbm, i_hbm, o_hbm, idx_v, rows_v):
        base = _sc_worker_id() * per_worker

        @pl.loop(0, n_chunks)
        def _(c):
            off = pl.multiple_of(base + c * rows, 8)
            pltpu.sync_copy(i_hbm.at[pl.ds(off, rows)], idx_v)
            pltpu.sync_copy(y_hbm.at[idx_v], rows_v)
            pltpu.sync_copy(rows_v, o_hbm.at[pl.ds(off, rows)])

    return gather(y, idx)


def _moe_kernel(blk_ref, exp_ref, lo_ref, hi_ref, cast_ref, init_ref, xs_ref, wgu_ref, bgu_ref,
                wd_ref, bd_ref, ys_ref, wgu_bf, wd_bf):
    i = pl.program_id(0)

    @pl.when(cast_ref[i] == 1)
    def _():
        wgu_bf[...] = wgu_ref[0].astype(BF16)
        wd_bf[...] = wd_ref[0].astype(BF16)

    @pl.when(init_ref[i] == 1)
    def _():
        ys_ref[...] = jnp.zeros_like(ys_ref)

    @pl.when(hi_ref[i] > lo_ref[i])
    def _():
        de = wd_bf.shape[0]
        gu = _dot(xs_ref[...].astype(BF16), wgu_bf[...]) + bgu_ref[0]
        g = jnp.minimum(gu[:, :de], SWIGLU_LIMIT)
        lin = jnp.clip(gu[:, de:], -SWIGLU_LIMIT, SWIGLU_LIMIT)
        act = g / (1.0 + jnp.exp(-SWIGLU_ALPHA * g)) * (lin + 1.0)
        y = _dot(act.astype(BF16), wd_bf[...]) + bd_ref[0]
        row = lax.broadcasted_iota(jnp.int32, (y.shape[0], 1), 0)
        mine = jnp.logical_and(row >= lo_ref[i], row < hi_ref[i])
        ys_ref[...] = jnp.where(mine, y, ys_ref[...])


def _moe_call(sched, xs, wgu, bgu, wd, bd):
    n_items = sched[0].shape[0]
    _, d, de2 = wgu.shape
    de = de2 // 2
    wmap = lambda i, blk, exp, lo, hi, cast, init: (exp[i], 0, 0)
    xmap = lambda i, blk, exp, lo, hi, cast, init: (blk[i], 0)
    return pl.pallas_call(
        _moe_kernel,
        grid_spec=pltpu.PrefetchScalarGridSpec(
            num_scalar_prefetch=6,
            grid=(n_items,),
            in_specs=[
                pl.BlockSpec((MOE_BLOCK, d), xmap),
                pl.BlockSpec((1, d, de2), wmap),
                pl.BlockSpec((1, 1, de2), wmap),
                pl.BlockSpec((1, de, d), wmap),
                pl.BlockSpec((1, 1, d), wmap),
            ],
            out_specs=pl.BlockSpec((MOE_BLOCK, d), xmap),
            scratch_shapes=[pltpu.VMEM((d, de2), BF16), pltpu.VMEM((de, d), BF16)],
        ),
        out_shape=jax.ShapeDtypeStruct(xs.shape, F32),
        compiler_params=pltpu.CompilerParams(
            dimension_semantics=("arbitrary",), vmem_limit_bytes=VMEM_LIMIT),
        name="moe",
    )(*sched, xs, wgu, bgu, wd, bd)


def _moe_schedule(counts, n_rows):
    n_blocks = n_rows // MOE_BLOCK
    n_items = n_blocks + N_EXPERTS
    ends = jnp.cumsum(counts)
    starts = ends - counts
    first_blk = starts // MOE_BLOCK
    last_blk = (ends - 1) // MOE_BLOCK
    items_per = jnp.where(counts > 0, last_blk - first_blk + 1, 0)
    item_ends = jnp.cumsum(items_per)
    item_starts = item_ends - items_per
    total = item_ends[-1]
    it = jnp.arange(n_items, dtype=jnp.int32)
    live = it < total
    itc = jnp.minimum(it, total - 1)
    exp = jnp.sum((item_ends[None, :] <= itc[:, None]).astype(jnp.int32), axis=1)
    blk = first_blk[exp] + itc - item_starts[exp]
    lo = jnp.clip(starts[exp] - blk * MOE_BLOCK, 0, MOE_BLOCK)
    hi = jnp.clip(ends[exp] - blk * MOE_BLOCK, 0, MOE_BLOCK)
    hi = jnp.where(live, hi, lo)
    prev_exp = jnp.concatenate([jnp.full((1,), -1, jnp.int32), exp[:-1]])
    prev_blk = jnp.concatenate([jnp.full((1,), -1, jnp.int32), blk[:-1]])
    cast = jnp.logical_and(live, exp != prev_exp)
    init = jnp.logical_and(live, blk != prev_blk)
    as_i32 = lambda a: a.astype(jnp.int32)
    return tuple(as_i32(a) for a in (blk, exp, lo, hi, cast, init)), starts


def _combine_kernel(yg_ref, x1_ref, gate_ref, mod_ref, fg_ref, o_ref):
    d = x1_ref.shape[1]
    gates = gate_ref[...]
    y = gates[:, 0:1] * yg_ref[:, 0:d]
    for kk in range(1, TOP_K):
        y = y + gates[:, kk:kk + 1] * yg_ref[:, kk * d:(kk + 1) * d]
    m = mod_ref[0]
    x2 = x1_ref[...] + m[5:6] * y
    o_ref[...] = _rms(x2) * fg_ref[...]


def _combine_call(yg, x1, gates, mod, fg, n_seq):
    t, d = x1.shape
    tt = COMBINE_TT
    tiles_per_seq = n_seq // tt
    tok = lambda i: (i, 0)
    return pl.pallas_call(
        _combine_kernel,
        grid=(t // tt,),
        in_specs=[
            pl.BlockSpec((tt, TOP_K * d), tok),
            pl.BlockSpec((tt, d), tok),
            pl.BlockSpec((tt, LANES), tok),
            pl.BlockSpec((1, N_MOD, d), lambda i: (i // tiles_per_seq, 0, 0)),
            pl.BlockSpec((1, d), lambda i: (0, 0)),
        ],
        out_specs=pl.BlockSpec((tt, d), tok),
        out_shape=jax.ShapeDtypeStruct((t, d), F32),
        compiler_params=pltpu.CompilerParams(
            dimension_semantics=("arbitrary",), vmem_limit_bytes=VMEM_LIMIT),
        name="combine",
    )(yg, x1, gates, mod, fg)


def _pad_cols(a, width):
    return jnp.pad(a, ((0, 0), (0, width - a.shape[1])))


def _rope_slab(w_rope, swapped):
    half = QK_ROPE // 2
    if swapped:
        w_rope = jnp.concatenate([w_rope[:, half:], w_rope[:, :half]], axis=1)
    zeros = jnp.zeros((w_rope.shape[0], QK_NOPE), w_rope.dtype)
    return _pad_cols(jnp.concatenate([zeros, w_rope], axis=1), LANES)


def _prep_w_in(w_in):
    kr0 = Q_LORA + KV_LORA
    w_kr = w_in[:, kr0:kr0 + QK_ROPE]
    return jnp.concatenate(
        [w_in[:, :kr0], _rope_slab(w_kr, False), _rope_slab(w_kr, True), w_in[:, kr0 + QK_ROPE:]],
        axis=1).astype(BF16)


def _prep_w_uq(w_uq):
    per = QK_NOPE + QK_ROPE
    slabs_a, slabs_b = [], []
    for hd in range(N_HEADS):
        w = w_uq[:, hd * per:(hd + 1) * per]
        slabs_a.append(_pad_cols(w, LANES))
        slabs_b.append(_rope_slab(w[:, QK_NOPE:], True))
    return jnp.concatenate(slabs_a + slabs_b, axis=1).astype(BF16)


def _prep_w_ukv(w_ukv):
    per = QK_NOPE + V_DIM
    ks, vs = [], []
    for hd in range(N_HEADS):
        w = w_ukv[:, hd * per:(hd + 1) * per]
        ks.append(_pad_cols(w[:, :QK_NOPE], LANES))
        vs.append(w[:, QK_NOPE:])
    return jnp.concatenate(ks + vs, axis=1).astype(BF16)


def _rope_tables(n_lat):
    rows = n_lat // GRID_W
    nf = QK_ROPE // 4
    row = jnp.repeat(jnp.arange(rows, dtype=F32), GRID_W)
    col = jnp.tile(jnp.arange(GRID_W, dtype=F32), rows)
    freqs = ROPE_BASE ** (-jnp.arange(nf, dtype=F32) / nf)
    ang = jnp.concatenate([row[:, None] * freqs, col[:, None] * freqs], axis=-1)
    cos, sin = jnp.cos(ang), jnp.sin(ang)
    ones = jnp.ones((n_lat, QK_NOPE), F32)
    zeros = jnp.zeros((n_lat, QK_NOPE), F32)
    cs = _pad_cols(jnp.concatenate([ones, cos, cos], axis=1), LANES)
    sn = _pad_cols(jnp.concatenate([zeros, -sin, sin], axis=1), LANES)
    return cs, sn


def kernel(x, c, ctx, c_ctx, w_mod, b_mod, norm1_g, w_in, q_norm_g, kv_norm_g, w_uq, w_ukv, w_pool,
           pool_scale, w_out, norm2_g, router_w, router_b, w_gate_up, b_gate_up, w_down, b_down,
           final_g):
    bsz, n, d = x.shape
    n_ctx = ctx.shape[1]
    t = bsz * n
    l = 0

    cc = jnp.concatenate([c, c_ctx[None, :], jnp.zeros((8 - bsz - 1, d), F32)], axis=0)
    mod = _mod_call(cc, w_mod[l], b_mod[l][None, :]).reshape(8, N_MOD, d)
    mod_lat, mod_ctx = mod[:bsz], mod[bsz:bsz + 1]

    win = _prep_w_in(w_in[l])
    wuq = _prep_w_uq(w_uq[l])
    wukv = _prep_w_ukv(w_ukv[l])
    cs, sn = _rope_tables(n)
    cs_ctx = jnp.broadcast_to((jnp.arange(LANES) < QK_NOPE + QK_ROPE).astype(F32), (n_ctx, LANES))
    sn_ctx = jnp.zeros((n_ctx, LANES), F32)
    g1 = norm1_g[l][None, :]
    qg = q_norm_g[l][None, :]
    kvg = kv_norm_g[l][None, :]

    q, k, v, u = _front_call(x, mod_lat, True, g1, win, qg, kvg, wuq, wukv, cs, sn,
                             is_ctx=False, ts=FRONT_TS)
    kc, vc = _front_call(ctx, mod_ctx, False, g1, win[:, Q_LORA:Q_LORA + 3 * LANES], qg, kvg, wuq,
                         wukv, cs_ctx, sn_ctx, is_ctx=True, ts=n_ctx)
    att = _attn_call(q, k, v, kc, vc)

    rw = _pad_cols(router_w[l], LANES)
    rb = jnp.concatenate([router_b[l], jnp.full((LANES - N_EXPERTS,), -jnp.inf, F32)])[None, :]
    x1, h2, gates, ridx, cnt = _mix_call(
        att, u, x, mod_lat, w_pool[l].astype(BF16), pool_scale[l][None, :], w_out[l].astype(BF16),
        norm2_g[l][None, :], rw, rb)

    counts = cnt[0, :N_EXPERTS].astype(jnp.int32)
    sched, starts = _moe_schedule(counts, t * TOP_K)
    ridx = ridx.reshape(t, LANES)
    dest = starts[ridx[:, :TOP_K]] + ridx[:, TOP_K:2 * TOP_K]
    dest3 = dest.T.reshape(TOP_K, t // SC_ROWS, SC_ROWS)

    xs = _scatter_rows(h2.reshape(t, d), dest3, t * TOP_K)
    ys = _moe_call(sched, xs, w_gate_up[l], b_gate_up[l][:, None, :], w_down[l],
                   b_down[l][:, None, :])
    yg = _gather_rows(ys, dest.reshape(-1)).reshape(t, TOP_K * d)
    out = _combine_call(yg, x1.reshape(t, d), gates.reshape(t, LANES), mod_lat, final_g[None, :], n)
    return out.reshape(bsz, n, d)
```

```python
import functools
import math

import jax
import jax.numpy as jnp
from jax import lax
from jax.experimental import pallas as pl
from jax.experimental.pallas import tpu as pltpu
from jax.experimental.pallas import tpu_sc as plsc

F32 = jnp.float32
BF16 = jnp.bfloat16
HIGHEST = lax.Precision.HIGHEST

N_HEADS = 8
QK_NOPE = 64
QK_ROPE = 32
V_DIM = 64
Q_LORA = 256
KV_LORA = 128
GRID_W = 64
ROPE_BASE = 10000.0
POOL_WINDOWS = (2, 4, 8, 16)
POOL_CH = 128
N_EXPERTS = 32
TOP_K = 4
SWIGLU_LIMIT = 7.0
SWIGLU_ALPHA = 1.702
MOE_BLOCK = 256
N_MOD = 6
EPS = 1e-6

LANES = 128
POOL_HALO = 8
VMEM_LIMIT = 56 * 1024 * 1024

FRONT_TS = 512
ATTN_TQ = 512
ATTN_KC = 512
MIX_TS = 512
COMBINE_TT = 256
SC_ROWS = 64


def _dot(a, b, **kw):
    return jnp.dot(a, b, preferred_element_type=F32, **kw)


def _dot_nt(a, b):
    return lax.dot_general(a, b, (((1,), (1,)), ((), ())), preferred_element_type=F32)


def _rms(x):
    return x * lax.rsqrt(jnp.mean(x * x, axis=-1, keepdims=True) + EPS)


def _mod_kernel(c_ref, w_ref, b_ref, o_ref):
    c = c_ref[...]
    a = c / (1.0 + jnp.exp(-c))
    o_ref[...] = _dot(a, w_ref[...], precision=HIGHEST) + b_ref[...]


def _mod_call(cc, w_mod, b_mod):
    d = w_mod.shape[0]
    return pl.pallas_call(
        _mod_kernel,
        grid=(N_MOD,),
        in_specs=[
            pl.BlockSpec((8, d), lambda j: (0, 0)),
            pl.BlockSpec((d, d), lambda j: (0, j)),
            pl.BlockSpec((1, d), lambda j: (0, j)),
        ],
        out_specs=pl.BlockSpec((8, d), lambda j: (0, j)),
        out_shape=jax.ShapeDtypeStruct((8, N_MOD * d), F32),
        compiler_params=pltpu.CompilerParams(
            dimension_semantics=("arbitrary",), vmem_limit_bytes=VMEM_LIMIT),
        name="mod",
    )(cc, w_mod, b_mod)


def _front_kernel(x_ref, mod_ref, g1_ref, win_ref, qg_ref, kvg_ref, wuq_ref, wukv_ref,
                  cs_ref, sn_ref, *out_refs, is_ctx, scale):
    x = x_ref[0]
    m = mod_ref[0]
    h = _rms(x) * g1_ref[...] * (1.0 + m[1:2]) + m[0:1]
    z = _dot(h.astype(BF16), win_ref[...])
    if is_ctx:
        k_ref, v_ref = out_refs
        zkv = z
    else:
        q_ref, k_ref, v_ref, u_ref = out_refs
        zkv = z[:, Q_LORA:Q_LORA + 3 * LANES]
        u_ref[0] = z[:, Q_LORA + 3 * LANES:]
    cs = cs_ref[...]
    sn = sn_ref[...]

    ckv = _rms(zkv[:, :KV_LORA]) * kvg_ref[...]
    kv = _dot(ckv.astype(BF16), wukv_ref[...])
    kr = zkv[:, LANES:2 * LANES] * cs + zkv[:, 2 * LANES:3 * LANES] * sn
    for hd in range(N_HEADS):
        k_ref[0, hd] = (kv[:, hd * LANES:(hd + 1) * LANES] + kr).astype(BF16)
    voff = N_HEADS * LANES
    for j in range(N_HEADS // 2):
        v_ref[0, j] = kv[:, voff + j * LANES: voff + (j + 1) * LANES].astype(BF16)

    if not is_ctx:
        cq = _rms(z[:, :Q_LORA]) * qg_ref[...]
        qq = _dot(cq.astype(BF16), wuq_ref[...])
        boff = N_HEADS * LANES
        for hd in range(N_HEADS):
            qa = qq[:, hd * LANES:(hd + 1) * LANES]
            qb = qq[:, boff + hd * LANES: boff + (hd + 1) * LANES]
            q_ref[0, hd] = ((qa * cs + qb * sn) * scale).astype(BF16)


def _front_call(xs, mod, mod_per_batch, g1, win, qg, kvg, wuq, wukv, cs, sn, *, is_ctx, ts):
    bsz, n, d = xs.shape
    nt = n // ts
    scale = 1.0 / math.sqrt(QK_NOPE + QK_ROPE)
    const = lambda b, i: (0, 0)
    mod_map = (lambda b, i: (b, 0, 0)) if mod_per_batch else (lambda b, i: (0, 0, 0))
    in_specs = [
        pl.BlockSpec((1, ts, d), lambda b, i: (b, i, 0)),
        pl.BlockSpec((1, N_MOD, d), mod_map),
        pl.BlockSpec((1, d), const),
        pl.BlockSpec(win.shape, const),
        pl.BlockSpec((1, Q_LORA), const),
        pl.BlockSpec((1, KV_LORA), const),
        pl.BlockSpec(wuq.shape, const),
        pl.BlockSpec(wukv.shape, const),
        pl.BlockSpec((ts, LANES), lambda b, i: (i, 0)),
        pl.BlockSpec((ts, LANES), lambda b, i: (i, 0)),
    ]
    k_spec = pl.BlockSpec((1, N_HEADS, ts, LANES), lambda b, i: (b, 0, i, 0))
    v_spec = pl.BlockSpec((1, N_HEADS // 2, ts, LANES), lambda b, i: (b, 0, i, 0))
    k_shape = jax.ShapeDtypeStruct((bsz, N_HEADS, n, LANES), BF16)
    v_shape = jax.ShapeDtypeStruct((bsz, N_HEADS // 2, n, LANES), BF16)
    if is_ctx:
        out_specs = [k_spec, v_spec]
        out_shape = [k_shape, v_shape]
    else:
        pool_w = win.shape[1] - Q_LORA - 3 * LANES
        out_specs = [k_spec, k_spec, v_spec, pl.BlockSpec((1, ts, pool_w), lambda b, i: (b, i, 0))]
        out_shape = [k_shape, k_shape, v_shape, jax.ShapeDtypeStruct((bsz, n, pool_w), F32)]
    return pl.pallas_call(
        functools.partial(_front_kernel, is_ctx=is_ctx, scale=scale),
        grid=(bsz, nt),
        in_specs=in_specs,
        out_specs=out_specs,
        out_shape=out_shape,
        compiler_params=pltpu.CompilerParams(
            dimension_semantics=("arbitrary", "arbitrary"), vmem_limit_bytes=VMEM_LIMIT),
        name="front_ctx" if is_ctx else "front",
    )(xs, mod, g1, win, qg, kvg, wuq, wukv, cs, sn)


def _attn_kernel(q_ref, k_ref, v_ref, kc_ref, vc_ref, o_ref, s_ref, ml_ref, ll_ref, acc_ref):
    n_ctx = kc_ref.shape[2]
    n_lat = k_ref.shape[2]
    n_chunks = n_lat // ATTN_KC
    groups = ATTN_KC // LANES

    def one_head(hd, j):
        q = q_ref[0, hd]
        s = _dot_nt(q, kc_ref[0, hd])
        s_ref[:, 0:n_ctx] = s
        ml = s[:, 0:LANES]
        for g in range(1, n_ctx // LANES):
            ml = jnp.maximum(ml, s[:, g * LANES:(g + 1) * LANES])
        ml_ref[...] = ml

        def scores(c, carry):
            off = pl.multiple_of(c * ATTN_KC, ATTN_KC)
            s = _dot_nt(q, k_ref[0, hd, pl.ds(off, ATTN_KC), :])
            s_ref[:, pl.ds(pl.multiple_of(n_ctx + off, LANES), ATTN_KC)] = s
            ml = ml_ref[...]
            for g in range(groups):
                ml = jnp.maximum(ml, s[:, g * LANES:(g + 1) * LANES])
            ml_ref[...] = ml
            return carry

        lax.fori_loop(0, n_chunks, scores, 0)
        mx = jnp.max(ml_ref[...], axis=-1, keepdims=True)

        p = jnp.exp(s_ref[:, 0:n_ctx] - mx)
        ll = p[:, 0:LANES]
        for g in range(1, n_ctx // LANES):
            ll = ll + p[:, g * LANES:(g + 1) * LANES]
        ll_ref[...] = ll
        acc_ref[...] = _dot(p.astype(BF16), vc_ref[0, j])

        def weighted(c, carry):
            off = pl.multiple_of(c * ATTN_KC, ATTN_KC)
            p = jnp.exp(s_ref[:, pl.ds(pl.multiple_of(n_ctx + off, LANES), ATTN_KC)] - mx)
            ll = ll_ref[...]
            for g in range(groups):
                ll = ll + p[:, g * LANES:(g + 1) * LANES]
            ll_ref[...] = ll
            acc_ref[...] += _dot(p.astype(BF16), v_ref[0, j, pl.ds(off, ATTN_KC), :])
            return carry

        lax.fori_loop(0, n_chunks, weighted, 0)
        denom = jnp.sum(ll_ref[...], axis=-1, keepdims=True)
        return acc_ref[...] / denom

    def head_pair(j, carry):
        o0 = one_head(2 * j, j)
        o1 = one_head(2 * j + 1, j)
        lane = lax.broadcasted_iota(jnp.int32, o0.shape, 1)
        o_ref[0, j] = jnp.where(lane < V_DIM, o0, o1).astype(BF16)
        return carry

    lax.fori_loop(0, N_HEADS // 2, head_pair, 0)


def _attn_call(q, k, v, kc, vc):
    bsz, _, n, _ = q.shape
    n_ctx = kc.shape[2]
    tq = ATTN_TQ
    return pl.pallas_call(
        _attn_kernel,
        grid=(bsz, n // tq),
        in_specs=[
            pl.BlockSpec((1, N_HEADS, tq, LANES), lambda b, i: (b, 0, i, 0)),
            pl.BlockSpec((1, N_HEADS, n, LANES), lambda b, i: (b, 0, 0, 0)),
            pl.BlockSpec((1, N_HEADS // 2, n, LANES), lambda b, i: (b, 0, 0, 0)),
            pl.BlockSpec((1, N_HEADS, n_ctx, LANES), lambda b, i: (b, 0, 0, 0)),
            pl.BlockSpec((1, N_HEADS // 2, n_ctx, LANES), lambda b, i: (b, 0, 0, 0)),
        ],
        out_specs=pl.BlockSpec((1, N_HEADS // 2, tq, LANES), lambda b, i: (b, 0, i, 0)),
        out_shape=jax.ShapeDtypeStruct((bsz, N_HEADS // 2, n, LANES), BF16),
        scratch_shapes=[
            pltpu.VMEM((tq, n_ctx + n), F32),
            pltpu.VMEM((tq, LANES), F32),
            pltpu.VMEM((tq, LANES), F32),
            pltpu.VMEM((tq, LANES), F32),
        ],
        compiler_params=pltpu.CompilerParams(
            dimension_semantics=("arbitrary", "arbitrary"), vmem_limit_bytes=VMEM_LIMIT),
        name="attn",
    )(q, k, v, kc, vc)


def _mix_kernel(att_ref, u_ref, up_ref, un_ref, x_ref, mod_ref, wpool_ref, pscale_ref, wout_ref,
                g2_ref, rw_ref, rb_ref,
                x1_ref, h2_ref, gate_ref, ridx_ref, cnt_ref, ue_ref, base_ref, *, n_seq):
    b = pl.program_id(0)
    i = pl.program_id(1)
    n_tiles = pl.num_programs(1)
    ts = u_ref.shape[1]

    @pl.when(jnp.logical_and(b == 0, i == 0))
    def _():
        base_ref[...] = jnp.zeros_like(base_ref)

    u = u_ref[0]
    ue_ref[0:POOL_HALO] = jnp.where(i > 0, up_ref[0], 0.0)
    ue_ref[POOL_HALO:POOL_HALO + ts] = u
    ue_ref[POOL_HALO + ts:2 * POOL_HALO + ts] = jnp.where(i < n_tiles - 1, un_ref[0], 0.0)
    t = i * ts + lax.broadcasted_iota(jnp.int32, (ts, 1), 0)
    ys = []
    for g, w in enumerate(POOL_WINDOWS):
        half = w // 2
        lanes = slice(g * POOL_CH, (g + 1) * POOL_CH)
        ws = ue_ref[POOL_HALO - half:POOL_HALO - half + ts, lanes]
        for jj in range(-half + 1, half):
            ws = ws + ue_ref[POOL_HALO + jj:POOL_HALO + jj + ts, lanes]
        count = (jnp.minimum(t + half, n_seq) - jnp.maximum(t - half, 0)).astype(F32)
        mixed = (ws / count - u[:, lanes]).astype(BF16)
        ys.append(_dot(mixed, wpool_ref[g]))
    pool = jnp.concatenate(ys, axis=-1) * pscale_ref[...]

    cat = jnp.concatenate([att_ref[0, j] for j in range(N_HEADS // 2)] + [pool.astype(BF16)], axis=-1)
    m = mod_ref[0]
    x1 = x_ref[0] + m[2:3] * _dot(cat, wout_ref[...])
    x1_ref[0] = x1
    h2 = _rms(x1) * g2_ref[...] * (1.0 + m[4:5]) + m[3:4]
    h2_ref[0] = h2

    logits = _dot(h2, rw_ref[...], precision=HIGHEST) + rb_ref[...]
    lane = lax.broadcasted_iota(jnp.int32, logits.shape, 1).astype(F32)
    vals, idxs = [], []
    for _k in range(TOP_K):
        mv = jnp.max(logits, axis=-1, keepdims=True)
        ix = jnp.min(jnp.where(logits == mv, lane, float(LANES)), axis=-1, keepdims=True)
        vals.append(mv)
        idxs.append(ix)
        logits = jnp.where(lane == ix, -jnp.inf, logits)
    es = [jnp.exp(v - vals[0]) for v in vals]
    den = es[0] + es[1] + es[2] + es[3]

    onehot = jnp.zeros(lane.shape, F32)
    for ix in idxs:
        onehot = onehot + jnp.where(lane == ix, 1.0, 0.0)
    row = lax.broadcasted_iota(jnp.int32, (ts, ts), 0)
    col = lax.broadcasted_iota(jnp.int32, (ts, ts), 1)
    tri = jnp.where(col < row, 1.0, 0.0).astype(BF16)
    before = _dot(tri, onehot.astype(BF16)) + base_ref[0:1, :]
    base_new = base_ref[0:1, :] + jnp.sum(onehot, axis=0, keepdims=True)
    base_ref[...] = jnp.broadcast_to(base_new, base_ref.shape)
    cnt_ref[...] = jnp.broadcast_to(base_new, cnt_ref.shape)

    gate_out = jnp.zeros(lane.shape, F32)
    ridx_out = jnp.zeros(lane.shape, F32)
    for kk in range(TOP_K):
        rank = jnp.sum(jnp.where(lane == idxs[kk], before, 0.0), axis=-1, keepdims=True)
        gate_out = jnp.where(lane == float(kk), es[kk] / den, gate_out)
        ridx_out = jnp.where(lane == float(kk), idxs[kk], ridx_out)
        ridx_out = jnp.where(lane == float(TOP_K + kk), rank, ridx_out)
    gate_ref[0] = gate_out
    ridx_ref[0] = ridx_out.astype(jnp.int32)


def _mix_call(att, u, x, mod, wpool, pscale, wout, g2, rw, rb):
    bsz, n, d = x.shape
    ts = MIX_TS
    nt = n // ts
    pool_w = u.shape[2]
    hb = ts // POOL_HALO
    n_halo_blocks = n // POOL_HALO
    const2 = lambda b, i: (0, 0)
    tok = lambda b, i: (b, i, 0)
    return pl.pallas_call(
        functools.partial(_mix_kernel, n_seq=n),
        grid=(bsz, nt),
        in_specs=[
            pl.BlockSpec((1, N_HEADS // 2, ts, LANES), lambda b, i: (b, 0, i, 0)),
            pl.BlockSpec((1, ts, pool_w), tok),
            pl.BlockSpec((1, POOL_HALO, pool_w), lambda b, i: (b, jnp.maximum(i * hb - 1, 0), 0)),
            pl.BlockSpec((1, POOL_HALO, pool_w),
                         lambda b, i: (b, jnp.minimum((i + 1) * hb, n_halo_blocks - 1), 0)),
            pl.BlockSpec((1, ts, d), tok),
            pl.BlockSpec((1, N_MOD, d), lambda b, i: (b, 0, 0)),
            pl.BlockSpec(wpool.shape, lambda b, i: (0, 0, 0)),
            pl.BlockSpec((1, pool_w), const2),
            pl.BlockSpec(wout.shape, const2),
            pl.BlockSpec((1, d), const2),
            pl.BlockSpec(rw.shape, const2),
            pl.BlockSpec((1, LANES), const2),
        ],
        out_specs=[
            pl.BlockSpec((1, ts, d), tok),
            pl.BlockSpec((1, ts, d), tok),
            pl.BlockSpec((1, ts, LANES), tok),
            pl.BlockSpec((1, ts, LANES), tok),
            pl.BlockSpec((8, LANES), const2),
        ],
        out_shape=[
            jax.ShapeDtypeStruct((bsz, n, d), F32),
            jax.ShapeDtypeStruct((bsz, n, d), F32),
            jax.ShapeDtypeStruct((bsz, n, LANES), F32),
            jax.ShapeDtypeStruct((bsz, n, LANES), jnp.int32),
            jax.ShapeDtypeStruct((8, LANES), F32),
        ],
        scratch_shapes=[
            pltpu.VMEM((ts + 2 * POOL_HALO, pool_w), F32),
            pltpu.VMEM((8, LANES), F32),
        ],
        compiler_params=pltpu.CompilerParams(
            dimension_semantics=("arbitrary", "arbitrary"), vmem_limit_bytes=VMEM_LIMIT),
        name="mix",
    )(att, u, u, u, x, mod, wpool, pscale, wout, g2, rw, rb)


def _sc_mesh():
    return plsc.VectorSubcoreMesh(core_axis_name="core", subcore_axis_name="subcore")


def _sc_worker_id():
    info = plsc.get_sparse_core_info()
    return lax.axis_index("subcore") * info.num_cores + lax.axis_index("core")


def _sc_num_workers():
    info = plsc.get_sparse_core_info()
    return info.num_cores * info.num_subcores


def _scatter_rows(x, dest3, n_out):
    t, d = x.shape
    top_k, n_chunks, rows = dest3.shape
    per_worker = n_chunks // _sc_num_workers()

    @functools.partial(
        pl.kernel,
        out_type=jax.ShapeDtypeStruct((n_out, d), x.dtype),
        mesh=_sc_mesh(),
        scratch_types=[pltpu.VMEM((top_k, rows), jnp.int32), pltpu.VMEM((rows, d), x.dtype)],
        name="sc_scatter",
    )
    def scatter(x_hbm, i_hbm, o_hbm, idx_v, rows_v):
        wid = _sc_worker_id()

        @pl.loop(0, per_worker)
        def _(c):
            chunk = wid * per_worker + c
            for kk in range(top_k):
                pltpu.sync_copy(i_hbm.at[kk, chunk], idx_v.at[kk])
            pltpu.sync_copy(x_hbm.at[pl.ds(pl.multiple_of(chunk * rows, 8), rows)], rows_v)
            for kk in range(top_k):
                pltpu.sync_copy(rows_v, o_hbm.at[idx_v.at[kk]])

    return scatter(x, dest3)


def _gather_rows(y, idx):
    n = idx.shape[0]
    d = y.shape[1]
    rows = SC_ROWS
    per_worker = n // _sc_num_workers()
    n_chunks = per_worker // rows

    @functools.partial(
        pl.kernel,
        out_type=jax.ShapeDtypeStruct((n, d), y.dtype),
        mesh=_sc_mesh(),
        scratch_types=[pltpu.VMEM((rows,), jnp.int32), pltpu.VMEM((rows, d), y.dtype)],
        name="sc_gather",
    )
    def gather(y_hbm, i_hbm, o_hbm, idx_v, rows_v):
        base = _sc_worker_id() * per_worker

        @pl.loop(0, n_chunks)
        def _(c):
            off = pl.multiple_of(base + c * rows, 8)
            pltpu.sync_copy(i_hbm.at[pl.ds(off, rows)], idx_v)
            pltpu.sync_copy(y_hbm.at[idx_v], rows_v)
            pltpu.sync_copy(rows_v, o_hbm.at[pl.ds(off, rows)])

    return gather(y, idx)


def _moe_kernel(blk_ref, exp_ref, lo_ref, hi_ref, cast_ref, init_ref, xs_ref, wgu_ref, bgu_ref,
                wd_ref, bd_ref, ys_ref, wgu_bf, wd_bf):
    i = pl.program_id(0)

    @pl.when(cast_ref[i] == 1)
    def _():
        wgu_bf[...] = wgu_ref[0].astype(BF16)
        wd_bf[...] = wd_ref[0].astype(BF16)

    @pl.when(init_ref[i] == 1)
    def _():
        ys_ref[...] = jnp.zeros_like(ys_ref)

    @pl.when(hi_ref[i] > lo_ref[i])
    def _():
        de = wd_bf.shape[0]
        gu = _dot(xs_ref[...].astype(BF16), wgu_bf[...]) + bgu_ref[0]
        g = jnp.minimum(gu[:, :de], SWIGLU_LIMIT)
        lin = jnp.clip(gu[:, de:], -SWIGLU_LIMIT, SWIGLU_LIMIT)
        act = g / (1.0 + jnp.exp(-SWIGLU_ALPHA * g)) * (lin + 1.0)
        y = _dot(act.astype(BF16), wd_bf[...]) + bd_ref[0]
        row = lax.broadcasted_iota(jnp.int32, (y.shape[0], 1), 0)
        mine = jnp.logical_and(row >= lo_ref[i], row < hi_ref[i])
        ys_ref[...] = jnp.where(mine, y, ys_ref[...])


def _moe_call(sched, xs, wgu, bgu, wd, bd):
    n_items = sched[0].shape[0]
    _, d, de2 = wgu.shape
    de = de2 // 2
    wmap = lambda i, blk, exp, lo, hi, cast, init: (exp[i], 0, 0)
    xmap = lambda i, blk, exp, lo, hi, cast, init: (blk[i], 0)
    return pl.pallas_call(
        _moe_kernel,
        grid_spec=pltpu.PrefetchScalarGridSpec(
            num_scalar_prefetch=6,
            grid=(n_items,),
            in_specs=[
                pl.BlockSpec((MOE_BLOCK, d), xmap),
                pl.BlockSpec((1, d, de2), wmap),
                pl.BlockSpec((1, 1, de2), wmap),
                pl.BlockSpec((1, de, d), wmap),
                pl.BlockSpec((1, 1, d), wmap),
            ],
            out_specs=pl.BlockSpec((MOE_BLOCK, d), xmap),
            scratch_shapes=[pltpu.VMEM((d, de2), BF16), pltpu.VMEM((de, d), BF16)],
        ),
        out_shape=jax.ShapeDtypeStruct(xs.shape, F32),
        compiler_params=pltpu.CompilerParams(
            dimension_semantics=("arbitrary",), vmem_limit_bytes=VMEM_LIMIT),
        name="moe",
    )(*sched, xs, wgu, bgu, wd, bd)


def _moe_schedule(counts, n_rows):
    n_blocks = n_rows // MOE_BLOCK
    n_items = n_blocks + N_EXPERTS
    ends = jnp.cumsum(counts)
    starts = ends - counts
    first_blk = starts // MOE_BLOCK
    last_blk = (ends - 1) // MOE_BLOCK
    items_per = jnp.where(counts > 0, last_blk - first_blk + 1, 0)
    item_ends = jnp.cumsum(items_per)
    item_starts = item_ends - items_per
    total = item_ends[-1]
    it = jnp.arange(n_items, dtype=jnp.int32)
    live = it < total
    itc = jnp.minimum(it, total - 1)
    exp = jnp.sum((item_ends[None, :] <= itc[:, None]).astype(jnp.int32), axis=1)
    is_exp = exp[:, None] == jnp.arange(N_EXPERTS, dtype=jnp.int32)[None, :]
    pick = lambda table: jnp.sum(jnp.where(is_exp, table[None, :], 0), axis=1)
    blk = pick(first_blk) + itc - pick(item_starts)
    lo = jnp.clip(pick(starts) - blk * MOE_BLOCK, 0, MOE_BLOCK)
    hi = jnp.clip(pick(ends) - blk * MOE_BLOCK, 0, MOE_BLOCK)
    hi = jnp.where(live, hi, lo)
    prev_exp = jnp.concatenate([jnp.full((1,), -1, jnp.int32), exp[:-1]])
    prev_blk = jnp.concatenate([jnp.full((1,), -1, jnp.int32), blk[:-1]])
    cast = jnp.logical_and(live, exp != prev_exp)
    init = jnp.logical_and(live, blk != prev_blk)
    as_i32 = lambda a: a.astype(jnp.int32)
    return tuple(as_i32(a) for a in (blk, exp, lo, hi, cast, init)), starts


def _combine_kernel(*refs):
    yg_refs = refs[:TOP_K]
    x1_ref, gate_ref, mod_ref, fg_ref, o_ref = refs[TOP_K:]
    gates = gate_ref[...]
    y = gates[:, 0:1] * yg_refs[0][...]
    for kk in range(1, TOP_K):
        y = y + gates[:, kk:kk + 1] * yg_refs[kk][...]
    m = mod_ref[0]
    x2 = x1_ref[...] + m[5:6] * y
    o_ref[...] = _rms(x2) * fg_ref[...]


def _combine_call(yg, x1, gates, mod, fg, n_seq):
    t, d = x1.shape
    tt = COMBINE_TT
    tiles_per_seq = n_seq // tt
    tok = lambda i: (i, 0)
    n_tiles = t // tt
    slot_specs = [pl.BlockSpec((tt, d), functools.partial(lambda kk, i: (kk * n_tiles + i, 0), kk))
                  for kk in range(TOP_K)]
    return pl.pallas_call(
        _combine_kernel,
        grid=(n_tiles,),
        in_specs=slot_specs + [
            pl.BlockSpec((tt, d), tok),
            pl.BlockSpec((tt, LANES), tok),
            pl.BlockSpec((1, N_MOD, d), lambda i: (i // tiles_per_seq, 0, 0)),
            pl.BlockSpec((1, d), lambda i: (0, 0)),
        ],
        out_specs=pl.BlockSpec((tt, d), tok),
        out_shape=jax.ShapeDtypeStruct((t, d), F32),
        compiler_params=pltpu.CompilerParams(
            dimension_semantics=("arbitrary",), vmem_limit_bytes=VMEM_LIMIT),
        name="combine",
    )(*([yg] * TOP_K), x1, gates, mod, fg)


def _pad_cols(a, width):
    return jnp.pad(a, ((0, 0), (0, width - a.shape[1])))


def _rope_slab(w_rope, swapped):
    half = QK_ROPE // 2
    if swapped:
        w_rope = jnp.concatenate([w_rope[:, half:], w_rope[:, :half]], axis=1)
    zeros = jnp.zeros((w_rope.shape[0], QK_NOPE), w_rope.dtype)
    return _pad_cols(jnp.concatenate([zeros, w_rope], axis=1), LANES)


def _prep_w_in(w_in):
    kr0 = Q_LORA + KV_LORA
    w_kr = w_in[:, kr0:kr0 + QK_ROPE]
    return jnp.concatenate(
        [w_in[:, :kr0], _rope_slab(w_kr, False), _rope_slab(w_kr, True), w_in[:, kr0 + QK_ROPE:]],
        axis=1).astype(BF16)


def _prep_w_uq(w_uq):
    per = QK_NOPE + QK_ROPE
    slabs_a, slabs_b = [], []
    for hd in range(N_HEADS):
        w = w_uq[:, hd * per:(hd + 1) * per]
        slabs_a.append(_pad_cols(w, LANES))
        slabs_b.append(_rope_slab(w[:, QK_NOPE:], True))
    return jnp.concatenate(slabs_a + slabs_b, axis=1).astype(BF16)


def _prep_w_ukv(w_ukv):
    per = QK_NOPE + V_DIM
    ks, vs = [], []
    for hd in range(N_HEADS):
        w = w_ukv[:, hd * per:(hd + 1) * per]
        ks.append(_pad_cols(w[:, :QK_NOPE], LANES))
        vs.append(w[:, QK_NOPE:])
    return jnp.concatenate(ks + vs, axis=1).astype(BF16)


def _rope_tables(n_lat):
    rows = n_lat // GRID_W
    nf = QK_ROPE // 4
    row = jnp.repeat(jnp.arange(rows, dtype=F32), GRID_W)
    col = jnp.tile(jnp.arange(GRID_W, dtype=F32), rows)
    freqs = ROPE_BASE ** (-jnp.arange(nf, dtype=F32) / nf)
    ang = jnp.concatenate([row[:, None] * freqs, col[:, None] * freqs], axis=-1)
    cos, sin = jnp.cos(ang), jnp.sin(ang)
    ones = jnp.ones((n_lat, QK_NOPE), F32)
    zeros = jnp.zeros((n_lat, QK_NOPE), F32)
    cs = _pad_cols(jnp.concatenate([ones, cos, cos], axis=1), LANES)
    sn = _pad_cols(jnp.concatenate([zeros, -sin, sin], axis=1), LANES)
    return cs, sn


def kernel(x, c, ctx, c_ctx, w_mod, b_mod, norm1_g, w_in, q_norm_g, kv_norm_g, w_uq, w_ukv, w_pool,
           pool_scale, w_out, norm2_g, router_w, router_b, w_gate_up, b_gate_up, w_down, b_down,
           final_g):
    bsz, n, d = x.shape
    n_ctx = ctx.shape[1]
    t = bsz * n
    l = 0

    cc = jnp.concatenate([c, c_ctx[None, :], jnp.zeros((8 - bsz - 1, d), F32)], axis=0)
    mod = _mod_call(cc, w_mod[l], b_mod[l][None, :]).reshape(8, N_MOD, d)
    mod_lat, mod_ctx = mod[:bsz], mod[bsz:bsz + 1]

    win = _prep_w_in(w_in[l])
    wuq = _prep_w_uq(w_uq[l])
    wukv = _prep_w_ukv(w_ukv[l])
    cs, sn = _rope_tables(n)
    cs_ctx = jnp.broadcast_to((jnp.arange(LANES) < QK_NOPE + QK_ROPE).astype(F32), (n_ctx, LANES))
    sn_ctx = jnp.zeros((n_ctx, LANES), F32)
    g1 = norm1_g[l][None, :]
    qg = q_norm_g[l][None, :]
    kvg = kv_norm_g[l][None, :]

    q, k, v, u = _front_call(x, mod_lat, True, g1, win, qg, kvg, wuq, wukv, cs, sn,
                             is_ctx=False, ts=FRONT_TS)
    kc, vc = _front_call(ctx, mod_ctx, False, g1, win[:, Q_LORA:Q_LORA + 3 * LANES], qg, kvg, wuq,
                         wukv, cs_ctx, sn_ctx, is_ctx=True, ts=n_ctx)
    att = _attn_call(q, k, v, kc, vc)

    rw = _pad_cols(router_w[l], LANES)
    rb = jnp.concatenate([router_b[l], jnp.full((LANES - N_EXPERTS,), -jnp.inf, F32)])[None, :]
    x1, h2, gates, ridx, cnt = _mix_call(
        att, u, x, mod_lat, w_pool[l].astype(BF16), pool_scale[l][None, :], w_out[l].astype(BF16),
        norm2_g[l][None, :], rw, rb)

    counts = cnt[0, :N_EXPERTS].astype(jnp.int32)
    sched, starts = _moe_schedule(counts, t * TOP_K)
    ridx = ridx.reshape(t, LANES)
    is_exp = ridx[:, :TOP_K, None] == jnp.arange(N_EXPERTS, dtype=jnp.int32)
    dest = jnp.sum(jnp.where(is_exp, starts, 0), axis=-1) + ridx[:, TOP_K:2 * TOP_K]
    dest_t = dest.T

    xs = _scatter_rows(h2.reshape(t, d), dest_t.reshape(TOP_K, t // SC_ROWS, SC_ROWS), t * TOP_K)
    ys = _moe_call(sched, xs, w_gate_up[l], b_gate_up[l][:, None, :], w_down[l],
                   b_down[l][:, None, :])
    yg = _gather_rows(ys, dest_t.reshape(-1))
    out = _combine_call(yg, x1.reshape(t, d), gates.reshape(t, LANES), mod_lat, final_g[None, :], n)
    return out.reshape(bsz, n, d)
```

```python
import functools
import math

import jax
import jax.numpy as jnp
from jax import lax
from jax.experimental import pallas as pl
from jax.experimental.pallas import tpu as pltpu
from jax.experimental.pallas import tpu_sc as plsc

F32 = jnp.float32
BF16 = jnp.bfloat16
HIGHEST = lax.Precision.HIGHEST

N_HEADS = 8
QK_NOPE = 64
QK_ROPE = 32
V_DIM = 64
Q_LORA = 256
KV_LORA = 128
GRID_W = 64
ROPE_BASE = 10000.0
POOL_WINDOWS = (2, 4, 8, 16)
POOL_CH = 128
N_EXPERTS = 32
TOP_K = 4
SWIGLU_LIMIT = 7.0
SWIGLU_ALPHA = 1.702
MOE_BLOCK = 256
N_MOD = 6
EPS = 1e-6

LANES = 128
POOL_HALO = 8
VMEM_LIMIT = 56 * 1024 * 1024

FRONT_TS = 512
ATTN_TQ = 512
ATTN_KC = 512
MIX_TS = 512
COMBINE_TT = 256
SC_ROWS = 64


def _dot(a, b, **kw):
    return jnp.dot(a, b, preferred_element_type=F32, **kw)


def _dot_nt(a, b):
    return lax.dot_general(a, b, (((1,), (1,)), ((), ())), preferred_element_type=F32)


def _rms(x):
    return x * lax.rsqrt(jnp.mean(x * x, axis=-1, keepdims=True) + EPS)


def _value_lanes(hd):
    return (0, V_DIM) if hd % 2 == 0 else (V_DIM, 2 * V_DIM)


def _ones_lane(hd):
    return V_DIM if hd % 2 == 0 else 0


def _mod_kernel(c_ref, w_ref, b_ref, o_ref):
    c = c_ref[...]
    a = c / (1.0 + jnp.exp(-c))
    o_ref[...] = _dot(a, w_ref[...], precision=HIGHEST) + b_ref[...]


def _mod_call(cc, w_mod, b_mod):
    d = w_mod.shape[0]
    return pl.pallas_call(
        _mod_kernel,
        grid=(N_MOD,),
        in_specs=[
            pl.BlockSpec((8, d), lambda j: (0, 0)),
            pl.BlockSpec((d, d), lambda j: (0, j)),
            pl.BlockSpec((1, d), lambda j: (0, j)),
        ],
        out_specs=pl.BlockSpec((8, d), lambda j: (0, j)),
        out_shape=jax.ShapeDtypeStruct((8, N_MOD * d), F32),
        compiler_params=pltpu.CompilerParams(
            dimension_semantics=("arbitrary",), vmem_limit_bytes=VMEM_LIMIT),
        name="mod",
    )(cc, w_mod, b_mod)


def _front_kernel(x_ref, mod_ref, g1_ref, win_ref, qg_ref, kvg_ref, wuq_ref, wukv_ref,
                  cs_ref, sn_ref, *out_refs, is_ctx, scale):
    x = x_ref[0]
    m = mod_ref[0]
    h = _rms(x) * g1_ref[...] * (1.0 + m[1:2]) + m[0:1]
    z = _dot(h.astype(BF16), win_ref[...])
    if is_ctx:
        k_ref, v_ref = out_refs
        zkv = z
    else:
        q_ref, k_ref, v_ref, u_ref = out_refs
        zkv = z[:, Q_LORA:Q_LORA + 3 * LANES]
        u_ref[0] = z[:, Q_LORA + 3 * LANES:]
    cs = cs_ref[...]
    sn = sn_ref[...]

    ckv = _rms(zkv[:, :KV_LORA]) * kvg_ref[...]
    kv = _dot(ckv.astype(BF16), wukv_ref[...])
    kr = zkv[:, LANES:2 * LANES] * cs + zkv[:, 2 * LANES:3 * LANES] * sn
    for hd in range(N_HEADS):
        k_ref[0, hd] = (kv[:, hd * LANES:(hd + 1) * LANES] + kr).astype(BF16)
    voff = N_HEADS * LANES
    lane = lax.broadcasted_iota(jnp.int32, (1, LANES), 1)
    for hd in range(N_HEADS):
        ones = jnp.where(lane == _ones_lane(hd), 1.0, 0.0)
        v_ref[0, hd] = (kv[:, voff + hd * LANES: voff + (hd + 1) * LANES] + ones).astype(BF16)

    if not is_ctx:
        cq = _rms(z[:, :Q_LORA]) * qg_ref[...]
        qq = _dot(cq.astype(BF16), wuq_ref[...])
        boff = N_HEADS * LANES
        for hd in range(N_HEADS):
            qa = qq[:, hd * LANES:(hd + 1) * LANES]
            qb = qq[:, boff + hd * LANES: boff + (hd + 1) * LANES]
            q_ref[0, hd] = ((qa * cs + qb * sn) * scale).astype(BF16)


def _front_call(xs, mod, mod_per_batch, g1, win, qg, kvg, wuq, wukv, cs, sn, *, is_ctx, ts):
    bsz, n, d = xs.shape
    nt = n // ts
    scale = math.log2(math.e) / math.sqrt(QK_NOPE + QK_ROPE)
    const = lambda b, i: (0, 0)
    mod_map = (lambda b, i: (b, 0, 0)) if mod_per_batch else (lambda b, i: (0, 0, 0))
    in_specs = [
        pl.BlockSpec((1, ts, d), lambda b, i: (b, i, 0)),
        pl.BlockSpec((1, N_MOD, d), mod_map),
        pl.BlockSpec((1, d), const),
        pl.BlockSpec(win.shape, const),
        pl.BlockSpec((1, Q_LORA), const),
        pl.BlockSpec((1, KV_LORA), const),
        pl.BlockSpec(wuq.shape, const),
        pl.BlockSpec(wukv.shape, const),
        pl.BlockSpec((ts, LANES), lambda b, i: (i, 0)),
        pl.BlockSpec((ts, LANES), lambda b, i: (i, 0)),
    ]
    k_spec = pl.BlockSpec((1, N_HEADS, ts, LANES), lambda b, i: (b, 0, i, 0))
    k_shape = jax.ShapeDtypeStruct((bsz, N_HEADS, n, LANES), BF16)
    if is_ctx:
        out_specs = [k_spec, k_spec]
        out_shape = [k_shape, k_shape]
    else:
        pool_w = win.shape[1] - Q_LORA - 3 * LANES
        out_specs = [k_spec, k_spec, k_spec, pl.BlockSpec((1, ts, pool_w), lambda b, i: (b, i, 0))]
        out_shape = [k_shape, k_shape, k_shape, jax.ShapeDtypeStruct((bsz, n, pool_w), F32)]
    return pl.pallas_call(
        functools.partial(_front_kernel, is_ctx=is_ctx, scale=scale),
        grid=(bsz, nt),
        in_specs=in_specs,
        out_specs=out_specs,
        out_shape=out_shape,
        compiler_params=pltpu.CompilerParams(
            dimension_semantics=("arbitrary", "arbitrary"), vmem_limit_bytes=VMEM_LIMIT),
        name="front_ctx" if is_ctx else "front",
    )(xs, mod, g1, win, qg, kvg, wuq, wukv, cs, sn)


def _attn_kernel(q_ref, k_ref, v_ref, kc_ref, vc_ref, o_ref, s_ref, mlc_ref, mxb_ref, oe_ref):
    n_ctx = kc_ref.shape[2]
    n_lat = k_ref.shape[2]
    chunks = [(None, 0, n_ctx)] + [(c * ATTN_KC, n_ctx + c * ATTN_KC, ATTN_KC)
                                   for c in range(n_lat // ATTN_KC)]

    def rows(lat_ref, ctx_ref, hd, ci):
        off, _, w = chunks[ci]
        return ctx_ref[0, hd] if off is None else lat_ref[0, hd, off:off + w, :]

    def score_chunk(hd, buf, ci):
        _, soff, w = chunks[ci]
        s = _dot_nt(q_ref[0, hd], rows(k_ref, kc_ref, hd, ci))
        s_ref[buf, :, soff:soff + w] = s
        ml = s[:, 0:LANES]
        for g in range(1, w // LANES):
            ml = jnp.maximum(ml, s[:, g * LANES:(g + 1) * LANES])
        mlc_ref[buf, ci] = ml

    def row_max(buf):
        ml = mlc_ref[buf, 0]
        for ci in range(1, len(chunks)):
            ml = jnp.maximum(ml, mlc_ref[buf, ci])
        mxb_ref[...] = jnp.broadcast_to(jnp.max(ml, axis=-1, keepdims=True), ml.shape)

    def weight_chunk(hd, buf, ci):
        _, soff, w = chunks[ci]
        mxb = mxb_ref[...]
        ps = [jnp.exp2(s_ref[buf, :, soff + g * LANES:soff + (g + 1) * LANES] - mxb)
              for g in range(w // LANES)]
        return _dot(jnp.concatenate(ps, axis=-1).astype(BF16), rows(v_ref, vc_ref, hd, ci))

    def stage(hw, bw, hs, bs, parity):
        row_max(bw)
        acc = None
        for ci in range(len(chunks)):
            if hs is not None:
                score_chunk(hs, bs, ci)
            part = weight_chunk(hw, bw, ci)
            acc = part if acc is None else acc + part
        one = _ones_lane(parity)
        return acc / acc[:, one:one + 1]

    def write_pair(j, o_odd):
        lane = lax.broadcasted_iota(jnp.int32, o_odd.shape, 1)
        o_ref[0, j] = jnp.where(lane < V_DIM, oe_ref[...], o_odd).astype(BF16)

    for ci in range(len(chunks)):
        score_chunk(0, 0, ci)

    def head_pair(j, carry):
        oe_ref[...] = stage(2 * j, 0, 2 * j + 1, 1, 0)
        write_pair(j, stage(2 * j + 1, 1, 2 * j + 2, 0, 1))
        return carry

    last = N_HEADS // 2 - 1
    lax.fori_loop(0, last, head_pair, 0)
    oe_ref[...] = stage(2 * last, 0, 2 * last + 1, 1, 0)
    write_pair(last, stage(2 * last + 1, 1, None, None, 1))


def _attn_call(q, k, v, kc, vc):
    bsz, _, n, _ = q.shape
    n_ctx = kc.shape[2]
    tq = ATTN_TQ
    n_chunks = 1 + n // ATTN_KC
    per_batch = lambda b, i: (b, 0, 0, 0)
    resident = dict(pipeline_mode=pl.Buffered(1))
    return pl.pallas_call(
        _attn_kernel,
        grid=(bsz, n // tq),
        in_specs=[
            pl.BlockSpec((1, N_HEADS, tq, LANES), lambda b, i: (b, 0, i, 0)),
            pl.BlockSpec((1, N_HEADS, n, LANES), per_batch, **resident),
            pl.BlockSpec((1, N_HEADS, n, LANES), per_batch, **resident),
            pl.BlockSpec((1, N_HEADS, n_ctx, LANES), per_batch, **resident),
            pl.BlockSpec((1, N_HEADS, n_ctx, LANES), per_batch, **resident),
        ],
        out_specs=pl.BlockSpec((1, N_HEADS // 2, tq, LANES), lambda b, i: (b, 0, i, 0)),
        out_shape=jax.ShapeDtypeStruct((bsz, N_HEADS // 2, n, LANES), BF16),
        scratch_shapes=[
            pltpu.VMEM((2, tq, n_ctx + n), F32),
            pltpu.VMEM((2, n_chunks, tq, LANES), F32),
            pltpu.VMEM((tq, LANES), F32),
            pltpu.VMEM((tq, LANES), F32),
        ],
        compiler_params=pltpu.CompilerParams(
            dimension_semantics=("arbitrary", "arbitrary"), vmem_limit_bytes=VMEM_LIMIT),
        name="attn",
    )(q, k, v, kc, vc)


def _mix_kernel(att_ref, u_ref, up_ref, un_ref, x_ref, mod_ref, wpool_ref, pscale_ref, wout_ref,
                g2_ref, rw_ref, rb_ref,
                x1_ref, h2_ref, gate_ref, ridx_ref, cnt_ref, ue_ref, base_ref, *, n_seq):
    b = pl.program_id(0)
    i = pl.program_id(1)
    n_tiles = pl.num_programs(1)
    ts = u_ref.shape[1]

    @pl.when(jnp.logical_and(b == 0, i == 0))
    def _():
        base_ref[...] = jnp.zeros_like(base_ref)

    u = u_ref[0]
    ue_ref[0:POOL_HALO] = jnp.where(i > 0, up_ref[0], 0.0)
    ue_ref[POOL_HALO:POOL_HALO + ts] = u
    ue_ref[POOL_HALO + ts:2 * POOL_HALO + ts] = jnp.where(i < n_tiles - 1, un_ref[0], 0.0)
    t = i * ts + lax.broadcasted_iota(jnp.int32, (ts, 1), 0)
    ys = []
    for g, w in enumerate(POOL_WINDOWS):
        half = w // 2
        lanes = slice(g * POOL_CH, (g + 1) * POOL_CH)
        ws = ue_ref[POOL_HALO - half:POOL_HALO - half + ts, lanes]
        for jj in range(-half + 1, half):
            ws = ws + ue_ref[POOL_HALO + jj:POOL_HALO + jj + ts, lanes]
        count = (jnp.minimum(t + half, n_seq) - jnp.maximum(t - half, 0)).astype(F32)
        mixed = (ws / count - u[:, lanes]).astype(BF16)
        ys.append(_dot(mixed, wpool_ref[g]))
    pool = jnp.concatenate(ys, axis=-1) * pscale_ref[...]

    cat = jnp.concatenate([att_ref[0, j] for j in range(N_HEADS // 2)] + [pool.astype(BF16)], axis=-1)
    m = mod_ref[0]
    x1 = x_ref[0] + m[2:3] * _dot(cat, wout_ref[...])
    x1_ref[0] = x1
    h2 = _rms(x1) * g2_ref[...] * (1.0 + m[4:5]) + m[3:4]
    h2_ref[0] = h2

    logits = _dot(h2, rw_ref[...], precision=HIGHEST) + rb_ref[...]
    lane = lax.broadcasted_iota(jnp.int32, logits.shape, 1).astype(F32)
    vals, idxs = [], []
    for _k in range(TOP_K):
        mv = jnp.max(logits, axis=-1, keepdims=True)
        ix = jnp.min(jnp.where(logits == mv, lane, float(LANES)), axis=-1, keepdims=True)
        vals.append(mv)
        idxs.append(ix)
        logits = jnp.where(lane == ix, -jnp.inf, logits)
    es = [jnp.exp(v - vals[0]) for v in vals]
    den = es[0] + es[1] + es[2] + es[3]

    onehot = jnp.zeros(lane.shape, F32)
    for ix in idxs:
        onehot = onehot + jnp.where(lane == ix, 1.0, 0.0)
    row = lax.broadcasted_iota(jnp.int32, (ts, ts), 0)
    col = lax.broadcasted_iota(jnp.int32, (ts, ts), 1)
    tri = jnp.where(col < row, 1.0, 0.0).astype(BF16)
    before = _dot(tri, onehot.astype(BF16)) + base_ref[0:1, :]
    base_new = base_ref[0:1, :] + jnp.sum(onehot, axis=0, keepdims=True)
    base_ref[...] = jnp.broadcast_to(base_new, base_ref.shape)
    cnt_ref[...] = jnp.broadcast_to(base_new, cnt_ref.shape)

    gate_out = jnp.zeros(lane.shape, F32)
    ridx_out = jnp.zeros(lane.shape, F32)
    for kk in range(TOP_K):
        rank = jnp.sum(jnp.where(lane == idxs[kk], before, 0.0), axis=-1, keepdims=True)
        gate_out = jnp.where(lane == float(kk), es[kk] / den, gate_out)
        ridx_out = jnp.where(lane == float(kk), idxs[kk], ridx_out)
        ridx_out = jnp.where(lane == float(TOP_K + kk), rank, ridx_out)
    gate_ref[0] = gate_out
    ridx_ref[0] = ridx_out.astype(jnp.int32)


def _mix_call(att, u, x, mod, wpool, pscale, wout, g2, rw, rb):
    bsz, n, d = x.shape
    ts = MIX_TS
    nt = n // ts
    pool_w = u.shape[2]
    hb = ts // POOL_HALO
    n_halo_blocks = n // POOL_HALO
    const2 = lambda b, i: (0, 0)
    tok = lambda b, i: (b, i, 0)
    return pl.pallas_call(
        functools.partial(_mix_kernel, n_seq=n),
        grid=(bsz, nt),
        in_specs=[
            pl.BlockSpec((1, N_HEADS // 2, ts, LANES), lambda b, i: (b, 0, i, 0)),
            pl.BlockSpec((1, ts, pool_w), tok),
            pl.BlockSpec((1, POOL_HALO, pool_w), lambda b, i: (b, jnp.maximum(i * hb - 1, 0), 0)),
            pl.BlockSpec((1, POOL_HALO, pool_w),
                         lambda b, i: (b, jnp.minimum((i + 1) * hb, n_halo_blocks - 1), 0)),
            pl.BlockSpec((1, ts, d), tok),
            pl.BlockSpec((1, N_MOD, d), lambda b, i: (b, 0, 0)),
            pl.BlockSpec(wpool.shape, lambda b, i: (0, 0, 0)),
            pl.BlockSpec((1, pool_w), const2),
            pl.BlockSpec(wout.shape, const2),
            pl.BlockSpec((1, d), const2),
            pl.BlockSpec(rw.shape, const2),
            pl.BlockSpec((1, LANES), const2),
        ],
        out_specs=[
            pl.BlockSpec((1, ts, d), tok),
            pl.BlockSpec((1, ts, d), tok),
            pl.BlockSpec((1, ts, LANES), tok),
            pl.BlockSpec((1, ts, LANES), tok),
            pl.BlockSpec((8, LANES), const2),
        ],
        out_shape=[
            jax.ShapeDtypeStruct((bsz, n, d), F32),
            jax.ShapeDtypeStruct((bsz, n, d), F32),
            jax.ShapeDtypeStruct((bsz, n, LANES), F32),
            jax.ShapeDtypeStruct((bsz, n, LANES), jnp.int32),
            jax.ShapeDtypeStruct((8, LANES), F32),
        ],
        scratch_shapes=[
            pltpu.VMEM((ts + 2 * POOL_HALO, pool_w), F32),
            pltpu.VMEM((8, LANES), F32),
        ],
        compiler_params=pltpu.CompilerParams(
            dimension_semantics=("arbitrary", "arbitrary"), vmem_limit_bytes=VMEM_LIMIT),
        name="mix",
    )(att, u, u, u, x, mod, wpool, pscale, wout, g2, rw, rb)


def _sc_mesh():
    return plsc.VectorSubcoreMesh(core_axis_name="core", subcore_axis_name="subcore")


def _sc_worker_id():
    info = plsc.get_sparse_core_info()
    return lax.axis_index("subcore") * info.num_cores + lax.axis_index("core")


def _sc_num_workers():
    info = plsc.get_sparse_core_info()
    return info.num_cores * info.num_subcores


def _scatter_rows(x, dest3, n_out):
    t, d = x.shape
    top_k, n_chunks, rows = dest3.shape
    per_worker = n_chunks // _sc_num_workers()

    @functools.partial(
        pl.kernel,
        out_type=jax.ShapeDtypeStruct((n_out, d), x.dtype),
        mesh=_sc_mesh(),
        scratch_types=[pltpu.VMEM((top_k, rows), jnp.int32), pltpu.VMEM((rows, d), x.dtype)],
        name="sc_scatter",
    )
    def scatter(x_hbm, i_hbm, o_hbm, idx_v, rows_v):
        wid = _sc_worker_id()

        @pl.loop(0, per_worker)
        def _(c):
            chunk = wid * per_worker + c
            for kk in range(top_k):
                pltpu.sync_copy(i_hbm.at[kk, chunk], idx_v.at[kk])
            pltpu.sync_copy(x_hbm.at[pl.ds(pl.multiple_of(chunk * rows, 8), rows)], rows_v)
            for kk in range(top_k):
                pltpu.sync_copy(rows_v, o_hbm.at[idx_v.at[kk]])

    return scatter(x, dest3)


def _gather_rows(y, idx):
    n = idx.shape[0]
    d = y.shape[1]
    rows = SC_ROWS
    per_worker = n // _sc_num_workers()
    n_chunks = per_worker // rows

    @functools.partial(
        pl.kernel,
        out_type=jax.ShapeDtypeStruct((n, d), y.dtype),
        mesh=_sc_mesh(),
        scratch_types=[pltpu.VMEM((rows,), jnp.int32), pltpu.VMEM((rows, d), y.dtype)],
        name="sc_gather",
    )
    def gather(y_hbm, i_hbm, o_hbm, idx_v, rows_v):
        base = _sc_worker_id() * per_worker

        @pl.loop(0, n_chunks)
        def _(c):
            off = pl.multiple_of(base + c * rows, 8)
            pltpu.sync_copy(i_hbm.at[pl.ds(off, rows)], idx_v)
            pltpu.sync_copy(y_hbm.at[idx_v], rows_v)
            pltpu.sync_copy(rows_v, o_hbm.at[pl.ds(off, rows)])

    return gather(y, idx)


def _moe_kernel(blk_ref, exp_ref, lo_ref, hi_ref, cast_ref, init_ref, xs_ref, wgu_ref, bgu_ref,
                wd_ref, bd_ref, ys_ref, wgu_bf, wd_bf):
    i = pl.program_id(0)

    @pl.when(cast_ref[i] == 1)
    def _():
        wgu_bf[...] = wgu_ref[0].astype(BF16)
        wd_bf[...] = wd_ref[0].astype(BF16)

    @pl.when(init_ref[i] == 1)
    def _():
        ys_ref[...] = jnp.zeros_like(ys_ref)

    @pl.when(hi_ref[i] > lo_ref[i])
    def _():
        de = wd_bf.shape[0]
        gu = _dot(xs_ref[...].astype(BF16), wgu_bf[...]) + bgu_ref[0]
        g = jnp.minimum(gu[:, :de], SWIGLU_LIMIT)
        lin = jnp.clip(gu[:, de:], -SWIGLU_LIMIT, SWIGLU_LIMIT)
        act = g / (1.0 + jnp.exp(-SWIGLU_ALPHA * g)) * (lin + 1.0)
        y = _dot(act.astype(BF16), wd_bf[...]) + bd_ref[0]
        row = lax.broadcasted_iota(jnp.int32, (y.shape[0], 1), 0)
        mine = jnp.logical_and(row >= lo_ref[i], row < hi_ref[i])
        ys_ref[...] = jnp.where(mine, y, ys_ref[...])


def _moe_call(sched, xs, wgu, bgu, wd, bd):
    n_items = sched[0].shape[0]
    _, d, de2 = wgu.shape
    de = de2 // 2
    wmap = lambda i, blk, exp, lo, hi, cast, init: (exp[i], 0, 0)
    xmap = lambda i, blk, exp, lo, hi, cast, init: (blk[i], 0)
    return pl.pallas_call(
        _moe_kernel,
        grid_spec=pltpu.PrefetchScalarGridSpec(
            num_scalar_prefetch=6,
            grid=(n_items,),
            in_specs=[
                pl.BlockSpec((MOE_BLOCK, d), xmap),
                pl.BlockSpec((1, d, de2), wmap),
                pl.BlockSpec((1, 1, de2), wmap),
                pl.BlockSpec((1, de, d), wmap),
                pl.BlockSpec((1, 1, d), wmap),
            ],
            out_specs=pl.BlockSpec((MOE_BLOCK, d), xmap),
            scratch_shapes=[pltpu.VMEM((d, de2), BF16), pltpu.VMEM((de, d), BF16)],
        ),
        out_shape=jax.ShapeDtypeStruct(xs.shape, F32),
        compiler_params=pltpu.CompilerParams(
            dimension_semantics=("arbitrary",), vmem_limit_bytes=VMEM_LIMIT),
        name="moe",
    )(*sched, xs, wgu, bgu, wd, bd)


def _moe_schedule(counts, n_rows):
    n_blocks = n_rows // MOE_BLOCK
    n_items = n_blocks + N_EXPERTS
    ends = jnp.cumsum(counts)
    starts = ends - counts
    first_blk = starts // MOE_BLOCK
    last_blk = (ends - 1) // MOE_BLOCK
    items_per = jnp.where(counts > 0, last_blk - first_blk + 1, 0)
    item_ends = jnp.cumsum(items_per)
    item_starts = item_ends - items_per
    total = item_ends[-1]
    it = jnp.arange(n_items, dtype=jnp.int32)
    live = it < total
    itc = jnp.minimum(it, total - 1)
    exp = jnp.sum((item_ends[None, :] <= itc[:, None]).astype(jnp.int32), axis=1)
    is_exp = exp[:, None] == jnp.arange(N_EXPERTS, dtype=jnp.int32)[None, :]
    pick = lambda table: jnp.sum(jnp.where(is_exp, table[None, :], 0), axis=1)
    blk = pick(first_blk) + itc - pick(item_starts)
    lo = jnp.clip(pick(starts) - blk * MOE_BLOCK, 0, MOE_BLOCK)
    hi = jnp.clip(pick(ends) - blk * MOE_BLOCK, 0, MOE_BLOCK)
    hi = jnp.where(live, hi, lo)
    prev_exp = jnp.concatenate([jnp.full((1,), -1, jnp.int32), exp[:-1]])
    prev_blk = jnp.concatenate([jnp.full((1,), -1, jnp.int32), blk[:-1]])
    cast = jnp.logical_and(live, exp != prev_exp)
    init = jnp.logical_and(live, blk != prev_blk)
    as_i32 = lambda a: a.astype(jnp.int32)
    return tuple(as_i32(a) for a in (blk, exp, lo, hi, cast, init)), starts


def _combine_kernel(*refs):
    yg_refs = refs[:TOP_K]
    x1_ref, gate_ref, mod_ref, fg_ref, o_ref = refs[TOP_K:]
    gates = gate_ref[...]
    y = gates[:, 0:1] * yg_refs[0][...]
    for kk in range(1, TOP_K):
        y = y + gates[:, kk:kk + 1] * yg_refs[kk][...]
    m = mod_ref[0]
    x2 = x1_ref[...] + m[5:6] * y
    o_ref[...] = _rms(x2) * fg_ref[...]


def _combine_call(yg, x1, gates, mod, fg, n_seq):
    t, d = x1.shape
    tt = COMBINE_TT
    tiles_per_seq = n_seq // tt
    tok = lambda i: (i, 0)
    n_tiles = t // tt
    slot_specs = [pl.BlockSpec((tt, d), functools.partial(lambda kk, i: (kk * n_tiles + i, 0), kk))
                  for kk in range(TOP_K)]
    return pl.pallas_call(
        _combine_kernel,
        grid=(n_tiles,),
        in_specs=slot_specs + [
            pl.BlockSpec((tt, d), tok),
            pl.BlockSpec((tt, LANES), tok),
            pl.BlockSpec((1, N_MOD, d), lambda i: (i // tiles_per_seq, 0, 0)),
            pl.BlockSpec((1, d), lambda i: (0, 0)),
        ],
        out_specs=pl.BlockSpec((tt, d), tok),
        out_shape=jax.ShapeDtypeStruct((t, d), F32),
        compiler_params=pltpu.CompilerParams(
            dimension_semantics=("arbitrary",), vmem_limit_bytes=VMEM_LIMIT),
        name="combine",
    )(*([yg] * TOP_K), x1, gates, mod, fg)


def _pad_cols(a, width):
    return jnp.pad(a, ((0, 0), (0, width - a.shape[1])))


def _rope_slab(w_rope, swapped):
    half = QK_ROPE // 2
    if swapped:
        w_rope = jnp.concatenate([w_rope[:, half:], w_rope[:, :half]], axis=1)
    zeros = jnp.zeros((w_rope.shape[0], QK_NOPE), w_rope.dtype)
    return _pad_cols(jnp.concatenate([zeros, w_rope], axis=1), LANES)


def _prep_w_in(w_in):
    kr0 = Q_LORA + KV_LORA
    w_kr = w_in[:, kr0:kr0 + QK_ROPE]
    return jnp.concatenate(
        [w_in[:, :kr0], _rope_slab(w_kr, False), _rope_slab(w_kr, True), w_in[:, kr0 + QK_ROPE:]],
        axis=1).astype(BF16)


def _prep_w_uq(w_uq):
    per = QK_NOPE + QK_ROPE
    slabs_a, slabs_b = [], []
    for hd in range(N_HEADS):
        w = w_uq[:, hd * per:(hd + 1) * per]
        slabs_a.append(_pad_cols(w, LANES))
        slabs_b.append(_rope_slab(w[:, QK_NOPE:], True))
    return jnp.concatenate(slabs_a + slabs_b, axis=1).astype(BF16)


def _prep_w_ukv(w_ukv):
    per = QK_NOPE + V_DIM
    ks, vs = [], []
    for hd in range(N_HEADS):
        w = w_ukv[:, hd * per:(hd + 1) * per]
        ks.append(_pad_cols(w[:, :QK_NOPE], LANES))
        lo, _ = _value_lanes(hd)
        vs.append(jnp.pad(w[:, QK_NOPE:], ((0, 0), (lo, LANES - V_DIM - lo))))
    return jnp.concatenate(ks + vs, axis=1).astype(BF16)


def _rope_tables(n_lat):
    rows = n_lat // GRID_W
    nf = QK_ROPE // 4
    row = jnp.repeat(jnp.arange(rows, dtype=F32), GRID_W)
    col = jnp.tile(jnp.arange(GRID_W, dtype=F32), rows)
    freqs = ROPE_BASE ** (-jnp.arange(nf, dtype=F32) / nf)
    ang = jnp.concatenate([row[:, None] * freqs, col[:, None] * freqs], axis=-1)
    cos, sin = jnp.cos(ang), jnp.sin(ang)
    ones = jnp.ones((n_lat, QK_NOPE), F32)
    zeros = jnp.zeros((n_lat, QK_NOPE), F32)
    cs = _pad_cols(jnp.concatenate([ones, cos, cos], axis=1), LANES)
    sn = _pad_cols(jnp.concatenate([zeros, -sin, sin], axis=1), LANES)
    return cs, sn


def kernel(x, c, ctx, c_ctx, w_mod, b_mod, norm1_g, w_in, q_norm_g, kv_norm_g, w_uq, w_ukv, w_pool,
           pool_scale, w_out, norm2_g, router_w, router_b, w_gate_up, b_gate_up, w_down, b_down,
           final_g):
    bsz, n, d = x.shape
    n_ctx = ctx.shape[1]
    t = bsz * n
    l = 0

    cc = jnp.concatenate([c, c_ctx[None, :], jnp.zeros((8 - bsz - 1, d), F32)], axis=0)
    mod = _mod_call(cc, w_mod[l], b_mod[l][None, :]).reshape(8, N_MOD, d)
    mod_lat, mod_ctx = mod[:bsz], mod[bsz:bsz + 1]

    win = _prep_w_in(w_in[l])
    wuq = _prep_w_uq(w_uq[l])
    wukv = _prep_w_ukv(w_ukv[l])
    cs, sn = _rope_tables(n)
    cs_ctx = jnp.broadcast_to((jnp.arange(LANES) < QK_NOPE + QK_ROPE).astype(F32), (n_ctx, LANES))
    sn_ctx = jnp.zeros((n_ctx, LANES), F32)
    g1 = norm1_g[l][None, :]
    qg = q_norm_g[l][None, :]
    kvg = kv_norm_g[l][None, :]

    q, k, v, u = _front_call(x, mod_lat, True, g1, win, qg, kvg, wuq, wukv, cs, sn,
                             is_ctx=False, ts=FRONT_TS)
    kc, vc = _front_call(ctx, mod_ctx, False, g1, win[:, Q_LORA:Q_LORA + 3 * LANES], qg, kvg, wuq,
                         wukv, cs_ctx, sn_ctx, is_ctx=True, ts=n_ctx)
    att = _attn_call(q, k, v, kc, vc)

    rw = _pad_cols(router_w[l], LANES)
    rb = jnp.concatenate([router_b[l], jnp.full((LANES - N_EXPERTS,), -jnp.inf, F32)])[None, :]
    x1, h2, gates, ridx, cnt = _mix_call(
        att, u, x, mod_lat, w_pool[l].astype(BF16), pool_scale[l][None, :], w_out[l].astype(BF16),
        norm2_g[l][None, :], rw, rb)

    counts = cnt[0, :N_EXPERTS].astype(jnp.int32)
    sched, starts = _moe_schedule(counts, t * TOP_K)
    ridx = ridx.reshape(t, LANES)
    is_exp = ridx[:, :TOP_K, None] == jnp.arange(N_EXPERTS, dtype=jnp.int32)
    dest = jnp.sum(jnp.where(is_exp, starts, 0), axis=-1) + ridx[:, TOP_K:2 * TOP_K]
    dest_t = dest.T

    xs = _scatter_rows(h2.reshape(t, d), dest_t.reshape(TOP_K, t // SC_ROWS, SC_ROWS), t * TOP_K)
    ys = _moe_call(sched, xs, w_gate_up[l], b_gate_up[l][:, None, :], w_down[l],
                   b_down[l][:, None, :])
    yg = _gather_rows(ys, dest_t.reshape(-1))
    out = _combine_call(yg, x1.reshape(t, d), gates.reshape(t, LANES), mod_lat, final_g[None, :], n)
    return out.reshape(bsz, n, d)
```

```python
import functools
import math

import jax
import jax.numpy as jnp
from jax import lax
from jax.experimental import pallas as pl
from jax.experimental.pallas import tpu as pltpu
from jax.experimental.pallas import tpu_sc as plsc

F32 = jnp.float32
BF16 = jnp.bfloat16
HIGHEST = lax.Precision.HIGHEST

N_HEADS = 8
QK_NOPE = 64
QK_ROPE = 32
V_DIM = 64
Q_LORA = 256
KV_LORA = 128
GRID_W = 64
ROPE_BASE = 10000.0
POOL_WINDOWS = (2, 4, 8, 16)
POOL_CH = 128
N_EXPERTS = 32
TOP_K = 4
SWIGLU_LIMIT = 7.0
SWIGLU_ALPHA = 1.702
MOE_BLOCK = 256
N_MOD = 6
EPS = 1e-6

LANES = 128
POOL_HALO = 8
VMEM_LIMIT = 56 * 1024 * 1024

FRONT_TS = 512
ATTN_TQ = 512
ATTN_KC = 512
MIX_TS = 512
COMBINE_TT = 256
SC_ROWS = 64


def _dot(a, b, **kw):
    return jnp.dot(a, b, preferred_element_type=F32, **kw)


def _dot_nt(a, b):
    return lax.dot_general(a, b, (((1,), (1,)), ((), ())), preferred_element_type=F32)


def _rms(x):
    return x * lax.rsqrt(jnp.mean(x * x, axis=-1, keepdims=True) + EPS)


def _value_lanes(hd):
    return (0, V_DIM) if hd % 2 == 0 else (V_DIM, 2 * V_DIM)


def _ones_lane(hd):
    return V_DIM if hd % 2 == 0 else 0


def _mod_kernel(c_ref, w_ref, b_ref, o_ref):
    c = c_ref[...]
    a = c / (1.0 + jnp.exp(-c))
    o_ref[...] = _dot(a, w_ref[...], precision=HIGHEST) + b_ref[...]


def _mod_call(cc, w_mod, b_mod):
    d = w_mod.shape[0]
    return pl.pallas_call(
        _mod_kernel,
        grid=(N_MOD,),
        in_specs=[
            pl.BlockSpec((8, d), lambda j: (0, 0)),
            pl.BlockSpec((d, d), lambda j: (0, j)),
            pl.BlockSpec((1, d), lambda j: (0, j)),
        ],
        out_specs=pl.BlockSpec((8, d), lambda j: (0, j)),
        out_shape=jax.ShapeDtypeStruct((8, N_MOD * d), F32),
        compiler_params=pltpu.CompilerParams(
            dimension_semantics=("arbitrary",), vmem_limit_bytes=VMEM_LIMIT),
        name="mod",
    )(cc, w_mod, b_mod)


def _front_kernel(x_ref, mod_ref, g1_ref, win_ref, qg_ref, kvg_ref, wuq_ref, wukv_ref,
                  cs_ref, sn_ref, *out_refs, is_ctx, scale):
    x = x_ref[0]
    m = mod_ref[0]
    h = _rms(x) * g1_ref[...] * (1.0 + m[1:2]) + m[0:1]
    z = _dot(h.astype(BF16), win_ref[...])
    if is_ctx:
        k_ref, v_ref = out_refs
        zkv = z
    else:
        q_ref, k_ref, v_ref, u_ref = out_refs
        zkv = z[:, Q_LORA:Q_LORA + 3 * LANES]
        u_ref[0] = z[:, Q_LORA + 3 * LANES:]
    cs = cs_ref[...]
    sn = sn_ref[...]

    ckv = _rms(zkv[:, :KV_LORA]) * kvg_ref[...]
    kv = _dot(ckv.astype(BF16), wukv_ref[...])
    kr = zkv[:, LANES:2 * LANES] * cs + zkv[:, 2 * LANES:3 * LANES] * sn
    for hd in range(N_HEADS):
        k_ref[0, hd] = (kv[:, hd * LANES:(hd + 1) * LANES] + kr).astype(BF16)
    voff = N_HEADS * LANES
    lane = lax.broadcasted_iota(jnp.int32, (1, LANES), 1)
    for hd in range(N_HEADS):
        ones = jnp.where(lane == _ones_lane(hd), 1.0, 0.0)
        v_ref[0, hd] = (kv[:, voff + hd * LANES: voff + (hd + 1) * LANES] + ones).astype(BF16)

    if not is_ctx:
        cq = _rms(z[:, :Q_LORA]) * qg_ref[...]
        qq = _dot(cq.astype(BF16), wuq_ref[...])
        boff = N_HEADS * LANES
        for hd in range(N_HEADS):
            qa = qq[:, hd * LANES:(hd + 1) * LANES]
            qb = qq[:, boff + hd * LANES: boff + (hd + 1) * LANES]
            q_ref[0, hd] = ((qa * cs + qb * sn) * scale).astype(BF16)


def _front_call(xs, mod, mod_per_batch, g1, win, qg, kvg, wuq, wukv, cs, sn, *, is_ctx, ts):
    bsz, n, d = xs.shape
    nt = n // ts
    scale = math.log2(math.e) / math.sqrt(QK_NOPE + QK_ROPE)
    const = lambda b, i: (0, 0)
    mod_map = (lambda b, i: (b, 0, 0)) if mod_per_batch else (lambda b, i: (0, 0, 0))
    in_specs = [
        pl.BlockSpec((1, ts, d), lambda b, i: (b, i, 0)),
        pl.BlockSpec((1, N_MOD, d), mod_map),
        pl.BlockSpec((1, d), const),
        pl.BlockSpec(win.shape, const),
        pl.BlockSpec((1, Q_LORA), const),
        pl.BlockSpec((1, KV_LORA), const),
        pl.BlockSpec(wuq.shape, const),
        pl.BlockSpec(wukv.shape, const),
        pl.BlockSpec((ts, LANES), lambda b, i: (i, 0)),
        pl.BlockSpec((ts, LANES), lambda b, i: (i, 0)),
    ]
    k_spec = pl.BlockSpec((1, N_HEADS, ts, LANES), lambda b, i: (b, 0, i, 0))
    k_shape = jax.ShapeDtypeStruct((bsz, N_HEADS, n, LANES), BF16)
    if is_ctx:
        out_specs = [k_spec, k_spec]
        out_shape = [k_shape, k_shape]
    else:
        pool_w = win.shape[1] - Q_LORA - 3 * LANES
        out_specs = [k_spec, k_spec, k_spec, pl.BlockSpec((1, ts, pool_w), lambda b, i: (b, i, 0))]
        out_shape = [k_shape, k_shape, k_shape, jax.ShapeDtypeStruct((bsz, n, pool_w), F32)]
    return pl.pallas_call(
        functools.partial(_front_kernel, is_ctx=is_ctx, scale=scale),
        grid=(bsz, nt),
        in_specs=in_specs,
        out_specs=out_specs,
        out_shape=out_shape,
        compiler_params=pltpu.CompilerParams(
            dimension_semantics=("arbitrary", "arbitrary"), vmem_limit_bytes=VMEM_LIMIT),
        name="front_ctx" if is_ctx else "front",
    )(xs, mod, g1, win, qg, kvg, wuq, wukv, cs, sn)


def _attn_kernel(q_ref, k_ref, v_ref, kc_ref, vc_ref, o_ref, s_ref, mlc_ref, mxb_ref, oe_ref):
    n_ctx = kc_ref.shape[2]
    n_lat = k_ref.shape[2]
    chunks = [(None, 0, n_ctx)] + [(c * ATTN_KC, n_ctx + c * ATTN_KC, ATTN_KC)
                                   for c in range(n_lat // ATTN_KC)]

    def rows(lat_ref, ctx_ref, hd, ci):
        off, _, w = chunks[ci]
        return ctx_ref[0, hd] if off is None else lat_ref[0, hd, off:off + w, :]

    def score_chunk(hd, buf, ci):
        _, soff, w = chunks[ci]
        s = _dot_nt(q_ref[0, hd], rows(k_ref, kc_ref, hd, ci))
        s_ref[buf, :, soff:soff + w] = s
        ml = s[:, 0:LANES]
        for g in range(1, w // LANES):
            ml = jnp.maximum(ml, s[:, g * LANES:(g + 1) * LANES])
        mlc_ref[buf, ci] = ml

    def row_max(buf):
        ml = mlc_ref[buf, 0]
        for ci in range(1, len(chunks)):
            ml = jnp.maximum(ml, mlc_ref[buf, ci])
        mxb_ref[...] = jnp.broadcast_to(jnp.max(ml, axis=-1, keepdims=True), ml.shape)

    def weight_chunk(hd, buf, ci):
        _, soff, w = chunks[ci]
        mxb = mxb_ref[...]
        ps = [jnp.exp2(s_ref[buf, :, soff + g * LANES:soff + (g + 1) * LANES] - mxb)
              for g in range(w // LANES)]
        return _dot(jnp.concatenate(ps, axis=-1).astype(BF16), rows(v_ref, vc_ref, hd, ci))

    def stage(hw, bw, hs, bs, parity):
        row_max(bw)
        acc = None
        for ci in range(len(chunks)):
            if hs is not None:
                score_chunk(hs, bs, ci)
            part = weight_chunk(hw, bw, ci)
            acc = part if acc is None else acc + part
        one = _ones_lane(parity)
        return acc / acc[:, one:one + 1]

    def write_pair(j, o_odd):
        lane = lax.broadcasted_iota(jnp.int32, o_odd.shape, 1)
        o_ref[0, j] = jnp.where(lane < V_DIM, oe_ref[...], o_odd).astype(BF16)

    for ci in range(len(chunks)):
        score_chunk(0, 0, ci)

    def head_pair(j, carry):
        oe_ref[...] = stage(2 * j, 0, 2 * j + 1, 1, 0)
        write_pair(j, stage(2 * j + 1, 1, 2 * j + 2, 0, 1))
        return carry

    last = N_HEADS // 2 - 1
    lax.fori_loop(0, last, head_pair, 0)
    oe_ref[...] = stage(2 * last, 0, 2 * last + 1, 1, 0)
    write_pair(last, stage(2 * last + 1, 1, None, None, 1))


def _attn_call(q, k, v, kc, vc):
    bsz, _, n, _ = q.shape
    n_ctx = kc.shape[2]
    tq = ATTN_TQ
    n_chunks = 1 + n // ATTN_KC
    per_batch = lambda b, i: (b, 0, 0, 0)
    resident = dict(pipeline_mode=pl.Buffered(1))
    return pl.pallas_call(
        _attn_kernel,
        grid=(bsz, n // tq),
        in_specs=[
            pl.BlockSpec((1, N_HEADS, tq, LANES), lambda b, i: (b, 0, i, 0)),
            pl.BlockSpec((1, N_HEADS, n, LANES), per_batch, **resident),
            pl.BlockSpec((1, N_HEADS, n, LANES), per_batch, **resident),
            pl.BlockSpec((1, N_HEADS, n_ctx, LANES), per_batch, **resident),
            pl.BlockSpec((1, N_HEADS, n_ctx, LANES), per_batch, **resident),
        ],
        out_specs=pl.BlockSpec((1, N_HEADS // 2, tq, LANES), lambda b, i: (b, 0, i, 0)),
        out_shape=jax.ShapeDtypeStruct((bsz, N_HEADS // 2, n, LANES), BF16),
        scratch_shapes=[
            pltpu.VMEM((2, tq, n_ctx + n), F32),
            pltpu.VMEM((2, n_chunks, tq, LANES), F32),
            pltpu.VMEM((tq, LANES), F32),
            pltpu.VMEM((tq, LANES), F32),
        ],
        compiler_params=pltpu.CompilerParams(
            dimension_semantics=("arbitrary", "arbitrary"), vmem_limit_bytes=VMEM_LIMIT),
        name="attn",
    )(q, k, v, kc, vc)


def _mix_kernel(att_ref, u_ref, up_ref, un_ref, x_ref, mod_ref, wpool_ref, pscale_ref, wout_ref,
                g2_ref, rw_ref, rb_ref,
                x1_ref, h2_ref, gate_ref, ridx_ref, cnt_ref, ue_ref, base_ref, *, n_seq):
    b = pl.program_id(0)
    i = pl.program_id(1)
    n_tiles = pl.num_programs(1)
    ts = u_ref.shape[1]

    @pl.when(jnp.logical_and(b == 0, i == 0))
    def _():
        base_ref[...] = jnp.zeros_like(base_ref)

    u = u_ref[0]
    ue_ref[0:POOL_HALO] = jnp.where(i > 0, up_ref[0], 0.0)
    ue_ref[POOL_HALO:POOL_HALO + ts] = u
    ue_ref[POOL_HALO + ts:2 * POOL_HALO + ts] = jnp.where(i < n_tiles - 1, un_ref[0], 0.0)
    t = i * ts + lax.broadcasted_iota(jnp.int32, (ts, 1), 0)
    ys = []
    for g, w in enumerate(POOL_WINDOWS):
        half = w // 2
        lanes = slice(g * POOL_CH, (g + 1) * POOL_CH)
        ws = ue_ref[POOL_HALO - half:POOL_HALO - half + ts, lanes]
        for jj in range(-half + 1, half):
            ws = ws + ue_ref[POOL_HALO + jj:POOL_HALO + jj + ts, lanes]
        count = (jnp.minimum(t + half, n_seq) - jnp.maximum(t - half, 0)).astype(F32)
        mixed = (ws / count - u[:, lanes]).astype(BF16)
        ys.append(_dot(mixed, wpool_ref[g]))
    pool = jnp.concatenate(ys, axis=-1) * pscale_ref[...]

    cat = jnp.concatenate([att_ref[0, j] for j in range(N_HEADS // 2)] + [pool.astype(BF16)], axis=-1)
    m = mod_ref[0]
    x1 = x_ref[0] + m[2:3] * _dot(cat, wout_ref[...])
    x1_ref[0] = x1
    h2 = _rms(x1) * g2_ref[...] * (1.0 + m[4:5]) + m[3:4]
    h2_ref[0] = h2

    logits = _dot(h2, rw_ref[...], precision=HIGHEST) + rb_ref[...]
    lane = lax.broadcasted_iota(jnp.int32, logits.shape, 1).astype(F32)
    vals, idxs = [], []
    for _k in range(TOP_K):
        mv = jnp.max(logits, axis=-1, keepdims=True)
        ix = jnp.min(jnp.where(logits == mv, lane, float(LANES)), axis=-1, keepdims=True)
        vals.append(mv)
        idxs.append(ix)
        logits = jnp.where(lane == ix, -jnp.inf, logits)
    es = [jnp.exp(v - vals[0]) for v in vals]
    den = es[0] + es[1] + es[2] + es[3]

    onehot = jnp.zeros(lane.shape, F32)
    for ix in idxs:
        onehot = onehot + jnp.where(lane == ix, 1.0, 0.0)
    row = lax.broadcasted_iota(jnp.int32, (ts, ts), 0)
    col = lax.broadcasted_iota(jnp.int32, (ts, ts), 1)
    tri = jnp.where(col < row, 1.0, 0.0).astype(BF16)
    before = _dot(tri, onehot.astype(BF16)) + base_ref[0:1, :]
    base_new = base_ref[0:1, :] + jnp.sum(onehot, axis=0, keepdims=True)
    base_ref[...] = jnp.broadcast_to(base_new, base_ref.shape)
    cnt_ref[...] = jnp.broadcast_to(base_new, cnt_ref.shape)

    gate_out = jnp.zeros(lane.shape, F32)
    ridx_out = jnp.zeros(lane.shape, F32)
    for kk in range(TOP_K):
        rank = jnp.sum(jnp.where(lane == idxs[kk], before, 0.0), axis=-1, keepdims=True)
        gate_out = jnp.where(lane == float(kk), es[kk] / den, gate_out)
        ridx_out = jnp.where(lane == float(kk), idxs[kk], ridx_out)
        ridx_out = jnp.where(lane == float(TOP_K + kk), rank, ridx_out)
    gate_ref[0] = gate_out
    ridx_ref[0] = ridx_out.astype(jnp.int32)


def _mix_call(att, u, x, mod, wpool, pscale, wout, g2, rw, rb):
    bsz, n, d = x.shape
    ts = MIX_TS
    nt = n // ts
    pool_w = u.shape[2]
    hb = ts // POOL_HALO
    n_halo_blocks = n // POOL_HALO
    const2 = lambda b, i: (0, 0)
    tok = lambda b, i: (b, i, 0)
    return pl.pallas_call(
        functools.partial(_mix_kernel, n_seq=n),
        grid=(bsz, nt),
        in_specs=[
            pl.BlockSpec((1, N_HEADS // 2, ts, LANES), lambda b, i: (b, 0, i, 0)),
            pl.BlockSpec((1, ts, pool_w), tok),
            pl.BlockSpec((1, POOL_HALO, pool_w), lambda b, i: (b, jnp.maximum(i * hb - 1, 0), 0)),
            pl.BlockSpec((1, POOL_HALO, pool_w),
                         lambda b, i: (b, jnp.minimum((i + 1) * hb, n_halo_blocks - 1), 0)),
            pl.BlockSpec((1, ts, d), tok),
            pl.BlockSpec((1, N_MOD, d), lambda b, i: (b, 0, 0)),
            pl.BlockSpec(wpool.shape, lambda b, i: (0, 0, 0)),
            pl.BlockSpec((1, pool_w), const2),
            pl.BlockSpec(wout.shape, const2),
            pl.BlockSpec((1, d), const2),
            pl.BlockSpec(rw.shape, const2),
            pl.BlockSpec((1, LANES), const2),
        ],
        out_specs=[
            pl.BlockSpec((1, ts, d), tok),
            pl.BlockSpec((1, ts, d), tok),
            pl.BlockSpec((1, ts, LANES), tok),
            pl.BlockSpec((1, ts, LANES), tok),
            pl.BlockSpec((8, LANES), const2),
        ],
        out_shape=[
            jax.ShapeDtypeStruct((bsz, n, d), F32),
            jax.ShapeDtypeStruct((bsz, n, d), F32),
            jax.ShapeDtypeStruct((bsz, n, LANES), F32),
            jax.ShapeDtypeStruct((bsz, n, LANES), jnp.int32),
            jax.ShapeDtypeStruct((8, LANES), F32),
        ],
        scratch_shapes=[
            pltpu.VMEM((ts + 2 * POOL_HALO, pool_w), F32),
            pltpu.VMEM((8, LANES), F32),
        ],
        compiler_params=pltpu.CompilerParams(
            dimension_semantics=("arbitrary", "arbitrary"), vmem_limit_bytes=VMEM_LIMIT),
        name="mix",
    )(att, u, u, u, x, mod, wpool, pscale, wout, g2, rw, rb)


def _sc_mesh():
    return plsc.VectorSubcoreMesh(core_axis_name="core", subcore_axis_name="subcore")


def _sc_worker_id():
    info = plsc.get_sparse_core_info()
    return lax.axis_index("subcore") * info.num_cores + lax.axis_index("core")


def _sc_num_workers():
    info = plsc.get_sparse_core_info()
    return info.num_cores * info.num_subcores


def _scatter_rows(x, dest3, n_out):
    t, d = x.shape
    top_k, n_chunks, rows = dest3.shape
    per_worker = n_chunks // _sc_num_workers()

    @functools.partial(
        pl.kernel,
        out_type=jax.ShapeDtypeStruct((n_out, d), x.dtype),
        mesh=_sc_mesh(),
        scratch_types=[pltpu.VMEM((top_k, rows), jnp.int32), pltpu.VMEM((rows, d), x.dtype)],
        name="sc_scatter",
    )
    def scatter(x_hbm, i_hbm, o_hbm, idx_v, rows_v):
        wid = _sc_worker_id()

        @pl.loop(0, per_worker)
        def _(c):
            chunk = wid * per_worker + c
            for kk in range(top_k):
                pltpu.sync_copy(i_hbm.at[kk, chunk], idx_v.at[kk])
            pltpu.sync_copy(x_hbm.at[pl.ds(pl.multiple_of(chunk * rows, 8), rows)], rows_v)
            for kk in range(top_k):
                pltpu.sync_copy(rows_v, o_hbm.at[idx_v.at[kk]])

    return scatter(x, dest3)


def _gather_rows(y, idx):
    n = idx.shape[0]
    d = y.shape[1]
    rows = SC_ROWS
    per_worker = n // _sc_num_workers()
    n_chunks = per_worker // rows

    @functools.partial(
        pl.kernel,
        out_type=jax.ShapeDtypeStruct((n, d), y.dtype),
        mesh=_sc_mesh(),
        scratch_types=[pltpu.VMEM((rows,), jnp.int32), pltpu.VMEM((rows, d), y.dtype)],
        name="sc_gather",
    )
    def gather(y_hbm, i_hbm, o_hbm, idx_v, rows_v):
        base = _sc_worker_id() * per_worker

        @pl.loop(0, n_chunks)
        def _(c):
            off = pl.multiple_of(base + c * rows, 8)
            pltpu.sync_copy(i_hbm.at[pl.ds(off, rows)], idx_v)
            pltpu.sync_copy(y_hbm.at[idx_v], rows_v)
            pltpu.sync_copy(rows_v, o_hbm.at[pl.ds(off, rows)])

    return gather(y, idx)


def _moe_kernel(blk_ref, exp_ref, lo_ref, hi_ref, cast_ref, init_ref, next_ref, xs_ref, wgu_hbm,
                bgu_ref, wd_hbm, bd_ref, ys_ref, wgu_f32, wd_f32, wgu_bf, wd_bf, sems):
    i = pl.program_id(0)

    def weight_copies(e):
        return (pltpu.make_async_copy(wgu_hbm.at[e], wgu_f32, sems.at[0]),
                pltpu.make_async_copy(wd_hbm.at[e], wd_f32, sems.at[1]))

    @pl.when(i == 0)
    def _():
        for cp in weight_copies(exp_ref[0]):
            cp.start()

    @pl.when(cast_ref[i] == 1)
    def _():
        for cp in weight_copies(exp_ref[i]):
            cp.wait()
        wgu_bf[...] = wgu_f32[...].astype(BF16)
        wd_bf[...] = wd_f32[...].astype(BF16)

        @pl.when(next_ref[i] >= 0)
        def _():
            for cp in weight_copies(next_ref[i]):
                cp.start()

    @pl.when(init_ref[i] == 1)
    def _():
        ys_ref[...] = jnp.zeros_like(ys_ref)

    @pl.when(hi_ref[i] > lo_ref[i])
    def _():
        de = wd_bf.shape[0]
        gu = _dot(xs_ref[...].astype(BF16), wgu_bf[...]) + bgu_ref[0]
        g = jnp.minimum(gu[:, :de], SWIGLU_LIMIT)
        lin = jnp.clip(gu[:, de:], -SWIGLU_LIMIT, SWIGLU_LIMIT)
        act = g / (1.0 + jnp.exp(-SWIGLU_ALPHA * g)) * (lin + 1.0)
        y = _dot(act.astype(BF16), wd_bf[...]) + bd_ref[0]
        row = lax.broadcasted_iota(jnp.int32, (y.shape[0], 1), 0)
        mine = jnp.logical_and(row >= lo_ref[i], row < hi_ref[i])
        ys_ref[...] = jnp.where(mine, y, ys_ref[...])


def _moe_call(sched, xs, wgu, bgu, wd, bd):
    n_items = sched[0].shape[0]
    _, d, de2 = wgu.shape
    de = de2 // 2
    wmap = lambda i, blk, exp, lo, hi, cast, init, nxt: (exp[i], 0, 0)
    xmap = lambda i, blk, exp, lo, hi, cast, init, nxt: (blk[i], 0)
    return pl.pallas_call(
        _moe_kernel,
        grid_spec=pltpu.PrefetchScalarGridSpec(
            num_scalar_prefetch=7,
            grid=(n_items,),
            in_specs=[
                pl.BlockSpec((MOE_BLOCK, d), xmap),
                pl.BlockSpec(memory_space=pl.ANY),
                pl.BlockSpec((1, 1, de2), wmap),
                pl.BlockSpec(memory_space=pl.ANY),
                pl.BlockSpec((1, 1, d), wmap),
            ],
            out_specs=pl.BlockSpec((MOE_BLOCK, d), xmap),
            scratch_shapes=[
                pltpu.VMEM((d, de2), F32), pltpu.VMEM((de, d), F32),
                pltpu.VMEM((d, de2), BF16), pltpu.VMEM((de, d), BF16),
                pltpu.SemaphoreType.DMA((2,)),
            ],
        ),
        out_shape=jax.ShapeDtypeStruct(xs.shape, F32),
        compiler_params=pltpu.CompilerParams(
            dimension_semantics=("arbitrary",), vmem_limit_bytes=VMEM_LIMIT),
        name="moe",
    )(*sched, xs, wgu, bgu, wd, bd)


def _moe_schedule(counts, n_rows):
    n_blocks = n_rows // MOE_BLOCK
    n_items = n_blocks + N_EXPERTS
    ends = jnp.cumsum(counts)
    starts = ends - counts
    first_blk = starts // MOE_BLOCK
    last_blk = (ends - 1) // MOE_BLOCK
    items_per = jnp.where(counts > 0, last_blk - first_blk + 1, 0)
    item_ends = jnp.cumsum(items_per)
    item_starts = item_ends - items_per
    total = item_ends[-1]
    it = jnp.arange(n_items, dtype=jnp.int32)
    live = it < total
    itc = jnp.minimum(it, total - 1)
    exp = jnp.sum((item_ends[None, :] <= itc[:, None]).astype(jnp.int32), axis=1)
    is_exp = exp[:, None] == jnp.arange(N_EXPERTS, dtype=jnp.int32)[None, :]
    pick = lambda table: jnp.sum(jnp.where(is_exp, table[None, :], 0), axis=1)
    blk = pick(first_blk) + itc - pick(item_starts)
    lo = jnp.clip(pick(starts) - blk * MOE_BLOCK, 0, MOE_BLOCK)
    hi = jnp.clip(pick(ends) - blk * MOE_BLOCK, 0, MOE_BLOCK)
    hi = jnp.where(live, hi, lo)
    prev_exp = jnp.concatenate([jnp.full((1,), -1, jnp.int32), exp[:-1]])
    prev_blk = jnp.concatenate([jnp.full((1,), -1, jnp.int32), blk[:-1]])
    cast = jnp.logical_and(live, exp != prev_exp)
    init = jnp.logical_and(live, blk != prev_blk)
    ar = jnp.arange(N_EXPERTS, dtype=jnp.int32)
    later = jnp.logical_and(counts[None, :] > 0, ar[None, :] > ar[:, None])
    next_exp = jnp.min(jnp.where(later, ar[None, :], N_EXPERTS), axis=1)
    next_exp = jnp.where(next_exp == N_EXPERTS, -1, next_exp)
    nxt = pick(next_exp)
    as_i32 = lambda a: a.astype(jnp.int32)
    return tuple(as_i32(a) for a in (blk, exp, lo, hi, cast, init, nxt)), starts


def _combine_kernel(*refs):
    yg_refs = refs[:TOP_K]
    x1_ref, gate_ref, mod_ref, fg_ref, o_ref = refs[TOP_K:]
    gates = gate_ref[...]
    y = gates[:, 0:1] * yg_refs[0][...]
    for kk in range(1, TOP_K):
        y = y + gates[:, kk:kk + 1] * yg_refs[kk][...]
    m = mod_ref[0]
    x2 = x1_ref[...] + m[5:6] * y
    o_ref[...] = _rms(x2) * fg_ref[...]


def _combine_call(yg, x1, gates, mod, fg, n_seq):
    t, d = x1.shape
    tt = COMBINE_TT
    tiles_per_seq = n_seq // tt
    tok = lambda i: (i, 0)
    n_tiles = t // tt
    slot_specs = [pl.BlockSpec((tt, d), functools.partial(lambda kk, i: (kk * n_tiles + i, 0), kk))
                  for kk in range(TOP_K)]
    return pl.pallas_call(
        _combine_kernel,
        grid=(n_tiles,),
        in_specs=slot_specs + [
            pl.BlockSpec((tt, d), tok),
            pl.BlockSpec((tt, LANES), tok),
            pl.BlockSpec((1, N_MOD, d), lambda i: (i // tiles_per_seq, 0, 0)),
            pl.BlockSpec((1, d), lambda i: (0, 0)),
        ],
        out_specs=pl.BlockSpec((tt, d), tok),
        out_shape=jax.ShapeDtypeStruct((t, d), F32),
        compiler_params=pltpu.CompilerParams(
            dimension_semantics=("arbitrary",), vmem_limit_bytes=VMEM_LIMIT),
        name="combine",
    )(*([yg] * TOP_K), x1, gates, mod, fg)


def _pad_cols(a, width):
    return jnp.pad(a, ((0, 0), (0, width - a.shape[1])))


def _rope_slab(w_rope, swapped):
    half = QK_ROPE // 2
    if swapped:
        w_rope = jnp.concatenate([w_rope[:, half:], w_rope[:, :half]], axis=1)
    zeros = jnp.zeros((w_rope.shape[0], QK_NOPE), w_rope.dtype)
    return _pad_cols(jnp.concatenate([zeros, w_rope], axis=1), LANES)


def _prep_w_in(w_in):
    kr0 = Q_LORA + KV_LORA
    w_kr = w_in[:, kr0:kr0 + QK_ROPE]
    return jnp.concatenate(
        [w_in[:, :kr0], _rope_slab(w_kr, False), _rope_slab(w_kr, True), w_in[:, kr0 + QK_ROPE:]],
        axis=1).astype(BF16)


def _prep_w_uq(w_uq):
    per = QK_NOPE + QK_ROPE
    slabs_a, slabs_b = [], []
    for hd in range(N_HEADS):
        w = w_uq[:, hd * per:(hd + 1) * per]
        slabs_a.append(_pad_cols(w, LANES))
        slabs_b.append(_rope_slab(w[:, QK_NOPE:], True))
    return jnp.concatenate(slabs_a + slabs_b, axis=1).astype(BF16)


def _prep_w_ukv(w_ukv):
    per = QK_NOPE + V_DIM
    ks, vs = [], []
    for hd in range(N_HEADS):
        w = w_ukv[:, hd * per:(hd + 1) * per]
        ks.append(_pad_cols(w[:, :QK_NOPE], LANES))
        lo, _ = _value_lanes(hd)
        vs.append(jnp.pad(w[:, QK_NOPE:], ((0, 0), (lo, LANES - V_DIM - lo))))
    return jnp.concatenate(ks + vs, axis=1).astype(BF16)


def _rope_tables(n_lat):
    rows = n_lat // GRID_W
    nf = QK_ROPE // 4
    row = jnp.repeat(jnp.arange(rows, dtype=F32), GRID_W)
    col = jnp.tile(jnp.arange(GRID_W, dtype=F32), rows)
    freqs = ROPE_BASE ** (-jnp.arange(nf, dtype=F32) / nf)
    ang = jnp.concatenate([row[:, None] * freqs, col[:, None] * freqs], axis=-1)
    cos, sin = jnp.cos(ang), jnp.sin(ang)
    ones = jnp.ones((n_lat, QK_NOPE), F32)
    zeros = jnp.zeros((n_lat, QK_NOPE), F32)
    cs = _pad_cols(jnp.concatenate([ones, cos, cos], axis=1), LANES)
    sn = _pad_cols(jnp.concatenate([zeros, -sin, sin], axis=1), LANES)
    return cs, sn


def kernel(x, c, ctx, c_ctx, w_mod, b_mod, norm1_g, w_in, q_norm_g, kv_norm_g, w_uq, w_ukv, w_pool,
           pool_scale, w_out, norm2_g, router_w, router_b, w_gate_up, b_gate_up, w_down, b_down,
           final_g):
    bsz, n, d = x.shape
    n_ctx = ctx.shape[1]
    t = bsz * n
    l = 0

    cc = jnp.concatenate([c, c_ctx[None, :], jnp.zeros((8 - bsz - 1, d), F32)], axis=0)
    mod = _mod_call(cc, w_mod[l], b_mod[l][None, :]).reshape(8, N_MOD, d)
    mod_lat, mod_ctx = mod[:bsz], mod[bsz:bsz + 1]

    win = _prep_w_in(w_in[l])
    wuq = _prep_w_uq(w_uq[l])
    wukv = _prep_w_ukv(w_ukv[l])
    cs, sn = _rope_tables(n)
    cs_ctx = jnp.broadcast_to((jnp.arange(LANES) < QK_NOPE + QK_ROPE).astype(F32), (n_ctx, LANES))
    sn_ctx = jnp.zeros((n_ctx, LANES), F32)
    g1 = norm1_g[l][None, :]
    qg = q_norm_g[l][None, :]
    kvg = kv_norm_g[l][None, :]

    q, k, v, u = _front_call(x, mod_lat, True, g1, win, qg, kvg, wuq, wukv, cs, sn,
                             is_ctx=False, ts=FRONT_TS)
    kc, vc = _front_call(ctx, mod_ctx, False, g1, win[:, Q_LORA:Q_LORA + 3 * LANES], qg, kvg, wuq,
                         wukv, cs_ctx, sn_ctx, is_ctx=True, ts=n_ctx)
    att = _attn_call(q, k, v, kc, vc)

    rw = _pad_cols(router_w[l], LANES)
    rb = jnp.concatenate([router_b[l], jnp.full((LANES - N_EXPERTS,), -jnp.inf, F32)])[None, :]
    x1, h2, gates, ridx, cnt = _mix_call(
        att, u, x, mod_lat, w_pool[l].astype(BF16), pool_scale[l][None, :], w_out[l].astype(BF16),
        norm2_g[l][None, :], rw, rb)

    counts = cnt[0, :N_EXPERTS].astype(jnp.int32)
    sched, starts = _moe_schedule(counts, t * TOP_K)
    ridx = ridx.reshape(t, LANES)
    is_exp = ridx[:, :TOP_K, None] == jnp.arange(N_EXPERTS, dtype=jnp.int32)
    dest = jnp.sum(jnp.where(is_exp, starts, 0), axis=-1) + ridx[:, TOP_K:2 * TOP_K]
    dest_t = dest.T

    xs = _scatter_rows(h2.reshape(t, d), dest_t.reshape(TOP_K, t // SC_ROWS, SC_ROWS), t * TOP_K)
    ys = _moe_call(sched, xs, w_gate_up[l], b_gate_up[l][:, None, :], w_down[l],
                   b_down[l][:, None, :])
    yg = _gather_rows(ys, dest_t.reshape(-1))
    out = _combine_call(yg, x1.reshape(t, d), gates.reshape(t, LANES), mod_lat, final_g[None, :], n)
    return out.reshape(bsz, n, d)
```

```python
import functools
import math

import jax
import jax.numpy as jnp
from jax import lax
from jax.experimental import pallas as pl
from jax.experimental.pallas import tpu as pltpu
from jax.experimental.pallas import tpu_sc as plsc

F32 = jnp.float32
BF16 = jnp.bfloat16
HIGHEST = lax.Precision.HIGHEST

N_HEADS = 8
QK_NOPE = 64
QK_ROPE = 32
V_DIM = 64
Q_LORA = 256
KV_LORA = 128
GRID_W = 64
ROPE_BASE = 10000.0
POOL_WINDOWS = (2, 4, 8, 16)
POOL_CH = 128
N_EXPERTS = 32
TOP_K = 4
SWIGLU_LIMIT = 7.0
SWIGLU_ALPHA = 1.702
MOE_BLOCK = 256
N_MOD = 6
EPS = 1e-6

LANES = 128
POOL_HALO = 8
VMEM_LIMIT = 56 * 1024 * 1024

FRONT_TS = 512
ATTN_TQ = 512
ATTN_KC = 512
MIX_TS = 512
COMBINE_TT = 256
SC_ROWS = 128


def _dot(a, b, **kw):
    return jnp.dot(a, b, preferred_element_type=F32, **kw)


def _dot_nt(a, b):
    return lax.dot_general(a, b, (((1,), (1,)), ((), ())), preferred_element_type=F32)


def _rms(x):
    return x * lax.rsqrt(jnp.mean(x * x, axis=-1, keepdims=True) + EPS)


def _pack_bf16_pairs(x):
    w = x.shape[1] // 2
    hi = lax.bitcast_convert_type(x[:, :w].astype(BF16).astype(F32), jnp.uint32)
    lo = lax.bitcast_convert_type(x[:, w:].astype(BF16).astype(F32), jnp.uint32)
    return hi | (lo >> 16)


def _unpack_bf16_pairs(words):
    hi = lax.bitcast_convert_type(words & jnp.uint32(0xFFFF0000), F32)
    lo = lax.bitcast_convert_type(words << 16, F32)
    return jnp.concatenate([hi, lo], axis=-1)


def _value_lanes(hd):
    return (0, V_DIM) if hd % 2 == 0 else (V_DIM, 2 * V_DIM)


def _ones_lane(hd):
    return V_DIM if hd % 2 == 0 else 0


def _mod_kernel(c_ref, w_ref, b_ref, o_ref):
    c = c_ref[...]
    a = c / (1.0 + jnp.exp(-c))
    o_ref[...] = _dot(a, w_ref[...], precision=HIGHEST) + b_ref[...]


def _mod_call(cc, w_mod, b_mod):
    d = w_mod.shape[0]
    return pl.pallas_call(
        _mod_kernel,
        grid=(N_MOD,),
        in_specs=[
            pl.BlockSpec((8, d), lambda j: (0, 0)),
            pl.BlockSpec((d, d), lambda j: (0, j)),
            pl.BlockSpec((1, d), lambda j: (0, j)),
        ],
        out_specs=pl.BlockSpec((8, d), lambda j: (0, j)),
        out_shape=jax.ShapeDtypeStruct((8, N_MOD * d), F32),
        compiler_params=pltpu.CompilerParams(
            dimension_semantics=("arbitrary",), vmem_limit_bytes=VMEM_LIMIT),
        name="mod",
    )(cc, w_mod, b_mod)


def _front_kernel(x_ref, mod_ref, g1_ref, win_ref, qg_ref, kvg_ref, wuq_ref, wukv_ref,
                  cs_ref, sn_ref, *out_refs, is_ctx, scale):
    x = x_ref[0]
    m = mod_ref[0]
    h = _rms(x) * g1_ref[...] * (1.0 + m[1:2]) + m[0:1]
    z = _dot(h.astype(BF16), win_ref[...])
    if is_ctx:
        k_ref, v_ref = out_refs
        zkv = z
    else:
        q_ref, k_ref, v_ref, u_ref = out_refs
        zkv = z[:, Q_LORA:Q_LORA + 3 * LANES]
        u_ref[0] = z[:, Q_LORA + 3 * LANES:]
    cs = cs_ref[...]
    sn = sn_ref[...]

    ckv = _rms(zkv[:, :KV_LORA]) * kvg_ref[...]
    kv = _dot(ckv.astype(BF16), wukv_ref[...])
    kr = zkv[:, LANES:2 * LANES] * cs + zkv[:, 2 * LANES:3 * LANES] * sn
    for hd in range(N_HEADS):
        k_ref[0, hd] = (kv[:, hd * LANES:(hd + 1) * LANES] + kr).astype(BF16)
    voff = N_HEADS * LANES
    lane = lax.broadcasted_iota(jnp.int32, (1, LANES), 1)
    for hd in range(N_HEADS):
        ones = jnp.where(lane == _ones_lane(hd), 1.0, 0.0)
        v_ref[0, hd] = (kv[:, voff + hd * LANES: voff + (hd + 1) * LANES] + ones).astype(BF16)

    if not is_ctx:
        cq = _rms(z[:, :Q_LORA]) * qg_ref[...]
        qq = _dot(cq.astype(BF16), wuq_ref[...])
        boff = N_HEADS * LANES
        for hd in range(N_HEADS):
            qa = qq[:, hd * LANES:(hd + 1) * LANES]
            qb = qq[:, boff + hd * LANES: boff + (hd + 1) * LANES]
            q_ref[0, hd] = ((qa * cs + qb * sn) * scale).astype(BF16)


def _front_call(xs, mod, mod_per_batch, g1, win, qg, kvg, wuq, wukv, cs, sn, *, is_ctx, ts):
    bsz, n, d = xs.shape
    nt = n // ts
    scale = math.log2(math.e) / math.sqrt(QK_NOPE + QK_ROPE)
    const = lambda b, i: (0, 0)
    mod_map = (lambda b, i: (b, 0, 0)) if mod_per_batch else (lambda b, i: (0, 0, 0))
    in_specs = [
        pl.BlockSpec((1, ts, d), lambda b, i: (b, i, 0)),
        pl.BlockSpec((1, N_MOD, d), mod_map),
        pl.BlockSpec((1, d), const),
        pl.BlockSpec(win.shape, const),
        pl.BlockSpec((1, Q_LORA), const),
        pl.BlockSpec((1, KV_LORA), const),
        pl.BlockSpec(wuq.shape, const),
        pl.BlockSpec(wukv.shape, const),
        pl.BlockSpec((ts, LANES), lambda b, i: (i, 0)),
        pl.BlockSpec((ts, LANES), lambda b, i: (i, 0)),
    ]
    k_spec = pl.BlockSpec((1, N_HEADS, ts, LANES), lambda b, i: (b, 0, i, 0))
    k_shape = jax.ShapeDtypeStruct((bsz, N_HEADS, n, LANES), BF16)
    if is_ctx:
        out_specs = [k_spec, k_spec]
        out_shape = [k_shape, k_shape]
    else:
        pool_w = win.shape[1] - Q_LORA - 3 * LANES
        out_specs = [k_spec, k_spec, k_spec, pl.BlockSpec((1, ts, pool_w), lambda b, i: (b, i, 0))]
        out_shape = [k_shape, k_shape, k_shape, jax.ShapeDtypeStruct((bsz, n, pool_w), F32)]
    return pl.pallas_call(
        functools.partial(_front_kernel, is_ctx=is_ctx, scale=scale),
        grid=(bsz, nt),
        in_specs=in_specs,
        out_specs=out_specs,
        out_shape=out_shape,
        compiler_params=pltpu.CompilerParams(
            dimension_semantics=("arbitrary", "arbitrary"), vmem_limit_bytes=VMEM_LIMIT),
        name="front_ctx" if is_ctx else "front",
    )(xs, mod, g1, win, qg, kvg, wuq, wukv, cs, sn)


def _attn_kernel(q_ref, k_ref, v_ref, kc_ref, vc_ref, o_ref, s_ref, mlc_ref, mxb_ref, oe_ref):
    n_ctx = kc_ref.shape[2]
    n_lat = k_ref.shape[2]
    chunks = [(None, 0, n_ctx)] + [(c * ATTN_KC, n_ctx + c * ATTN_KC, ATTN_KC)
                                   for c in range(n_lat // ATTN_KC)]

    def rows(lat_ref, ctx_ref, hd, ci):
        off, _, w = chunks[ci]
        return ctx_ref[0, hd] if off is None else lat_ref[0, hd, off:off + w, :]

    def score_chunk(hd, buf, ci):
        _, soff, w = chunks[ci]
        s = _dot_nt(q_ref[0, hd], rows(k_ref, kc_ref, hd, ci))
        s_ref[buf, :, soff:soff + w] = s
        ml = s[:, 0:LANES]
        for g in range(1, w // LANES):
            ml = jnp.maximum(ml, s[:, g * LANES:(g + 1) * LANES])
        mlc_ref[buf, ci] = ml

    def row_max(buf):
        ml = mlc_ref[buf, 0]
        for ci in range(1, len(chunks)):
            ml = jnp.maximum(ml, mlc_ref[buf, ci])
        mxb_ref[...] = jnp.broadcast_to(jnp.max(ml, axis=-1, keepdims=True), ml.shape)

    def weight_chunk(hd, buf, ci):
        _, soff, w = chunks[ci]
        mxb = mxb_ref[...]
        ps = [jnp.exp2(s_ref[buf, :, soff + g * LANES:soff + (g + 1) * LANES] - mxb)
              for g in range(w // LANES)]
        return _dot(jnp.concatenate(ps, axis=-1).astype(BF16), rows(v_ref, vc_ref, hd, ci))

    def stage(hw, bw, hs, bs, parity):
        row_max(bw)
        acc = None
        for ci in range(len(chunks)):
            if hs is not None:
                score_chunk(hs, bs, ci)
            part = weight_chunk(hw, bw, ci)
            acc = part if acc is None else acc + part
        one = _ones_lane(parity)
        return acc / acc[:, one:one + 1]

    def write_pair(j, o_odd):
        lane = lax.broadcasted_iota(jnp.int32, o_odd.shape, 1)
        o_ref[0, j] = jnp.where(lane < V_DIM, oe_ref[...], o_odd).astype(BF16)

    for ci in range(len(chunks)):
        score_chunk(0, 0, ci)

    def head_pair(j, carry):
        oe_ref[...] = stage(2 * j, 0, 2 * j + 1, 1, 0)
        write_pair(j, stage(2 * j + 1, 1, 2 * j + 2, 0, 1))
        return carry

    last = N_HEADS // 2 - 1
    lax.fori_loop(0, last, head_pair, 0)
    oe_ref[...] = stage(2 * last, 0, 2 * last + 1, 1, 0)
    write_pair(last, stage(2 * last + 1, 1, None, None, 1))


def _attn_call(q, k, v, kc, vc):
    bsz, _, n, _ = q.shape
    n_ctx = kc.shape[2]
    tq = ATTN_TQ
    n_chunks = 1 + n // ATTN_KC
    per_batch = lambda b, i: (b, 0, 0, 0)
    resident = dict(pipeline_mode=pl.Buffered(1))
    return pl.pallas_call(
        _attn_kernel,
        grid=(bsz, n // tq),
        in_specs=[
            pl.BlockSpec((1, N_HEADS, tq, LANES), lambda b, i: (b, 0, i, 0)),
            pl.BlockSpec((1, N_HEADS, n, LANES), per_batch, **resident),
            pl.BlockSpec((1, N_HEADS, n, LANES), per_batch, **resident),
            pl.BlockSpec((1, N_HEADS, n_ctx, LANES), per_batch, **resident),
            pl.BlockSpec((1, N_HEADS, n_ctx, LANES), per_batch, **resident),
        ],
        out_specs=pl.BlockSpec((1, N_HEADS // 2, tq, LANES), lambda b, i: (b, 0, i, 0)),
        out_shape=jax.ShapeDtypeStruct((bsz, N_HEADS // 2, n, LANES), BF16),
        scratch_shapes=[
            pltpu.VMEM((2, tq, n_ctx + n), F32),
            pltpu.VMEM((2, n_chunks, tq, LANES), F32),
            pltpu.VMEM((tq, LANES), F32),
            pltpu.VMEM((tq, LANES), F32),
        ],
        compiler_params=pltpu.CompilerParams(
            dimension_semantics=("arbitrary", "arbitrary"), vmem_limit_bytes=VMEM_LIMIT),
        name="attn",
    )(q, k, v, kc, vc)


def _mix_kernel(att_ref, u_ref, up_ref, un_ref, x_ref, mod_ref, wpool_ref, pscale_ref, wout_ref,
                g2_ref, rw_ref, rb_ref,
                x1_ref, h2_ref, gate_ref, ridx_ref, cnt_ref, ue_ref, base_ref, *, n_seq):
    b = pl.program_id(0)
    i = pl.program_id(1)
    n_tiles = pl.num_programs(1)
    ts = u_ref.shape[1]

    @pl.when(jnp.logical_and(b == 0, i == 0))
    def _():
        base_ref[...] = jnp.zeros_like(base_ref)

    u = u_ref[0]
    ue_ref[0:POOL_HALO] = jnp.where(i > 0, up_ref[0], 0.0)
    ue_ref[POOL_HALO:POOL_HALO + ts] = u
    ue_ref[POOL_HALO + ts:2 * POOL_HALO + ts] = jnp.where(i < n_tiles - 1, un_ref[0], 0.0)
    t = i * ts + lax.broadcasted_iota(jnp.int32, (ts, 1), 0)
    ys = []
    for g, w in enumerate(POOL_WINDOWS):
        half = w // 2
        lanes = slice(g * POOL_CH, (g + 1) * POOL_CH)
        ws = ue_ref[POOL_HALO - half:POOL_HALO - half + ts, lanes]
        for jj in range(-half + 1, half):
            ws = ws + ue_ref[POOL_HALO + jj:POOL_HALO + jj + ts, lanes]
        count = (jnp.minimum(t + half, n_seq) - jnp.maximum(t - half, 0)).astype(F32)
        mixed = (ws / count - u[:, lanes]).astype(BF16)
        ys.append(_dot(mixed, wpool_ref[g]))
    pool = jnp.concatenate(ys, axis=-1) * pscale_ref[...]

    cat = jnp.concatenate([att_ref[0, j] for j in range(N_HEADS // 2)] + [pool.astype(BF16)], axis=-1)
    m = mod_ref[0]
    x1 = x_ref[0] + m[2:3] * _dot(cat, wout_ref[...])
    x1_ref[0] = x1
    h2 = _rms(x1) * g2_ref[...] * (1.0 + m[4:5]) + m[3:4]
    h2_ref[0] = _pack_bf16_pairs(h2)

    logits = _dot(h2, rw_ref[...], precision=HIGHEST) + rb_ref[...]
    lane = lax.broadcasted_iota(jnp.int32, logits.shape, 1).astype(F32)
    vals, idxs = [], []
    for _k in range(TOP_K):
        mv = jnp.max(logits, axis=-1, keepdims=True)
        ix = jnp.min(jnp.where(logits == mv, lane, float(LANES)), axis=-1, keepdims=True)
        vals.append(mv)
        idxs.append(ix)
        logits = jnp.where(lane == ix, -jnp.inf, logits)
    es = [jnp.exp(v - vals[0]) for v in vals]
    den = es[0] + es[1] + es[2] + es[3]

    onehot = jnp.zeros(lane.shape, F32)
    for ix in idxs:
        onehot = onehot + jnp.where(lane == ix, 1.0, 0.0)
    row = lax.broadcasted_iota(jnp.int32, (ts, ts), 0)
    col = lax.broadcasted_iota(jnp.int32, (ts, ts), 1)
    tri = jnp.where(col < row, 1.0, 0.0).astype(BF16)
    before = _dot(tri, onehot.astype(BF16)) + base_ref[0:1, :]
    base_new = base_ref[0:1, :] + jnp.sum(onehot, axis=0, keepdims=True)
    base_ref[...] = jnp.broadcast_to(base_new, base_ref.shape)
    cnt_ref[...] = jnp.broadcast_to(base_new, cnt_ref.shape)

    gate_out = jnp.zeros(lane.shape, F32)
    ridx_out = jnp.zeros(lane.shape, F32)
    for kk in range(TOP_K):
        rank = jnp.sum(jnp.where(lane == idxs[kk], before, 0.0), axis=-1, keepdims=True)
        gate_out = jnp.where(lane == float(kk), es[kk] / den, gate_out)
        ridx_out = jnp.where(lane == float(kk), idxs[kk], ridx_out)
        ridx_out = jnp.where(lane == float(TOP_K + kk), rank, ridx_out)
    gate_ref[0] = gate_out
    ridx_ref[0] = ridx_out.astype(jnp.int32)


def _mix_call(att, u, x, mod, wpool, pscale, wout, g2, rw, rb):
    bsz, n, d = x.shape
    ts = MIX_TS
    nt = n // ts
    pool_w = u.shape[2]
    hb = ts // POOL_HALO
    n_halo_blocks = n // POOL_HALO
    const2 = lambda b, i: (0, 0)
    tok = lambda b, i: (b, i, 0)
    return pl.pallas_call(
        functools.partial(_mix_kernel, n_seq=n),
        grid=(bsz, nt),
        in_specs=[
            pl.BlockSpec((1, N_HEADS // 2, ts, LANES), lambda b, i: (b, 0, i, 0)),
            pl.BlockSpec((1, ts, pool_w), tok),
            pl.BlockSpec((1, POOL_HALO, pool_w), lambda b, i: (b, jnp.maximum(i * hb - 1, 0), 0)),
            pl.BlockSpec((1, POOL_HALO, pool_w),
                         lambda b, i: (b, jnp.minimum((i + 1) * hb, n_halo_blocks - 1), 0)),
            pl.BlockSpec((1, ts, d), tok),
            pl.BlockSpec((1, N_MOD, d), lambda b, i: (b, 0, 0)),
            pl.BlockSpec(wpool.shape, lambda b, i: (0, 0, 0)),
            pl.BlockSpec((1, pool_w), const2),
            pl.BlockSpec(wout.shape, const2),
            pl.BlockSpec((1, d), const2),
            pl.BlockSpec(rw.shape, const2),
            pl.BlockSpec((1, LANES), const2),
        ],
        out_specs=[
            pl.BlockSpec((1, ts, d), tok),
            pl.BlockSpec((1, ts, d // 2), tok),
            pl.BlockSpec((1, ts, LANES), tok),
            pl.BlockSpec((1, ts, LANES), tok),
            pl.BlockSpec((8, LANES), const2),
        ],
        out_shape=[
            jax.ShapeDtypeStruct((bsz, n, d), F32),
            jax.ShapeDtypeStruct((bsz, n, d // 2), jnp.uint32),
            jax.ShapeDtypeStruct((bsz, n, LANES), F32),
            jax.ShapeDtypeStruct((bsz, n, LANES), jnp.int32),
            jax.ShapeDtypeStruct((8, LANES), F32),
        ],
        scratch_shapes=[
            pltpu.VMEM((ts + 2 * POOL_HALO, pool_w), F32),
            pltpu.VMEM((8, LANES), F32),
        ],
        compiler_params=pltpu.CompilerParams(
            dimension_semantics=("arbitrary", "arbitrary"), vmem_limit_bytes=VMEM_LIMIT),
        name="mix",
    )(att, u, u, u, x, mod, wpool, pscale, wout, g2, rw, rb)


def _sc_mesh():
    return plsc.VectorSubcoreMesh(core_axis_name="core", subcore_axis_name="subcore")


def _sc_worker_id():
    info = plsc.get_sparse_core_info()
    return lax.axis_index("subcore") * info.num_cores + lax.axis_index("core")


def _sc_num_workers():
    info = plsc.get_sparse_core_info()
    return info.num_cores * info.num_subcores


def _scatter_rows(x, dest3, n_out):
    t, d = x.shape
    top_k, n_chunks, rows = dest3.shape
    per_worker = n_chunks // _sc_num_workers()

    @functools.partial(
        pl.kernel,
        out_type=jax.ShapeDtypeStruct((n_out, d), x.dtype),
        mesh=_sc_mesh(),
        scratch_types=[pltpu.VMEM((top_k, rows), jnp.int32), pltpu.VMEM((rows, d), x.dtype)],
        name="sc_scatter",
    )
    def scatter(x_hbm, i_hbm, o_hbm, idx_v, rows_v):
        wid = _sc_worker_id()

        @pl.loop(0, per_worker)
        def _(c):
            chunk = wid * per_worker + c
            for kk in range(top_k):
                pltpu.sync_copy(i_hbm.at[kk, chunk], idx_v.at[kk])
            pltpu.sync_copy(x_hbm.at[pl.ds(pl.multiple_of(chunk * rows, 8), rows)], rows_v)
            for kk in range(top_k):
                pltpu.sync_copy(rows_v, o_hbm.at[idx_v.at[kk]])

    return scatter(x, dest3)


def _gather_rows(y, idx):
    n = idx.shape[0]
    d = y.shape[1]
    rows = SC_ROWS
    per_worker = n // _sc_num_workers()
    n_chunks = per_worker // rows

    @functools.partial(
        pl.kernel,
        out_type=jax.ShapeDtypeStruct((n, d), y.dtype),
        mesh=_sc_mesh(),
        scratch_types=[pltpu.VMEM((rows,), jnp.int32), pltpu.VMEM((rows, d), y.dtype)],
        name="sc_gather",
    )
    def gather(y_hbm, i_hbm, o_hbm, idx_v, rows_v):
        base = _sc_worker_id() * per_worker

        @pl.loop(0, n_chunks)
        def _(c):
            off = pl.multiple_of(base + c * rows, 8)
            pltpu.sync_copy(i_hbm.at[pl.ds(off, rows)], idx_v)
            pltpu.sync_copy(y_hbm.at[idx_v], rows_v)
            pltpu.sync_copy(rows_v, o_hbm.at[pl.ds(off, rows)])

    return gather(y, idx)


def _moe_kernel(blk_ref, exp_ref, lo_ref, hi_ref, cast_ref, init_ref, next_ref, xs_ref, wgu_hbm,
                bgu_ref, wd_hbm, bd_ref, ys_ref, wgu_f32, wd_f32, wgu_bf, wd_bf, sems):
    i = pl.program_id(0)

    def weight_copies(e):
        return (pltpu.make_async_copy(wgu_hbm.at[e], wgu_f32, sems.at[0]),
                pltpu.make_async_copy(wd_hbm.at[e], wd_f32, sems.at[1]))

    @pl.when(i == 0)
    def _():
        for cp in weight_copies(exp_ref[0]):
            cp.start()

    @pl.when(cast_ref[i] == 1)
    def _():
        for cp in weight_copies(exp_ref[i]):
            cp.wait()
        wgu_bf[...] = wgu_f32[...].astype(BF16)
        wd_bf[...] = wd_f32[...].astype(BF16)

        @pl.when(next_ref[i] >= 0)
        def _():
            for cp in weight_copies(next_ref[i]):
                cp.start()

    @pl.when(init_ref[i] == 1)
    def _():
        ys_ref[...] = jnp.zeros_like(ys_ref)

    @pl.when(hi_ref[i] > lo_ref[i])
    def _():
        de = wd_bf.shape[0]
        x = _unpack_bf16_pairs(xs_ref[...]).astype(BF16)
        gu = _dot(x, wgu_bf[...]) + bgu_ref[0]
        g = jnp.minimum(gu[:, :de], SWIGLU_LIMIT)
        lin = jnp.clip(gu[:, de:], -SWIGLU_LIMIT, SWIGLU_LIMIT)
        act = g / (1.0 + jnp.exp(-SWIGLU_ALPHA * g)) * (lin + 1.0)
        y = _dot(act.astype(BF16), wd_bf[...]) + bd_ref[0]
        row = lax.broadcasted_iota(jnp.int32, (y.shape[0], 1), 0)
        mine = jnp.logical_and(row >= lo_ref[i], row < hi_ref[i])
        ys_ref[...] = jnp.where(mine, _pack_bf16_pairs(y), ys_ref[...])


def _moe_call(sched, xs, wgu, bgu, wd, bd):
    n_items = sched[0].shape[0]
    _, d, de2 = wgu.shape
    de = de2 // 2
    wmap = lambda i, blk, exp, lo, hi, cast, init, nxt: (exp[i], 0, 0)
    xmap = lambda i, blk, exp, lo, hi, cast, init, nxt: (blk[i], 0)
    return pl.pallas_call(
        _moe_kernel,
        grid_spec=pltpu.PrefetchScalarGridSpec(
            num_scalar_prefetch=7,
            grid=(n_items,),
            in_specs=[
                pl.BlockSpec((MOE_BLOCK, d // 2), xmap),
                pl.BlockSpec(memory_space=pl.ANY),
                pl.BlockSpec((1, 1, de2), wmap),
                pl.BlockSpec(memory_space=pl.ANY),
                pl.BlockSpec((1, 1, d), wmap),
            ],
            out_specs=pl.BlockSpec((MOE_BLOCK, d // 2), xmap),
            scratch_shapes=[
                pltpu.VMEM((d, de2), F32), pltpu.VMEM((de, d), F32),
                pltpu.VMEM((d, de2), BF16), pltpu.VMEM((de, d), BF16),
                pltpu.SemaphoreType.DMA((2,)),
            ],
        ),
        out_shape=jax.ShapeDtypeStruct(xs.shape, xs.dtype),
        compiler_params=pltpu.CompilerParams(
            dimension_semantics=("arbitrary",), vmem_limit_bytes=VMEM_LIMIT),
        name="moe",
    )(*sched, xs, wgu, bgu, wd, bd)


def _moe_schedule(counts, n_rows):
    n_blocks = n_rows // MOE_BLOCK
    n_items = n_blocks + N_EXPERTS
    ends = jnp.cumsum(counts)
    starts = ends - counts
    first_blk = starts // MOE_BLOCK
    last_blk = (ends - 1) // MOE_BLOCK
    items_per = jnp.where(counts > 0, last_blk - first_blk + 1, 0)
    item_ends = jnp.cumsum(items_per)
    item_starts = item_ends - items_per
    total = item_ends[-1]
    it = jnp.arange(n_items, dtype=jnp.int32)
    live = it < total
    itc = jnp.minimum(it, total - 1)
    exp = jnp.sum((item_ends[None, :] <= itc[:, None]).astype(jnp.int32), axis=1)
    is_exp = exp[:, None] == jnp.arange(N_EXPERTS, dtype=jnp.int32)[None, :]
    pick = lambda table: jnp.sum(jnp.where(is_exp, table[None, :], 0), axis=1)
    blk = pick(first_blk) + itc - pick(item_starts)
    lo = jnp.clip(pick(starts) - blk * MOE_BLOCK, 0, MOE_BLOCK)
    hi = jnp.clip(pick(ends) - blk * MOE_BLOCK, 0, MOE_BLOCK)
    hi = jnp.where(live, hi, lo)
    prev_exp = jnp.concatenate([jnp.full((1,), -1, jnp.int32), exp[:-1]])
    prev_blk = jnp.concatenate([jnp.full((1,), -1, jnp.int32), blk[:-1]])
    cast = jnp.logical_and(live, exp != prev_exp)
    init = jnp.logical_and(live, blk != prev_blk)
    ar = jnp.arange(N_EXPERTS, dtype=jnp.int32)
    later = jnp.logical_and(counts[None, :] > 0, ar[None, :] > ar[:, None])
    next_exp = jnp.min(jnp.where(later, ar[None, :], N_EXPERTS), axis=1)
    next_exp = jnp.where(next_exp == N_EXPERTS, -1, next_exp)
    nxt = pick(next_exp)
    as_i32 = lambda a: a.astype(jnp.int32)
    return tuple(as_i32(a) for a in (blk, exp, lo, hi, cast, init, nxt)), starts


def _combine_kernel(*refs):
    yg_refs = refs[:TOP_K]
    x1_ref, gate_ref, mod_ref, fg_ref, o_ref = refs[TOP_K:]
    gates = gate_ref[...]
    y = gates[:, 0:1] * _unpack_bf16_pairs(yg_refs[0][...])
    for kk in range(1, TOP_K):
        y = y + gates[:, kk:kk + 1] * _unpack_bf16_pairs(yg_refs[kk][...])
    m = mod_ref[0]
    x2 = x1_ref[...] + m[5:6] * y
    o_ref[...] = _rms(x2) * fg_ref[...]


def _combine_call(yg, x1, gates, mod, fg, n_seq):
    t, d = x1.shape
    tt = COMBINE_TT
    tiles_per_seq = n_seq // tt
    tok = lambda i: (i, 0)
    n_tiles = t // tt
    slot_specs = [pl.BlockSpec((tt, d // 2), functools.partial(lambda kk, i: (kk * n_tiles + i, 0), kk))
                  for kk in range(TOP_K)]
    return pl.pallas_call(
        _combine_kernel,
        grid=(n_tiles,),
        in_specs=slot_specs + [
            pl.BlockSpec((tt, d), tok),
            pl.BlockSpec((tt, LANES), tok),
            pl.BlockSpec((1, N_MOD, d), lambda i: (i // tiles_per_seq, 0, 0)),
            pl.BlockSpec((1, d), lambda i: (0, 0)),
        ],
        out_specs=pl.BlockSpec((tt, d), tok),
        out_shape=jax.ShapeDtypeStruct((t, d), F32),
        compiler_params=pltpu.CompilerParams(
            dimension_semantics=("arbitrary",), vmem_limit_bytes=VMEM_LIMIT),
        name="combine",
    )(*([yg] * TOP_K), x1, gates, mod, fg)


def _pad_cols(a, width):
    return jnp.pad(a, ((0, 0), (0, width - a.shape[1])))


def _rope_slab(w_rope, swapped):
    half = QK_ROPE // 2
    if swapped:
        w_rope = jnp.concatenate([w_rope[:, half:], w_rope[:, :half]], axis=1)
    zeros = jnp.zeros((w_rope.shape[0], QK_NOPE), w_rope.dtype)
    return _pad_cols(jnp.concatenate([zeros, w_rope], axis=1), LANES)


def _prep_w_in(w_in):
    kr0 = Q_LORA + KV_LORA
    w_kr = w_in[:, kr0:kr0 + QK_ROPE]
    return jnp.concatenate(
        [w_in[:, :kr0], _rope_slab(w_kr, False), _rope_slab(w_kr, True), w_in[:, kr0 + QK_ROPE:]],
        axis=1).astype(BF16)


def _prep_w_uq(w_uq):
    per = QK_NOPE + QK_ROPE
    slabs_a, slabs_b = [], []
    for hd in range(N_HEADS):
        w = w_uq[:, hd * per:(hd + 1) * per]
        slabs_a.append(_pad_cols(w, LANES))
        slabs_b.append(_rope_slab(w[:, QK_NOPE:], True))
    return jnp.concatenate(slabs_a + slabs_b, axis=1).astype(BF16)


def _prep_w_ukv(w_ukv):
    per = QK_NOPE + V_DIM
    ks, vs = [], []
    for hd in range(N_HEADS):
        w = w_ukv[:, hd * per:(hd + 1) * per]
        ks.append(_pad_cols(w[:, :QK_NOPE], LANES))
        lo, _ = _value_lanes(hd)
        vs.append(jnp.pad(w[:, QK_NOPE:], ((0, 0), (lo, LANES - V_DIM - lo))))
    return jnp.concatenate(ks + vs, axis=1).astype(BF16)


def _rope_tables(n_lat):
    rows = n_lat // GRID_W
    nf = QK_ROPE // 4
    row = jnp.repeat(jnp.arange(rows, dtype=F32), GRID_W)
    col = jnp.tile(jnp.arange(GRID_W, dtype=F32), rows)
    freqs = ROPE_BASE ** (-jnp.arange(nf, dtype=F32) / nf)
    ang = jnp.concatenate([row[:, None] * freqs, col[:, None] * freqs], axis=-1)
    cos, sin = jnp.cos(ang), jnp.sin(ang)
    ones = jnp.ones((n_lat, QK_NOPE), F32)
    zeros = jnp.zeros((n_lat, QK_NOPE), F32)
    cs = _pad_cols(jnp.concatenate([ones, cos, cos], axis=1), LANES)
    sn = _pad_cols(jnp.concatenate([zeros, -sin, sin], axis=1), LANES)
    return cs, sn


def kernel(x, c, ctx, c_ctx, w_mod, b_mod, norm1_g, w_in, q_norm_g, kv_norm_g, w_uq, w_ukv, w_pool,
           pool_scale, w_out, norm2_g, router_w, router_b, w_gate_up, b_gate_up, w_down, b_down,
           final_g):
    bsz, n, d = x.shape
    n_ctx = ctx.shape[1]
    t = bsz * n
    l = 0

    cc = jnp.concatenate([c, c_ctx[None, :], jnp.zeros((8 - bsz - 1, d), F32)], axis=0)
    mod = _mod_call(cc, w_mod[l], b_mod[l][None, :]).reshape(8, N_MOD, d)
    mod_lat, mod_ctx = mod[:bsz], mod[bsz:bsz + 1]

    win = _prep_w_in(w_in[l])
    wuq = _prep_w_uq(w_uq[l])
    wukv = _prep_w_ukv(w_ukv[l])
    cs, sn = _rope_tables(n)
    cs_ctx = jnp.broadcast_to((jnp.arange(LANES) < QK_NOPE + QK_ROPE).astype(F32), (n_ctx, LANES))
    sn_ctx = jnp.zeros((n_ctx, LANES), F32)
    g1 = norm1_g[l][None, :]
    qg = q_norm_g[l][None, :]
    kvg = kv_norm_g[l][None, :]

    q, k, v, u = _front_call(x, mod_lat, True, g1, win, qg, kvg, wuq, wukv, cs, sn,
                             is_ctx=False, ts=FRONT_TS)
    kc, vc = _front_call(ctx, mod_ctx, False, g1, win[:, Q_LORA:Q_LORA + 3 * LANES], qg, kvg, wuq,
                         wukv, cs_ctx, sn_ctx, is_ctx=True, ts=n_ctx)
    att = _attn_call(q, k, v, kc, vc)

    rw = _pad_cols(router_w[l], LANES)
    rb = jnp.concatenate([router_b[l], jnp.full((LANES - N_EXPERTS,), -jnp.inf, F32)])[None, :]
    x1, h2, gates, ridx, cnt = _mix_call(
        att, u, x, mod_lat, w_pool[l].astype(BF16), pool_scale[l][None, :], w_out[l].astype(BF16),
        norm2_g[l][None, :], rw, rb)

    counts = cnt[0, :N_EXPERTS].astype(jnp.int32)
    sched, starts = _moe_schedule(counts, t * TOP_K)
    ridx = ridx.reshape(t, LANES)
    is_exp = ridx[:, :TOP_K, None] == jnp.arange(N_EXPERTS, dtype=jnp.int32)
    dest = jnp.sum(jnp.where(is_exp, starts, 0), axis=-1) + ridx[:, TOP_K:2 * TOP_K]
    dest_t = dest.T

    xs = _scatter_rows(h2.reshape(t, d // 2), dest_t.reshape(TOP_K, t // SC_ROWS, SC_ROWS), t * TOP_K)
    ys = _moe_call(sched, xs, w_gate_up[l], b_gate_up[l][:, None, :], w_down[l],
                   b_down[l][:, None, :])
    yg = _gather_rows(ys, dest_t.reshape(-1))
    out = _combine_call(yg, x1.reshape(t, d), gates.reshape(t, LANES), mod_lat, final_g[None, :], n)
    return out.reshape(bsz, n, d)
```

```python
import functools
import math

import jax
import jax.numpy as jnp
from jax import lax
from jax.experimental import pallas as pl
from jax.experimental.pallas import tpu as pltpu
from jax.experimental.pallas import tpu_sc as plsc

F32 = jnp.float32
BF16 = jnp.bfloat16
HIGHEST = lax.Precision.HIGHEST

N_HEADS = 8
QK_NOPE = 64
QK_ROPE = 32
V_DIM = 64
Q_LORA = 256
KV_LORA = 128
GRID_W = 64
ROPE_BASE = 10000.0
POOL_WINDOWS = (2, 4, 8, 16)
POOL_CH = 128
N_EXPERTS = 32
TOP_K = 4
SWIGLU_LIMIT = 7.0
SWIGLU_ALPHA = 1.702
MOE_BLOCK = 256
N_MOD = 6
EPS = 1e-6

LANES = 128
POOL_HALO = 8
VMEM_LIMIT = 56 * 1024 * 1024

FRONT_TS = 512
ATTN_TQ = 512
ATTN_TILES = 4
ATTN_KC = 512
MIX_TS = 512
COMBINE_TT = 256
SC_ROWS = 128


def _dot(a, b, **kw):
    return jnp.dot(a, b, preferred_element_type=F32, **kw)


def _dot_nt(a, b):
    return lax.dot_general(a, b, (((1,), (1,)), ((), ())), preferred_element_type=F32)


def _rms(x):
    return x * lax.rsqrt(jnp.mean(x * x, axis=-1, keepdims=True) + EPS)


def _pack_bf16_pairs(x):
    w = x.shape[1] // 2
    hi = lax.bitcast_convert_type(x[:, :w].astype(BF16).astype(F32), jnp.uint32)
    lo = lax.bitcast_convert_type(x[:, w:].astype(BF16).astype(F32), jnp.uint32)
    return hi | (lo >> 16)


def _unpack_bf16_pairs(words):
    hi = lax.bitcast_convert_type(words & jnp.uint32(0xFFFF0000), F32)
    lo = lax.bitcast_convert_type(words << 16, F32)
    return jnp.concatenate([hi, lo], axis=-1)


def _value_lanes(hd):
    return (0, V_DIM) if hd % 2 == 0 else (V_DIM, 2 * V_DIM)


def _ones_lane(hd):
    return V_DIM if hd % 2 == 0 else 0


def _mod_kernel(c_ref, w_ref, b_ref, o_ref):
    c = c_ref[...]
    a = c / (1.0 + jnp.exp(-c))
    o_ref[...] = _dot(a, w_ref[...], precision=HIGHEST) + b_ref[...]


def _mod_call(cc, w_mod, b_mod):
    d = w_mod.shape[0]
    return pl.pallas_call(
        _mod_kernel,
        grid=(N_MOD,),
        in_specs=[
            pl.BlockSpec((8, d), lambda j: (0, 0)),
            pl.BlockSpec((d, d), lambda j: (0, j)),
            pl.BlockSpec((1, d), lambda j: (0, j)),
        ],
        out_specs=pl.BlockSpec((8, d), lambda j: (0, j)),
        out_shape=jax.ShapeDtypeStruct((8, N_MOD * d), F32),
        compiler_params=pltpu.CompilerParams(
            dimension_semantics=("arbitrary",), vmem_limit_bytes=VMEM_LIMIT),
        name="mod",
    )(cc, w_mod, b_mod)


def _front_kernel(x_ref, mod_ref, g1_ref, win_ref, qg_ref, kvg_ref, wuq_ref, wukv_ref,
                  cs_ref, sn_ref, *out_refs, is_ctx, scale):
    x = x_ref[0]
    m = mod_ref[0]
    h = _rms(x) * g1_ref[...] * (1.0 + m[1:2]) + m[0:1]
    z = _dot(h.astype(BF16), win_ref[...])
    if is_ctx:
        k_ref, v_ref = out_refs
        zkv = z
    else:
        q_ref, k_ref, v_ref, u_ref = out_refs
        zkv = z[:, Q_LORA:Q_LORA + 3 * LANES]
        u_ref[0] = z[:, Q_LORA + 3 * LANES:]
    cs = cs_ref[...]
    sn = sn_ref[...]

    ckv = _rms(zkv[:, :KV_LORA]) * kvg_ref[...]
    kv = _dot(ckv.astype(BF16), wukv_ref[...])
    kr = zkv[:, LANES:2 * LANES] * cs + zkv[:, 2 * LANES:3 * LANES] * sn
    for hd in range(N_HEADS):
        k_ref[0, hd] = (kv[:, hd * LANES:(hd + 1) * LANES] + kr).astype(BF16)
    voff = N_HEADS * LANES
    lane = lax.broadcasted_iota(jnp.int32, (1, LANES), 1)
    for hd in range(N_HEADS):
        ones = jnp.where(lane == _ones_lane(hd), 1.0, 0.0)
        v_ref[0, hd] = (kv[:, voff + hd * LANES: voff + (hd + 1) * LANES] + ones).astype(BF16)

    if not is_ctx:
        cq = _rms(z[:, :Q_LORA]) * qg_ref[...]
        qq = _dot(cq.astype(BF16), wuq_ref[...])
        boff = N_HEADS * LANES
        for hd in range(N_HEADS):
            qa = qq[:, hd * LANES:(hd + 1) * LANES]
            qb = qq[:, boff + hd * LANES: boff + (hd + 1) * LANES]
            q_ref[0, hd] = ((qa * cs + qb * sn) * scale).astype(BF16)


def _front_call(xs, mod, mod_per_batch, g1, win, qg, kvg, wuq, wukv, cs, sn, *, is_ctx, ts):
    bsz, n, d = xs.shape
    nt = n // ts
    scale = math.log2(math.e) / math.sqrt(QK_NOPE + QK_ROPE)
    const = lambda b, i: (0, 0)
    mod_map = (lambda b, i: (b, 0, 0)) if mod_per_batch else (lambda b, i: (0, 0, 0))
    in_specs = [
        pl.BlockSpec((1, ts, d), lambda b, i: (b, i, 0)),
        pl.BlockSpec((1, N_MOD, d), mod_map),
        pl.BlockSpec((1, d), const),
        pl.BlockSpec(win.shape, const),
        pl.BlockSpec((1, Q_LORA), const),
        pl.BlockSpec((1, KV_LORA), const),
        pl.BlockSpec(wuq.shape, const),
        pl.BlockSpec(wukv.shape, const),
        pl.BlockSpec((ts, LANES), lambda b, i: (i, 0)),
        pl.BlockSpec((ts, LANES), lambda b, i: (i, 0)),
    ]
    k_spec = pl.BlockSpec((1, N_HEADS, ts, LANES), lambda b, i: (b, 0, i, 0))
    k_shape = jax.ShapeDtypeStruct((bsz, N_HEADS, n, LANES), BF16)
    if is_ctx:
        out_specs = [k_spec, k_spec]
        out_shape = [k_shape, k_shape]
    else:
        pool_w = win.shape[1] - Q_LORA - 3 * LANES
        out_specs = [k_spec, k_spec, k_spec, pl.BlockSpec((1, ts, pool_w), lambda b, i: (b, i, 0))]
        out_shape = [k_shape, k_shape, k_shape, jax.ShapeDtypeStruct((bsz, n, pool_w), F32)]
    return pl.pallas_call(
        functools.partial(_front_kernel, is_ctx=is_ctx, scale=scale),
        grid=(bsz, nt),
        in_specs=in_specs,
        out_specs=out_specs,
        out_shape=out_shape,
        compiler_params=pltpu.CompilerParams(
            dimension_semantics=("arbitrary", "arbitrary"), vmem_limit_bytes=VMEM_LIMIT),
        name="front_ctx" if is_ctx else "front",
    )(xs, mod, g1, win, qg, kvg, wuq, wukv, cs, sn)


def _attn_kernel(q_ref, k_ref, v_ref, kc_ref, vc_ref, o_ref, s_ref, mlc_ref, mxb_ref, oe_ref):
    n_ctx = kc_ref.shape[2]
    n_lat = k_ref.shape[2]
    chunks = [(None, 0, n_ctx)] + [(c * ATTN_KC, n_ctx + c * ATTN_KC, ATTN_KC)
                                   for c in range(n_lat // ATTN_KC)]

    def rows(lat_ref, ctx_ref, hd, ci):
        off, _, w = chunks[ci]
        return ctx_ref[0, hd] if off is None else lat_ref[0, hd, off:off + w, :]

    def q_rows(tile):
        if isinstance(tile, int):
            return pl.ds(tile * ATTN_TQ, ATTN_TQ)
        return pl.ds(pl.multiple_of(tile * ATTN_TQ, ATTN_TQ), ATTN_TQ)

    def score_chunk(tile, hd, buf, ci):
        _, soff, w = chunks[ci]
        s = _dot_nt(q_ref[0, hd, q_rows(tile), :], rows(k_ref, kc_ref, hd, ci))
        s_ref[buf, :, soff:soff + w] = s
        ml = s[:, 0:LANES]
        for g in range(1, w // LANES):
            ml = jnp.maximum(ml, s[:, g * LANES:(g + 1) * LANES])
        mlc_ref[buf, ci] = ml

    def row_max(buf):
        ml = mlc_ref[buf, 0]
        for ci in range(1, len(chunks)):
            ml = jnp.maximum(ml, mlc_ref[buf, ci])
        mxb_ref[...] = jnp.broadcast_to(jnp.max(ml, axis=-1, keepdims=True), ml.shape)

    def weight_chunk(hd, buf, ci):
        _, soff, w = chunks[ci]
        mxb = mxb_ref[...]
        ps = [jnp.exp2(s_ref[buf, :, soff + g * LANES:soff + (g + 1) * LANES] - mxb)
              for g in range(w // LANES)]
        return _dot(jnp.concatenate(ps, axis=-1).astype(BF16), rows(v_ref, vc_ref, hd, ci))

    def stage(hw, bw, scoring, bs, parity):
        row_max(bw)
        acc = None
        for ci in range(len(chunks)):
            if scoring is not None:
                score_chunk(*scoring, bs, ci)
            part = weight_chunk(hw, bw, ci)
            acc = part if acc is None else acc + part
        one = _ones_lane(parity)
        return acc / acc[:, one:one + 1]

    def write_pair(tile, j, o_odd):
        lane = lax.broadcasted_iota(jnp.int32, o_odd.shape, 1)
        o_ref[0, j, q_rows(tile), :] = jnp.where(lane < V_DIM, oe_ref[...], o_odd).astype(BF16)

    pairs_per_tile = N_HEADS // 2
    n_pairs = (q_ref.shape[2] // ATTN_TQ) * pairs_per_tile

    def split(p):
        return p // pairs_per_tile, p % pairs_per_tile

    for ci in range(len(chunks)):
        score_chunk(0, 0, 0, ci)

    def head_pair(p, carry):
        tile, j = split(p)
        nxt_tile, nxt_j = split(p + 1)
        oe_ref[...] = stage(2 * j, 0, (tile, 2 * j + 1), 1, 0)
        write_pair(tile, j, stage(2 * j + 1, 1, (nxt_tile, 2 * nxt_j), 0, 1))
        return carry

    lax.fori_loop(0, n_pairs - 1, head_pair, 0)
    tile, j = split(n_pairs - 1)
    oe_ref[...] = stage(2 * j, 0, (tile, 2 * j + 1), 1, 0)
    write_pair(tile, j, stage(2 * j + 1, 1, None, None, 1))


def _attn_call(q, k, v, kc, vc):
    bsz, _, n, _ = q.shape
    n_ctx = kc.shape[2]
    tq = ATTN_TQ
    n_chunks = 1 + n // ATTN_KC
    per_batch = lambda b, i: (b, 0, 0, 0)
    resident = dict(pipeline_mode=pl.Buffered(1))
    rows_per_step = ATTN_TILES * tq
    return pl.pallas_call(
        _attn_kernel,
        grid=(bsz, n // rows_per_step),
        in_specs=[
            pl.BlockSpec((1, N_HEADS, rows_per_step, LANES), lambda b, i: (b, 0, i, 0)),
            pl.BlockSpec((1, N_HEADS, n, LANES), per_batch, **resident),
            pl.BlockSpec((1, N_HEADS, n, LANES), per_batch, **resident),
            pl.BlockSpec((1, N_HEADS, n_ctx, LANES), per_batch, **resident),
            pl.BlockSpec((1, N_HEADS, n_ctx, LANES), per_batch, **resident),
        ],
        out_specs=pl.BlockSpec((1, N_HEADS // 2, rows_per_step, LANES), lambda b, i: (b, 0, i, 0)),
        out_shape=jax.ShapeDtypeStruct((bsz, N_HEADS // 2, n, LANES), BF16),
        scratch_shapes=[
            pltpu.VMEM((2, tq, n_ctx + n), F32),
            pltpu.VMEM((2, n_chunks, tq, LANES), F32),
            pltpu.VMEM((tq, LANES), F32),
            pltpu.VMEM((tq, LANES), F32),
        ],
        compiler_params=pltpu.CompilerParams(
            dimension_semantics=("arbitrary", "arbitrary"), vmem_limit_bytes=VMEM_LIMIT),
        name="attn",
    )(q, k, v, kc, vc)


def _mix_kernel(att_ref, u_ref, up_ref, un_ref, x_ref, mod_ref, wpool_ref, pscale_ref, wout_ref,
                g2_ref, rw_ref, rb_ref,
                x1_ref, h2_ref, gate_ref, ridx_ref, cnt_ref, ue_ref, base_ref, *, n_seq):
    b = pl.program_id(0)
    i = pl.program_id(1)
    n_tiles = pl.num_programs(1)
    ts = u_ref.shape[1]

    @pl.when(jnp.logical_and(b == 0, i == 0))
    def _():
        base_ref[...] = jnp.zeros_like(base_ref)

    u = u_ref[0]
    ue_ref[0:POOL_HALO] = jnp.where(i > 0, up_ref[0], 0.0)
    ue_ref[POOL_HALO:POOL_HALO + ts] = u
    ue_ref[POOL_HALO + ts:2 * POOL_HALO + ts] = jnp.where(i < n_tiles - 1, un_ref[0], 0.0)
    t = i * ts + lax.broadcasted_iota(jnp.int32, (ts, 1), 0)
    ys = []
    for g, w in enumerate(POOL_WINDOWS):
        half = w // 2
        lanes = slice(g * POOL_CH, (g + 1) * POOL_CH)
        ws = ue_ref[POOL_HALO - half:POOL_HALO - half + ts, lanes]
        for jj in range(-half + 1, half):
            ws = ws + ue_ref[POOL_HALO + jj:POOL_HALO + jj + ts, lanes]
        count = (jnp.minimum(t + half, n_seq) - jnp.maximum(t - half, 0)).astype(F32)
        mixed = (ws / count - u[:, lanes]).astype(BF16)
        ys.append(_dot(mixed, wpool_ref[g]))
    pool = jnp.concatenate(ys, axis=-1) * pscale_ref[...]

    cat = jnp.concatenate([att_ref[0, j] for j in range(N_HEADS // 2)] + [pool.astype(BF16)], axis=-1)
    m = mod_ref[0]
    x1 = x_ref[0] + m[2:3] * _dot(cat, wout_ref[...])
    x1_ref[0] = x1
    h2 = _rms(x1) * g2_ref[...] * (1.0 + m[4:5]) + m[3:4]
    h2_ref[0] = _pack_bf16_pairs(h2)

    h_hi = h2.astype(BF16)
    h_lo = (h2 - h_hi.astype(F32)).astype(BF16)
    hi_part = _dot(h_hi, rw_ref[...])
    logits = (hi_part[:, :LANES] + hi_part[:, LANES:] + _dot(h_lo, rw_ref[:, :LANES])) + rb_ref[...]
    lane = lax.broadcasted_iota(jnp.int32, logits.shape, 1).astype(F32)
    vals, idxs = [], []
    for _k in range(TOP_K):
        mv = jnp.max(logits, axis=-1, keepdims=True)
        ix = jnp.min(jnp.where(logits == mv, lane, float(LANES)), axis=-1, keepdims=True)
        vals.append(mv)
        idxs.append(ix)
        logits = jnp.where(lane == ix, -jnp.inf, logits)
    es = [jnp.exp(v - vals[0]) for v in vals]
    den = es[0] + es[1] + es[2] + es[3]

    onehot = jnp.zeros(lane.shape, F32)
    for ix in idxs:
        onehot = onehot + jnp.where(lane == ix, 1.0, 0.0)
    row = lax.broadcasted_iota(jnp.int32, (ts, ts), 0)
    col = lax.broadcasted_iota(jnp.int32, (ts, ts), 1)
    tri = jnp.where(col < row, 1.0, 0.0).astype(BF16)
    before = _dot(tri, onehot.astype(BF16)) + base_ref[0:1, :]
    base_new = base_ref[0:1, :] + jnp.sum(onehot, axis=0, keepdims=True)
    base_ref[...] = jnp.broadcast_to(base_new, base_ref.shape)
    cnt_ref[...] = jnp.broadcast_to(base_new, cnt_ref.shape)

    gate_out = jnp.zeros(lane.shape, F32)
    ridx_out = jnp.zeros(lane.shape, F32)
    for kk in range(TOP_K):
        rank = jnp.sum(jnp.where(lane == idxs[kk], before, 0.0), axis=-1, keepdims=True)
        gate_out = jnp.where(lane == float(kk), es[kk] / den, gate_out)
        ridx_out = jnp.where(lane == float(kk), idxs[kk], ridx_out)
        ridx_out = jnp.where(lane == float(TOP_K + kk), rank, ridx_out)
    gate_ref[0] = gate_out
    ridx_ref[0] = ridx_out.astype(jnp.int32)


def _mix_call(att, u, x, mod, wpool, pscale, wout, g2, rw, rb):
    bsz, n, d = x.shape
    ts = MIX_TS
    nt = n // ts
    pool_w = u.shape[2]
    hb = ts // POOL_HALO
    n_halo_blocks = n // POOL_HALO
    const2 = lambda b, i: (0, 0)
    tok = lambda b, i: (b, i, 0)
    return pl.pallas_call(
        functools.partial(_mix_kernel, n_seq=n),
        grid=(bsz, nt),
        in_specs=[
            pl.BlockSpec((1, N_HEADS // 2, ts, LANES), lambda b, i: (b, 0, i, 0)),
            pl.BlockSpec((1, ts, pool_w), tok),
            pl.BlockSpec((1, POOL_HALO, pool_w), lambda b, i: (b, jnp.maximum(i * hb - 1, 0), 0)),
            pl.BlockSpec((1, POOL_HALO, pool_w),
                         lambda b, i: (b, jnp.minimum((i + 1) * hb, n_halo_blocks - 1), 0)),
            pl.BlockSpec((1, ts, d), tok),
            pl.BlockSpec((1, N_MOD, d), lambda b, i: (b, 0, 0)),
            pl.BlockSpec(wpool.shape, lambda b, i: (0, 0, 0)),
            pl.BlockSpec((1, pool_w), const2),
            pl.BlockSpec(wout.shape, const2),
            pl.BlockSpec((1, d), const2),
            pl.BlockSpec(rw.shape, const2),
            pl.BlockSpec((1, LANES), const2),
        ],
        out_specs=[
            pl.BlockSpec((1, ts, d), tok),
            pl.BlockSpec((1, ts, d // 2), tok),
            pl.BlockSpec((1, ts, LANES), tok),
            pl.BlockSpec((1, ts, LANES), tok),
            pl.BlockSpec((8, LANES), const2),
        ],
        out_shape=[
            jax.ShapeDtypeStruct((bsz, n, d), F32),
            jax.ShapeDtypeStruct((bsz, n, d // 2), jnp.uint32),
            jax.ShapeDtypeStruct((bsz, n, LANES), F32),
            jax.ShapeDtypeStruct((bsz, n, LANES), jnp.int32),
            jax.ShapeDtypeStruct((8, LANES), F32),
        ],
        scratch_shapes=[
            pltpu.VMEM((ts + 2 * POOL_HALO, pool_w), F32),
            pltpu.VMEM((8, LANES), F32),
        ],
        compiler_params=pltpu.CompilerParams(
            dimension_semantics=("arbitrary", "arbitrary"), vmem_limit_bytes=VMEM_LIMIT),
        name="mix",
    )(att, u, u, u, x, mod, wpool, pscale, wout, g2, rw, rb)


def _sc_mesh():
    return plsc.VectorSubcoreMesh(core_axis_name="core", subcore_axis_name="subcore")


def _sc_worker_id():
    info = plsc.get_sparse_core_info()
    return lax.axis_index("subcore") * info.num_cores + lax.axis_index("core")


def _sc_num_workers():
    info = plsc.get_sparse_core_info()
    return info.num_cores * info.num_subcores


def _scatter_rows(x, dest3, n_out):
    t, d = x.shape
    top_k, n_chunks, rows = dest3.shape
    per_worker = n_chunks // _sc_num_workers()

    @functools.partial(
        pl.kernel,
        out_type=jax.ShapeDtypeStruct((n_out, d), x.dtype),
        mesh=_sc_mesh(),
        scratch_types=[pltpu.VMEM((top_k, rows), jnp.int32), pltpu.VMEM((rows, d), x.dtype)],
        name="sc_scatter",
    )
    def scatter(x_hbm, i_hbm, o_hbm, idx_v, rows_v):
        wid = _sc_worker_id()

        @pl.loop(0, per_worker)
        def _(c):
            chunk = wid * per_worker + c
            for kk in range(top_k):
                pltpu.sync_copy(i_hbm.at[kk, chunk], idx_v.at[kk])
            pltpu.sync_copy(x_hbm.at[pl.ds(pl.multiple_of(chunk * rows, 8), rows)], rows_v)
            for kk in range(top_k):
                pltpu.sync_copy(rows_v, o_hbm.at[idx_v.at[kk]])

    return scatter(x, dest3)


def _gather_rows(y, idx):
    n = idx.shape[0]
    d = y.shape[1]
    rows = SC_ROWS
    per_worker = n // _sc_num_workers()
    n_chunks = per_worker // rows

    @functools.partial(
        pl.kernel,
        out_type=jax.ShapeDtypeStruct((n, d), y.dtype),
        mesh=_sc_mesh(),
        scratch_types=[pltpu.VMEM((rows,), jnp.int32), pltpu.VMEM((rows, d), y.dtype)],
        name="sc_gather",
    )
    def gather(y_hbm, i_hbm, o_hbm, idx_v, rows_v):
        base = _sc_worker_id() * per_worker

        @pl.loop(0, n_chunks)
        def _(c):
            off = pl.multiple_of(base + c * rows, 8)
            pltpu.sync_copy(i_hbm.at[pl.ds(off, rows)], idx_v)
            pltpu.sync_copy(y_hbm.at[idx_v], rows_v)
            pltpu.sync_copy(rows_v, o_hbm.at[pl.ds(off, rows)])

    return gather(y, idx)


def _moe_kernel(blk_ref, exp_ref, lo_ref, hi_ref, cast_ref, init_ref, next_ref, xs_ref, wgu_hbm,
                bgu_ref, wd_hbm, bd_ref, ys_ref, wgu_f32, wd_f32, wgu_bf, wd_bf, sems):
    i = pl.program_id(0)

    def weight_copies(e):
        return (pltpu.make_async_copy(wgu_hbm.at[e], wgu_f32, sems.at[0]),
                pltpu.make_async_copy(wd_hbm.at[e], wd_f32, sems.at[1]))

    @pl.when(i == 0)
    def _():
        for cp in weight_copies(exp_ref[0]):
            cp.start()

    @pl.when(cast_ref[i] == 1)
    def _():
        for cp in weight_copies(exp_ref[i]):
            cp.wait()
        wgu_bf[...] = wgu_f32[...].astype(BF16)
        wd_bf[...] = wd_f32[...].astype(BF16)

        @pl.when(next_ref[i] >= 0)
        def _():
            for cp in weight_copies(next_ref[i]):
                cp.start()

    @pl.when(init_ref[i] == 1)
    def _():
        ys_ref[...] = jnp.zeros_like(ys_ref)

    @pl.when(hi_ref[i] > lo_ref[i])
    def _():
        de = wd_bf.shape[0]
        x = _unpack_bf16_pairs(xs_ref[...]).astype(BF16)
        gu = _dot(x, wgu_bf[...]) + bgu_ref[0]
        g = jnp.minimum(gu[:, :de], SWIGLU_LIMIT)
        lin = jnp.clip(gu[:, de:], -SWIGLU_LIMIT, SWIGLU_LIMIT)
        act = g / (1.0 + jnp.exp(-SWIGLU_ALPHA * g)) * (lin + 1.0)
        y = _dot(act.astype(BF16), wd_bf[...]) + bd_ref[0]
        row = lax.broadcasted_iota(jnp.int32, (y.shape[0], 1), 0)
        mine = jnp.logical_and(row >= lo_ref[i], row < hi_ref[i])
        ys_ref[...] = jnp.where(mine, _pack_bf16_pairs(y), ys_ref[...])


def _moe_call(sched, xs, wgu, bgu, wd, bd):
    n_items = sched[0].shape[0]
    _, d, de2 = wgu.shape
    de = de2 // 2
    wmap = lambda i, blk, exp, lo, hi, cast, init, nxt: (exp[i], 0, 0)
    xmap = lambda i, blk, exp, lo, hi, cast, init, nxt: (blk[i], 0)
    return pl.pallas_call(
        _moe_kernel,
        grid_spec=pltpu.PrefetchScalarGridSpec(
            num_scalar_prefetch=7,
            grid=(n_items,),
            in_specs=[
                pl.BlockSpec((MOE_BLOCK, d // 2), xmap),
                pl.BlockSpec(memory_space=pl.ANY),
                pl.BlockSpec((1, 1, de2), wmap),
                pl.BlockSpec(memory_space=pl.ANY),
                pl.BlockSpec((1, 1, d), wmap),
            ],
            out_specs=pl.BlockSpec((MOE_BLOCK, d // 2), xmap),
            scratch_shapes=[
                pltpu.VMEM((d, de2), F32), pltpu.VMEM((de, d), F32),
                pltpu.VMEM((d, de2), BF16), pltpu.VMEM((de, d), BF16),
                pltpu.SemaphoreType.DMA((2,)),
            ],
        ),
        out_shape=jax.ShapeDtypeStruct(xs.shape, xs.dtype),
        compiler_params=pltpu.CompilerParams(
            dimension_semantics=("arbitrary",), vmem_limit_bytes=VMEM_LIMIT),
        name="moe",
    )(*sched, xs, wgu, bgu, wd, bd)


def _moe_schedule(counts, n_rows):
    n_blocks = n_rows // MOE_BLOCK
    n_items = n_blocks + N_EXPERTS
    ends = jnp.cumsum(counts)
    starts = ends - counts
    first_blk = starts // MOE_BLOCK
    last_blk = (ends - 1) // MOE_BLOCK
    items_per = jnp.where(counts > 0, last_blk - first_blk + 1, 0)
    item_ends = jnp.cumsum(items_per)
    item_starts = item_ends - items_per
    total = item_ends[-1]
    it = jnp.arange(n_items, dtype=jnp.int32)
    live = it < total
    itc = jnp.minimum(it, total - 1)
    exp = jnp.sum((item_ends[None, :] <= itc[:, None]).astype(jnp.int32), axis=1)
    is_exp = exp[:, None] == jnp.arange(N_EXPERTS, dtype=jnp.int32)[None, :]
    pick = lambda table: jnp.sum(jnp.where(is_exp, table[None, :], 0), axis=1)
    blk = pick(first_blk) + itc - pick(item_starts)
    lo = jnp.clip(pick(starts) - blk * MOE_BLOCK, 0, MOE_BLOCK)
    hi = jnp.clip(pick(ends) - blk * MOE_BLOCK, 0, MOE_BLOCK)
    hi = jnp.where(live, hi, lo)
    prev_exp = jnp.concatenate([jnp.full((1,), -1, jnp.int32), exp[:-1]])
    prev_blk = jnp.concatenate([jnp.full((1,), -1, jnp.int32), blk[:-1]])
    cast = jnp.logical_and(live, exp != prev_exp)
    init = jnp.logical_and(live, blk != prev_blk)
    ar = jnp.arange(N_EXPERTS, dtype=jnp.int32)
    later = jnp.logical_and(counts[None, :] > 0, ar[None, :] > ar[:, None])
    next_exp = jnp.min(jnp.where(later, ar[None, :], N_EXPERTS), axis=1)
    next_exp = jnp.where(next_exp == N_EXPERTS, -1, next_exp)
    nxt = pick(next_exp)
    as_i32 = lambda a: a.astype(jnp.int32)
    return tuple(as_i32(a) for a in (blk, exp, lo, hi, cast, init, nxt)), starts


def _combine_kernel(*refs):
    yg_refs = refs[:TOP_K]
    x1_ref, gate_ref, mod_ref, fg_ref, o_ref = refs[TOP_K:]
    gates = gate_ref[...]
    y = gates[:, 0:1] * _unpack_bf16_pairs(yg_refs[0][...])
    for kk in range(1, TOP_K):
        y = y + gates[:, kk:kk + 1] * _unpack_bf16_pairs(yg_refs[kk][...])
    m = mod_ref[0]
    x2 = x1_ref[...] + m[5:6] * y
    o_ref[...] = _rms(x2) * fg_ref[...]


def _combine_call(yg, x1, gates, mod, fg, n_seq):
    t, d = x1.shape
    tt = COMBINE_TT
    tiles_per_seq = n_seq // tt
    tok = lambda i: (i, 0)
    n_tiles = t // tt
    slot_specs = [pl.BlockSpec((tt, d // 2), functools.partial(lambda kk, i: (kk * n_tiles + i, 0), kk))
                  for kk in range(TOP_K)]
    return pl.pallas_call(
        _combine_kernel,
        grid=(n_tiles,),
        in_specs=slot_specs + [
            pl.BlockSpec((tt, d), tok),
            pl.BlockSpec((tt, LANES), tok),
            pl.BlockSpec((1, N_MOD, d), lambda i: (i // tiles_per_seq, 0, 0)),
            pl.BlockSpec((1, d), lambda i: (0, 0)),
        ],
        out_specs=pl.BlockSpec((tt, d), tok),
        out_shape=jax.ShapeDtypeStruct((t, d), F32),
        compiler_params=pltpu.CompilerParams(
            dimension_semantics=("arbitrary",), vmem_limit_bytes=VMEM_LIMIT),
        name="combine",
    )(*([yg] * TOP_K), x1, gates, mod, fg)


def _pad_cols(a, width):
    return jnp.pad(a, ((0, 0), (0, width - a.shape[1])))


def _rope_slab(w_rope, swapped):
    half = QK_ROPE // 2
    if swapped:
        w_rope = jnp.concatenate([w_rope[:, half:], w_rope[:, :half]], axis=1)
    zeros = jnp.zeros((w_rope.shape[0], QK_NOPE), w_rope.dtype)
    return _pad_cols(jnp.concatenate([zeros, w_rope], axis=1), LANES)


def _prep_w_in(w_in):
    kr0 = Q_LORA + KV_LORA
    w_kr = w_in[:, kr0:kr0 + QK_ROPE]
    return jnp.concatenate(
        [w_in[:, :kr0], _rope_slab(w_kr, False), _rope_slab(w_kr, True), w_in[:, kr0 + QK_ROPE:]],
        axis=1).astype(BF16)


def _prep_w_uq(w_uq):
    per = QK_NOPE + QK_ROPE
    slabs_a, slabs_b = [], []
    for hd in range(N_HEADS):
        w = w_uq[:, hd * per:(hd + 1) * per]
        slabs_a.append(_pad_cols(w, LANES))
        slabs_b.append(_rope_slab(w[:, QK_NOPE:], True))
    return jnp.concatenate(slabs_a + slabs_b, axis=1).astype(BF16)


def _prep_w_ukv(w_ukv):
    per = QK_NOPE + V_DIM
    ks, vs = [], []
    for hd in range(N_HEADS):
        w = w_ukv[:, hd * per:(hd + 1) * per]
        ks.append(_pad_cols(w[:, :QK_NOPE], LANES))
        lo, _ = _value_lanes(hd)
        vs.append(jnp.pad(w[:, QK_NOPE:], ((0, 0), (lo, LANES - V_DIM - lo))))
    return jnp.concatenate(ks + vs, axis=1).astype(BF16)


def _rope_tables(n_lat):
    rows = n_lat // GRID_W
    nf = QK_ROPE // 4
    row = jnp.repeat(jnp.arange(rows, dtype=F32), GRID_W)
    col = jnp.tile(jnp.arange(GRID_W, dtype=F32), rows)
    freqs = ROPE_BASE ** (-jnp.arange(nf, dtype=F32) / nf)
    ang = jnp.concatenate([row[:, None] * freqs, col[:, None] * freqs], axis=-1)
    cos, sin = jnp.cos(ang), jnp.sin(ang)
    ones = jnp.ones((n_lat, QK_NOPE), F32)
    zeros = jnp.zeros((n_lat, QK_NOPE), F32)
    cs = _pad_cols(jnp.concatenate([ones, cos, cos], axis=1), LANES)
    sn = _pad_cols(jnp.concatenate([zeros, -sin, sin], axis=1), LANES)
    return cs, sn


def kernel(x, c, ctx, c_ctx, w_mod, b_mod, norm1_g, w_in, q_norm_g, kv_norm_g, w_uq, w_ukv, w_pool,
           pool_scale, w_out, norm2_g, router_w, router_b, w_gate_up, b_gate_up, w_down, b_down,
           final_g):
    bsz, n, d = x.shape
    n_ctx = ctx.shape[1]
    t = bsz * n
    l = 0

    cc = jnp.concatenate([c, c_ctx[None, :], jnp.zeros((8 - bsz - 1, d), F32)], axis=0)
    mod = _mod_call(cc, w_mod[l], b_mod[l][None, :]).reshape(8, N_MOD, d)
    mod_lat, mod_ctx = mod[:bsz], mod[bsz:bsz + 1]

    win = _prep_w_in(w_in[l])
    wuq = _prep_w_uq(w_uq[l])
    wukv = _prep_w_ukv(w_ukv[l])
    cs, sn = _rope_tables(n)
    cs_ctx = jnp.broadcast_to((jnp.arange(LANES) < QK_NOPE + QK_ROPE).astype(F32), (n_ctx, LANES))
    sn_ctx = jnp.zeros((n_ctx, LANES), F32)
    g1 = norm1_g[l][None, :]
    qg = q_norm_g[l][None, :]
    kvg = kv_norm_g[l][None, :]

    q, k, v, u = _front_call(x, mod_lat, True, g1, win, qg, kvg, wuq, wukv, cs, sn,
                             is_ctx=False, ts=FRONT_TS)
    kc, vc = _front_call(ctx, mod_ctx, False, g1, win[:, Q_LORA:Q_LORA + 3 * LANES], qg, kvg, wuq,
                         wukv, cs_ctx, sn_ctx, is_ctx=True, ts=n_ctx)
    att = _attn_call(q, k, v, kc, vc)

    rw = _pad_cols(router_w[l], LANES)
    rw_hi = rw.astype(BF16)
    rw = jnp.concatenate([rw_hi, (rw - rw_hi.astype(F32)).astype(BF16)], axis=1)
    rb = jnp.concatenate([router_b[l], jnp.full((LANES - N_EXPERTS,), -jnp.inf, F32)])[None, :]
    x1, h2, gates, ridx, cnt = _mix_call(
        att, u, x, mod_lat, w_pool[l].astype(BF16), pool_scale[l][None, :], w_out[l].astype(BF16),
        norm2_g[l][None, :], rw, rb)

    counts = cnt[0, :N_EXPERTS].astype(jnp.int32)
    sched, starts = _moe_schedule(counts, t * TOP_K)
    ridx = ridx.reshape(t, LANES)
    is_exp = ridx[:, :TOP_K, None] == jnp.arange(N_EXPERTS, dtype=jnp.int32)
    dest = jnp.sum(jnp.where(is_exp, starts, 0), axis=-1) + ridx[:, TOP_K:2 * TOP_K]
    dest_t = dest.T

    xs = _scatter_rows(h2.reshape(t, d // 2), dest_t.reshape(TOP_K, t // SC_ROWS, SC_ROWS), t * TOP_K)
    ys = _moe_call(sched, xs, w_gate_up[l], b_gate_up[l][:, None, :], w_down[l],
                   b_down[l][:, None, :])
    yg = _gather_rows(ys, dest_t.reshape(-1))
    out = _combine_call(yg, x1.reshape(t, d), gates.reshape(t, LANES), mod_lat, final_g[None, :], n)
    return out.reshape(bsz, n, d)
```

```python
import functools
import math

import jax
import jax.numpy as jnp
from jax import lax
from jax.experimental import pallas as pl
from jax.experimental.pallas import tpu as pltpu
from jax.experimental.pallas import tpu_sc as plsc

F32 = jnp.float32
BF16 = jnp.bfloat16
HIGHEST = lax.Precision.HIGHEST

N_HEADS = 8
QK_NOPE = 64
QK_ROPE = 32
V_DIM = 64
Q_LORA = 256
KV_LORA = 128
GRID_W = 64
ROPE_BASE = 10000.0
POOL_WINDOWS = (2, 4, 8, 16)
POOL_CH = 128
N_EXPERTS = 32
TOP_K = 4
SWIGLU_LIMIT = 7.0
SWIGLU_ALPHA = 1.702
MOE_BLOCK = 256
N_MOD = 6
EPS = 1e-6

LANES = 128
POOL_HALO = 8
VMEM_LIMIT = 56 * 1024 * 1024

FRONT_TS = 512
ATTN_TQ = 512
ATTN_TILES = 4
ATTN_KC = 512
MIX_TS = 512
COMBINE_TT = 256
COMBINE_PARTS = 4
SC_ROWS = 128


def _dot(a, b, **kw):
    return jnp.dot(a, b, preferred_element_type=F32, **kw)


def _dot_nt(a, b):
    return lax.dot_general(a, b, (((1,), (1,)), ((), ())), preferred_element_type=F32)


def _rms(x):
    return x * lax.rsqrt(jnp.mean(x * x, axis=-1, keepdims=True) + EPS)


def _pack_bf16_pairs(x):
    w = x.shape[1] // 2
    hi = lax.bitcast_convert_type(x[:, :w].astype(BF16).astype(F32), jnp.uint32)
    lo = lax.bitcast_convert_type(x[:, w:].astype(BF16).astype(F32), jnp.uint32)
    return hi | (lo >> 16)


def _unpack_bf16_pairs(words):
    hi = lax.bitcast_convert_type(words & jnp.uint32(0xFFFF0000), F32)
    lo = lax.bitcast_convert_type(words << 16, F32)
    return jnp.concatenate([hi, lo], axis=-1)


def _value_lanes(hd):
    return (0, V_DIM) if hd % 2 == 0 else (V_DIM, 2 * V_DIM)


def _ones_lane(hd):
    return V_DIM if hd % 2 == 0 else 0


def _mod_kernel(c_ref, w_ref, b_ref, o_ref):
    c = c_ref[...]
    a = c / (1.0 + jnp.exp(-c))
    o_ref[...] = _dot(a, w_ref[...], precision=HIGHEST) + b_ref[...]


def _mod_call(cc, w_mod, b_mod):
    d = w_mod.shape[0]
    return pl.pallas_call(
        _mod_kernel,
        grid=(N_MOD,),
        in_specs=[
            pl.BlockSpec((8, d), lambda j: (0, 0)),
            pl.BlockSpec((d, d), lambda j: (0, j)),
            pl.BlockSpec((1, d), lambda j: (0, j)),
        ],
        out_specs=pl.BlockSpec((8, d), lambda j: (0, j)),
        out_shape=jax.ShapeDtypeStruct((8, N_MOD * d), F32),
        compiler_params=pltpu.CompilerParams(
            dimension_semantics=("arbitrary",), vmem_limit_bytes=VMEM_LIMIT),
        name="mod",
    )(cc, w_mod, b_mod)


def _front_kernel(x_ref, mod_ref, g1_ref, win_ref, qg_ref, kvg_ref, wuq_ref, wukv_ref,
                  cs_ref, sn_ref, *out_refs, is_ctx, scale):
    x = x_ref[0]
    m = mod_ref[0]
    h = _rms(x) * g1_ref[...] * (1.0 + m[1:2]) + m[0:1]
    z = _dot(h.astype(BF16), win_ref[...])
    if is_ctx:
        k_ref, v_ref = out_refs
        zkv = z
    else:
        q_ref, k_ref, v_ref, u_ref = out_refs
        zkv = z[:, Q_LORA:Q_LORA + 3 * LANES]
        u_ref[0] = z[:, Q_LORA + 3 * LANES:]
    cs = cs_ref[...]
    sn = sn_ref[...]

    ckv = _rms(zkv[:, :KV_LORA]) * kvg_ref[...]
    kv = _dot(ckv.astype(BF16), wukv_ref[...])
    kr = zkv[:, LANES:2 * LANES] * cs + zkv[:, 2 * LANES:3 * LANES] * sn
    for hd in range(N_HEADS):
        k_ref[0, hd] = (kv[:, hd * LANES:(hd + 1) * LANES] + kr).astype(BF16)
    voff = N_HEADS * LANES
    lane = lax.broadcasted_iota(jnp.int32, (1, LANES), 1)
    for hd in range(N_HEADS):
        ones = jnp.where(lane == _ones_lane(hd), 1.0, 0.0)
        v_ref[0, hd] = (kv[:, voff + hd * LANES: voff + (hd + 1) * LANES] + ones).astype(BF16)

    if not is_ctx:
        cq = _rms(z[:, :Q_LORA]) * qg_ref[...]
        qq = _dot(cq.astype(BF16), wuq_ref[...])
        boff = N_HEADS * LANES
        for hd in range(N_HEADS):
            qa = qq[:, hd * LANES:(hd + 1) * LANES]
            qb = qq[:, boff + hd * LANES: boff + (hd + 1) * LANES]
            q_ref[0, hd] = ((qa * cs + qb * sn) * scale).astype(BF16)


def _front_call(xs, mod, mod_per_batch, g1, win, qg, kvg, wuq, wukv, cs, sn, *, is_ctx, ts):
    bsz, n, d = xs.shape
    nt = n // ts
    scale = math.log2(math.e) / math.sqrt(QK_NOPE + QK_ROPE)
    const = lambda b, i: (0, 0)
    mod_map = (lambda b, i: (b, 0, 0)) if mod_per_batch else (lambda b, i: (0, 0, 0))
    in_specs = [
        pl.BlockSpec((1, ts, d), lambda b, i: (b, i, 0)),
        pl.BlockSpec((1, N_MOD, d), mod_map),
        pl.BlockSpec((1, d), const),
        pl.BlockSpec(win.shape, const),
        pl.BlockSpec((1, Q_LORA), const),
        pl.BlockSpec((1, KV_LORA), const),
        pl.BlockSpec(wuq.shape, const),
        pl.BlockSpec(wukv.shape, const),
        pl.BlockSpec((ts, LANES), lambda b, i: (i, 0)),
        pl.BlockSpec((ts, LANES), lambda b, i: (i, 0)),
    ]
    k_spec = pl.BlockSpec((1, N_HEADS, ts, LANES), lambda b, i: (b, 0, i, 0))
    k_shape = jax.ShapeDtypeStruct((bsz, N_HEADS, n, LANES), BF16)
    if is_ctx:
        out_specs = [k_spec, k_spec]
        out_shape = [k_shape, k_shape]
    else:
        pool_w = win.shape[1] - Q_LORA - 3 * LANES
        out_specs = [k_spec, k_spec, k_spec, pl.BlockSpec((1, ts, pool_w), lambda b, i: (b, i, 0))]
        out_shape = [k_shape, k_shape, k_shape, jax.ShapeDtypeStruct((bsz, n, pool_w), F32)]
    return pl.pallas_call(
        functools.partial(_front_kernel, is_ctx=is_ctx, scale=scale),
        grid=(bsz, nt),
        in_specs=in_specs,
        out_specs=out_specs,
        out_shape=out_shape,
        compiler_params=pltpu.CompilerParams(
            dimension_semantics=("arbitrary", "arbitrary"), vmem_limit_bytes=VMEM_LIMIT),
        name="front_ctx" if is_ctx else "front",
    )(xs, mod, g1, win, qg, kvg, wuq, wukv, cs, sn)


def _attn_kernel(q_ref, k_ref, v_ref, kc_ref, vc_ref, o_ref, s_ref, mlc_ref, mxb_ref, oe_ref):
    n_ctx = kc_ref.shape[2]
    n_lat = k_ref.shape[2]
    chunks = [(None, 0, n_ctx)] + [(c * ATTN_KC, n_ctx + c * ATTN_KC, ATTN_KC)
                                   for c in range(n_lat // ATTN_KC)]

    def rows(lat_ref, ctx_ref, hd, ci):
        off, _, w = chunks[ci]
        return ctx_ref[0, hd] if off is None else lat_ref[0, hd, off:off + w, :]

    def q_rows(tile):
        if isinstance(tile, int):
            return pl.ds(tile * ATTN_TQ, ATTN_TQ)
        return pl.ds(pl.multiple_of(tile * ATTN_TQ, ATTN_TQ), ATTN_TQ)

    def score_chunk(tile, hd, buf, ci):
        _, soff, w = chunks[ci]
        s = _dot_nt(q_ref[0, hd, q_rows(tile), :], rows(k_ref, kc_ref, hd, ci))
        s_ref[buf, :, soff:soff + w] = s
        ml = s[:, 0:LANES]
        for g in range(1, w // LANES):
            ml = jnp.maximum(ml, s[:, g * LANES:(g + 1) * LANES])
        mlc_ref[buf, ci] = ml

    def row_max(buf):
        ml = mlc_ref[buf, 0]
        for ci in range(1, len(chunks)):
            ml = jnp.maximum(ml, mlc_ref[buf, ci])
        mxb_ref[...] = jnp.broadcast_to(jnp.max(ml, axis=-1, keepdims=True), ml.shape)

    def weight_chunk(hd, buf, ci):
        _, soff, w = chunks[ci]
        mxb = mxb_ref[...]
        ps = [jnp.exp2(s_ref[buf, :, soff + g * LANES:soff + (g + 1) * LANES] - mxb)
              for g in range(w // LANES)]
        return _dot(jnp.concatenate(ps, axis=-1).astype(BF16), rows(v_ref, vc_ref, hd, ci))

    def stage(hw, bw, scoring, bs, parity):
        row_max(bw)
        acc = None
        for ci in range(len(chunks)):
            if scoring is not None:
                score_chunk(*scoring, bs, ci)
            part = weight_chunk(hw, bw, ci)
            acc = part if acc is None else acc + part
        one = _ones_lane(parity)
        return acc / acc[:, one:one + 1]

    def write_pair(tile, j, o_odd):
        lane = lax.broadcasted_iota(jnp.int32, o_odd.shape, 1)
        o_ref[0, j, q_rows(tile), :] = jnp.where(lane < V_DIM, oe_ref[...], o_odd).astype(BF16)

    pairs_per_tile = N_HEADS // 2
    n_pairs = (q_ref.shape[2] // ATTN_TQ) * pairs_per_tile

    def split(p):
        return p // pairs_per_tile, p % pairs_per_tile

    for ci in range(len(chunks)):
        score_chunk(0, 0, 0, ci)

    def head_pair(p, carry):
        tile, j = split(p)
        nxt_tile, nxt_j = split(p + 1)
        oe_ref[...] = stage(2 * j, 0, (tile, 2 * j + 1), 1, 0)
        write_pair(tile, j, stage(2 * j + 1, 1, (nxt_tile, 2 * nxt_j), 0, 1))
        return carry

    lax.fori_loop(0, n_pairs - 1, head_pair, 0)
    tile, j = split(n_pairs - 1)
    oe_ref[...] = stage(2 * j, 0, (tile, 2 * j + 1), 1, 0)
    write_pair(tile, j, stage(2 * j + 1, 1, None, None, 1))


def _attn_call(q, k, v, kc, vc):
    bsz, _, n, _ = q.shape
    n_ctx = kc.shape[2]
    tq = ATTN_TQ
    n_chunks = 1 + n // ATTN_KC
    per_batch = lambda b, i: (b, 0, 0, 0)
    resident = dict(pipeline_mode=pl.Buffered(1))
    rows_per_step = ATTN_TILES * tq
    return pl.pallas_call(
        _attn_kernel,
        grid=(bsz, n // rows_per_step),
        in_specs=[
            pl.BlockSpec((1, N_HEADS, rows_per_step, LANES), lambda b, i: (b, 0, i, 0)),
            pl.BlockSpec((1, N_HEADS, n, LANES), per_batch, **resident),
            pl.BlockSpec((1, N_HEADS, n, LANES), per_batch, **resident),
            pl.BlockSpec((1, N_HEADS, n_ctx, LANES), per_batch, **resident),
            pl.BlockSpec((1, N_HEADS, n_ctx, LANES), per_batch, **resident),
        ],
        out_specs=pl.BlockSpec((1, N_HEADS // 2, rows_per_step, LANES), lambda b, i: (b, 0, i, 0)),
        out_shape=jax.ShapeDtypeStruct((bsz, N_HEADS // 2, n, LANES), BF16),
        scratch_shapes=[
            pltpu.VMEM((2, tq, n_ctx + n), F32),
            pltpu.VMEM((2, n_chunks, tq, LANES), F32),
            pltpu.VMEM((tq, LANES), F32),
            pltpu.VMEM((tq, LANES), F32),
        ],
        compiler_params=pltpu.CompilerParams(
            dimension_semantics=("arbitrary", "arbitrary"), vmem_limit_bytes=VMEM_LIMIT),
        name="attn",
    )(q, k, v, kc, vc)


def _mix_kernel(att_ref, u_ref, up_ref, un_ref, x_ref, mod_ref, wpool_ref, pscale_ref, wout_ref,
                g2_ref, rw_ref, rb_ref,
                x1_ref, h2_ref, gate_ref, ridx_ref, cnt_ref, ue_ref, base_ref, *, n_seq):
    b = pl.program_id(0)
    i = pl.program_id(1)
    n_tiles = pl.num_programs(1)
    ts = u_ref.shape[1]

    @pl.when(jnp.logical_and(b == 0, i == 0))
    def _():
        base_ref[...] = jnp.zeros_like(base_ref)

    u = u_ref[0]
    ue_ref[0:POOL_HALO] = jnp.where(i > 0, up_ref[0], 0.0)
    ue_ref[POOL_HALO:POOL_HALO + ts] = u
    ue_ref[POOL_HALO + ts:2 * POOL_HALO + ts] = jnp.where(i < n_tiles - 1, un_ref[0], 0.0)
    t = i * ts + lax.broadcasted_iota(jnp.int32, (ts, 1), 0)
    ys = []
    for g, w in enumerate(POOL_WINDOWS):
        half = w // 2
        lanes = slice(g * POOL_CH, (g + 1) * POOL_CH)
        ws = ue_ref[POOL_HALO - half:POOL_HALO - half + ts, lanes]
        for jj in range(-half + 1, half):
            ws = ws + ue_ref[POOL_HALO + jj:POOL_HALO + jj + ts, lanes]
        count = (jnp.minimum(t + half, n_seq) - jnp.maximum(t - half, 0)).astype(F32)
        mixed = (ws / count - u[:, lanes]).astype(BF16)
        ys.append(_dot(mixed, wpool_ref[g]))
    pool = jnp.concatenate(ys, axis=-1) * pscale_ref[...]

    cat = jnp.concatenate([att_ref[0, j] for j in range(N_HEADS // 2)] + [pool.astype(BF16)], axis=-1)
    m = mod_ref[0]
    x1 = x_ref[0] + m[2:3] * _dot(cat, wout_ref[...])
    x1_ref[0] = x1
    h2 = _rms(x1) * g2_ref[...] * (1.0 + m[4:5]) + m[3:4]
    h2_ref[0] = _pack_bf16_pairs(h2)

    h_hi = h2.astype(BF16)
    h_lo = (h2 - h_hi.astype(F32)).astype(BF16)
    hi_part = _dot(h_hi, rw_ref[...])
    logits = (hi_part[:, :LANES] + hi_part[:, LANES:] + _dot(h_lo, rw_ref[:, :LANES])) + rb_ref[...]
    lane = lax.broadcasted_iota(jnp.int32, logits.shape, 1).astype(F32)
    vals, idxs = [], []
    for _k in range(TOP_K):
        mv = jnp.max(logits, axis=-1, keepdims=True)
        ix = jnp.min(jnp.where(logits == mv, lane, float(LANES)), axis=-1, keepdims=True)
        vals.append(mv)
        idxs.append(ix)
        logits = jnp.where(lane == ix, -jnp.inf, logits)
    es = [jnp.exp(v - vals[0]) for v in vals]
    den = es[0] + es[1] + es[2] + es[3]

    onehot = jnp.zeros(lane.shape, F32)
    for ix in idxs:
        onehot = onehot + jnp.where(lane == ix, 1.0, 0.0)
    row = lax.broadcasted_iota(jnp.int32, (ts, ts), 0)
    col = lax.broadcasted_iota(jnp.int32, (ts, ts), 1)
    tri = jnp.where(col < row, 1.0, 0.0).astype(BF16)
    before = _dot(tri, onehot.astype(BF16)) + base_ref[0:1, :]
    base_new = base_ref[0:1, :] + jnp.sum(onehot, axis=0, keepdims=True)
    base_ref[...] = jnp.broadcast_to(base_new, base_ref.shape)
    cnt_ref[...] = jnp.broadcast_to(base_new, cnt_ref.shape)

    gate_out = jnp.zeros(lane.shape, F32)
    ridx_out = jnp.zeros(lane.shape, F32)
    for kk in range(TOP_K):
        rank = jnp.sum(jnp.where(lane == idxs[kk], before, 0.0), axis=-1, keepdims=True)
        gate_out = jnp.where(lane == float(kk), es[kk] / den, gate_out)
        ridx_out = jnp.where(lane == float(kk), idxs[kk], ridx_out)
        ridx_out = jnp.where(lane == float(TOP_K + kk), rank, ridx_out)
    gate_ref[0] = gate_out
    ridx_ref[0] = ridx_out.astype(jnp.int32)


def _mix_call(att, u, x, mod, wpool, pscale, wout, g2, rw, rb):
    bsz, n, d = x.shape
    ts = MIX_TS
    nt = n // ts
    pool_w = u.shape[2]
    hb = ts // POOL_HALO
    n_halo_blocks = n // POOL_HALO
    const2 = lambda b, i: (0, 0)
    tok = lambda b, i: (b, i, 0)
    return pl.pallas_call(
        functools.partial(_mix_kernel, n_seq=n),
        grid=(bsz, nt),
        in_specs=[
            pl.BlockSpec((1, N_HEADS // 2, ts, LANES), lambda b, i: (b, 0, i, 0)),
            pl.BlockSpec((1, ts, pool_w), tok),
            pl.BlockSpec((1, POOL_HALO, pool_w), lambda b, i: (b, jnp.maximum(i * hb - 1, 0), 0)),
            pl.BlockSpec((1, POOL_HALO, pool_w),
                         lambda b, i: (b, jnp.minimum((i + 1) * hb, n_halo_blocks - 1), 0)),
            pl.BlockSpec((1, ts, d), tok),
            pl.BlockSpec((1, N_MOD, d), lambda b, i: (b, 0, 0)),
            pl.BlockSpec(wpool.shape, lambda b, i: (0, 0, 0)),
            pl.BlockSpec((1, pool_w), const2),
            pl.BlockSpec(wout.shape, const2),
            pl.BlockSpec((1, d), const2),
            pl.BlockSpec(rw.shape, const2),
            pl.BlockSpec((1, LANES), const2),
        ],
        out_specs=[
            pl.BlockSpec((1, ts, d), tok),
            pl.BlockSpec((1, ts, d // 2), tok),
            pl.BlockSpec((1, ts, LANES), tok),
            pl.BlockSpec((1, ts, LANES), tok),
            pl.BlockSpec((8, LANES), const2),
        ],
        out_shape=[
            jax.ShapeDtypeStruct((bsz, n, d), F32),
            jax.ShapeDtypeStruct((bsz, n, d // 2), jnp.uint32),
            jax.ShapeDtypeStruct((bsz, n, LANES), F32),
            jax.ShapeDtypeStruct((bsz, n, LANES), jnp.int32),
            jax.ShapeDtypeStruct((8, LANES), F32),
        ],
        scratch_shapes=[
            pltpu.VMEM((ts + 2 * POOL_HALO, pool_w), F32),
            pltpu.VMEM((8, LANES), F32),
        ],
        compiler_params=pltpu.CompilerParams(
            dimension_semantics=("arbitrary", "arbitrary"), vmem_limit_bytes=VMEM_LIMIT),
        name="mix",
    )(att, u, u, u, x, mod, wpool, pscale, wout, g2, rw, rb)


def _sc_mesh():
    return plsc.VectorSubcoreMesh(core_axis_name="core", subcore_axis_name="subcore")


def _sc_worker_id():
    info = plsc.get_sparse_core_info()
    return lax.axis_index("subcore") * info.num_cores + lax.axis_index("core")


def _sc_num_workers():
    info = plsc.get_sparse_core_info()
    return info.num_cores * info.num_subcores


def _scatter_rows(x, dest3, n_out):
    t, d = x.shape
    top_k, n_chunks, rows = dest3.shape
    per_worker = n_chunks // _sc_num_workers()

    @functools.partial(
        pl.kernel,
        out_type=jax.ShapeDtypeStruct((n_out, d), x.dtype),
        mesh=_sc_mesh(),
        scratch_types=[pltpu.VMEM((top_k, rows), jnp.int32), pltpu.VMEM((rows, d), x.dtype)],
        name="sc_scatter",
    )
    def scatter(x_hbm, i_hbm, o_hbm, idx_v, rows_v):
        wid = _sc_worker_id()

        @pl.loop(0, per_worker)
        def _(c):
            chunk = wid * per_worker + c
            for kk in range(top_k):
                pltpu.sync_copy(i_hbm.at[kk, chunk], idx_v.at[kk])
            pltpu.sync_copy(x_hbm.at[pl.ds(pl.multiple_of(chunk * rows, 8), rows)], rows_v)
            for kk in range(top_k):
                pltpu.sync_copy(rows_v, o_hbm.at[idx_v.at[kk]])

    return scatter(x, dest3)


def _gather_rows(y, idx):
    n = idx.shape[0]
    d = y.shape[1]
    rows = SC_ROWS
    per_worker = n // _sc_num_workers()
    n_chunks = per_worker // rows

    @functools.partial(
        pl.kernel,
        out_type=jax.ShapeDtypeStruct((n, d), y.dtype),
        mesh=_sc_mesh(),
        scratch_types=[pltpu.VMEM((rows,), jnp.int32), pltpu.VMEM((rows, d), y.dtype)],
        name="sc_gather",
    )
    def gather(y_hbm, i_hbm, o_hbm, idx_v, rows_v):
        base = _sc_worker_id() * per_worker

        @pl.loop(0, n_chunks)
        def _(c):
            off = pl.multiple_of(base + c * rows, 8)
            pltpu.sync_copy(i_hbm.at[pl.ds(off, rows)], idx_v)
            pltpu.sync_copy(y_hbm.at[idx_v], rows_v)
            pltpu.sync_copy(rows_v, o_hbm.at[pl.ds(off, rows)])

    return gather(y, idx)


def _moe_kernel(blk_ref, exp_ref, lo_ref, hi_ref, cast_ref, init_ref, next_ref, xs_ref, wgu_hbm,
                bgu_ref, wd_hbm, bd_ref, ys_ref, wgu_f32, wd_f32, wgu_bf, wd_bf, sems):
    i = pl.program_id(0)

    def weight_copies(e):
        return (pltpu.make_async_copy(wgu_hbm.at[e], wgu_f32, sems.at[0]),
                pltpu.make_async_copy(wd_hbm.at[e], wd_f32, sems.at[1]))

    @pl.when(i == 0)
    def _():
        for cp in weight_copies(exp_ref[0]):
            cp.start()

    @pl.when(cast_ref[i] == 1)
    def _():
        for cp in weight_copies(exp_ref[i]):
            cp.wait()
        wgu_bf[...] = wgu_f32[...].astype(BF16)
        wd_bf[...] = wd_f32[...].astype(BF16)

        @pl.when(next_ref[i] >= 0)
        def _():
            for cp in weight_copies(next_ref[i]):
                cp.start()

    @pl.when(init_ref[i] == 1)
    def _():
        ys_ref[...] = jnp.zeros_like(ys_ref)

    @pl.when(hi_ref[i] > lo_ref[i])
    def _():
        de = wd_bf.shape[0]
        x = _unpack_bf16_pairs(xs_ref[...]).astype(BF16)
        gu = _dot(x, wgu_bf[...]) + bgu_ref[0]
        g = jnp.minimum(gu[:, :de], SWIGLU_LIMIT)
        lin = jnp.clip(gu[:, de:], -SWIGLU_LIMIT, SWIGLU_LIMIT)
        act = g / (1.0 + jnp.exp(-SWIGLU_ALPHA * g)) * (lin + 1.0)
        y = _dot(act.astype(BF16), wd_bf[...]) + bd_ref[0]
        row = lax.broadcasted_iota(jnp.int32, (y.shape[0], 1), 0)
        mine = jnp.logical_and(row >= lo_ref[i], row < hi_ref[i])
        ys_ref[...] = jnp.where(mine, _pack_bf16_pairs(y), ys_ref[...])


def _moe_call(sched, xs, wgu, bgu, wd, bd):
    n_items = sched[0].shape[0]
    _, d, de2 = wgu.shape
    de = de2 // 2
    wmap = lambda i, blk, exp, lo, hi, cast, init, nxt: (exp[i], 0, 0)
    xmap = lambda i, blk, exp, lo, hi, cast, init, nxt: (blk[i], 0)
    return pl.pallas_call(
        _moe_kernel,
        grid_spec=pltpu.PrefetchScalarGridSpec(
            num_scalar_prefetch=7,
            grid=(n_items,),
            in_specs=[
                pl.BlockSpec((MOE_BLOCK, d // 2), xmap),
                pl.BlockSpec(memory_space=pl.ANY),
                pl.BlockSpec((1, 1, de2), wmap),
                pl.BlockSpec(memory_space=pl.ANY),
                pl.BlockSpec((1, 1, d), wmap),
            ],
            out_specs=pl.BlockSpec((MOE_BLOCK, d // 2), xmap),
            scratch_shapes=[
                pltpu.VMEM((d, de2), F32), pltpu.VMEM((de, d), F32),
                pltpu.VMEM((d, de2), BF16), pltpu.VMEM((de, d), BF16),
                pltpu.SemaphoreType.DMA((2,)),
            ],
        ),
        out_shape=jax.ShapeDtypeStruct(xs.shape, xs.dtype),
        compiler_params=pltpu.CompilerParams(
            dimension_semantics=("arbitrary",), vmem_limit_bytes=VMEM_LIMIT),
        name="moe",
    )(*sched, xs, wgu, bgu, wd, bd)


def _moe_schedule(counts, n_rows):
    n_blocks = n_rows // MOE_BLOCK
    n_items = n_blocks + N_EXPERTS
    ends = jnp.cumsum(counts)
    starts = ends - counts
    first_blk = starts // MOE_BLOCK
    last_blk = (ends - 1) // MOE_BLOCK
    items_per = jnp.where(counts > 0, last_blk - first_blk + 1, 0)
    item_ends = jnp.cumsum(items_per)
    item_starts = item_ends - items_per
    total = item_ends[-1]
    it = jnp.arange(n_items, dtype=jnp.int32)
    live = it < total
    itc = jnp.minimum(it, total - 1)
    exp = jnp.sum((item_ends[None, :] <= itc[:, None]).astype(jnp.int32), axis=1)
    is_exp = exp[:, None] == jnp.arange(N_EXPERTS, dtype=jnp.int32)[None, :]
    pick = lambda table: jnp.sum(jnp.where(is_exp, table[None, :], 0), axis=1)
    blk = pick(first_blk) + itc - pick(item_starts)
    lo = jnp.clip(pick(starts) - blk * MOE_BLOCK, 0, MOE_BLOCK)
    hi = jnp.clip(pick(ends) - blk * MOE_BLOCK, 0, MOE_BLOCK)
    hi = jnp.where(live, hi, lo)
    prev_exp = jnp.concatenate([jnp.full((1,), -1, jnp.int32), exp[:-1]])
    prev_blk = jnp.concatenate([jnp.full((1,), -1, jnp.int32), blk[:-1]])
    cast = jnp.logical_and(live, exp != prev_exp)
    init = jnp.logical_and(live, blk != prev_blk)
    ar = jnp.arange(N_EXPERTS, dtype=jnp.int32)
    later = jnp.logical_and(counts[None, :] > 0, ar[None, :] > ar[:, None])
    next_exp = jnp.min(jnp.where(later, ar[None, :], N_EXPERTS), axis=1)
    next_exp = jnp.where(next_exp == N_EXPERTS, -1, next_exp)
    nxt = pick(next_exp)
    as_i32 = lambda a: a.astype(jnp.int32)
    return tuple(as_i32(a) for a in (blk, exp, lo, hi, cast, init, nxt)), starts


def _combine_kernel(*refs):
    yg_refs = refs[:TOP_K]
    x1_ref, gate_ref, mod_ref, fg_ref, o_ref = refs[TOP_K:]
    gates = gate_ref[...]
    y = gates[:, 0:1] * _unpack_bf16_pairs(yg_refs[0][...])
    for kk in range(1, TOP_K):
        y = y + gates[:, kk:kk + 1] * _unpack_bf16_pairs(yg_refs[kk][...])
    m = mod_ref[0]
    x2 = x1_ref[...] + m[5:6] * y
    o_ref[...] = _rms(x2) * fg_ref[...]


def _combine_call(yg, x1, gates, mod, fg, n_seq, part):
    t, d = x1.shape
    tt = COMBINE_TT
    tiles_per_seq = n_seq // tt
    n_tiles = yg.shape[0] // (TOP_K * tt)
    first = part * n_tiles
    tok = lambda i: (first + i, 0)
    slot_specs = [pl.BlockSpec((tt, d // 2), functools.partial(lambda kk, i: (kk * n_tiles + i, 0), kk))
                  for kk in range(TOP_K)]
    return pl.pallas_call(
        _combine_kernel,
        grid=(n_tiles,),
        in_specs=slot_specs + [
            pl.BlockSpec((tt, d), tok),
            pl.BlockSpec((tt, LANES), tok),
            pl.BlockSpec((1, N_MOD, d), lambda i: ((first + i) // tiles_per_seq, 0, 0)),
            pl.BlockSpec((1, d), lambda i: (0, 0)),
        ],
        out_specs=pl.BlockSpec((tt, d), tok),
        out_shape=jax.ShapeDtypeStruct((t, d), F32),
        input_output_aliases={TOP_K: 0},
        compiler_params=pltpu.CompilerParams(
            dimension_semantics=("arbitrary",), vmem_limit_bytes=VMEM_LIMIT),
        name="combine",
    )(*([yg] * TOP_K), x1, gates, mod, fg)


def _pad_cols(a, width):
    return jnp.pad(a, ((0, 0), (0, width - a.shape[1])))


def _rope_slab(w_rope, swapped):
    half = QK_ROPE // 2
    if swapped:
        w_rope = jnp.concatenate([w_rope[:, half:], w_rope[:, :half]], axis=1)
    zeros = jnp.zeros((w_rope.shape[0], QK_NOPE), w_rope.dtype)
    return _pad_cols(jnp.concatenate([zeros, w_rope], axis=1), LANES)


def _prep_w_in(w_in):
    kr0 = Q_LORA + KV_LORA
    w_kr = w_in[:, kr0:kr0 + QK_ROPE]
    return jnp.concatenate(
        [w_in[:, :kr0], _rope_slab(w_kr, False), _rope_slab(w_kr, True), w_in[:, kr0 + QK_ROPE:]],
        axis=1).astype(BF16)


def _prep_w_uq(w_uq):
    per = QK_NOPE + QK_ROPE
    half = QK_ROPE // 2
    rows = w_uq.shape[0]
    w = w_uq.reshape(rows, N_HEADS, per)
    slab_a = jnp.pad(w, ((0, 0), (0, 0), (0, LANES - per)))
    rope = w[:, :, QK_NOPE:]
    swapped = jnp.concatenate([rope[:, :, half:], rope[:, :, :half]], axis=2)
    slab_b = jnp.pad(swapped, ((0, 0), (0, 0), (QK_NOPE, LANES - per)))
    return jnp.concatenate([slab_a.reshape(rows, -1), slab_b.reshape(rows, -1)], axis=1).astype(BF16)


def _prep_w_ukv(w_ukv):
    per = QK_NOPE + V_DIM
    rows = w_ukv.shape[0]
    w = w_ukv.reshape(rows, N_HEADS, per)
    ks = jnp.pad(w[:, :, :QK_NOPE], ((0, 0), (0, 0), (0, LANES - QK_NOPE)))
    v = w[:, :, QK_NOPE:].reshape(rows, N_HEADS // 2, 2, V_DIM)
    even = jnp.pad(v[:, :, 0], ((0, 0), (0, 0), (_value_lanes(0)[0], LANES - _value_lanes(0)[1])))
    odd = jnp.pad(v[:, :, 1], ((0, 0), (0, 0), (_value_lanes(1)[0], LANES - _value_lanes(1)[1])))
    vs = jnp.stack([even, odd], axis=2)
    return jnp.concatenate([ks.reshape(rows, -1), vs.reshape(rows, -1)], axis=1).astype(BF16)


def _rope_tables(n_lat):
    rows = n_lat // GRID_W
    nf = QK_ROPE // 4
    row = jnp.repeat(jnp.arange(rows, dtype=F32), GRID_W)
    col = jnp.tile(jnp.arange(GRID_W, dtype=F32), rows)
    freqs = ROPE_BASE ** (-jnp.arange(nf, dtype=F32) / nf)
    ang = jnp.concatenate([row[:, None] * freqs, col[:, None] * freqs], axis=-1)
    cos, sin = jnp.cos(ang), jnp.sin(ang)
    ones = jnp.ones((n_lat, QK_NOPE), F32)
    zeros = jnp.zeros((n_lat, QK_NOPE), F32)
    cs = _pad_cols(jnp.concatenate([ones, cos, cos], axis=1), LANES)
    sn = _pad_cols(jnp.concatenate([zeros, -sin, sin], axis=1), LANES)
    return cs, sn


def kernel(x, c, ctx, c_ctx, w_mod, b_mod, norm1_g, w_in, q_norm_g, kv_norm_g, w_uq, w_ukv, w_pool,
           pool_scale, w_out, norm2_g, router_w, router_b, w_gate_up, b_gate_up, w_down, b_down,
           final_g):
    bsz, n, d = x.shape
    n_ctx = ctx.shape[1]
    t = bsz * n
    l = 0

    cc = jnp.concatenate([c, c_ctx[None, :], jnp.zeros((8 - bsz - 1, d), F32)], axis=0)
    mod = _mod_call(cc, w_mod[l], b_mod[l][None, :]).reshape(8, N_MOD, d)
    mod_lat, mod_ctx = mod[:bsz], mod[bsz:bsz + 1]

    win = _prep_w_in(w_in[l])
    wuq = _prep_w_uq(w_uq[l])
    wukv = _prep_w_ukv(w_ukv[l])
    cs, sn = _rope_tables(n)
    cs_ctx = jnp.broadcast_to((jnp.arange(LANES) < QK_NOPE + QK_ROPE).astype(F32), (n_ctx, LANES))
    sn_ctx = jnp.zeros((n_ctx, LANES), F32)
    g1 = norm1_g[l][None, :]
    qg = q_norm_g[l][None, :]
    kvg = kv_norm_g[l][None, :]

    q, k, v, u = _front_call(x, mod_lat, True, g1, win, qg, kvg, wuq, wukv, cs, sn,
                             is_ctx=False, ts=FRONT_TS)
    kc, vc = _front_call(ctx, mod_ctx, False, g1, win[:, Q_LORA:Q_LORA + 3 * LANES], qg, kvg, wuq,
                         wukv, cs_ctx, sn_ctx, is_ctx=True, ts=n_ctx)
    att = _attn_call(q, k, v, kc, vc)

    rw = _pad_cols(router_w[l], LANES)
    rw_hi = rw.astype(BF16)
    rw = jnp.concatenate([rw_hi, (rw - rw_hi.astype(F32)).astype(BF16)], axis=1)
    rb = jnp.concatenate([router_b[l], jnp.full((LANES - N_EXPERTS,), -jnp.inf, F32)])[None, :]
    x1, h2, gates, ridx, cnt = _mix_call(
        att, u, x, mod_lat, w_pool[l].astype(BF16), pool_scale[l][None, :], w_out[l].astype(BF16),
        norm2_g[l][None, :], rw, rb)

    counts = cnt[0, :N_EXPERTS].astype(jnp.int32)
    sched, starts = _moe_schedule(counts, t * TOP_K)
    ridx = ridx.reshape(t, LANES)
    is_exp = ridx[:, :TOP_K, None] == jnp.arange(N_EXPERTS, dtype=jnp.int32)
    dest = jnp.sum(jnp.where(is_exp, starts, 0), axis=-1) + ridx[:, TOP_K:2 * TOP_K]
    dest_t = dest.T

    xs = _scatter_rows(h2.reshape(t, d // 2), dest_t.reshape(TOP_K, t // SC_ROWS, SC_ROWS), t * TOP_K)
    ys = _moe_call(sched, xs, w_gate_up[l], b_gate_up[l][:, None, :], w_down[l],
                   b_down[l][:, None, :])
    out = x1.reshape(t, d)
    gates = gates.reshape(t, LANES)
    tp = t // COMBINE_PARTS
    for part in range(COMBINE_PARTS):
        idx = dest_t[:, part * tp:(part + 1) * tp].reshape(-1)
        yg = _gather_rows(ys, idx)
        out = _combine_call(yg, out, gates, mod_lat, final_g[None, :], n, part)
    return out.reshape(bsz, n, d)
```

```python
import functools
import math

import jax
import jax.numpy as jnp
from jax import lax
from jax.experimental import pallas as pl
from jax.experimental.pallas import tpu as pltpu
from jax.experimental.pallas import tpu_sc as plsc

F32 = jnp.float32
BF16 = jnp.bfloat16
HIGHEST = lax.Precision.HIGHEST

N_HEADS = 8
QK_NOPE = 64
QK_ROPE = 32
V_DIM = 64
Q_LORA = 256
KV_LORA = 128
GRID_W = 64
ROPE_BASE = 10000.0
POOL_WINDOWS = (2, 4, 8, 16)
POOL_CH = 128
N_EXPERTS = 32
TOP_K = 4
SWIGLU_LIMIT = 7.0
SWIGLU_ALPHA = 1.702
MOE_BLOCK = 256
N_MOD = 6
EPS = 1e-6

LANES = 128
POOL_HALO = 8
VT_ROWS = 80
VMEM_LIMIT = 56 * 1024 * 1024

FRONT_TS = 512
ATTN_TQ = 512
ATTN_TILES = 4
ATTN_KC = 512
MIX_TS = 512
COMBINE_TT = 256
COMBINE_PARTS = 4
SC_ROWS = 128


def _dot(a, b, **kw):
    return jnp.dot(a, b, preferred_element_type=F32, **kw)


def _dot_nt(a, b):
    return lax.dot_general(a, b, (((1,), (1,)), ((), ())), preferred_element_type=F32)


def _rms(x):
    return x * lax.rsqrt(jnp.mean(x * x, axis=-1, keepdims=True) + EPS)


def _pack_bf16_pairs(x):
    w = x.shape[1] // 2
    hi = lax.bitcast_convert_type(x[:, :w].astype(BF16).astype(F32), jnp.uint32)
    lo = lax.bitcast_convert_type(x[:, w:].astype(BF16).astype(F32), jnp.uint32)
    return hi | (lo >> 16)


def _unpack_bf16_pairs(words):
    hi = lax.bitcast_convert_type(words & jnp.uint32(0xFFFF0000), F32)
    lo = lax.bitcast_convert_type(words << 16, F32)
    return jnp.concatenate([hi, lo], axis=-1)


def _mod_kernel(c_ref, w_ref, b_ref, o_ref):
    c = c_ref[...]
    a = c / (1.0 + jnp.exp(-c))
    o_ref[...] = _dot(a, w_ref[...], precision=HIGHEST) + b_ref[...]


def _mod_call(cc, w_mod, b_mod):
    d = w_mod.shape[0]
    return pl.pallas_call(
        _mod_kernel,
        grid=(N_MOD,),
        in_specs=[
            pl.BlockSpec((8, d), lambda j: (0, 0)),
            pl.BlockSpec((d, d), lambda j: (0, j)),
            pl.BlockSpec((1, d), lambda j: (0, j)),
        ],
        out_specs=pl.BlockSpec((8, d), lambda j: (0, j)),
        out_shape=jax.ShapeDtypeStruct((8, N_MOD * d), F32),
        compiler_params=pltpu.CompilerParams(
            dimension_semantics=("arbitrary",), vmem_limit_bytes=VMEM_LIMIT),
        name="mod",
    )(cc, w_mod, b_mod)


def _front_kernel(x_ref, mod_ref, g1_ref, win_ref, qg_ref, kvg_ref, wuqt_ref, wk_ref, wvt_ref,
                  cs_ref, sn_ref, cst_ref, snt_ref, *out_refs, is_ctx, scale):
    x = x_ref[0]
    m = mod_ref[0]
    h = _rms(x) * g1_ref[...] * (1.0 + m[1:2]) + m[0:1]
    z = _dot(h.astype(BF16), win_ref[...])
    if is_ctx:
        k_ref, vt_ref = out_refs
        zkv = z
    else:
        qt_ref, k_ref, vt_ref, u_ref = out_refs
        zkv = z[:, Q_LORA:Q_LORA + 3 * LANES]
        u_ref[0] = z[:, Q_LORA + 3 * LANES:]

    ckv = _rms(zkv[:, :KV_LORA]) * kvg_ref[...]
    kk = _dot(ckv.astype(BF16), wk_ref[...])
    kr = zkv[:, LANES:2 * LANES] * cs_ref[...] + zkv[:, 2 * LANES:3 * LANES] * sn_ref[...]
    for hd in range(N_HEADS):
        k_ref[0, hd] = (kk[:, hd * LANES:(hd + 1) * LANES] + kr).astype(BF16)
    vt = _dot(wvt_ref[...], ckv.T.astype(BF16))
    row = lax.broadcasted_iota(jnp.int32, (VT_ROWS, 1), 0)
    ones = jnp.where(row == V_DIM, 1.0, 0.0)
    for hd in range(N_HEADS):
        vt_ref[0, hd] = (vt[hd * VT_ROWS:(hd + 1) * VT_ROWS] + ones).astype(BF16)

    if not is_ctx:
        cq = _rms(z[:, :Q_LORA]) * qg_ref[...]
        qqt = _dot(wuqt_ref[...], cq.T.astype(BF16))
        cst = cst_ref[...]
        snt = snt_ref[...]
        boff = N_HEADS * LANES
        for hd in range(N_HEADS):
            qa = qqt[hd * LANES:(hd + 1) * LANES]
            qb = qqt[boff + hd * LANES: boff + (hd + 1) * LANES]
            qt_ref[0, hd] = ((qa * cst + qb * snt) * scale).astype(BF16)


def _front_call(xs, mod, mod_per_batch, g1, win, qg, kvg, wuqt, wk, wvt, cs, sn, *, is_ctx, ts):
    bsz, n, d = xs.shape
    nt = n // ts
    scale = math.log2(math.e) / math.sqrt(QK_NOPE + QK_ROPE)
    const = lambda b, i: (0, 0)
    mod_map = (lambda b, i: (b, 0, 0)) if mod_per_batch else (lambda b, i: (0, 0, 0))
    in_specs = [
        pl.BlockSpec((1, ts, d), lambda b, i: (b, i, 0)),
        pl.BlockSpec((1, N_MOD, d), mod_map),
        pl.BlockSpec((1, d), const),
        pl.BlockSpec(win.shape, const),
        pl.BlockSpec((1, Q_LORA), const),
        pl.BlockSpec((1, KV_LORA), const),
        pl.BlockSpec(wuqt.shape, const),
        pl.BlockSpec(wk.shape, const),
        pl.BlockSpec(wvt.shape, const),
        pl.BlockSpec((ts, LANES), lambda b, i: (i, 0)),
        pl.BlockSpec((ts, LANES), lambda b, i: (i, 0)),
        pl.BlockSpec((LANES, ts), lambda b, i: (0, i)),
        pl.BlockSpec((LANES, ts), lambda b, i: (0, i)),
    ]
    k_spec = pl.BlockSpec((1, N_HEADS, ts, LANES), lambda b, i: (b, 0, i, 0))
    k_shape = jax.ShapeDtypeStruct((bsz, N_HEADS, n, LANES), BF16)
    qt_spec = pl.BlockSpec((1, N_HEADS, LANES, ts), lambda b, i: (b, 0, 0, i))
    qt_shape = jax.ShapeDtypeStruct((bsz, N_HEADS, LANES, n), BF16)
    vt_spec = pl.BlockSpec((1, N_HEADS, VT_ROWS, ts), lambda b, i: (b, 0, 0, i))
    vt_shape = jax.ShapeDtypeStruct((bsz, N_HEADS, VT_ROWS, n), BF16)
    if is_ctx:
        out_specs = [k_spec, vt_spec]
        out_shape = [k_shape, vt_shape]
    else:
        pool_w = win.shape[1] - Q_LORA - 3 * LANES
        out_specs = [qt_spec, k_spec, vt_spec, pl.BlockSpec((1, ts, pool_w), lambda b, i: (b, i, 0))]
        out_shape = [qt_shape, k_shape, vt_shape, jax.ShapeDtypeStruct((bsz, n, pool_w), F32)]
    return pl.pallas_call(
        functools.partial(_front_kernel, is_ctx=is_ctx, scale=scale),
        grid=(bsz, nt),
        in_specs=in_specs,
        out_specs=out_specs,
        out_shape=out_shape,
        compiler_params=pltpu.CompilerParams(
            dimension_semantics=("arbitrary", "arbitrary"), vmem_limit_bytes=VMEM_LIMIT),
        name="front_ctx" if is_ctx else "front",
    )(xs, mod, g1, win, qg, kvg, wuqt, wk, wvt, cs, sn, cs.T, sn.T)


def _attn_kernel(qt_ref, k_ref, vt_ref, kc_ref, vct_ref, o_ref, s_ref, mlc_ref, mxb_ref, oe_ref):
    n_ctx = kc_ref.shape[2]
    n_lat = k_ref.shape[2]
    chunks = [(None, 0, n_ctx)] + [(c * ATTN_KC, n_ctx + c * ATTN_KC, ATTN_KC)
                                   for c in range(n_lat // ATTN_KC)]
    sub = 8

    def q_cols(tile):
        if isinstance(tile, int):
            return pl.ds(tile * ATTN_TQ, ATTN_TQ)
        return pl.ds(pl.multiple_of(tile * ATTN_TQ, ATTN_TQ), ATTN_TQ)

    def score_chunk(tile, hd, buf, ci):
        off, soff, w = chunks[ci]
        keys = kc_ref[0, hd] if off is None else k_ref[0, hd, off:off + w, :]
        s = _dot(keys, qt_ref[0, hd, :, q_cols(tile)])
        s_ref[buf, soff:soff + w, :] = s
        mx = s[0:sub]
        for r in range(1, w // sub):
            mx = jnp.maximum(mx, s[r * sub:(r + 1) * sub])
        mlc_ref[buf, ci] = mx

    def row_max(buf):
        mx = mlc_ref[buf, 0]
        for ci in range(1, len(chunks)):
            mx = jnp.maximum(mx, mlc_ref[buf, ci])
        mxb_ref[...] = jnp.broadcast_to(jnp.max(mx, axis=0, keepdims=True), mx.shape)

    def weight_chunk(hd, buf, ci):
        off, soff, w = chunks[ci]
        p = jnp.exp2(s_ref[buf, soff:soff + w, :] - mxb_ref[0:1, :]).astype(BF16)
        vt = vct_ref[0, hd] if off is None else vt_ref[0, hd, :, off:off + w]
        return _dot(vt, p)

    def stage(hw, bw, scoring, bs):
        row_max(bw)
        acc = None
        for ci in range(len(chunks)):
            if scoring is not None:
                score_chunk(*scoring, bs, ci)
            part = weight_chunk(hw, bw, ci)
            acc = part if acc is None else acc + part
        return acc[0:V_DIM] / acc[V_DIM:V_DIM + 1]

    def write_pair(tile, j, ot_odd):
        pair_t = jnp.concatenate([oe_ref[...], ot_odd], axis=0)
        o_ref[0, j, q_cols(tile), :] = pair_t.T.astype(BF16)

    pairs_per_tile = N_HEADS // 2
    n_pairs = (qt_ref.shape[3] // ATTN_TQ) * pairs_per_tile

    def split(p):
        return p // pairs_per_tile, p % pairs_per_tile

    for ci in range(len(chunks)):
        score_chunk(0, 0, 0, ci)

    def head_pair(p, carry):
        tile, j = split(p)
        nxt_tile, nxt_j = split(p + 1)
        oe_ref[...] = stage(2 * j, 0, (tile, 2 * j + 1), 1)
        write_pair(tile, j, stage(2 * j + 1, 1, (nxt_tile, 2 * nxt_j), 0))
        return carry

    lax.fori_loop(0, n_pairs - 1, head_pair, 0)
    tile, j = split(n_pairs - 1)
    oe_ref[...] = stage(2 * j, 0, (tile, 2 * j + 1), 1)
    write_pair(tile, j, stage(2 * j + 1, 1, None, None))


def _attn_call(qt, k, vt, kc, vct):
    bsz, _, _, n = qt.shape
    n_ctx = kc.shape[2]
    tq = ATTN_TQ
    n_chunks = 1 + n // ATTN_KC
    per_batch = lambda b, i: (b, 0, 0, 0)
    resident = dict(pipeline_mode=pl.Buffered(1))
    rows_per_step = ATTN_TILES * tq
    return pl.pallas_call(
        _attn_kernel,
        grid=(bsz, n // rows_per_step),
        in_specs=[
            pl.BlockSpec((1, N_HEADS, LANES, rows_per_step), lambda b, i: (b, 0, 0, i)),
            pl.BlockSpec((1, N_HEADS, n, LANES), per_batch, **resident),
            pl.BlockSpec((1, N_HEADS, VT_ROWS, n), per_batch, **resident),
            pl.BlockSpec((1, N_HEADS, n_ctx, LANES), per_batch, **resident),
            pl.BlockSpec((1, N_HEADS, VT_ROWS, n_ctx), per_batch, **resident),
        ],
        out_specs=pl.BlockSpec((1, N_HEADS // 2, rows_per_step, LANES), lambda b, i: (b, 0, i, 0)),
        out_shape=jax.ShapeDtypeStruct((bsz, N_HEADS // 2, n, LANES), BF16),
        scratch_shapes=[
            pltpu.VMEM((2, n_ctx + n, tq), F32),
            pltpu.VMEM((2, n_chunks, 8, tq), F32),
            pltpu.VMEM((8, tq), F32),
            pltpu.VMEM((V_DIM, tq), F32),
        ],
        compiler_params=pltpu.CompilerParams(
            dimension_semantics=("arbitrary", "arbitrary"), vmem_limit_bytes=VMEM_LIMIT),
        name="attn",
    )(qt, k, vt, kc, vct)


def _mix_kernel(att_ref, u_ref, up_ref, un_ref, x_ref, mod_ref, wpool_ref, pscale_ref, wout_ref,
                g2_ref, rw_ref, rb_ref,
                x1_ref, h2_ref, gate_ref, ridx_ref, cnt_ref, ue_ref, base_ref, *, n_seq):
    b = pl.program_id(0)
    i = pl.program_id(1)
    n_tiles = pl.num_programs(1)
    ts = u_ref.shape[1]

    @pl.when(jnp.logical_and(b == 0, i == 0))
    def _():
        base_ref[...] = jnp.zeros_like(base_ref)

    u = u_ref[0]
    ue_ref[0:POOL_HALO] = jnp.where(i > 0, up_ref[0], 0.0)
    ue_ref[POOL_HALO:POOL_HALO + ts] = u
    ue_ref[POOL_HALO + ts:2 * POOL_HALO + ts] = jnp.where(i < n_tiles - 1, un_ref[0], 0.0)
    t = i * ts + lax.broadcasted_iota(jnp.int32, (ts, 1), 0)
    ys = []
    for g, w in enumerate(POOL_WINDOWS):
        half = w // 2
        lanes = slice(g * POOL_CH, (g + 1) * POOL_CH)
        ws = ue_ref[POOL_HALO - half:POOL_HALO - half + ts, lanes]
        for jj in range(-half + 1, half):
            ws = ws + ue_ref[POOL_HALO + jj:POOL_HALO + jj + ts, lanes]
        count = (jnp.minimum(t + half, n_seq) - jnp.maximum(t - half, 0)).astype(F32)
        mixed = (ws / count - u[:, lanes]).astype(BF16)
        ys.append(_dot(mixed, wpool_ref[g]))
    pool = jnp.concatenate(ys, axis=-1) * pscale_ref[...]

    cat = jnp.concatenate([att_ref[0, j] for j in range(N_HEADS // 2)] + [pool.astype(BF16)], axis=-1)
    m = mod_ref[0]
    x1 = x_ref[0] + m[2:3] * _dot(cat, wout_ref[...])
    x1_ref[0] = x1
    h2 = _rms(x1) * g2_ref[...] * (1.0 + m[4:5]) + m[3:4]
    h2_ref[0] = _pack_bf16_pairs(h2)

    h_hi = h2.astype(BF16)
    h_lo = (h2 - h_hi.astype(F32)).astype(BF16)
    hi_part = _dot(h_hi, rw_ref[...])
    logits = (hi_part[:, :LANES] + hi_part[:, LANES:] + _dot(h_lo, rw_ref[:, :LANES])) + rb_ref[...]
    lane = lax.broadcasted_iota(jnp.int32, logits.shape, 1).astype(F32)
    vals, idxs = [], []
    for _k in range(TOP_K):
        mv = jnp.max(logits, axis=-1, keepdims=True)
        ix = jnp.min(jnp.where(logits == mv, lane, float(LANES)), axis=-1, keepdims=True)
        vals.append(mv)
        idxs.append(ix)
        logits = jnp.where(lane == ix, -jnp.inf, logits)
    es = [jnp.exp(v - vals[0]) for v in vals]
    den = es[0] + es[1] + es[2] + es[3]

    onehot = jnp.zeros(lane.shape, F32)
    for ix in idxs:
        onehot = onehot + jnp.where(lane == ix, 1.0, 0.0)
    row = lax.broadcasted_iota(jnp.int32, (ts, ts), 0)
    col = lax.broadcasted_iota(jnp.int32, (ts, ts), 1)
    tri = jnp.where(col < row, 1.0, 0.0).astype(BF16)
    before = _dot(tri, onehot.astype(BF16)) + base_ref[0:1, :]
    base_new = base_ref[0:1, :] + jnp.sum(onehot, axis=0, keepdims=True)
    base_ref[...] = jnp.broadcast_to(base_new, base_ref.shape)
    cnt_ref[...] = jnp.broadcast_to(base_new, cnt_ref.shape)

    gate_out = jnp.zeros(lane.shape, F32)
    ridx_out = jnp.zeros(lane.shape, F32)
    for kk in range(TOP_K):
        rank = jnp.sum(jnp.where(lane == idxs[kk], before, 0.0), axis=-1, keepdims=True)
        gate_out = jnp.where(lane == float(kk), es[kk] / den, gate_out)
        ridx_out = jnp.where(lane == float(kk), idxs[kk], ridx_out)
        ridx_out = jnp.where(lane == float(TOP_K + kk), rank, ridx_out)
    gate_ref[0] = gate_out
    ridx_ref[0] = ridx_out.astype(jnp.int32)


def _mix_call(att, u, x, mod, wpool, pscale, wout, g2, rw, rb):
    bsz, n, d = x.shape
    ts = MIX_TS
    nt = n // ts
    pool_w = u.shape[2]
    hb = ts // POOL_HALO
    n_halo_blocks = n // POOL_HALO
    const2 = lambda b, i: (0, 0)
    tok = lambda b, i: (b, i, 0)
    return pl.pallas_call(
        functools.partial(_mix_kernel, n_seq=n),
        grid=(bsz, nt),
        in_specs=[
            pl.BlockSpec((1, N_HEADS // 2, ts, LANES), lambda b, i: (b, 0, i, 0)),
            pl.BlockSpec((1, ts, pool_w), tok),
            pl.BlockSpec((1, POOL_HALO, pool_w), lambda b, i: (b, jnp.maximum(i * hb - 1, 0), 0)),
            pl.BlockSpec((1, POOL_HALO, pool_w),
                         lambda b, i: (b, jnp.minimum((i + 1) * hb, n_halo_blocks - 1), 0)),
            pl.BlockSpec((1, ts, d), tok),
            pl.BlockSpec((1, N_MOD, d), lambda b, i: (b, 0, 0)),
            pl.BlockSpec(wpool.shape, lambda b, i: (0, 0, 0)),
            pl.BlockSpec((1, pool_w), const2),
            pl.BlockSpec(wout.shape, const2),
            pl.BlockSpec((1, d), const2),
            pl.BlockSpec(rw.shape, const2),
            pl.BlockSpec((1, LANES), const2),
        ],
        out_specs=[
            pl.BlockSpec((1, ts, d), tok),
            pl.BlockSpec((1, ts, d // 2), tok),
            pl.BlockSpec((1, ts, LANES), tok),
            pl.BlockSpec((1, ts, LANES), tok),
            pl.BlockSpec((8, LANES), const2),
        ],
        out_shape=[
            jax.ShapeDtypeStruct((bsz, n, d), F32),
            jax.ShapeDtypeStruct((bsz, n, d // 2), jnp.uint32),
            jax.ShapeDtypeStruct((bsz, n, LANES), F32),
            jax.ShapeDtypeStruct((bsz, n, LANES), jnp.int32),
            jax.ShapeDtypeStruct((8, LANES), F32),
        ],
        scratch_shapes=[
            pltpu.VMEM((ts + 2 * POOL_HALO, pool_w), F32),
            pltpu.VMEM((8, LANES), F32),
        ],
        compiler_params=pltpu.CompilerParams(
            dimension_semantics=("arbitrary", "arbitrary"), vmem_limit_bytes=VMEM_LIMIT),
        name="mix",
    )(att, u, u, u, x, mod, wpool, pscale, wout, g2, rw, rb)


def _sc_mesh():
    return plsc.VectorSubcoreMesh(core_axis_name="core", subcore_axis_name="subcore")


def _sc_worker_id():
    info = plsc.get_sparse_core_info()
    return lax.axis_index("subcore") * info.num_cores + lax.axis_index("core")


def _sc_num_workers():
    info = plsc.get_sparse_core_info()
    return info.num_cores * info.num_subcores


def _scatter_rows(x, dest3, n_out):
    t, d = x.shape
    top_k, n_chunks, rows = dest3.shape
    per_worker = n_chunks // _sc_num_workers()

    @functools.partial(
        pl.kernel,
        out_type=jax.ShapeDtypeStruct((n_out, d), x.dtype),
        mesh=_sc_mesh(),
        scratch_types=[pltpu.VMEM((top_k, rows), jnp.int32), pltpu.VMEM((rows, d), x.dtype)],
        name="sc_scatter",
    )
    def scatter(x_hbm, i_hbm, o_hbm, idx_v, rows_v):
        wid = _sc_worker_id()

        @pl.loop(0, per_worker)
        def _(c):
            chunk = wid * per_worker + c
            for kk in range(top_k):
                pltpu.sync_copy(i_hbm.at[kk, chunk], idx_v.at[kk])
            pltpu.sync_copy(x_hbm.at[pl.ds(pl.multiple_of(chunk * rows, 8), rows)], rows_v)
            for kk in range(top_k):
                pltpu.sync_copy(rows_v, o_hbm.at[idx_v.at[kk]])

    return scatter(x, dest3)


def _gather_rows(y, idx):
    n = idx.shape[0]
    d = y.shape[1]
    rows = SC_ROWS
    per_worker = n // _sc_num_workers()
    n_chunks = per_worker // rows

    @functools.partial(
        pl.kernel,
        out_type=jax.ShapeDtypeStruct((n, d), y.dtype),
        mesh=_sc_mesh(),
        scratch_types=[pltpu.VMEM((rows,), jnp.int32), pltpu.VMEM((rows, d), y.dtype)],
        name="sc_gather",
    )
    def gather(y_hbm, i_hbm, o_hbm, idx_v, rows_v):
        base = _sc_worker_id() * per_worker

        @pl.loop(0, n_chunks)
        def _(c):
            off = pl.multiple_of(base + c * rows, 8)
            pltpu.sync_copy(i_hbm.at[pl.ds(off, rows)], idx_v)
            pltpu.sync_copy(y_hbm.at[idx_v], rows_v)
            pltpu.sync_copy(rows_v, o_hbm.at[pl.ds(off, rows)])

    return gather(y, idx)


def _moe_kernel(blk_ref, exp_ref, lo_ref, hi_ref, cast_ref, init_ref, next_ref, xs_ref, wgu_hbm,
                bgu_ref, wd_hbm, bd_ref, ys_ref, wgu_f32, wd_f32, wgu_bf, wd_bf, sems):
    i = pl.program_id(0)

    def weight_copies(e):
        return (pltpu.make_async_copy(wgu_hbm.at[e], wgu_f32, sems.at[0]),
                pltpu.make_async_copy(wd_hbm.at[e], wd_f32, sems.at[1]))

    @pl.when(i == 0)
    def _():
        for cp in weight_copies(exp_ref[0]):
            cp.start()

    @pl.when(cast_ref[i] == 1)
    def _():
        for cp in weight_copies(exp_ref[i]):
            cp.wait()
        wgu_bf[...] = wgu_f32[...].astype(BF16)
        wd_bf[...] = wd_f32[...].astype(BF16)

        @pl.when(next_ref[i] >= 0)
        def _():
            for cp in weight_copies(next_ref[i]):
                cp.start()

    @pl.when(init_ref[i] == 1)
    def _():
        ys_ref[...] = jnp.zeros_like(ys_ref)

    @pl.when(hi_ref[i] > lo_ref[i])
    def _():
        de = wd_bf.shape[0]
        x = _unpack_bf16_pairs(xs_ref[...]).astype(BF16)
        gu = _dot(x, wgu_bf[...]) + bgu_ref[0]
        g = jnp.minimum(gu[:, :de], SWIGLU_LIMIT)
        lin = jnp.clip(gu[:, de:], -SWIGLU_LIMIT, SWIGLU_LIMIT)
        act = g / (1.0 + jnp.exp(-SWIGLU_ALPHA * g)) * (lin + 1.0)
        y = _dot(act.astype(BF16), wd_bf[...]) + bd_ref[0]
        row = lax.broadcasted_iota(jnp.int32, (y.shape[0], 1), 0)
        mine = jnp.logical_and(row >= lo_ref[i], row < hi_ref[i])
        ys_ref[...] = jnp.where(mine, _pack_bf16_pairs(y), ys_ref[...])


def _moe_call(sched, xs, wgu, bgu, wd, bd):
    n_items = sched[0].shape[0]
    _, d, de2 = wgu.shape
    de = de2 // 2
    wmap = lambda i, blk, exp, lo, hi, cast, init, nxt: (exp[i], 0, 0)
    xmap = lambda i, blk, exp, lo, hi, cast, init, nxt: (blk[i], 0)
    return pl.pallas_call(
        _moe_kernel,
        grid_spec=pltpu.PrefetchScalarGridSpec(
            num_scalar_prefetch=7,
            grid=(n_items,),
            in_specs=[
                pl.BlockSpec((MOE_BLOCK, d // 2), xmap),
                pl.BlockSpec(memory_space=pl.ANY),
                pl.BlockSpec((1, 1, de2), wmap),
                pl.BlockSpec(memory_space=pl.ANY),
                pl.BlockSpec((1, 1, d), wmap),
            ],
            out_specs=pl.BlockSpec((MOE_BLOCK, d // 2), xmap),
            scratch_shapes=[
                pltpu.VMEM((d, de2), F32), pltpu.VMEM((de, d), F32),
                pltpu.VMEM((d, de2), BF16), pltpu.VMEM((de, d), BF16),
                pltpu.SemaphoreType.DMA((2,)),
            ],
        ),
        out_shape=jax.ShapeDtypeStruct(xs.shape, xs.dtype),
        compiler_params=pltpu.CompilerParams(
            dimension_semantics=("arbitrary",), vmem_limit_bytes=VMEM_LIMIT),
        name="moe",
    )(*sched, xs, wgu, bgu, wd, bd)


def _moe_schedule(counts, n_rows):
    n_blocks = n_rows // MOE_BLOCK
    n_items = n_blocks + N_EXPERTS
    ends = jnp.cumsum(counts)
    starts = ends - counts
    first_blk = starts // MOE_BLOCK
    last_blk = (ends - 1) // MOE_BLOCK
    items_per = jnp.where(counts > 0, last_blk - first_blk + 1, 0)
    item_ends = jnp.cumsum(items_per)
    item_starts = item_ends - items_per
    total = item_ends[-1]
    it = jnp.arange(n_items, dtype=jnp.int32)
    live = it < total
    itc = jnp.minimum(it, total - 1)
    exp = jnp.sum((item_ends[None, :] <= itc[:, None]).astype(jnp.int32), axis=1)
    is_exp = exp[:, None] == jnp.arange(N_EXPERTS, dtype=jnp.int32)[None, :]
    pick = lambda table: jnp.sum(jnp.where(is_exp, table[None, :], 0), axis=1)
    blk = pick(first_blk) + itc - pick(item_starts)
    lo = jnp.clip(pick(starts) - blk * MOE_BLOCK, 0, MOE_BLOCK)
    hi = jnp.clip(pick(ends) - blk * MOE_BLOCK, 0, MOE_BLOCK)
    hi = jnp.where(live, hi, lo)
    prev_exp = jnp.concatenate([jnp.full((1,), -1, jnp.int32), exp[:-1]])
    prev_blk = jnp.concatenate([jnp.full((1,), -1, jnp.int32), blk[:-1]])
    cast = jnp.logical_and(live, exp != prev_exp)
    init = jnp.logical_and(live, blk != prev_blk)
    ar = jnp.arange(N_EXPERTS, dtype=jnp.int32)
    later = jnp.logical_and(counts[None, :] > 0, ar[None, :] > ar[:, None])
    next_exp = jnp.min(jnp.where(later, ar[None, :], N_EXPERTS), axis=1)
    next_exp = jnp.where(next_exp == N_EXPERTS, -1, next_exp)
    nxt = pick(next_exp)
    as_i32 = lambda a: a.astype(jnp.int32)
    return tuple(as_i32(a) for a in (blk, exp, lo, hi, cast, init, nxt)), starts


def _combine_kernel(*refs):
    yg_refs = refs[:TOP_K]
    x1_ref, gate_ref, mod_ref, fg_ref, o_ref = refs[TOP_K:]
    gates = gate_ref[...]
    y = gates[:, 0:1] * _unpack_bf16_pairs(yg_refs[0][...])
    for kk in range(1, TOP_K):
        y = y + gates[:, kk:kk + 1] * _unpack_bf16_pairs(yg_refs[kk][...])
    m = mod_ref[0]
    x2 = x1_ref[...] + m[5:6] * y
    o_ref[...] = _rms(x2) * fg_ref[...]


def _combine_call(yg, x1, gates, mod, fg, n_seq, part):
    t, d = x1.shape
    tt = COMBINE_TT
    tiles_per_seq = n_seq // tt
    n_tiles = yg.shape[0] // (TOP_K * tt)
    first = part * n_tiles
    tok = lambda i: (first + i, 0)
    slot_specs = [pl.BlockSpec((tt, d // 2), functools.partial(lambda kk, i: (kk * n_tiles + i, 0), kk))
                  for kk in range(TOP_K)]
    return pl.pallas_call(
        _combine_kernel,
        grid=(n_tiles,),
        in_specs=slot_specs + [
            pl.BlockSpec((tt, d), tok),
            pl.BlockSpec((tt, LANES), tok),
            pl.BlockSpec((1, N_MOD, d), lambda i: ((first + i) // tiles_per_seq, 0, 0)),
            pl.BlockSpec((1, d), lambda i: (0, 0)),
        ],
        out_specs=pl.BlockSpec((tt, d), tok),
        out_shape=jax.ShapeDtypeStruct((t, d), F32),
        input_output_aliases={TOP_K: 0},
        compiler_params=pltpu.CompilerParams(
            dimension_semantics=("arbitrary",), vmem_limit_bytes=VMEM_LIMIT),
        name="combine",
    )(*([yg] * TOP_K), x1, gates, mod, fg)


def _pad_cols(a, width):
    return jnp.pad(a, ((0, 0), (0, width - a.shape[1])))


def _rope_slab(w_rope, swapped):
    half = QK_ROPE // 2
    if swapped:
        w_rope = jnp.concatenate([w_rope[:, half:], w_rope[:, :half]], axis=1)
    zeros = jnp.zeros((w_rope.shape[0], QK_NOPE), w_rope.dtype)
    return _pad_cols(jnp.concatenate([zeros, w_rope], axis=1), LANES)


def _prep_w_in(w_in):
    kr0 = Q_LORA + KV_LORA
    w_kr = w_in[:, kr0:kr0 + QK_ROPE]
    return jnp.concatenate(
        [w_in[:, :kr0], _rope_slab(w_kr, False), _rope_slab(w_kr, True), w_in[:, kr0 + QK_ROPE:]],
        axis=1).astype(BF16)


def _prep_w_uq(w_uq):
    per = QK_NOPE + QK_ROPE
    half = QK_ROPE // 2
    rows = w_uq.shape[0]
    w = w_uq.reshape(rows, N_HEADS, per)
    slab_a = jnp.pad(w, ((0, 0), (0, 0), (0, LANES - per)))
    rope = w[:, :, QK_NOPE:]
    swapped = jnp.concatenate([rope[:, :, half:], rope[:, :, :half]], axis=2)
    slab_b = jnp.pad(swapped, ((0, 0), (0, 0), (QK_NOPE, LANES - per)))
    both = jnp.concatenate([slab_a.reshape(rows, -1), slab_b.reshape(rows, -1)], axis=1)
    return both.T.astype(BF16)


def _prep_w_ukv(w_ukv):
    per = QK_NOPE + V_DIM
    rows = w_ukv.shape[0]
    w = w_ukv.reshape(rows, N_HEADS, per)
    wk = jnp.pad(w[:, :, :QK_NOPE], ((0, 0), (0, 0), (0, LANES - QK_NOPE))).reshape(rows, -1)
    wvt = jnp.pad(jnp.transpose(w[:, :, QK_NOPE:], (1, 2, 0)), ((0, 0), (0, VT_ROWS - V_DIM), (0, 0)))
    return wk.astype(BF16), wvt.reshape(N_HEADS * VT_ROWS, rows).astype(BF16)


def _rope_tables(n_lat):
    rows = n_lat // GRID_W
    nf = QK_ROPE // 4
    row = jnp.repeat(jnp.arange(rows, dtype=F32), GRID_W)
    col = jnp.tile(jnp.arange(GRID_W, dtype=F32), rows)
    freqs = ROPE_BASE ** (-jnp.arange(nf, dtype=F32) / nf)
    ang = jnp.concatenate([row[:, None] * freqs, col[:, None] * freqs], axis=-1)
    cos, sin = jnp.cos(ang), jnp.sin(ang)
    ones = jnp.ones((n_lat, QK_NOPE), F32)
    zeros = jnp.zeros((n_lat, QK_NOPE), F32)
    cs = _pad_cols(jnp.concatenate([ones, cos, cos], axis=1), LANES)
    sn = _pad_cols(jnp.concatenate([zeros, -sin, sin], axis=1), LANES)
    return cs, sn


def kernel(x, c, ctx, c_ctx, w_mod, b_mod, norm1_g, w_in, q_norm_g, kv_norm_g, w_uq, w_ukv, w_pool,
           pool_scale, w_out, norm2_g, router_w, router_b, w_gate_up, b_gate_up, w_down, b_down,
           final_g):
    bsz, n, d = x.shape
    n_ctx = ctx.shape[1]
    t = bsz * n
    l = 0

    cc = jnp.concatenate([c, c_ctx[None, :], jnp.zeros((8 - bsz - 1, d), F32)], axis=0)
    mod = _mod_call(cc, w_mod[l], b_mod[l][None, :]).reshape(8, N_MOD, d)
    mod_lat, mod_ctx = mod[:bsz], mod[bsz:bsz + 1]

    win = _prep_w_in(w_in[l])
    wuqt = _prep_w_uq(w_uq[l])
    wk, wvt = _prep_w_ukv(w_ukv[l])
    cs, sn = _rope_tables(n)
    cs_ctx = jnp.broadcast_to((jnp.arange(LANES) < QK_NOPE + QK_ROPE).astype(F32), (n_ctx, LANES))
    sn_ctx = jnp.zeros((n_ctx, LANES), F32)
    g1 = norm1_g[l][None, :]
    qg = q_norm_g[l][None, :]
    kvg = kv_norm_g[l][None, :]

    qt, k, vt, u = _front_call(x, mod_lat, True, g1, win, qg, kvg, wuqt, wk, wvt, cs, sn,
                               is_ctx=False, ts=FRONT_TS)
    kc, vct = _front_call(ctx, mod_ctx, False, g1, win[:, Q_LORA:Q_LORA + 3 * LANES], qg, kvg, wuqt,
                          wk, wvt, cs_ctx, sn_ctx, is_ctx=True, ts=n_ctx)
    att = _attn_call(qt, k, vt, kc, vct)

    rw = _pad_cols(router_w[l], LANES)
    rw_hi = rw.astype(BF16)
    rw = jnp.concatenate([rw_hi, (rw - rw_hi.astype(F32)).astype(BF16)], axis=1)
    rb = jnp.concatenate([router_b[l], jnp.full((LANES - N_EXPERTS,), -jnp.inf, F32)])[None, :]
    x1, h2, gates, ridx, cnt = _mix_call(
        att, u, x, mod_lat, w_pool[l].astype(BF16), pool_scale[l][None, :], w_out[l].astype(BF16),
        norm2_g[l][None, :], rw, rb)

    counts = cnt[0, :N_EXPERTS].astype(jnp.int32)
    sched, starts = _moe_schedule(counts, t * TOP_K)
    ridx = ridx.reshape(t, LANES)
    is_exp = ridx[:, :TOP_K, None] == jnp.arange(N_EXPERTS, dtype=jnp.int32)
    dest = jnp.sum(jnp.where(is_exp, starts, 0), axis=-1) + ridx[:, TOP_K:2 * TOP_K]
    dest_t = dest.T

    xs = _scatter_rows(h2.reshape(t, d // 2), dest_t.reshape(TOP_K, t // SC_ROWS, SC_ROWS), t * TOP_K)
    ys = _moe_call(sched, xs, w_gate_up[l], b_gate_up[l][:, None, :], w_down[l],
                   b_down[l][:, None, :])
    out = x1.reshape(t, d)
    gates = gates.reshape(t, LANES)
    tp = t // COMBINE_PARTS
    for part in range(COMBINE_PARTS):
        idx = dest_t[:, part * tp:(part + 1) * tp].reshape(-1)
        yg = _gather_rows(ys, idx)
        out = _combine_call(yg, out, gates, mod_lat, final_g[None, :], n, part)
    return out.reshape(bsz, n, d)
```

```python
import functools
import math

import jax
import jax.numpy as jnp
from jax import lax
from jax.experimental import pallas as pl
from jax.experimental.pallas import tpu as pltpu
from jax.experimental.pallas import tpu_sc as plsc

F32 = jnp.float32
BF16 = jnp.bfloat16
HIGHEST = lax.Precision.HIGHEST

N_HEADS = 8
QK_NOPE = 64
QK_ROPE = 32
V_DIM = 64
Q_LORA = 256
KV_LORA = 128
GRID_W = 64
ROPE_BASE = 10000.0
POOL_WINDOWS = (2, 4, 8, 16)
POOL_CH = 128
N_EXPERTS = 32
TOP_K = 4
SWIGLU_LIMIT = 7.0
SWIGLU_ALPHA = 1.702
MOE_BLOCK = 512
N_MOD = 6
EPS = 1e-6

LANES = 128
POOL_HALO = 8
VT_ROWS = 80
VMEM_LIMIT = 56 * 1024 * 1024

FRONT_TS = 512
ATTN_TQ = 512
ATTN_TILES = 4
ATTN_KC = 512
MIX_TS = 512
COMBINE_TT = 256
COMBINE_PARTS = 4
SC_ROWS = 128


def _dot(a, b, **kw):
    return jnp.dot(a, b, preferred_element_type=F32, **kw)


def _dot_nt(a, b):
    return lax.dot_general(a, b, (((1,), (1,)), ((), ())), preferred_element_type=F32)


def _rms(x):
    return x * lax.rsqrt(jnp.mean(x * x, axis=-1, keepdims=True) + EPS)


def _pack_bf16_pairs(x):
    w = x.shape[1] // 2
    hi = lax.bitcast_convert_type(x[:, :w].astype(BF16).astype(F32), jnp.uint32)
    lo = lax.bitcast_convert_type(x[:, w:].astype(BF16).astype(F32), jnp.uint32)
    return hi | (lo >> 16)


def _unpack_bf16_pairs(words):
    hi = lax.bitcast_convert_type(words & jnp.uint32(0xFFFF0000), F32)
    lo = lax.bitcast_convert_type(words << 16, F32)
    return jnp.concatenate([hi, lo], axis=-1)


def _mod_kernel(c_ref, w_ref, b_ref, o_ref):
    c = c_ref[...]
    a = c / (1.0 + jnp.exp(-c))
    o_ref[...] = _dot(a, w_ref[...], precision=HIGHEST) + b_ref[...]


def _mod_call(cc, w_mod, b_mod):
    d = w_mod.shape[0]
    return pl.pallas_call(
        _mod_kernel,
        grid=(N_MOD,),
        in_specs=[
            pl.BlockSpec((8, d), lambda j: (0, 0)),
            pl.BlockSpec((d, d), lambda j: (0, j)),
            pl.BlockSpec((1, d), lambda j: (0, j)),
        ],
        out_specs=pl.BlockSpec((8, d), lambda j: (0, j)),
        out_shape=jax.ShapeDtypeStruct((8, N_MOD * d), F32),
        compiler_params=pltpu.CompilerParams(
            dimension_semantics=("arbitrary",), vmem_limit_bytes=VMEM_LIMIT),
        name="mod",
    )(cc, w_mod, b_mod)


def _front_kernel(x_ref, mod_ref, g1_ref, win_ref, qg_ref, kvg_ref, wuqt_ref, wk_ref, wvt_ref,
                  cs_ref, sn_ref, cst_ref, snt_ref, *out_refs, is_ctx, scale):
    x = x_ref[0]
    m = mod_ref[0]
    h = _rms(x) * g1_ref[...] * (1.0 + m[1:2]) + m[0:1]
    z = _dot(h.astype(BF16), win_ref[...])
    if is_ctx:
        k_ref, vt_ref = out_refs
        zkv = z
    else:
        qt_ref, k_ref, vt_ref, u_ref = out_refs
        zkv = z[:, Q_LORA:Q_LORA + 3 * LANES]
        u_ref[0] = z[:, Q_LORA + 3 * LANES:]

    ckv = _rms(zkv[:, :KV_LORA]) * kvg_ref[...]
    kk = _dot(ckv.astype(BF16), wk_ref[...])
    kr = zkv[:, LANES:2 * LANES] * cs_ref[...] + zkv[:, 2 * LANES:3 * LANES] * sn_ref[...]
    for hd in range(N_HEADS):
        k_ref[0, hd] = (kk[:, hd * LANES:(hd + 1) * LANES] + kr).astype(BF16)
    vt = _dot(wvt_ref[...], ckv.T.astype(BF16))
    row = lax.broadcasted_iota(jnp.int32, (VT_ROWS, 1), 0)
    ones = jnp.where(row == V_DIM, 1.0, 0.0)
    for hd in range(N_HEADS):
        vt_ref[0, hd] = (vt[hd * VT_ROWS:(hd + 1) * VT_ROWS] + ones).astype(BF16)

    if not is_ctx:
        cq = _rms(z[:, :Q_LORA]) * qg_ref[...]
        qqt = _dot(wuqt_ref[...], cq.T.astype(BF16))
        cst = cst_ref[...]
        snt = snt_ref[...]
        boff = N_HEADS * LANES
        for hd in range(N_HEADS):
            qa = qqt[hd * LANES:(hd + 1) * LANES]
            qb = qqt[boff + hd * LANES: boff + (hd + 1) * LANES]
            qt_ref[0, hd] = ((qa * cst + qb * snt) * scale).astype(BF16)


def _front_call(xs, mod, mod_per_batch, g1, win, qg, kvg, wuqt, wk, wvt, cs, sn, *, is_ctx, ts):
    bsz, n, d = xs.shape
    nt = n // ts
    scale = math.log2(math.e) / math.sqrt(QK_NOPE + QK_ROPE)
    const = lambda b, i: (0, 0)
    mod_map = (lambda b, i: (b, 0, 0)) if mod_per_batch else (lambda b, i: (0, 0, 0))
    in_specs = [
        pl.BlockSpec((1, ts, d), lambda b, i: (b, i, 0)),
        pl.BlockSpec((1, N_MOD, d), mod_map),
        pl.BlockSpec((1, d), const),
        pl.BlockSpec(win.shape, const),
        pl.BlockSpec((1, Q_LORA), const),
        pl.BlockSpec((1, KV_LORA), const),
        pl.BlockSpec(wuqt.shape, const),
        pl.BlockSpec(wk.shape, const),
        pl.BlockSpec(wvt.shape, const),
        pl.BlockSpec((ts, LANES), lambda b, i: (i, 0)),
        pl.BlockSpec((ts, LANES), lambda b, i: (i, 0)),
        pl.BlockSpec((LANES, ts), lambda b, i: (0, i)),
        pl.BlockSpec((LANES, ts), lambda b, i: (0, i)),
    ]
    k_spec = pl.BlockSpec((1, N_HEADS, ts, LANES), lambda b, i: (b, 0, i, 0))
    k_shape = jax.ShapeDtypeStruct((bsz, N_HEADS, n, LANES), BF16)
    qt_spec = pl.BlockSpec((1, N_HEADS, LANES, ts), lambda b, i: (b, 0, 0, i))
    qt_shape = jax.ShapeDtypeStruct((bsz, N_HEADS, LANES, n), BF16)
    vt_spec = pl.BlockSpec((1, N_HEADS, VT_ROWS, ts), lambda b, i: (b, 0, 0, i))
    vt_shape = jax.ShapeDtypeStruct((bsz, N_HEADS, VT_ROWS, n), BF16)
    if is_ctx:
        out_specs = [k_spec, vt_spec]
        out_shape = [k_shape, vt_shape]
    else:
        pool_w = win.shape[1] - Q_LORA - 3 * LANES
        out_specs = [qt_spec, k_spec, vt_spec, pl.BlockSpec((1, ts, pool_w), lambda b, i: (b, i, 0))]
        out_shape = [qt_shape, k_shape, vt_shape, jax.ShapeDtypeStruct((bsz, n, pool_w), F32)]
    return pl.pallas_call(
        functools.partial(_front_kernel, is_ctx=is_ctx, scale=scale),
        grid=(bsz, nt),
        in_specs=in_specs,
        out_specs=out_specs,
        out_shape=out_shape,
        compiler_params=pltpu.CompilerParams(
            dimension_semantics=("arbitrary", "arbitrary"), vmem_limit_bytes=VMEM_LIMIT),
        name="front_ctx" if is_ctx else "front",
    )(xs, mod, g1, win, qg, kvg, wuqt, wk, wvt, cs, sn, cs.T, sn.T)


def _attn_kernel(qt_ref, k_ref, vt_ref, kc_ref, vct_ref, o_ref, s_ref, mlc_ref, mxb_ref, oe_ref):
    n_ctx = kc_ref.shape[2]
    n_lat = k_ref.shape[2]
    chunks = [(None, 0, n_ctx)] + [(c * ATTN_KC, n_ctx + c * ATTN_KC, ATTN_KC)
                                   for c in range(n_lat // ATTN_KC)]
    sub = 8

    def q_cols(tile):
        if isinstance(tile, int):
            return pl.ds(tile * ATTN_TQ, ATTN_TQ)
        return pl.ds(pl.multiple_of(tile * ATTN_TQ, ATTN_TQ), ATTN_TQ)

    def score_chunk(tile, hd, buf, ci):
        off, soff, w = chunks[ci]
        keys = kc_ref[0, hd] if off is None else k_ref[0, hd, off:off + w, :]
        s = _dot(keys, qt_ref[0, hd, :, q_cols(tile)])
        s_ref[buf, soff:soff + w, :] = s
        mx = s[0:sub]
        for r in range(1, w // sub):
            mx = jnp.maximum(mx, s[r * sub:(r + 1) * sub])
        mlc_ref[buf, ci] = mx

    def row_max(buf):
        mx = mlc_ref[buf, 0]
        for ci in range(1, len(chunks)):
            mx = jnp.maximum(mx, mlc_ref[buf, ci])
        mxb_ref[...] = jnp.broadcast_to(jnp.max(mx, axis=0, keepdims=True), mx.shape)

    def weight_chunk(hd, buf, ci):
        off, soff, w = chunks[ci]
        p = jnp.exp2(s_ref[buf, soff:soff + w, :] - mxb_ref[0:1, :]).astype(BF16)
        vt = vct_ref[0, hd] if off is None else vt_ref[0, hd, :, off:off + w]
        return _dot(vt, p)

    def stage(hw, bw, scoring, bs):
        row_max(bw)
        acc = None
        for ci in range(len(chunks)):
            if scoring is not None:
                score_chunk(*scoring, bs, ci)
            part = weight_chunk(hw, bw, ci)
            acc = part if acc is None else acc + part
        return acc[0:V_DIM] / acc[V_DIM:V_DIM + 1]

    def write_pair(tile, j, ot_odd):
        pair_t = jnp.concatenate([oe_ref[...], ot_odd], axis=0)
        o_ref[0, j, q_cols(tile), :] = pair_t.T.astype(BF16)

    pairs_per_tile = N_HEADS // 2
    n_pairs = (qt_ref.shape[3] // ATTN_TQ) * pairs_per_tile

    def split(p):
        return p // pairs_per_tile, p % pairs_per_tile

    for ci in range(len(chunks)):
        score_chunk(0, 0, 0, ci)

    def head_pair(p, carry):
        tile, j = split(p)
        nxt_tile, nxt_j = split(p + 1)
        oe_ref[...] = stage(2 * j, 0, (tile, 2 * j + 1), 1)
        write_pair(tile, j, stage(2 * j + 1, 1, (nxt_tile, 2 * nxt_j), 0))
        return carry

    lax.fori_loop(0, n_pairs - 1, head_pair, 0)
    tile, j = split(n_pairs - 1)
    oe_ref[...] = stage(2 * j, 0, (tile, 2 * j + 1), 1)
    write_pair(tile, j, stage(2 * j + 1, 1, None, None))


def _attn_call(qt, k, vt, kc, vct):
    bsz, _, _, n = qt.shape
    n_ctx = kc.shape[2]
    tq = ATTN_TQ
    n_chunks = 1 + n // ATTN_KC
    per_batch = lambda b, i: (b, 0, 0, 0)
    resident = dict(pipeline_mode=pl.Buffered(1))
    rows_per_step = ATTN_TILES * tq
    return pl.pallas_call(
        _attn_kernel,
        grid=(bsz, n // rows_per_step),
        in_specs=[
            pl.BlockSpec((1, N_HEADS, LANES, rows_per_step), lambda b, i: (b, 0, 0, i)),
            pl.BlockSpec((1, N_HEADS, n, LANES), per_batch, **resident),
            pl.BlockSpec((1, N_HEADS, VT_ROWS, n), per_batch, **resident),
            pl.BlockSpec((1, N_HEADS, n_ctx, LANES), per_batch, **resident),
            pl.BlockSpec((1, N_HEADS, VT_ROWS, n_ctx), per_batch, **resident),
        ],
        out_specs=pl.BlockSpec((1, N_HEADS // 2, rows_per_step, LANES), lambda b, i: (b, 0, i, 0)),
        out_shape=jax.ShapeDtypeStruct((bsz, N_HEADS // 2, n, LANES), BF16),
        scratch_shapes=[
            pltpu.VMEM((2, n_ctx + n, tq), F32),
            pltpu.VMEM((2, n_chunks, 8, tq), F32),
            pltpu.VMEM((8, tq), F32),
            pltpu.VMEM((V_DIM, tq), F32),
        ],
        compiler_params=pltpu.CompilerParams(
            dimension_semantics=("arbitrary", "arbitrary"), vmem_limit_bytes=VMEM_LIMIT),
        name="attn",
    )(qt, k, vt, kc, vct)


def _mix_kernel(att_ref, u_ref, up_ref, un_ref, x_ref, mod_ref, wpool_ref, pscale_ref, wout_ref,
                g2_ref, rw_ref, rb_ref,
                x1_ref, h2_ref, gate_ref, ridx_ref, cnt_ref, ue_ref, base_ref, *, n_seq):
    b = pl.program_id(0)
    i = pl.program_id(1)
    n_tiles = pl.num_programs(1)
    ts = u_ref.shape[1]

    @pl.when(jnp.logical_and(b == 0, i == 0))
    def _():
        base_ref[...] = jnp.zeros_like(base_ref)

    u = u_ref[0]
    ue_ref[0:POOL_HALO] = jnp.where(i > 0, up_ref[0], 0.0)
    ue_ref[POOL_HALO:POOL_HALO + ts] = u
    ue_ref[POOL_HALO + ts:2 * POOL_HALO + ts] = jnp.where(i < n_tiles - 1, un_ref[0], 0.0)
    t = i * ts + lax.broadcasted_iota(jnp.int32, (ts, 1), 0)
    ys = []
    for g, w in enumerate(POOL_WINDOWS):
        half = w // 2
        lanes = slice(g * POOL_CH, (g + 1) * POOL_CH)
        ws = ue_ref[POOL_HALO - half:POOL_HALO - half + ts, lanes]
        for jj in range(-half + 1, half):
            ws = ws + ue_ref[POOL_HALO + jj:POOL_HALO + jj + ts, lanes]
        count = (jnp.minimum(t + half, n_seq) - jnp.maximum(t - half, 0)).astype(F32)
        mixed = (ws / count - u[:, lanes]).astype(BF16)
        ys.append(_dot(mixed, wpool_ref[g]))
    pool = jnp.concatenate(ys, axis=-1) * pscale_ref[...]

    cat = jnp.concatenate([att_ref[0, j] for j in range(N_HEADS // 2)] + [pool.astype(BF16)], axis=-1)
    m = mod_ref[0]
    x1 = x_ref[0] + m[2:3] * _dot(cat, wout_ref[...])
    x1_ref[0] = x1
    h2 = _rms(x1) * g2_ref[...] * (1.0 + m[4:5]) + m[3:4]
    h2_ref[0] = _pack_bf16_pairs(h2)

    h_hi = h2.astype(BF16)
    h_lo = (h2 - h_hi.astype(F32)).astype(BF16)
    hi_part = _dot(h_hi, rw_ref[...])
    logits = (hi_part[:, :LANES] + hi_part[:, LANES:] + _dot(h_lo, rw_ref[:, :LANES])) + rb_ref[...]
    lane = lax.broadcasted_iota(jnp.int32, logits.shape, 1).astype(F32)
    vals, idxs = [], []
    for _k in range(TOP_K):
        mv = jnp.max(logits, axis=-1, keepdims=True)
        ix = jnp.min(jnp.where(logits == mv, lane, float(LANES)), axis=-1, keepdims=True)
        vals.append(mv)
        idxs.append(ix)
        logits = jnp.where(lane == ix, -jnp.inf, logits)
    es = [jnp.exp(v - vals[0]) for v in vals]
    den = es[0] + es[1] + es[2] + es[3]

    onehot = jnp.zeros(lane.shape, F32)
    for ix in idxs:
        onehot = onehot + jnp.where(lane == ix, 1.0, 0.0)
    row = lax.broadcasted_iota(jnp.int32, (ts, ts), 0)
    col = lax.broadcasted_iota(jnp.int32, (ts, ts), 1)
    tri = jnp.where(col < row, 1.0, 0.0).astype(BF16)
    before = _dot(tri, onehot.astype(BF16)) + base_ref[0:1, :]
    base_new = base_ref[0:1, :] + jnp.sum(onehot, axis=0, keepdims=True)
    base_ref[...] = jnp.broadcast_to(base_new, base_ref.shape)
    cnt_ref[...] = jnp.broadcast_to(base_new, cnt_ref.shape)

    gate_out = jnp.zeros(lane.shape, F32)
    ridx_out = jnp.zeros(lane.shape, F32)
    for kk in range(TOP_K):
        rank = jnp.sum(jnp.where(lane == idxs[kk], before, 0.0), axis=-1, keepdims=True)
        gate_out = jnp.where(lane == float(kk), es[kk] / den, gate_out)
        ridx_out = jnp.where(lane == float(kk), idxs[kk], ridx_out)
        ridx_out = jnp.where(lane == float(TOP_K + kk), rank, ridx_out)
    gate_ref[0] = gate_out
    ridx_ref[0] = ridx_out.astype(jnp.int32)


def _mix_call(att, u, x, mod, wpool, pscale, wout, g2, rw, rb):
    bsz, n, d = x.shape
    ts = MIX_TS
    nt = n // ts
    pool_w = u.shape[2]
    hb = ts // POOL_HALO
    n_halo_blocks = n // POOL_HALO
    const2 = lambda b, i: (0, 0)
    tok = lambda b, i: (b, i, 0)
    return pl.pallas_call(
        functools.partial(_mix_kernel, n_seq=n),
        grid=(bsz, nt),
        in_specs=[
            pl.BlockSpec((1, N_HEADS // 2, ts, LANES), lambda b, i: (b, 0, i, 0)),
            pl.BlockSpec((1, ts, pool_w), tok),
            pl.BlockSpec((1, POOL_HALO, pool_w), lambda b, i: (b, jnp.maximum(i * hb - 1, 0), 0)),
            pl.BlockSpec((1, POOL_HALO, pool_w),
                         lambda b, i: (b, jnp.minimum((i + 1) * hb, n_halo_blocks - 1), 0)),
            pl.BlockSpec((1, ts, d), tok),
            pl.BlockSpec((1, N_MOD, d), lambda b, i: (b, 0, 0)),
            pl.BlockSpec(wpool.shape, lambda b, i: (0, 0, 0)),
            pl.BlockSpec((1, pool_w), const2),
            pl.BlockSpec(wout.shape, const2),
            pl.BlockSpec((1, d), const2),
            pl.BlockSpec(rw.shape, const2),
            pl.BlockSpec((1, LANES), const2),
        ],
        out_specs=[
            pl.BlockSpec((1, ts, d), tok),
            pl.BlockSpec((1, ts, d // 2), tok),
            pl.BlockSpec((1, ts, LANES), tok),
            pl.BlockSpec((1, ts, LANES), tok),
            pl.BlockSpec((8, LANES), const2),
        ],
        out_shape=[
            jax.ShapeDtypeStruct((bsz, n, d), F32),
            jax.ShapeDtypeStruct((bsz, n, d // 2), jnp.uint32),
            jax.ShapeDtypeStruct((bsz, n, LANES), F32),
            jax.ShapeDtypeStruct((bsz, n, LANES), jnp.int32),
            jax.ShapeDtypeStruct((8, LANES), F32),
        ],
        scratch_shapes=[
            pltpu.VMEM((ts + 2 * POOL_HALO, pool_w), F32),
            pltpu.VMEM((8, LANES), F32),
        ],
        compiler_params=pltpu.CompilerParams(
            dimension_semantics=("arbitrary", "arbitrary"), vmem_limit_bytes=VMEM_LIMIT),
        name="mix",
    )(att, u, u, u, x, mod, wpool, pscale, wout, g2, rw, rb)


def _sc_mesh():
    return plsc.VectorSubcoreMesh(core_axis_name="core", subcore_axis_name="subcore")


def _sc_worker_id():
    info = plsc.get_sparse_core_info()
    return lax.axis_index("subcore") * info.num_cores + lax.axis_index("core")


def _sc_num_workers():
    info = plsc.get_sparse_core_info()
    return info.num_cores * info.num_subcores


def _scatter_rows(x, dest3, n_out):
    t, d = x.shape
    top_k, n_chunks, rows = dest3.shape
    per_worker = n_chunks // _sc_num_workers()

    @functools.partial(
        pl.kernel,
        out_type=jax.ShapeDtypeStruct((n_out, d), x.dtype),
        mesh=_sc_mesh(),
        scratch_types=[pltpu.VMEM((top_k, rows), jnp.int32), pltpu.VMEM((rows, d), x.dtype)],
        name="sc_scatter",
    )
    def scatter(x_hbm, i_hbm, o_hbm, idx_v, rows_v):
        wid = _sc_worker_id()

        @pl.loop(0, per_worker)
        def _(c):
            chunk = wid * per_worker + c
            for kk in range(top_k):
                pltpu.sync_copy(i_hbm.at[kk, chunk], idx_v.at[kk])
            pltpu.sync_copy(x_hbm.at[pl.ds(pl.multiple_of(chunk * rows, 8), rows)], rows_v)
            for kk in range(top_k):
                pltpu.sync_copy(rows_v, o_hbm.at[idx_v.at[kk]])

    return scatter(x, dest3)


def _gather_rows(y, idx):
    n = idx.shape[0]
    d = y.shape[1]
    rows = SC_ROWS
    per_worker = n // _sc_num_workers()
    n_chunks = per_worker // rows

    @functools.partial(
        pl.kernel,
        out_type=jax.ShapeDtypeStruct((n, d), y.dtype),
        mesh=_sc_mesh(),
        scratch_types=[pltpu.VMEM((rows,), jnp.int32), pltpu.VMEM((rows, d), y.dtype)],
        name="sc_gather",
    )
    def gather(y_hbm, i_hbm, o_hbm, idx_v, rows_v):
        base = _sc_worker_id() * per_worker

        @pl.loop(0, n_chunks)
        def _(c):
            off = pl.multiple_of(base + c * rows, 8)
            pltpu.sync_copy(i_hbm.at[pl.ds(off, rows)], idx_v)
            pltpu.sync_copy(y_hbm.at[idx_v], rows_v)
            pltpu.sync_copy(rows_v, o_hbm.at[pl.ds(off, rows)])

    return gather(y, idx)


def _moe_kernel(blk_ref, exp_ref, lo_ref, hi_ref, cast_ref, init_ref, next_ref, xs_ref, wgu_hbm,
                bgu_ref, wd_hbm, bd_ref, ys_ref, wgu_f32, wd_f32, wgu_bf, wd_bf, sems):
    i = pl.program_id(0)

    def weight_copies(e):
        return (pltpu.make_async_copy(wgu_hbm.at[e], wgu_f32, sems.at[0]),
                pltpu.make_async_copy(wd_hbm.at[e], wd_f32, sems.at[1]))

    @pl.when(i == 0)
    def _():
        for cp in weight_copies(exp_ref[0]):
            cp.start()

    @pl.when(cast_ref[i] == 1)
    def _():
        for cp in weight_copies(exp_ref[i]):
            cp.wait()
        wgu_bf[...] = wgu_f32[...].astype(BF16)
        wd_bf[...] = wd_f32[...].astype(BF16)

        @pl.when(next_ref[i] >= 0)
        def _():
            for cp in weight_copies(next_ref[i]):
                cp.start()

    @pl.when(init_ref[i] == 1)
    def _():
        ys_ref[...] = jnp.zeros_like(ys_ref)

    lo = lo_ref[i]
    hi = hi_ref[i]
    half = MOE_BLOCK // 2

    def expert_rows(r0, n_rows):
        de = wd_bf.shape[0]
        x = _unpack_bf16_pairs(xs_ref[r0:r0 + n_rows, :]).astype(BF16)
        gu = _dot(x, wgu_bf[...]) + bgu_ref[0]
        g = jnp.minimum(gu[:, :de], SWIGLU_LIMIT)
        lin = jnp.clip(gu[:, de:], -SWIGLU_LIMIT, SWIGLU_LIMIT)
        act = g / (1.0 + jnp.exp(-SWIGLU_ALPHA * g)) * (lin + 1.0)
        y = _dot(act.astype(BF16), wd_bf[...]) + bd_ref[0]
        row = r0 + lax.broadcasted_iota(jnp.int32, (n_rows, 1), 0)
        mine = jnp.logical_and(row >= lo, row < hi)
        ys_ref[r0:r0 + n_rows, :] = jnp.where(mine, _pack_bf16_pairs(y), ys_ref[r0:r0 + n_rows, :])

    live = hi > lo
    first_only = jnp.logical_and(live, hi <= half)
    second_only = jnp.logical_and(live, lo >= half)

    @pl.when(jnp.logical_and(live, jnp.logical_not(jnp.logical_or(first_only, second_only))))
    def _():
        expert_rows(0, MOE_BLOCK)

    @pl.when(first_only)
    def _():
        expert_rows(0, half)

    @pl.when(second_only)
    def _():
        expert_rows(half, half)


def _moe_call(sched, xs, wgu, bgu, wd, bd):
    n_items = sched[0].shape[0]
    _, d, de2 = wgu.shape
    de = de2 // 2
    wmap = lambda i, blk, exp, lo, hi, cast, init, nxt: (exp[i], 0, 0)
    xmap = lambda i, blk, exp, lo, hi, cast, init, nxt: (blk[i], 0)
    return pl.pallas_call(
        _moe_kernel,
        grid_spec=pltpu.PrefetchScalarGridSpec(
            num_scalar_prefetch=7,
            grid=(n_items,),
            in_specs=[
                pl.BlockSpec((MOE_BLOCK, d // 2), xmap),
                pl.BlockSpec(memory_space=pl.ANY),
                pl.BlockSpec((1, 1, de2), wmap),
                pl.BlockSpec(memory_space=pl.ANY),
                pl.BlockSpec((1, 1, d), wmap),
            ],
            out_specs=pl.BlockSpec((MOE_BLOCK, d // 2), xmap),
            scratch_shapes=[
                pltpu.VMEM((d, de2), F32), pltpu.VMEM((de, d), F32),
                pltpu.VMEM((d, de2), BF16), pltpu.VMEM((de, d), BF16),
                pltpu.SemaphoreType.DMA((2,)),
            ],
        ),
        out_shape=jax.ShapeDtypeStruct(xs.shape, xs.dtype),
        compiler_params=pltpu.CompilerParams(
            dimension_semantics=("arbitrary",), vmem_limit_bytes=VMEM_LIMIT),
        name="moe",
    )(*sched, xs, wgu, bgu, wd, bd)


def _moe_schedule(counts, n_rows):
    n_blocks = n_rows // MOE_BLOCK
    n_items = n_blocks + N_EXPERTS
    ends = jnp.cumsum(counts)
    starts = ends - counts
    first_blk = starts // MOE_BLOCK
    last_blk = (ends - 1) // MOE_BLOCK
    items_per = jnp.where(counts > 0, last_blk - first_blk + 1, 0)
    item_ends = jnp.cumsum(items_per)
    item_starts = item_ends - items_per
    total = item_ends[-1]
    it = jnp.arange(n_items, dtype=jnp.int32)
    live = it < total
    itc = jnp.minimum(it, total - 1)
    exp = jnp.sum((item_ends[None, :] <= itc[:, None]).astype(jnp.int32), axis=1)
    is_exp = exp[:, None] == jnp.arange(N_EXPERTS, dtype=jnp.int32)[None, :]
    pick = lambda table: jnp.sum(jnp.where(is_exp, table[None, :], 0), axis=1)
    blk = pick(first_blk) + itc - pick(item_starts)
    lo = jnp.clip(pick(starts) - blk * MOE_BLOCK, 0, MOE_BLOCK)
    hi = jnp.clip(pick(ends) - blk * MOE_BLOCK, 0, MOE_BLOCK)
    hi = jnp.where(live, hi, lo)
    prev_exp = jnp.concatenate([jnp.full((1,), -1, jnp.int32), exp[:-1]])
    prev_blk = jnp.concatenate([jnp.full((1,), -1, jnp.int32), blk[:-1]])
    cast = jnp.logical_and(live, exp != prev_exp)
    init = jnp.logical_and(live, blk != prev_blk)
    ar = jnp.arange(N_EXPERTS, dtype=jnp.int32)
    later = jnp.logical_and(counts[None, :] > 0, ar[None, :] > ar[:, None])
    next_exp = jnp.min(jnp.where(later, ar[None, :], N_EXPERTS), axis=1)
    next_exp = jnp.where(next_exp == N_EXPERTS, -1, next_exp)
    nxt = pick(next_exp)
    as_i32 = lambda a: a.astype(jnp.int32)
    return tuple(as_i32(a) for a in (blk, exp, lo, hi, cast, init, nxt)), starts


def _combine_kernel(*refs):
    yg_refs = refs[:TOP_K]
    x1_ref, gate_ref, mod_ref, fg_ref, o_ref = refs[TOP_K:]
    gates = gate_ref[...]
    y = gates[:, 0:1] * _unpack_bf16_pairs(yg_refs[0][...])
    for kk in range(1, TOP_K):
        y = y + gates[:, kk:kk + 1] * _unpack_bf16_pairs(yg_refs[kk][...])
    m = mod_ref[0]
    x2 = x1_ref[...] + m[5:6] * y
    o_ref[...] = _rms(x2) * fg_ref[...]


def _combine_call(yg, x1, gates, mod, fg, n_seq, part):
    t, d = x1.shape
    tt = COMBINE_TT
    tiles_per_seq = n_seq // tt
    n_tiles = yg.shape[0] // (TOP_K * tt)
    first = part * n_tiles
    tok = lambda i: (first + i, 0)
    slot_specs = [pl.BlockSpec((tt, d // 2), functools.partial(lambda kk, i: (kk * n_tiles + i, 0), kk))
                  for kk in range(TOP_K)]
    return pl.pallas_call(
        _combine_kernel,
        grid=(n_tiles,),
        in_specs=slot_specs + [
            pl.BlockSpec((tt, d), tok),
            pl.BlockSpec((tt, LANES), tok),
            pl.BlockSpec((1, N_MOD, d), lambda i: ((first + i) // tiles_per_seq, 0, 0)),
            pl.BlockSpec((1, d), lambda i: (0, 0)),
        ],
        out_specs=pl.BlockSpec((tt, d), tok),
        out_shape=jax.ShapeDtypeStruct((t, d), F32),
        input_output_aliases={TOP_K: 0},
        compiler_params=pltpu.CompilerParams(
            dimension_semantics=("arbitrary",), vmem_limit_bytes=VMEM_LIMIT),
        name="combine",
    )(*([yg] * TOP_K), x1, gates, mod, fg)


def _pad_cols(a, width):
    return jnp.pad(a, ((0, 0), (0, width - a.shape[1])))


def _rope_slab(w_rope, swapped):
    half = QK_ROPE // 2
    if swapped:
        w_rope = jnp.concatenate([w_rope[:, half:], w_rope[:, :half]], axis=1)
    zeros = jnp.zeros((w_rope.shape[0], QK_NOPE), w_rope.dtype)
    return _pad_cols(jnp.concatenate([zeros, w_rope], axis=1), LANES)


def _prep_w_in(w_in):
    kr0 = Q_LORA + KV_LORA
    w_kr = w_in[:, kr0:kr0 + QK_ROPE]
    return jnp.concatenate(
        [w_in[:, :kr0], _rope_slab(w_kr, False), _rope_slab(w_kr, True), w_in[:, kr0 + QK_ROPE:]],
        axis=1).astype(BF16)


def _prep_w_uq(w_uq):
    per = QK_NOPE + QK_ROPE
    half = QK_ROPE // 2
    rows = w_uq.shape[0]
    w = w_uq.reshape(rows, N_HEADS, per)
    slab_a = jnp.pad(w, ((0, 0), (0, 0), (0, LANES - per)))
    rope = w[:, :, QK_NOPE:]
    swapped = jnp.concatenate([rope[:, :, half:], rope[:, :, :half]], axis=2)
    slab_b = jnp.pad(swapped, ((0, 0), (0, 0), (QK_NOPE, LANES - per)))
    both = jnp.concatenate([slab_a.reshape(rows, -1), slab_b.reshape(rows, -1)], axis=1)
    return both.T.astype(BF16)


def _prep_w_ukv(w_ukv):
    per = QK_NOPE + V_DIM
    rows = w_ukv.shape[0]
    w = w_ukv.reshape(rows, N_HEADS, per)
    wk = jnp.pad(w[:, :, :QK_NOPE], ((0, 0), (0, 0), (0, LANES - QK_NOPE))).reshape(rows, -1)
    wvt = jnp.pad(jnp.transpose(w[:, :, QK_NOPE:], (1, 2, 0)), ((0, 0), (0, VT_ROWS - V_DIM), (0, 0)))
    return wk.astype(BF16), wvt.reshape(N_HEADS * VT_ROWS, rows).astype(BF16)


def _rope_tables(n_lat):
    rows = n_lat // GRID_W
    nf = QK_ROPE // 4
    row = jnp.repeat(jnp.arange(rows, dtype=F32), GRID_W)
    col = jnp.tile(jnp.arange(GRID_W, dtype=F32), rows)
    freqs = ROPE_BASE ** (-jnp.arange(nf, dtype=F32) / nf)
    ang = jnp.concatenate([row[:, None] * freqs, col[:, None] * freqs], axis=-1)
    cos, sin = jnp.cos(ang), jnp.sin(ang)
    ones = jnp.ones((n_lat, QK_NOPE), F32)
    zeros = jnp.zeros((n_lat, QK_NOPE), F32)
    cs = _pad_cols(jnp.concatenate([ones, cos, cos], axis=1), LANES)
    sn = _pad_cols(jnp.concatenate([zeros, -sin, sin], axis=1), LANES)
    return cs, sn


def kernel(x, c, ctx, c_ctx, w_mod, b_mod, norm1_g, w_in, q_norm_g, kv_norm_g, w_uq, w_ukv, w_pool,
           pool_scale, w_out, norm2_g, router_w, router_b, w_gate_up, b_gate_up, w_down, b_down,
           final_g):
    bsz, n, d = x.shape
    n_ctx = ctx.shape[1]
    t = bsz * n
    l = 0

    cc = jnp.concatenate([c, c_ctx[None, :], jnp.zeros((8 - bsz - 1, d), F32)], axis=0)
    mod = _mod_call(cc, w_mod[l], b_mod[l][None, :]).reshape(8, N_MOD, d)
    mod_lat, mod_ctx = mod[:bsz], mod[bsz:bsz + 1]

    win = _prep_w_in(w_in[l])
    wuqt = _prep_w_uq(w_uq[l])
    wk, wvt = _prep_w_ukv(w_ukv[l])
    cs, sn = _rope_tables(n)
    cs_ctx = jnp.broadcast_to((jnp.arange(LANES) < QK_NOPE + QK_ROPE).astype(F32), (n_ctx, LANES))
    sn_ctx = jnp.zeros((n_ctx, LANES), F32)
    g1 = norm1_g[l][None, :]
    qg = q_norm_g[l][None, :]
    kvg = kv_norm_g[l][None, :]

    qt, k, vt, u = _front_call(x, mod_lat, True, g1, win, qg, kvg, wuqt, wk, wvt, cs, sn,
                               is_ctx=False, ts=FRONT_TS)
    kc, vct = _front_call(ctx, mod_ctx, False, g1, win[:, Q_LORA:Q_LORA + 3 * LANES], qg, kvg, wuqt,
                          wk, wvt, cs_ctx, sn_ctx, is_ctx=True, ts=n_ctx)
    att = _attn_call(qt, k, vt, kc, vct)

    rw = _pad_cols(router_w[l], LANES)
    rw_hi = rw.astype(BF16)
    rw = jnp.concatenate([rw_hi, (rw - rw_hi.astype(F32)).astype(BF16)], axis=1)
    rb = jnp.concatenate([router_b[l], jnp.full((LANES - N_EXPERTS,), -jnp.inf, F32)])[None, :]
    x1, h2, gates, ridx, cnt = _mix_call(
        att, u, x, mod_lat, w_pool[l].astype(BF16), pool_scale[l][None, :], w_out[l].astype(BF16),
        norm2_g[l][None, :], rw, rb)

    counts = cnt[0, :N_EXPERTS].astype(jnp.int32)
    sched, starts = _moe_schedule(counts, t * TOP_K)
    ridx = ridx.reshape(t, LANES)
    is_exp = ridx[:, :TOP_K, None] == jnp.arange(N_EXPERTS, dtype=jnp.int32)
    dest = jnp.sum(jnp.where(is_exp, starts, 0), axis=-1) + ridx[:, TOP_K:2 * TOP_K]
    dest_t = dest.T

    xs = _scatter_rows(h2.reshape(t, d // 2), dest_t.reshape(TOP_K, t // SC_ROWS, SC_ROWS), t * TOP_K)
    ys = _moe_call(sched, xs, w_gate_up[l], b_gate_up[l][:, None, :], w_down[l],
                   b_down[l][:, None, :])
    out = x1.reshape(t, d)
    gates = gates.reshape(t, LANES)
    tp = t // COMBINE_PARTS
    for part in range(COMBINE_PARTS):
        idx = dest_t[:, part * tp:(part + 1) * tp].reshape(-1)
        yg = _gather_rows(ys, idx)
        out = _combine_call(yg, out, gates, mod_lat, final_g[None, :], n, part)
    return out.reshape(bsz, n, d)
```

```python
import functools
import math

import jax
import jax.numpy as jnp
from jax import lax
from jax.experimental import pallas as pl
from jax.experimental.pallas import tpu as pltpu
from jax.experimental.pallas import tpu_sc as plsc

F32 = jnp.float32
BF16 = jnp.bfloat16
HIGHEST = lax.Precision.HIGHEST

N_HEADS = 8
QK_NOPE = 64
QK_ROPE = 32
V_DIM = 64
Q_LORA = 256
KV_LORA = 128
GRID_W = 64
ROPE_BASE = 10000.0
POOL_WINDOWS = (2, 4, 8, 16)
POOL_CH = 128
N_EXPERTS = 32
TOP_K = 4
SWIGLU_LIMIT = 7.0
SWIGLU_ALPHA = 1.702
MOE_BLOCK = 512
MOE_SUB = 128
N_MOD = 6
EPS = 1e-6

LANES = 128
POOL_HALO = 8
VT_ROWS = 80
VMEM_LIMIT = 56 * 1024 * 1024

FRONT_TS = 512
ATTN_TQ = 512
ATTN_TILES = 4
ATTN_KC = 512
MIX_TS = 512
COMBINE_TT = 256
COMBINE_PARTS = 4
SC_ROWS = 128


def _dot(a, b, **kw):
    return jnp.dot(a, b, preferred_element_type=F32, **kw)


def _dot_nt(a, b):
    return lax.dot_general(a, b, (((1,), (1,)), ((), ())), preferred_element_type=F32)


def _rms(x):
    return x * lax.rsqrt(jnp.mean(x * x, axis=-1, keepdims=True) + EPS)


def _pack_bf16_pairs(x):
    w = x.shape[1] // 2
    hi = lax.bitcast_convert_type(x[:, :w].astype(BF16).astype(F32), jnp.uint32)
    lo = lax.bitcast_convert_type(x[:, w:].astype(BF16).astype(F32), jnp.uint32)
    return hi | (lo >> 16)


def _unpack_bf16_pairs(words):
    hi = lax.bitcast_convert_type(words & jnp.uint32(0xFFFF0000), F32)
    lo = lax.bitcast_convert_type(words << 16, F32)
    return jnp.concatenate([hi, lo], axis=-1)


def _mod_kernel(c_ref, w_ref, b_ref, o_ref):
    c = c_ref[...]
    a = c / (1.0 + jnp.exp(-c))
    o_ref[...] = _dot(a, w_ref[...], precision=HIGHEST) + b_ref[...]


def _mod_call(cc, w_mod, b_mod):
    d = w_mod.shape[0]
    return pl.pallas_call(
        _mod_kernel,
        grid=(N_MOD,),
        in_specs=[
            pl.BlockSpec((8, d), lambda j: (0, 0)),
            pl.BlockSpec((d, d), lambda j: (0, j)),
            pl.BlockSpec((1, d), lambda j: (0, j)),
        ],
        out_specs=pl.BlockSpec((8, d), lambda j: (0, j)),
        out_shape=jax.ShapeDtypeStruct((8, N_MOD * d), F32),
        compiler_params=pltpu.CompilerParams(
            dimension_semantics=("arbitrary",), vmem_limit_bytes=VMEM_LIMIT),
        name="mod",
    )(cc, w_mod, b_mod)


def _front_kernel(x_ref, mod_ref, g1_ref, win_ref, qg_ref, kvg_ref, wuqt_ref, wk_ref, wvt_ref,
                  cs_ref, sn_ref, cst_ref, snt_ref, *out_refs, is_ctx, scale):
    x = x_ref[0]
    m = mod_ref[0]
    h = _rms(x) * g1_ref[...] * (1.0 + m[1:2]) + m[0:1]
    z = _dot(h.astype(BF16), win_ref[...])
    if is_ctx:
        k_ref, vt_ref = out_refs
        zkv = z
    else:
        qt_ref, k_ref, vt_ref, u_ref = out_refs
        zkv = z[:, Q_LORA:Q_LORA + 3 * LANES]
        u_ref[0] = z[:, Q_LORA + 3 * LANES:]

    ckv = _rms(zkv[:, :KV_LORA]) * kvg_ref[...]
    kk = _dot(ckv.astype(BF16), wk_ref[...])
    kr = zkv[:, LANES:2 * LANES] * cs_ref[...] + zkv[:, 2 * LANES:3 * LANES] * sn_ref[...]
    for hd in range(N_HEADS):
        k_ref[0, hd] = (kk[:, hd * LANES:(hd + 1) * LANES] + kr).astype(BF16)
    vt = _dot(wvt_ref[...], ckv.T.astype(BF16))
    row = lax.broadcasted_iota(jnp.int32, (VT_ROWS, 1), 0)
    ones = jnp.where(row == V_DIM, 1.0, 0.0)
    for hd in range(N_HEADS):
        vt_ref[0, hd] = (vt[hd * VT_ROWS:(hd + 1) * VT_ROWS] + ones).astype(BF16)

    if not is_ctx:
        cq = _rms(z[:, :Q_LORA]) * qg_ref[...]
        qqt = _dot(wuqt_ref[...], cq.T.astype(BF16))
        cst = cst_ref[...]
        snt = snt_ref[...]
        boff = N_HEADS * LANES
        for hd in range(N_HEADS):
            qa = qqt[hd * LANES:(hd + 1) * LANES]
            qb = qqt[boff + hd * LANES: boff + (hd + 1) * LANES]
            qt_ref[0, hd] = ((qa * cst + qb * snt) * scale).astype(BF16)


def _front_call(xs, mod, mod_per_batch, g1, win, qg, kvg, wuqt, wk, wvt, cs, sn, *, is_ctx, ts):
    bsz, n, d = xs.shape
    nt = n // ts
    scale = math.log2(math.e) / math.sqrt(QK_NOPE + QK_ROPE)
    const = lambda b, i: (0, 0)
    mod_map = (lambda b, i: (b, 0, 0)) if mod_per_batch else (lambda b, i: (0, 0, 0))
    in_specs = [
        pl.BlockSpec((1, ts, d), lambda b, i: (b, i, 0)),
        pl.BlockSpec((1, N_MOD, d), mod_map),
        pl.BlockSpec((1, d), const),
        pl.BlockSpec(win.shape, const),
        pl.BlockSpec((1, Q_LORA), const),
        pl.BlockSpec((1, KV_LORA), const),
        pl.BlockSpec(wuqt.shape, const),
        pl.BlockSpec(wk.shape, const),
        pl.BlockSpec(wvt.shape, const),
        pl.BlockSpec((ts, LANES), lambda b, i: (i, 0)),
        pl.BlockSpec((ts, LANES), lambda b, i: (i, 0)),
        pl.BlockSpec((LANES, ts), lambda b, i: (0, i)),
        pl.BlockSpec((LANES, ts), lambda b, i: (0, i)),
    ]
    k_spec = pl.BlockSpec((1, N_HEADS, ts, LANES), lambda b, i: (b, 0, i, 0))
    k_shape = jax.ShapeDtypeStruct((bsz, N_HEADS, n, LANES), BF16)
    qt_spec = pl.BlockSpec((1, N_HEADS, LANES, ts), lambda b, i: (b, 0, 0, i))
    qt_shape = jax.ShapeDtypeStruct((bsz, N_HEADS, LANES, n), BF16)
    vt_spec = pl.BlockSpec((1, N_HEADS, VT_ROWS, ts), lambda b, i: (b, 0, 0, i))
    vt_shape = jax.ShapeDtypeStruct((bsz, N_HEADS, VT_ROWS, n), BF16)
    if is_ctx:
        out_specs = [k_spec, vt_spec]
        out_shape = [k_shape, vt_shape]
    else:
        pool_w = win.shape[1] - Q_LORA - 3 * LANES
        out_specs = [qt_spec, k_spec, vt_spec, pl.BlockSpec((1, ts, pool_w), lambda b, i: (b, i, 0))]
        out_shape = [qt_shape, k_shape, vt_shape, jax.ShapeDtypeStruct((bsz, n, pool_w), F32)]
    return pl.pallas_call(
        functools.partial(_front_kernel, is_ctx=is_ctx, scale=scale),
        grid=(bsz, nt),
        in_specs=in_specs,
        out_specs=out_specs,
        out_shape=out_shape,
        compiler_params=pltpu.CompilerParams(
            dimension_semantics=("arbitrary", "arbitrary"), vmem_limit_bytes=VMEM_LIMIT),
        name="front_ctx" if is_ctx else "front",
    )(xs, mod, g1, win, qg, kvg, wuqt, wk, wvt, cs, sn, cs.T, sn.T)


def _attn_kernel(qt_ref, k_ref, vt_ref, kc_ref, vct_ref, o_ref, s_ref, mlc_ref, mxb_ref, oe_ref):
    n_ctx = kc_ref.shape[2]
    n_lat = k_ref.shape[2]
    chunks = [(None, 0, n_ctx)] + [(c * ATTN_KC, n_ctx + c * ATTN_KC, ATTN_KC)
                                   for c in range(n_lat // ATTN_KC)]
    sub = 8

    def q_cols(tile):
        if isinstance(tile, int):
            return pl.ds(tile * ATTN_TQ, ATTN_TQ)
        return pl.ds(pl.multiple_of(tile * ATTN_TQ, ATTN_TQ), ATTN_TQ)

    def score_chunk(tile, hd, buf, ci):
        off, soff, w = chunks[ci]
        keys = kc_ref[0, hd] if off is None else k_ref[0, hd, off:off + w, :]
        s = _dot(keys, qt_ref[0, hd, :, q_cols(tile)])
        s_ref[buf, soff:soff + w, :] = s
        mx = s[0:sub]
        for r in range(1, w // sub):
            mx = jnp.maximum(mx, s[r * sub:(r + 1) * sub])
        mlc_ref[buf, ci] = mx

    def row_max(buf):
        mx = mlc_ref[buf, 0]
        for ci in range(1, len(chunks)):
            mx = jnp.maximum(mx, mlc_ref[buf, ci])
        mxb_ref[...] = jnp.broadcast_to(jnp.max(mx, axis=0, keepdims=True), mx.shape)

    def weight_chunk(hd, buf, ci):
        off, soff, w = chunks[ci]
        p = jnp.exp2(s_ref[buf, soff:soff + w, :] - mxb_ref[0:1, :]).astype(BF16)
        vt = vct_ref[0, hd] if off is None else vt_ref[0, hd, :, off:off + w]
        return _dot(vt, p)

    def stage(hw, bw, scoring, bs):
        row_max(bw)
        acc = None
        for ci in range(len(chunks)):
            if scoring is not None:
                score_chunk(*scoring, bs, ci)
            part = weight_chunk(hw, bw, ci)
            acc = part if acc is None else acc + part
        return acc[0:V_DIM] / acc[V_DIM:V_DIM + 1]

    def write_pair(tile, j, ot_odd):
        pair_t = jnp.concatenate([oe_ref[...], ot_odd], axis=0)
        o_ref[0, j, q_cols(tile), :] = pair_t.T.astype(BF16)

    pairs_per_tile = N_HEADS // 2
    n_pairs = (qt_ref.shape[3] // ATTN_TQ) * pairs_per_tile

    def split(p):
        return p // pairs_per_tile, p % pairs_per_tile

    for ci in range(len(chunks)):
        score_chunk(0, 0, 0, ci)

    def head_pair(p, carry):
        tile, j = split(p)
        nxt_tile, nxt_j = split(p + 1)
        oe_ref[...] = stage(2 * j, 0, (tile, 2 * j + 1), 1)
        write_pair(tile, j, stage(2 * j + 1, 1, (nxt_tile, 2 * nxt_j), 0))
        return carry

    lax.fori_loop(0, n_pairs - 1, head_pair, 0)
    tile, j = split(n_pairs - 1)
    oe_ref[...] = stage(2 * j, 0, (tile, 2 * j + 1), 1)
    write_pair(tile, j, stage(2 * j + 1, 1, None, None))


def _attn_call(qt, k, vt, kc, vct):
    bsz, _, _, n = qt.shape
    n_ctx = kc.shape[2]
    tq = ATTN_TQ
    n_chunks = 1 + n // ATTN_KC
    per_batch = lambda b, i: (b, 0, 0, 0)
    resident = dict(pipeline_mode=pl.Buffered(1))
    rows_per_step = ATTN_TILES * tq
    return pl.pallas_call(
        _attn_kernel,
        grid=(bsz, n // rows_per_step),
        in_specs=[
            pl.BlockSpec((1, N_HEADS, LANES, rows_per_step), lambda b, i: (b, 0, 0, i)),
            pl.BlockSpec((1, N_HEADS, n, LANES), per_batch, **resident),
            pl.BlockSpec((1, N_HEADS, VT_ROWS, n), per_batch, **resident),
            pl.BlockSpec((1, N_HEADS, n_ctx, LANES), per_batch, **resident),
            pl.BlockSpec((1, N_HEADS, VT_ROWS, n_ctx), per_batch, **resident),
        ],
        out_specs=pl.BlockSpec((1, N_HEADS // 2, rows_per_step, LANES), lambda b, i: (b, 0, i, 0)),
        out_shape=jax.ShapeDtypeStruct((bsz, N_HEADS // 2, n, LANES), BF16),
        scratch_shapes=[
            pltpu.VMEM((2, n_ctx + n, tq), F32),
            pltpu.VMEM((2, n_chunks, 8, tq), F32),
            pltpu.VMEM((8, tq), F32),
            pltpu.VMEM((V_DIM, tq), F32),
        ],
        compiler_params=pltpu.CompilerParams(
            dimension_semantics=("arbitrary", "arbitrary"), vmem_limit_bytes=VMEM_LIMIT),
        name="attn",
    )(qt, k, vt, kc, vct)


def _mix_kernel(att_ref, u_ref, up_ref, un_ref, x_ref, mod_ref, wpool_ref, pscale_ref, wout_ref,
                g2_ref, rw_ref, rb_ref,
                x1_ref, h2_ref, gate_ref, ridx_ref, cnt_ref, ue_ref, base_ref, *, n_seq):
    b = pl.program_id(0)
    i = pl.program_id(1)
    n_tiles = pl.num_programs(1)
    ts = u_ref.shape[1]

    @pl.when(jnp.logical_and(b == 0, i == 0))
    def _():
        base_ref[...] = jnp.zeros_like(base_ref)

    u = u_ref[0]
    ue_ref[0:POOL_HALO] = jnp.where(i > 0, up_ref[0], 0.0)
    ue_ref[POOL_HALO:POOL_HALO + ts] = u
    ue_ref[POOL_HALO + ts:2 * POOL_HALO + ts] = jnp.where(i < n_tiles - 1, un_ref[0], 0.0)
    t = i * ts + lax.broadcasted_iota(jnp.int32, (ts, 1), 0)
    ys = []
    for g, w in enumerate(POOL_WINDOWS):
        half = w // 2
        lanes = slice(g * POOL_CH, (g + 1) * POOL_CH)
        ws = ue_ref[POOL_HALO - half:POOL_HALO - half + ts, lanes]
        for jj in range(-half + 1, half):
            ws = ws + ue_ref[POOL_HALO + jj:POOL_HALO + jj + ts, lanes]
        count = (jnp.minimum(t + half, n_seq) - jnp.maximum(t - half, 0)).astype(F32)
        mixed = (ws / count - u[:, lanes]).astype(BF16)
        ys.append(_dot(mixed, wpool_ref[g]))
    pool = jnp.concatenate(ys, axis=-1) * pscale_ref[...]

    cat = jnp.concatenate([att_ref[0, j] for j in range(N_HEADS // 2)] + [pool.astype(BF16)], axis=-1)
    m = mod_ref[0]
    x1 = x_ref[0] + m[2:3] * _dot(cat, wout_ref[...])
    x1_ref[0] = x1
    h2 = _rms(x1) * g2_ref[...] * (1.0 + m[4:5]) + m[3:4]
    h2_ref[0] = _pack_bf16_pairs(h2)

    h_hi = h2.astype(BF16)
    h_lo = (h2 - h_hi.astype(F32)).astype(BF16)
    hi_part = _dot(h_hi, rw_ref[...])
    logits = (hi_part[:, :LANES] + hi_part[:, LANES:] + _dot(h_lo, rw_ref[:, :LANES])) + rb_ref[...]
    lane = lax.broadcasted_iota(jnp.int32, logits.shape, 1).astype(F32)
    vals, idxs = [], []
    for _k in range(TOP_K):
        mv = jnp.max(logits, axis=-1, keepdims=True)
        ix = jnp.min(jnp.where(logits == mv, lane, float(LANES)), axis=-1, keepdims=True)
        vals.append(mv)
        idxs.append(ix)
        logits = jnp.where(lane == ix, -jnp.inf, logits)
    es = [jnp.exp(v - vals[0]) for v in vals]
    den = es[0] + es[1] + es[2] + es[3]

    onehot = jnp.zeros(lane.shape, F32)
    for ix in idxs:
        onehot = onehot + jnp.where(lane == ix, 1.0, 0.0)
    row = lax.broadcasted_iota(jnp.int32, (ts, ts), 0)
    col = lax.broadcasted_iota(jnp.int32, (ts, ts), 1)
    tri = jnp.where(col < row, 1.0, 0.0).astype(BF16)
    before = _dot(tri, onehot.astype(BF16)) + base_ref[0:1, :]
    base_new = base_ref[0:1, :] + jnp.sum(onehot, axis=0, keepdims=True)
    base_ref[...] = jnp.broadcast_to(base_new, base_ref.shape)
    cnt_ref[...] = jnp.broadcast_to(base_new, cnt_ref.shape)

    gate_out = jnp.zeros(lane.shape, F32)
    ridx_out = jnp.zeros(lane.shape, F32)
    for kk in range(TOP_K):
        rank = jnp.sum(jnp.where(lane == idxs[kk], before, 0.0), axis=-1, keepdims=True)
        gate_out = jnp.where(lane == float(kk), es[kk] / den, gate_out)
        ridx_out = jnp.where(lane == float(kk), idxs[kk], ridx_out)
        ridx_out = jnp.where(lane == float(TOP_K + kk), rank, ridx_out)
    gate_ref[0] = gate_out
    ridx_ref[0] = ridx_out.astype(jnp.int32)


def _mix_call(att, u, x, mod, wpool, pscale, wout, g2, rw, rb):
    bsz, n, d = x.shape
    ts = MIX_TS
    nt = n // ts
    pool_w = u.shape[2]
    hb = ts // POOL_HALO
    n_halo_blocks = n // POOL_HALO
    const2 = lambda b, i: (0, 0)
    tok = lambda b, i: (b, i, 0)
    return pl.pallas_call(
        functools.partial(_mix_kernel, n_seq=n),
        grid=(bsz, nt),
        in_specs=[
            pl.BlockSpec((1, N_HEADS // 2, ts, LANES), lambda b, i: (b, 0, i, 0)),
            pl.BlockSpec((1, ts, pool_w), tok),
            pl.BlockSpec((1, POOL_HALO, pool_w), lambda b, i: (b, jnp.maximum(i * hb - 1, 0), 0)),
            pl.BlockSpec((1, POOL_HALO, pool_w),
                         lambda b, i: (b, jnp.minimum((i + 1) * hb, n_halo_blocks - 1), 0)),
            pl.BlockSpec((1, ts, d), tok),
            pl.BlockSpec((1, N_MOD, d), lambda b, i: (b, 0, 0)),
            pl.BlockSpec(wpool.shape, lambda b, i: (0, 0, 0)),
            pl.BlockSpec((1, pool_w), const2),
            pl.BlockSpec(wout.shape, const2),
            pl.BlockSpec((1, d), const2),
            pl.BlockSpec(rw.shape, const2),
            pl.BlockSpec((1, LANES), const2),
        ],
        out_specs=[
            pl.BlockSpec((1, ts, d), tok),
            pl.BlockSpec((1, ts, d // 2), tok),
            pl.BlockSpec((1, ts, LANES), tok),
            pl.BlockSpec((1, ts, LANES), tok),
            pl.BlockSpec((8, LANES), const2),
        ],
        out_shape=[
            jax.ShapeDtypeStruct((bsz, n, d), F32),
            jax.ShapeDtypeStruct((bsz, n, d // 2), jnp.uint32),
            jax.ShapeDtypeStruct((bsz, n, LANES), F32),
            jax.ShapeDtypeStruct((bsz, n, LANES), jnp.int32),
            jax.ShapeDtypeStruct((8, LANES), F32),
        ],
        scratch_shapes=[
            pltpu.VMEM((ts + 2 * POOL_HALO, pool_w), F32),
            pltpu.VMEM((8, LANES), F32),
        ],
        compiler_params=pltpu.CompilerParams(
            dimension_semantics=("arbitrary", "arbitrary"), vmem_limit_bytes=VMEM_LIMIT),
        name="mix",
    )(att, u, u, u, x, mod, wpool, pscale, wout, g2, rw, rb)


def _sc_mesh():
    return plsc.VectorSubcoreMesh(core_axis_name="core", subcore_axis_name="subcore")


def _sc_worker_id():
    info = plsc.get_sparse_core_info()
    return lax.axis_index("subcore") * info.num_cores + lax.axis_index("core")


def _sc_num_workers():
    info = plsc.get_sparse_core_info()
    return info.num_cores * info.num_subcores


def _scatter_rows(x, dest3, n_out):
    t, d = x.shape
    top_k, n_chunks, rows = dest3.shape
    per_worker = n_chunks // _sc_num_workers()

    @functools.partial(
        pl.kernel,
        out_type=jax.ShapeDtypeStruct((n_out, d), x.dtype),
        mesh=_sc_mesh(),
        scratch_types=[pltpu.VMEM((top_k, rows), jnp.int32), pltpu.VMEM((rows, d), x.dtype)],
        name="sc_scatter",
    )
    def scatter(x_hbm, i_hbm, o_hbm, idx_v, rows_v):
        wid = _sc_worker_id()

        @pl.loop(0, per_worker)
        def _(c):
            chunk = wid * per_worker + c
            for kk in range(top_k):
                pltpu.sync_copy(i_hbm.at[kk, chunk], idx_v.at[kk])
            pltpu.sync_copy(x_hbm.at[pl.ds(pl.multiple_of(chunk * rows, 8), rows)], rows_v)
            for kk in range(top_k):
                pltpu.sync_copy(rows_v, o_hbm.at[idx_v.at[kk]])

    return scatter(x, dest3)


def _gather_rows(y, idx):
    n = idx.shape[0]
    d = y.shape[1]
    rows = SC_ROWS
    per_worker = n // _sc_num_workers()
    n_chunks = per_worker // rows

    @functools.partial(
        pl.kernel,
        out_type=jax.ShapeDtypeStruct((n, d), y.dtype),
        mesh=_sc_mesh(),
        scratch_types=[pltpu.VMEM((rows,), jnp.int32), pltpu.VMEM((rows, d), y.dtype)],
        name="sc_gather",
    )
    def gather(y_hbm, i_hbm, o_hbm, idx_v, rows_v):
        base = _sc_worker_id() * per_worker

        @pl.loop(0, n_chunks)
        def _(c):
            off = pl.multiple_of(base + c * rows, 8)
            pltpu.sync_copy(i_hbm.at[pl.ds(off, rows)], idx_v)
            pltpu.sync_copy(y_hbm.at[idx_v], rows_v)
            pltpu.sync_copy(rows_v, o_hbm.at[pl.ds(off, rows)])

    return gather(y, idx)


def _moe_kernel(blk_ref, exp_ref, lo_ref, hi_ref, cast_ref, init_ref, next_ref, xs_ref, wgu_hbm,
                bgu_ref, wd_hbm, bd_ref, ys_ref, wgu_f32, wd_f32, wgu_bf, wd_bf, sems):
    i = pl.program_id(0)

    def weight_copies(e):
        return (pltpu.make_async_copy(wgu_hbm.at[e], wgu_f32, sems.at[0]),
                pltpu.make_async_copy(wd_hbm.at[e], wd_f32, sems.at[1]))

    @pl.when(i == 0)
    def _():
        for cp in weight_copies(exp_ref[0]):
            cp.start()

    @pl.when(cast_ref[i] == 1)
    def _():
        for cp in weight_copies(exp_ref[i]):
            cp.wait()
        wgu_bf[...] = wgu_f32[...].astype(BF16)
        wd_bf[...] = wd_f32[...].astype(BF16)

        @pl.when(next_ref[i] >= 0)
        def _():
            for cp in weight_copies(next_ref[i]):
                cp.start()

    @pl.when(init_ref[i] == 1)
    def _():
        ys_ref[...] = jnp.zeros_like(ys_ref)

    lo = lo_ref[i]
    hi = hi_ref[i]

    def expert_rows(r0, n_rows):
        de = wd_bf.shape[0]
        x = _unpack_bf16_pairs(xs_ref[pl.ds(r0, n_rows), :]).astype(BF16)
        gu = _dot(x, wgu_bf[...]) + bgu_ref[0]
        g = jnp.minimum(gu[:, :de], SWIGLU_LIMIT)
        lin = jnp.clip(gu[:, de:], -SWIGLU_LIMIT, SWIGLU_LIMIT)
        act = g / (1.0 + jnp.exp(-SWIGLU_ALPHA * g)) * (lin + 1.0)
        y = _dot(act.astype(BF16), wd_bf[...]) + bd_ref[0]
        row = r0 + lax.broadcasted_iota(jnp.int32, (n_rows, 1), 0)
        mine = jnp.logical_and(row >= lo, row < hi)
        ys_ref[pl.ds(r0, n_rows), :] = jnp.where(mine, _pack_bf16_pairs(y), ys_ref[pl.ds(r0, n_rows), :])

    first = lo // MOE_SUB
    pieces = jnp.where(hi > lo, (hi + MOE_SUB - 1) // MOE_SUB - first, 0)
    for cnt in range(1, MOE_BLOCK // MOE_SUB + 1):
        @pl.when(pieces == cnt)
        def _(cnt=cnt):
            r0 = 0 if cnt * MOE_SUB == MOE_BLOCK else pl.multiple_of(first * MOE_SUB, MOE_SUB)
            expert_rows(r0, cnt * MOE_SUB)


def _moe_call(sched, xs, wgu, bgu, wd, bd):
    n_items = sched[0].shape[0]
    _, d, de2 = wgu.shape
    de = de2 // 2
    wmap = lambda i, blk, exp, lo, hi, cast, init, nxt: (exp[i], 0, 0)
    xmap = lambda i, blk, exp, lo, hi, cast, init, nxt: (blk[i], 0)
    return pl.pallas_call(
        _moe_kernel,
        grid_spec=pltpu.PrefetchScalarGridSpec(
            num_scalar_prefetch=7,
            grid=(n_items,),
            in_specs=[
                pl.BlockSpec((MOE_BLOCK, d // 2), xmap),
                pl.BlockSpec(memory_space=pl.ANY),
                pl.BlockSpec((1, 1, de2), wmap),
                pl.BlockSpec(memory_space=pl.ANY),
                pl.BlockSpec((1, 1, d), wmap),
            ],
            out_specs=pl.BlockSpec((MOE_BLOCK, d // 2), xmap),
            scratch_shapes=[
                pltpu.VMEM((d, de2), F32), pltpu.VMEM((de, d), F32),
                pltpu.VMEM((d, de2), BF16), pltpu.VMEM((de, d), BF16),
                pltpu.SemaphoreType.DMA((2,)),
            ],
        ),
        out_shape=jax.ShapeDtypeStruct(xs.shape, xs.dtype),
        compiler_params=pltpu.CompilerParams(
            dimension_semantics=("arbitrary",), vmem_limit_bytes=VMEM_LIMIT),
        name="moe",
    )(*sched, xs, wgu, bgu, wd, bd)


def _moe_schedule(counts, n_rows):
    n_blocks = n_rows // MOE_BLOCK
    n_items = n_blocks + N_EXPERTS
    ends = jnp.cumsum(counts)
    starts = ends - counts
    first_blk = starts // MOE_BLOCK
    last_blk = (ends - 1) // MOE_BLOCK
    items_per = jnp.where(counts > 0, last_blk - first_blk + 1, 0)
    item_ends = jnp.cumsum(items_per)
    item_starts = item_ends - items_per
    total = item_ends[-1]
    it = jnp.arange(n_items, dtype=jnp.int32)
    live = it < total
    itc = jnp.minimum(it, total - 1)
    exp = jnp.sum((item_ends[None, :] <= itc[:, None]).astype(jnp.int32), axis=1)
    is_exp = exp[:, None] == jnp.arange(N_EXPERTS, dtype=jnp.int32)[None, :]
    pick = lambda table: jnp.sum(jnp.where(is_exp, table[None, :], 0), axis=1)
    blk = pick(first_blk) + itc - pick(item_starts)
    lo = jnp.clip(pick(starts) - blk * MOE_BLOCK, 0, MOE_BLOCK)
    hi = jnp.clip(pick(ends) - blk * MOE_BLOCK, 0, MOE_BLOCK)
    hi = jnp.where(live, hi, lo)
    prev_exp = jnp.concatenate([jnp.full((1,), -1, jnp.int32), exp[:-1]])
    prev_blk = jnp.concatenate([jnp.full((1,), -1, jnp.int32), blk[:-1]])
    cast = jnp.logical_and(live, exp != prev_exp)
    init = jnp.logical_and(live, blk != prev_blk)
    ar = jnp.arange(N_EXPERTS, dtype=jnp.int32)
    later = jnp.logical_and(counts[None, :] > 0, ar[None, :] > ar[:, None])
    next_exp = jnp.min(jnp.where(later, ar[None, :], N_EXPERTS), axis=1)
    next_exp = jnp.where(next_exp == N_EXPERTS, -1, next_exp)
    nxt = pick(next_exp)
    as_i32 = lambda a: a.astype(jnp.int32)
    return tuple(as_i32(a) for a in (blk, exp, lo, hi, cast, init, nxt)), starts


def _combine_kernel(*refs):
    yg_refs = refs[:TOP_K]
    x1_ref, gate_ref, mod_ref, fg_ref, o_ref = refs[TOP_K:]
    gates = gate_ref[...]
    y = gates[:, 0:1] * _unpack_bf16_pairs(yg_refs[0][...])
    for kk in range(1, TOP_K):
        y = y + gates[:, kk:kk + 1] * _unpack_bf16_pairs(yg_refs[kk][...])
    m = mod_ref[0]
    x2 = x1_ref[...] + m[5:6] * y
    o_ref[...] = _rms(x2) * fg_ref[...]


def _combine_call(yg, x1, gates, mod, fg, n_seq, part):
    t, d = x1.shape
    tt = COMBINE_TT
    tiles_per_seq = n_seq // tt
    n_tiles = yg.shape[0] // (TOP_K * tt)
    first = part * n_tiles
    tok = lambda i: (first + i, 0)
    slot_specs = [pl.BlockSpec((tt, d // 2), functools.partial(lambda kk, i: (kk * n_tiles + i, 0), kk))
                  for kk in range(TOP_K)]
    return pl.pallas_call(
        _combine_kernel,
        grid=(n_tiles,),
        in_specs=slot_specs + [
            pl.BlockSpec((tt, d), tok),
            pl.BlockSpec((tt, LANES), tok),
            pl.BlockSpec((1, N_MOD, d), lambda i: ((first + i) // tiles_per_seq, 0, 0)),
            pl.BlockSpec((1, d), lambda i: (0, 0)),
        ],
        out_specs=pl.BlockSpec((tt, d), tok),
        out_shape=jax.ShapeDtypeStruct((t, d), F32),
        input_output_aliases={TOP_K: 0},
        compiler_params=pltpu.CompilerParams(
            dimension_semantics=("arbitrary",), vmem_limit_bytes=VMEM_LIMIT),
        name="combine",
    )(*([yg] * TOP_K), x1, gates, mod, fg)


def _pad_cols(a, width):
    return jnp.pad(a, ((0, 0), (0, width - a.shape[1])))


def _rope_slab(w_rope, swapped):
    half = QK_ROPE // 2
    if swapped:
        w_rope = jnp.concatenate([w_rope[:, half:], w_rope[:, :half]], axis=1)
    zeros = jnp.zeros((w_rope.shape[0], QK_NOPE), w_rope.dtype)
    return _pad_cols(jnp.concatenate([zeros, w_rope], axis=1), LANES)


def _prep_w_in(w_in):
    kr0 = Q_LORA + KV_LORA
    w_kr = w_in[:, kr0:kr0 + QK_ROPE]
    return jnp.concatenate(
        [w_in[:, :kr0], _rope_slab(w_kr, False), _rope_slab(w_kr, True), w_in[:, kr0 + QK_ROPE:]],
        axis=1).astype(BF16)


def _prep_w_uq(w_uq):
    per = QK_NOPE + QK_ROPE
    half = QK_ROPE // 2
    rows = w_uq.shape[0]
    w = w_uq.reshape(rows, N_HEADS, per)
    slab_a = jnp.pad(w, ((0, 0), (0, 0), (0, LANES - per)))
    rope = w[:, :, QK_NOPE:]
    swapped = jnp.concatenate([rope[:, :, half:], rope[:, :, :half]], axis=2)
    slab_b = jnp.pad(swapped, ((0, 0), (0, 0), (QK_NOPE, LANES - per)))
    both = jnp.concatenate([slab_a.reshape(rows, -1), slab_b.reshape(rows, -1)], axis=1)
    return both.T.astype(BF16)


def _prep_w_ukv(w_ukv):
    per = QK_NOPE + V_DIM
    rows = w_ukv.shape[0]
    w = w_ukv.reshape(rows, N_HEADS, per)
    wk = jnp.pad(w[:, :, :QK_NOPE], ((0, 0), (0, 0), (0, LANES - QK_NOPE))).reshape(rows, -1)
    wvt = jnp.pad(jnp.transpose(w[:, :, QK_NOPE:], (1, 2, 0)), ((0, 0), (0, VT_ROWS - V_DIM), (0, 0)))
    return wk.astype(BF16), wvt.reshape(N_HEADS * VT_ROWS, rows).astype(BF16)


def _rope_tables(n_lat):
    rows = n_lat // GRID_W
    nf = QK_ROPE // 4
    row = jnp.repeat(jnp.arange(rows, dtype=F32), GRID_W)
    col = jnp.tile(jnp.arange(GRID_W, dtype=F32), rows)
    freqs = ROPE_BASE ** (-jnp.arange(nf, dtype=F32) / nf)
    ang = jnp.concatenate([row[:, None] * freqs, col[:, None] * freqs], axis=-1)
    cos, sin = jnp.cos(ang), jnp.sin(ang)
    ones = jnp.ones((n_lat, QK_NOPE), F32)
    zeros = jnp.zeros((n_lat, QK_NOPE), F32)
    cs = _pad_cols(jnp.concatenate([ones, cos, cos], axis=1), LANES)
    sn = _pad_cols(jnp.concatenate([zeros, -sin, sin], axis=1), LANES)
    return cs, sn


def kernel(x, c, ctx, c_ctx, w_mod, b_mod, norm1_g, w_in, q_norm_g, kv_norm_g, w_uq, w_ukv, w_pool,
           pool_scale, w_out, norm2_g, router_w, router_b, w_gate_up, b_gate_up, w_down, b_down,
           final_g):
    bsz, n, d = x.shape
    n_ctx = ctx.shape[1]
    t = bsz * n
    l = 0

    cc = jnp.concatenate([c, c_ctx[None, :], jnp.zeros((8 - bsz - 1, d), F32)], axis=0)
    mod = _mod_call(cc, w_mod[l], b_mod[l][None, :]).reshape(8, N_MOD, d)
    mod_lat, mod_ctx = mod[:bsz], mod[bsz:bsz + 1]

    win = _prep_w_in(w_in[l])
    wuqt = _prep_w_uq(w_uq[l])
    wk, wvt = _prep_w_ukv(w_ukv[l])
    cs, sn = _rope_tables(n)
    cs_ctx = jnp.broadcast_to((jnp.arange(LANES) < QK_NOPE + QK_ROPE).astype(F32), (n_ctx, LANES))
    sn_ctx = jnp.zeros((n_ctx, LANES), F32)
    g1 = norm1_g[l][None, :]
    qg = q_norm_g[l][None, :]
    kvg = kv_norm_g[l][None, :]

    qt, k, vt, u = _front_call(x, mod_lat, True, g1, win, qg, kvg, wuqt, wk, wvt, cs, sn,
                               is_ctx=False, ts=FRONT_TS)
    kc, vct = _front_call(ctx, mod_ctx, False, g1, win[:, Q_LORA:Q_LORA + 3 * LANES], qg, kvg, wuqt,
                          wk, wvt, cs_ctx, sn_ctx, is_ctx=True, ts=n_ctx)
    att = _attn_call(qt, k, vt, kc, vct)

    rw = _pad_cols(router_w[l], LANES)
    rw_hi = rw.astype(BF16)
    rw = jnp.concatenate([rw_hi, (rw - rw_hi.astype(F32)).astype(BF16)], axis=1)
    rb = jnp.concatenate([router_b[l], jnp.full((LANES - N_EXPERTS,), -jnp.inf, F32)])[None, :]
    x1, h2, gates, ridx, cnt = _mix_call(
        att, u, x, mod_lat, w_pool[l].astype(BF16), pool_scale[l][None, :], w_out[l].astype(BF16),
        norm2_g[l][None, :], rw, rb)

    counts = cnt[0, :N_EXPERTS].astype(jnp.int32)
    sched, starts = _moe_schedule(counts, t * TOP_K)
    ridx = ridx.reshape(t, LANES)
    is_exp = ridx[:, :TOP_K, None] == jnp.arange(N_EXPERTS, dtype=jnp.int32)
    dest = jnp.sum(jnp.where(is_exp, starts, 0), axis=-1) + ridx[:, TOP_K:2 * TOP_K]
    dest_t = dest.T

    xs = _scatter_rows(h2.reshape(t, d // 2), dest_t.reshape(TOP_K, t // SC_ROWS, SC_ROWS), t * TOP_K)
    ys = _moe_call(sched, xs, w_gate_up[l], b_gate_up[l][:, None, :], w_down[l],
                   b_down[l][:, None, :])
    out = x1.reshape(t, d)
    gates = gates.reshape(t, LANES)
    tp = t // COMBINE_PARTS
    for part in range(COMBINE_PARTS):
        idx = dest_t[:, part * tp:(part + 1) * tp].reshape(-1)
        yg = _gather_rows(ys, idx)
        out = _combine_call(yg, out, gates, mod_lat, final_g[None, :], n, part)
    return out.reshape(bsz, n, d)
```

```python
import functools
import math

import jax
import jax.numpy as jnp
from jax import lax
from jax.experimental import pallas as pl
from jax.experimental.pallas import tpu as pltpu
from jax.experimental.pallas import tpu_sc as plsc

F32 = jnp.float32
BF16 = jnp.bfloat16
HIGHEST = lax.Precision.HIGHEST

N_HEADS = 8
QK_NOPE = 64
QK_ROPE = 32
V_DIM = 64
Q_LORA = 256
KV_LORA = 128
GRID_W = 64
ROPE_BASE = 10000.0
POOL_WINDOWS = (2, 4, 8, 16)
POOL_CH = 128
N_EXPERTS = 32
TOP_K = 4
SWIGLU_LIMIT = 7.0
SWIGLU_ALPHA = 1.702
MOE_BLOCK = 512
MOE_SUB = 128
N_MOD = 6
EPS = 1e-6

LANES = 128
POOL_HALO = 8
VT_ROWS = 80
VMEM_LIMIT = 56 * 1024 * 1024

FRONT_TS = 512
ATTN_TQ = 512
ATTN_TILES = 4
ATTN_KC = 512
MIX_TS = 512
COMBINE_TT = 256
COMBINE_PARTS = 4
SC_ROWS = 128


def _dot(a, b, **kw):
    return jnp.dot(a, b, preferred_element_type=F32, **kw)


def _dot_nt(a, b):
    return lax.dot_general(a, b, (((1,), (1,)), ((), ())), preferred_element_type=F32)


def _rms(x):
    return x * lax.rsqrt(jnp.mean(x * x, axis=-1, keepdims=True) + EPS)


def _pack_bf16_pairs(x):
    w = x.shape[1] // 2
    hi = lax.bitcast_convert_type(x[:, :w].astype(BF16).astype(F32), jnp.uint32)
    lo = lax.bitcast_convert_type(x[:, w:].astype(BF16).astype(F32), jnp.uint32)
    return hi | (lo >> 16)


def _unpack_bf16_pairs(words):
    hi = lax.bitcast_convert_type(words & jnp.uint32(0xFFFF0000), F32)
    lo = lax.bitcast_convert_type(words << 16, F32)
    return jnp.concatenate([hi, lo], axis=-1)


def _mod_kernel(c_ref, w_ref, b_ref, o_ref):
    c = c_ref[...]
    a = c / (1.0 + jnp.exp(-c))
    o_ref[...] = _dot(a, w_ref[...], precision=HIGHEST) + b_ref[...]


def _mod_call(cc, w_mod, b_mod):
    d = w_mod.shape[0]
    return pl.pallas_call(
        _mod_kernel,
        grid=(N_MOD,),
        in_specs=[
            pl.BlockSpec((8, d), lambda j: (0, 0)),
            pl.BlockSpec((d, d), lambda j: (0, j)),
            pl.BlockSpec((1, d), lambda j: (0, j)),
        ],
        out_specs=pl.BlockSpec((8, d), lambda j: (0, j)),
        out_shape=jax.ShapeDtypeStruct((8, N_MOD * d), F32),
        compiler_params=pltpu.CompilerParams(
            dimension_semantics=("arbitrary",), vmem_limit_bytes=VMEM_LIMIT),
        name="mod",
    )(cc, w_mod, b_mod)


def _front_kernel(x_ref, mod_ref, g1_ref, win_ref, qg_ref, kvg_ref, wuqt_ref, wk_ref, wvt_ref,
                  cs_ref, sn_ref, cst_ref, snt_ref, *out_refs, is_ctx, scale):
    x = x_ref[0]
    m = mod_ref[0]
    h = _rms(x) * g1_ref[...] * (1.0 + m[1:2]) + m[0:1]
    z = _dot(h.astype(BF16), win_ref[...])
    if is_ctx:
        k_ref, vt_ref = out_refs
        zkv = z
    else:
        qt_ref, k_ref, vt_ref, u_ref = out_refs
        zkv = z[:, Q_LORA:Q_LORA + 2 * LANES]
        u_ref[0] = z[:, Q_LORA + 2 * LANES:]

    ckv = _rms(zkv[:, :KV_LORA]) * kvg_ref[...]
    kk = _dot(ckv.astype(BF16), wk_ref[...])
    kra = zkv[:, LANES:2 * LANES]
    lane = lax.broadcasted_iota(jnp.int32, kra.shape, 1)
    half = QK_ROPE // 2
    partner = jnp.where(lane < QK_NOPE + half, pltpu.roll(kra, LANES - half, axis=1),
                        pltpu.roll(kra, half, axis=1))
    kr = kra * cs_ref[...] + partner * sn_ref[...]
    for hd in range(N_HEADS):
        k_ref[0, hd] = (kk[:, hd * LANES:(hd + 1) * LANES] + kr).astype(BF16)
    vt = _dot(wvt_ref[...], ckv.T.astype(BF16))
    row = lax.broadcasted_iota(jnp.int32, (VT_ROWS, 1), 0)
    ones = jnp.where(row == V_DIM, 1.0, 0.0)
    for hd in range(N_HEADS):
        vt_ref[0, hd] = (vt[hd * VT_ROWS:(hd + 1) * VT_ROWS] + ones).astype(BF16)

    if not is_ctx:
        cq = _rms(z[:, :Q_LORA]) * qg_ref[...]
        qqt = _dot(wuqt_ref[...], cq.T.astype(BF16))
        cst = cst_ref[...]
        snt = snt_ref[...]
        r1, r2, r3 = QK_NOPE, QK_NOPE + QK_ROPE // 2, QK_NOPE + QK_ROPE
        for hd in range(N_HEADS):
            qa = qqt[hd * LANES:(hd + 1) * LANES]
            qb = jnp.concatenate([qa[:r1], qa[r2:r3], qa[r1:r2], qa[r3:]], axis=0)
            qt_ref[0, hd] = ((qa * cst + qb * snt) * scale).astype(BF16)


def _front_call(xs, mod, mod_per_batch, g1, win, qg, kvg, wuqt, wk, wvt, cs, sn, *, is_ctx, ts):
    bsz, n, d = xs.shape
    nt = n // ts
    scale = math.log2(math.e) / math.sqrt(QK_NOPE + QK_ROPE)
    const = lambda b, i: (0, 0)
    mod_map = (lambda b, i: (b, 0, 0)) if mod_per_batch else (lambda b, i: (0, 0, 0))
    in_specs = [
        pl.BlockSpec((1, ts, d), lambda b, i: (b, i, 0)),
        pl.BlockSpec((1, N_MOD, d), mod_map),
        pl.BlockSpec((1, d), const),
        pl.BlockSpec(win.shape, const),
        pl.BlockSpec((1, Q_LORA), const),
        pl.BlockSpec((1, KV_LORA), const),
        pl.BlockSpec(wuqt.shape, const),
        pl.BlockSpec(wk.shape, const),
        pl.BlockSpec(wvt.shape, const),
        pl.BlockSpec((ts, LANES), lambda b, i: (i, 0)),
        pl.BlockSpec((ts, LANES), lambda b, i: (i, 0)),
        pl.BlockSpec((LANES, ts), lambda b, i: (0, i)),
        pl.BlockSpec((LANES, ts), lambda b, i: (0, i)),
    ]
    k_spec = pl.BlockSpec((1, N_HEADS, ts, LANES), lambda b, i: (b, 0, i, 0))
    k_shape = jax.ShapeDtypeStruct((bsz, N_HEADS, n, LANES), BF16)
    qt_spec = pl.BlockSpec((1, N_HEADS, LANES, ts), lambda b, i: (b, 0, 0, i))
    qt_shape = jax.ShapeDtypeStruct((bsz, N_HEADS, LANES, n), BF16)
    vt_spec = pl.BlockSpec((1, N_HEADS, VT_ROWS, ts), lambda b, i: (b, 0, 0, i))
    vt_shape = jax.ShapeDtypeStruct((bsz, N_HEADS, VT_ROWS, n), BF16)
    if is_ctx:
        out_specs = [k_spec, vt_spec]
        out_shape = [k_shape, vt_shape]
    else:
        pool_w = win.shape[1] - Q_LORA - 2 * LANES
        out_specs = [qt_spec, k_spec, vt_spec, pl.BlockSpec((1, ts, pool_w), lambda b, i: (b, i, 0))]
        out_shape = [qt_shape, k_shape, vt_shape, jax.ShapeDtypeStruct((bsz, n, pool_w), F32)]
    return pl.pallas_call(
        functools.partial(_front_kernel, is_ctx=is_ctx, scale=scale),
        grid=(bsz, nt),
        in_specs=in_specs,
        out_specs=out_specs,
        out_shape=out_shape,
        compiler_params=pltpu.CompilerParams(
            dimension_semantics=("arbitrary", "arbitrary"), vmem_limit_bytes=VMEM_LIMIT),
        name="front_ctx" if is_ctx else "front",
    )(xs, mod, g1, win, qg, kvg, wuqt, wk, wvt, cs, sn, cs.T, sn.T)


def _attn_kernel(qt_ref, k_ref, vt_ref, kc_ref, vct_ref, o_ref, s_ref, mlc_ref, mxb_ref, oe_ref):
    n_ctx = kc_ref.shape[2]
    n_lat = k_ref.shape[2]
    chunks = [(None, 0, n_ctx)] + [(c * ATTN_KC, n_ctx + c * ATTN_KC, ATTN_KC)
                                   for c in range(n_lat // ATTN_KC)]
    sub = 8

    def q_cols(tile):
        if isinstance(tile, int):
            return pl.ds(tile * ATTN_TQ, ATTN_TQ)
        return pl.ds(pl.multiple_of(tile * ATTN_TQ, ATTN_TQ), ATTN_TQ)

    def score_chunk(tile, hd, buf, ci):
        off, soff, w = chunks[ci]
        keys = kc_ref[0, hd] if off is None else k_ref[0, hd, off:off + w, :]
        s = _dot(keys, qt_ref[0, hd, :, q_cols(tile)])
        s_ref[buf, soff:soff + w, :] = s
        mx = s[0:sub]
        for r in range(1, w // sub):
            mx = jnp.maximum(mx, s[r * sub:(r + 1) * sub])
        mlc_ref[buf, ci] = mx

    def row_max(buf):
        mx = mlc_ref[buf, 0]
        for ci in range(1, len(chunks)):
            mx = jnp.maximum(mx, mlc_ref[buf, ci])
        mxb_ref[...] = jnp.broadcast_to(jnp.max(mx, axis=0, keepdims=True), mx.shape)

    def weight_chunk(hd, buf, ci):
        off, soff, w = chunks[ci]
        p = jnp.exp2(s_ref[buf, soff:soff + w, :] - mxb_ref[0:1, :]).astype(BF16)
        vt = vct_ref[0, hd] if off is None else vt_ref[0, hd, :, off:off + w]
        return _dot(vt, p)

    def stage(hw, bw, scoring, bs):
        row_max(bw)
        acc = None
        for ci in range(len(chunks)):
            if scoring is not None:
                score_chunk(*scoring, bs, ci)
            part = weight_chunk(hw, bw, ci)
            acc = part if acc is None else acc + part
        return acc[0:V_DIM] / acc[V_DIM:V_DIM + 1]

    def write_pair(tile, j, ot_odd):
        pair_t = jnp.concatenate([oe_ref[...], ot_odd], axis=0)
        o_ref[0, j, q_cols(tile), :] = pair_t.T.astype(BF16)

    pairs_per_tile = N_HEADS // 2
    n_pairs = (qt_ref.shape[3] // ATTN_TQ) * pairs_per_tile

    def split(p):
        return p // pairs_per_tile, p % pairs_per_tile

    for ci in range(len(chunks)):
        score_chunk(0, 0, 0, ci)

    def head_pair(p, carry):
        tile, j = split(p)
        nxt_tile, nxt_j = split(p + 1)
        oe_ref[...] = stage(2 * j, 0, (tile, 2 * j + 1), 1)
        write_pair(tile, j, stage(2 * j + 1, 1, (nxt_tile, 2 * nxt_j), 0))
        return carry

    lax.fori_loop(0, n_pairs - 1, head_pair, 0)
    tile, j = split(n_pairs - 1)
    oe_ref[...] = stage(2 * j, 0, (tile, 2 * j + 1), 1)
    write_pair(tile, j, stage(2 * j + 1, 1, None, None))


def _attn_call(qt, k, vt, kc, vct):
    bsz, _, _, n = qt.shape
    n_ctx = kc.shape[2]
    tq = ATTN_TQ
    n_chunks = 1 + n // ATTN_KC
    per_batch = lambda b, i: (b, 0, 0, 0)
    resident = dict(pipeline_mode=pl.Buffered(1))
    rows_per_step = ATTN_TILES * tq
    return pl.pallas_call(
        _attn_kernel,
        grid=(bsz, n // rows_per_step),
        in_specs=[
            pl.BlockSpec((1, N_HEADS, LANES, rows_per_step), lambda b, i: (b, 0, 0, i)),
            pl.BlockSpec((1, N_HEADS, n, LANES), per_batch, **resident),
            pl.BlockSpec((1, N_HEADS, VT_ROWS, n), per_batch, **resident),
            pl.BlockSpec((1, N_HEADS, n_ctx, LANES), per_batch, **resident),
            pl.BlockSpec((1, N_HEADS, VT_ROWS, n_ctx), per_batch, **resident),
        ],
        out_specs=pl.BlockSpec((1, N_HEADS // 2, rows_per_step, LANES), lambda b, i: (b, 0, i, 0)),
        out_shape=jax.ShapeDtypeStruct((bsz, N_HEADS // 2, n, LANES), BF16),
        scratch_shapes=[
            pltpu.VMEM((2, n_ctx + n, tq), F32),
            pltpu.VMEM((2, n_chunks, 8, tq), F32),
            pltpu.VMEM((8, tq), F32),
            pltpu.VMEM((V_DIM, tq), F32),
        ],
        compiler_params=pltpu.CompilerParams(
            dimension_semantics=("arbitrary", "arbitrary"), vmem_limit_bytes=VMEM_LIMIT),
        name="attn",
    )(qt, k, vt, kc, vct)


def _mix_kernel(att_ref, u_ref, up_ref, un_ref, x_ref, mod_ref, wpool_ref, pscale_ref, wout_ref,
                g2_ref, rw_ref, rb_ref,
                x1_ref, h2_ref, gate_ref, ridx_ref, cnt_ref, ue_ref, base_ref, *, n_seq):
    b = pl.program_id(0)
    i = pl.program_id(1)
    n_tiles = pl.num_programs(1)
    ts = u_ref.shape[1]

    @pl.when(jnp.logical_and(b == 0, i == 0))
    def _():
        base_ref[...] = jnp.zeros_like(base_ref)

    u = u_ref[0]
    ue_ref[0:POOL_HALO] = jnp.where(i > 0, up_ref[0], 0.0)
    ue_ref[POOL_HALO:POOL_HALO + ts] = u
    ue_ref[POOL_HALO + ts:2 * POOL_HALO + ts] = jnp.where(i < n_tiles - 1, un_ref[0], 0.0)
    t = i * ts + lax.broadcasted_iota(jnp.int32, (ts, 1), 0)
    ys = []
    for g, w in enumerate(POOL_WINDOWS):
        half = w // 2
        lanes = slice(g * POOL_CH, (g + 1) * POOL_CH)
        ws = ue_ref[POOL_HALO - half:POOL_HALO - half + ts, lanes]
        for jj in range(-half + 1, half):
            ws = ws + ue_ref[POOL_HALO + jj:POOL_HALO + jj + ts, lanes]
        count = (jnp.minimum(t + half, n_seq) - jnp.maximum(t - half, 0)).astype(F32)
        mixed = (ws / count - u[:, lanes]).astype(BF16)
        ys.append(_dot(mixed, wpool_ref[g]))
    pool = jnp.concatenate(ys, axis=-1) * pscale_ref[...]

    cat = jnp.concatenate([att_ref[0, j] for j in range(N_HEADS // 2)] + [pool.astype(BF16)], axis=-1)
    m = mod_ref[0]
    x1 = x_ref[0] + m[2:3] * _dot(cat, wout_ref[...])
    x1_ref[0] = x1
    h2 = _rms(x1) * g2_ref[...] * (1.0 + m[4:5]) + m[3:4]
    h2_ref[0] = _pack_bf16_pairs(h2)

    h_hi = h2.astype(BF16)
    h_lo = (h2 - h_hi.astype(F32)).astype(BF16)
    hi_part = _dot(h_hi, rw_ref[...])
    logits = (hi_part[:, :LANES] + hi_part[:, LANES:] + _dot(h_lo, rw_ref[:, :LANES])) + rb_ref[...]
    lane = lax.broadcasted_iota(jnp.int32, logits.shape, 1).astype(F32)
    vals, idxs = [], []
    for _k in range(TOP_K):
        mv = jnp.max(logits, axis=-1, keepdims=True)
        ix = jnp.min(jnp.where(logits == mv, lane, float(LANES)), axis=-1, keepdims=True)
        vals.append(mv)
        idxs.append(ix)
        logits = jnp.where(lane == ix, -jnp.inf, logits)
    es = [jnp.exp(v - vals[0]) for v in vals]
    den = es[0] + es[1] + es[2] + es[3]

    onehot = jnp.zeros(lane.shape, F32)
    for ix in idxs:
        onehot = onehot + jnp.where(lane == ix, 1.0, 0.0)
    row = lax.broadcasted_iota(jnp.int32, (ts, ts), 0)
    col = lax.broadcasted_iota(jnp.int32, (ts, ts), 1)
    tri = jnp.where(col < row, 1.0, 0.0).astype(BF16)
    before = _dot(tri, onehot.astype(BF16)) + base_ref[0:1, :]
    base_new = base_ref[0:1, :] + jnp.sum(onehot, axis=0, keepdims=True)
    base_ref[...] = jnp.broadcast_to(base_new, base_ref.shape)
    cnt_ref[...] = jnp.broadcast_to(base_new, cnt_ref.shape)

    gate_out = jnp.zeros(lane.shape, F32)
    ridx_out = jnp.zeros(lane.shape, F32)
    for kk in range(TOP_K):
        rank = jnp.sum(jnp.where(lane == idxs[kk], before, 0.0), axis=-1, keepdims=True)
        gate_out = jnp.where(lane == float(kk), es[kk] / den, gate_out)
        ridx_out = jnp.where(lane == float(kk), idxs[kk], ridx_out)
        ridx_out = jnp.where(lane == float(TOP_K + kk), rank, ridx_out)
    gate_ref[0] = gate_out
    ridx_ref[0] = ridx_out.astype(jnp.int32)


def _mix_call(att, u, x, mod, wpool, pscale, wout, g2, rw, rb):
    bsz, n, d = x.shape
    ts = MIX_TS
    nt = n // ts
    pool_w = u.shape[2]
    hb = ts // POOL_HALO
    n_halo_blocks = n // POOL_HALO
    const2 = lambda b, i: (0, 0)
    tok = lambda b, i: (b, i, 0)
    return pl.pallas_call(
        functools.partial(_mix_kernel, n_seq=n),
        grid=(bsz, nt),
        in_specs=[
            pl.BlockSpec((1, N_HEADS // 2, ts, LANES), lambda b, i: (b, 0, i, 0)),
            pl.BlockSpec((1, ts, pool_w), tok),
            pl.BlockSpec((1, POOL_HALO, pool_w), lambda b, i: (b, jnp.maximum(i * hb - 1, 0), 0)),
            pl.BlockSpec((1, POOL_HALO, pool_w),
                         lambda b, i: (b, jnp.minimum((i + 1) * hb, n_halo_blocks - 1), 0)),
            pl.BlockSpec((1, ts, d), tok),
            pl.BlockSpec((1, N_MOD, d), lambda b, i: (b, 0, 0)),
            pl.BlockSpec(wpool.shape, lambda b, i: (0, 0, 0)),
            pl.BlockSpec((1, pool_w), const2),
            pl.BlockSpec(wout.shape, const2),
            pl.BlockSpec((1, d), const2),
            pl.BlockSpec(rw.shape, const2),
            pl.BlockSpec((1, LANES), const2),
        ],
        out_specs=[
            pl.BlockSpec((1, ts, d), tok),
            pl.BlockSpec((1, ts, d // 2), tok),
            pl.BlockSpec((1, ts, LANES), tok),
            pl.BlockSpec((1, ts, LANES), tok),
            pl.BlockSpec((8, LANES), const2),
        ],
        out_shape=[
            jax.ShapeDtypeStruct((bsz, n, d), F32),
            jax.ShapeDtypeStruct((bsz, n, d // 2), jnp.uint32),
            jax.ShapeDtypeStruct((bsz, n, LANES), F32),
            jax.ShapeDtypeStruct((bsz, n, LANES), jnp.int32),
            jax.ShapeDtypeStruct((8, LANES), F32),
        ],
        scratch_shapes=[
            pltpu.VMEM((ts + 2 * POOL_HALO, pool_w), F32),
            pltpu.VMEM((8, LANES), F32),
        ],
        compiler_params=pltpu.CompilerParams(
            dimension_semantics=("arbitrary", "arbitrary"), vmem_limit_bytes=VMEM_LIMIT),
        name="mix",
    )(att, u, u, u, x, mod, wpool, pscale, wout, g2, rw, rb)


def _sc_mesh():
    return plsc.VectorSubcoreMesh(core_axis_name="core", subcore_axis_name="subcore")


def _sc_worker_id():
    info = plsc.get_sparse_core_info()
    return lax.axis_index("subcore") * info.num_cores + lax.axis_index("core")


def _sc_num_workers():
    info = plsc.get_sparse_core_info()
    return info.num_cores * info.num_subcores


def _scatter_rows(x, dest3, n_out):
    t, d = x.shape
    top_k, n_chunks, rows = dest3.shape
    per_worker = n_chunks // _sc_num_workers()

    @functools.partial(
        pl.kernel,
        out_type=jax.ShapeDtypeStruct((n_out, d), x.dtype),
        mesh=_sc_mesh(),
        scratch_types=[pltpu.VMEM((top_k, rows), jnp.int32), pltpu.VMEM((rows, d), x.dtype)],
        name="sc_scatter",
    )
    def scatter(x_hbm, i_hbm, o_hbm, idx_v, rows_v):
        wid = _sc_worker_id()

        @pl.loop(0, per_worker)
        def _(c):
            chunk = wid * per_worker + c
            for kk in range(top_k):
                pltpu.sync_copy(i_hbm.at[kk, chunk], idx_v.at[kk])
            pltpu.sync_copy(x_hbm.at[pl.ds(pl.multiple_of(chunk * rows, 8), rows)], rows_v)
            for kk in range(top_k):
                pltpu.sync_copy(rows_v, o_hbm.at[idx_v.at[kk]])

    return scatter(x, dest3)


def _gather_rows(y, idx):
    n = idx.shape[0]
    d = y.shape[1]
    rows = SC_ROWS
    per_worker = n // _sc_num_workers()
    n_chunks = per_worker // rows

    @functools.partial(
        pl.kernel,
        out_type=jax.ShapeDtypeStruct((n, d), y.dtype),
        mesh=_sc_mesh(),
        scratch_types=[pltpu.VMEM((rows,), jnp.int32), pltpu.VMEM((rows, d), y.dtype)],
        name="sc_gather",
    )
    def gather(y_hbm, i_hbm, o_hbm, idx_v, rows_v):
        base = _sc_worker_id() * per_worker

        @pl.loop(0, n_chunks)
        def _(c):
            off = pl.multiple_of(base + c * rows, 8)
            pltpu.sync_copy(i_hbm.at[pl.ds(off, rows)], idx_v)
            pltpu.sync_copy(y_hbm.at[idx_v], rows_v)
            pltpu.sync_copy(rows_v, o_hbm.at[pl.ds(off, rows)])

    return gather(y, idx)


def _moe_kernel(blk_ref, exp_ref, lo_ref, hi_ref, cast_ref, init_ref, next_ref, xs_ref, wgu_hbm,
                bgu_ref, wd_hbm, bd_ref, ys_ref, wgu_f32, wd_f32, wgu_bf, wd_bf, sems):
    i = pl.program_id(0)

    def weight_copies(e):
        return (pltpu.make_async_copy(wgu_hbm.at[e], wgu_f32, sems.at[0]),
                pltpu.make_async_copy(wd_hbm.at[e], wd_f32, sems.at[1]))

    @pl.when(i == 0)
    def _():
        for cp in weight_copies(exp_ref[0]):
            cp.start()

    @pl.when(cast_ref[i] == 1)
    def _():
        for cp in weight_copies(exp_ref[i]):
            cp.wait()
        wgu_bf[...] = wgu_f32[...].astype(BF16)
        wd_bf[...] = wd_f32[...].astype(BF16)

        @pl.when(next_ref[i] >= 0)
        def _():
            for cp in weight_copies(next_ref[i]):
                cp.start()

    @pl.when(init_ref[i] == 1)
    def _():
        ys_ref[...] = jnp.zeros_like(ys_ref)

    lo = lo_ref[i]
    hi = hi_ref[i]

    def expert_rows(r0, n_rows):
        de = wd_bf.shape[0]
        x = _unpack_bf16_pairs(xs_ref[pl.ds(r0, n_rows), :]).astype(BF16)
        gu = _dot(x, wgu_bf[...]) + bgu_ref[0]
        g = jnp.minimum(gu[:, :de], SWIGLU_LIMIT)
        lin = jnp.clip(gu[:, de:], -SWIGLU_LIMIT, SWIGLU_LIMIT)
        act = g / (1.0 + jnp.exp(-SWIGLU_ALPHA * g)) * (lin + 1.0)
        y = _dot(act.astype(BF16), wd_bf[...]) + bd_ref[0]
        row = r0 + lax.broadcasted_iota(jnp.int32, (n_rows, 1), 0)
        mine = jnp.logical_and(row >= lo, row < hi)
        ys_ref[pl.ds(r0, n_rows), :] = jnp.where(mine, _pack_bf16_pairs(y), ys_ref[pl.ds(r0, n_rows), :])

    first = lo // MOE_SUB
    pieces = jnp.where(hi > lo, (hi + MOE_SUB - 1) // MOE_SUB - first, 0)
    for cnt in range(1, MOE_BLOCK // MOE_SUB + 1):
        @pl.when(pieces == cnt)
        def _(cnt=cnt):
            r0 = 0 if cnt * MOE_SUB == MOE_BLOCK else pl.multiple_of(first * MOE_SUB, MOE_SUB)
            expert_rows(r0, cnt * MOE_SUB)


def _moe_call(sched, xs, wgu, bgu, wd, bd):
    n_items = sched[0].shape[0]
    _, d, de2 = wgu.shape
    de = de2 // 2
    wmap = lambda i, blk, exp, lo, hi, cast, init, nxt: (exp[i], 0, 0)
    xmap = lambda i, blk, exp, lo, hi, cast, init, nxt: (blk[i], 0)
    return pl.pallas_call(
        _moe_kernel,
        grid_spec=pltpu.PrefetchScalarGridSpec(
            num_scalar_prefetch=7,
            grid=(n_items,),
            in_specs=[
                pl.BlockSpec((MOE_BLOCK, d // 2), xmap),
                pl.BlockSpec(memory_space=pl.ANY),
                pl.BlockSpec((1, 1, de2), wmap),
                pl.BlockSpec(memory_space=pl.ANY),
                pl.BlockSpec((1, 1, d), wmap),
            ],
            out_specs=pl.BlockSpec((MOE_BLOCK, d // 2), xmap),
            scratch_shapes=[
                pltpu.VMEM((d, de2), F32), pltpu.VMEM((de, d), F32),
                pltpu.VMEM((d, de2), BF16), pltpu.VMEM((de, d), BF16),
                pltpu.SemaphoreType.DMA((2,)),
            ],
        ),
        out_shape=jax.ShapeDtypeStruct(xs.shape, xs.dtype),
        compiler_params=pltpu.CompilerParams(
            dimension_semantics=("arbitrary",), vmem_limit_bytes=VMEM_LIMIT),
        name="moe",
    )(*sched, xs, wgu, bgu, wd, bd)


def _moe_schedule(counts, n_rows):
    n_blocks = n_rows // MOE_BLOCK
    n_items = n_blocks + N_EXPERTS
    ends = jnp.cumsum(counts)
    starts = ends - counts
    first_blk = starts // MOE_BLOCK
    last_blk = (ends - 1) // MOE_BLOCK
    items_per = jnp.where(counts > 0, last_blk - first_blk + 1, 0)
    item_ends = jnp.cumsum(items_per)
    item_starts = item_ends - items_per
    total = item_ends[-1]
    it = jnp.arange(n_items, dtype=jnp.int32)
    live = it < total
    itc = jnp.minimum(it, total - 1)
    exp = jnp.sum((item_ends[None, :] <= itc[:, None]).astype(jnp.int32), axis=1)
    is_exp = exp[:, None] == jnp.arange(N_EXPERTS, dtype=jnp.int32)[None, :]
    pick = lambda table: jnp.sum(jnp.where(is_exp, table[None, :], 0), axis=1)
    blk = pick(first_blk) + itc - pick(item_starts)
    lo = jnp.clip(pick(starts) - blk * MOE_BLOCK, 0, MOE_BLOCK)
    hi = jnp.clip(pick(ends) - blk * MOE_BLOCK, 0, MOE_BLOCK)
    hi = jnp.where(live, hi, lo)
    prev_exp = jnp.concatenate([jnp.full((1,), -1, jnp.int32), exp[:-1]])
    prev_blk = jnp.concatenate([jnp.full((1,), -1, jnp.int32), blk[:-1]])
    cast = jnp.logical_and(live, exp != prev_exp)
    init = jnp.logical_and(live, blk != prev_blk)
    ar = jnp.arange(N_EXPERTS, dtype=jnp.int32)
    later = jnp.logical_and(counts[None, :] > 0, ar[None, :] > ar[:, None])
    next_exp = jnp.min(jnp.where(later, ar[None, :], N_EXPERTS), axis=1)
    next_exp = jnp.where(next_exp == N_EXPERTS, -1, next_exp)
    nxt = pick(next_exp)
    as_i32 = lambda a: a.astype(jnp.int32)
    return tuple(as_i32(a) for a in (blk, exp, lo, hi, cast, init, nxt)), starts


def _combine_kernel(*refs):
    yg_refs = refs[:TOP_K]
    x1_ref, gate_ref, mod_ref, fg_ref, o_ref = refs[TOP_K:]
    gates = gate_ref[...]
    y = gates[:, 0:1] * _unpack_bf16_pairs(yg_refs[0][...])
    for kk in range(1, TOP_K):
        y = y + gates[:, kk:kk + 1] * _unpack_bf16_pairs(yg_refs[kk][...])
    m = mod_ref[0]
    x2 = x1_ref[...] + m[5:6] * y
    o_ref[...] = _rms(x2) * fg_ref[...]


def _combine_call(yg, x1, gates, mod, fg, n_seq, part):
    t, d = x1.shape
    tt = COMBINE_TT
    tiles_per_seq = n_seq // tt
    n_tiles = yg.shape[0] // (TOP_K * tt)
    first = part * n_tiles
    tok = lambda i: (first + i, 0)
    slot_specs = [pl.BlockSpec((tt, d // 2), functools.partial(lambda kk, i: (kk * n_tiles + i, 0), kk))
                  for kk in range(TOP_K)]
    return pl.pallas_call(
        _combine_kernel,
        grid=(n_tiles,),
        in_specs=slot_specs + [
            pl.BlockSpec((tt, d), tok),
            pl.BlockSpec((tt, LANES), tok),
            pl.BlockSpec((1, N_MOD, d), lambda i: ((first + i) // tiles_per_seq, 0, 0)),
            pl.BlockSpec((1, d), lambda i: (0, 0)),
        ],
        out_specs=pl.BlockSpec((tt, d), tok),
        out_shape=jax.ShapeDtypeStruct((t, d), F32),
        input_output_aliases={TOP_K: 0},
        compiler_params=pltpu.CompilerParams(
            dimension_semantics=("arbitrary",), vmem_limit_bytes=VMEM_LIMIT),
        name="combine",
    )(*([yg] * TOP_K), x1, gates, mod, fg)


def _pad_cols(a, width):
    return jnp.pad(a, ((0, 0), (0, width - a.shape[1])))


def _rope_slab(w_rope):
    return jnp.pad(w_rope, ((0, 0), (QK_NOPE, LANES - QK_NOPE - QK_ROPE)))


def _prep_w_in(w_in):
    kr0 = Q_LORA + KV_LORA
    w_kr = w_in[:, kr0:kr0 + QK_ROPE]
    return jnp.concatenate(
        [w_in[:, :kr0], _rope_slab(w_kr), w_in[:, kr0 + QK_ROPE:]], axis=1).astype(BF16)


def _prep_w_uq(w_uq):
    per = QK_NOPE + QK_ROPE
    rows = w_uq.shape[0]
    w = w_uq.reshape(rows, N_HEADS, per)
    slab = jnp.pad(w, ((0, 0), (0, 0), (0, LANES - per)))
    return slab.reshape(rows, -1).T.astype(BF16)


def _prep_w_ukv(w_ukv):
    per = QK_NOPE + V_DIM
    rows = w_ukv.shape[0]
    w = w_ukv.reshape(rows, N_HEADS, per)
    wk = jnp.pad(w[:, :, :QK_NOPE], ((0, 0), (0, 0), (0, LANES - QK_NOPE))).reshape(rows, -1)
    wvt = jnp.pad(jnp.transpose(w[:, :, QK_NOPE:], (1, 2, 0)), ((0, 0), (0, VT_ROWS - V_DIM), (0, 0)))
    return wk.astype(BF16), wvt.reshape(N_HEADS * VT_ROWS, rows).astype(BF16)


def _rope_tables(n_lat):
    rows = n_lat // GRID_W
    nf = QK_ROPE // 4
    row = jnp.repeat(jnp.arange(rows, dtype=F32), GRID_W)
    col = jnp.tile(jnp.arange(GRID_W, dtype=F32), rows)
    freqs = ROPE_BASE ** (-jnp.arange(nf, dtype=F32) / nf)
    ang = jnp.concatenate([row[:, None] * freqs, col[:, None] * freqs], axis=-1)
    cos, sin = jnp.cos(ang), jnp.sin(ang)
    ones = jnp.ones((n_lat, QK_NOPE), F32)
    zeros = jnp.zeros((n_lat, QK_NOPE), F32)
    cs = _pad_cols(jnp.concatenate([ones, cos, cos], axis=1), LANES)
    sn = _pad_cols(jnp.concatenate([zeros, -sin, sin], axis=1), LANES)
    return cs, sn


def kernel(x, c, ctx, c_ctx, w_mod, b_mod, norm1_g, w_in, q_norm_g, kv_norm_g, w_uq, w_ukv, w_pool,
           pool_scale, w_out, norm2_g, router_w, router_b, w_gate_up, b_gate_up, w_down, b_down,
           final_g):
    bsz, n, d = x.shape
    n_ctx = ctx.shape[1]
    t = bsz * n
    l = 0

    cc = jnp.concatenate([c, c_ctx[None, :], jnp.zeros((8 - bsz - 1, d), F32)], axis=0)
    mod = _mod_call(cc, w_mod[l], b_mod[l][None, :]).reshape(8, N_MOD, d)
    mod_lat, mod_ctx = mod[:bsz], mod[bsz:bsz + 1]

    win = _prep_w_in(w_in[l])
    wuqt = _prep_w_uq(w_uq[l])
    wk, wvt = _prep_w_ukv(w_ukv[l])
    cs, sn = _rope_tables(n)
    cs_ctx = jnp.broadcast_to((jnp.arange(LANES) < QK_NOPE + QK_ROPE).astype(F32), (n_ctx, LANES))
    sn_ctx = jnp.zeros((n_ctx, LANES), F32)
    g1 = norm1_g[l][None, :]
    qg = q_norm_g[l][None, :]
    kvg = kv_norm_g[l][None, :]

    qt, k, vt, u = _front_call(x, mod_lat, True, g1, win, qg, kvg, wuqt, wk, wvt, cs, sn,
                               is_ctx=False, ts=FRONT_TS)
    kc, vct = _front_call(ctx, mod_ctx, False, g1, win[:, Q_LORA:Q_LORA + 2 * LANES], qg, kvg, wuqt,
                          wk, wvt, cs_ctx, sn_ctx, is_ctx=True, ts=n_ctx)
    att = _attn_call(qt, k, vt, kc, vct)

    rw = _pad_cols(router_w[l], LANES)
    rw_hi = rw.astype(BF16)
    rw = jnp.concatenate([rw_hi, (rw - rw_hi.astype(F32)).astype(BF16)], axis=1)
    rb = jnp.concatenate([router_b[l], jnp.full((LANES - N_EXPERTS,), -jnp.inf, F32)])[None, :]
    x1, h2, gates, ridx, cnt = _mix_call(
        att, u, x, mod_lat, w_pool[l].astype(BF16), pool_scale[l][None, :], w_out[l].astype(BF16),
        norm2_g[l][None, :], rw, rb)

    counts = cnt[0, :N_EXPERTS].astype(jnp.int32)
    sched, starts = _moe_schedule(counts, t * TOP_K)
    ridx = ridx.reshape(t, LANES)
    is_exp = ridx[:, :TOP_K, None] == jnp.arange(N_EXPERTS, dtype=jnp.int32)
    dest = jnp.sum(jnp.where(is_exp, starts, 0), axis=-1) + ridx[:, TOP_K:2 * TOP_K]
    dest_t = dest.T

    xs = _scatter_rows(h2.reshape(t, d // 2), dest_t.reshape(TOP_K, t // SC_ROWS, SC_ROWS), t * TOP_K)
    ys = _moe_call(sched, xs, w_gate_up[l], b_gate_up[l][:, None, :], w_down[l],
                   b_down[l][:, None, :])
    out = x1.reshape(t, d)
    gates = gates.reshape(t, LANES)
    tp = t // COMBINE_PARTS
    for part in range(COMBINE_PARTS):
        idx = dest_t[:, part * tp:(part + 1) * tp].reshape(-1)
        yg = _gather_rows(ys, idx)
        out = _combine_call(yg, out, gates, mod_lat, final_g[None, :], n, part)
    return out.reshape(bsz, n, d)
```

```python
import functools
import math

import jax
import jax.numpy as jnp
from jax import lax
from jax.experimental import pallas as pl
from jax.experimental.pallas import tpu as pltpu
from jax.experimental.pallas import tpu_sc as plsc

F32 = jnp.float32
BF16 = jnp.bfloat16
HIGHEST = lax.Precision.HIGHEST

N_HEADS = 8
QK_NOPE = 64
QK_ROPE = 32
V_DIM = 64
Q_LORA = 256
KV_LORA = 128
GRID_W = 64
ROPE_BASE = 10000.0
POOL_WINDOWS = (2, 4, 8, 16)
POOL_CH = 128
N_EXPERTS = 32
TOP_K = 4
SWIGLU_LIMIT = 7.0
SWIGLU_ALPHA = 1.702
MOE_BLOCK = 512
MOE_SUB = 128
N_MOD = 6
EPS = 1e-6

LANES = 128
POOL_HALO = 8
VT_ROWS = 80
VMEM_LIMIT = 56 * 1024 * 1024

FRONT_TS = 512
ATTN_TQ = 512
ATTN_TILES = 4
ATTN_KC = 512
MIX_TS = 512
COMBINE_TT = 256
COMBINE_PARTS = 4
SC_ROWS = 128


def _dot(a, b, **kw):
    return jnp.dot(a, b, preferred_element_type=F32, **kw)


def _dot_nt(a, b):
    return lax.dot_general(a, b, (((1,), (1,)), ((), ())), preferred_element_type=F32)


def _rms(x):
    return x * lax.rsqrt(jnp.mean(x * x, axis=-1, keepdims=True) + EPS)


def _pack_bf16_pairs(x):
    w = x.shape[1] // 2
    return pltpu.pack_elementwise([x[:, :w], x[:, w:]], packed_dtype=BF16)


def _unpack_bf16_pairs(words):
    halves = [pltpu.unpack_elementwise(words, index=h, packed_dtype=BF16, unpacked_dtype=F32)
              for h in range(2)]
    return jnp.concatenate(halves, axis=-1)


def _mod_kernel(c_ref, w_ref, b_ref, o_ref):
    c = c_ref[...]
    a = c / (1.0 + jnp.exp(-c))
    o_ref[...] = _dot(a, w_ref[...], precision=HIGHEST) + b_ref[...]


def _mod_call(cc, w_mod, b_mod):
    d = w_mod.shape[0]
    return pl.pallas_call(
        _mod_kernel,
        grid=(N_MOD,),
        in_specs=[
            pl.BlockSpec((8, d), lambda j: (0, 0)),
            pl.BlockSpec((d, d), lambda j: (0, j)),
            pl.BlockSpec((1, d), lambda j: (0, j)),
        ],
        out_specs=pl.BlockSpec((8, d), lambda j: (0, j)),
        out_shape=jax.ShapeDtypeStruct((8, N_MOD * d), F32),
        compiler_params=pltpu.CompilerParams(
            dimension_semantics=("arbitrary",), vmem_limit_bytes=VMEM_LIMIT),
        name="mod",
    )(cc, w_mod, b_mod)


def _front_kernel(x_ref, mod_ref, g1_ref, win_ref, qg_ref, kvg_ref, wuqt_ref, wk_ref, wvt_ref,
                  cs_ref, sn_ref, cst_ref, snt_ref, *out_refs, is_ctx, scale):
    x = x_ref[0]
    m = mod_ref[0]
    h = _rms(x) * g1_ref[...] * (1.0 + m[1:2]) + m[0:1]
    z = _dot(h.astype(BF16), win_ref[...])
    if is_ctx:
        k_ref, vt_ref = out_refs
        zkv = z
    else:
        qt_ref, k_ref, vt_ref, u_ref = out_refs
        zkv = z[:, Q_LORA:Q_LORA + 2 * LANES]
        u_ref[0] = z[:, Q_LORA + 2 * LANES:]

    ckv = _rms(zkv[:, :KV_LORA]) * kvg_ref[...]
    kk = _dot(ckv.astype(BF16), wk_ref[...])
    kra = zkv[:, LANES:2 * LANES]
    lane = lax.broadcasted_iota(jnp.int32, kra.shape, 1)
    half = QK_ROPE // 2
    partner = jnp.where(lane < QK_NOPE + half, pltpu.roll(kra, LANES - half, axis=1),
                        pltpu.roll(kra, half, axis=1))
    kr = kra * cs_ref[...] + partner * sn_ref[...]
    for hd in range(N_HEADS):
        k_ref[0, hd] = (kk[:, hd * LANES:(hd + 1) * LANES] + kr).astype(BF16)
    vt = _dot(wvt_ref[...], ckv.T.astype(BF16))
    row = lax.broadcasted_iota(jnp.int32, (VT_ROWS, 1), 0)
    ones = jnp.where(row == V_DIM, 1.0, 0.0)
    for hd in range(N_HEADS):
        vt_ref[0, hd] = (vt[hd * VT_ROWS:(hd + 1) * VT_ROWS] + ones).astype(BF16)

    if not is_ctx:
        cq = _rms(z[:, :Q_LORA]) * qg_ref[...]
        qqt = _dot(wuqt_ref[...], cq.T.astype(BF16))
        cst = cst_ref[...]
        snt = snt_ref[...]
        r1, r2, r3 = QK_NOPE, QK_NOPE + QK_ROPE // 2, QK_NOPE + QK_ROPE
        for hd in range(N_HEADS):
            qa = qqt[hd * LANES:(hd + 1) * LANES]
            qb = jnp.concatenate([qa[:r1], qa[r2:r3], qa[r1:r2], qa[r3:]], axis=0)
            qt_ref[0, hd] = ((qa * cst + qb * snt) * scale).astype(BF16)


def _front_call(xs, mod, mod_per_batch, g1, win, qg, kvg, wuqt, wk, wvt, cs, sn, *, is_ctx, ts):
    bsz, n, d = xs.shape
    nt = n // ts
    scale = math.log2(math.e) / math.sqrt(QK_NOPE + QK_ROPE)
    const = lambda b, i: (0, 0)
    mod_map = (lambda b, i: (b, 0, 0)) if mod_per_batch else (lambda b, i: (0, 0, 0))
    in_specs = [
        pl.BlockSpec((1, ts, d), lambda b, i: (b, i, 0)),
        pl.BlockSpec((1, N_MOD, d), mod_map),
        pl.BlockSpec((1, d), const),
        pl.BlockSpec(win.shape, const),
        pl.BlockSpec((1, Q_LORA), const),
        pl.BlockSpec((1, KV_LORA), const),
        pl.BlockSpec(wuqt.shape, const),
        pl.BlockSpec(wk.shape, const),
        pl.BlockSpec(wvt.shape, const),
        pl.BlockSpec((ts, LANES), lambda b, i: (i, 0)),
        pl.BlockSpec((ts, LANES), lambda b, i: (i, 0)),
        pl.BlockSpec((LANES, ts), lambda b, i: (0, i)),
        pl.BlockSpec((LANES, ts), lambda b, i: (0, i)),
    ]
    k_spec = pl.BlockSpec((1, N_HEADS, ts, LANES), lambda b, i: (b, 0, i, 0))
    k_shape = jax.ShapeDtypeStruct((bsz, N_HEADS, n, LANES), BF16)
    qt_spec = pl.BlockSpec((1, N_HEADS, LANES, ts), lambda b, i: (b, 0, 0, i))
    qt_shape = jax.ShapeDtypeStruct((bsz, N_HEADS, LANES, n), BF16)
    vt_spec = pl.BlockSpec((1, N_HEADS, VT_ROWS, ts), lambda b, i: (b, 0, 0, i))
    vt_shape = jax.ShapeDtypeStruct((bsz, N_HEADS, VT_ROWS, n), BF16)
    if is_ctx:
        out_specs = [k_spec, vt_spec]
        out_shape = [k_shape, vt_shape]
    else:
        pool_w = win.shape[1] - Q_LORA - 2 * LANES
        out_specs = [qt_spec, k_spec, vt_spec, pl.BlockSpec((1, ts, pool_w), lambda b, i: (b, i, 0))]
        out_shape = [qt_shape, k_shape, vt_shape, jax.ShapeDtypeStruct((bsz, n, pool_w), F32)]
    return pl.pallas_call(
        functools.partial(_front_kernel, is_ctx=is_ctx, scale=scale),
        grid=(bsz, nt),
        in_specs=in_specs,
        out_specs=out_specs,
        out_shape=out_shape,
        compiler_params=pltpu.CompilerParams(
            dimension_semantics=("arbitrary", "arbitrary"), vmem_limit_bytes=VMEM_LIMIT),
        name="front_ctx" if is_ctx else "front",
    )(xs, mod, g1, win, qg, kvg, wuqt, wk, wvt, cs, sn, cs.T, sn.T)


def _attn_kernel(qt_ref, k_ref, vt_ref, kc_ref, vct_ref, o_ref, s_ref, mlc_ref, mxb_ref, oe_ref):
    n_ctx = kc_ref.shape[2]
    n_lat = k_ref.shape[2]
    chunks = [(None, 0, n_ctx)] + [(c * ATTN_KC, n_ctx + c * ATTN_KC, ATTN_KC)
                                   for c in range(n_lat // ATTN_KC)]
    sub = 8

    def q_cols(tile):
        if isinstance(tile, int):
            return pl.ds(tile * ATTN_TQ, ATTN_TQ)
        return pl.ds(pl.multiple_of(tile * ATTN_TQ, ATTN_TQ), ATTN_TQ)

    def score_chunk(tile, hd, buf, ci):
        off, soff, w = chunks[ci]
        keys = kc_ref[0, hd] if off is None else k_ref[0, hd, off:off + w, :]
        s = _dot(keys, qt_ref[0, hd, :, q_cols(tile)])
        s_ref[buf, soff:soff + w, :] = s
        mx = s[0:sub]
        for r in range(1, w // sub):
            mx = jnp.maximum(mx, s[r * sub:(r + 1) * sub])
        mlc_ref[buf, ci] = mx

    def row_max(buf):
        mx = mlc_ref[buf, 0]
        for ci in range(1, len(chunks)):
            mx = jnp.maximum(mx, mlc_ref[buf, ci])
        mxb_ref[...] = jnp.broadcast_to(jnp.max(mx, axis=0, keepdims=True), mx.shape)

    def weight_chunk(hd, buf, ci):
        off, soff, w = chunks[ci]
        p = jnp.exp2(s_ref[buf, soff:soff + w, :] - mxb_ref[0:1, :]).astype(BF16)
        vt = vct_ref[0, hd] if off is None else vt_ref[0, hd, :, off:off + w]
        return _dot(vt, p)

    def stage(hw, bw, scoring, bs):
        row_max(bw)
        acc = None
        for ci in range(len(chunks)):
            if scoring is not None:
                score_chunk(*scoring, bs, ci)
            part = weight_chunk(hw, bw, ci)
            acc = part if acc is None else acc + part
        return acc[0:V_DIM] / acc[V_DIM:V_DIM + 1]

    def write_pair(tile, j, ot_odd):
        pair_t = jnp.concatenate([oe_ref[...], ot_odd], axis=0)
        o_ref[0, j, q_cols(tile), :] = pair_t.T.astype(BF16)

    pairs_per_tile = N_HEADS // 2
    n_pairs = (qt_ref.shape[3] // ATTN_TQ) * pairs_per_tile

    def split(p):
        return p // pairs_per_tile, p % pairs_per_tile

    for ci in range(len(chunks)):
        score_chunk(0, 0, 0, ci)

    def head_pair(p, carry):
        tile, j = split(p)
        nxt_tile, nxt_j = split(p + 1)
        oe_ref[...] = stage(2 * j, 0, (tile, 2 * j + 1), 1)
        write_pair(tile, j, stage(2 * j + 1, 1, (nxt_tile, 2 * nxt_j), 0))
        return carry

    lax.fori_loop(0, n_pairs - 1, head_pair, 0)
    tile, j = split(n_pairs - 1)
    oe_ref[...] = stage(2 * j, 0, (tile, 2 * j + 1), 1)
    write_pair(tile, j, stage(2 * j + 1, 1, None, None))


def _attn_call(qt, k, vt, kc, vct):
    bsz, _, _, n = qt.shape
    n_ctx = kc.shape[2]
    tq = ATTN_TQ
    n_chunks = 1 + n // ATTN_KC
    per_batch = lambda b, i: (b, 0, 0, 0)
    resident = dict(pipeline_mode=pl.Buffered(1))
    rows_per_step = ATTN_TILES * tq
    return pl.pallas_call(
        _attn_kernel,
        grid=(bsz, n // rows_per_step),
        in_specs=[
            pl.BlockSpec((1, N_HEADS, LANES, rows_per_step), lambda b, i: (b, 0, 0, i)),
            pl.BlockSpec((1, N_HEADS, n, LANES), per_batch, **resident),
            pl.BlockSpec((1, N_HEADS, VT_ROWS, n), per_batch, **resident),
            pl.BlockSpec((1, N_HEADS, n_ctx, LANES), per_batch, **resident),
            pl.BlockSpec((1, N_HEADS, VT_ROWS, n_ctx), per_batch, **resident),
        ],
        out_specs=pl.BlockSpec((1, N_HEADS // 2, rows_per_step, LANES), lambda b, i: (b, 0, i, 0)),
        out_shape=jax.ShapeDtypeStruct((bsz, N_HEADS // 2, n, LANES), BF16),
        scratch_shapes=[
            pltpu.VMEM((2, n_ctx + n, tq), F32),
            pltpu.VMEM((2, n_chunks, 8, tq), F32),
            pltpu.VMEM((8, tq), F32),
            pltpu.VMEM((V_DIM, tq), F32),
        ],
        compiler_params=pltpu.CompilerParams(
            dimension_semantics=("arbitrary", "arbitrary"), vmem_limit_bytes=VMEM_LIMIT),
        name="attn",
    )(qt, k, vt, kc, vct)


def _mix_kernel(att_ref, u_ref, up_ref, un_ref, x_ref, mod_ref, wpool_ref, pscale_ref, wout_ref,
                g2_ref, rw_ref, rb_ref,
                x1_ref, h2_ref, gate_ref, ridx_ref, cnt_ref, ue_ref, base_ref, *, n_seq):
    b = pl.program_id(0)
    i = pl.program_id(1)
    n_tiles = pl.num_programs(1)
    ts = u_ref.shape[1]

    @pl.when(jnp.logical_and(b == 0, i == 0))
    def _():
        base_ref[...] = jnp.zeros_like(base_ref)

    u = u_ref[0]
    ue_ref[0:POOL_HALO] = jnp.where(i > 0, up_ref[0], 0.0)
    ue_ref[POOL_HALO:POOL_HALO + ts] = u
    ue_ref[POOL_HALO + ts:2 * POOL_HALO + ts] = jnp.where(i < n_tiles - 1, un_ref[0], 0.0)
    t = i * ts + lax.broadcasted_iota(jnp.int32, (ts, 1), 0)
    ys = []
    for g, w in enumerate(POOL_WINDOWS):
        half = w // 2
        lanes = slice(g * POOL_CH, (g + 1) * POOL_CH)
        ws = ue_ref[POOL_HALO - half:POOL_HALO - half + ts, lanes]
        for jj in range(-half + 1, half):
            ws = ws + ue_ref[POOL_HALO + jj:POOL_HALO + jj + ts, lanes]
        count = (jnp.minimum(t + half, n_seq) - jnp.maximum(t - half, 0)).astype(F32)
        mixed = (ws / count - u[:, lanes]).astype(BF16)
        ys.append(_dot(mixed, wpool_ref[g]))
    pool = jnp.concatenate(ys, axis=-1) * pscale_ref[...]

    cat = jnp.concatenate([att_ref[0, j] for j in range(N_HEADS // 2)] + [pool.astype(BF16)], axis=-1)
    m = mod_ref[0]
    x1 = x_ref[0] + m[2:3] * _dot(cat, wout_ref[...])
    x1_ref[0] = x1
    h2 = _rms(x1) * g2_ref[...] * (1.0 + m[4:5]) + m[3:4]
    h2_ref[0] = _pack_bf16_pairs(h2)

    h_hi = h2.astype(BF16)
    h_lo = (h2 - h_hi.astype(F32)).astype(BF16)
    hi_part = _dot(h_hi, rw_ref[...])
    logits = (hi_part[:, :LANES] + hi_part[:, LANES:] + _dot(h_lo, rw_ref[:, :LANES])) + rb_ref[...]
    lane = lax.broadcasted_iota(jnp.int32, logits.shape, 1).astype(F32)
    vals, idxs = [], []
    for _k in range(TOP_K):
        mv = jnp.max(logits, axis=-1, keepdims=True)
        ix = jnp.min(jnp.where(logits == mv, lane, float(LANES)), axis=-1, keepdims=True)
        vals.append(mv)
        idxs.append(ix)
        logits = jnp.where(lane == ix, -jnp.inf, logits)
    es = [jnp.exp(v - vals[0]) for v in vals]
    den = es[0] + es[1] + es[2] + es[3]

    onehot = jnp.zeros(lane.shape, F32)
    for ix in idxs:
        onehot = onehot + jnp.where(lane == ix, 1.0, 0.0)
    row = lax.broadcasted_iota(jnp.int32, (ts, ts), 0)
    col = lax.broadcasted_iota(jnp.int32, (ts, ts), 1)
    tri = jnp.where(col < row, 1.0, 0.0).astype(BF16)
    before = _dot(tri, onehot.astype(BF16)) + base_ref[0:1, :]
    base_new = base_ref[0:1, :] + jnp.sum(onehot, axis=0, keepdims=True)
    base_ref[...] = jnp.broadcast_to(base_new, base_ref.shape)
    cnt_ref[...] = jnp.broadcast_to(base_new, cnt_ref.shape)

    gate_out = jnp.zeros(lane.shape, F32)
    ridx_out = jnp.zeros(lane.shape, F32)
    for kk in range(TOP_K):
        rank = jnp.sum(jnp.where(lane == idxs[kk], before, 0.0), axis=-1, keepdims=True)
        gate_out = jnp.where(lane == float(kk), es[kk] / den, gate_out)
        ridx_out = jnp.where(lane == float(kk), idxs[kk], ridx_out)
        ridx_out = jnp.where(lane == float(TOP_K + kk), rank, ridx_out)
    gate_ref[0] = gate_out
    ridx_ref[0] = ridx_out.astype(jnp.int32)


def _mix_call(att, u, x, mod, wpool, pscale, wout, g2, rw, rb):
    bsz, n, d = x.shape
    ts = MIX_TS
    nt = n // ts
    pool_w = u.shape[2]
    hb = ts // POOL_HALO
    n_halo_blocks = n // POOL_HALO
    const2 = lambda b, i: (0, 0)
    tok = lambda b, i: (b, i, 0)
    return pl.pallas_call(
        functools.partial(_mix_kernel, n_seq=n),
        grid=(bsz, nt),
        in_specs=[
            pl.BlockSpec((1, N_HEADS // 2, ts, LANES), lambda b, i: (b, 0, i, 0)),
            pl.BlockSpec((1, ts, pool_w), tok),
            pl.BlockSpec((1, POOL_HALO, pool_w), lambda b, i: (b, jnp.maximum(i * hb - 1, 0), 0)),
            pl.BlockSpec((1, POOL_HALO, pool_w),
                         lambda b, i: (b, jnp.minimum((i + 1) * hb, n_halo_blocks - 1), 0)),
            pl.BlockSpec((1, ts, d), tok),
            pl.BlockSpec((1, N_MOD, d), lambda b, i: (b, 0, 0)),
            pl.BlockSpec(wpool.shape, lambda b, i: (0, 0, 0)),
            pl.BlockSpec((1, pool_w), const2),
            pl.BlockSpec(wout.shape, const2),
            pl.BlockSpec((1, d), const2),
            pl.BlockSpec(rw.shape, const2),
            pl.BlockSpec((1, LANES), const2),
        ],
        out_specs=[
            pl.BlockSpec((1, ts, d), tok),
            pl.BlockSpec((1, ts, d // 2), tok),
            pl.BlockSpec((1, ts, LANES), tok),
            pl.BlockSpec((1, ts, LANES), tok),
            pl.BlockSpec((8, LANES), const2),
        ],
        out_shape=[
            jax.ShapeDtypeStruct((bsz, n, d), F32),
            jax.ShapeDtypeStruct((bsz, n, d // 2), jnp.uint32),
            jax.ShapeDtypeStruct((bsz, n, LANES), F32),
            jax.ShapeDtypeStruct((bsz, n, LANES), jnp.int32),
            jax.ShapeDtypeStruct((8, LANES), F32),
        ],
        scratch_shapes=[
            pltpu.VMEM((ts + 2 * POOL_HALO, pool_w), F32),
            pltpu.VMEM((8, LANES), F32),
        ],
        compiler_params=pltpu.CompilerParams(
            dimension_semantics=("arbitrary", "arbitrary"), vmem_limit_bytes=VMEM_LIMIT),
        name="mix",
    )(att, u, u, u, x, mod, wpool, pscale, wout, g2, rw, rb)


def _sc_mesh():
    return plsc.VectorSubcoreMesh(core_axis_name="core", subcore_axis_name="subcore")


def _sc_worker_id():
    info = plsc.get_sparse_core_info()
    return lax.axis_index("subcore") * info.num_cores + lax.axis_index("core")


def _sc_num_workers():
    info = plsc.get_sparse_core_info()
    return info.num_cores * info.num_subcores


def _scatter_rows(x, dest3, n_out):
    t, d = x.shape
    top_k, n_chunks, rows = dest3.shape
    per_worker = n_chunks // _sc_num_workers()

    @functools.partial(
        pl.kernel,
        out_type=jax.ShapeDtypeStruct((n_out, d), x.dtype),
        mesh=_sc_mesh(),
        scratch_types=[pltpu.VMEM((top_k, rows), jnp.int32), pltpu.VMEM((rows, d), x.dtype)],
        name="sc_scatter",
    )
    def scatter(x_hbm, i_hbm, o_hbm, idx_v, rows_v):
        wid = _sc_worker_id()

        @pl.loop(0, per_worker)
        def _(c):
            chunk = wid * per_worker + c
            for kk in range(top_k):
                pltpu.sync_copy(i_hbm.at[kk, chunk], idx_v.at[kk])
            pltpu.sync_copy(x_hbm.at[pl.ds(pl.multiple_of(chunk * rows, 8), rows)], rows_v)
            for kk in range(top_k):
                pltpu.sync_copy(rows_v, o_hbm.at[idx_v.at[kk]])

    return scatter(x, dest3)


def _gather_rows(y, idx):
    n = idx.shape[0]
    d = y.shape[1]
    rows = SC_ROWS
    per_worker = n // _sc_num_workers()
    n_chunks = per_worker // rows

    @functools.partial(
        pl.kernel,
        out_type=jax.ShapeDtypeStruct((n, d), y.dtype),
        mesh=_sc_mesh(),
        scratch_types=[pltpu.VMEM((rows,), jnp.int32), pltpu.VMEM((rows, d), y.dtype)],
        name="sc_gather",
    )
    def gather(y_hbm, i_hbm, o_hbm, idx_v, rows_v):
        base = _sc_worker_id() * per_worker

        @pl.loop(0, n_chunks)
        def _(c):
            off = pl.multiple_of(base + c * rows, 8)
            pltpu.sync_copy(i_hbm.at[pl.ds(off, rows)], idx_v)
            pltpu.sync_copy(y_hbm.at[idx_v], rows_v)
            pltpu.sync_copy(rows_v, o_hbm.at[pl.ds(off, rows)])

    return gather(y, idx)


def _moe_kernel(blk_ref, exp_ref, lo_ref, hi_ref, cast_ref, init_ref, next_ref, xs_ref, wgu_hbm,
                bgu_ref, wd_hbm, bd_ref, ys_ref, wgu_f32, wd_f32, wgu_bf, wd_bf, sems):
    i = pl.program_id(0)

    def weight_copies(e):
        return (pltpu.make_async_copy(wgu_hbm.at[e], wgu_f32, sems.at[0]),
                pltpu.make_async_copy(wd_hbm.at[e], wd_f32, sems.at[1]))

    @pl.when(i == 0)
    def _():
        for cp in weight_copies(exp_ref[0]):
            cp.start()

    @pl.when(cast_ref[i] == 1)
    def _():
        for cp in weight_copies(exp_ref[i]):
            cp.wait()
        wgu_bf[...] = wgu_f32[...].astype(BF16)
        wd_bf[...] = wd_f32[...].astype(BF16)

        @pl.when(next_ref[i] >= 0)
        def _():
            for cp in weight_copies(next_ref[i]):
                cp.start()

    @pl.when(init_ref[i] == 1)
    def _():
        ys_ref[...] = jnp.zeros_like(ys_ref)

    lo = lo_ref[i]
    hi = hi_ref[i]

    def expert_rows(r0, n_rows):
        de = wd_bf.shape[0]
        x = _unpack_bf16_pairs(xs_ref[pl.ds(r0, n_rows), :]).astype(BF16)
        gu = _dot(x, wgu_bf[...]) + bgu_ref[0]
        g = jnp.minimum(gu[:, :de], SWIGLU_LIMIT)
        lin = jnp.clip(gu[:, de:], -SWIGLU_LIMIT, SWIGLU_LIMIT)
        act = g / (1.0 + jnp.exp(-SWIGLU_ALPHA * g)) * (lin + 1.0)
        y = _dot(act.astype(BF16), wd_bf[...]) + bd_ref[0]
        row = r0 + lax.broadcasted_iota(jnp.int32, (n_rows, 1), 0)
        mine = jnp.logical_and(row >= lo, row < hi)
        ys_ref[pl.ds(r0, n_rows), :] = jnp.where(mine, _pack_bf16_pairs(y), ys_ref[pl.ds(r0, n_rows), :])

    first = lo // MOE_SUB
    pieces = jnp.where(hi > lo, (hi + MOE_SUB - 1) // MOE_SUB - first, 0)
    for cnt in range(1, MOE_BLOCK // MOE_SUB + 1):
        @pl.when(pieces == cnt)
        def _(cnt=cnt):
            r0 = 0 if cnt * MOE_SUB == MOE_BLOCK else pl.multiple_of(first * MOE_SUB, MOE_SUB)
            expert_rows(r0, cnt * MOE_SUB)


def _moe_call(sched, xs, wgu, bgu, wd, bd):
    n_items = sched[0].shape[0]
    _, d, de2 = wgu.shape
    de = de2 // 2
    wmap = lambda i, blk, exp, lo, hi, cast, init, nxt: (exp[i], 0, 0)
    xmap = lambda i, blk, exp, lo, hi, cast, init, nxt: (blk[i], 0)
    return pl.pallas_call(
        _moe_kernel,
        grid_spec=pltpu.PrefetchScalarGridSpec(
            num_scalar_prefetch=7,
            grid=(n_items,),
            in_specs=[
                pl.BlockSpec((MOE_BLOCK, d // 2), xmap),
                pl.BlockSpec(memory_space=pl.ANY),
                pl.BlockSpec((1, 1, de2), wmap),
                pl.BlockSpec(memory_space=pl.ANY),
                pl.BlockSpec((1, 1, d), wmap),
            ],
            out_specs=pl.BlockSpec((MOE_BLOCK, d // 2), xmap),
            scratch_shapes=[
                pltpu.VMEM((d, de2), F32), pltpu.VMEM((de, d), F32),
                pltpu.VMEM((d, de2), BF16), pltpu.VMEM((de, d), BF16),
                pltpu.SemaphoreType.DMA((2,)),
            ],
        ),
        out_shape=jax.ShapeDtypeStruct(xs.shape, xs.dtype),
        compiler_params=pltpu.CompilerParams(
            dimension_semantics=("arbitrary",), vmem_limit_bytes=VMEM_LIMIT),
        name="moe",
    )(*sched, xs, wgu, bgu, wd, bd)


def _moe_schedule(counts, n_rows):
    n_blocks = n_rows // MOE_BLOCK
    n_items = n_blocks + N_EXPERTS
    ends = jnp.cumsum(counts)
    starts = ends - counts
    first_blk = starts // MOE_BLOCK
    last_blk = (ends - 1) // MOE_BLOCK
    items_per = jnp.where(counts > 0, last_blk - first_blk + 1, 0)
    item_ends = jnp.cumsum(items_per)
    item_starts = item_ends - items_per
    total = item_ends[-1]
    it = jnp.arange(n_items, dtype=jnp.int32)
    live = it < total
    itc = jnp.minimum(it, total - 1)
    exp = jnp.sum((item_ends[None, :] <= itc[:, None]).astype(jnp.int32), axis=1)
    is_exp = exp[:, None] == jnp.arange(N_EXPERTS, dtype=jnp.int32)[None, :]
    pick = lambda table: jnp.sum(jnp.where(is_exp, table[None, :], 0), axis=1)
    blk = pick(first_blk) + itc - pick(item_starts)
    lo = jnp.clip(pick(starts) - blk * MOE_BLOCK, 0, MOE_BLOCK)
    hi = jnp.clip(pick(ends) - blk * MOE_BLOCK, 0, MOE_BLOCK)
    hi = jnp.where(live, hi, lo)
    prev_exp = jnp.concatenate([jnp.full((1,), -1, jnp.int32), exp[:-1]])
    prev_blk = jnp.concatenate([jnp.full((1,), -1, jnp.int32), blk[:-1]])
    cast = jnp.logical_and(live, exp != prev_exp)
    init = jnp.logical_and(live, blk != prev_blk)
    ar = jnp.arange(N_EXPERTS, dtype=jnp.int32)
    later = jnp.logical_and(counts[None, :] > 0, ar[None, :] > ar[:, None])
    next_exp = jnp.min(jnp.where(later, ar[None, :], N_EXPERTS), axis=1)
    next_exp = jnp.where(next_exp == N_EXPERTS, -1, next_exp)
    nxt = pick(next_exp)
    as_i32 = lambda a: a.astype(jnp.int32)
    return tuple(as_i32(a) for a in (blk, exp, lo, hi, cast, init, nxt)), starts


def _combine_kernel(*refs):
    yg_refs = refs[:TOP_K]
    x1_ref, gate_ref, mod_ref, fg_ref, o_ref = refs[TOP_K:]
    gates = gate_ref[...]
    y = gates[:, 0:1] * _unpack_bf16_pairs(yg_refs[0][...])
    for kk in range(1, TOP_K):
        y = y + gates[:, kk:kk + 1] * _unpack_bf16_pairs(yg_refs[kk][...])
    m = mod_ref[0]
    x2 = x1_ref[...] + m[5:6] * y
    o_ref[...] = _rms(x2) * fg_ref[...]


def _combine_call(yg, x1, gates, mod, fg, n_seq, part):
    t, d = x1.shape
    tt = COMBINE_TT
    tiles_per_seq = n_seq // tt
    n_tiles = yg.shape[0] // (TOP_K * tt)
    first = part * n_tiles
    tok = lambda i: (first + i, 0)
    slot_specs = [pl.BlockSpec((tt, d // 2), functools.partial(lambda kk, i: (kk * n_tiles + i, 0), kk))
                  for kk in range(TOP_K)]
    return pl.pallas_call(
        _combine_kernel,
        grid=(n_tiles,),
        in_specs=slot_specs + [
            pl.BlockSpec((tt, d), tok),
            pl.BlockSpec((tt, LANES), tok),
            pl.BlockSpec((1, N_MOD, d), lambda i: ((first + i) // tiles_per_seq, 0, 0)),
            pl.BlockSpec((1, d), lambda i: (0, 0)),
        ],
        out_specs=pl.BlockSpec((tt, d), tok),
        out_shape=jax.ShapeDtypeStruct((t, d), F32),
        input_output_aliases={TOP_K: 0},
        compiler_params=pltpu.CompilerParams(
            dimension_semantics=("arbitrary",), vmem_limit_bytes=VMEM_LIMIT),
        name="combine",
    )(*([yg] * TOP_K), x1, gates, mod, fg)


def _pad_cols(a, width):
    return jnp.pad(a, ((0, 0), (0, width - a.shape[1])))


def _rope_slab(w_rope):
    return jnp.pad(w_rope, ((0, 0), (QK_NOPE, LANES - QK_NOPE - QK_ROPE)))


def _prep_w_in(w_in):
    kr0 = Q_LORA + KV_LORA
    w_kr = w_in[:, kr0:kr0 + QK_ROPE]
    return jnp.concatenate(
        [w_in[:, :kr0], _rope_slab(w_kr), w_in[:, kr0 + QK_ROPE:]], axis=1).astype(BF16)


def _prep_w_uq(w_uq):
    per = QK_NOPE + QK_ROPE
    rows = w_uq.shape[0]
    w = w_uq.reshape(rows, N_HEADS, per)
    slab = jnp.pad(w, ((0, 0), (0, 0), (0, LANES - per)))
    return slab.reshape(rows, -1).T.astype(BF16)


def _prep_w_ukv(w_ukv):
    per = QK_NOPE + V_DIM
    rows = w_ukv.shape[0]
    w = w_ukv.reshape(rows, N_HEADS, per)
    wk = jnp.pad(w[:, :, :QK_NOPE], ((0, 0), (0, 0), (0, LANES - QK_NOPE))).reshape(rows, -1)
    wvt = jnp.pad(jnp.transpose(w[:, :, QK_NOPE:], (1, 2, 0)), ((0, 0), (0, VT_ROWS - V_DIM), (0, 0)))
    return wk.astype(BF16), wvt.reshape(N_HEADS * VT_ROWS, rows).astype(BF16)


def _rope_tables(n_lat):
    rows = n_lat // GRID_W
    nf = QK_ROPE // 4
    row = jnp.repeat(jnp.arange(rows, dtype=F32), GRID_W)
    col = jnp.tile(jnp.arange(GRID_W, dtype=F32), rows)
    freqs = ROPE_BASE ** (-jnp.arange(nf, dtype=F32) / nf)
    ang = jnp.concatenate([row[:, None] * freqs, col[:, None] * freqs], axis=-1)
    cos, sin = jnp.cos(ang), jnp.sin(ang)
    ones = jnp.ones((n_lat, QK_NOPE), F32)
    zeros = jnp.zeros((n_lat, QK_NOPE), F32)
    cs = _pad_cols(jnp.concatenate([ones, cos, cos], axis=1), LANES)
    sn = _pad_cols(jnp.concatenate([zeros, -sin, sin], axis=1), LANES)
    return cs, sn


def kernel(x, c, ctx, c_ctx, w_mod, b_mod, norm1_g, w_in, q_norm_g, kv_norm_g, w_uq, w_ukv, w_pool,
           pool_scale, w_out, norm2_g, router_w, router_b, w_gate_up, b_gate_up, w_down, b_down,
           final_g):
    bsz, n, d = x.shape
    n_ctx = ctx.shape[1]
    t = bsz * n
    l = 0

    cc = jnp.concatenate([c, c_ctx[None, :], jnp.zeros((8 - bsz - 1, d), F32)], axis=0)
    mod = _mod_call(cc, w_mod[l], b_mod[l][None, :]).reshape(8, N_MOD, d)
    mod_lat, mod_ctx = mod[:bsz], mod[bsz:bsz + 1]

    win = _prep_w_in(w_in[l])
    wuqt = _prep_w_uq(w_uq[l])
    wk, wvt = _prep_w_ukv(w_ukv[l])
    cs, sn = _rope_tables(n)
    cs_ctx = jnp.broadcast_to((jnp.arange(LANES) < QK_NOPE + QK_ROPE).astype(F32), (n_ctx, LANES))
    sn_ctx = jnp.zeros((n_ctx, LANES), F32)
    g1 = norm1_g[l][None, :]
    qg = q_norm_g[l][None, :]
    kvg = kv_norm_g[l][None, :]

    qt, k, vt, u = _front_call(x, mod_lat, True, g1, win, qg, kvg, wuqt, wk, wvt, cs, sn,
                               is_ctx=False, ts=FRONT_TS)
    kc, vct = _front_call(ctx, mod_ctx, False, g1, win[:, Q_LORA:Q_LORA + 2 * LANES], qg, kvg, wuqt,
                          wk, wvt, cs_ctx, sn_ctx, is_ctx=True, ts=n_ctx)
    att = _attn_call(qt, k, vt, kc, vct)

    rw = _pad_cols(router_w[l], LANES)
    rw_hi = rw.astype(BF16)
    rw = jnp.concatenate([rw_hi, (rw - rw_hi.astype(F32)).astype(BF16)], axis=1)
    rb = jnp.concatenate([router_b[l], jnp.full((LANES - N_EXPERTS,), -jnp.inf, F32)])[None, :]
    x1, h2, gates, ridx, cnt = _mix_call(
        att, u, x, mod_lat, w_pool[l].astype(BF16), pool_scale[l][None, :], w_out[l].astype(BF16),
        norm2_g[l][None, :], rw, rb)

    counts = cnt[0, :N_EXPERTS].astype(jnp.int32)
    sched, starts = _moe_schedule(counts, t * TOP_K)
    ridx = ridx.reshape(t, LANES)
    is_exp = ridx[:, :TOP_K, None] == jnp.arange(N_EXPERTS, dtype=jnp.int32)
    dest = jnp.sum(jnp.where(is_exp, starts, 0), axis=-1) + ridx[:, TOP_K:2 * TOP_K]
    dest_t = dest.T

    xs = _scatter_rows(h2.reshape(t, d // 2), dest_t.reshape(TOP_K, t // SC_ROWS, SC_ROWS), t * TOP_K)
    ys = _moe_call(sched, xs, w_gate_up[l], b_gate_up[l][:, None, :], w_down[l],
                   b_down[l][:, None, :])
    out = x1.reshape(t, d)
    gates = gates.reshape(t, LANES)
    tp = t // COMBINE_PARTS
    for part in range(COMBINE_PARTS):
        idx = dest_t[:, part * tp:(part + 1) * tp].reshape(-1)
        yg = _gather_rows(ys, idx)
        out = _combine_call(yg, out, gates, mod_lat, final_g[None, :], n, part)
    return out.reshape(bsz, n, d)
```

```python
import functools
import math

import jax
import jax.numpy as jnp
from jax import lax
from jax.experimental import pallas as pl
from jax.experimental.pallas import tpu as pltpu
from jax.experimental.pallas import tpu_sc as plsc

F32 = jnp.float32
BF16 = jnp.bfloat16
HIGHEST = lax.Precision.HIGHEST

N_HEADS = 8
QK_NOPE = 64
QK_ROPE = 32
V_DIM = 64
Q_LORA = 256
KV_LORA = 128
GRID_W = 64
ROPE_BASE = 10000.0
POOL_WINDOWS = (2, 4, 8, 16)
POOL_CH = 128
N_EXPERTS = 32
TOP_K = 4
SWIGLU_LIMIT = 7.0
SWIGLU_ALPHA = 1.702
MOE_BLOCK = 512
MOE_SUB = 128
N_MOD = 6
EPS = 1e-6

LANES = 128
POOL_HALO = 8
POOL_ROWS = 256
POOL_K = 384
VT_ROWS = 80
VMEM_LIMIT = 56 * 1024 * 1024

FRONT_TS = 512
ATTN_TQ = 512
ATTN_TILES = 4
ATTN_KC = 512
MIX_TS = 512
COMBINE_TT = 256
COMBINE_PARTS = 4
SC_ROWS = 128


def _dot(a, b, **kw):
    return jnp.dot(a, b, preferred_element_type=F32, **kw)


def _dot_nt(a, b):
    return lax.dot_general(a, b, (((1,), (1,)), ((), ())), preferred_element_type=F32)


def _rms(x):
    return x * lax.rsqrt(jnp.mean(x * x, axis=-1, keepdims=True) + EPS)


def _pack_bf16_pairs(x):
    w = x.shape[1] // 2
    return pltpu.pack_elementwise([x[:, :w], x[:, w:]], packed_dtype=BF16)


def _unpack_bf16_pairs(words):
    halves = [pltpu.unpack_elementwise(words, index=h, packed_dtype=BF16, unpacked_dtype=F32)
              for h in range(2)]
    return jnp.concatenate(halves, axis=-1)


def _mod_kernel(c_ref, w_ref, b_ref, o_ref):
    c = c_ref[...]
    a = c / (1.0 + jnp.exp(-c))
    o_ref[...] = _dot(a, w_ref[...], precision=HIGHEST) + b_ref[...]


def _mod_call(cc, w_mod, b_mod):
    d = w_mod.shape[0]
    return pl.pallas_call(
        _mod_kernel,
        grid=(N_MOD,),
        in_specs=[
            pl.BlockSpec((8, d), lambda j: (0, 0)),
            pl.BlockSpec((d, d), lambda j: (0, j)),
            pl.BlockSpec((1, d), lambda j: (0, j)),
        ],
        out_specs=pl.BlockSpec((8, d), lambda j: (0, j)),
        out_shape=jax.ShapeDtypeStruct((8, N_MOD * d), F32),
        compiler_params=pltpu.CompilerParams(
            dimension_semantics=("arbitrary",), vmem_limit_bytes=VMEM_LIMIT),
        name="mod",
    )(cc, w_mod, b_mod)


def _front_kernel(x_ref, mod_ref, g1_ref, win_ref, qg_ref, kvg_ref, wuqt_ref, wk_ref, wvt_ref,
                  cs_ref, sn_ref, cst_ref, snt_ref, *out_refs, is_ctx, scale):
    x = x_ref[0]
    m = mod_ref[0]
    h = _rms(x) * g1_ref[...] * (1.0 + m[1:2]) + m[0:1]
    z = _dot(h.astype(BF16), win_ref[...])
    if is_ctx:
        k_ref, vt_ref = out_refs
        zkv = z
    else:
        qt_ref, k_ref, vt_ref, u_ref = out_refs
        zkv = z[:, Q_LORA:Q_LORA + 2 * LANES]
        u_ref[0] = z[:, Q_LORA + 2 * LANES:]

    ckv = _rms(zkv[:, :KV_LORA]) * kvg_ref[...]
    kk = _dot(ckv.astype(BF16), wk_ref[...])
    kra = zkv[:, LANES:2 * LANES]
    lane = lax.broadcasted_iota(jnp.int32, kra.shape, 1)
    half = QK_ROPE // 2
    partner = jnp.where(lane < QK_NOPE + half, pltpu.roll(kra, LANES - half, axis=1),
                        pltpu.roll(kra, half, axis=1))
    kr = kra * cs_ref[...] + partner * sn_ref[...]
    for hd in range(N_HEADS):
        k_ref[0, hd] = (kk[:, hd * LANES:(hd + 1) * LANES] + kr).astype(BF16)
    vt = _dot(wvt_ref[...], ckv.T.astype(BF16))
    row = lax.broadcasted_iota(jnp.int32, (VT_ROWS, 1), 0)
    ones = jnp.where(row == V_DIM, 1.0, 0.0)
    for hd in range(N_HEADS):
        vt_ref[0, hd] = (vt[hd * VT_ROWS:(hd + 1) * VT_ROWS] + ones).astype(BF16)

    if not is_ctx:
        cq = _rms(z[:, :Q_LORA]) * qg_ref[...]
        qqt = _dot(wuqt_ref[...], cq.T.astype(BF16))
        cst = cst_ref[...]
        snt = snt_ref[...]
        r1, r2, r3 = QK_NOPE, QK_NOPE + QK_ROPE // 2, QK_NOPE + QK_ROPE
        for hd in range(N_HEADS):
            qa = qqt[hd * LANES:(hd + 1) * LANES]
            qb = jnp.concatenate([qa[:r1], qa[r2:r3], qa[r1:r2], qa[r3:]], axis=0)
            qt_ref[0, hd] = ((qa * cst + qb * snt) * scale).astype(BF16)


def _front_call(xs, mod, mod_per_batch, g1, win, qg, kvg, wuqt, wk, wvt, cs, sn, *, is_ctx, ts):
    bsz, n, d = xs.shape
    nt = n // ts
    scale = math.log2(math.e) / math.sqrt(QK_NOPE + QK_ROPE)
    const = lambda b, i: (0, 0)
    mod_map = (lambda b, i: (b, 0, 0)) if mod_per_batch else (lambda b, i: (0, 0, 0))
    in_specs = [
        pl.BlockSpec((1, ts, d), lambda b, i: (b, i, 0)),
        pl.BlockSpec((1, N_MOD, d), mod_map),
        pl.BlockSpec((1, d), const),
        pl.BlockSpec(win.shape, const),
        pl.BlockSpec((1, Q_LORA), const),
        pl.BlockSpec((1, KV_LORA), const),
        pl.BlockSpec(wuqt.shape, const),
        pl.BlockSpec(wk.shape, const),
        pl.BlockSpec(wvt.shape, const),
        pl.BlockSpec((ts, LANES), lambda b, i: (i, 0)),
        pl.BlockSpec((ts, LANES), lambda b, i: (i, 0)),
        pl.BlockSpec((LANES, ts), lambda b, i: (0, i)),
        pl.BlockSpec((LANES, ts), lambda b, i: (0, i)),
    ]
    k_spec = pl.BlockSpec((1, N_HEADS, ts, LANES), lambda b, i: (b, 0, i, 0))
    k_shape = jax.ShapeDtypeStruct((bsz, N_HEADS, n, LANES), BF16)
    qt_spec = pl.BlockSpec((1, N_HEADS, LANES, ts), lambda b, i: (b, 0, 0, i))
    qt_shape = jax.ShapeDtypeStruct((bsz, N_HEADS, LANES, n), BF16)
    vt_spec = pl.BlockSpec((1, N_HEADS, VT_ROWS, ts), lambda b, i: (b, 0, 0, i))
    vt_shape = jax.ShapeDtypeStruct((bsz, N_HEADS, VT_ROWS, n), BF16)
    if is_ctx:
        out_specs = [k_spec, vt_spec]
        out_shape = [k_shape, vt_shape]
    else:
        pool_w = win.shape[1] - Q_LORA - 2 * LANES
        out_specs = [qt_spec, k_spec, vt_spec, pl.BlockSpec((1, ts, pool_w), lambda b, i: (b, i, 0))]
        out_shape = [qt_shape, k_shape, vt_shape, jax.ShapeDtypeStruct((bsz, n, pool_w), F32)]
    return pl.pallas_call(
        functools.partial(_front_kernel, is_ctx=is_ctx, scale=scale),
        grid=(bsz, nt),
        in_specs=in_specs,
        out_specs=out_specs,
        out_shape=out_shape,
        compiler_params=pltpu.CompilerParams(
            dimension_semantics=("arbitrary", "arbitrary"), vmem_limit_bytes=VMEM_LIMIT),
        name="front_ctx" if is_ctx else "front",
    )(xs, mod, g1, win, qg, kvg, wuqt, wk, wvt, cs, sn, cs.T, sn.T)


def _attn_kernel(qt_ref, k_ref, vt_ref, kc_ref, vct_ref, o_ref, s_ref, mlc_ref, mxb_ref, oe_ref):
    n_ctx = kc_ref.shape[2]
    n_lat = k_ref.shape[2]
    chunks = [(None, 0, n_ctx)] + [(c * ATTN_KC, n_ctx + c * ATTN_KC, ATTN_KC)
                                   for c in range(n_lat // ATTN_KC)]
    sub = 8

    def q_cols(tile):
        if isinstance(tile, int):
            return pl.ds(tile * ATTN_TQ, ATTN_TQ)
        return pl.ds(pl.multiple_of(tile * ATTN_TQ, ATTN_TQ), ATTN_TQ)

    def score_chunk(tile, hd, buf, ci):
        off, soff, w = chunks[ci]
        keys = kc_ref[0, hd] if off is None else k_ref[0, hd, off:off + w, :]
        s = _dot(keys, qt_ref[0, hd, :, q_cols(tile)])
        s_ref[buf, soff:soff + w, :] = s
        mx = s[0:sub]
        for r in range(1, w // sub):
            mx = jnp.maximum(mx, s[r * sub:(r + 1) * sub])
        mlc_ref[buf, ci] = mx

    def row_max(buf):
        mx = mlc_ref[buf, 0]
        for ci in range(1, len(chunks)):
            mx = jnp.maximum(mx, mlc_ref[buf, ci])
        mxb_ref[...] = jnp.broadcast_to(jnp.max(mx, axis=0, keepdims=True), mx.shape)

    def weight_chunk(hd, buf, ci):
        off, soff, w = chunks[ci]
        p = jnp.exp2(s_ref[buf, soff:soff + w, :] - mxb_ref[0:1, :]).astype(BF16)
        vt = vct_ref[0, hd] if off is None else vt_ref[0, hd, :, off:off + w]
        return _dot(vt, p)

    def stage(hw, bw, scoring, bs):
        row_max(bw)
        acc = None
        for ci in range(len(chunks)):
            if scoring is not None:
                score_chunk(*scoring, bs, ci)
            part = weight_chunk(hw, bw, ci)
            acc = part if acc is None else acc + part
        return acc[0:V_DIM] / acc[V_DIM:V_DIM + 1]

    def write_pair(tile, j, ot_odd):
        pair_t = jnp.concatenate([oe_ref[...], ot_odd], axis=0)
        o_ref[0, j, q_cols(tile), :] = pair_t.T.astype(BF16)

    pairs_per_tile = N_HEADS // 2
    n_pairs = (qt_ref.shape[3] // ATTN_TQ) * pairs_per_tile

    def split(p):
        return p // pairs_per_tile, p % pairs_per_tile

    for ci in range(len(chunks)):
        score_chunk(0, 0, 0, ci)

    def head_pair(p, carry):
        tile, j = split(p)
        nxt_tile, nxt_j = split(p + 1)
        oe_ref[...] = stage(2 * j, 0, (tile, 2 * j + 1), 1)
        write_pair(tile, j, stage(2 * j + 1, 1, (nxt_tile, 2 * nxt_j), 0))
        return carry

    lax.fori_loop(0, n_pairs - 1, head_pair, 0)
    tile, j = split(n_pairs - 1)
    oe_ref[...] = stage(2 * j, 0, (tile, 2 * j + 1), 1)
    write_pair(tile, j, stage(2 * j + 1, 1, None, None))


def _attn_call(qt, k, vt, kc, vct):
    bsz, _, _, n = qt.shape
    n_ctx = kc.shape[2]
    tq = ATTN_TQ
    n_chunks = 1 + n // ATTN_KC
    per_batch = lambda b, i: (b, 0, 0, 0)
    resident = dict(pipeline_mode=pl.Buffered(1))
    rows_per_step = ATTN_TILES * tq
    return pl.pallas_call(
        _attn_kernel,
        grid=(bsz, n // rows_per_step),
        in_specs=[
            pl.BlockSpec((1, N_HEADS, LANES, rows_per_step), lambda b, i: (b, 0, 0, i)),
            pl.BlockSpec((1, N_HEADS, n, LANES), per_batch, **resident),
            pl.BlockSpec((1, N_HEADS, VT_ROWS, n), per_batch, **resident),
            pl.BlockSpec((1, N_HEADS, n_ctx, LANES), per_batch, **resident),
            pl.BlockSpec((1, N_HEADS, VT_ROWS, n_ctx), per_batch, **resident),
        ],
        out_specs=pl.BlockSpec((1, N_HEADS // 2, rows_per_step, LANES), lambda b, i: (b, 0, i, 0)),
        out_shape=jax.ShapeDtypeStruct((bsz, N_HEADS // 2, n, LANES), BF16),
        scratch_shapes=[
            pltpu.VMEM((2, n_ctx + n, tq), F32),
            pltpu.VMEM((2, n_chunks, 8, tq), F32),
            pltpu.VMEM((8, tq), F32),
            pltpu.VMEM((V_DIM, tq), F32),
        ],
        compiler_params=pltpu.CompilerParams(
            dimension_semantics=("arbitrary", "arbitrary"), vmem_limit_bytes=VMEM_LIMIT),
        name="attn",
    )(qt, k, vt, kc, vct)


def _mix_kernel(att_ref, u_ref, up_ref, un_ref, x_ref, mod_ref, band_ref, wpool_ref, pscale_ref,
                wout_ref, g2_ref, rw_ref, rb_ref,
                x1_ref, h2_ref, gate_ref, ridx_ref, cnt_ref, ue_ref, base_ref, *, n_seq):
    b = pl.program_id(0)
    i = pl.program_id(1)
    n_tiles = pl.num_programs(1)
    ts = u_ref.shape[1]

    @pl.when(jnp.logical_and(b == 0, i == 0))
    def _():
        base_ref[...] = jnp.zeros_like(base_ref)
        ue_ref[2 * POOL_HALO + ts:] = jnp.zeros_like(ue_ref[2 * POOL_HALO + ts:])

    u = u_ref[0]
    ue_ref[0:POOL_HALO] = jnp.where(i > 0, up_ref[0], 0.0)
    ue_ref[POOL_HALO:POOL_HALO + ts] = u
    ue_ref[POOL_HALO + ts:2 * POOL_HALO + ts] = jnp.where(i < n_tiles - 1, un_ref[0], 0.0)
    t = i * ts + lax.broadcasted_iota(jnp.int32, (ts, 1), 0)
    ys = []
    for g, w in enumerate(POOL_WINDOWS):
        half = w // 2
        lanes = slice(g * POOL_CH, (g + 1) * POOL_CH)
        sums = []
        for r in range(ts // POOL_ROWS):
            win = ue_ref[r * POOL_ROWS:r * POOL_ROWS + POOL_K, lanes]
            hi = win.astype(BF16)
            lo = (win - hi.astype(F32)).astype(BF16)
            both = _dot(band_ref[g], jnp.concatenate([hi, lo], axis=1))
            sums.append(both[:, :POOL_CH] + both[:, POOL_CH:])
        ws = jnp.concatenate(sums, axis=0)
        count = (jnp.minimum(t + half, n_seq) - jnp.maximum(t - half, 0)).astype(F32)
        mixed = (ws / count - u[:, lanes]).astype(BF16)
        ys.append(_dot(mixed, wpool_ref[g]))
    pool = jnp.concatenate(ys, axis=-1) * pscale_ref[...]

    cat = jnp.concatenate([att_ref[0, j] for j in range(N_HEADS // 2)] + [pool.astype(BF16)], axis=-1)
    m = mod_ref[0]
    x1 = x_ref[0] + m[2:3] * _dot(cat, wout_ref[...])
    x1_ref[0] = x1
    h2 = _rms(x1) * g2_ref[...] * (1.0 + m[4:5]) + m[3:4]
    h2_ref[0] = _pack_bf16_pairs(h2)

    h_hi = h2.astype(BF16)
    h_lo = (h2 - h_hi.astype(F32)).astype(BF16)
    hi_part = _dot(h_hi, rw_ref[...])
    logits = (hi_part[:, :LANES] + hi_part[:, LANES:] + _dot(h_lo, rw_ref[:, :LANES])) + rb_ref[...]
    lane = lax.broadcasted_iota(jnp.int32, logits.shape, 1).astype(F32)
    vals, idxs = [], []
    for _k in range(TOP_K):
        mv = jnp.max(logits, axis=-1, keepdims=True)
        ix = jnp.min(jnp.where(logits == mv, lane, float(LANES)), axis=-1, keepdims=True)
        vals.append(mv)
        idxs.append(ix)
        logits = jnp.where(lane == ix, -jnp.inf, logits)
    es = [jnp.exp(v - vals[0]) for v in vals]
    den = es[0] + es[1] + es[2] + es[3]

    onehot = jnp.zeros(lane.shape, F32)
    for ix in idxs:
        onehot = onehot + jnp.where(lane == ix, 1.0, 0.0)
    row = lax.broadcasted_iota(jnp.int32, (ts, ts), 0)
    col = lax.broadcasted_iota(jnp.int32, (ts, ts), 1)
    tri = jnp.where(col < row, 1.0, 0.0).astype(BF16)
    before = _dot(tri, onehot.astype(BF16)) + base_ref[0:1, :]
    base_new = base_ref[0:1, :] + jnp.sum(onehot, axis=0, keepdims=True)
    base_ref[...] = jnp.broadcast_to(base_new, base_ref.shape)
    cnt_ref[...] = jnp.broadcast_to(base_new, cnt_ref.shape)

    gate_out = jnp.zeros(lane.shape, F32)
    ridx_out = jnp.zeros(lane.shape, F32)
    for kk in range(TOP_K):
        rank = jnp.sum(jnp.where(lane == idxs[kk], before, 0.0), axis=-1, keepdims=True)
        gate_out = jnp.where(lane == float(kk), es[kk] / den, gate_out)
        ridx_out = jnp.where(lane == float(kk), idxs[kk], ridx_out)
        ridx_out = jnp.where(lane == float(TOP_K + kk), rank, ridx_out)
    gate_ref[0] = gate_out
    ridx_ref[0] = ridx_out.astype(jnp.int32)


def _mix_call(att, u, x, mod, wpool, pscale, wout, g2, rw, rb):
    bsz, n, d = x.shape
    ts = MIX_TS
    nt = n // ts
    pool_w = u.shape[2]
    hb = ts // POOL_HALO
    n_halo_blocks = n // POOL_HALO
    const2 = lambda b, i: (0, 0)
    tok = lambda b, i: (b, i, 0)
    r_i = jnp.arange(POOL_ROWS)[None, :, None]
    c_i = jnp.arange(POOL_K)[None, None, :]
    h_i = jnp.asarray([w // 2 for w in POOL_WINDOWS])[:, None, None]
    band = jnp.logical_and(c_i >= r_i + POOL_HALO - h_i, c_i < r_i + POOL_HALO + h_i).astype(BF16)
    return pl.pallas_call(
        functools.partial(_mix_kernel, n_seq=n),
        grid=(bsz, nt),
        in_specs=[
            pl.BlockSpec((1, N_HEADS // 2, ts, LANES), lambda b, i: (b, 0, i, 0)),
            pl.BlockSpec((1, ts, pool_w), tok),
            pl.BlockSpec((1, POOL_HALO, pool_w), lambda b, i: (b, jnp.maximum(i * hb - 1, 0), 0)),
            pl.BlockSpec((1, POOL_HALO, pool_w),
                         lambda b, i: (b, jnp.minimum((i + 1) * hb, n_halo_blocks - 1), 0)),
            pl.BlockSpec((1, ts, d), tok),
            pl.BlockSpec((1, N_MOD, d), lambda b, i: (b, 0, 0)),
            pl.BlockSpec(band.shape, lambda b, i: (0, 0, 0)),
            pl.BlockSpec(wpool.shape, lambda b, i: (0, 0, 0)),
            pl.BlockSpec((1, pool_w), const2),
            pl.BlockSpec(wout.shape, const2),
            pl.BlockSpec((1, d), const2),
            pl.BlockSpec(rw.shape, const2),
            pl.BlockSpec((1, LANES), const2),
        ],
        out_specs=[
            pl.BlockSpec((1, ts, d), tok),
            pl.BlockSpec((1, ts, d // 2), tok),
            pl.BlockSpec((1, ts, LANES), tok),
            pl.BlockSpec((1, ts, LANES), tok),
            pl.BlockSpec((8, LANES), const2),
        ],
        out_shape=[
            jax.ShapeDtypeStruct((bsz, n, d), F32),
            jax.ShapeDtypeStruct((bsz, n, d // 2), jnp.uint32),
            jax.ShapeDtypeStruct((bsz, n, LANES), F32),
            jax.ShapeDtypeStruct((bsz, n, LANES), jnp.int32),
            jax.ShapeDtypeStruct((8, LANES), F32),
        ],
        scratch_shapes=[
            pltpu.VMEM((ts - POOL_ROWS + POOL_K, pool_w), F32),
            pltpu.VMEM((8, LANES), F32),
        ],
        compiler_params=pltpu.CompilerParams(
            dimension_semantics=("arbitrary", "arbitrary"), vmem_limit_bytes=VMEM_LIMIT),
        name="mix",
    )(att, u, u, u, x, mod, band, wpool, pscale, wout, g2, rw, rb)


def _sc_mesh():
    return plsc.VectorSubcoreMesh(core_axis_name="core", subcore_axis_name="subcore")


def _sc_worker_id():
    info = plsc.get_sparse_core_info()
    return lax.axis_index("subcore") * info.num_cores + lax.axis_index("core")


def _sc_num_workers():
    info = plsc.get_sparse_core_info()
    return info.num_cores * info.num_subcores


def _scatter_rows(x, dest3, n_out):
    t, d = x.shape
    top_k, n_chunks, rows = dest3.shape
    per_worker = n_chunks // _sc_num_workers()

    @functools.partial(
        pl.kernel,
        out_type=jax.ShapeDtypeStruct((n_out, d), x.dtype),
        mesh=_sc_mesh(),
        scratch_types=[pltpu.VMEM((top_k, rows), jnp.int32), pltpu.VMEM((rows, d), x.dtype)],
        name="sc_scatter",
    )
    def scatter(x_hbm, i_hbm, o_hbm, idx_v, rows_v):
        wid = _sc_worker_id()

        @pl.loop(0, per_worker)
        def _(c):
            chunk = wid * per_worker + c
            for kk in range(top_k):
                pltpu.sync_copy(i_hbm.at[kk, chunk], idx_v.at[kk])
            pltpu.sync_copy(x_hbm.at[pl.ds(pl.multiple_of(chunk * rows, 8), rows)], rows_v)
            for kk in range(top_k):
                pltpu.sync_copy(rows_v, o_hbm.at[idx_v.at[kk]])

    return scatter(x, dest3)


def _gather_rows(y, idx):
    n = idx.shape[0]
    d = y.shape[1]
    rows = SC_ROWS
    per_worker = n // _sc_num_workers()
    n_chunks = per_worker // rows

    @functools.partial(
        pl.kernel,
        out_type=jax.ShapeDtypeStruct((n, d), y.dtype),
        mesh=_sc_mesh(),
        scratch_types=[pltpu.VMEM((rows,), jnp.int32), pltpu.VMEM((rows, d), y.dtype)],
        name="sc_gather",
    )
    def gather(y_hbm, i_hbm, o_hbm, idx_v, rows_v):
        base = _sc_worker_id() * per_worker

        @pl.loop(0, n_chunks)
        def _(c):
            off = pl.multiple_of(base + c * rows, 8)
            pltpu.sync_copy(i_hbm.at[pl.ds(off, rows)], idx_v)
            pltpu.sync_copy(y_hbm.at[idx_v], rows_v)
            pltpu.sync_copy(rows_v, o_hbm.at[pl.ds(off, rows)])

    return gather(y, idx)


def _moe_kernel(blk_ref, exp_ref, lo_ref, hi_ref, cast_ref, init_ref, next_ref, xs_ref, wgu_hbm,
                bgu_ref, wd_hbm, bd_ref, ys_ref, wgu_f32, wd_f32, wgu_bf, wd_bf, sems):
    i = pl.program_id(0)

    def weight_copies(e):
        return (pltpu.make_async_copy(wgu_hbm.at[e], wgu_f32, sems.at[0]),
                pltpu.make_async_copy(wd_hbm.at[e], wd_f32, sems.at[1]))

    @pl.when(i == 0)
    def _():
        for cp in weight_copies(exp_ref[0]):
            cp.start()

    @pl.when(cast_ref[i] == 1)
    def _():
        for cp in weight_copies(exp_ref[i]):
            cp.wait()
        wgu_bf[...] = wgu_f32[...].astype(BF16)
        wd_bf[...] = wd_f32[...].astype(BF16)

        @pl.when(next_ref[i] >= 0)
        def _():
            for cp in weight_copies(next_ref[i]):
                cp.start()

    @pl.when(init_ref[i] == 1)
    def _():
        ys_ref[...] = jnp.zeros_like(ys_ref)

    lo = lo_ref[i]
    hi = hi_ref[i]

    def expert_rows(r0, n_rows):
        de = wd_bf.shape[0]
        x = _unpack_bf16_pairs(xs_ref[pl.ds(r0, n_rows), :]).astype(BF16)
        gu = _dot(x, wgu_bf[...]) + bgu_ref[0]
        g = jnp.minimum(gu[:, :de], SWIGLU_LIMIT)
        lin = jnp.clip(gu[:, de:], -SWIGLU_LIMIT, SWIGLU_LIMIT)
        act = g / (1.0 + jnp.exp(-SWIGLU_ALPHA * g)) * (lin + 1.0)
        y = _dot(act.astype(BF16), wd_bf[...]) + bd_ref[0]
        row = r0 + lax.broadcasted_iota(jnp.int32, (n_rows, 1), 0)
        mine = jnp.logical_and(row >= lo, row < hi)
        ys_ref[pl.ds(r0, n_rows), :] = jnp.where(mine, _pack_bf16_pairs(y), ys_ref[pl.ds(r0, n_rows), :])

    first = lo // MOE_SUB
    pieces = jnp.where(hi > lo, (hi + MOE_SUB - 1) // MOE_SUB - first, 0)
    for cnt in range(1, MOE_BLOCK // MOE_SUB + 1):
        @pl.when(pieces == cnt)
        def _(cnt=cnt):
            r0 = 0 if cnt * MOE_SUB == MOE_BLOCK else pl.multiple_of(first * MOE_SUB, MOE_SUB)
            expert_rows(r0, cnt * MOE_SUB)


def _moe_call(sched, xs, wgu, bgu, wd, bd):
    n_items = sched[0].shape[0]
    _, d, de2 = wgu.shape
    de = de2 // 2
    wmap = lambda i, blk, exp, lo, hi, cast, init, nxt: (exp[i], 0, 0)
    xmap = lambda i, blk, exp, lo, hi, cast, init, nxt: (blk[i], 0)
    return pl.pallas_call(
        _moe_kernel,
        grid_spec=pltpu.PrefetchScalarGridSpec(
            num_scalar_prefetch=7,
            grid=(n_items,),
            in_specs=[
                pl.BlockSpec((MOE_BLOCK, d // 2), xmap),
                pl.BlockSpec(memory_space=pl.ANY),
                pl.BlockSpec((1, 1, de2), wmap),
                pl.BlockSpec(memory_space=pl.ANY),
                pl.BlockSpec((1, 1, d), wmap),
            ],
            out_specs=pl.BlockSpec((MOE_BLOCK, d // 2), xmap),
            scratch_shapes=[
                pltpu.VMEM((d, de2), F32), pltpu.VMEM((de, d), F32),
                pltpu.VMEM((d, de2), BF16), pltpu.VMEM((de, d), BF16),
                pltpu.SemaphoreType.DMA((2,)),
            ],
        ),
        out_shape=jax.ShapeDtypeStruct(xs.shape, xs.dtype),
        compiler_params=pltpu.CompilerParams(
            dimension_semantics=("arbitrary",), vmem_limit_bytes=VMEM_LIMIT),
        name="moe",
    )(*sched, xs, wgu, bgu, wd, bd)


def _moe_schedule(counts, n_rows):
    n_blocks = n_rows // MOE_BLOCK
    n_items = n_blocks + N_EXPERTS
    ends = jnp.cumsum(counts)
    starts = ends - counts
    first_blk = starts // MOE_BLOCK
    last_blk = (ends - 1) // MOE_BLOCK
    items_per = jnp.where(counts > 0, last_blk - first_blk + 1, 0)
    item_ends = jnp.cumsum(items_per)
    item_starts = item_ends - items_per
    total = item_ends[-1]
    it = jnp.arange(n_items, dtype=jnp.int32)
    live = it < total
    itc = jnp.minimum(it, total - 1)
    exp = jnp.sum((item_ends[None, :] <= itc[:, None]).astype(jnp.int32), axis=1)
    is_exp = exp[:, None] == jnp.arange(N_EXPERTS, dtype=jnp.int32)[None, :]
    pick = lambda table: jnp.sum(jnp.where(is_exp, table[None, :], 0), axis=1)
    blk = pick(first_blk) + itc - pick(item_starts)
    lo = jnp.clip(pick(starts) - blk * MOE_BLOCK, 0, MOE_BLOCK)
    hi = jnp.clip(pick(ends) - blk * MOE_BLOCK, 0, MOE_BLOCK)
    hi = jnp.where(live, hi, lo)
    prev_exp = jnp.concatenate([jnp.full((1,), -1, jnp.int32), exp[:-1]])
    prev_blk = jnp.concatenate([jnp.full((1,), -1, jnp.int32), blk[:-1]])
    cast = jnp.logical_and(live, exp != prev_exp)
    init = jnp.logical_and(live, blk != prev_blk)
    ar = jnp.arange(N_EXPERTS, dtype=jnp.int32)
    later = jnp.logical_and(counts[None, :] > 0, ar[None, :] > ar[:, None])
    next_exp = jnp.min(jnp.where(later, ar[None, :], N_EXPERTS), axis=1)
    next_exp = jnp.where(next_exp == N_EXPERTS, -1, next_exp)
    nxt = pick(next_exp)
    as_i32 = lambda a: a.astype(jnp.int32)
    return tuple(as_i32(a) for a in (blk, exp, lo, hi, cast, init, nxt)), starts


def _combine_kernel(*refs):
    yg_refs = refs[:TOP_K]
    x1_ref, gate_ref, mod_ref, fg_ref, o_ref = refs[TOP_K:]
    gates = gate_ref[...]
    y = gates[:, 0:1] * _unpack_bf16_pairs(yg_refs[0][...])
    for kk in range(1, TOP_K):
        y = y + gates[:, kk:kk + 1] * _unpack_bf16_pairs(yg_refs[kk][...])
    m = mod_ref[0]
    x2 = x1_ref[...] + m[5:6] * y
    o_ref[...] = _rms(x2) * fg_ref[...]


def _combine_call(yg, x1, gates, mod, fg, n_seq, part):
    t, d = x1.shape
    tt = COMBINE_TT
    tiles_per_seq = n_seq // tt
    n_tiles = yg.shape[0] // (TOP_K * tt)
    first = part * n_tiles
    tok = lambda i: (first + i, 0)
    slot_specs = [pl.BlockSpec((tt, d // 2), functools.partial(lambda kk, i: (kk * n_tiles + i, 0), kk))
                  for kk in range(TOP_K)]
    return pl.pallas_call(
        _combine_kernel,
        grid=(n_tiles,),
        in_specs=slot_specs + [
            pl.BlockSpec((tt, d), tok),
            pl.BlockSpec((tt, LANES), tok),
            pl.BlockSpec((1, N_MOD, d), lambda i: ((first + i) // tiles_per_seq, 0, 0)),
            pl.BlockSpec((1, d), lambda i: (0, 0)),
        ],
        out_specs=pl.BlockSpec((tt, d), tok),
        out_shape=jax.ShapeDtypeStruct((t, d), F32),
        input_output_aliases={TOP_K: 0},
        compiler_params=pltpu.CompilerParams(
            dimension_semantics=("arbitrary",), vmem_limit_bytes=VMEM_LIMIT),
        name="combine",
    )(*([yg] * TOP_K), x1, gates, mod, fg)


def _pad_cols(a, width):
    return jnp.pad(a, ((0, 0), (0, width - a.shape[1])))


def _rope_slab(w_rope):
    return jnp.pad(w_rope, ((0, 0), (QK_NOPE, LANES - QK_NOPE - QK_ROPE)))


def _prep_w_in(w_in):
    kr0 = Q_LORA + KV_LORA
    w_kr = w_in[:, kr0:kr0 + QK_ROPE]
    return jnp.concatenate(
        [w_in[:, :kr0], _rope_slab(w_kr), w_in[:, kr0 + QK_ROPE:]], axis=1).astype(BF16)


def _prep_w_uq(w_uq):
    per = QK_NOPE + QK_ROPE
    rows = w_uq.shape[0]
    w = w_uq.reshape(rows, N_HEADS, per)
    slab = jnp.pad(w, ((0, 0), (0, 0), (0, LANES - per)))
    return slab.reshape(rows, -1).T.astype(BF16)


def _prep_w_ukv(w_ukv):
    per = QK_NOPE + V_DIM
    rows = w_ukv.shape[0]
    w = w_ukv.reshape(rows, N_HEADS, per)
    wk = jnp.pad(w[:, :, :QK_NOPE], ((0, 0), (0, 0), (0, LANES - QK_NOPE))).reshape(rows, -1)
    wvt = jnp.pad(jnp.transpose(w[:, :, QK_NOPE:], (1, 2, 0)), ((0, 0), (0, VT_ROWS - V_DIM), (0, 0)))
    return wk.astype(BF16), wvt.reshape(N_HEADS * VT_ROWS, rows).astype(BF16)


def _rope_tables(n_lat):
    rows = n_lat // GRID_W
    nf = QK_ROPE // 4
    row = jnp.repeat(jnp.arange(rows, dtype=F32), GRID_W)
    col = jnp.tile(jnp.arange(GRID_W, dtype=F32), rows)
    freqs = ROPE_BASE ** (-jnp.arange(nf, dtype=F32) / nf)
    ang = jnp.concatenate([row[:, None] * freqs, col[:, None] * freqs], axis=-1)
    cos, sin = jnp.cos(ang), jnp.sin(ang)
    ones = jnp.ones((n_lat, QK_NOPE), F32)
    zeros = jnp.zeros((n_lat, QK_NOPE), F32)
    cs = _pad_cols(jnp.concatenate([ones, cos, cos], axis=1), LANES)
    sn = _pad_cols(jnp.concatenate([zeros, -sin, sin], axis=1), LANES)
    return cs, sn


def kernel(x, c, ctx, c_ctx, w_mod, b_mod, norm1_g, w_in, q_norm_g, kv_norm_g, w_uq, w_ukv, w_pool,
           pool_scale, w_out, norm2_g, router_w, router_b, w_gate_up, b_gate_up, w_down, b_down,
           final_g):
    bsz, n, d = x.shape
    n_ctx = ctx.shape[1]
    t = bsz * n
    l = 0

    cc = jnp.concatenate([c, c_ctx[None, :], jnp.zeros((8 - bsz - 1, d), F32)], axis=0)
    mod = _mod_call(cc, w_mod[l], b_mod[l][None, :]).reshape(8, N_MOD, d)
    mod_lat, mod_ctx = mod[:bsz], mod[bsz:bsz + 1]

    win = _prep_w_in(w_in[l])
    wuqt = _prep_w_uq(w_uq[l])
    wk, wvt = _prep_w_ukv(w_ukv[l])
    cs, sn = _rope_tables(n)
    cs_ctx = jnp.broadcast_to((jnp.arange(LANES) < QK_NOPE + QK_ROPE).astype(F32), (n_ctx, LANES))
    sn_ctx = jnp.zeros((n_ctx, LANES), F32)
    g1 = norm1_g[l][None, :]
    qg = q_norm_g[l][None, :]
    kvg = kv_norm_g[l][None, :]

    qt, k, vt, u = _front_call(x, mod_lat, True, g1, win, qg, kvg, wuqt, wk, wvt, cs, sn,
                               is_ctx=False, ts=FRONT_TS)
    kc, vct = _front_call(ctx, mod_ctx, False, g1, win[:, Q_LORA:Q_LORA + 2 * LANES], qg, kvg, wuqt,
                          wk, wvt, cs_ctx, sn_ctx, is_ctx=True, ts=n_ctx)
    att = _attn_call(qt, k, vt, kc, vct)

    rw = _pad_cols(router_w[l], LANES)
    rw_hi = rw.astype(BF16)
    rw = jnp.concatenate([rw_hi, (rw - rw_hi.astype(F32)).astype(BF16)], axis=1)
    rb = jnp.concatenate([router_b[l], jnp.full((LANES - N_EXPERTS,), -jnp.inf, F32)])[None, :]
    x1, h2, gates, ridx, cnt = _mix_call(
        att, u, x, mod_lat, w_pool[l].astype(BF16), pool_scale[l][None, :], w_out[l].astype(BF16),
        norm2_g[l][None, :], rw, rb)

    counts = cnt[0, :N_EXPERTS].astype(jnp.int32)
    sched, starts = _moe_schedule(counts, t * TOP_K)
    ridx = ridx.reshape(t, LANES)
    is_exp = ridx[:, :TOP_K, None] == jnp.arange(N_EXPERTS, dtype=jnp.int32)
    dest = jnp.sum(jnp.where(is_exp, starts, 0), axis=-1) + ridx[:, TOP_K:2 * TOP_K]
    dest_t = dest.T

    xs = _scatter_rows(h2.reshape(t, d // 2), dest_t.reshape(TOP_K, t // SC_ROWS, SC_ROWS), t * TOP_K)
    ys = _moe_call(sched, xs, w_gate_up[l], b_gate_up[l][:, None, :], w_down[l],
                   b_down[l][:, None, :])
    out = x1.reshape(t, d)
    gates = gates.reshape(t, LANES)
    tp = t // COMBINE_PARTS
    for part in range(COMBINE_PARTS):
        idx = dest_t[:, part * tp:(part + 1) * tp].reshape(-1)
        yg = _gather_rows(ys, idx)
        out = _combine_call(yg, out, gates, mod_lat, final_g[None, :], n, part)
    return out.reshape(bsz, n, d)
```

```python
import functools
import math

import jax
import jax.numpy as jnp
from jax import lax
from jax.experimental import pallas as pl
from jax.experimental.pallas import tpu as pltpu
from jax.experimental.pallas import tpu_sc as plsc

F32 = jnp.float32
BF16 = jnp.bfloat16
HIGHEST = lax.Precision.HIGHEST

N_HEADS = 8
QK_NOPE = 64
QK_ROPE = 32
V_DIM = 64
Q_LORA = 256
KV_LORA = 128
GRID_W = 64
ROPE_BASE = 10000.0
POOL_WINDOWS = (2, 4, 8, 16)
POOL_CH = 128
N_EXPERTS = 32
TOP_K = 4
SWIGLU_LIMIT = 7.0
SWIGLU_ALPHA = 1.702
MOE_BLOCK = 512
MOE_SUB = 128
N_MOD = 6
EPS = 1e-6

LANES = 128
POOL_HALO = 8
VT_ROWS = 80
VMEM_LIMIT = 60 * 1024 * 1024

FRONT_TS = 1024
ATTN_TQ = 512
ATTN_TILES = 8
ATTN_KC = 512
MIX_TS = 512
COMBINE_TT = 256
COMBINE_PARTS = 4
SC_ROWS = 128


def _dot(a, b, **kw):
    return jnp.dot(a, b, preferred_element_type=F32, **kw)


def _dot_nt(a, b):
    return lax.dot_general(a, b, (((1,), (1,)), ((), ())), preferred_element_type=F32)


def _rms(x):
    return x * lax.rsqrt(jnp.mean(x * x, axis=-1, keepdims=True) + EPS)


def _pack_bf16_pairs(x):
    w = x.shape[1] // 2
    return pltpu.pack_elementwise([x[:, :w], x[:, w:]], packed_dtype=BF16)


def _unpack_bf16_pairs(words):
    halves = [pltpu.unpack_elementwise(words, index=h, packed_dtype=BF16, unpacked_dtype=F32)
              for h in range(2)]
    return jnp.concatenate(halves, axis=-1)


def _mod_kernel(c_ref, w_ref, b_ref, o_ref):
    c = c_ref[...]
    a = c / (1.0 + jnp.exp(-c))
    o_ref[...] = _dot(a, w_ref[...], precision=HIGHEST) + b_ref[...]


def _mod_call(cc, w_mod, b_mod):
    d = w_mod.shape[0]
    return pl.pallas_call(
        _mod_kernel,
        grid=(N_MOD,),
        in_specs=[
            pl.BlockSpec((8, d), lambda j: (0, 0)),
            pl.BlockSpec((d, d), lambda j: (0, j)),
            pl.BlockSpec((1, d), lambda j: (0, j)),
        ],
        out_specs=pl.BlockSpec((8, d), lambda j: (0, j)),
        out_shape=jax.ShapeDtypeStruct((8, N_MOD * d), F32),
        compiler_params=pltpu.CompilerParams(
            dimension_semantics=("arbitrary",), vmem_limit_bytes=VMEM_LIMIT),
        name="mod",
    )(cc, w_mod, b_mod)


def _front_kernel(x_ref, mod_ref, g1_ref, win_ref, qg_ref, kvg_ref, wuqt_ref, wk_ref, wvt_ref,
                  cs_ref, sn_ref, cst_ref, snt_ref, *out_refs, is_ctx, scale):
    x = x_ref[0]
    m = mod_ref[0]
    h = _rms(x) * g1_ref[...] * (1.0 + m[1:2]) + m[0:1]
    z = _dot(h.astype(BF16), win_ref[...])
    if is_ctx:
        k_ref, vt_ref = out_refs
        zkv = z
    else:
        qt_ref, k_ref, vt_ref, u_ref = out_refs
        zkv = z[:, Q_LORA:Q_LORA + 2 * LANES]
        u_ref[0] = z[:, Q_LORA + 2 * LANES:]

    ckv = _rms(zkv[:, :KV_LORA]) * kvg_ref[...]
    kk = _dot(ckv.astype(BF16), wk_ref[...])
    kra = zkv[:, LANES:2 * LANES]
    lane = lax.broadcasted_iota(jnp.int32, kra.shape, 1)
    half = QK_ROPE // 2
    partner = jnp.where(lane < QK_NOPE + half, pltpu.roll(kra, LANES - half, axis=1),
                        pltpu.roll(kra, half, axis=1))
    kr = kra * cs_ref[...] + partner * sn_ref[...]
    for hd in range(N_HEADS):
        k_ref[0, hd] = (kk[:, hd * LANES:(hd + 1) * LANES] + kr).astype(BF16)
    vt = _dot(wvt_ref[...], ckv.T.astype(BF16))
    row = lax.broadcasted_iota(jnp.int32, (VT_ROWS, 1), 0)
    ones = jnp.where(row == V_DIM, 1.0, 0.0)
    for hd in range(N_HEADS):
        vt_ref[0, hd] = (vt[hd * VT_ROWS:(hd + 1) * VT_ROWS] + ones).astype(BF16)

    if not is_ctx:
        cq = _rms(z[:, :Q_LORA]) * qg_ref[...]
        qqt = _dot(wuqt_ref[...], cq.T.astype(BF16))
        cst = cst_ref[...]
        snt = snt_ref[...]
        r1, r2, r3 = QK_NOPE, QK_NOPE + QK_ROPE // 2, QK_NOPE + QK_ROPE
        for hd in range(N_HEADS):
            qa = qqt[hd * LANES:(hd + 1) * LANES]
            qb = jnp.concatenate([qa[:r1], qa[r2:r3], qa[r1:r2], qa[r3:]], axis=0)
            qt_ref[0, hd] = ((qa * cst + qb * snt) * scale).astype(BF16)


def _front_call(xs, mod, mod_per_batch, g1, win, qg, kvg, wuqt, wk, wvt, cs, sn, *, is_ctx, ts):
    bsz, n, d = xs.shape
    nt = n // ts
    scale = math.log2(math.e) / math.sqrt(QK_NOPE + QK_ROPE)
    const = lambda b, i: (0, 0)
    mod_map = (lambda b, i: (b, 0, 0)) if mod_per_batch else (lambda b, i: (0, 0, 0))
    in_specs = [
        pl.BlockSpec((1, ts, d), lambda b, i: (b, i, 0)),
        pl.BlockSpec((1, N_MOD, d), mod_map),
        pl.BlockSpec((1, d), const),
        pl.BlockSpec(win.shape, const),
        pl.BlockSpec((1, Q_LORA), const),
        pl.BlockSpec((1, KV_LORA), const),
        pl.BlockSpec(wuqt.shape, const),
        pl.BlockSpec(wk.shape, const),
        pl.BlockSpec(wvt.shape, const),
        pl.BlockSpec((ts, LANES), lambda b, i: (i, 0)),
        pl.BlockSpec((ts, LANES), lambda b, i: (i, 0)),
        pl.BlockSpec((LANES, ts), lambda b, i: (0, i)),
        pl.BlockSpec((LANES, ts), lambda b, i: (0, i)),
    ]
    k_spec = pl.BlockSpec((1, N_HEADS, ts, LANES), lambda b, i: (b, 0, i, 0))
    k_shape = jax.ShapeDtypeStruct((bsz, N_HEADS, n, LANES), BF16)
    qt_spec = pl.BlockSpec((1, N_HEADS, LANES, ts), lambda b, i: (b, 0, 0, i))
    qt_shape = jax.ShapeDtypeStruct((bsz, N_HEADS, LANES, n), BF16)
    vt_spec = pl.BlockSpec((1, N_HEADS, VT_ROWS, ts), lambda b, i: (b, 0, 0, i))
    vt_shape = jax.ShapeDtypeStruct((bsz, N_HEADS, VT_ROWS, n), BF16)
    if is_ctx:
        out_specs = [k_spec, vt_spec]
        out_shape = [k_shape, vt_shape]
    else:
        pool_w = win.shape[1] - Q_LORA - 2 * LANES
        out_specs = [qt_spec, k_spec, vt_spec, pl.BlockSpec((1, ts, pool_w), lambda b, i: (b, i, 0))]
        out_shape = [qt_shape, k_shape, vt_shape, jax.ShapeDtypeStruct((bsz, n, pool_w), F32)]
    return pl.pallas_call(
        functools.partial(_front_kernel, is_ctx=is_ctx, scale=scale),
        grid=(bsz, nt),
        in_specs=in_specs,
        out_specs=out_specs,
        out_shape=out_shape,
        compiler_params=pltpu.CompilerParams(
            dimension_semantics=("arbitrary", "arbitrary"), vmem_limit_bytes=VMEM_LIMIT),
        name="front_ctx" if is_ctx else "front",
    )(xs, mod, g1, win, qg, kvg, wuqt, wk, wvt, cs, sn, cs.T, sn.T)


def _attn_kernel(qt_ref, k_ref, vt_ref, kc_ref, vct_ref, o_ref, s_ref, mlc_ref, mxb_ref, oe_ref):
    n_ctx = kc_ref.shape[2]
    n_lat = k_ref.shape[2]
    chunks = [(None, 0, n_ctx)] + [(c * ATTN_KC, n_ctx + c * ATTN_KC, ATTN_KC)
                                   for c in range(n_lat // ATTN_KC)]
    sub = 8

    def q_cols(tile):
        if isinstance(tile, int):
            return pl.ds(tile * ATTN_TQ, ATTN_TQ)
        return pl.ds(pl.multiple_of(tile * ATTN_TQ, ATTN_TQ), ATTN_TQ)

    def score_chunk(tile, hd, buf, ci):
        off, soff, w = chunks[ci]
        keys = kc_ref[0, hd] if off is None else k_ref[0, hd, off:off + w, :]
        s = _dot(keys, qt_ref[0, hd, :, q_cols(tile)])
        s_ref[buf, soff:soff + w, :] = s
        mx = s[0:sub]
        for r in range(1, w // sub):
            mx = jnp.maximum(mx, s[r * sub:(r + 1) * sub])
        mlc_ref[buf, ci] = mx

    def row_max(buf):
        mx = mlc_ref[buf, 0]
        for ci in range(1, len(chunks)):
            mx = jnp.maximum(mx, mlc_ref[buf, ci])
        mxb_ref[...] = jnp.broadcast_to(jnp.max(mx, axis=0, keepdims=True), mx.shape)

    def weight_chunk(hd, buf, ci):
        off, soff, w = chunks[ci]
        p = jnp.exp2(s_ref[buf, soff:soff + w, :] - mxb_ref[0:1, :]).astype(BF16)
        vt = vct_ref[0, hd] if off is None else vt_ref[0, hd, :, off:off + w]
        return _dot(vt, p)

    def stage(hw, bw, scoring, bs):
        row_max(bw)
        acc = None
        for ci in range(len(chunks)):
            if scoring is not None:
                score_chunk(*scoring, bs, ci)
            part = weight_chunk(hw, bw, ci)
            acc = part if acc is None else acc + part
        return acc[0:V_DIM] / acc[V_DIM:V_DIM + 1]

    def write_pair(tile, j, ot_odd):
        pair_t = jnp.concatenate([oe_ref[...], ot_odd], axis=0)
        o_ref[0, j, q_cols(tile), :] = pair_t.T.astype(BF16)

    pairs_per_tile = N_HEADS // 2
    n_pairs = (qt_ref.shape[3] // ATTN_TQ) * pairs_per_tile

    def split(p):
        return p // pairs_per_tile, p % pairs_per_tile

    for ci in range(len(chunks)):
        score_chunk(0, 0, 0, ci)

    def head_pair(p, carry):
        tile, j = split(p)
        nxt_tile, nxt_j = split(p + 1)
        oe_ref[...] = stage(2 * j, 0, (tile, 2 * j + 1), 1)
        write_pair(tile, j, stage(2 * j + 1, 1, (nxt_tile, 2 * nxt_j), 0))
        return carry

    lax.fori_loop(0, n_pairs - 1, head_pair, 0)
    tile, j = split(n_pairs - 1)
    oe_ref[...] = stage(2 * j, 0, (tile, 2 * j + 1), 1)
    write_pair(tile, j, stage(2 * j + 1, 1, None, None))


def _attn_call(qt, k, vt, kc, vct):
    bsz, _, _, n = qt.shape
    n_ctx = kc.shape[2]
    tq = ATTN_TQ
    n_chunks = 1 + n // ATTN_KC
    per_batch = lambda b, i: (b, 0, 0, 0)
    resident = dict(pipeline_mode=pl.Buffered(1))
    rows_per_step = ATTN_TILES * tq
    return pl.pallas_call(
        _attn_kernel,
        grid=(bsz, n // rows_per_step),
        in_specs=[
            pl.BlockSpec((1, N_HEADS, LANES, rows_per_step), lambda b, i: (b, 0, 0, i)),
            pl.BlockSpec((1, N_HEADS, n, LANES), per_batch, **resident),
            pl.BlockSpec((1, N_HEADS, VT_ROWS, n), per_batch, **resident),
            pl.BlockSpec((1, N_HEADS, n_ctx, LANES), per_batch, **resident),
            pl.BlockSpec((1, N_HEADS, VT_ROWS, n_ctx), per_batch, **resident),
        ],
        out_specs=pl.BlockSpec((1, N_HEADS // 2, rows_per_step, LANES), lambda b, i: (b, 0, i, 0)),
        out_shape=jax.ShapeDtypeStruct((bsz, N_HEADS // 2, n, LANES), BF16),
        scratch_shapes=[
            pltpu.VMEM((2, n_ctx + n, tq), F32),
            pltpu.VMEM((2, n_chunks, 8, tq), F32),
            pltpu.VMEM((8, tq), F32),
            pltpu.VMEM((V_DIM, tq), F32),
        ],
        compiler_params=pltpu.CompilerParams(
            dimension_semantics=("arbitrary", "arbitrary"), vmem_limit_bytes=VMEM_LIMIT),
        name="attn",
    )(qt, k, vt, kc, vct)


def _mix_kernel(att_ref, u_ref, up_ref, un_ref, x_ref, mod_ref, wpool_ref, pscale_ref, wout_ref,
                g2_ref, rw_ref, rb_ref,
                x1_ref, h2_ref, gate_ref, ridx_ref, cnt_ref, ue_ref, base_ref, *, n_seq):
    b = pl.program_id(0)
    i = pl.program_id(1)
    n_tiles = pl.num_programs(1)
    ts = u_ref.shape[1]

    @pl.when(jnp.logical_and(b == 0, i == 0))
    def _():
        base_ref[...] = jnp.zeros_like(base_ref)

    u = u_ref[0]
    ue_ref[0:POOL_HALO] = jnp.where(i > 0, up_ref[0], 0.0)
    ue_ref[POOL_HALO:POOL_HALO + ts] = u
    ue_ref[POOL_HALO + ts:2 * POOL_HALO + ts] = jnp.where(i < n_tiles - 1, un_ref[0], 0.0)
    t = i * ts + lax.broadcasted_iota(jnp.int32, (ts, 1), 0)
    ys = []
    for g, w in enumerate(POOL_WINDOWS):
        half = w // 2
        lanes = slice(g * POOL_CH, (g + 1) * POOL_CH)
        ws = ue_ref[POOL_HALO - half:POOL_HALO - half + ts, lanes]
        for jj in range(-half + 1, half):
            ws = ws + ue_ref[POOL_HALO + jj:POOL_HALO + jj + ts, lanes]
        count = (jnp.minimum(t + half, n_seq) - jnp.maximum(t - half, 0)).astype(F32)
        mixed = (ws / count - u[:, lanes]).astype(BF16)
        ys.append(_dot(mixed, wpool_ref[g]))
    pool = jnp.concatenate(ys, axis=-1) * pscale_ref[...]

    cat = jnp.concatenate([att_ref[0, j] for j in range(N_HEADS // 2)] + [pool.astype(BF16)], axis=-1)
    m = mod_ref[0]
    x1 = x_ref[0] + m[2:3] * _dot(cat, wout_ref[...])
    x1_ref[0] = x1
    h2 = _rms(x1) * g2_ref[...] * (1.0 + m[4:5]) + m[3:4]
    h2_ref[0] = _pack_bf16_pairs(h2)

    h_hi = h2.astype(BF16)
    h_lo = (h2 - h_hi.astype(F32)).astype(BF16)
    hi_part = _dot(h_hi, rw_ref[...])
    logits = (hi_part[:, :LANES] + hi_part[:, LANES:] + _dot(h_lo, rw_ref[:, :LANES])) + rb_ref[...]
    lane = lax.broadcasted_iota(jnp.int32, logits.shape, 1).astype(F32)
    vals, idxs = [], []
    for _k in range(TOP_K):
        mv = jnp.max(logits, axis=-1, keepdims=True)
        ix = jnp.min(jnp.where(logits == mv, lane, float(LANES)), axis=-1, keepdims=True)
        vals.append(mv)
        idxs.append(ix)
        logits = jnp.where(lane == ix, -jnp.inf, logits)
    es = [jnp.exp(v - vals[0]) for v in vals]
    den = es[0] + es[1] + es[2] + es[3]

    onehot = jnp.zeros(lane.shape, F32)
    for ix in idxs:
        onehot = onehot + jnp.where(lane == ix, 1.0, 0.0)
    row = lax.broadcasted_iota(jnp.int32, (ts, ts), 0)
    col = lax.broadcasted_iota(jnp.int32, (ts, ts), 1)
    tri = jnp.where(col < row, 1.0, 0.0).astype(BF16)
    before = _dot(tri, onehot.astype(BF16)) + base_ref[0:1, :]
    base_new = base_ref[0:1, :] + jnp.sum(onehot, axis=0, keepdims=True)
    base_ref[...] = jnp.broadcast_to(base_new, base_ref.shape)
    cnt_ref[...] = jnp.broadcast_to(base_new, cnt_ref.shape)

    gate_out = jnp.zeros(lane.shape, F32)
    ridx_out = jnp.zeros(lane.shape, F32)
    for kk in range(TOP_K):
        rank = jnp.sum(jnp.where(lane == idxs[kk], before, 0.0), axis=-1, keepdims=True)
        gate_out = jnp.where(lane == float(kk), es[kk] / den, gate_out)
        ridx_out = jnp.where(lane == float(kk), idxs[kk], ridx_out)
        ridx_out = jnp.where(lane == float(TOP_K + kk), rank, ridx_out)
    gate_ref[0] = gate_out
    ridx_ref[0] = ridx_out.astype(jnp.int32)


def _mix_call(att, u, x, mod, wpool, pscale, wout, g2, rw, rb):
    bsz, n, d = x.shape
    ts = MIX_TS
    nt = n // ts
    pool_w = u.shape[2]
    hb = ts // POOL_HALO
    n_halo_blocks = n // POOL_HALO
    const2 = lambda b, i: (0, 0)
    tok = lambda b, i: (b, i, 0)
    return pl.pallas_call(
        functools.partial(_mix_kernel, n_seq=n),
        grid=(bsz, nt),
        in_specs=[
            pl.BlockSpec((1, N_HEADS // 2, ts, LANES), lambda b, i: (b, 0, i, 0)),
            pl.BlockSpec((1, ts, pool_w), tok),
            pl.BlockSpec((1, POOL_HALO, pool_w), lambda b, i: (b, jnp.maximum(i * hb - 1, 0), 0)),
            pl.BlockSpec((1, POOL_HALO, pool_w),
                         lambda b, i: (b, jnp.minimum((i + 1) * hb, n_halo_blocks - 1), 0)),
            pl.BlockSpec((1, ts, d), tok),
            pl.BlockSpec((1, N_MOD, d), lambda b, i: (b, 0, 0)),
            pl.BlockSpec(wpool.shape, lambda b, i: (0, 0, 0)),
            pl.BlockSpec((1, pool_w), const2),
            pl.BlockSpec(wout.shape, const2),
            pl.BlockSpec((1, d), const2),
            pl.BlockSpec(rw.shape, const2),
            pl.BlockSpec((1, LANES), const2),
        ],
        out_specs=[
            pl.BlockSpec((1, ts, d), tok),
            pl.BlockSpec((1, ts, d // 2), tok),
            pl.BlockSpec((1, ts, LANES), tok),
            pl.BlockSpec((1, ts, LANES), tok),
            pl.BlockSpec((8, LANES), const2),
        ],
        out_shape=[
            jax.ShapeDtypeStruct((bsz, n, d), F32),
            jax.ShapeDtypeStruct((bsz, n, d // 2), jnp.uint32),
            jax.ShapeDtypeStruct((bsz, n, LANES), F32),
            jax.ShapeDtypeStruct((bsz, n, LANES), jnp.int32),
            jax.ShapeDtypeStruct((8, LANES), F32),
        ],
        scratch_shapes=[
            pltpu.VMEM((ts + 2 * POOL_HALO, pool_w), F32),
            pltpu.VMEM((8, LANES), F32),
        ],
        compiler_params=pltpu.CompilerParams(
            dimension_semantics=("arbitrary", "arbitrary"), vmem_limit_bytes=VMEM_LIMIT),
        name="mix",
    )(att, u, u, u, x, mod, wpool, pscale, wout, g2, rw, rb)


def _sc_mesh():
    return plsc.VectorSubcoreMesh(core_axis_name="core", subcore_axis_name="subcore")


def _sc_worker_id():
    info = plsc.get_sparse_core_info()
    return lax.axis_index("subcore") * info.num_cores + lax.axis_index("core")


def _sc_num_workers():
    info = plsc.get_sparse_core_info()
    return info.num_cores * info.num_subcores


def _scatter_rows(x, dest3, n_out):
    t, d = x.shape
    top_k, n_chunks, rows = dest3.shape
    per_worker = n_chunks // _sc_num_workers()

    @functools.partial(
        pl.kernel,
        out_type=jax.ShapeDtypeStruct((n_out, d), x.dtype),
        mesh=_sc_mesh(),
        scratch_types=[pltpu.VMEM((top_k, rows), jnp.int32), pltpu.VMEM((rows, d), x.dtype)],
        name="sc_scatter",
    )
    def scatter(x_hbm, i_hbm, o_hbm, idx_v, rows_v):
        wid = _sc_worker_id()

        @pl.loop(0, per_worker)
        def _(c):
            chunk = wid * per_worker + c
            for kk in range(top_k):
                pltpu.sync_copy(i_hbm.at[kk, chunk], idx_v.at[kk])
            pltpu.sync_copy(x_hbm.at[pl.ds(pl.multiple_of(chunk * rows, 8), rows)], rows_v)
            for kk in range(top_k):
                pltpu.sync_copy(rows_v, o_hbm.at[idx_v.at[kk]])

    return scatter(x, dest3)


def _gather_rows(y, idx):
    n = idx.shape[0]
    d = y.shape[1]
    rows = SC_ROWS
    per_worker = n // _sc_num_workers()
    n_chunks = per_worker // rows

    @functools.partial(
        pl.kernel,
        out_type=jax.ShapeDtypeStruct((n, d), y.dtype),
        mesh=_sc_mesh(),
        scratch_types=[pltpu.VMEM((rows,), jnp.int32), pltpu.VMEM((rows, d), y.dtype)],
        name="sc_gather",
    )
    def gather(y_hbm, i_hbm, o_hbm, idx_v, rows_v):
        base = _sc_worker_id() * per_worker

        @pl.loop(0, n_chunks)
        def _(c):
            off = pl.multiple_of(base + c * rows, 8)
            pltpu.sync_copy(i_hbm.at[pl.ds(off, rows)], idx_v)
            pltpu.sync_copy(y_hbm.at[idx_v], rows_v)
            pltpu.sync_copy(rows_v, o_hbm.at[pl.ds(off, rows)])

    return gather(y, idx)


def _moe_kernel(blk_ref, exp_ref, lo_ref, hi_ref, cast_ref, init_ref, next_ref, xs_ref, wgu_hbm,
                bgu_ref, wd_hbm, bd_ref, ys_ref, wgu_f32, wd_f32, wgu_bf, wd_bf, sems):
    i = pl.program_id(0)

    def weight_copies(e):
        return (pltpu.make_async_copy(wgu_hbm.at[e], wgu_f32, sems.at[0]),
                pltpu.make_async_copy(wd_hbm.at[e], wd_f32, sems.at[1]))

    @pl.when(i == 0)
    def _():
        for cp in weight_copies(exp_ref[0]):
            cp.start()

    @pl.when(cast_ref[i] == 1)
    def _():
        for cp in weight_copies(exp_ref[i]):
            cp.wait()
        wgu_bf[...] = wgu_f32[...].astype(BF16)
        wd_bf[...] = wd_f32[...].astype(BF16)

        @pl.when(next_ref[i] >= 0)
        def _():
            for cp in weight_copies(next_ref[i]):
                cp.start()

    @pl.when(init_ref[i] == 1)
    def _():
        ys_ref[...] = jnp.zeros_like(ys_ref)

    lo = lo_ref[i]
    hi = hi_ref[i]

    def expert_rows(r0, n_rows):
        de = wd_bf.shape[0]
        x = _unpack_bf16_pairs(xs_ref[pl.ds(r0, n_rows), :]).astype(BF16)
        gu = _dot(x, wgu_bf[...]) + bgu_ref[0]
        g = jnp.minimum(gu[:, :de], SWIGLU_LIMIT)
        lin = jnp.clip(gu[:, de:], -SWIGLU_LIMIT, SWIGLU_LIMIT)
        act = g / (1.0 + jnp.exp(-SWIGLU_ALPHA * g)) * (lin + 1.0)
        y = _dot(act.astype(BF16), wd_bf[...]) + bd_ref[0]
        row = r0 + lax.broadcasted_iota(jnp.int32, (n_rows, 1), 0)
        mine = jnp.logical_and(row >= lo, row < hi)
        ys_ref[pl.ds(r0, n_rows), :] = jnp.where(mine, _pack_bf16_pairs(y), ys_ref[pl.ds(r0, n_rows), :])

    first = lo // MOE_SUB
    pieces = jnp.where(hi > lo, (hi + MOE_SUB - 1) // MOE_SUB - first, 0)
    for cnt in range(1, MOE_BLOCK // MOE_SUB + 1):
        @pl.when(pieces == cnt)
        def _(cnt=cnt):
            r0 = 0 if cnt * MOE_SUB == MOE_BLOCK else pl.multiple_of(first * MOE_SUB, MOE_SUB)
            expert_rows(r0, cnt * MOE_SUB)


def _moe_call(sched, xs, wgu, bgu, wd, bd):
    n_items = sched[0].shape[0]
    _, d, de2 = wgu.shape
    de = de2 // 2
    wmap = lambda i, blk, exp, lo, hi, cast, init, nxt: (exp[i], 0, 0)
    xmap = lambda i, blk, exp, lo, hi, cast, init, nxt: (blk[i], 0)
    return pl.pallas_call(
        _moe_kernel,
        grid_spec=pltpu.PrefetchScalarGridSpec(
            num_scalar_prefetch=7,
            grid=(n_items,),
            in_specs=[
                pl.BlockSpec((MOE_BLOCK, d // 2), xmap),
                pl.BlockSpec(memory_space=pl.ANY),
                pl.BlockSpec((1, 1, de2), wmap),
                pl.BlockSpec(memory_space=pl.ANY),
                pl.BlockSpec((1, 1, d), wmap),
            ],
            out_specs=pl.BlockSpec((MOE_BLOCK, d // 2), xmap),
            scratch_shapes=[
                pltpu.VMEM((d, de2), F32), pltpu.VMEM((de, d), F32),
                pltpu.VMEM((d, de2), BF16), pltpu.VMEM((de, d), BF16),
                pltpu.SemaphoreType.DMA((2,)),
            ],
        ),
        out_shape=jax.ShapeDtypeStruct(xs.shape, xs.dtype),
        compiler_params=pltpu.CompilerParams(
            dimension_semantics=("arbitrary",), vmem_limit_bytes=VMEM_LIMIT),
        name="moe",
    )(*sched, xs, wgu, bgu, wd, bd)


def _moe_schedule(counts, n_rows):
    n_blocks = n_rows // MOE_BLOCK
    n_items = n_blocks + N_EXPERTS
    ends = jnp.cumsum(counts)
    starts = ends - counts
    first_blk = starts // MOE_BLOCK
    last_blk = (ends - 1) // MOE_BLOCK
    items_per = jnp.where(counts > 0, last_blk - first_blk + 1, 0)
    item_ends = jnp.cumsum(items_per)
    item_starts = item_ends - items_per
    total = item_ends[-1]
    it = jnp.arange(n_items, dtype=jnp.int32)
    live = it < total
    itc = jnp.minimum(it, total - 1)
    exp = jnp.sum((item_ends[None, :] <= itc[:, None]).astype(jnp.int32), axis=1)
    is_exp = exp[:, None] == jnp.arange(N_EXPERTS, dtype=jnp.int32)[None, :]
    pick = lambda table: jnp.sum(jnp.where(is_exp, table[None, :], 0), axis=1)
    blk = pick(first_blk) + itc - pick(item_starts)
    lo = jnp.clip(pick(starts) - blk * MOE_BLOCK, 0, MOE_BLOCK)
    hi = jnp.clip(pick(ends) - blk * MOE_BLOCK, 0, MOE_BLOCK)
    hi = jnp.where(live, hi, lo)
    prev_exp = jnp.concatenate([jnp.full((1,), -1, jnp.int32), exp[:-1]])
    prev_blk = jnp.concatenate([jnp.full((1,), -1, jnp.int32), blk[:-1]])
    cast = jnp.logical_and(live, exp != prev_exp)
    init = jnp.logical_and(live, blk != prev_blk)
    ar = jnp.arange(N_EXPERTS, dtype=jnp.int32)
    later = jnp.logical_and(counts[None, :] > 0, ar[None, :] > ar[:, None])
    next_exp = jnp.min(jnp.where(later, ar[None, :], N_EXPERTS), axis=1)
    next_exp = jnp.where(next_exp == N_EXPERTS, -1, next_exp)
    nxt = pick(next_exp)
    as_i32 = lambda a: a.astype(jnp.int32)
    return tuple(as_i32(a) for a in (blk, exp, lo, hi, cast, init, nxt)), starts


def _combine_kernel(*refs):
    yg_refs = refs[:TOP_K]
    x1_ref, gate_ref, mod_ref, fg_ref, o_ref = refs[TOP_K:]
    gates = gate_ref[...]
    y = gates[:, 0:1] * _unpack_bf16_pairs(yg_refs[0][...])
    for kk in range(1, TOP_K):
        y = y + gates[:, kk:kk + 1] * _unpack_bf16_pairs(yg_refs[kk][...])
    m = mod_ref[0]
    x2 = x1_ref[...] + m[5:6] * y
    o_ref[...] = _rms(x2) * fg_ref[...]


def _combine_call(yg, x1, gates, mod, fg, n_seq, part):
    t, d = x1.shape
    tt = COMBINE_TT
    tiles_per_seq = n_seq // tt
    n_tiles = yg.shape[0] // (TOP_K * tt)
    first = part * n_tiles
    tok = lambda i: (first + i, 0)
    slot_specs = [pl.BlockSpec((tt, d // 2), functools.partial(lambda kk, i: (kk * n_tiles + i, 0), kk))
                  for kk in range(TOP_K)]
    return pl.pallas_call(
        _combine_kernel,
        grid=(n_tiles,),
        in_specs=slot_specs + [
            pl.BlockSpec((tt, d), tok),
            pl.BlockSpec((tt, LANES), tok),
            pl.BlockSpec((1, N_MOD, d), lambda i: ((first + i) // tiles_per_seq, 0, 0)),
            pl.BlockSpec((1, d), lambda i: (0, 0)),
        ],
        out_specs=pl.BlockSpec((tt, d), tok),
        out_shape=jax.ShapeDtypeStruct((t, d), F32),
        input_output_aliases={TOP_K: 0},
        compiler_params=pltpu.CompilerParams(
            dimension_semantics=("arbitrary",), vmem_limit_bytes=VMEM_LIMIT),
        name="combine",
    )(*([yg] * TOP_K), x1, gates, mod, fg)


def _pad_cols(a, width):
    return jnp.pad(a, ((0, 0), (0, width - a.shape[1])))


def _rope_slab(w_rope):
    return jnp.pad(w_rope, ((0, 0), (QK_NOPE, LANES - QK_NOPE - QK_ROPE)))


def _prep_w_in(w_in):
    kr0 = Q_LORA + KV_LORA
    w_kr = w_in[:, kr0:kr0 + QK_ROPE]
    return jnp.concatenate(
        [w_in[:, :kr0], _rope_slab(w_kr), w_in[:, kr0 + QK_ROPE:]], axis=1).astype(BF16)


def _prep_w_uq(w_uq):
    per = QK_NOPE + QK_ROPE
    rows = w_uq.shape[0]
    w = w_uq.reshape(rows, N_HEADS, per)
    slab = jnp.pad(w, ((0, 0), (0, 0), (0, LANES - per)))
    return slab.reshape(rows, -1).T.astype(BF16)


def _prep_w_ukv(w_ukv):
    per = QK_NOPE + V_DIM
    rows = w_ukv.shape[0]
    w = w_ukv.reshape(rows, N_HEADS, per)
    wk = jnp.pad(w[:, :, :QK_NOPE], ((0, 0), (0, 0), (0, LANES - QK_NOPE))).reshape(rows, -1)
    wvt = jnp.pad(jnp.transpose(w[:, :, QK_NOPE:], (1, 2, 0)), ((0, 0), (0, VT_ROWS - V_DIM), (0, 0)))
    return wk.astype(BF16), wvt.reshape(N_HEADS * VT_ROWS, rows).astype(BF16)


def _rope_tables(n_lat):
    rows = n_lat // GRID_W
    nf = QK_ROPE // 4
    row = jnp.repeat(jnp.arange(rows, dtype=F32), GRID_W)
    col = jnp.tile(jnp.arange(GRID_W, dtype=F32), rows)
    freqs = ROPE_BASE ** (-jnp.arange(nf, dtype=F32) / nf)
    ang = jnp.concatenate([row[:, None] * freqs, col[:, None] * freqs], axis=-1)
    cos, sin = jnp.cos(ang), jnp.sin(ang)
    ones = jnp.ones((n_lat, QK_NOPE), F32)
    zeros = jnp.zeros((n_lat, QK_NOPE), F32)
    cs = _pad_cols(jnp.concatenate([ones, cos, cos], axis=1), LANES)
    sn = _pad_cols(jnp.concatenate([zeros, -sin, sin], axis=1), LANES)
    return cs, sn


def kernel(x, c, ctx, c_ctx, w_mod, b_mod, norm1_g, w_in, q_norm_g, kv_norm_g, w_uq, w_ukv, w_pool,
           pool_scale, w_out, norm2_g, router_w, router_b, w_gate_up, b_gate_up, w_down, b_down,
           final_g):
    bsz, n, d = x.shape
    n_ctx = ctx.shape[1]
    t = bsz * n
    l = 0

    cc = jnp.concatenate([c, c_ctx[None, :], jnp.zeros((8 - bsz - 1, d), F32)], axis=0)
    mod = _mod_call(cc, w_mod[l], b_mod[l][None, :]).reshape(8, N_MOD, d)
    mod_lat, mod_ctx = mod[:bsz], mod[bsz:bsz + 1]

    win = _prep_w_in(w_in[l])
    wuqt = _prep_w_uq(w_uq[l])
    wk, wvt = _prep_w_ukv(w_ukv[l])
    cs, sn = _rope_tables(n)
    cs_ctx = jnp.broadcast_to((jnp.arange(LANES) < QK_NOPE + QK_ROPE).astype(F32), (n_ctx, LANES))
    sn_ctx = jnp.zeros((n_ctx, LANES), F32)
    g1 = norm1_g[l][None, :]
    qg = q_norm_g[l][None, :]
    kvg = kv_norm_g[l][None, :]

    qt, k, vt, u = _front_call(x, mod_lat, True, g1, win, qg, kvg, wuqt, wk, wvt, cs, sn,
                               is_ctx=False, ts=FRONT_TS)
    kc, vct = _front_call(ctx, mod_ctx, False, g1, win[:, Q_LORA:Q_LORA + 2 * LANES], qg, kvg, wuqt,
                          wk, wvt, cs_ctx, sn_ctx, is_ctx=True, ts=n_ctx)
    att = _attn_call(qt, k, vt, kc, vct)

    rw = _pad_cols(router_w[l], LANES)
    rw_hi = rw.astype(BF16)
    rw = jnp.concatenate([rw_hi, (rw - rw_hi.astype(F32)).astype(BF16)], axis=1)
    rb = jnp.concatenate([router_b[l], jnp.full((LANES - N_EXPERTS,), -jnp.inf, F32)])[None, :]
    x1, h2, gates, ridx, cnt = _mix_call(
        att, u, x, mod_lat, w_pool[l].astype(BF16), pool_scale[l][None, :], w_out[l].astype(BF16),
        norm2_g[l][None, :], rw, rb)

    counts = cnt[0, :N_EXPERTS].astype(jnp.int32)
    sched, starts = _moe_schedule(counts, t * TOP_K)
    ridx = ridx.reshape(t, LANES)
    is_exp = ridx[:, :TOP_K, None] == jnp.arange(N_EXPERTS, dtype=jnp.int32)
    dest = jnp.sum(jnp.where(is_exp, starts, 0), axis=-1) + ridx[:, TOP_K:2 * TOP_K]
    dest_t = dest.T

    xs = _scatter_rows(h2.reshape(t, d // 2), dest_t.reshape(TOP_K, t // SC_ROWS, SC_ROWS), t * TOP_K)
    ys = _moe_call(sched, xs, w_gate_up[l], b_gate_up[l][:, None, :], w_down[l],
                   b_down[l][:, None, :])
    out = x1.reshape(t, d)
    gates = gates.reshape(t, LANES)
    tp = t // COMBINE_PARTS
    for part in range(COMBINE_PARTS):
        idx = dest_t[:, part * tp:(part + 1) * tp].reshape(-1)
        yg = _gather_rows(ys, idx)
        out = _combine_call(yg, out, gates, mod_lat, final_g[None, :], n, part)
    return out.reshape(bsz, n, d)
```

```python
import functools
import math

import jax
import jax.numpy as jnp
from jax import lax
from jax.experimental import pallas as pl
from jax.experimental.pallas import tpu as pltpu
from jax.experimental.pallas import tpu_sc as plsc

F32 = jnp.float32
BF16 = jnp.bfloat16
HIGHEST = lax.Precision.HIGHEST

N_HEADS = 8
QK_NOPE = 64
QK_ROPE = 32
V_DIM = 64
Q_LORA = 256
KV_LORA = 128
GRID_W = 64
ROPE_BASE = 10000.0
POOL_WINDOWS = (2, 4, 8, 16)
POOL_CH = 128
N_EXPERTS = 32
TOP_K = 4
SWIGLU_LIMIT = 7.0
SWIGLU_ALPHA = 1.702
MOE_BLOCK = 512
MOE_SUB = 128
N_MOD = 6
EPS = 1e-6

LANES = 128
POOL_HALO = 8
VT_ROWS = 80
VMEM_LIMIT = 56 * 1024 * 1024

FRONT_TS = 1024
ATTN_TQ = 512
ATTN_TILES = 4
ATTN_KC = 512
MIX_TS = 512
COMBINE_TT = 256
COMBINE_PARTS = 4
SC_ROWS = 128


def _dot(a, b, **kw):
    return jnp.dot(a, b, preferred_element_type=F32, **kw)


def _dot_nt(a, b):
    return lax.dot_general(a, b, (((1,), (1,)), ((), ())), preferred_element_type=F32)


def _rms(x):
    return x * lax.rsqrt(jnp.mean(x * x, axis=-1, keepdims=True) + EPS)


def _pack_bf16_pairs(x):
    w = x.shape[1] // 2
    return pltpu.pack_elementwise([x[:, :w], x[:, w:]], packed_dtype=BF16)


def _unpack_bf16_pairs(words):
    halves = [pltpu.unpack_elementwise(words, index=h, packed_dtype=BF16, unpacked_dtype=F32)
              for h in range(2)]
    return jnp.concatenate(halves, axis=-1)


def _mod_kernel(c_ref, w_ref, b_ref, o_ref):
    c = c_ref[...]
    a = c / (1.0 + jnp.exp(-c))
    o_ref[...] = _dot(a, w_ref[...], precision=HIGHEST) + b_ref[...]


def _mod_call(cc, w_mod, b_mod):
    d = w_mod.shape[0]
    return pl.pallas_call(
        _mod_kernel,
        grid=(N_MOD,),
        in_specs=[
            pl.BlockSpec((8, d), lambda j: (0, 0)),
            pl.BlockSpec((d, d), lambda j: (0, j)),
            pl.BlockSpec((1, d), lambda j: (0, j)),
        ],
        out_specs=pl.BlockSpec((8, d), lambda j: (0, j)),
        out_shape=jax.ShapeDtypeStruct((8, N_MOD * d), F32),
        compiler_params=pltpu.CompilerParams(
            dimension_semantics=("arbitrary",), vmem_limit_bytes=VMEM_LIMIT),
        name="mod",
    )(cc, w_mod, b_mod)


def _front_kernel(x_ref, mod_ref, g1_ref, win_ref, qg_ref, kvg_ref, wuqt_ref, wk_ref, wvt_ref,
                  cs_ref, sn_ref, cst_ref, snt_ref, *out_refs, is_ctx, scale):
    x = x_ref[0]
    m = mod_ref[0]
    h = _rms(x) * g1_ref[...] * (1.0 + m[1:2]) + m[0:1]
    z = _dot(h.astype(BF16), win_ref[...])
    if is_ctx:
        k_ref, vt_ref = out_refs
        zkv = z
    else:
        qt_ref, k_ref, vt_ref, u_ref = out_refs
        zkv = z[:, Q_LORA:Q_LORA + 2 * LANES]
        u_ref[0] = z[:, Q_LORA + 2 * LANES:]

    ckv = _rms(zkv[:, :KV_LORA]) * kvg_ref[...]
    kk = _dot(ckv.astype(BF16), wk_ref[...])
    kra = zkv[:, LANES:2 * LANES]
    lane = lax.broadcasted_iota(jnp.int32, kra.shape, 1)
    half = QK_ROPE // 2
    partner = jnp.where(lane < QK_NOPE + half, pltpu.roll(kra, LANES - half, axis=1),
                        pltpu.roll(kra, half, axis=1))
    kr = kra * cs_ref[...] + partner * sn_ref[...]
    for hd in range(N_HEADS):
        k_ref[0, hd] = (kk[:, hd * LANES:(hd + 1) * LANES] + kr).astype(BF16)
    vt = _dot(wvt_ref[...], ckv.T.astype(BF16))
    row = lax.broadcasted_iota(jnp.int32, (VT_ROWS, 1), 0)
    ones = jnp.where(row == V_DIM, 1.0, 0.0)
    for hd in range(N_HEADS):
        vt_ref[0, hd] = (vt[hd * VT_ROWS:(hd + 1) * VT_ROWS] + ones).astype(BF16)

    if not is_ctx:
        cq = _rms(z[:, :Q_LORA]) * qg_ref[...]
        qqt = _dot(wuqt_ref[...], cq.T.astype(BF16))
        cst = cst_ref[...]
        snt = snt_ref[...]
        r1, r2, r3 = QK_NOPE, QK_NOPE + QK_ROPE // 2, QK_NOPE + QK_ROPE
        for hd in range(N_HEADS):
            qa = qqt[hd * LANES:(hd + 1) * LANES]
            qb = jnp.concatenate([qa[:r1], qa[r2:r3], qa[r1:r2], qa[r3:]], axis=0)
            qt_ref[0, hd] = ((qa * cst + qb * snt) * scale).astype(BF16)


def _front_call(xs, mod, mod_per_batch, g1, win, qg, kvg, wuqt, wk, wvt, cs, sn, *, is_ctx, ts):
    bsz, n, d = xs.shape
    nt = n // ts
    scale = math.log2(math.e) / math.sqrt(QK_NOPE + QK_ROPE)
    const = lambda b, i: (0, 0)
    mod_map = (lambda b, i: (b, 0, 0)) if mod_per_batch else (lambda b, i: (0, 0, 0))
    in_specs = [
        pl.BlockSpec((1, ts, d), lambda b, i: (b, i, 0)),
        pl.BlockSpec((1, N_MOD, d), mod_map),
        pl.BlockSpec((1, d), const),
        pl.BlockSpec(win.shape, const),
        pl.BlockSpec((1, Q_LORA), const),
        pl.BlockSpec((1, KV_LORA), const),
        pl.BlockSpec(wuqt.shape, const),
        pl.BlockSpec(wk.shape, const),
        pl.BlockSpec(wvt.shape, const),
        pl.BlockSpec((ts, LANES), lambda b, i: (i, 0)),
        pl.BlockSpec((ts, LANES), lambda b, i: (i, 0)),
        pl.BlockSpec((LANES, ts), lambda b, i: (0, i)),
        pl.BlockSpec((LANES, ts), lambda b, i: (0, i)),
    ]
    k_spec = pl.BlockSpec((1, N_HEADS, ts, LANES), lambda b, i: (b, 0, i, 0))
    k_shape = jax.ShapeDtypeStruct((bsz, N_HEADS, n, LANES), BF16)
    qt_spec = pl.BlockSpec((1, N_HEADS, LANES, ts), lambda b, i: (b, 0, 0, i))
    qt_shape = jax.ShapeDtypeStruct((bsz, N_HEADS, LANES, n), BF16)
    vt_spec = pl.BlockSpec((1, N_HEADS, VT_ROWS, ts), lambda b, i: (b, 0, 0, i))
    vt_shape = jax.ShapeDtypeStruct((bsz, N_HEADS, VT_ROWS, n), BF16)
    if is_ctx:
        out_specs = [k_spec, vt_spec]
        out_shape = [k_shape, vt_shape]
    else:
        pool_w = win.shape[1] - Q_LORA - 2 * LANES
        out_specs = [qt_spec, k_spec, vt_spec, pl.BlockSpec((1, ts, pool_w), lambda b, i: (b, i, 0))]
        out_shape = [qt_shape, k_shape, vt_shape, jax.ShapeDtypeStruct((bsz, n, pool_w), F32)]
    return pl.pallas_call(
        functools.partial(_front_kernel, is_ctx=is_ctx, scale=scale),
        grid=(bsz, nt),
        in_specs=in_specs,
        out_specs=out_specs,
        out_shape=out_shape,
        compiler_params=pltpu.CompilerParams(
            dimension_semantics=("arbitrary", "arbitrary"), vmem_limit_bytes=VMEM_LIMIT),
        name="front_ctx" if is_ctx else "front",
    )(xs, mod, g1, win, qg, kvg, wuqt, wk, wvt, cs, sn, cs.T, sn.T)


def _attn_kernel(qt_ref, k_ref, vt_ref, kc_ref, vct_ref, o_ref, s_ref, mlc_ref, mxb_ref, oe_ref):
    n_ctx = kc_ref.shape[2]
    n_lat = k_ref.shape[2]
    chunks = [(None, 0, n_ctx)] + [(c * ATTN_KC, n_ctx + c * ATTN_KC, ATTN_KC)
                                   for c in range(n_lat // ATTN_KC)]
    sub = 8

    def q_cols(tile):
        if isinstance(tile, int):
            return pl.ds(tile * ATTN_TQ, ATTN_TQ)
        return pl.ds(pl.multiple_of(tile * ATTN_TQ, ATTN_TQ), ATTN_TQ)

    def score_chunk(tile, hd, buf, ci):
        off, soff, w = chunks[ci]
        keys = kc_ref[0, hd] if off is None else k_ref[0, hd, off:off + w, :]
        s = _dot(keys, qt_ref[0, hd, :, q_cols(tile)])
        s_ref[buf, soff:soff + w, :] = s
        mx = s[0:sub]
        for r in range(1, w // sub):
            mx = jnp.maximum(mx, s[r * sub:(r + 1) * sub])
        mlc_ref[buf, ci] = mx

    def row_max(buf):
        mx = mlc_ref[buf, 0]
        for ci in range(1, len(chunks)):
            mx = jnp.maximum(mx, mlc_ref[buf, ci])
        mxb_ref[...] = jnp.broadcast_to(jnp.max(mx, axis=0, keepdims=True), mx.shape)

    def weight_chunk(hd, buf, ci):
        off, soff, w = chunks[ci]
        p = jnp.exp2(s_ref[buf, soff:soff + w, :] - mxb_ref[0:1, :]).astype(BF16)
        vt = vct_ref[0, hd] if off is None else vt_ref[0, hd, :, off:off + w]
        return _dot(vt, p)

    def stage(hw, bw, scoring, bs):
        row_max(bw)
        acc = None
        for ci in range(len(chunks)):
            if scoring is not None:
                score_chunk(*scoring, bs, ci)
            part = weight_chunk(hw, bw, ci)
            acc = part if acc is None else acc + part
        return acc[0:V_DIM] / acc[V_DIM:V_DIM + 1]

    def write_pair(tile, j, ot_odd):
        pair_t = jnp.concatenate([oe_ref[...], ot_odd], axis=0)
        o_ref[0, j, q_cols(tile), :] = pair_t.T.astype(BF16)

    pairs_per_tile = N_HEADS // 2
    n_pairs = (qt_ref.shape[3] // ATTN_TQ) * pairs_per_tile

    def split(p):
        return p // pairs_per_tile, p % pairs_per_tile

    for ci in range(len(chunks)):
        score_chunk(0, 0, 0, ci)

    def head_pair(p, carry):
        tile, j = split(p)
        nxt_tile, nxt_j = split(p + 1)
        oe_ref[...] = stage(2 * j, 0, (tile, 2 * j + 1), 1)
        write_pair(tile, j, stage(2 * j + 1, 1, (nxt_tile, 2 * nxt_j), 0))
        return carry

    lax.fori_loop(0, n_pairs - 1, head_pair, 0)
    tile, j = split(n_pairs - 1)
    oe_ref[...] = stage(2 * j, 0, (tile, 2 * j + 1), 1)
    write_pair(tile, j, stage(2 * j + 1, 1, None, None))


def _attn_call(qt, k, vt, kc, vct):
    bsz, _, _, n = qt.shape
    n_ctx = kc.shape[2]
    tq = ATTN_TQ
    n_chunks = 1 + n // ATTN_KC
    per_batch = lambda b, i: (b, 0, 0, 0)
    resident = dict(pipeline_mode=pl.Buffered(1))
    rows_per_step = ATTN_TILES * tq
    return pl.pallas_call(
        _attn_kernel,
        grid=(bsz, n // rows_per_step),
        in_specs=[
            pl.BlockSpec((1, N_HEADS, LANES, rows_per_step), lambda b, i: (b, 0, 0, i)),
            pl.BlockSpec((1, N_HEADS, n, LANES), per_batch, **resident),
            pl.BlockSpec((1, N_HEADS, VT_ROWS, n), per_batch, **resident),
            pl.BlockSpec((1, N_HEADS, n_ctx, LANES), per_batch, **resident),
            pl.BlockSpec((1, N_HEADS, VT_ROWS, n_ctx), per_batch, **resident),
        ],
        out_specs=pl.BlockSpec((1, N_HEADS // 2, rows_per_step, LANES), lambda b, i: (b, 0, i, 0)),
        out_shape=jax.ShapeDtypeStruct((bsz, N_HEADS // 2, n, LANES), BF16),
        scratch_shapes=[
            pltpu.VMEM((2, n_ctx + n, tq), F32),
            pltpu.VMEM((2, n_chunks, 8, tq), F32),
            pltpu.VMEM((8, tq), F32),
            pltpu.VMEM((V_DIM, tq), F32),
        ],
        compiler_params=pltpu.CompilerParams(
            dimension_semantics=("arbitrary", "arbitrary"), vmem_limit_bytes=VMEM_LIMIT),
        name="attn",
    )(qt, k, vt, kc, vct)


def _mix_kernel(att_ref, u_ref, up_ref, un_ref, x_ref, mod_ref, wpool_ref, pscale_ref, wout_ref,
                g2_ref, rw_ref, rb_ref,
                x1_ref, h2_ref, gate_ref, ridx_ref, cnt_ref, ue_ref, base_ref, *, n_seq):
    b = pl.program_id(0)
    i = pl.program_id(1)
    n_tiles = pl.num_programs(1)
    ts = u_ref.shape[1]

    @pl.when(jnp.logical_and(b == 0, i == 0))
    def _():
        base_ref[...] = jnp.zeros_like(base_ref)

    u = u_ref[0]
    ue_ref[0:POOL_HALO] = jnp.where(i > 0, up_ref[0], 0.0)
    ue_ref[POOL_HALO:POOL_HALO + ts] = u
    ue_ref[POOL_HALO + ts:2 * POOL_HALO + ts] = jnp.where(i < n_tiles - 1, un_ref[0], 0.0)
    t = i * ts + lax.broadcasted_iota(jnp.int32, (ts, 1), 0)
    ys = []
    for g, w in enumerate(POOL_WINDOWS):
        half = w // 2
        lanes = slice(g * POOL_CH, (g + 1) * POOL_CH)
        ws = ue_ref[POOL_HALO - half:POOL_HALO - half + ts, lanes]
        for jj in range(-half + 1, half):
            ws = ws + ue_ref[POOL_HALO + jj:POOL_HALO + jj + ts, lanes]
        count = (jnp.minimum(t + half, n_seq) - jnp.maximum(t - half, 0)).astype(F32)
        mixed = (ws / count - u[:, lanes]).astype(BF16)
        ys.append(_dot(mixed, wpool_ref[g]))
    pool = jnp.concatenate(ys, axis=-1) * pscale_ref[...]

    cat = jnp.concatenate([att_ref[0, j] for j in range(N_HEADS // 2)] + [pool.astype(BF16)], axis=-1)
    m = mod_ref[0]
    x1 = x_ref[0] + m[2:3] * _dot(cat, wout_ref[...])
    x1_ref[0] = x1
    h2 = _rms(x1) * g2_ref[...] * (1.0 + m[4:5]) + m[3:4]
    h2_ref[0] = _pack_bf16_pairs(h2)

    h_hi = h2.astype(BF16)
    h_lo = (h2 - h_hi.astype(F32)).astype(BF16)
    hi_part = _dot(h_hi, rw_ref[...])
    logits = (hi_part[:, :LANES] + hi_part[:, LANES:] + _dot(h_lo, rw_ref[:, :LANES])) + rb_ref[...]
    lane = lax.broadcasted_iota(jnp.int32, logits.shape, 1).astype(F32)
    vals, idxs = [], []
    for _k in range(TOP_K):
        mv = jnp.max(logits, axis=-1, keepdims=True)
        ix = jnp.min(jnp.where(logits == mv, lane, float(LANES)), axis=-1, keepdims=True)
        vals.append(mv)
        idxs.append(ix)
        logits = jnp.where(lane == ix, -jnp.inf, logits)
    es = [jnp.exp(v - vals[0]) for v in vals]
    den = es[0] + es[1] + es[2] + es[3]

    onehot = jnp.zeros(lane.shape, F32)
    for ix in idxs:
        onehot = onehot + jnp.where(lane == ix, 1.0, 0.0)
    row = lax.broadcasted_iota(jnp.int32, (ts, ts), 0)
    col = lax.broadcasted_iota(jnp.int32, (ts, ts), 1)
    tri = jnp.where(col < row, 1.0, 0.0).astype(BF16)
    before = _dot(tri, onehot.astype(BF16)) + base_ref[0:1, :]
    base_new = base_ref[0:1, :] + jnp.sum(onehot, axis=0, keepdims=True)
    base_ref[...] = jnp.broadcast_to(base_new, base_ref.shape)
    cnt_ref[...] = jnp.broadcast_to(base_new, cnt_ref.shape)

    gate_out = jnp.zeros(lane.shape, F32)
    ridx_out = jnp.zeros(lane.shape, F32)
    for kk in range(TOP_K):
        rank = jnp.sum(jnp.where(lane == idxs[kk], before, 0.0), axis=-1, keepdims=True)
        gate_out = jnp.where(lane == float(kk), es[kk] / den, gate_out)
        ridx_out = jnp.where(lane == float(kk), idxs[kk], ridx_out)
        ridx_out = jnp.where(lane == float(TOP_K + kk), rank, ridx_out)
    gate_ref[0] = gate_out
    ridx_ref[0] = ridx_out.astype(jnp.int32)


def _mix_call(att, u, x, mod, wpool, pscale, wout, g2, rw, rb):
    bsz, n, d = x.shape
    ts = MIX_TS
    nt = n // ts
    pool_w = u.shape[2]
    hb = ts // POOL_HALO
    n_halo_blocks = n // POOL_HALO
    const2 = lambda b, i: (0, 0)
    tok = lambda b, i: (b, i, 0)
    return pl.pallas_call(
        functools.partial(_mix_kernel, n_seq=n),
        grid=(bsz, nt),
        in_specs=[
            pl.BlockSpec((1, N_HEADS // 2, ts, LANES), lambda b, i: (b, 0, i, 0)),
            pl.BlockSpec((1, ts, pool_w), tok),
            pl.BlockSpec((1, POOL_HALO, pool_w), lambda b, i: (b, jnp.maximum(i * hb - 1, 0), 0)),
            pl.BlockSpec((1, POOL_HALO, pool_w),
                         lambda b, i: (b, jnp.minimum((i + 1) * hb, n_halo_blocks - 1), 0)),
            pl.BlockSpec((1, ts, d), tok),
            pl.BlockSpec((1, N_MOD, d), lambda b, i: (b, 0, 0)),
            pl.BlockSpec(wpool.shape, lambda b, i: (0, 0, 0)),
            pl.BlockSpec((1, pool_w), const2),
            pl.BlockSpec(wout.shape, const2),
            pl.BlockSpec((1, d), const2),
            pl.BlockSpec(rw.shape, const2),
            pl.BlockSpec((1, LANES), const2),
        ],
        out_specs=[
            pl.BlockSpec((1, ts, d), tok),
            pl.BlockSpec((1, ts, d // 2), tok),
            pl.BlockSpec((1, ts, LANES), tok),
            pl.BlockSpec((1, ts, LANES), tok),
            pl.BlockSpec((8, LANES), const2),
        ],
        out_shape=[
            jax.ShapeDtypeStruct((bsz, n, d), F32),
            jax.ShapeDtypeStruct((bsz, n, d // 2), jnp.uint32),
            jax.ShapeDtypeStruct((bsz, n, LANES), F32),
            jax.ShapeDtypeStruct((bsz, n, LANES), jnp.int32),
            jax.ShapeDtypeStruct((8, LANES), F32),
        ],
        scratch_shapes=[
            pltpu.VMEM((ts + 2 * POOL_HALO, pool_w), F32),
            pltpu.VMEM((8, LANES), F32),
        ],
        compiler_params=pltpu.CompilerParams(
            dimension_semantics=("arbitrary", "arbitrary"), vmem_limit_bytes=VMEM_LIMIT),
        name="mix",
    )(att, u, u, u, x, mod, wpool, pscale, wout, g2, rw, rb)


def _sc_mesh():
    return plsc.VectorSubcoreMesh(core_axis_name="core", subcore_axis_name="subcore")


def _sc_worker_id():
    info = plsc.get_sparse_core_info()
    return lax.axis_index("subcore") * info.num_cores + lax.axis_index("core")


def _sc_num_workers():
    info = plsc.get_sparse_core_info()
    return info.num_cores * info.num_subcores


def _scatter_rows(x, dest3, n_out):
    t, d = x.shape
    top_k, n_chunks, rows = dest3.shape
    per_worker = n_chunks // _sc_num_workers()

    @functools.partial(
        pl.kernel,
        out_type=jax.ShapeDtypeStruct((n_out, d), x.dtype),
        mesh=_sc_mesh(),
        scratch_types=[pltpu.VMEM((top_k, rows), jnp.int32), pltpu.VMEM((rows, d), x.dtype)],
        name="sc_scatter",
    )
    def scatter(x_hbm, i_hbm, o_hbm, idx_v, rows_v):
        wid = _sc_worker_id()

        @pl.loop(0, per_worker)
        def _(c):
            chunk = wid * per_worker + c
            for kk in range(top_k):
                pltpu.sync_copy(i_hbm.at[kk, chunk], idx_v.at[kk])
            pltpu.sync_copy(x_hbm.at[pl.ds(pl.multiple_of(chunk * rows, 8), rows)], rows_v)
            for kk in range(top_k):
                pltpu.sync_copy(rows_v, o_hbm.at[idx_v.at[kk]])

    return scatter(x, dest3)


def _gather_rows(y, idx):
    n = idx.shape[0]
    d = y.shape[1]
    rows = SC_ROWS
    per_worker = n // _sc_num_workers()
    n_chunks = per_worker // rows

    @functools.partial(
        pl.kernel,
        out_type=jax.ShapeDtypeStruct((n, d), y.dtype),
        mesh=_sc_mesh(),
        scratch_types=[pltpu.VMEM((rows,), jnp.int32), pltpu.VMEM((rows, d), y.dtype)],
        name="sc_gather",
    )
    def gather(y_hbm, i_hbm, o_hbm, idx_v, rows_v):
        base = _sc_worker_id() * per_worker

        @pl.loop(0, n_chunks)
        def _(c):
            off = pl.multiple_of(base + c * rows, 8)
            pltpu.sync_copy(i_hbm.at[pl.ds(off, rows)], idx_v)
            pltpu.sync_copy(y_hbm.at[idx_v], rows_v)
            pltpu.sync_copy(rows_v, o_hbm.at[pl.ds(off, rows)])

    return gather(y, idx)


def _moe_kernel(blk_ref, exp_ref, lo_ref, hi_ref, new_ref, init_ref, next_ref, slot_ref, xs_ref,
                wgu_hbm, bgu_ref, wd_hbm, bd_ref, ys_ref, wgu_buf, wd_buf, sems):
    i = pl.program_id(0)
    slot = slot_ref[i]

    def weight_copies(e, s):
        return (pltpu.make_async_copy(wgu_hbm.at[e], wgu_buf.at[s], sems.at[s, 0]),
                pltpu.make_async_copy(wd_hbm.at[e], wd_buf.at[s], sems.at[s, 1]))

    @pl.when(i == 0)
    def _():
        for cp in weight_copies(exp_ref[0], slot):
            cp.start()

    @pl.when(new_ref[i] == 1)
    def _():
        for cp in weight_copies(exp_ref[i], slot):
            cp.wait()

        @pl.when(next_ref[i] >= 0)
        def _():
            for cp in weight_copies(next_ref[i], 1 - slot):
                cp.start()

    @pl.when(init_ref[i] == 1)
    def _():
        ys_ref[...] = jnp.zeros_like(ys_ref)

    lo = lo_ref[i]
    hi = hi_ref[i]

    def expert_rows(r0, n_rows):
        de = wd_buf.shape[1]
        x = _unpack_bf16_pairs(xs_ref[pl.ds(r0, n_rows), :])
        gu = _dot(x, wgu_buf[slot]) + bgu_ref[0]
        g = jnp.minimum(gu[:, :de], SWIGLU_LIMIT)
        lin = jnp.clip(gu[:, de:], -SWIGLU_LIMIT, SWIGLU_LIMIT)
        act = g / (1.0 + jnp.exp(-SWIGLU_ALPHA * g)) * (lin + 1.0)
        y = _dot(act, wd_buf[slot]) + bd_ref[0]
        row = r0 + lax.broadcasted_iota(jnp.int32, (n_rows, 1), 0)
        mine = jnp.logical_and(row >= lo, row < hi)
        ys_ref[pl.ds(r0, n_rows), :] = jnp.where(mine, _pack_bf16_pairs(y), ys_ref[pl.ds(r0, n_rows), :])

    first = lo // MOE_SUB
    pieces = jnp.where(hi > lo, (hi + MOE_SUB - 1) // MOE_SUB - first, 0)
    for cnt in range(1, MOE_BLOCK // MOE_SUB + 1):
        @pl.when(pieces == cnt)
        def _(cnt=cnt):
            r0 = 0 if cnt * MOE_SUB == MOE_BLOCK else pl.multiple_of(first * MOE_SUB, MOE_SUB)
            expert_rows(r0, cnt * MOE_SUB)


def _moe_call(sched, xs, wgu, bgu, wd, bd):
    n_items = sched[0].shape[0]
    _, d, de2 = wgu.shape
    de = de2 // 2
    wmap = lambda i, blk, exp, lo, hi, new, init, nxt, slot: (exp[i], 0, 0)
    xmap = lambda i, blk, exp, lo, hi, new, init, nxt, slot: (blk[i], 0)
    return pl.pallas_call(
        _moe_kernel,
        grid_spec=pltpu.PrefetchScalarGridSpec(
            num_scalar_prefetch=8,
            grid=(n_items,),
            in_specs=[
                pl.BlockSpec((MOE_BLOCK, d // 2), xmap),
                pl.BlockSpec(memory_space=pl.ANY),
                pl.BlockSpec((1, 1, de2), wmap),
                pl.BlockSpec(memory_space=pl.ANY),
                pl.BlockSpec((1, 1, d), wmap),
            ],
            out_specs=pl.BlockSpec((MOE_BLOCK, d // 2), xmap),
            scratch_shapes=[
                pltpu.VMEM((2, d, de2), F32), pltpu.VMEM((2, de, d), F32),
                pltpu.SemaphoreType.DMA((2, 2)),
            ],
        ),
        out_shape=jax.ShapeDtypeStruct(xs.shape, xs.dtype),
        compiler_params=pltpu.CompilerParams(
            dimension_semantics=("arbitrary",), vmem_limit_bytes=VMEM_LIMIT),
        name="moe",
    )(*sched, xs, wgu, bgu, wd, bd)


def _moe_schedule(counts, n_rows):
    n_blocks = n_rows // MOE_BLOCK
    n_items = n_blocks + N_EXPERTS
    ends = jnp.cumsum(counts)
    starts = ends - counts
    first_blk = starts // MOE_BLOCK
    last_blk = (ends - 1) // MOE_BLOCK
    items_per = jnp.where(counts > 0, last_blk - first_blk + 1, 0)
    item_ends = jnp.cumsum(items_per)
    item_starts = item_ends - items_per
    total = item_ends[-1]
    it = jnp.arange(n_items, dtype=jnp.int32)
    live = it < total
    itc = jnp.minimum(it, total - 1)
    exp = jnp.sum((item_ends[None, :] <= itc[:, None]).astype(jnp.int32), axis=1)
    is_exp = exp[:, None] == jnp.arange(N_EXPERTS, dtype=jnp.int32)[None, :]
    pick = lambda table: jnp.sum(jnp.where(is_exp, table[None, :], 0), axis=1)
    blk = pick(first_blk) + itc - pick(item_starts)
    lo = jnp.clip(pick(starts) - blk * MOE_BLOCK, 0, MOE_BLOCK)
    hi = jnp.clip(pick(ends) - blk * MOE_BLOCK, 0, MOE_BLOCK)
    hi = jnp.where(live, hi, lo)
    prev_exp = jnp.concatenate([jnp.full((1,), -1, jnp.int32), exp[:-1]])
    prev_blk = jnp.concatenate([jnp.full((1,), -1, jnp.int32), blk[:-1]])
    new = jnp.logical_and(live, exp != prev_exp)
    init = jnp.logical_and(live, blk != prev_blk)
    slot = (jnp.cumsum(new.astype(jnp.int32)) - 1) % 2
    ar = jnp.arange(N_EXPERTS, dtype=jnp.int32)
    later = jnp.logical_and(counts[None, :] > 0, ar[None, :] > ar[:, None])
    next_exp = jnp.min(jnp.where(later, ar[None, :], N_EXPERTS), axis=1)
    next_exp = jnp.where(next_exp == N_EXPERTS, -1, next_exp)
    nxt = pick(next_exp)
    as_i32 = lambda a: a.astype(jnp.int32)
    return tuple(as_i32(a) for a in (blk, exp, lo, hi, new, init, nxt, slot)), starts


def _combine_kernel(*refs):
    yg_refs = refs[:TOP_K]
    x1_ref, gate_ref, mod_ref, fg_ref, o_ref = refs[TOP_K:]
    gates = gate_ref[...]
    y = gates[:, 0:1] * _unpack_bf16_pairs(yg_refs[0][...])
    for kk in range(1, TOP_K):
        y = y + gates[:, kk:kk + 1] * _unpack_bf16_pairs(yg_refs[kk][...])
    m = mod_ref[0]
    x2 = x1_ref[...] + m[5:6] * y
    o_ref[...] = _rms(x2) * fg_ref[...]


def _combine_call(yg, x1, gates, mod, fg, n_seq, part):
    t, d = x1.shape
    tt = COMBINE_TT
    tiles_per_seq = n_seq // tt
    n_tiles = yg.shape[0] // (TOP_K * tt)
    first = part * n_tiles
    tok = lambda i: (first + i, 0)
    slot_specs = [pl.BlockSpec((tt, d // 2), functools.partial(lambda kk, i: (kk * n_tiles + i, 0), kk))
                  for kk in range(TOP_K)]
    return pl.pallas_call(
        _combine_kernel,
        grid=(n_tiles,),
        in_specs=slot_specs + [
            pl.BlockSpec((tt, d), tok),
            pl.BlockSpec((tt, LANES), tok),
            pl.BlockSpec((1, N_MOD, d), lambda i: ((first + i) // tiles_per_seq, 0, 0)),
            pl.BlockSpec((1, d), lambda i: (0, 0)),
        ],
        out_specs=pl.BlockSpec((tt, d), tok),
        out_shape=jax.ShapeDtypeStruct((t, d), F32),
        input_output_aliases={TOP_K: 0},
        compiler_params=pltpu.CompilerParams(
            dimension_semantics=("arbitrary",), vmem_limit_bytes=VMEM_LIMIT),
        name="combine",
    )(*([yg] * TOP_K), x1, gates, mod, fg)


def _pad_cols(a, width):
    return jnp.pad(a, ((0, 0), (0, width - a.shape[1])))


def _rope_slab(w_rope):
    return jnp.pad(w_rope, ((0, 0), (QK_NOPE, LANES - QK_NOPE - QK_ROPE)))


def _prep_w_in(w_in):
    kr0 = Q_LORA + KV_LORA
    w_kr = w_in[:, kr0:kr0 + QK_ROPE]
    return jnp.concatenate(
        [w_in[:, :kr0], _rope_slab(w_kr), w_in[:, kr0 + QK_ROPE:]], axis=1).astype(BF16)


def _prep_w_uq(w_uq):
    per = QK_NOPE + QK_ROPE
    rows = w_uq.shape[0]
    w = w_uq.reshape(rows, N_HEADS, per)
    slab = jnp.pad(w, ((0, 0), (0, 0), (0, LANES - per)))
    return slab.reshape(rows, -1).T.astype(BF16)


def _prep_w_ukv(w_ukv):
    per = QK_NOPE + V_DIM
    rows = w_ukv.shape[0]
    w = w_ukv.reshape(rows, N_HEADS, per)
    wk = jnp.pad(w[:, :, :QK_NOPE], ((0, 0), (0, 0), (0, LANES - QK_NOPE))).reshape(rows, -1)
    wvt = jnp.pad(jnp.transpose(w[:, :, QK_NOPE:], (1, 2, 0)), ((0, 0), (0, VT_ROWS - V_DIM), (0, 0)))
    return wk.astype(BF16), wvt.reshape(N_HEADS * VT_ROWS, rows).astype(BF16)


def _rope_tables(n_lat):
    rows = n_lat // GRID_W
    nf = QK_ROPE // 4
    row = jnp.repeat(jnp.arange(rows, dtype=F32), GRID_W)
    col = jnp.tile(jnp.arange(GRID_W, dtype=F32), rows)
    freqs = ROPE_BASE ** (-jnp.arange(nf, dtype=F32) / nf)
    ang = jnp.concatenate([row[:, None] * freqs, col[:, None] * freqs], axis=-1)
    cos, sin = jnp.cos(ang), jnp.sin(ang)
    ones = jnp.ones((n_lat, QK_NOPE), F32)
    zeros = jnp.zeros((n_lat, QK_NOPE), F32)
    cs = _pad_cols(jnp.concatenate([ones, cos, cos], axis=1), LANES)
    sn = _pad_cols(jnp.concatenate([zeros, -sin, sin], axis=1), LANES)
    return cs, sn


def kernel(x, c, ctx, c_ctx, w_mod, b_mod, norm1_g, w_in, q_norm_g, kv_norm_g, w_uq, w_ukv, w_pool,
           pool_scale, w_out, norm2_g, router_w, router_b, w_gate_up, b_gate_up, w_down, b_down,
           final_g):
    bsz, n, d = x.shape
    n_ctx = ctx.shape[1]
    t = bsz * n
    l = 0

    cc = jnp.concatenate([c, c_ctx[None, :], jnp.zeros((8 - bsz - 1, d), F32)], axis=0)
    mod = _mod_call(cc, w_mod[l], b_mod[l][None, :]).reshape(8, N_MOD, d)
    mod_lat, mod_ctx = mod[:bsz], mod[bsz:bsz + 1]

    win = _prep_w_in(w_in[l])
    wuqt = _prep_w_uq(w_uq[l])
    wk, wvt = _prep_w_ukv(w_ukv[l])
    cs, sn = _rope_tables(n)
    cs_ctx = jnp.broadcast_to((jnp.arange(LANES) < QK_NOPE + QK_ROPE).astype(F32), (n_ctx, LANES))
    sn_ctx = jnp.zeros((n_ctx, LANES), F32)
    g1 = norm1_g[l][None, :]
    qg = q_norm_g[l][None, :]
    kvg = kv_norm_g[l][None, :]

    qt, k, vt, u = _front_call(x, mod_lat, True, g1, win, qg, kvg, wuqt, wk, wvt, cs, sn,
                               is_ctx=False, ts=FRONT_TS)
    kc, vct = _front_call(ctx, mod_ctx, False, g1, win[:, Q_LORA:Q_LORA + 2 * LANES], qg, kvg, wuqt,
                          wk, wvt, cs_ctx, sn_ctx, is_ctx=True, ts=n_ctx)
    att = _attn_call(qt, k, vt, kc, vct)

    rw = _pad_cols(router_w[l], LANES)
    rw_hi = rw.astype(BF16)
    rw = jnp.concatenate([rw_hi, (rw - rw_hi.astype(F32)).astype(BF16)], axis=1)
    rb = jnp.concatenate([router_b[l], jnp.full((LANES - N_EXPERTS,), -jnp.inf, F32)])[None, :]
    x1, h2, gates, ridx, cnt = _mix_call(
        att, u, x, mod_lat, w_pool[l].astype(BF16), pool_scale[l][None, :], w_out[l].astype(BF16),
        norm2_g[l][None, :], rw, rb)

    counts = cnt[0, :N_EXPERTS].astype(jnp.int32)
    sched, starts = _moe_schedule(counts, t * TOP_K)
    ridx = ridx.reshape(t, LANES)
    is_exp = ridx[:, :TOP_K, None] == jnp.arange(N_EXPERTS, dtype=jnp.int32)
    dest = jnp.sum(jnp.where(is_exp, starts, 0), axis=-1) + ridx[:, TOP_K:2 * TOP_K]
    dest_t = dest.T

    xs = _scatter_rows(h2.reshape(t, d // 2), dest_t.reshape(TOP_K, t // SC_ROWS, SC_ROWS), t * TOP_K)
    ys = _moe_call(sched, xs, w_gate_up[l], b_gate_up[l][:, None, :], w_down[l],
                   b_down[l][:, None, :])
    out = x1.reshape(t, d)
    gates = gates.reshape(t, LANES)
    tp = t // COMBINE_PARTS
    for part in range(COMBINE_PARTS):
        idx = dest_t[:, part * tp:(part + 1) * tp].reshape(-1)
        yg = _gather_rows(ys, idx)
        out = _combine_call(yg, out, gates, mod_lat, final_g[None, :], n, part)
    return out.reshape(bsz, n, d)
```

```python
import functools
import math

import jax
import jax.numpy as jnp
from jax import lax
from jax.experimental import pallas as pl
from jax.experimental.pallas import tpu as pltpu
from jax.experimental.pallas import tpu_sc as plsc

F32 = jnp.float32
BF16 = jnp.bfloat16
HIGHEST = lax.Precision.HIGHEST

N_HEADS = 8
QK_NOPE = 64
QK_ROPE = 32
V_DIM = 64
Q_LORA = 256
KV_LORA = 128
GRID_W = 64
ROPE_BASE = 10000.0
POOL_WINDOWS = (2, 4, 8, 16)
POOL_CH = 128
N_EXPERTS = 32
TOP_K = 4
SWIGLU_LIMIT = 7.0
SWIGLU_ALPHA = 1.702
MOE_BLOCK = 512
MOE_SUB = 128
N_MOD = 6
EPS = 1e-6

LANES = 128
POOL_HALO = 8
VT_ROWS = 80
VMEM_LIMIT = 56 * 1024 * 1024

FRONT_TS = 1024
ATTN_TQ = 512
ATTN_TILES = 4
ATTN_KC = 1024
MIX_TS = 512
COMBINE_TT = 512
COMBINE_PARTS = 4
SC_ROWS = 128


def _dot(a, b, **kw):
    return jnp.dot(a, b, preferred_element_type=F32, **kw)


def _dot_nt(a, b):
    return lax.dot_general(a, b, (((1,), (1,)), ((), ())), preferred_element_type=F32)


def _rms(x):
    return x * lax.rsqrt(jnp.mean(x * x, axis=-1, keepdims=True) + EPS)


def _pack_bf16_pairs(x):
    w = x.shape[1] // 2
    return pltpu.pack_elementwise([x[:, :w], x[:, w:]], packed_dtype=BF16)


def _unpack_bf16_pairs(words):
    halves = [pltpu.unpack_elementwise(words, index=h, packed_dtype=BF16, unpacked_dtype=F32)
              for h in range(2)]
    return jnp.concatenate(halves, axis=-1)


def _mod_kernel(c_ref, w_ref, b_ref, o_ref):
    c = c_ref[...]
    a = c / (1.0 + jnp.exp(-c))
    o_ref[...] = _dot(a, w_ref[...], precision=HIGHEST) + b_ref[...]


def _mod_call(cc, w_mod, b_mod):
    d = w_mod.shape[0]
    return pl.pallas_call(
        _mod_kernel,
        grid=(N_MOD,),
        in_specs=[
            pl.BlockSpec((8, d), lambda j: (0, 0)),
            pl.BlockSpec((d, d), lambda j: (0, j)),
            pl.BlockSpec((1, d), lambda j: (0, j)),
        ],
        out_specs=pl.BlockSpec((8, d), lambda j: (0, j)),
        out_shape=jax.ShapeDtypeStruct((8, N_MOD * d), F32),
        compiler_params=pltpu.CompilerParams(
            dimension_semantics=("arbitrary",), vmem_limit_bytes=VMEM_LIMIT),
        name="mod",
    )(cc, w_mod, b_mod)


def _front_kernel(x_ref, mod_ref, g1_ref, win_ref, qg_ref, kvg_ref, wuqt_ref, wk_ref, wvt_ref,
                  cs_ref, sn_ref, cst_ref, snt_ref, *out_refs, is_ctx, scale):
    x = x_ref[0]
    m = mod_ref[0]
    h = _rms(x) * g1_ref[...] * (1.0 + m[1:2]) + m[0:1]
    z = _dot(h.astype(BF16), win_ref[...])
    if is_ctx:
        k_ref, vt_ref = out_refs
        zkv = z
    else:
        qt_ref, k_ref, vt_ref, u_ref = out_refs
        zkv = z[:, Q_LORA:Q_LORA + 2 * LANES]
        u_ref[0] = z[:, Q_LORA + 2 * LANES:]

    ckv = _rms(zkv[:, :KV_LORA]) * kvg_ref[...]
    kk = _dot(ckv.astype(BF16), wk_ref[...])
    kra = zkv[:, LANES:2 * LANES]
    lane = lax.broadcasted_iota(jnp.int32, kra.shape, 1)
    half = QK_ROPE // 2
    partner = jnp.where(lane < QK_NOPE + half, pltpu.roll(kra, LANES - half, axis=1),
                        pltpu.roll(kra, half, axis=1))
    kr = kra * cs_ref[...] + partner * sn_ref[...]
    for hd in range(N_HEADS):
        k_ref[0, hd] = (kk[:, hd * LANES:(hd + 1) * LANES] + kr).astype(BF16)
    vt = _dot(wvt_ref[...], ckv.T.astype(BF16))
    row = lax.broadcasted_iota(jnp.int32, (VT_ROWS, 1), 0)
    ones = jnp.where(row == V_DIM, 1.0, 0.0)
    for hd in range(N_HEADS):
        vt_ref[0, hd] = (vt[hd * VT_ROWS:(hd + 1) * VT_ROWS] + ones).astype(BF16)

    if not is_ctx:
        cq = _rms(z[:, :Q_LORA]) * qg_ref[...]
        qqt = _dot(wuqt_ref[...], cq.T.astype(BF16))
        cst = cst_ref[...]
        snt = snt_ref[...]
        r1, r2, r3 = QK_NOPE, QK_NOPE + QK_ROPE // 2, QK_NOPE + QK_ROPE
        for hd in range(N_HEADS):
            qa = qqt[hd * LANES:(hd + 1) * LANES]
            qb = jnp.concatenate([qa[:r1], qa[r2:r3], qa[r1:r2], qa[r3:]], axis=0)
            qt_ref[0, hd] = ((qa * cst + qb * snt) * scale).astype(BF16)


def _front_call(xs, mod, mod_per_batch, g1, win, qg, kvg, wuqt, wk, wvt, cs, sn, *, is_ctx, ts):
    bsz, n, d = xs.shape
    nt = n // ts
    scale = math.log2(math.e) / math.sqrt(QK_NOPE + QK_ROPE)
    const = lambda b, i: (0, 0)
    mod_map = (lambda b, i: (b, 0, 0)) if mod_per_batch else (lambda b, i: (0, 0, 0))
    in_specs = [
        pl.BlockSpec((1, ts, d), lambda b, i: (b, i, 0)),
        pl.BlockSpec((1, N_MOD, d), mod_map),
        pl.BlockSpec((1, d), const),
        pl.BlockSpec(win.shape, const),
        pl.BlockSpec((1, Q_LORA), const),
        pl.BlockSpec((1, KV_LORA), const),
        pl.BlockSpec(wuqt.shape, const),
        pl.BlockSpec(wk.shape, const),
        pl.BlockSpec(wvt.shape, const),
        pl.BlockSpec((ts, LANES), lambda b, i: (i, 0)),
        pl.BlockSpec((ts, LANES), lambda b, i: (i, 0)),
        pl.BlockSpec((LANES, ts), lambda b, i: (0, i)),
        pl.BlockSpec((LANES, ts), lambda b, i: (0, i)),
    ]
    k_spec = pl.BlockSpec((1, N_HEADS, ts, LANES), lambda b, i: (b, 0, i, 0))
    k_shape = jax.ShapeDtypeStruct((bsz, N_HEADS, n, LANES), BF16)
    qt_spec = pl.BlockSpec((1, N_HEADS, LANES, ts), lambda b, i: (b, 0, 0, i))
    qt_shape = jax.ShapeDtypeStruct((bsz, N_HEADS, LANES, n), BF16)
    vt_spec = pl.BlockSpec((1, N_HEADS, VT_ROWS, ts), lambda b, i: (b, 0, 0, i))
    vt_shape = jax.ShapeDtypeStruct((bsz, N_HEADS, VT_ROWS, n), BF16)
    if is_ctx:
        out_specs = [k_spec, vt_spec]
        out_shape = [k_shape, vt_shape]
    else:
        pool_w = win.shape[1] - Q_LORA - 2 * LANES
        out_specs = [qt_spec, k_spec, vt_spec, pl.BlockSpec((1, ts, pool_w), lambda b, i: (b, i, 0))]
        out_shape = [qt_shape, k_shape, vt_shape, jax.ShapeDtypeStruct((bsz, n, pool_w), F32)]
    return pl.pallas_call(
        functools.partial(_front_kernel, is_ctx=is_ctx, scale=scale),
        grid=(bsz, nt),
        in_specs=in_specs,
        out_specs=out_specs,
        out_shape=out_shape,
        compiler_params=pltpu.CompilerParams(
            dimension_semantics=("arbitrary", "arbitrary"), vmem_limit_bytes=VMEM_LIMIT),
        name="front_ctx" if is_ctx else "front",
    )(xs, mod, g1, win, qg, kvg, wuqt, wk, wvt, cs, sn, cs.T, sn.T)


def _attn_kernel(qt_ref, k_ref, vt_ref, kc_ref, vct_ref, o_ref, s_ref, mlc_ref, mxb_ref, oe_ref):
    n_ctx = kc_ref.shape[2]
    n_lat = k_ref.shape[2]
    chunks = [(None, 0, n_ctx)] + [(c * ATTN_KC, n_ctx + c * ATTN_KC, ATTN_KC)
                                   for c in range(n_lat // ATTN_KC)]
    sub = 8

    def q_cols(tile):
        if isinstance(tile, int):
            return pl.ds(tile * ATTN_TQ, ATTN_TQ)
        return pl.ds(pl.multiple_of(tile * ATTN_TQ, ATTN_TQ), ATTN_TQ)

    def score_chunk(tile, hd, buf, ci):
        off, soff, w = chunks[ci]
        keys = kc_ref[0, hd] if off is None else k_ref[0, hd, off:off + w, :]
        s = _dot(keys, qt_ref[0, hd, :, q_cols(tile)])
        s_ref[buf, soff:soff + w, :] = s
        mx = s[0:sub]
        for r in range(1, w // sub):
            mx = jnp.maximum(mx, s[r * sub:(r + 1) * sub])
        mlc_ref[buf, ci] = mx

    def row_max(buf):
        mx = mlc_ref[buf, 0]
        for ci in range(1, len(chunks)):
            mx = jnp.maximum(mx, mlc_ref[buf, ci])
        mxb_ref[...] = jnp.broadcast_to(jnp.max(mx, axis=0, keepdims=True), mx.shape)

    def weight_chunk(hd, buf, ci):
        off, soff, w = chunks[ci]
        p = jnp.exp2(s_ref[buf, soff:soff + w, :] - mxb_ref[0:1, :]).astype(BF16)
        vt = vct_ref[0, hd] if off is None else vt_ref[0, hd, :, off:off + w]
        return _dot(vt, p)

    def stage(hw, bw, scoring, bs):
        row_max(bw)
        acc = None
        for ci in range(len(chunks)):
            if scoring is not None:
                score_chunk(*scoring, bs, ci)
            part = weight_chunk(hw, bw, ci)
            acc = part if acc is None else acc + part
        return acc[0:V_DIM] / acc[V_DIM:V_DIM + 1]

    def write_pair(tile, j, ot_odd):
        pair_t = jnp.concatenate([oe_ref[...], ot_odd], axis=0)
        o_ref[0, j, q_cols(tile), :] = pair_t.T.astype(BF16)

    pairs_per_tile = N_HEADS // 2
    n_pairs = (qt_ref.shape[3] // ATTN_TQ) * pairs_per_tile

    def split(p):
        return p // pairs_per_tile, p % pairs_per_tile

    for ci in range(len(chunks)):
        score_chunk(0, 0, 0, ci)

    def head_pair(p, carry):
        tile, j = split(p)
        nxt_tile, nxt_j = split(p + 1)
        oe_ref[...] = stage(2 * j, 0, (tile, 2 * j + 1), 1)
        write_pair(tile, j, stage(2 * j + 1, 1, (nxt_tile, 2 * nxt_j), 0))
        return carry

    lax.fori_loop(0, n_pairs - 1, head_pair, 0)
    tile, j = split(n_pairs - 1)
    oe_ref[...] = stage(2 * j, 0, (tile, 2 * j + 1), 1)
    write_pair(tile, j, stage(2 * j + 1, 1, None, None))


def _attn_call(qt, k, vt, kc, vct):
    bsz, _, _, n = qt.shape
    n_ctx = kc.shape[2]
    tq = ATTN_TQ
    n_chunks = 1 + n // ATTN_KC
    per_batch = lambda b, i: (b, 0, 0, 0)
    resident = dict(pipeline_mode=pl.Buffered(1))
    rows_per_step = ATTN_TILES * tq
    return pl.pallas_call(
        _attn_kernel,
        grid=(bsz, n // rows_per_step),
        in_specs=[
            pl.BlockSpec((1, N_HEADS, LANES, rows_per_step), lambda b, i: (b, 0, 0, i)),
            pl.BlockSpec((1, N_HEADS, n, LANES), per_batch, **resident),
            pl.BlockSpec((1, N_HEADS, VT_ROWS, n), per_batch, **resident),
            pl.BlockSpec((1, N_HEADS, n_ctx, LANES), per_batch, **resident),
            pl.BlockSpec((1, N_HEADS, VT_ROWS, n_ctx), per_batch, **resident),
        ],
        out_specs=pl.BlockSpec((1, N_HEADS // 2, rows_per_step, LANES), lambda b, i: (b, 0, i, 0)),
        out_shape=jax.ShapeDtypeStruct((bsz, N_HEADS // 2, n, LANES), BF16),
        scratch_shapes=[
            pltpu.VMEM((2, n_ctx + n, tq), F32),
            pltpu.VMEM((2, n_chunks, 8, tq), F32),
            pltpu.VMEM((8, tq), F32),
            pltpu.VMEM((V_DIM, tq), F32),
        ],
        compiler_params=pltpu.CompilerParams(
            dimension_semantics=("arbitrary", "arbitrary"), vmem_limit_bytes=VMEM_LIMIT),
        name="attn",
    )(qt, k, vt, kc, vct)


def _mix_kernel(att_ref, u_ref, up_ref, un_ref, x_ref, mod_ref, wpool_ref, pscale_ref, wout_ref,
                g2_ref, rw_ref, rb_ref,
                x1_ref, h2_ref, gate_ref, ridx_ref, cnt_ref, ue_ref, base_ref, *, n_seq):
    b = pl.program_id(0)
    i = pl.program_id(1)
    n_tiles = pl.num_programs(1)
    ts = u_ref.shape[1]

    @pl.when(jnp.logical_and(b == 0, i == 0))
    def _():
        base_ref[...] = jnp.zeros_like(base_ref)

    u = u_ref[0]
    ue_ref[0:POOL_HALO] = jnp.where(i > 0, up_ref[0], 0.0)
    ue_ref[POOL_HALO:POOL_HALO + ts] = u
    ue_ref[POOL_HALO + ts:2 * POOL_HALO + ts] = jnp.where(i < n_tiles - 1, un_ref[0], 0.0)
    t = i * ts + lax.broadcasted_iota(jnp.int32, (ts, 1), 0)
    ys = []
    for g, w in enumerate(POOL_WINDOWS):
        half = w // 2
        lanes = slice(g * POOL_CH, (g + 1) * POOL_CH)
        ws = ue_ref[POOL_HALO - half:POOL_HALO - half + ts, lanes]
        for jj in range(-half + 1, half):
            ws = ws + ue_ref[POOL_HALO + jj:POOL_HALO + jj + ts, lanes]
        count = (jnp.minimum(t + half, n_seq) - jnp.maximum(t - half, 0)).astype(F32)
        mixed = (ws / count - u[:, lanes]).astype(BF16)
        ys.append(_dot(mixed, wpool_ref[g]))
    pool = jnp.concatenate(ys, axis=-1) * pscale_ref[...]

    cat = jnp.concatenate([att_ref[0, j] for j in range(N_HEADS // 2)] + [pool.astype(BF16)], axis=-1)
    m = mod_ref[0]
    x1 = x_ref[0] + m[2:3] * _dot(cat, wout_ref[...])
    x1_ref[0] = x1
    h2 = _rms(x1) * g2_ref[...] * (1.0 + m[4:5]) + m[3:4]
    h2_ref[0] = _pack_bf16_pairs(h2)

    h_hi = h2.astype(BF16)
    h_lo = (h2 - h_hi.astype(F32)).astype(BF16)
    hi_part = _dot(h_hi, rw_ref[...])
    logits = (hi_part[:, :LANES] + hi_part[:, LANES:] + _dot(h_lo, rw_ref[:, :LANES])) + rb_ref[...]
    lane = lax.broadcasted_iota(jnp.int32, logits.shape, 1).astype(F32)
    vals, idxs = [], []
    for _k in range(TOP_K):
        mv = jnp.max(logits, axis=-1, keepdims=True)
        ix = jnp.min(jnp.where(logits == mv, lane, float(LANES)), axis=-1, keepdims=True)
        vals.append(mv)
        idxs.append(ix)
        logits = jnp.where(lane == ix, -jnp.inf, logits)
    es = [jnp.exp(v - vals[0]) for v in vals]
    den = es[0] + es[1] + es[2] + es[3]

    onehot = jnp.zeros(lane.shape, F32)
    for ix in idxs:
        onehot = onehot + jnp.where(lane == ix, 1.0, 0.0)
    row = lax.broadcasted_iota(jnp.int32, (ts, ts), 0)
    col = lax.broadcasted_iota(jnp.int32, (ts, ts), 1)
    tri = jnp.where(col < row, 1.0, 0.0).astype(BF16)
    before = _dot(tri, onehot.astype(BF16)) + base_ref[0:1, :]
    base_new = base_ref[0:1, :] + jnp.sum(onehot, axis=0, keepdims=True)
    base_ref[...] = jnp.broadcast_to(base_new, base_ref.shape)
    cnt_ref[...] = jnp.broadcast_to(base_new, cnt_ref.shape)

    gate_out = jnp.zeros(lane.shape, F32)
    ridx_out = jnp.zeros(lane.shape, F32)
    for kk in range(TOP_K):
        rank = jnp.sum(jnp.where(lane == idxs[kk], before, 0.0), axis=-1, keepdims=True)
        gate_out = jnp.where(lane == float(kk), es[kk] / den, gate_out)
        ridx_out = jnp.where(lane == float(kk), idxs[kk], ridx_out)
        ridx_out = jnp.where(lane == float(TOP_K + kk), rank, ridx_out)
    gate_ref[0] = gate_out
    ridx_ref[0] = ridx_out.astype(jnp.int32)


def _mix_call(att, u, x, mod, wpool, pscale, wout, g2, rw, rb):
    bsz, n, d = x.shape
    ts = MIX_TS
    nt = n // ts
    pool_w = u.shape[2]
    hb = ts // POOL_HALO
    n_halo_blocks = n // POOL_HALO
    const2 = lambda b, i: (0, 0)
    tok = lambda b, i: (b, i, 0)
    return pl.pallas_call(
        functools.partial(_mix_kernel, n_seq=n),
        grid=(bsz, nt),
        in_specs=[
            pl.BlockSpec((1, N_HEADS // 2, ts, LANES), lambda b, i: (b, 0, i, 0)),
            pl.BlockSpec((1, ts, pool_w), tok),
            pl.BlockSpec((1, POOL_HALO, pool_w), lambda b, i: (b, jnp.maximum(i * hb - 1, 0), 0)),
            pl.BlockSpec((1, POOL_HALO, pool_w),
                         lambda b, i: (b, jnp.minimum((i + 1) * hb, n_halo_blocks - 1), 0)),
            pl.BlockSpec((1, ts, d), tok),
            pl.BlockSpec((1, N_MOD, d), lambda b, i: (b, 0, 0)),
            pl.BlockSpec(wpool.shape, lambda b, i: (0, 0, 0)),
            pl.BlockSpec((1, pool_w), const2),
            pl.BlockSpec(wout.shape, const2),
            pl.BlockSpec((1, d), const2),
            pl.BlockSpec(rw.shape, const2),
            pl.BlockSpec((1, LANES), const2),
        ],
        out_specs=[
            pl.BlockSpec((1, ts, d), tok),
            pl.BlockSpec((1, ts, d // 2), tok),
            pl.BlockSpec((1, ts, LANES), tok),
            pl.BlockSpec((1, ts, LANES), tok),
            pl.BlockSpec((8, LANES), const2),
        ],
        out_shape=[
            jax.ShapeDtypeStruct((bsz, n, d), F32),
            jax.ShapeDtypeStruct((bsz, n, d // 2), jnp.uint32),
            jax.ShapeDtypeStruct((bsz, n, LANES), F32),
            jax.ShapeDtypeStruct((bsz, n, LANES), jnp.int32),
            jax.ShapeDtypeStruct((8, LANES), F32),
        ],
        scratch_shapes=[
            pltpu.VMEM((ts + 2 * POOL_HALO, pool_w), F32),
            pltpu.VMEM((8, LANES), F32),
        ],
        compiler_params=pltpu.CompilerParams(
            dimension_semantics=("arbitrary", "arbitrary"), vmem_limit_bytes=VMEM_LIMIT),
        name="mix",
    )(att, u, u, u, x, mod, wpool, pscale, wout, g2, rw, rb)


def _sc_mesh():
    return plsc.VectorSubcoreMesh(core_axis_name="core", subcore_axis_name="subcore")


def _sc_worker_id():
    info = plsc.get_sparse_core_info()
    return lax.axis_index("subcore") * info.num_cores + lax.axis_index("core")


def _sc_num_workers():
    info = plsc.get_sparse_core_info()
    return info.num_cores * info.num_subcores


def _scatter_rows(x, dest3, n_out):
    t, d = x.shape
    top_k, n_chunks, rows = dest3.shape
    per_worker = n_chunks // _sc_num_workers()

    @functools.partial(
        pl.kernel,
        out_type=jax.ShapeDtypeStruct((n_out, d), x.dtype),
        mesh=_sc_mesh(),
        scratch_types=[pltpu.VMEM((top_k, rows), jnp.int32), pltpu.VMEM((rows, d), x.dtype)],
        name="sc_scatter",
    )
    def scatter(x_hbm, i_hbm, o_hbm, idx_v, rows_v):
        wid = _sc_worker_id()

        @pl.loop(0, per_worker)
        def _(c):
            chunk = wid * per_worker + c
            for kk in range(top_k):
                pltpu.sync_copy(i_hbm.at[kk, chunk], idx_v.at[kk])
            pltpu.sync_copy(x_hbm.at[pl.ds(pl.multiple_of(chunk * rows, 8), rows)], rows_v)
            for kk in range(top_k):
                pltpu.sync_copy(rows_v, o_hbm.at[idx_v.at[kk]])

    return scatter(x, dest3)


def _gather_rows(y, idx):
    n = idx.shape[0]
    d = y.shape[1]
    rows = SC_ROWS
    per_worker = n // _sc_num_workers()
    n_chunks = per_worker // rows

    @functools.partial(
        pl.kernel,
        out_type=jax.ShapeDtypeStruct((n, d), y.dtype),
        mesh=_sc_mesh(),
        scratch_types=[pltpu.VMEM((rows,), jnp.int32), pltpu.VMEM((rows, d), y.dtype)],
        name="sc_gather",
    )
    def gather(y_hbm, i_hbm, o_hbm, idx_v, rows_v):
        base = _sc_worker_id() * per_worker

        @pl.loop(0, n_chunks)
        def _(c):
            off = pl.multiple_of(base + c * rows, 8)
            pltpu.sync_copy(i_hbm.at[pl.ds(off, rows)], idx_v)
            pltpu.sync_copy(y_hbm.at[idx_v], rows_v)
            pltpu.sync_copy(rows_v, o_hbm.at[pl.ds(off, rows)])

    return gather(y, idx)


def _moe_kernel(blk_ref, exp_ref, lo_ref, hi_ref, new_ref, init_ref, next_ref, slot_ref, xs_ref,
                wgu_hbm, bgu_ref, wd_hbm, bd_ref, ys_ref, wgu_buf, wd_buf, sems):
    i = pl.program_id(0)
    slot = slot_ref[i]

    def weight_copies(e, s):
        return (pltpu.make_async_copy(wgu_hbm.at[e], wgu_buf.at[s], sems.at[s, 0]),
                pltpu.make_async_copy(wd_hbm.at[e], wd_buf.at[s], sems.at[s, 1]))

    @pl.when(i == 0)
    def _():
        for cp in weight_copies(exp_ref[0], slot):
            cp.start()

    @pl.when(new_ref[i] == 1)
    def _():
        for cp in weight_copies(exp_ref[i], slot):
            cp.wait()

        @pl.when(next_ref[i] >= 0)
        def _():
            for cp in weight_copies(next_ref[i], 1 - slot):
                cp.start()

    @pl.when(init_ref[i] == 1)
    def _():
        ys_ref[...] = jnp.zeros_like(ys_ref)

    lo = lo_ref[i]
    hi = hi_ref[i]

    def expert_rows(r0, n_rows):
        de = wd_buf.shape[1]
        x = _unpack_bf16_pairs(xs_ref[pl.ds(r0, n_rows), :])
        gu = _dot(x, wgu_buf[slot]) + bgu_ref[0]
        g = jnp.minimum(gu[:, :de], SWIGLU_LIMIT)
        lin = jnp.clip(gu[:, de:], -SWIGLU_LIMIT, SWIGLU_LIMIT)
        act = g / (1.0 + jnp.exp(-SWIGLU_ALPHA * g)) * (lin + 1.0)
        y = _dot(act, wd_buf[slot]) + bd_ref[0]
        row = r0 + lax.broadcasted_iota(jnp.int32, (n_rows, 1), 0)
        mine = jnp.logical_and(row >= lo, row < hi)
        ys_ref[pl.ds(r0, n_rows), :] = jnp.where(mine, _pack_bf16_pairs(y), ys_ref[pl.ds(r0, n_rows), :])

    first = lo // MOE_SUB
    pieces = jnp.where(hi > lo, (hi + MOE_SUB - 1) // MOE_SUB - first, 0)
    for cnt in range(1, MOE_BLOCK // MOE_SUB + 1):
        @pl.when(pieces == cnt)
        def _(cnt=cnt):
            r0 = 0 if cnt * MOE_SUB == MOE_BLOCK else pl.multiple_of(first * MOE_SUB, MOE_SUB)
            expert_rows(r0, cnt * MOE_SUB)


def _moe_call(sched, xs, wgu, bgu, wd, bd):
    n_items = sched[0].shape[0]
    _, d, de2 = wgu.shape
    de = de2 // 2
    wmap = lambda i, blk, exp, lo, hi, new, init, nxt, slot: (exp[i], 0, 0)
    xmap = lambda i, blk, exp, lo, hi, new, init, nxt, slot: (blk[i], 0)
    return pl.pallas_call(
        _moe_kernel,
        grid_spec=pltpu.PrefetchScalarGridSpec(
            num_scalar_prefetch=8,
            grid=(n_items,),
            in_specs=[
                pl.BlockSpec((MOE_BLOCK, d // 2), xmap),
                pl.BlockSpec(memory_space=pl.ANY),
                pl.BlockSpec((1, 1, de2), wmap),
                pl.BlockSpec(memory_space=pl.ANY),
                pl.BlockSpec((1, 1, d), wmap),
            ],
            out_specs=pl.BlockSpec((MOE_BLOCK, d // 2), xmap),
            scratch_shapes=[
                pltpu.VMEM((2, d, de2), F32), pltpu.VMEM((2, de, d), F32),
                pltpu.SemaphoreType.DMA((2, 2)),
            ],
        ),
        out_shape=jax.ShapeDtypeStruct(xs.shape, xs.dtype),
        compiler_params=pltpu.CompilerParams(
            dimension_semantics=("arbitrary",), vmem_limit_bytes=VMEM_LIMIT),
        name="moe",
    )(*sched, xs, wgu, bgu, wd, bd)


def _moe_schedule(counts, n_rows):
    n_blocks = n_rows // MOE_BLOCK
    n_items = n_blocks + N_EXPERTS
    ends = jnp.cumsum(counts)
    starts = ends - counts
    first_blk = starts // MOE_BLOCK
    last_blk = (ends - 1) // MOE_BLOCK
    items_per = jnp.where(counts > 0, last_blk - first_blk + 1, 0)
    item_ends = jnp.cumsum(items_per)
    item_starts = item_ends - items_per
    total = item_ends[-1]
    it = jnp.arange(n_items, dtype=jnp.int32)
    live = it < total
    itc = jnp.minimum(it, total - 1)
    exp = jnp.sum((item_ends[None, :] <= itc[:, None]).astype(jnp.int32), axis=1)
    is_exp = exp[:, None] == jnp.arange(N_EXPERTS, dtype=jnp.int32)[None, :]
    pick = lambda table: jnp.sum(jnp.where(is_exp, table[None, :], 0), axis=1)
    blk = pick(first_blk) + itc - pick(item_starts)
    lo = jnp.clip(pick(starts) - blk * MOE_BLOCK, 0, MOE_BLOCK)
    hi = jnp.clip(pick(ends) - blk * MOE_BLOCK, 0, MOE_BLOCK)
    hi = jnp.where(live, hi, lo)
    prev_exp = jnp.concatenate([jnp.full((1,), -1, jnp.int32), exp[:-1]])
    prev_blk = jnp.concatenate([jnp.full((1,), -1, jnp.int32), blk[:-1]])
    new = jnp.logical_and(live, exp != prev_exp)
    init = jnp.logical_and(live, blk != prev_blk)
    slot = (jnp.cumsum(new.astype(jnp.int32)) - 1) % 2
    ar = jnp.arange(N_EXPERTS, dtype=jnp.int32)
    later = jnp.logical_and(counts[None, :] > 0, ar[None, :] > ar[:, None])
    next_exp = jnp.min(jnp.where(later, ar[None, :], N_EXPERTS), axis=1)
    next_exp = jnp.where(next_exp == N_EXPERTS, -1, next_exp)
    nxt = pick(next_exp)
    as_i32 = lambda a: a.astype(jnp.int32)
    return tuple(as_i32(a) for a in (blk, exp, lo, hi, new, init, nxt, slot)), starts


def _combine_kernel(*refs):
    yg_refs = refs[:TOP_K]
    x1_ref, gate_ref, mod_ref, fg_ref, o_ref = refs[TOP_K:]
    gates = gate_ref[...]
    y = gates[:, 0:1] * _unpack_bf16_pairs(yg_refs[0][...])
    for kk in range(1, TOP_K):
        y = y + gates[:, kk:kk + 1] * _unpack_bf16_pairs(yg_refs[kk][...])
    m = mod_ref[0]
    x2 = x1_ref[...] + m[5:6] * y
    o_ref[...] = _rms(x2) * fg_ref[...]


def _combine_call(yg, x1, gates, mod, fg, n_seq, part):
    t, d = x1.shape
    tt = COMBINE_TT
    tiles_per_seq = n_seq // tt
    n_tiles = yg.shape[0] // (TOP_K * tt)
    first = part * n_tiles
    tok = lambda i: (first + i, 0)
    slot_specs = [pl.BlockSpec((tt, d // 2), functools.partial(lambda kk, i: (kk * n_tiles + i, 0), kk))
                  for kk in range(TOP_K)]
    return pl.pallas_call(
        _combine_kernel,
        grid=(n_tiles,),
        in_specs=slot_specs + [
            pl.BlockSpec((tt, d), tok),
            pl.BlockSpec((tt, LANES), tok),
            pl.BlockSpec((1, N_MOD, d), lambda i: ((first + i) // tiles_per_seq, 0, 0)),
            pl.BlockSpec((1, d), lambda i: (0, 0)),
        ],
        out_specs=pl.BlockSpec((tt, d), tok),
        out_shape=jax.ShapeDtypeStruct((t, d), F32),
        input_output_aliases={TOP_K: 0},
        compiler_params=pltpu.CompilerParams(
            dimension_semantics=("arbitrary",), vmem_limit_bytes=VMEM_LIMIT),
        name="combine",
    )(*([yg] * TOP_K), x1, gates, mod, fg)


def _pad_cols(a, width):
    return jnp.pad(a, ((0, 0), (0, width - a.shape[1])))


def _rope_slab(w_rope):
    return jnp.pad(w_rope, ((0, 0), (QK_NOPE, LANES - QK_NOPE - QK_ROPE)))


def _prep_w_in(w_in):
    kr0 = Q_LORA + KV_LORA
    w_kr = w_in[:, kr0:kr0 + QK_ROPE]
    return jnp.concatenate(
        [w_in[:, :kr0], _rope_slab(w_kr), w_in[:, kr0 + QK_ROPE:]], axis=1).astype(BF16)


def _prep_w_uq(w_uq):
    per = QK_NOPE + QK_ROPE
    rows = w_uq.shape[0]
    w = w_uq.reshape(rows, N_HEADS, per)
    slab = jnp.pad(w, ((0, 0), (0, 0), (0, LANES - per)))
    return slab.reshape(rows, -1).T.astype(BF16)


def _prep_w_ukv(w_ukv):
    per = QK_NOPE + V_DIM
    rows = w_ukv.shape[0]
    w = w_ukv.reshape(rows, N_HEADS, per)
    wk = jnp.pad(w[:, :, :QK_NOPE], ((0, 0), (0, 0), (0, LANES - QK_NOPE))).reshape(rows, -1)
    wvt = jnp.pad(jnp.transpose(w[:, :, QK_NOPE:], (1, 2, 0)), ((0, 0), (0, VT_ROWS - V_DIM), (0, 0)))
    return wk.astype(BF16), wvt.reshape(N_HEADS * VT_ROWS, rows).astype(BF16)


def _rope_tables(n_lat):
    rows = n_lat // GRID_W
    nf = QK_ROPE // 4
    row = jnp.repeat(jnp.arange(rows, dtype=F32), GRID_W)
    col = jnp.tile(jnp.arange(GRID_W, dtype=F32), rows)
    freqs = ROPE_BASE ** (-jnp.arange(nf, dtype=F32) / nf)
    ang = jnp.concatenate([row[:, None] * freqs, col[:, None] * freqs], axis=-1)
    cos, sin = jnp.cos(ang), jnp.sin(ang)
    ones = jnp.ones((n_lat, QK_NOPE), F32)
    zeros = jnp.zeros((n_lat, QK_NOPE), F32)
    cs = _pad_cols(jnp.concatenate([ones, cos, cos], axis=1), LANES)
    sn = _pad_cols(jnp.concatenate([zeros, -sin, sin], axis=1), LANES)
    return cs, sn


def kernel(x, c, ctx, c_ctx, w_mod, b_mod, norm1_g, w_in, q_norm_g, kv_norm_g, w_uq, w_ukv, w_pool,
           pool_scale, w_out, norm2_g, router_w, router_b, w_gate_up, b_gate_up, w_down, b_down,
           final_g):
    bsz, n, d = x.shape
    n_ctx = ctx.shape[1]
    t = bsz * n
    l = 0

    cc = jnp.concatenate([c, c_ctx[None, :], jnp.zeros((8 - bsz - 1, d), F32)], axis=0)
    mod = _mod_call(cc, w_mod[l], b_mod[l][None, :]).reshape(8, N_MOD, d)
    mod_lat, mod_ctx = mod[:bsz], mod[bsz:bsz + 1]

    win = _prep_w_in(w_in[l])
    wuqt = _prep_w_uq(w_uq[l])
    wk, wvt = _prep_w_ukv(w_ukv[l])
    cs, sn = _rope_tables(n)
    cs_ctx = jnp.broadcast_to((jnp.arange(LANES) < QK_NOPE + QK_ROPE).astype(F32), (n_ctx, LANES))
    sn_ctx = jnp.zeros((n_ctx, LANES), F32)
    g1 = norm1_g[l][None, :]
    qg = q_norm_g[l][None, :]
    kvg = kv_norm_g[l][None, :]

    qt, k, vt, u = _front_call(x, mod_lat, True, g1, win, qg, kvg, wuqt, wk, wvt, cs, sn,
                               is_ctx=False, ts=FRONT_TS)
    kc, vct = _front_call(ctx, mod_ctx, False, g1, win[:, Q_LORA:Q_LORA + 2 * LANES], qg, kvg, wuqt,
                          wk, wvt, cs_ctx, sn_ctx, is_ctx=True, ts=n_ctx)
    att = _attn_call(qt, k, vt, kc, vct)

    rw = _pad_cols(router_w[l], LANES)
    rw_hi = rw.astype(BF16)
    rw = jnp.concatenate([rw_hi, (rw - rw_hi.astype(F32)).astype(BF16)], axis=1)
    rb = jnp.concatenate([router_b[l], jnp.full((LANES - N_EXPERTS,), -jnp.inf, F32)])[None, :]
    x1, h2, gates, ridx, cnt = _mix_call(
        att, u, x, mod_lat, w_pool[l].astype(BF16), pool_scale[l][None, :], w_out[l].astype(BF16),
        norm2_g[l][None, :], rw, rb)

    counts = cnt[0, :N_EXPERTS].astype(jnp.int32)
    sched, starts = _moe_schedule(counts, t * TOP_K)
    ridx = ridx.reshape(t, LANES)
    is_exp = ridx[:, :TOP_K, None] == jnp.arange(N_EXPERTS, dtype=jnp.int32)
    dest = jnp.sum(jnp.where(is_exp, starts, 0), axis=-1) + ridx[:, TOP_K:2 * TOP_K]
    dest_t = dest.T

    xs = _scatter_rows(h2.reshape(t, d // 2), dest_t.reshape(TOP_K, t // SC_ROWS, SC_ROWS), t * TOP_K)
    ys = _moe_call(sched, xs, w_gate_up[l], b_gate_up[l][:, None, :], w_down[l],
                   b_down[l][:, None, :])
    out = x1.reshape(t, d)
    gates = gates.reshape(t, LANES)
    tp = t // COMBINE_PARTS
    for part in range(COMBINE_PARTS):
        idx = dest_t[:, part * tp:(part + 1) * tp].reshape(-1)
        yg = _gather_rows(ys, idx)
        out = _combine_call(yg, out, gates, mod_lat, final_g[None, :], n, part)
    return out.reshape(bsz, n, d)
```

```python
import functools
import math

import jax
import jax.numpy as jnp
from jax import lax
from jax.experimental import pallas as pl
from jax.experimental.pallas import tpu as pltpu
from jax.experimental.pallas import tpu_sc as plsc

F32 = jnp.float32
BF16 = jnp.bfloat16
HIGHEST = lax.Precision.HIGHEST

N_HEADS = 8
QK_NOPE = 64
QK_ROPE = 32
V_DIM = 64
Q_LORA = 256
KV_LORA = 128
GRID_W = 64
ROPE_BASE = 10000.0
POOL_WINDOWS = (2, 4, 8, 16)
POOL_CH = 128
N_EXPERTS = 32
TOP_K = 4
SWIGLU_LIMIT = 7.0
SWIGLU_ALPHA = 1.702
MOE_BLOCK = 512
MOE_SUB = 128
N_MOD = 6
EPS = 1e-6

LANES = 128
POOL_HALO = 8
VT_ROWS = 80
VMEM_LIMIT = 56 * 1024 * 1024

FRONT_TS = 1024
ATTN_TQ = 512
ATTN_TILES = 4
ATTN_KC = 256
MIX_TS = 512
COMBINE_TT = 1024
COMBINE_PARTS = 4
SC_ROWS = 128


def _dot(a, b, **kw):
    return jnp.dot(a, b, preferred_element_type=F32, **kw)


def _dot_nt(a, b):
    return lax.dot_general(a, b, (((1,), (1,)), ((), ())), preferred_element_type=F32)


def _rms(x):
    return x * lax.rsqrt(jnp.mean(x * x, axis=-1, keepdims=True) + EPS)


def _pack_bf16_pairs(x):
    w = x.shape[1] // 2
    return pltpu.pack_elementwise([x[:, :w], x[:, w:]], packed_dtype=BF16)


def _unpack_bf16_pairs(words):
    halves = [pltpu.unpack_elementwise(words, index=h, packed_dtype=BF16, unpacked_dtype=F32)
              for h in range(2)]
    return jnp.concatenate(halves, axis=-1)


def _mod_kernel(c_ref, w_ref, b_ref, o_ref):
    c = c_ref[...]
    a = c / (1.0 + jnp.exp(-c))
    o_ref[...] = _dot(a, w_ref[...], precision=HIGHEST) + b_ref[...]


def _mod_call(cc, w_mod, b_mod):
    d = w_mod.shape[0]
    return pl.pallas_call(
        _mod_kernel,
        grid=(N_MOD,),
        in_specs=[
            pl.BlockSpec((8, d), lambda j: (0, 0)),
            pl.BlockSpec((d, d), lambda j: (0, j)),
            pl.BlockSpec((1, d), lambda j: (0, j)),
        ],
        out_specs=pl.BlockSpec((8, d), lambda j: (0, j)),
        out_shape=jax.ShapeDtypeStruct((8, N_MOD * d), F32),
        compiler_params=pltpu.CompilerParams(
            dimension_semantics=("arbitrary",), vmem_limit_bytes=VMEM_LIMIT),
        name="mod",
    )(cc, w_mod, b_mod)


def _front_kernel(x_ref, mod_ref, g1_ref, win_ref, qg_ref, kvg_ref, wuqt_ref, wk_ref, wvt_ref,
                  cs_ref, sn_ref, cst_ref, snt_ref, *out_refs, is_ctx, scale):
    x = x_ref[0]
    m = mod_ref[0]
    h = _rms(x) * g1_ref[...] * (1.0 + m[1:2]) + m[0:1]
    z = _dot(h.astype(BF16), win_ref[...])
    if is_ctx:
        k_ref, vt_ref = out_refs
        zkv = z
    else:
        qt_ref, k_ref, vt_ref, u_ref = out_refs
        zkv = z[:, Q_LORA:Q_LORA + 2 * LANES]
        u_ref[0] = z[:, Q_LORA + 2 * LANES:]

    ckv = _rms(zkv[:, :KV_LORA]) * kvg_ref[...]
    kk = _dot(ckv.astype(BF16), wk_ref[...])
    kra = zkv[:, LANES:2 * LANES]
    lane = lax.broadcasted_iota(jnp.int32, kra.shape, 1)
    half = QK_ROPE // 2
    partner = jnp.where(lane < QK_NOPE + half, pltpu.roll(kra, LANES - half, axis=1),
                        pltpu.roll(kra, half, axis=1))
    kr = kra * cs_ref[...] + partner * sn_ref[...]
    for hd in range(N_HEADS):
        k_ref[0, hd] = (kk[:, hd * LANES:(hd + 1) * LANES] + kr).astype(BF16)
    vt = _dot(wvt_ref[...], ckv.T.astype(BF16))
    row = lax.broadcasted_iota(jnp.int32, (VT_ROWS, 1), 0)
    ones = jnp.where(row == V_DIM, 1.0, 0.0)
    for hd in range(N_HEADS):
        vt_ref[0, hd] = (vt[hd * VT_ROWS:(hd + 1) * VT_ROWS] + ones).astype(BF16)

    if not is_ctx:
        cq = _rms(z[:, :Q_LORA]) * qg_ref[...]
        qqt = _dot(wuqt_ref[...], cq.T.astype(BF16))
        cst = cst_ref[...]
        snt = snt_ref[...]
        r1, r2, r3 = QK_NOPE, QK_NOPE + QK_ROPE // 2, QK_NOPE + QK_ROPE
        for hd in range(N_HEADS):
            qa = qqt[hd * LANES:(hd + 1) * LANES]
            qb = jnp.concatenate([qa[:r1], qa[r2:r3], qa[r1:r2], qa[r3:]], axis=0)
            qt_ref[0, hd] = ((qa * cst + qb * snt) * scale).astype(BF16)


def _front_call(xs, mod, mod_per_batch, g1, win, qg, kvg, wuqt, wk, wvt, cs, sn, *, is_ctx, ts):
    bsz, n, d = xs.shape
    nt = n // ts
    scale = math.log2(math.e) / math.sqrt(QK_NOPE + QK_ROPE)
    const = lambda b, i: (0, 0)
    mod_map = (lambda b, i: (b, 0, 0)) if mod_per_batch else (lambda b, i: (0, 0, 0))
    in_specs = [
        pl.BlockSpec((1, ts, d), lambda b, i: (b, i, 0)),
        pl.BlockSpec((1, N_MOD, d), mod_map),
        pl.BlockSpec((1, d), const),
        pl.BlockSpec(win.shape, const),
        pl.BlockSpec((1, Q_LORA), const),
        pl.BlockSpec((1, KV_LORA), const),
        pl.BlockSpec(wuqt.shape, const),
        pl.BlockSpec(wk.shape, const),
        pl.BlockSpec(wvt.shape, const),
        pl.BlockSpec((ts, LANES), lambda b, i: (i, 0)),
        pl.BlockSpec((ts, LANES), lambda b, i: (i, 0)),
        pl.BlockSpec((LANES, ts), lambda b, i: (0, i)),
        pl.BlockSpec((LANES, ts), lambda b, i: (0, i)),
    ]
    k_spec = pl.BlockSpec((1, N_HEADS, ts, LANES), lambda b, i: (b, 0, i, 0))
    k_shape = jax.ShapeDtypeStruct((bsz, N_HEADS, n, LANES), BF16)
    qt_spec = pl.BlockSpec((1, N_HEADS, LANES, ts), lambda b, i: (b, 0, 0, i))
    qt_shape = jax.ShapeDtypeStruct((bsz, N_HEADS, LANES, n), BF16)
    vt_spec = pl.BlockSpec((1, N_HEADS, VT_ROWS, ts), lambda b, i: (b, 0, 0, i))
    vt_shape = jax.ShapeDtypeStruct((bsz, N_HEADS, VT_ROWS, n), BF16)
    if is_ctx:
        out_specs = [k_spec, vt_spec]
        out_shape = [k_shape, vt_shape]
    else:
        pool_w = win.shape[1] - Q_LORA - 2 * LANES
        out_specs = [qt_spec, k_spec, vt_spec, pl.BlockSpec((1, ts, pool_w), lambda b, i: (b, i, 0))]
        out_shape = [qt_shape, k_shape, vt_shape, jax.ShapeDtypeStruct((bsz, n, pool_w), F32)]
    return pl.pallas_call(
        functools.partial(_front_kernel, is_ctx=is_ctx, scale=scale),
        grid=(bsz, nt),
        in_specs=in_specs,
        out_specs=out_specs,
        out_shape=out_shape,
        compiler_params=pltpu.CompilerParams(
            dimension_semantics=("arbitrary", "arbitrary"), vmem_limit_bytes=VMEM_LIMIT),
        name="front_ctx" if is_ctx else "front",
    )(xs, mod, g1, win, qg, kvg, wuqt, wk, wvt, cs, sn, cs.T, sn.T)


def _attn_kernel(qt_ref, k_ref, vt_ref, kc_ref, vct_ref, o_ref, s_ref, mlc_ref, mxb_ref, oe_ref):
    n_ctx = kc_ref.shape[2]
    n_lat = k_ref.shape[2]
    chunks = [(None, 0, n_ctx)] + [(c * ATTN_KC, n_ctx + c * ATTN_KC, ATTN_KC)
                                   for c in range(n_lat // ATTN_KC)]
    sub = 8

    def q_cols(tile):
        if isinstance(tile, int):
            return pl.ds(tile * ATTN_TQ, ATTN_TQ)
        return pl.ds(pl.multiple_of(tile * ATTN_TQ, ATTN_TQ), ATTN_TQ)

    def score_chunk(tile, hd, buf, ci):
        off, soff, w = chunks[ci]
        keys = kc_ref[0, hd] if off is None else k_ref[0, hd, off:off + w, :]
        s = _dot(keys, qt_ref[0, hd, :, q_cols(tile)])
        s_ref[buf, soff:soff + w, :] = s
        mx = s[0:sub]
        for r in range(1, w // sub):
            mx = jnp.maximum(mx, s[r * sub:(r + 1) * sub])
        mlc_ref[buf, ci] = mx

    def row_max(buf):
        mx = mlc_ref[buf, 0]
        for ci in range(1, len(chunks)):
            mx = jnp.maximum(mx, mlc_ref[buf, ci])
        mxb_ref[...] = jnp.broadcast_to(jnp.max(mx, axis=0, keepdims=True), mx.shape)

    def weight_chunk(hd, buf, ci):
        off, soff, w = chunks[ci]
        p = jnp.exp2(s_ref[buf, soff:soff + w, :] - mxb_ref[0:1, :]).astype(BF16)
        vt = vct_ref[0, hd] if off is None else vt_ref[0, hd, :, off:off + w]
        return _dot(vt, p)

    def stage(hw, bw, scoring, bs):
        row_max(bw)
        acc = None
        for ci in range(len(chunks)):
            if scoring is not None:
                score_chunk(*scoring, bs, ci)
            part = weight_chunk(hw, bw, ci)
            acc = part if acc is None else acc + part
        return acc[0:V_DIM] / acc[V_DIM:V_DIM + 1]

    def write_pair(tile, j, ot_odd):
        pair_t = jnp.concatenate([oe_ref[...], ot_odd], axis=0)
        o_ref[0, j, q_cols(tile), :] = pair_t.T.astype(BF16)

    pairs_per_tile = N_HEADS // 2
    n_pairs = (qt_ref.shape[3] // ATTN_TQ) * pairs_per_tile

    def split(p):
        return p // pairs_per_tile, p % pairs_per_tile

    for ci in range(len(chunks)):
        score_chunk(0, 0, 0, ci)

    def head_pair(p, carry):
        tile, j = split(p)
        nxt_tile, nxt_j = split(p + 1)
        oe_ref[...] = stage(2 * j, 0, (tile, 2 * j + 1), 1)
        write_pair(tile, j, stage(2 * j + 1, 1, (nxt_tile, 2 * nxt_j), 0))
        return carry

    lax.fori_loop(0, n_pairs - 1, head_pair, 0)
    tile, j = split(n_pairs - 1)
    oe_ref[...] = stage(2 * j, 0, (tile, 2 * j + 1), 1)
    write_pair(tile, j, stage(2 * j + 1, 1, None, None))


def _attn_call(qt, k, vt, kc, vct):
    bsz, _, _, n = qt.shape
    n_ctx = kc.shape[2]
    tq = ATTN_TQ
    n_chunks = 1 + n // ATTN_KC
    per_batch = lambda b, i: (b, 0, 0, 0)
    resident = dict(pipeline_mode=pl.Buffered(1))
    rows_per_step = ATTN_TILES * tq
    return pl.pallas_call(
        _attn_kernel,
        grid=(bsz, n // rows_per_step),
        in_specs=[
            pl.BlockSpec((1, N_HEADS, LANES, rows_per_step), lambda b, i: (b, 0, 0, i)),
            pl.BlockSpec((1, N_HEADS, n, LANES), per_batch, **resident),
            pl.BlockSpec((1, N_HEADS, VT_ROWS, n), per_batch, **resident),
            pl.BlockSpec((1, N_HEADS, n_ctx, LANES), per_batch, **resident),
            pl.BlockSpec((1, N_HEADS, VT_ROWS, n_ctx), per_batch, **resident),
        ],
        out_specs=pl.BlockSpec((1, N_HEADS // 2, rows_per_step, LANES), lambda b, i: (b, 0, i, 0)),
        out_shape=jax.ShapeDtypeStruct((bsz, N_HEADS // 2, n, LANES), BF16),
        scratch_shapes=[
            pltpu.VMEM((2, n_ctx + n, tq), F32),
            pltpu.VMEM((2, n_chunks, 8, tq), F32),
            pltpu.VMEM((8, tq), F32),
            pltpu.VMEM((V_DIM, tq), F32),
        ],
        compiler_params=pltpu.CompilerParams(
            dimension_semantics=("arbitrary", "arbitrary"), vmem_limit_bytes=VMEM_LIMIT),
        name="attn",
    )(qt, k, vt, kc, vct)


def _mix_kernel(att_ref, u_ref, up_ref, un_ref, x_ref, mod_ref, wpool_ref, pscale_ref, wout_ref,
                g2_ref, rw_ref, rb_ref,
                x1_ref, h2_ref, gate_ref, ridx_ref, cnt_ref, ue_ref, base_ref, *, n_seq):
    b = pl.program_id(0)
    i = pl.program_id(1)
    n_tiles = pl.num_programs(1)
    ts = u_ref.shape[1]

    @pl.when(jnp.logical_and(b == 0, i == 0))
    def _():
        base_ref[...] = jnp.zeros_like(base_ref)

    u = u_ref[0]
    ue_ref[0:POOL_HALO] = jnp.where(i > 0, up_ref[0], 0.0)
    ue_ref[POOL_HALO:POOL_HALO + ts] = u
    ue_ref[POOL_HALO + ts:2 * POOL_HALO + ts] = jnp.where(i < n_tiles - 1, un_ref[0], 0.0)
    t = i * ts + lax.broadcasted_iota(jnp.int32, (ts, 1), 0)
    ys = []
    for g, w in enumerate(POOL_WINDOWS):
        half = w // 2
        lanes = slice(g * POOL_CH, (g + 1) * POOL_CH)
        ws = ue_ref[POOL_HALO - half:POOL_HALO - half + ts, lanes]
        for jj in range(-half + 1, half):
            ws = ws + ue_ref[POOL_HALO + jj:POOL_HALO + jj + ts, lanes]
        count = (jnp.minimum(t + half, n_seq) - jnp.maximum(t - half, 0)).astype(F32)
        mixed = (ws / count - u[:, lanes]).astype(BF16)
        ys.append(_dot(mixed, wpool_ref[g]))
    pool = jnp.concatenate(ys, axis=-1) * pscale_ref[...]

    cat = jnp.concatenate([att_ref[0, j] for j in range(N_HEADS // 2)] + [pool.astype(BF16)], axis=-1)
    m = mod_ref[0]
    x1 = x_ref[0] + m[2:3] * _dot(cat, wout_ref[...])
    x1_ref[0] = x1
    h2 = _rms(x1) * g2_ref[...] * (1.0 + m[4:5]) + m[3:4]
    h2_ref[0] = _pack_bf16_pairs(h2)

    h_hi = h2.astype(BF16)
    h_lo = (h2 - h_hi.astype(F32)).astype(BF16)
    hi_part = _dot(h_hi, rw_ref[...])
    logits = (hi_part[:, :LANES] + hi_part[:, LANES:] + _dot(h_lo, rw_ref[:, :LANES])) + rb_ref[...]
    lane = lax.broadcasted_iota(jnp.int32, logits.shape, 1).astype(F32)
    vals, idxs = [], []
    for _k in range(TOP_K):
        mv = jnp.max(logits, axis=-1, keepdims=True)
        ix = jnp.min(jnp.where(logits == mv, lane, float(LANES)), axis=-1, keepdims=True)
        vals.append(mv)
        idxs.append(ix)
        logits = jnp.where(lane == ix, -jnp.inf, logits)
    es = [jnp.exp(v - vals[0]) for v in vals]
    den = es[0] + es[1] + es[2] + es[3]

    onehot = jnp.zeros(lane.shape, F32)
    for ix in idxs:
        onehot = onehot + jnp.where(lane == ix, 1.0, 0.0)
    row = lax.broadcasted_iota(jnp.int32, (ts, ts), 0)
    col = lax.broadcasted_iota(jnp.int32, (ts, ts), 1)
    tri = jnp.where(col < row, 1.0, 0.0).astype(BF16)
    before = _dot(tri, onehot.astype(BF16)) + base_ref[0:1, :]
    base_new = base_ref[0:1, :] + jnp.sum(onehot, axis=0, keepdims=True)
    base_ref[...] = jnp.broadcast_to(base_new, base_ref.shape)
    cnt_ref[...] = jnp.broadcast_to(base_new, cnt_ref.shape)

    gate_out = jnp.zeros(lane.shape, F32)
    ridx_out = jnp.zeros(lane.shape, F32)
    for kk in range(TOP_K):
        rank = jnp.sum(jnp.where(lane == idxs[kk], before, 0.0), axis=-1, keepdims=True)
        gate_out = jnp.where(lane == float(kk), es[kk] / den, gate_out)
        ridx_out = jnp.where(lane == float(kk), idxs[kk], ridx_out)
        ridx_out = jnp.where(lane == float(TOP_K + kk), rank, ridx_out)
    gate_ref[0] = gate_out
    ridx_ref[0] = ridx_out.astype(jnp.int32)


def _mix_call(att, u, x, mod, wpool, pscale, wout, g2, rw, rb):
    bsz, n, d = x.shape
    ts = MIX_TS
    nt = n // ts
    pool_w = u.shape[2]
    hb = ts // POOL_HALO
    n_halo_blocks = n // POOL_HALO
    const2 = lambda b, i: (0, 0)
    tok = lambda b, i: (b, i, 0)
    return pl.pallas_call(
        functools.partial(_mix_kernel, n_seq=n),
        grid=(bsz, nt),
        in_specs=[
            pl.BlockSpec((1, N_HEADS // 2, ts, LANES), lambda b, i: (b, 0, i, 0)),
            pl.BlockSpec((1, ts, pool_w), tok),
            pl.BlockSpec((1, POOL_HALO, pool_w), lambda b, i: (b, jnp.maximum(i * hb - 1, 0), 0)),
            pl.BlockSpec((1, POOL_HALO, pool_w),
                         lambda b, i: (b, jnp.minimum((i + 1) * hb, n_halo_blocks - 1), 0)),
            pl.BlockSpec((1, ts, d), tok),
            pl.BlockSpec((1, N_MOD, d), lambda b, i: (b, 0, 0)),
            pl.BlockSpec(wpool.shape, lambda b, i: (0, 0, 0)),
            pl.BlockSpec((1, pool_w), const2),
            pl.BlockSpec(wout.shape, const2),
            pl.BlockSpec((1, d), const2),
            pl.BlockSpec(rw.shape, const2),
            pl.BlockSpec((1, LANES), const2),
        ],
        out_specs=[
            pl.BlockSpec((1, ts, d), tok),
            pl.BlockSpec((1, ts, d // 2), tok),
            pl.BlockSpec((1, ts, LANES), tok),
            pl.BlockSpec((1, ts, LANES), tok),
            pl.BlockSpec((8, LANES), const2),
        ],
        out_shape=[
            jax.ShapeDtypeStruct((bsz, n, d), F32),
            jax.ShapeDtypeStruct((bsz, n, d // 2), jnp.uint32),
            jax.ShapeDtypeStruct((bsz, n, LANES), F32),
            jax.ShapeDtypeStruct((bsz, n, LANES), jnp.int32),
            jax.ShapeDtypeStruct((8, LANES), F32),
        ],
        scratch_shapes=[
            pltpu.VMEM((ts + 2 * POOL_HALO, pool_w), F32),
            pltpu.VMEM((8, LANES), F32),
        ],
        compiler_params=pltpu.CompilerParams(
            dimension_semantics=("arbitrary", "arbitrary"), vmem_limit_bytes=VMEM_LIMIT),
        name="mix",
    )(att, u, u, u, x, mod, wpool, pscale, wout, g2, rw, rb)


def _sc_mesh():
    return plsc.VectorSubcoreMesh(core_axis_name="core", subcore_axis_name="subcore")


def _sc_worker_id():
    info = plsc.get_sparse_core_info()
    return lax.axis_index("subcore") * info.num_cores + lax.axis_index("core")


def _sc_num_workers():
    info = plsc.get_sparse_core_info()
    return info.num_cores * info.num_subcores


def _scatter_rows(x, dest3, n_out):
    t, d = x.shape
    top_k, n_chunks, rows = dest3.shape
    per_worker = n_chunks // _sc_num_workers()

    @functools.partial(
        pl.kernel,
        out_type=jax.ShapeDtypeStruct((n_out, d), x.dtype),
        mesh=_sc_mesh(),
        scratch_types=[pltpu.VMEM((top_k, rows), jnp.int32), pltpu.VMEM((rows, d), x.dtype)],
        name="sc_scatter",
    )
    def scatter(x_hbm, i_hbm, o_hbm, idx_v, rows_v):
        wid = _sc_worker_id()

        @pl.loop(0, per_worker)
        def _(c):
            chunk = wid * per_worker + c
            for kk in range(top_k):
                pltpu.sync_copy(i_hbm.at[kk, chunk], idx_v.at[kk])
            pltpu.sync_copy(x_hbm.at[pl.ds(pl.multiple_of(chunk * rows, 8), rows)], rows_v)
            for kk in range(top_k):
                pltpu.sync_copy(rows_v, o_hbm.at[idx_v.at[kk]])

    return scatter(x, dest3)


def _gather_rows(y, idx):
    n = idx.shape[0]
    d = y.shape[1]
    rows = SC_ROWS
    per_worker = n // _sc_num_workers()
    n_chunks = per_worker // rows

    @functools.partial(
        pl.kernel,
        out_type=jax.ShapeDtypeStruct((n, d), y.dtype),
        mesh=_sc_mesh(),
        scratch_types=[pltpu.VMEM((rows,), jnp.int32), pltpu.VMEM((rows, d), y.dtype)],
        name="sc_gather",
    )
    def gather(y_hbm, i_hbm, o_hbm, idx_v, rows_v):
        base = _sc_worker_id() * per_worker

        @pl.loop(0, n_chunks)
        def _(c):
            off = pl.multiple_of(base + c * rows, 8)
            pltpu.sync_copy(i_hbm.at[pl.ds(off, rows)], idx_v)
            pltpu.sync_copy(y_hbm.at[idx_v], rows_v)
            pltpu.sync_copy(rows_v, o_hbm.at[pl.ds(off, rows)])

    return gather(y, idx)


def _moe_kernel(blk_ref, exp_ref, lo_ref, hi_ref, new_ref, init_ref, next_ref, slot_ref, xs_ref,
                wgu_hbm, bgu_ref, wd_hbm, bd_ref, ys_ref, wgu_buf, wd_buf, sems):
    i = pl.program_id(0)
    slot = slot_ref[i]

    def weight_copies(e, s):
        return (pltpu.make_async_copy(wgu_hbm.at[e], wgu_buf.at[s], sems.at[s, 0]),
                pltpu.make_async_copy(wd_hbm.at[e], wd_buf.at[s], sems.at[s, 1]))

    @pl.when(i == 0)
    def _():
        for cp in weight_copies(exp_ref[0], slot):
            cp.start()

    @pl.when(new_ref[i] == 1)
    def _():
        for cp in weight_copies(exp_ref[i], slot):
            cp.wait()

        @pl.when(next_ref[i] >= 0)
        def _():
            for cp in weight_copies(next_ref[i], 1 - slot):
                cp.start()

    @pl.when(init_ref[i] == 1)
    def _():
        ys_ref[...] = jnp.zeros_like(ys_ref)

    lo = lo_ref[i]
    hi = hi_ref[i]

    def expert_rows(r0, n_rows):
        de = wd_buf.shape[1]
        x = _unpack_bf16_pairs(xs_ref[pl.ds(r0, n_rows), :])
        gu = _dot(x, wgu_buf[slot]) + bgu_ref[0]
        g = jnp.minimum(gu[:, :de], SWIGLU_LIMIT)
        lin = jnp.clip(gu[:, de:], -SWIGLU_LIMIT, SWIGLU_LIMIT)
        act = g / (1.0 + jnp.exp(-SWIGLU_ALPHA * g)) * (lin + 1.0)
        y = _dot(act, wd_buf[slot]) + bd_ref[0]
        row = r0 + lax.broadcasted_iota(jnp.int32, (n_rows, 1), 0)
        mine = jnp.logical_and(row >= lo, row < hi)
        ys_ref[pl.ds(r0, n_rows), :] = jnp.where(mine, _pack_bf16_pairs(y), ys_ref[pl.ds(r0, n_rows), :])

    first = lo // MOE_SUB
    pieces = jnp.where(hi > lo, (hi + MOE_SUB - 1) // MOE_SUB - first, 0)
    for cnt in range(1, MOE_BLOCK // MOE_SUB + 1):
        @pl.when(pieces == cnt)
        def _(cnt=cnt):
            r0 = 0 if cnt * MOE_SUB == MOE_BLOCK else pl.multiple_of(first * MOE_SUB, MOE_SUB)
            expert_rows(r0, cnt * MOE_SUB)


def _moe_call(sched, xs, wgu, bgu, wd, bd):
    n_items = sched[0].shape[0]
    _, d, de2 = wgu.shape
    de = de2 // 2
    wmap = lambda i, blk, exp, lo, hi, new, init, nxt, slot: (exp[i], 0, 0)
    xmap = lambda i, blk, exp, lo, hi, new, init, nxt, slot: (blk[i], 0)
    return pl.pallas_call(
        _moe_kernel,
        grid_spec=pltpu.PrefetchScalarGridSpec(
            num_scalar_prefetch=8,
            grid=(n_items,),
            in_specs=[
                pl.BlockSpec((MOE_BLOCK, d // 2), xmap),
                pl.BlockSpec(memory_space=pl.ANY),
                pl.BlockSpec((1, 1, de2), wmap),
                pl.BlockSpec(memory_space=pl.ANY),
                pl.BlockSpec((1, 1, d), wmap),
            ],
            out_specs=pl.BlockSpec((MOE_BLOCK, d // 2), xmap),
            scratch_shapes=[
                pltpu.VMEM((2, d, de2), F32), pltpu.VMEM((2, de, d), F32),
                pltpu.SemaphoreType.DMA((2, 2)),
            ],
        ),
        out_shape=jax.ShapeDtypeStruct(xs.shape, xs.dtype),
        compiler_params=pltpu.CompilerParams(
            dimension_semantics=("arbitrary",), vmem_limit_bytes=VMEM_LIMIT),
        name="moe",
    )(*sched, xs, wgu, bgu, wd, bd)


def _moe_schedule(counts, n_rows):
    n_blocks = n_rows // MOE_BLOCK
    n_items = n_blocks + N_EXPERTS
    ends = jnp.cumsum(counts)
    starts = ends - counts
    first_blk = starts // MOE_BLOCK
    last_blk = (ends - 1) // MOE_BLOCK
    items_per = jnp.where(counts > 0, last_blk - first_blk + 1, 0)
    item_ends = jnp.cumsum(items_per)
    item_starts = item_ends - items_per
    total = item_ends[-1]
    it = jnp.arange(n_items, dtype=jnp.int32)
    live = it < total
    itc = jnp.minimum(it, total - 1)
    exp = jnp.sum((item_ends[None, :] <= itc[:, None]).astype(jnp.int32), axis=1)
    is_exp = exp[:, None] == jnp.arange(N_EXPERTS, dtype=jnp.int32)[None, :]
    pick = lambda table: jnp.sum(jnp.where(is_exp, table[None, :], 0), axis=1)
    blk = pick(first_blk) + itc - pick(item_starts)
    lo = jnp.clip(pick(starts) - blk * MOE_BLOCK, 0, MOE_BLOCK)
    hi = jnp.clip(pick(ends) - blk * MOE_BLOCK, 0, MOE_BLOCK)
    hi = jnp.where(live, hi, lo)
    prev_exp = jnp.concatenate([jnp.full((1,), -1, jnp.int32), exp[:-1]])
    prev_blk = jnp.concatenate([jnp.full((1,), -1, jnp.int32), blk[:-1]])
    new = jnp.logical_and(live, exp != prev_exp)
    init = jnp.logical_and(live, blk != prev_blk)
    slot = (jnp.cumsum(new.astype(jnp.int32)) - 1) % 2
    ar = jnp.arange(N_EXPERTS, dtype=jnp.int32)
    later = jnp.logical_and(counts[None, :] > 0, ar[None, :] > ar[:, None])
    next_exp = jnp.min(jnp.where(later, ar[None, :], N_EXPERTS), axis=1)
    next_exp = jnp.where(next_exp == N_EXPERTS, -1, next_exp)
    nxt = pick(next_exp)
    as_i32 = lambda a: a.astype(jnp.int32)
    return tuple(as_i32(a) for a in (blk, exp, lo, hi, new, init, nxt, slot)), starts


def _combine_kernel(*refs):
    yg_refs = refs[:TOP_K]
    x1_ref, gate_ref, mod_ref, fg_ref, o_ref = refs[TOP_K:]
    gates = gate_ref[...]
    y = gates[:, 0:1] * _unpack_bf16_pairs(yg_refs[0][...])
    for kk in range(1, TOP_K):
        y = y + gates[:, kk:kk + 1] * _unpack_bf16_pairs(yg_refs[kk][...])
    m = mod_ref[0]
    x2 = x1_ref[...] + m[5:6] * y
    o_ref[...] = _rms(x2) * fg_ref[...]


def _combine_call(yg, x1, gates, mod, fg, n_seq, part):
    t, d = x1.shape
    tt = COMBINE_TT
    tiles_per_seq = n_seq // tt
    n_tiles = yg.shape[0] // (TOP_K * tt)
    first = part * n_tiles
    tok = lambda i: (first + i, 0)
    slot_specs = [pl.BlockSpec((tt, d // 2), functools.partial(lambda kk, i: (kk * n_tiles + i, 0), kk))
                  for kk in range(TOP_K)]
    return pl.pallas_call(
        _combine_kernel,
        grid=(n_tiles,),
        in_specs=slot_specs + [
            pl.BlockSpec((tt, d), tok),
            pl.BlockSpec((tt, LANES), tok),
            pl.BlockSpec((1, N_MOD, d), lambda i: ((first + i) // tiles_per_seq, 0, 0)),
            pl.BlockSpec((1, d), lambda i: (0, 0)),
        ],
        out_specs=pl.BlockSpec((tt, d), tok),
        out_shape=jax.ShapeDtypeStruct((t, d), F32),
        input_output_aliases={TOP_K: 0},
        compiler_params=pltpu.CompilerParams(
            dimension_semantics=("arbitrary",), vmem_limit_bytes=VMEM_LIMIT),
        name="combine",
    )(*([yg] * TOP_K), x1, gates, mod, fg)


def _pad_cols(a, width):
    return jnp.pad(a, ((0, 0), (0, width - a.shape[1])))


def _rope_slab(w_rope):
    return jnp.pad(w_rope, ((0, 0), (QK_NOPE, LANES - QK_NOPE - QK_ROPE)))


def _prep_w_in(w_in):
    kr0 = Q_LORA + KV_LORA
    w_kr = w_in[:, kr0:kr0 + QK_ROPE]
    return jnp.concatenate(
        [w_in[:, :kr0], _rope_slab(w_kr), w_in[:, kr0 + QK_ROPE:]], axis=1).astype(BF16)


def _prep_w_uq(w_uq):
    per = QK_NOPE + QK_ROPE
    rows = w_uq.shape[0]
    w = w_uq.reshape(rows, N_HEADS, per)
    slab = jnp.pad(w, ((0, 0), (0, 0), (0, LANES - per)))
    return slab.reshape(rows, -1).T.astype(BF16)


def _prep_w_ukv(w_ukv):
    per = QK_NOPE + V_DIM
    rows = w_ukv.shape[0]
    w = w_ukv.reshape(rows, N_HEADS, per)
    wk = jnp.pad(w[:, :, :QK_NOPE], ((0, 0), (0, 0), (0, LANES - QK_NOPE))).reshape(rows, -1)
    wvt = jnp.pad(jnp.transpose(w[:, :, QK_NOPE:], (1, 2, 0)), ((0, 0), (0, VT_ROWS - V_DIM), (0, 0)))
    return wk.astype(BF16), wvt.reshape(N_HEADS * VT_ROWS, rows).astype(BF16)


def _rope_tables(n_lat):
    rows = n_lat // GRID_W
    nf = QK_ROPE // 4
    row = jnp.repeat(jnp.arange(rows, dtype=F32), GRID_W)
    col = jnp.tile(jnp.arange(GRID_W, dtype=F32), rows)
    freqs = ROPE_BASE ** (-jnp.arange(nf, dtype=F32) / nf)
    ang = jnp.concatenate([row[:, None] * freqs, col[:, None] * freqs], axis=-1)
    cos, sin = jnp.cos(ang), jnp.sin(ang)
    ones = jnp.ones((n_lat, QK_NOPE), F32)
    zeros = jnp.zeros((n_lat, QK_NOPE), F32)
    cs = _pad_cols(jnp.concatenate([ones, cos, cos], axis=1), LANES)
    sn = _pad_cols(jnp.concatenate([zeros, -sin, sin], axis=1), LANES)
    return cs, sn


def kernel(x, c, ctx, c_ctx, w_mod, b_mod, norm1_g, w_in, q_norm_g, kv_norm_g, w_uq, w_ukv, w_pool,
           pool_scale, w_out, norm2_g, router_w, router_b, w_gate_up, b_gate_up, w_down, b_down,
           final_g):
    bsz, n, d = x.shape
    n_ctx = ctx.shape[1]
    t = bsz * n
    l = 0

    cc = jnp.concatenate([c, c_ctx[None, :], jnp.zeros((8 - bsz - 1, d), F32)], axis=0)
    mod = _mod_call(cc, w_mod[l], b_mod[l][None, :]).reshape(8, N_MOD, d)
    mod_lat, mod_ctx = mod[:bsz], mod[bsz:bsz + 1]

    win = _prep_w_in(w_in[l])
    wuqt = _prep_w_uq(w_uq[l])
    wk, wvt = _prep_w_ukv(w_ukv[l])
    cs, sn = _rope_tables(n)
    cs_ctx = jnp.broadcast_to((jnp.arange(LANES) < QK_NOPE + QK_ROPE).astype(F32), (n_ctx, LANES))
    sn_ctx = jnp.zeros((n_ctx, LANES), F32)
    g1 = norm1_g[l][None, :]
    qg = q_norm_g[l][None, :]
    kvg = kv_norm_g[l][None, :]

    qt, k, vt, u = _front_call(x, mod_lat, True, g1, win, qg, kvg, wuqt, wk, wvt, cs, sn,
                               is_ctx=False, ts=FRONT_TS)
    kc, vct = _front_call(ctx, mod_ctx, False, g1, win[:, Q_LORA:Q_LORA + 2 * LANES], qg, kvg, wuqt,
                          wk, wvt, cs_ctx, sn_ctx, is_ctx=True, ts=n_ctx)
    att = _attn_call(qt, k, vt, kc, vct)

    rw = _pad_cols(router_w[l], LANES)
    rw_hi = rw.astype(BF16)
    rw = jnp.concatenate([rw_hi, (rw - rw_hi.astype(F32)).astype(BF16)], axis=1)
    rb = jnp.concatenate([router_b[l], jnp.full((LANES - N_EXPERTS,), -jnp.inf, F32)])[None, :]
    x1, h2, gates, ridx, cnt = _mix_call(
        att, u, x, mod_lat, w_pool[l].astype(BF16), pool_scale[l][None, :], w_out[l].astype(BF16),
        norm2_g[l][None, :], rw, rb)

    counts = cnt[0, :N_EXPERTS].astype(jnp.int32)
    sched, starts = _moe_schedule(counts, t * TOP_K)
    ridx = ridx.reshape(t, LANES)
    is_exp = ridx[:, :TOP_K, None] == jnp.arange(N_EXPERTS, dtype=jnp.int32)
    dest = jnp.sum(jnp.where(is_exp, starts, 0), axis=-1) + ridx[:, TOP_K:2 * TOP_K]
    dest_t = dest.T

    xs = _scatter_rows(h2.reshape(t, d // 2), dest_t.reshape(TOP_K, t // SC_ROWS, SC_ROWS), t * TOP_K)
    ys = _moe_call(sched, xs, w_gate_up[l], b_gate_up[l][:, None, :], w_down[l],
                   b_down[l][:, None, :])
    out = x1.reshape(t, d)
    gates = gates.reshape(t, LANES)
    tp = t // COMBINE_PARTS
    for part in range(COMBINE_PARTS):
        idx = dest_t[:, part * tp:(part + 1) * tp].reshape(-1)
        yg = _gather_rows(ys, idx)
        out = _combine_call(yg, out, gates, mod_lat, final_g[None, :], n, part)
    return out.reshape(bsz, n, d)
```

```python
import functools
import math

import jax
import jax.numpy as jnp
from jax import lax
from jax.experimental import pallas as pl
from jax.experimental.pallas import tpu as pltpu
from jax.experimental.pallas import tpu_sc as plsc

F32 = jnp.float32
BF16 = jnp.bfloat16
HIGHEST = lax.Precision.HIGHEST

N_HEADS = 8
QK_NOPE = 64
QK_ROPE = 32
V_DIM = 64
Q_LORA = 256
KV_LORA = 128
GRID_W = 64
ROPE_BASE = 10000.0
POOL_WINDOWS = (2, 4, 8, 16)
POOL_CH = 128
N_EXPERTS = 32
TOP_K = 4
SWIGLU_LIMIT = 7.0
SWIGLU_ALPHA = 1.702
MOE_BLOCK = 512
MOE_SUB = 128
N_MOD = 6
EPS = 1e-6

LANES = 128
POOL_HALO = 8
VT_ROWS = 80
VMEM_LIMIT = 56 * 1024 * 1024

FRONT_TS = 1024
ATTN_TQ = 512
ATTN_TILES = 4
ATTN_KC = 128
MIX_TS = 512
COMBINE_TT = 512
COMBINE_PARTS = 4
SC_ROWS = 128


def _dot(a, b, **kw):
    return jnp.dot(a, b, preferred_element_type=F32, **kw)


def _dot_nt(a, b):
    return lax.dot_general(a, b, (((1,), (1,)), ((), ())), preferred_element_type=F32)


def _rms(x):
    return x * lax.rsqrt(jnp.mean(x * x, axis=-1, keepdims=True) + EPS)


def _pack_bf16_pairs(x):
    w = x.shape[1] // 2
    return pltpu.pack_elementwise([x[:, :w], x[:, w:]], packed_dtype=BF16)


def _unpack_bf16_pairs(words):
    halves = [pltpu.unpack_elementwise(words, index=h, packed_dtype=BF16, unpacked_dtype=F32)
              for h in range(2)]
    return jnp.concatenate(halves, axis=-1)


def _mod_kernel(c_ref, w_ref, b_ref, o_ref):
    c = c_ref[...]
    a = c / (1.0 + jnp.exp(-c))
    o_ref[...] = _dot(a, w_ref[...], precision=HIGHEST) + b_ref[...]


def _mod_call(cc, w_mod, b_mod):
    d = w_mod.shape[0]
    return pl.pallas_call(
        _mod_kernel,
        grid=(N_MOD,),
        in_specs=[
            pl.BlockSpec((8, d), lambda j: (0, 0)),
            pl.BlockSpec((d, d), lambda j: (0, j)),
            pl.BlockSpec((1, d), lambda j: (0, j)),
        ],
        out_specs=pl.BlockSpec((8, d), lambda j: (0, j)),
        out_shape=jax.ShapeDtypeStruct((8, N_MOD * d), F32),
        compiler_params=pltpu.CompilerParams(
            dimension_semantics=("arbitrary",), vmem_limit_bytes=VMEM_LIMIT),
        name="mod",
    )(cc, w_mod, b_mod)


def _front_kernel(x_ref, mod_ref, g1_ref, win_ref, qg_ref, kvg_ref, wuqt_ref, wk_ref, wvt_ref,
                  cs_ref, sn_ref, cst_ref, snt_ref, *out_refs, is_ctx, scale):
    x = x_ref[0]
    m = mod_ref[0]
    h = _rms(x) * g1_ref[...] * (1.0 + m[1:2]) + m[0:1]
    z = _dot(h.astype(BF16), win_ref[...])
    if is_ctx:
        k_ref, vt_ref = out_refs
        zkv = z
    else:
        qt_ref, k_ref, vt_ref, u_ref = out_refs
        zkv = z[:, Q_LORA:Q_LORA + 2 * LANES]
        u_ref[0] = z[:, Q_LORA + 2 * LANES:]

    ckv = _rms(zkv[:, :KV_LORA]) * kvg_ref[...]
    kk = _dot(ckv.astype(BF16), wk_ref[...])
    kra = zkv[:, LANES:2 * LANES]
    lane = lax.broadcasted_iota(jnp.int32, kra.shape, 1)
    half = QK_ROPE // 2
    partner = jnp.where(lane < QK_NOPE + half, pltpu.roll(kra, LANES - half, axis=1),
                        pltpu.roll(kra, half, axis=1))
    kr = kra * cs_ref[...] + partner * sn_ref[...]
    for hd in range(N_HEADS):
        k_ref[0, hd] = (kk[:, hd * LANES:(hd + 1) * LANES] + kr).astype(BF16)
    vt = _dot(wvt_ref[...], ckv.T.astype(BF16))
    row = lax.broadcasted_iota(jnp.int32, (VT_ROWS, 1), 0)
    ones = jnp.where(row == V_DIM, 1.0, 0.0)
    for hd in range(N_HEADS):
        vt_ref[0, hd] = (vt[hd * VT_ROWS:(hd + 1) * VT_ROWS] + ones).astype(BF16)

    if not is_ctx:
        cq = _rms(z[:, :Q_LORA]) * qg_ref[...]
        qqt = _dot(wuqt_ref[...], cq.T.astype(BF16))
        cst = cst_ref[...]
        snt = snt_ref[...]
        r1, r2, r3 = QK_NOPE, QK_NOPE + QK_ROPE // 2, QK_NOPE + QK_ROPE
        for hd in range(N_HEADS):
            qa = qqt[hd * LANES:(hd + 1) * LANES]
            qb = jnp.concatenate([qa[:r1], qa[r2:r3], qa[r1:r2], qa[r3:]], axis=0)
            qt_ref[0, hd] = ((qa * cst + qb * snt) * scale).astype(BF16)


def _front_call(xs, mod, mod_per_batch, g1, win, qg, kvg, wuqt, wk, wvt, cs, sn, *, is_ctx, ts):
    bsz, n, d = xs.shape
    nt = n // ts
    scale = math.log2(math.e) / math.sqrt(QK_NOPE + QK_ROPE)
    const = lambda b, i: (0, 0)
    mod_map = (lambda b, i: (b, 0, 0)) if mod_per_batch else (lambda b, i: (0, 0, 0))
    in_specs = [
        pl.BlockSpec((1, ts, d), lambda b, i: (b, i, 0)),
        pl.BlockSpec((1, N_MOD, d), mod_map),
        pl.BlockSpec((1, d), const),
        pl.BlockSpec(win.shape, const),
        pl.BlockSpec((1, Q_LORA), const),
        pl.BlockSpec((1, KV_LORA), const),
        pl.BlockSpec(wuqt.shape, const),
        pl.BlockSpec(wk.shape, const),
        pl.BlockSpec(wvt.shape, const),
        pl.BlockSpec((ts, LANES), lambda b, i: (i, 0)),
        pl.BlockSpec((ts, LANES), lambda b, i: (i, 0)),
        pl.BlockSpec((LANES, ts), lambda b, i: (0, i)),
        pl.BlockSpec((LANES, ts), lambda b, i: (0, i)),
    ]
    k_spec = pl.BlockSpec((1, N_HEADS, ts, LANES), lambda b, i: (b, 0, i, 0))
    k_shape = jax.ShapeDtypeStruct((bsz, N_HEADS, n, LANES), BF16)
    qt_spec = pl.BlockSpec((1, N_HEADS, LANES, ts), lambda b, i: (b, 0, 0, i))
    qt_shape = jax.ShapeDtypeStruct((bsz, N_HEADS, LANES, n), BF16)
    vt_spec = pl.BlockSpec((1, N_HEADS, VT_ROWS, ts), lambda b, i: (b, 0, 0, i))
    vt_shape = jax.ShapeDtypeStruct((bsz, N_HEADS, VT_ROWS, n), BF16)
    if is_ctx:
        out_specs = [k_spec, vt_spec]
        out_shape = [k_shape, vt_shape]
    else:
        pool_w = win.shape[1] - Q_LORA - 2 * LANES
        out_specs = [qt_spec, k_spec, vt_spec, pl.BlockSpec((1, ts, pool_w), lambda b, i: (b, i, 0))]
        out_shape = [qt_shape, k_shape, vt_shape, jax.ShapeDtypeStruct((bsz, n, pool_w), F32)]
    return pl.pallas_call(
        functools.partial(_front_kernel, is_ctx=is_ctx, scale=scale),
        grid=(bsz, nt),
        in_specs=in_specs,
        out_specs=out_specs,
        out_shape=out_shape,
        compiler_params=pltpu.CompilerParams(
            dimension_semantics=("arbitrary", "arbitrary"), vmem_limit_bytes=VMEM_LIMIT),
        name="front_ctx" if is_ctx else "front",
    )(xs, mod, g1, win, qg, kvg, wuqt, wk, wvt, cs, sn, cs.T, sn.T)


def _attn_kernel(qt_ref, k_ref, vt_ref, kc_ref, vct_ref, o_ref, s_ref, mlc_ref, mxb_ref, oe_ref):
    n_ctx = kc_ref.shape[2]
    n_lat = k_ref.shape[2]
    chunks = [(None, 0, n_ctx)] + [(c * ATTN_KC, n_ctx + c * ATTN_KC, ATTN_KC)
                                   for c in range(n_lat // ATTN_KC)]
    sub = 8

    def q_cols(tile):
        if isinstance(tile, int):
            return pl.ds(tile * ATTN_TQ, ATTN_TQ)
        return pl.ds(pl.multiple_of(tile * ATTN_TQ, ATTN_TQ), ATTN_TQ)

    def score_chunk(tile, hd, buf, ci):
        off, soff, w = chunks[ci]
        keys = kc_ref[0, hd] if off is None else k_ref[0, hd, off:off + w, :]
        s = _dot(keys, qt_ref[0, hd, :, q_cols(tile)])
        s_ref[buf, soff:soff + w, :] = s
        mx = s[0:sub]
        for r in range(1, w // sub):
            mx = jnp.maximum(mx, s[r * sub:(r + 1) * sub])
        mlc_ref[buf, ci] = mx

    def row_max(buf):
        mx = mlc_ref[buf, 0]
        for ci in range(1, len(chunks)):
            mx = jnp.maximum(mx, mlc_ref[buf, ci])
        mxb_ref[...] = jnp.broadcast_to(jnp.max(mx, axis=0, keepdims=True), mx.shape)

    def weight_chunk(hd, buf, ci):
        off, soff, w = chunks[ci]
        p = jnp.exp2(s_ref[buf, soff:soff + w, :] - mxb_ref[0:1, :]).astype(BF16)
        vt = vct_ref[0, hd] if off is None else vt_ref[0, hd, :, off:off + w]
        return _dot(vt, p)

    def stage(hw, bw, scoring, bs):
        row_max(bw)
        acc = None
        for ci in range(len(chunks)):
            if scoring is not None:
                score_chunk(*scoring, bs, ci)
            part = weight_chunk(hw, bw, ci)
            acc = part if acc is None else acc + part
        return acc[0:V_DIM] / acc[V_DIM:V_DIM + 1]

    def write_pair(tile, j, ot_odd):
        pair_t = jnp.concatenate([oe_ref[...], ot_odd], axis=0)
        o_ref[0, j, q_cols(tile), :] = pair_t.T.astype(BF16)

    pairs_per_tile = N_HEADS // 2
    n_pairs = (qt_ref.shape[3] // ATTN_TQ) * pairs_per_tile

    def split(p):
        return p // pairs_per_tile, p % pairs_per_tile

    for ci in range(len(chunks)):
        score_chunk(0, 0, 0, ci)

    def head_pair(p, carry):
        tile, j = split(p)
        nxt_tile, nxt_j = split(p + 1)
        oe_ref[...] = stage(2 * j, 0, (tile, 2 * j + 1), 1)
        write_pair(tile, j, stage(2 * j + 1, 1, (nxt_tile, 2 * nxt_j), 0))
        return carry

    lax.fori_loop(0, n_pairs - 1, head_pair, 0)
    tile, j = split(n_pairs - 1)
    oe_ref[...] = stage(2 * j, 0, (tile, 2 * j + 1), 1)
    write_pair(tile, j, stage(2 * j + 1, 1, None, None))


def _attn_call(qt, k, vt, kc, vct):
    bsz, _, _, n = qt.shape
    n_ctx = kc.shape[2]
    tq = ATTN_TQ
    n_chunks = 1 + n // ATTN_KC
    per_batch = lambda b, i: (b, 0, 0, 0)
    resident = dict(pipeline_mode=pl.Buffered(1))
    rows_per_step = ATTN_TILES * tq
    return pl.pallas_call(
        _attn_kernel,
        grid=(bsz, n // rows_per_step),
        in_specs=[
            pl.BlockSpec((1, N_HEADS, LANES, rows_per_step), lambda b, i: (b, 0, 0, i)),
            pl.BlockSpec((1, N_HEADS, n, LANES), per_batch, **resident),
            pl.BlockSpec((1, N_HEADS, VT_ROWS, n), per_batch, **resident),
            pl.BlockSpec((1, N_HEADS, n_ctx, LANES), per_batch, **resident),
            pl.BlockSpec((1, N_HEADS, VT_ROWS, n_ctx), per_batch, **resident),
        ],
        out_specs=pl.BlockSpec((1, N_HEADS // 2, rows_per_step, LANES), lambda b, i: (b, 0, i, 0)),
        out_shape=jax.ShapeDtypeStruct((bsz, N_HEADS // 2, n, LANES), BF16),
        scratch_shapes=[
            pltpu.VMEM((2, n_ctx + n, tq), F32),
            pltpu.VMEM((2, n_chunks, 8, tq), F32),
            pltpu.VMEM((8, tq), F32),
            pltpu.VMEM((V_DIM, tq), F32),
        ],
        compiler_params=pltpu.CompilerParams(
            dimension_semantics=("arbitrary", "arbitrary"), vmem_limit_bytes=VMEM_LIMIT),
        name="attn",
    )(qt, k, vt, kc, vct)


def _mix_kernel(att_ref, u_ref, up_ref, un_ref, x_ref, mod_ref, wpool_ref, pscale_ref, wout_ref,
                g2_ref, rw_ref, rb_ref,
                x1_ref, h2_ref, gate_ref, ridx_ref, cnt_ref, ue_ref, base_ref, *, n_seq):
    b = pl.program_id(0)
    i = pl.program_id(1)
    n_tiles = pl.num_programs(1)
    ts = u_ref.shape[1]

    @pl.when(jnp.logical_and(b == 0, i == 0))
    def _():
        base_ref[...] = jnp.zeros_like(base_ref)

    u = u_ref[0]
    ue_ref[0:POOL_HALO] = jnp.where(i > 0, up_ref[0], 0.0)
    ue_ref[POOL_HALO:POOL_HALO + ts] = u
    ue_ref[POOL_HALO + ts:2 * POOL_HALO + ts] = jnp.where(i < n_tiles - 1, un_ref[0], 0.0)
    t = i * ts + lax.broadcasted_iota(jnp.int32, (ts, 1), 0)
    ys = []
    for g, w in enumerate(POOL_WINDOWS):
        half = w // 2
        lanes = slice(g * POOL_CH, (g + 1) * POOL_CH)
        ws = ue_ref[POOL_HALO - half:POOL_HALO - half + ts, lanes]
        for jj in range(-half + 1, half):
            ws = ws + ue_ref[POOL_HALO + jj:POOL_HALO + jj + ts, lanes]
        count = (jnp.minimum(t + half, n_seq) - jnp.maximum(t - half, 0)).astype(F32)
        mixed = (ws / count - u[:, lanes]).astype(BF16)
        ys.append(_dot(mixed, wpool_ref[g]))
    pool = jnp.concatenate(ys, axis=-1) * pscale_ref[...]

    cat = jnp.concatenate([att_ref[0, j] for j in range(N_HEADS // 2)] + [pool.astype(BF16)], axis=-1)
    m = mod_ref[0]
    x1 = x_ref[0] + m[2:3] * _dot(cat, wout_ref[...])
    x1_ref[0] = x1
    h2 = _rms(x1) * g2_ref[...] * (1.0 + m[4:5]) + m[3:4]
    h2_ref[0] = _pack_bf16_pairs(h2)

    h_hi = h2.astype(BF16)
    h_lo = (h2 - h_hi.astype(F32)).astype(BF16)
    hi_part = _dot(h_hi, rw_ref[...])
    logits = (hi_part[:, :LANES] + hi_part[:, LANES:] + _dot(h_lo, rw_ref[:, :LANES])) + rb_ref[...]
    lane = lax.broadcasted_iota(jnp.int32, logits.shape, 1).astype(F32)
    vals, idxs = [], []
    for _k in range(TOP_K):
        mv = jnp.max(logits, axis=-1, keepdims=True)
        ix = jnp.min(jnp.where(logits == mv, lane, float(LANES)), axis=-1, keepdims=True)
        vals.append(mv)
        idxs.append(ix)
        logits = jnp.where(lane == ix, -jnp.inf, logits)
    es = [jnp.exp(v - vals[0]) for v in vals]
    den = es[0] + es[1] + es[2] + es[3]

    onehot = jnp.zeros(lane.shape, F32)
    for ix in idxs:
        onehot = onehot + jnp.where(lane == ix, 1.0, 0.0)
    row = lax.broadcasted_iota(jnp.int32, (ts, ts), 0)
    col = lax.broadcasted_iota(jnp.int32, (ts, ts), 1)
    tri = jnp.where(col < row, 1.0, 0.0).astype(BF16)
    before = _dot(tri, onehot.astype(BF16)) + base_ref[0:1, :]
    base_new = base_ref[0:1, :] + jnp.sum(onehot, axis=0, keepdims=True)
    base_ref[...] = jnp.broadcast_to(base_new, base_ref.shape)
    cnt_ref[...] = jnp.broadcast_to(base_new, cnt_ref.shape)

    gate_out = jnp.zeros(lane.shape, F32)
    ridx_out = jnp.zeros(lane.shape, F32)
    for kk in range(TOP_K):
        rank = jnp.sum(jnp.where(lane == idxs[kk], before, 0.0), axis=-1, keepdims=True)
        gate_out = jnp.where(lane == float(kk), es[kk] / den, gate_out)
        ridx_out = jnp.where(lane == float(kk), idxs[kk], ridx_out)
        ridx_out = jnp.where(lane == float(TOP_K + kk), rank, ridx_out)
    gate_ref[0] = gate_out
    ridx_ref[0] = ridx_out.astype(jnp.int32)


def _mix_call(att, u, x, mod, wpool, pscale, wout, g2, rw, rb):
    bsz, n, d = x.shape
    ts = MIX_TS
    nt = n // ts
    pool_w = u.shape[2]
    hb = ts // POOL_HALO
    n_halo_blocks = n // POOL_HALO
    const2 = lambda b, i: (0, 0)
    tok = lambda b, i: (b, i, 0)
    return pl.pallas_call(
        functools.partial(_mix_kernel, n_seq=n),
        grid=(bsz, nt),
        in_specs=[
            pl.BlockSpec((1, N_HEADS // 2, ts, LANES), lambda b, i: (b, 0, i, 0)),
            pl.BlockSpec((1, ts, pool_w), tok),
            pl.BlockSpec((1, POOL_HALO, pool_w), lambda b, i: (b, jnp.maximum(i * hb - 1, 0), 0)),
            pl.BlockSpec((1, POOL_HALO, pool_w),
                         lambda b, i: (b, jnp.minimum((i + 1) * hb, n_halo_blocks - 1), 0)),
            pl.BlockSpec((1, ts, d), tok),
            pl.BlockSpec((1, N_MOD, d), lambda b, i: (b, 0, 0)),
            pl.BlockSpec(wpool.shape, lambda b, i: (0, 0, 0)),
            pl.BlockSpec((1, pool_w), const2),
            pl.BlockSpec(wout.shape, const2),
            pl.BlockSpec((1, d), const2),
            pl.BlockSpec(rw.shape, const2),
            pl.BlockSpec((1, LANES), const2),
        ],
        out_specs=[
            pl.BlockSpec((1, ts, d), tok),
            pl.BlockSpec((1, ts, d // 2), tok),
            pl.BlockSpec((1, ts, LANES), tok),
            pl.BlockSpec((1, ts, LANES), tok),
            pl.BlockSpec((8, LANES), const2),
        ],
        out_shape=[
            jax.ShapeDtypeStruct((bsz, n, d), F32),
            jax.ShapeDtypeStruct((bsz, n, d // 2), jnp.uint32),
            jax.ShapeDtypeStruct((bsz, n, LANES), F32),
            jax.ShapeDtypeStruct((bsz, n, LANES), jnp.int32),
            jax.ShapeDtypeStruct((8, LANES), F32),
        ],
        scratch_shapes=[
            pltpu.VMEM((ts + 2 * POOL_HALO, pool_w), F32),
            pltpu.VMEM((8, LANES), F32),
        ],
        compiler_params=pltpu.CompilerParams(
            dimension_semantics=("arbitrary", "arbitrary"), vmem_limit_bytes=VMEM_LIMIT),
        name="mix",
    )(att, u, u, u, x, mod, wpool, pscale, wout, g2, rw, rb)


def _sc_mesh():
    return plsc.VectorSubcoreMesh(core_axis_name="core", subcore_axis_name="subcore")


def _sc_worker_id():
    info = plsc.get_sparse_core_info()
    return lax.axis_index("subcore") * info.num_cores + lax.axis_index("core")


def _sc_num_workers():
    info = plsc.get_sparse_core_info()
    return info.num_cores * info.num_subcores


def _scatter_rows(x, dest3, n_out):
    t, d = x.shape
    top_k, n_chunks, rows = dest3.shape
    per_worker = n_chunks // _sc_num_workers()

    @functools.partial(
        pl.kernel,
        out_type=jax.ShapeDtypeStruct((n_out, d), x.dtype),
        mesh=_sc_mesh(),
        scratch_types=[pltpu.VMEM((top_k, rows), jnp.int32), pltpu.VMEM((rows, d), x.dtype)],
        name="sc_scatter",
    )
    def scatter(x_hbm, i_hbm, o_hbm, idx_v, rows_v):
        wid = _sc_worker_id()

        @pl.loop(0, per_worker)
        def _(c):
            chunk = wid * per_worker + c
            for kk in range(top_k):
                pltpu.sync_copy(i_hbm.at[kk, chunk], idx_v.at[kk])
            pltpu.sync_copy(x_hbm.at[pl.ds(pl.multiple_of(chunk * rows, 8), rows)], rows_v)
            for kk in range(top_k):
                pltpu.sync_copy(rows_v, o_hbm.at[idx_v.at[kk]])

    return scatter(x, dest3)


def _gather_rows(y, idx):
    n = idx.shape[0]
    d = y.shape[1]
    rows = SC_ROWS
    per_worker = n // _sc_num_workers()
    n_chunks = per_worker // rows

    @functools.partial(
        pl.kernel,
        out_type=jax.ShapeDtypeStruct((n, d), y.dtype),
        mesh=_sc_mesh(),
        scratch_types=[pltpu.VMEM((rows,), jnp.int32), pltpu.VMEM((rows, d), y.dtype)],
        name="sc_gather",
    )
    def gather(y_hbm, i_hbm, o_hbm, idx_v, rows_v):
        base = _sc_worker_id() * per_worker

        @pl.loop(0, n_chunks)
        def _(c):
            off = pl.multiple_of(base + c * rows, 8)
            pltpu.sync_copy(i_hbm.at[pl.ds(off, rows)], idx_v)
            pltpu.sync_copy(y_hbm.at[idx_v], rows_v)
            pltpu.sync_copy(rows_v, o_hbm.at[pl.ds(off, rows)])

    return gather(y, idx)


def _moe_kernel(blk_ref, exp_ref, lo_ref, hi_ref, new_ref, init_ref, next_ref, slot_ref, xs_ref,
                wgu_hbm, bgu_ref, wd_hbm, bd_ref, ys_ref, wgu_buf, wd_buf, sems):
    i = pl.program_id(0)
    slot = slot_ref[i]

    def weight_copies(e, s):
        return (pltpu.make_async_copy(wgu_hbm.at[e], wgu_buf.at[s], sems.at[s, 0]),
                pltpu.make_async_copy(wd_hbm.at[e], wd_buf.at[s], sems.at[s, 1]))

    @pl.when(i == 0)
    def _():
        for cp in weight_copies(exp_ref[0], slot):
            cp.start()

    @pl.when(new_ref[i] == 1)
    def _():
        for cp in weight_copies(exp_ref[i], slot):
            cp.wait()

        @pl.when(next_ref[i] >= 0)
        def _():
            for cp in weight_copies(next_ref[i], 1 - slot):
                cp.start()

    @pl.when(init_ref[i] == 1)
    def _():
        ys_ref[...] = jnp.zeros_like(ys_ref)

    lo = lo_ref[i]
    hi = hi_ref[i]

    def expert_rows(r0, n_rows):
        de = wd_buf.shape[1]
        x = _unpack_bf16_pairs(xs_ref[pl.ds(r0, n_rows), :])
        gu = _dot(x, wgu_buf[slot]) + bgu_ref[0]
        g = jnp.minimum(gu[:, :de], SWIGLU_LIMIT)
        lin = jnp.clip(gu[:, de:], -SWIGLU_LIMIT, SWIGLU_LIMIT)
        act = g / (1.0 + jnp.exp(-SWIGLU_ALPHA * g)) * (lin + 1.0)
        y = _dot(act, wd_buf[slot]) + bd_ref[0]
        row = r0 + lax.broadcasted_iota(jnp.int32, (n_rows, 1), 0)
        mine = jnp.logical_and(row >= lo, row < hi)
        ys_ref[pl.ds(r0, n_rows), :] = jnp.where(mine, _pack_bf16_pairs(y), ys_ref[pl.ds(r0, n_rows), :])

    first = lo // MOE_SUB
    pieces = jnp.where(hi > lo, (hi + MOE_SUB - 1) // MOE_SUB - first, 0)
    for cnt in range(1, MOE_BLOCK // MOE_SUB + 1):
        @pl.when(pieces == cnt)
        def _(cnt=cnt):
            r0 = 0 if cnt * MOE_SUB == MOE_BLOCK else pl.multiple_of(first * MOE_SUB, MOE_SUB)
            expert_rows(r0, cnt * MOE_SUB)


def _moe_call(sched, xs, wgu, bgu, wd, bd):
    n_items = sched[0].shape[0]
    _, d, de2 = wgu.shape
    de = de2 // 2
    wmap = lambda i, blk, exp, lo, hi, new, init, nxt, slot: (exp[i], 0, 0)
    xmap = lambda i, blk, exp, lo, hi, new, init, nxt, slot: (blk[i], 0)
    return pl.pallas_call(
        _moe_kernel,
        grid_spec=pltpu.PrefetchScalarGridSpec(
            num_scalar_prefetch=8,
            grid=(n_items,),
            in_specs=[
                pl.BlockSpec((MOE_BLOCK, d // 2), xmap),
                pl.BlockSpec(memory_space=pl.ANY),
                pl.BlockSpec((1, 1, de2), wmap),
                pl.BlockSpec(memory_space=pl.ANY),
                pl.BlockSpec((1, 1, d), wmap),
            ],
            out_specs=pl.BlockSpec((MOE_BLOCK, d // 2), xmap),
            scratch_shapes=[
                pltpu.VMEM((2, d, de2), F32), pltpu.VMEM((2, de, d), F32),
                pltpu.SemaphoreType.DMA((2, 2)),
            ],
        ),
        out_shape=jax.ShapeDtypeStruct(xs.shape, xs.dtype),
        compiler_params=pltpu.CompilerParams(
            dimension_semantics=("arbitrary",), vmem_limit_bytes=VMEM_LIMIT),
        name="moe",
    )(*sched, xs, wgu, bgu, wd, bd)


def _moe_schedule(counts, n_rows):
    n_blocks = n_rows // MOE_BLOCK
    n_items = n_blocks + N_EXPERTS
    ends = jnp.cumsum(counts)
    starts = ends - counts
    first_blk = starts // MOE_BLOCK
    last_blk = (ends - 1) // MOE_BLOCK
    items_per = jnp.where(counts > 0, last_blk - first_blk + 1, 0)
    item_ends = jnp.cumsum(items_per)
    item_starts = item_ends - items_per
    total = item_ends[-1]
    it = jnp.arange(n_items, dtype=jnp.int32)
    live = it < total
    itc = jnp.minimum(it, total - 1)
    exp = jnp.sum((item_ends[None, :] <= itc[:, None]).astype(jnp.int32), axis=1)
    is_exp = exp[:, None] == jnp.arange(N_EXPERTS, dtype=jnp.int32)[None, :]
    pick = lambda table: jnp.sum(jnp.where(is_exp, table[None, :], 0), axis=1)
    blk = pick(first_blk) + itc - pick(item_starts)
    lo = jnp.clip(pick(starts) - blk * MOE_BLOCK, 0, MOE_BLOCK)
    hi = jnp.clip(pick(ends) - blk * MOE_BLOCK, 0, MOE_BLOCK)
    hi = jnp.where(live, hi, lo)
    prev_exp = jnp.concatenate([jnp.full((1,), -1, jnp.int32), exp[:-1]])
    prev_blk = jnp.concatenate([jnp.full((1,), -1, jnp.int32), blk[:-1]])
    new = jnp.logical_and(live, exp != prev_exp)
    init = jnp.logical_and(live, blk != prev_blk)
    slot = (jnp.cumsum(new.astype(jnp.int32)) - 1) % 2
    ar = jnp.arange(N_EXPERTS, dtype=jnp.int32)
    later = jnp.logical_and(counts[None, :] > 0, ar[None, :] > ar[:, None])
    next_exp = jnp.min(jnp.where(later, ar[None, :], N_EXPERTS), axis=1)
    next_exp = jnp.where(next_exp == N_EXPERTS, -1, next_exp)
    nxt = pick(next_exp)
    as_i32 = lambda a: a.astype(jnp.int32)
    return tuple(as_i32(a) for a in (blk, exp, lo, hi, new, init, nxt, slot)), starts


def _combine_kernel(*refs):
    yg_refs = refs[:TOP_K]
    x1_ref, gate_ref, mod_ref, fg_ref, o_ref = refs[TOP_K:]
    gates = gate_ref[...]
    y = gates[:, 0:1] * _unpack_bf16_pairs(yg_refs[0][...])
    for kk in range(1, TOP_K):
        y = y + gates[:, kk:kk + 1] * _unpack_bf16_pairs(yg_refs[kk][...])
    m = mod_ref[0]
    x2 = x1_ref[...] + m[5:6] * y
    o_ref[...] = _rms(x2) * fg_ref[...]


def _combine_call(yg, x1, gates, mod, fg, n_seq, part):
    t, d = x1.shape
    tt = COMBINE_TT
    tiles_per_seq = n_seq // tt
    n_tiles = yg.shape[0] // (TOP_K * tt)
    first = part * n_tiles
    tok = lambda i: (first + i, 0)
    slot_specs = [pl.BlockSpec((tt, d // 2), functools.partial(lambda kk, i: (kk * n_tiles + i, 0), kk))
                  for kk in range(TOP_K)]
    return pl.pallas_call(
        _combine_kernel,
        grid=(n_tiles,),
        in_specs=slot_specs + [
            pl.BlockSpec((tt, d), tok),
            pl.BlockSpec((tt, LANES), tok),
            pl.BlockSpec((1, N_MOD, d), lambda i: ((first + i) // tiles_per_seq, 0, 0)),
            pl.BlockSpec((1, d), lambda i: (0, 0)),
        ],
        out_specs=pl.BlockSpec((tt, d), tok),
        out_shape=jax.ShapeDtypeStruct((t, d), F32),
        input_output_aliases={TOP_K: 0},
        compiler_params=pltpu.CompilerParams(
            dimension_semantics=("arbitrary",), vmem_limit_bytes=VMEM_LIMIT),
        name="combine",
    )(*([yg] * TOP_K), x1, gates, mod, fg)


def _pad_cols(a, width):
    return jnp.pad(a, ((0, 0), (0, width - a.shape[1])))


def _rope_slab(w_rope):
    return jnp.pad(w_rope, ((0, 0), (QK_NOPE, LANES - QK_NOPE - QK_ROPE)))


def _prep_w_in(w_in):
    kr0 = Q_LORA + KV_LORA
    w_kr = w_in[:, kr0:kr0 + QK_ROPE]
    return jnp.concatenate(
        [w_in[:, :kr0], _rope_slab(w_kr), w_in[:, kr0 + QK_ROPE:]], axis=1).astype(BF16)


def _prep_w_uq(w_uq):
    per = QK_NOPE + QK_ROPE
    rows = w_uq.shape[0]
    w = w_uq.reshape(rows, N_HEADS, per)
    slab = jnp.pad(w, ((0, 0), (0, 0), (0, LANES - per)))
    return slab.reshape(rows, -1).T.astype(BF16)


def _prep_w_ukv(w_ukv):
    per = QK_NOPE + V_DIM
    rows = w_ukv.shape[0]
    w = w_ukv.reshape(rows, N_HEADS, per)
    wk = jnp.pad(w[:, :, :QK_NOPE], ((0, 0), (0, 0), (0, LANES - QK_NOPE))).reshape(rows, -1)
    wvt = jnp.pad(jnp.transpose(w[:, :, QK_NOPE:], (1, 2, 0)), ((0, 0), (0, VT_ROWS - V_DIM), (0, 0)))
    return wk.astype(BF16), wvt.reshape(N_HEADS * VT_ROWS, rows).astype(BF16)


def _rope_tables(n_lat):
    rows = n_lat // GRID_W
    nf = QK_ROPE // 4
    row = jnp.repeat(jnp.arange(rows, dtype=F32), GRID_W)
    col = jnp.tile(jnp.arange(GRID_W, dtype=F32), rows)
    freqs = ROPE_BASE ** (-jnp.arange(nf, dtype=F32) / nf)
    ang = jnp.concatenate([row[:, None] * freqs, col[:, None] * freqs], axis=-1)
    cos, sin = jnp.cos(ang), jnp.sin(ang)
    ones = jnp.ones((n_lat, QK_NOPE), F32)
    zeros = jnp.zeros((n_lat, QK_NOPE), F32)
    cs = _pad_cols(jnp.concatenate([ones, cos, cos], axis=1), LANES)
    sn = _pad_cols(jnp.concatenate([zeros, -sin, sin], axis=1), LANES)
    return cs, sn


def kernel(x, c, ctx, c_ctx, w_mod, b_mod, norm1_g, w_in, q_norm_g, kv_norm_g, w_uq, w_ukv, w_pool,
           pool_scale, w_out, norm2_g, router_w, router_b, w_gate_up, b_gate_up, w_down, b_down,
           final_g):
    bsz, n, d = x.shape
    n_ctx = ctx.shape[1]
    t = bsz * n
    l = 0

    cc = jnp.concatenate([c, c_ctx[None, :], jnp.zeros((8 - bsz - 1, d), F32)], axis=0)
    mod = _mod_call(cc, w_mod[l], b_mod[l][None, :]).reshape(8, N_MOD, d)
    mod_lat, mod_ctx = mod[:bsz], mod[bsz:bsz + 1]

    win = _prep_w_in(w_in[l])
    wuqt = _prep_w_uq(w_uq[l])
    wk, wvt = _prep_w_ukv(w_ukv[l])
    cs, sn = _rope_tables(n)
    cs_ctx = jnp.broadcast_to((jnp.arange(LANES) < QK_NOPE + QK_ROPE).astype(F32), (n_ctx, LANES))
    sn_ctx = jnp.zeros((n_ctx, LANES), F32)
    g1 = norm1_g[l][None, :]
    qg = q_norm_g[l][None, :]
    kvg = kv_norm_g[l][None, :]

    qt, k, vt, u = _front_call(x, mod_lat, True, g1, win, qg, kvg, wuqt, wk, wvt, cs, sn,
                               is_ctx=False, ts=FRONT_TS)
    kc, vct = _front_call(ctx, mod_ctx, False, g1, win[:, Q_LORA:Q_LORA + 2 * LANES], qg, kvg, wuqt,
                          wk, wvt, cs_ctx, sn_ctx, is_ctx=True, ts=n_ctx)
    att = _attn_call(qt, k, vt, kc, vct)

    rw = _pad_cols(router_w[l], LANES)
    rw_hi = rw.astype(BF16)
    rw = jnp.concatenate([rw_hi, (rw - rw_hi.astype(F32)).astype(BF16)], axis=1)
    rb = jnp.concatenate([router_b[l], jnp.full((LANES - N_EXPERTS,), -jnp.inf, F32)])[None, :]
    x1, h2, gates, ridx, cnt = _mix_call(
        att, u, x, mod_lat, w_pool[l].astype(BF16), pool_scale[l][None, :], w_out[l].astype(BF16),
        norm2_g[l][None, :], rw, rb)

    counts = cnt[0, :N_EXPERTS].astype(jnp.int32)
    sched, starts = _moe_schedule(counts, t * TOP_K)
    ridx = ridx.reshape(t, LANES)
    is_exp = ridx[:, :TOP_K, None] == jnp.arange(N_EXPERTS, dtype=jnp.int32)
    dest = jnp.sum(jnp.where(is_exp, starts, 0), axis=-1) + ridx[:, TOP_K:2 * TOP_K]
    dest_t = dest.T

    xs = _scatter_rows(h2.reshape(t, d // 2), dest_t.reshape(TOP_K, t // SC_ROWS, SC_ROWS), t * TOP_K)
    ys = _moe_call(sched, xs, w_gate_up[l], b_gate_up[l][:, None, :], w_down[l],
                   b_down[l][:, None, :])
    out = x1.reshape(t, d)
    gates = gates.reshape(t, LANES)
    tp = t // COMBINE_PARTS
    for part in range(COMBINE_PARTS):
        idx = dest_t[:, part * tp:(part + 1) * tp].reshape(-1)
        yg = _gather_rows(ys, idx)
        out = _combine_call(yg, out, gates, mod_lat, final_g[None, :], n, part)
    return out.reshape(bsz, n, d)
```

```python
import functools
import math

import jax
import jax.numpy as jnp
from jax import lax
from jax.experimental import pallas as pl
from jax.experimental.pallas import tpu as pltpu
from jax.experimental.pallas import tpu_sc as plsc

F32 = jnp.float32
BF16 = jnp.bfloat16
HIGHEST = lax.Precision.HIGHEST

N_HEADS = 8
QK_NOPE = 64
QK_ROPE = 32
V_DIM = 64
Q_LORA = 256
KV_LORA = 128
GRID_W = 64
ROPE_BASE = 10000.0
POOL_WINDOWS = (2, 4, 8, 16)
POOL_CH = 128
N_EXPERTS = 32
TOP_K = 4
SWIGLU_LIMIT = 7.0
SWIGLU_ALPHA = 1.702
MOE_BLOCK = 512
MOE_SUB = 64
N_MOD = 6
EPS = 1e-6

LANES = 128
POOL_HALO = 8
VT_ROWS = 80
VMEM_LIMIT = 56 * 1024 * 1024

FRONT_TS = 1024
ATTN_TQ = 512
ATTN_TILES = 4
ATTN_KC = 256
MIX_TS = 512
COMBINE_TT = 1024
COMBINE_PARTS = 4
SC_ROWS = 128


def _dot(a, b, **kw):
    return jnp.dot(a, b, preferred_element_type=F32, **kw)


def _dot_nt(a, b):
    return lax.dot_general(a, b, (((1,), (1,)), ((), ())), preferred_element_type=F32)


def _rms(x):
    return x * lax.rsqrt(jnp.mean(x * x, axis=-1, keepdims=True) + EPS)


def _pack_bf16_pairs(x):
    w = x.shape[1] // 2
    return pltpu.pack_elementwise([x[:, :w], x[:, w:]], packed_dtype=BF16)


def _unpack_bf16_pairs(words):
    halves = [pltpu.unpack_elementwise(words, index=h, packed_dtype=BF16, unpacked_dtype=F32)
              for h in range(2)]
    return jnp.concatenate(halves, axis=-1)


def _mod_kernel(c_ref, w_ref, b_ref, o_ref):
    c = c_ref[...]
    a = c / (1.0 + jnp.exp(-c))
    o_ref[...] = _dot(a, w_ref[...], precision=HIGHEST) + b_ref[...]


def _mod_call(cc, w_mod, b_mod):
    d = w_mod.shape[0]
    return pl.pallas_call(
        _mod_kernel,
        grid=(N_MOD,),
        in_specs=[
            pl.BlockSpec((8, d), lambda j: (0, 0)),
            pl.BlockSpec((d, d), lambda j: (0, j)),
            pl.BlockSpec((1, d), lambda j: (0, j)),
        ],
        out_specs=pl.BlockSpec((8, d), lambda j: (0, j)),
        out_shape=jax.ShapeDtypeStruct((8, N_MOD * d), F32),
        compiler_params=pltpu.CompilerParams(
            dimension_semantics=("arbitrary",), vmem_limit_bytes=VMEM_LIMIT),
        name="mod",
    )(cc, w_mod, b_mod)


def _front_kernel(x_ref, mod_ref, g1_ref, win_ref, qg_ref, kvg_ref, wuqt_ref, wk_ref, wvt_ref,
                  cs_ref, sn_ref, cst_ref, snt_ref, *out_refs, is_ctx, scale):
    x = x_ref[0]
    m = mod_ref[0]
    h = _rms(x) * g1_ref[...] * (1.0 + m[1:2]) + m[0:1]
    z = _dot(h.astype(BF16), win_ref[...])
    if is_ctx:
        k_ref, vt_ref = out_refs
        zkv = z
    else:
        qt_ref, k_ref, vt_ref, u_ref = out_refs
        zkv = z[:, Q_LORA:Q_LORA + 2 * LANES]
        u_ref[0] = z[:, Q_LORA + 2 * LANES:]

    ckv = _rms(zkv[:, :KV_LORA]) * kvg_ref[...]
    kk = _dot(ckv.astype(BF16), wk_ref[...])
    kra = zkv[:, LANES:2 * LANES]
    lane = lax.broadcasted_iota(jnp.int32, kra.shape, 1)
    half = QK_ROPE // 2
    partner = jnp.where(lane < QK_NOPE + half, pltpu.roll(kra, LANES - half, axis=1),
                        pltpu.roll(kra, half, axis=1))
    kr = kra * cs_ref[...] + partner * sn_ref[...]
    for hd in range(N_HEADS):
        k_ref[0, hd] = (kk[:, hd * LANES:(hd + 1) * LANES] + kr).astype(BF16)
    vt = _dot(wvt_ref[...], ckv.T.astype(BF16))
    row = lax.broadcasted_iota(jnp.int32, (VT_ROWS, 1), 0)
    ones = jnp.where(row == V_DIM, 1.0, 0.0)
    for hd in range(N_HEADS):
        vt_ref[0, hd] = (vt[hd * VT_ROWS:(hd + 1) * VT_ROWS] + ones).astype(BF16)

    if not is_ctx:
        cq = _rms(z[:, :Q_LORA]) * qg_ref[...]
        qqt = _dot(wuqt_ref[...], cq.T.astype(BF16))
        cst = cst_ref[...]
        snt = snt_ref[...]
        r1, r2, r3 = QK_NOPE, QK_NOPE + QK_ROPE // 2, QK_NOPE + QK_ROPE
        for hd in range(N_HEADS):
            qa = qqt[hd * LANES:(hd + 1) * LANES]
            qb = jnp.concatenate([qa[:r1], qa[r2:r3], qa[r1:r2], qa[r3:]], axis=0)
            qt_ref[0, hd] = ((qa * cst + qb * snt) * scale).astype(BF16)


def _front_call(xs, mod, mod_per_batch, g1, win, qg, kvg, wuqt, wk, wvt, cs, sn, *, is_ctx, ts):
    bsz, n, d = xs.shape
    nt = n // ts
    scale = math.log2(math.e) / math.sqrt(QK_NOPE + QK_ROPE)
    const = lambda b, i: (0, 0)
    mod_map = (lambda b, i: (b, 0, 0)) if mod_per_batch else (lambda b, i: (0, 0, 0))
    in_specs = [
        pl.BlockSpec((1, ts, d), lambda b, i: (b, i, 0)),
        pl.BlockSpec((1, N_MOD, d), mod_map),
        pl.BlockSpec((1, d), const),
        pl.BlockSpec(win.shape, const),
        pl.BlockSpec((1, Q_LORA), const),
        pl.BlockSpec((1, KV_LORA), const),
        pl.BlockSpec(wuqt.shape, const),
        pl.BlockSpec(wk.shape, const),
        pl.BlockSpec(wvt.shape, const),
        pl.BlockSpec((ts, LANES), lambda b, i: (i, 0)),
        pl.BlockSpec((ts, LANES), lambda b, i: (i, 0)),
        pl.BlockSpec((LANES, ts), lambda b, i: (0, i)),
        pl.BlockSpec((LANES, ts), lambda b, i: (0, i)),
    ]
    k_spec = pl.BlockSpec((1, N_HEADS, ts, LANES), lambda b, i: (b, 0, i, 0))
    k_shape = jax.ShapeDtypeStruct((bsz, N_HEADS, n, LANES), BF16)
    qt_spec = pl.BlockSpec((1, N_HEADS, LANES, ts), lambda b, i: (b, 0, 0, i))
    qt_shape = jax.ShapeDtypeStruct((bsz, N_HEADS, LANES, n), BF16)
    vt_spec = pl.BlockSpec((1, N_HEADS, VT_ROWS, ts), lambda b, i: (b, 0, 0, i))
    vt_shape = jax.ShapeDtypeStruct((bsz, N_HEADS, VT_ROWS, n), BF16)
    if is_ctx:
        out_specs = [k_spec, vt_spec]
        out_shape = [k_shape, vt_shape]
    else:
        pool_w = win.shape[1] - Q_LORA - 2 * LANES
        out_specs = [qt_spec, k_spec, vt_spec, pl.BlockSpec((1, ts, pool_w), lambda b, i: (b, i, 0))]
        out_shape = [qt_shape, k_shape, vt_shape, jax.ShapeDtypeStruct((bsz, n, pool_w), F32)]
    return pl.pallas_call(
        functools.partial(_front_kernel, is_ctx=is_ctx, scale=scale),
        grid=(bsz, nt),
        in_specs=in_specs,
        out_specs=out_specs,
        out_shape=out_shape,
        compiler_params=pltpu.CompilerParams(
            dimension_semantics=("arbitrary", "arbitrary"), vmem_limit_bytes=VMEM_LIMIT),
        name="front_ctx" if is_ctx else "front",
    )(xs, mod, g1, win, qg, kvg, wuqt, wk, wvt, cs, sn, cs.T, sn.T)


def _attn_kernel(qt_ref, k_ref, vt_ref, kc_ref, vct_ref, o_ref, s_ref, mlc_ref, mxb_ref, oe_ref):
    n_ctx = kc_ref.shape[2]
    n_lat = k_ref.shape[2]
    chunks = [(None, 0, n_ctx)] + [(c * ATTN_KC, n_ctx + c * ATTN_KC, ATTN_KC)
                                   for c in range(n_lat // ATTN_KC)]
    sub = 8

    def q_cols(tile):
        if isinstance(tile, int):
            return pl.ds(tile * ATTN_TQ, ATTN_TQ)
        return pl.ds(pl.multiple_of(tile * ATTN_TQ, ATTN_TQ), ATTN_TQ)

    def score_chunk(tile, hd, buf, ci):
        off, soff, w = chunks[ci]
        keys = kc_ref[0, hd] if off is None else k_ref[0, hd, off:off + w, :]
        s = _dot(keys, qt_ref[0, hd, :, q_cols(tile)])
        s_ref[buf, soff:soff + w, :] = s
        mx = s[0:sub]
        for r in range(1, w // sub):
            mx = jnp.maximum(mx, s[r * sub:(r + 1) * sub])
        mlc_ref[buf, ci] = mx

    def row_max(buf):
        mx = mlc_ref[buf, 0]
        for ci in range(1, len(chunks)):
            mx = jnp.maximum(mx, mlc_ref[buf, ci])
        mxb_ref[...] = jnp.broadcast_to(jnp.max(mx, axis=0, keepdims=True), mx.shape)

    def weight_chunk(hd, buf, ci):
        off, soff, w = chunks[ci]
        p = jnp.exp2(s_ref[buf, soff:soff + w, :] - mxb_ref[0:1, :]).astype(BF16)
        vt = vct_ref[0, hd] if off is None else vt_ref[0, hd, :, off:off + w]
        return _dot(vt, p)

    def stage(hw, bw, scoring, bs):
        row_max(bw)
        acc = None
        for ci in range(len(chunks)):
            if scoring is not None:
                score_chunk(*scoring, bs, ci)
            part = weight_chunk(hw, bw, ci)
            acc = part if acc is None else acc + part
        return acc[0:V_DIM] / acc[V_DIM:V_DIM + 1]

    def write_pair(tile, j, ot_odd):
        pair_t = jnp.concatenate([oe_ref[...], ot_odd], axis=0)
        o_ref[0, j, q_cols(tile), :] = pair_t.T.astype(BF16)

    pairs_per_tile = N_HEADS // 2
    n_pairs = (qt_ref.shape[3] // ATTN_TQ) * pairs_per_tile

    def split(p):
        return p // pairs_per_tile, p % pairs_per_tile

    for ci in range(len(chunks)):
        score_chunk(0, 0, 0, ci)

    def head_pair(p, carry):
        tile, j = split(p)
        nxt_tile, nxt_j = split(p + 1)
        oe_ref[...] = stage(2 * j, 0, (tile, 2 * j + 1), 1)
        write_pair(tile, j, stage(2 * j + 1, 1, (nxt_tile, 2 * nxt_j), 0))
        return carry

    lax.fori_loop(0, n_pairs - 1, head_pair, 0)
    tile, j = split(n_pairs - 1)
    oe_ref[...] = stage(2 * j, 0, (tile, 2 * j + 1), 1)
    write_pair(tile, j, stage(2 * j + 1, 1, None, None))


def _attn_call(qt, k, vt, kc, vct):
    bsz, _, _, n = qt.shape
    n_ctx = kc.shape[2]
    tq = ATTN_TQ
    n_chunks = 1 + n // ATTN_KC
    per_batch = lambda b, i: (b, 0, 0, 0)
    resident = dict(pipeline_mode=pl.Buffered(1))
    rows_per_step = ATTN_TILES * tq
    return pl.pallas_call(
        _attn_kernel,
        grid=(bsz, n // rows_per_step),
        in_specs=[
            pl.BlockSpec((1, N_HEADS, LANES, rows_per_step), lambda b, i: (b, 0, 0, i)),
            pl.BlockSpec((1, N_HEADS, n, LANES), per_batch, **resident),
            pl.BlockSpec((1, N_HEADS, VT_ROWS, n), per_batch, **resident),
            pl.BlockSpec((1, N_HEADS, n_ctx, LANES), per_batch, **resident),
            pl.BlockSpec((1, N_HEADS, VT_ROWS, n_ctx), per_batch, **resident),
        ],
        out_specs=pl.BlockSpec((1, N_HEADS // 2, rows_per_step, LANES), lambda b, i: (b, 0, i, 0)),
        out_shape=jax.ShapeDtypeStruct((bsz, N_HEADS // 2, n, LANES), BF16),
        scratch_shapes=[
            pltpu.VMEM((2, n_ctx + n, tq), F32),
            pltpu.VMEM((2, n_chunks, 8, tq), F32),
            pltpu.VMEM((8, tq), F32),
            pltpu.VMEM((V_DIM, tq), F32),
        ],
        compiler_params=pltpu.CompilerParams(
            dimension_semantics=("arbitrary", "arbitrary"), vmem_limit_bytes=VMEM_LIMIT),
        name="attn",
    )(qt, k, vt, kc, vct)


def _mix_kernel(att_ref, u_ref, up_ref, un_ref, x_ref, mod_ref, wpool_ref, pscale_ref, wout_ref,
                g2_ref, rw_ref, rb_ref,
                x1_ref, h2_ref, gate_ref, ridx_ref, cnt_ref, ue_ref, base_ref, *, n_seq):
    b = pl.program_id(0)
    i = pl.program_id(1)
    n_tiles = pl.num_programs(1)
    ts = u_ref.shape[1]

    @pl.when(jnp.logical_and(b == 0, i == 0))
    def _():
        base_ref[...] = jnp.zeros_like(base_ref)

    u = u_ref[0]
    ue_ref[0:POOL_HALO] = jnp.where(i > 0, up_ref[0], 0.0)
    ue_ref[POOL_HALO:POOL_HALO + ts] = u
    ue_ref[POOL_HALO + ts:2 * POOL_HALO + ts] = jnp.where(i < n_tiles - 1, un_ref[0], 0.0)
    t = i * ts + lax.broadcasted_iota(jnp.int32, (ts, 1), 0)
    ys = []
    for g, w in enumerate(POOL_WINDOWS):
        half = w // 2
        lanes = slice(g * POOL_CH, (g + 1) * POOL_CH)
        ws = ue_ref[POOL_HALO - half:POOL_HALO - half + ts, lanes]
        for jj in range(-half + 1, half):
            ws = ws + ue_ref[POOL_HALO + jj:POOL_HALO + jj + ts, lanes]
        count = (jnp.minimum(t + half, n_seq) - jnp.maximum(t - half, 0)).astype(F32)
        mixed = (ws / count - u[:, lanes]).astype(BF16)
        ys.append(_dot(mixed, wpool_ref[g]))
    pool = jnp.concatenate(ys, axis=-1) * pscale_ref[...]

    cat = jnp.concatenate([att_ref[0, j] for j in range(N_HEADS // 2)] + [pool.astype(BF16)], axis=-1)
    m = mod_ref[0]
    x1 = x_ref[0] + m[2:3] * _dot(cat, wout_ref[...])
    x1_ref[0] = x1
    h2 = _rms(x1) * g2_ref[...] * (1.0 + m[4:5]) + m[3:4]
    h2_ref[0] = _pack_bf16_pairs(h2)

    h_hi = h2.astype(BF16)
    h_lo = (h2 - h_hi.astype(F32)).astype(BF16)
    hi_part = _dot(h_hi, rw_ref[...])
    logits = (hi_part[:, :LANES] + hi_part[:, LANES:] + _dot(h_lo, rw_ref[:, :LANES])) + rb_ref[...]
    lane = lax.broadcasted_iota(jnp.int32, logits.shape, 1).astype(F32)
    vals, idxs = [], []
    for _k in range(TOP_K):
        mv = jnp.max(logits, axis=-1, keepdims=True)
        ix = jnp.min(jnp.where(logits == mv, lane, float(LANES)), axis=-1, keepdims=True)
        vals.append(mv)
        idxs.append(ix)
        logits = jnp.where(lane == ix, -jnp.inf, logits)
    es = [jnp.exp(v - vals[0]) for v in vals]
    den = es[0] + es[1] + es[2] + es[3]

    onehot = jnp.zeros(lane.shape, F32)
    for ix in idxs:
        onehot = onehot + jnp.where(lane == ix, 1.0, 0.0)
    row = lax.broadcasted_iota(jnp.int32, (ts, ts), 0)
    col = lax.broadcasted_iota(jnp.int32, (ts, ts), 1)
    tri = jnp.where(col < row, 1.0, 0.0).astype(BF16)
    before = _dot(tri, onehot.astype(BF16)) + base_ref[0:1, :]
    base_new = base_ref[0:1, :] + jnp.sum(onehot, axis=0, keepdims=True)
    base_ref[...] = jnp.broadcast_to(base_new, base_ref.shape)
    cnt_ref[...] = jnp.broadcast_to(base_new, cnt_ref.shape)

    gate_out = jnp.zeros(lane.shape, F32)
    ridx_out = jnp.zeros(lane.shape, F32)
    for kk in range(TOP_K):
        rank = jnp.sum(jnp.where(lane == idxs[kk], before, 0.0), axis=-1, keepdims=True)
        gate_out = jnp.where(lane == float(kk), es[kk] / den, gate_out)
        ridx_out = jnp.where(lane == float(kk), idxs[kk], ridx_out)
        ridx_out = jnp.where(lane == float(TOP_K + kk), rank, ridx_out)
    gate_ref[0] = gate_out
    ridx_ref[0] = ridx_out.astype(jnp.int32)


def _mix_call(att, u, x, mod, wpool, pscale, wout, g2, rw, rb):
    bsz, n, d = x.shape
    ts = MIX_TS
    nt = n // ts
    pool_w = u.shape[2]
    hb = ts // POOL_HALO
    n_halo_blocks = n // POOL_HALO
    const2 = lambda b, i: (0, 0)
    tok = lambda b, i: (b, i, 0)
    return pl.pallas_call(
        functools.partial(_mix_kernel, n_seq=n),
        grid=(bsz, nt),
        in_specs=[
            pl.BlockSpec((1, N_HEADS // 2, ts, LANES), lambda b, i: (b, 0, i, 0)),
            pl.BlockSpec((1, ts, pool_w), tok),
            pl.BlockSpec((1, POOL_HALO, pool_w), lambda b, i: (b, jnp.maximum(i * hb - 1, 0), 0)),
            pl.BlockSpec((1, POOL_HALO, pool_w),
                         lambda b, i: (b, jnp.minimum((i + 1) * hb, n_halo_blocks - 1), 0)),
            pl.BlockSpec((1, ts, d), tok),
            pl.BlockSpec((1, N_MOD, d), lambda b, i: (b, 0, 0)),
            pl.BlockSpec(wpool.shape, lambda b, i: (0, 0, 0)),
            pl.BlockSpec((1, pool_w), const2),
            pl.BlockSpec(wout.shape, const2),
            pl.BlockSpec((1, d), const2),
            pl.BlockSpec(rw.shape, const2),
            pl.BlockSpec((1, LANES), const2),
        ],
        out_specs=[
            pl.BlockSpec((1, ts, d), tok),
            pl.BlockSpec((1, ts, d // 2), tok),
            pl.BlockSpec((1, ts, LANES), tok),
            pl.BlockSpec((1, ts, LANES), tok),
            pl.BlockSpec((8, LANES), const2),
        ],
        out_shape=[
            jax.ShapeDtypeStruct((bsz, n, d), F32),
            jax.ShapeDtypeStruct((bsz, n, d // 2), jnp.uint32),
            jax.ShapeDtypeStruct((bsz, n, LANES), F32),
            jax.ShapeDtypeStruct((bsz, n, LANES), jnp.int32),
            jax.ShapeDtypeStruct((8, LANES), F32),
        ],
        scratch_shapes=[
            pltpu.VMEM((ts + 2 * POOL_HALO, pool_w), F32),
            pltpu.VMEM((8, LANES), F32),
        ],
        compiler_params=pltpu.CompilerParams(
            dimension_semantics=("arbitrary", "arbitrary"), vmem_limit_bytes=VMEM_LIMIT),
        name="mix",
    )(att, u, u, u, x, mod, wpool, pscale, wout, g2, rw, rb)


def _sc_mesh():
    return plsc.VectorSubcoreMesh(core_axis_name="core", subcore_axis_name="subcore")


def _sc_worker_id():
    info = plsc.get_sparse_core_info()
    return lax.axis_index("subcore") * info.num_cores + lax.axis_index("core")


def _sc_num_workers():
    info = plsc.get_sparse_core_info()
    return info.num_cores * info.num_subcores


def _scatter_rows(x, dest3, n_out):
    t, d = x.shape
    top_k, n_chunks, rows = dest3.shape
    per_worker = n_chunks // _sc_num_workers()

    @functools.partial(
        pl.kernel,
        out_type=jax.ShapeDtypeStruct((n_out, d), x.dtype),
        mesh=_sc_mesh(),
        scratch_types=[pltpu.VMEM((top_k, rows), jnp.int32), pltpu.VMEM((rows, d), x.dtype)],
        name="sc_scatter",
    )
    def scatter(x_hbm, i_hbm, o_hbm, idx_v, rows_v):
        wid = _sc_worker_id()

        @pl.loop(0, per_worker)
        def _(c):
            chunk = wid * per_worker + c
            for kk in range(top_k):
                pltpu.sync_copy(i_hbm.at[kk, chunk], idx_v.at[kk])
            pltpu.sync_copy(x_hbm.at[pl.ds(pl.multiple_of(chunk * rows, 8), rows)], rows_v)
            for kk in range(top_k):
                pltpu.sync_copy(rows_v, o_hbm.at[idx_v.at[kk]])

    return scatter(x, dest3)


def _gather_rows(y, idx):
    n = idx.shape[0]
    d = y.shape[1]
    rows = SC_ROWS
    per_worker = n // _sc_num_workers()
    n_chunks = per_worker // rows

    @functools.partial(
        pl.kernel,
        out_type=jax.ShapeDtypeStruct((n, d), y.dtype),
        mesh=_sc_mesh(),
        scratch_types=[pltpu.VMEM((rows,), jnp.int32), pltpu.VMEM((rows, d), y.dtype)],
        name="sc_gather",
    )
    def gather(y_hbm, i_hbm, o_hbm, idx_v, rows_v):
        base = _sc_worker_id() * per_worker

        @pl.loop(0, n_chunks)
        def _(c):
            off = pl.multiple_of(base + c * rows, 8)
            pltpu.sync_copy(i_hbm.at[pl.ds(off, rows)], idx_v)
            pltpu.sync_copy(y_hbm.at[idx_v], rows_v)
            pltpu.sync_copy(rows_v, o_hbm.at[pl.ds(off, rows)])

    return gather(y, idx)


def _moe_kernel(blk_ref, exp_ref, lo_ref, hi_ref, new_ref, init_ref, next_ref, slot_ref, xs_ref,
                wgu_hbm, bgu_ref, wd_hbm, bd_ref, ys_ref, wgu_buf, wd_buf, sems):
    i = pl.program_id(0)
    slot = slot_ref[i]

    def weight_copies(e, s):
        return (pltpu.make_async_copy(wgu_hbm.at[e], wgu_buf.at[s], sems.at[s, 0]),
                pltpu.make_async_copy(wd_hbm.at[e], wd_buf.at[s], sems.at[s, 1]))

    @pl.when(i == 0)
    def _():
        for cp in weight_copies(exp_ref[0], slot):
            cp.start()

    @pl.when(new_ref[i] == 1)
    def _():
        for cp in weight_copies(exp_ref[i], slot):
            cp.wait()

        @pl.when(next_ref[i] >= 0)
        def _():
            for cp in weight_copies(next_ref[i], 1 - slot):
                cp.start()

    @pl.when(init_ref[i] == 1)
    def _():
        ys_ref[...] = jnp.zeros_like(ys_ref)

    lo = lo_ref[i]
    hi = hi_ref[i]

    def expert_rows(r0, n_rows):
        de = wd_buf.shape[1]
        x = _unpack_bf16_pairs(xs_ref[pl.ds(r0, n_rows), :])
        gu = _dot(x, wgu_buf[slot]) + bgu_ref[0]
        g = jnp.minimum(gu[:, :de], SWIGLU_LIMIT)
        lin = jnp.clip(gu[:, de:], -SWIGLU_LIMIT, SWIGLU_LIMIT)
        act = g / (1.0 + jnp.exp(-SWIGLU_ALPHA * g)) * (lin + 1.0)
        y = _dot(act, wd_buf[slot]) + bd_ref[0]
        row = r0 + lax.broadcasted_iota(jnp.int32, (n_rows, 1), 0)
        mine = jnp.logical_and(row >= lo, row < hi)
        ys_ref[pl.ds(r0, n_rows), :] = jnp.where(mine, _pack_bf16_pairs(y), ys_ref[pl.ds(r0, n_rows), :])

    first = lo // MOE_SUB
    pieces = jnp.where(hi > lo, (hi + MOE_SUB - 1) // MOE_SUB - first, 0)
    for cnt in range(1, MOE_BLOCK // MOE_SUB + 1):
        @pl.when(pieces == cnt)
        def _(cnt=cnt):
            r0 = 0 if cnt * MOE_SUB == MOE_BLOCK else pl.multiple_of(first * MOE_SUB, MOE_SUB)
            expert_rows(r0, cnt * MOE_SUB)


def _moe_call(sched, xs, wgu, bgu, wd, bd):
    n_items = sched[0].shape[0]
    _, d, de2 = wgu.shape
    de = de2 // 2
    wmap = lambda i, blk, exp, lo, hi, new, init, nxt, slot: (exp[i], 0, 0)
    xmap = lambda i, blk, exp, lo, hi, new, init, nxt, slot: (blk[i], 0)
    return pl.pallas_call(
        _moe_kernel,
        grid_spec=pltpu.PrefetchScalarGridSpec(
            num_scalar_prefetch=8,
            grid=(n_items,),
            in_specs=[
                pl.BlockSpec((MOE_BLOCK, d // 2), xmap),
                pl.BlockSpec(memory_space=pl.ANY),
                pl.BlockSpec((1, 1, de2), wmap),
                pl.BlockSpec(memory_space=pl.ANY),
                pl.BlockSpec((1, 1, d), wmap),
            ],
            out_specs=pl.BlockSpec((MOE_BLOCK, d // 2), xmap),
            scratch_shapes=[
                pltpu.VMEM((2, d, de2), F32), pltpu.VMEM((2, de, d), F32),
                pltpu.SemaphoreType.DMA((2, 2)),
            ],
        ),
        out_shape=jax.ShapeDtypeStruct(xs.shape, xs.dtype),
        compiler_params=pltpu.CompilerParams(
            dimension_semantics=("arbitrary",), vmem_limit_bytes=VMEM_LIMIT),
        name="moe",
    )(*sched, xs, wgu, bgu, wd, bd)


def _moe_schedule(counts, n_rows):
    n_blocks = n_rows // MOE_BLOCK
    n_items = n_blocks + N_EXPERTS
    ends = jnp.cumsum(counts)
    starts = ends - counts
    first_blk = starts // MOE_BLOCK
    last_blk = (ends - 1) // MOE_BLOCK
    items_per = jnp.where(counts > 0, last_blk - first_blk + 1, 0)
    item_ends = jnp.cumsum(items_per)
    item_starts = item_ends - items_per
    total = item_ends[-1]
    it = jnp.arange(n_items, dtype=jnp.int32)
    live = it < total
    itc = jnp.minimum(it, total - 1)
    exp = jnp.sum((item_ends[None, :] <= itc[:, None]).astype(jnp.int32), axis=1)
    is_exp = exp[:, None] == jnp.arange(N_EXPERTS, dtype=jnp.int32)[None, :]
    pick = lambda table: jnp.sum(jnp.where(is_exp, table[None, :], 0), axis=1)
    blk = pick(first_blk) + itc - pick(item_starts)
    lo = jnp.clip(pick(starts) - blk * MOE_BLOCK, 0, MOE_BLOCK)
    hi = jnp.clip(pick(ends) - blk * MOE_BLOCK, 0, MOE_BLOCK)
    hi = jnp.where(live, hi, lo)
    prev_exp = jnp.concatenate([jnp.full((1,), -1, jnp.int32), exp[:-1]])
    prev_blk = jnp.concatenate([jnp.full((1,), -1, jnp.int32), blk[:-1]])
    new = jnp.logical_and(live, exp != prev_exp)
    init = jnp.logical_and(live, blk != prev_blk)
    slot = (jnp.cumsum(new.astype(jnp.int32)) - 1) % 2
    ar = jnp.arange(N_EXPERTS, dtype=jnp.int32)
    later = jnp.logical_and(counts[None, :] > 0, ar[None, :] > ar[:, None])
    next_exp = jnp.min(jnp.where(later, ar[None, :], N_EXPERTS), axis=1)
    next_exp = jnp.where(next_exp == N_EXPERTS, -1, next_exp)
    nxt = pick(next_exp)
    as_i32 = lambda a: a.astype(jnp.int32)
    return tuple(as_i32(a) for a in (blk, exp, lo, hi, new, init, nxt, slot)), starts


def _combine_kernel(*refs):
    yg_refs = refs[:TOP_K]
    x1_ref, gate_ref, mod_ref, fg_ref, o_ref = refs[TOP_K:]
    gates = gate_ref[...]
    y = gates[:, 0:1] * _unpack_bf16_pairs(yg_refs[0][...])
    for kk in range(1, TOP_K):
        y = y + gates[:, kk:kk + 1] * _unpack_bf16_pairs(yg_refs[kk][...])
    m = mod_ref[0]
    x2 = x1_ref[...] + m[5:6] * y
    o_ref[...] = _rms(x2) * fg_ref[...]


def _combine_call(yg, x1, gates, mod, fg, n_seq, part):
    t, d = x1.shape
    tt = COMBINE_TT
    tiles_per_seq = n_seq // tt
    n_tiles = yg.shape[0] // (TOP_K * tt)
    first = part * n_tiles
    tok = lambda i: (first + i, 0)
    slot_specs = [pl.BlockSpec((tt, d // 2), functools.partial(lambda kk, i: (kk * n_tiles + i, 0), kk))
                  for kk in range(TOP_K)]
    return pl.pallas_call(
        _combine_kernel,
        grid=(n_tiles,),
        in_specs=slot_specs + [
            pl.BlockSpec((tt, d), tok),
            pl.BlockSpec((tt, LANES), tok),
            pl.BlockSpec((1, N_MOD, d), lambda i: ((first + i) // tiles_per_seq, 0, 0)),
            pl.BlockSpec((1, d), lambda i: (0, 0)),
        ],
        out_specs=pl.BlockSpec((tt, d), tok),
        out_shape=jax.ShapeDtypeStruct((t, d), F32),
        input_output_aliases={TOP_K: 0},
        compiler_params=pltpu.CompilerParams(
            dimension_semantics=("arbitrary",), vmem_limit_bytes=VMEM_LIMIT),
        name="combine",
    )(*([yg] * TOP_K), x1, gates, mod, fg)


def _pad_cols(a, width):
    return jnp.pad(a, ((0, 0), (0, width - a.shape[1])))


def _rope_slab(w_rope):
    return jnp.pad(w_rope, ((0, 0), (QK_NOPE, LANES - QK_NOPE - QK_ROPE)))


def _prep_w_in(w_in):
    kr0 = Q_LORA + KV_LORA
    w_kr = w_in[:, kr0:kr0 + QK_ROPE]
    return jnp.concatenate(
        [w_in[:, :kr0], _rope_slab(w_kr), w_in[:, kr0 + QK_ROPE:]], axis=1).astype(BF16)


def _prep_w_uq(w_uq):
    per = QK_NOPE + QK_ROPE
    rows = w_uq.shape[0]
    w = w_uq.reshape(rows, N_HEADS, per)
    slab = jnp.pad(w, ((0, 0), (0, 0), (0, LANES - per)))
    return slab.reshape(rows, -1).T.astype(BF16)


def _prep_w_ukv(w_ukv):
    per = QK_NOPE + V_DIM
    rows = w_ukv.shape[0]
    w = w_ukv.reshape(rows, N_HEADS, per)
    wk = jnp.pad(w[:, :, :QK_NOPE], ((0, 0), (0, 0), (0, LANES - QK_NOPE))).reshape(rows, -1)
    wvt = jnp.pad(jnp.transpose(w[:, :, QK_NOPE:], (1, 2, 0)), ((0, 0), (0, VT_ROWS - V_DIM), (0, 0)))
    return wk.astype(BF16), wvt.reshape(N_HEADS * VT_ROWS, rows).astype(BF16)


def _rope_tables(n_lat):
    rows = n_lat // GRID_W
    nf = QK_ROPE // 4
    row = jnp.repeat(jnp.arange(rows, dtype=F32), GRID_W)
    col = jnp.tile(jnp.arange(GRID_W, dtype=F32), rows)
    freqs = ROPE_BASE ** (-jnp.arange(nf, dtype=F32) / nf)
    ang = jnp.concatenate([row[:, None] * freqs, col[:, None] * freqs], axis=-1)
    cos, sin = jnp.cos(ang), jnp.sin(ang)
    ones = jnp.ones((n_lat, QK_NOPE), F32)
    zeros = jnp.zeros((n_lat, QK_NOPE), F32)
    cs = _pad_cols(jnp.concatenate([ones, cos, cos], axis=1), LANES)
    sn = _pad_cols(jnp.concatenate([zeros, -sin, sin], axis=1), LANES)
    return cs, sn


def kernel(x, c, ctx, c_ctx, w_mod, b_mod, norm1_g, w_in, q_norm_g, kv_norm_g, w_uq, w_ukv, w_pool,
           pool_scale, w_out, norm2_g, router_w, router_b, w_gate_up, b_gate_up, w_down, b_down,
           final_g):
    bsz, n, d = x.shape
    n_ctx = ctx.shape[1]
    t = bsz * n
    l = 0

    cc = jnp.concatenate([c, c_ctx[None, :], jnp.zeros((8 - bsz - 1, d), F32)], axis=0)
    mod = _mod_call(cc, w_mod[l], b_mod[l][None, :]).reshape(8, N_MOD, d)
    mod_lat, mod_ctx = mod[:bsz], mod[bsz:bsz + 1]

    win = _prep_w_in(w_in[l])
    wuqt = _prep_w_uq(w_uq[l])
    wk, wvt = _prep_w_ukv(w_ukv[l])
    cs, sn = _rope_tables(n)
    cs_ctx = jnp.broadcast_to((jnp.arange(LANES) < QK_NOPE + QK_ROPE).astype(F32), (n_ctx, LANES))
    sn_ctx = jnp.zeros((n_ctx, LANES), F32)
    g1 = norm1_g[l][None, :]
    qg = q_norm_g[l][None, :]
    kvg = kv_norm_g[l][None, :]

    qt, k, vt, u = _front_call(x, mod_lat, True, g1, win, qg, kvg, wuqt, wk, wvt, cs, sn,
                               is_ctx=False, ts=FRONT_TS)
    kc, vct = _front_call(ctx, mod_ctx, False, g1, win[:, Q_LORA:Q_LORA + 2 * LANES], qg, kvg, wuqt,
                          wk, wvt, cs_ctx, sn_ctx, is_ctx=True, ts=n_ctx)
    att = _attn_call(qt, k, vt, kc, vct)

    rw = _pad_cols(router_w[l], LANES)
    rw_hi = rw.astype(BF16)
    rw = jnp.concatenate([rw_hi, (rw - rw_hi.astype(F32)).astype(BF16)], axis=1)
    rb = jnp.concatenate([router_b[l], jnp.full((LANES - N_EXPERTS,), -jnp.inf, F32)])[None, :]
    x1, h2, gates, ridx, cnt = _mix_call(
        att, u, x, mod_lat, w_pool[l].astype(BF16), pool_scale[l][None, :], w_out[l].astype(BF16),
        norm2_g[l][None, :], rw, rb)

    counts = cnt[0, :N_EXPERTS].astype(jnp.int32)
    sched, starts = _moe_schedule(counts, t * TOP_K)
    ridx = ridx.reshape(t, LANES)
    is_exp = ridx[:, :TOP_K, None] == jnp.arange(N_EXPERTS, dtype=jnp.int32)
    dest = jnp.sum(jnp.where(is_exp, starts, 0), axis=-1) + ridx[:, TOP_K:2 * TOP_K]
    dest_t = dest.T

    xs = _scatter_rows(h2.reshape(t, d // 2), dest_t.reshape(TOP_K, t // SC_ROWS, SC_ROWS), t * TOP_K)
    ys = _moe_call(sched, xs, w_gate_up[l], b_gate_up[l][:, None, :], w_down[l],
                   b_down[l][:, None, :])
    out = x1.reshape(t, d)
    gates = gates.reshape(t, LANES)
    tp = t // COMBINE_PARTS
    for part in range(COMBINE_PARTS):
        idx = dest_t[:, part * tp:(part + 1) * tp].reshape(-1)
        yg = _gather_rows(ys, idx)
        out = _combine_call(yg, out, gates, mod_lat, final_g[None, :], n, part)
    return out.reshape(bsz, n, d)
```

```python
import functools
import math

import jax
import jax.numpy as jnp
from jax import lax
from jax.experimental import pallas as pl
from jax.experimental.pallas import tpu as pltpu
from jax.experimental.pallas import tpu_sc as plsc

F32 = jnp.float32
BF16 = jnp.bfloat16
HIGHEST = lax.Precision.HIGHEST

N_HEADS = 8
QK_NOPE = 64
QK_ROPE = 32
V_DIM = 64
Q_LORA = 256
KV_LORA = 128
GRID_W = 64
ROPE_BASE = 10000.0
POOL_WINDOWS = (2, 4, 8, 16)
POOL_CH = 128
N_EXPERTS = 32
TOP_K = 4
SWIGLU_LIMIT = 7.0
SWIGLU_ALPHA = 1.702
N_MOD = 6
EPS = 1e-6

LANES = 128
SUBLANES = 8
BF16_SUBLANES = 16
MIB = 1024 * 1024

POOL_HALO = max(POOL_WINDOWS) // 2
VT_ROWS = -(-(V_DIM + 1) // BF16_SUBLANES) * BF16_SUBLANES

FRONT_TS = 1024
ATTN_TQ = 512
ATTN_TILES = 4
ATTN_KC = 256
MIX_TS = 512
MOE_BLOCK = 512
MOE_SUB = 64
COMBINE_TT = 512
COMBINE_PARTS = 4
SC_ROWS = 128

VMEM_SMALL = 32 * MIB
VMEM_LARGE = 56 * MIB


def _dot(a, b, **kw):
    return jnp.dot(a, b, preferred_element_type=F32, **kw)


def _rms(x):
    return x * lax.rsqrt(jnp.mean(x * x, axis=-1, keepdims=True) + EPS)


def _pack_bf16_pairs(x):
    w = x.shape[1] // 2
    return pltpu.pack_elementwise([x[:, :w], x[:, w:]], packed_dtype=BF16)


def _unpack_bf16_pairs(words):
    halves = [pltpu.unpack_elementwise(words, index=h, packed_dtype=BF16, unpacked_dtype=F32)
              for h in range(2)]
    return jnp.concatenate(halves, axis=-1)


def _mod_kernel(c_ref, w_ref, b_ref, o_ref):
    c = c_ref[...]
    a = c / (1.0 + jnp.exp(-c))
    o_ref[...] = _dot(a, w_ref[...], precision=HIGHEST) + b_ref[...]


def _mod_call(cc, w_mod, b_mod):
    d = w_mod.shape[0]
    return pl.pallas_call(
        _mod_kernel,
        grid=(N_MOD,),
        in_specs=[
            pl.BlockSpec((SUBLANES, d), lambda j: (0, 0)),
            pl.BlockSpec((d, d), lambda j: (0, j)),
            pl.BlockSpec((1, d), lambda j: (0, j)),
        ],
        out_specs=pl.BlockSpec((SUBLANES, d), lambda j: (0, j)),
        out_shape=jax.ShapeDtypeStruct((SUBLANES, N_MOD * d), F32),
        compiler_params=pltpu.CompilerParams(
            dimension_semantics=("arbitrary",), vmem_limit_bytes=VMEM_SMALL),
        name="mod",
    )(cc, w_mod, b_mod)


def _front_kernel(x_ref, mod_ref, g1_ref, win_ref, qg_ref, kvg_ref, wuqt_ref, wk_ref, wvt_ref,
                  cs_ref, sn_ref, cst_ref, snt_ref, *out_refs, is_ctx, scale):
    x = x_ref[0]
    m = mod_ref[0]
    h = _rms(x) * g1_ref[...] * (1.0 + m[1:2]) + m[0:1]
    z = _dot(h.astype(BF16), win_ref[...])
    if is_ctx:
        k_ref, vt_ref = out_refs
        zkv = z
    else:
        qt_ref, k_ref, vt_ref, u_ref = out_refs
        zkv = z[:, Q_LORA:Q_LORA + 2 * LANES]
        u_ref[0] = z[:, Q_LORA + 2 * LANES:]

    ckv = _rms(zkv[:, :KV_LORA]) * kvg_ref[...]
    kk = _dot(ckv.astype(BF16), wk_ref[...])
    kra = zkv[:, LANES:2 * LANES]
    lane = lax.broadcasted_iota(jnp.int32, kra.shape, 1)
    half = QK_ROPE // 2
    partner = jnp.where(lane < QK_NOPE + half, pltpu.roll(kra, LANES - half, axis=1),
                        pltpu.roll(kra, half, axis=1))
    kr = kra * cs_ref[...] + partner * sn_ref[...]
    for hd in range(N_HEADS):
        k_ref[0, hd] = (kk[:, hd * LANES:(hd + 1) * LANES] + kr).astype(BF16)
    vt = _dot(wvt_ref[...], ckv.T.astype(BF16))
    row = lax.broadcasted_iota(jnp.int32, (VT_ROWS, 1), 0)
    ones = jnp.where(row == V_DIM, 1.0, 0.0)
    for hd in range(N_HEADS):
        vt_ref[0, hd] = (vt[hd * VT_ROWS:(hd + 1) * VT_ROWS] + ones).astype(BF16)

    if not is_ctx:
        cq = _rms(z[:, :Q_LORA]) * qg_ref[...]
        qqt = _dot(wuqt_ref[...], cq.T.astype(BF16))
        cst = cst_ref[...]
        snt = snt_ref[...]
        r1, r2, r3 = QK_NOPE, QK_NOPE + QK_ROPE // 2, QK_NOPE + QK_ROPE
        for hd in range(N_HEADS):
            qa = qqt[hd * LANES:(hd + 1) * LANES]
            qb = jnp.concatenate([qa[:r1], qa[r2:r3], qa[r1:r2], qa[r3:]], axis=0)
            qt_ref[0, hd] = ((qa * cst + qb * snt) * scale).astype(BF16)


def _front_call(xs, mod, mod_per_batch, g1, win, qg, kvg, wuqt, wk, wvt, cs, sn, *, is_ctx, ts):
    bsz, n, d = xs.shape
    nt = n // ts
    scale = math.log2(math.e) / math.sqrt(QK_NOPE + QK_ROPE)
    const = lambda b, i: (0, 0)
    mod_map = (lambda b, i: (b, 0, 0)) if mod_per_batch else (lambda b, i: (0, 0, 0))
    in_specs = [
        pl.BlockSpec((1, ts, d), lambda b, i: (b, i, 0)),
        pl.BlockSpec((1, N_MOD, d), mod_map),
        pl.BlockSpec((1, d), const),
        pl.BlockSpec(win.shape, const),
        pl.BlockSpec((1, Q_LORA), const),
        pl.BlockSpec((1, KV_LORA), const),
        pl.BlockSpec(wuqt.shape, const),
        pl.BlockSpec(wk.shape, const),
        pl.BlockSpec(wvt.shape, const),
        pl.BlockSpec((ts, LANES), lambda b, i: (i, 0)),
        pl.BlockSpec((ts, LANES), lambda b, i: (i, 0)),
        pl.BlockSpec((LANES, ts), lambda b, i: (0, i)),
        pl.BlockSpec((LANES, ts), lambda b, i: (0, i)),
    ]
    k_spec = pl.BlockSpec((1, N_HEADS, ts, LANES), lambda b, i: (b, 0, i, 0))
    k_shape = jax.ShapeDtypeStruct((bsz, N_HEADS, n, LANES), BF16)
    qt_spec = pl.BlockSpec((1, N_HEADS, LANES, ts), lambda b, i: (b, 0, 0, i))
    qt_shape = jax.ShapeDtypeStruct((bsz, N_HEADS, LANES, n), BF16)
    vt_spec = pl.BlockSpec((1, N_HEADS, VT_ROWS, ts), lambda b, i: (b, 0, 0, i))
    vt_shape = jax.ShapeDtypeStruct((bsz, N_HEADS, VT_ROWS, n), BF16)
    if is_ctx:
        out_specs = [k_spec, vt_spec]
        out_shape = [k_shape, vt_shape]
    else:
        pool_w = win.shape[1] - Q_LORA - 2 * LANES
        out_specs = [qt_spec, k_spec, vt_spec, pl.BlockSpec((1, ts, pool_w), lambda b, i: (b, i, 0))]
        out_shape = [qt_shape, k_shape, vt_shape, jax.ShapeDtypeStruct((bsz, n, pool_w), F32)]
    return pl.pallas_call(
        functools.partial(_front_kernel, is_ctx=is_ctx, scale=scale),
        grid=(bsz, nt),
        in_specs=in_specs,
        out_specs=out_specs,
        out_shape=out_shape,
        compiler_params=pltpu.CompilerParams(
            dimension_semantics=("arbitrary", "arbitrary"), vmem_limit_bytes=VMEM_LARGE),
        name="front_ctx" if is_ctx else "front",
    )(xs, mod, g1, win, qg, kvg, wuqt, wk, wvt, cs, sn, cs.T, sn.T)


def _attn_kernel(qt_ref, k_ref, vt_ref, kc_ref, vct_ref, o_ref, s_ref, mlc_ref, mxb_ref, oe_ref):
    n_ctx = kc_ref.shape[2]
    n_lat = k_ref.shape[2]
    chunks = [(None, 0, n_ctx)] + [(c * ATTN_KC, n_ctx + c * ATTN_KC, ATTN_KC)
                                   for c in range(n_lat // ATTN_KC)]
    sub = SUBLANES

    def q_cols(tile):
        if isinstance(tile, int):
            return pl.ds(tile * ATTN_TQ, ATTN_TQ)
        return pl.ds(pl.multiple_of(tile * ATTN_TQ, ATTN_TQ), ATTN_TQ)

    def score_chunk(tile, hd, buf, ci):
        off, soff, w = chunks[ci]
        keys = kc_ref[0, hd] if off is None else k_ref[0, hd, off:off + w, :]
        s = _dot(keys, qt_ref[0, hd, :, q_cols(tile)])
        s_ref[buf, soff:soff + w, :] = s
        mx = s[0:sub]
        for r in range(1, w // sub):
            mx = jnp.maximum(mx, s[r * sub:(r + 1) * sub])
        mlc_ref[buf, ci] = mx

    def row_max(buf):
        mx = mlc_ref[buf, 0]
        for ci in range(1, len(chunks)):
            mx = jnp.maximum(mx, mlc_ref[buf, ci])
        mxb_ref[...] = jnp.broadcast_to(jnp.max(mx, axis=0, keepdims=True), mx.shape)

    def weight_chunk(hd, buf, ci):
        off, soff, w = chunks[ci]
        p = jnp.exp2(s_ref[buf, soff:soff + w, :] - mxb_ref[0:1, :]).astype(BF16)
        vt = vct_ref[0, hd] if off is None else vt_ref[0, hd, :, off:off + w]
        return _dot(vt, p)

    def stage(hw, bw, scoring, bs):
        row_max(bw)
        acc = None
        for ci in range(len(chunks)):
            if scoring is not None:
                score_chunk(*scoring, bs, ci)
            part = weight_chunk(hw, bw, ci)
            acc = part if acc is None else acc + part
        return acc[0:V_DIM] / acc[V_DIM:V_DIM + 1]

    def write_pair(tile, j, ot_odd):
        pair_t = jnp.concatenate([oe_ref[...], ot_odd], axis=0)
        o_ref[0, j, q_cols(tile), :] = pair_t.T.astype(BF16)

    pairs_per_tile = N_HEADS // 2
    n_pairs = (qt_ref.shape[3] // ATTN_TQ) * pairs_per_tile

    def split(p):
        return p // pairs_per_tile, p % pairs_per_tile

    for ci in range(len(chunks)):
        score_chunk(0, 0, 0, ci)

    def head_pair(p, carry):
        tile, j = split(p)
        nxt_tile, nxt_j = split(p + 1)
        oe_ref[...] = stage(2 * j, 0, (tile, 2 * j + 1), 1)
        write_pair(tile, j, stage(2 * j + 1, 1, (nxt_tile, 2 * nxt_j), 0))
        return carry

    lax.fori_loop(0, n_pairs - 1, head_pair, 0)
    tile, j = split(n_pairs - 1)
    oe_ref[...] = stage(2 * j, 0, (tile, 2 * j + 1), 1)
    write_pair(tile, j, stage(2 * j + 1, 1, None, None))


def _attn_call(qt, k, vt, kc, vct):
    bsz, _, _, n = qt.shape
    n_ctx = kc.shape[2]
    tq = ATTN_TQ
    n_chunks = 1 + n // ATTN_KC
    per_batch = lambda b, i: (b, 0, 0, 0)
    resident = dict(pipeline_mode=pl.Buffered(1))
    rows_per_step = ATTN_TILES * tq
    return pl.pallas_call(
        _attn_kernel,
        grid=(bsz, n // rows_per_step),
        in_specs=[
            pl.BlockSpec((1, N_HEADS, LANES, rows_per_step), lambda b, i: (b, 0, 0, i)),
            pl.BlockSpec((1, N_HEADS, n, LANES), per_batch, **resident),
            pl.BlockSpec((1, N_HEADS, VT_ROWS, n), per_batch, **resident),
            pl.BlockSpec((1, N_HEADS, n_ctx, LANES), per_batch, **resident),
            pl.BlockSpec((1, N_HEADS, VT_ROWS, n_ctx), per_batch, **resident),
        ],
        out_specs=pl.BlockSpec((1, N_HEADS // 2, rows_per_step, LANES), lambda b, i: (b, 0, i, 0)),
        out_shape=jax.ShapeDtypeStruct((bsz, N_HEADS // 2, n, LANES), BF16),
        scratch_shapes=[
            pltpu.VMEM((2, n_ctx + n, tq), F32),
            pltpu.VMEM((2, n_chunks, SUBLANES, tq), F32),
            pltpu.VMEM((SUBLANES, tq), F32),
            pltpu.VMEM((V_DIM, tq), F32),
        ],
        compiler_params=pltpu.CompilerParams(
            dimension_semantics=("arbitrary", "arbitrary"), vmem_limit_bytes=VMEM_LARGE),
        name="attn",
    )(qt, k, vt, kc, vct)


def _mix_kernel(att_ref, u_ref, up_ref, un_ref, x_ref, mod_ref, wpool_ref, pscale_ref, wout_ref,
                g2_ref, rw_ref, rb_ref,
                x1_ref, h2_ref, gate_ref, ridx_ref, cnt_ref, ue_ref, base_ref, *, n_seq):
    b = pl.program_id(0)
    i = pl.program_id(1)
    n_tiles = pl.num_programs(1)
    ts = u_ref.shape[1]

    @pl.when(jnp.logical_and(b == 0, i == 0))
    def _():
        base_ref[...] = jnp.zeros_like(base_ref)

    u = u_ref[0]
    ue_ref[0:POOL_HALO] = jnp.where(i > 0, up_ref[0], 0.0)
    ue_ref[POOL_HALO:POOL_HALO + ts] = u
    ue_ref[POOL_HALO + ts:2 * POOL_HALO + ts] = jnp.where(i < n_tiles - 1, un_ref[0], 0.0)
    t = i * ts + lax.broadcasted_iota(jnp.int32, (ts, 1), 0)
    ys = []
    for g, w in enumerate(POOL_WINDOWS):
        half = w // 2
        lanes = slice(g * POOL_CH, (g + 1) * POOL_CH)
        ws = ue_ref[POOL_HALO - half:POOL_HALO - half + ts, lanes]
        for jj in range(-half + 1, half):
            ws = ws + ue_ref[POOL_HALO + jj:POOL_HALO + jj + ts, lanes]
        count = (jnp.minimum(t + half, n_seq) - jnp.maximum(t - half, 0)).astype(F32)
        mixed = (ws / count - u[:, lanes]).astype(BF16)
        ys.append(_dot(mixed, wpool_ref[g]))
    pool = jnp.concatenate(ys, axis=-1) * pscale_ref[...]

    cat = jnp.concatenate([att_ref[0, j] for j in range(N_HEADS // 2)] + [pool.astype(BF16)], axis=-1)
    m = mod_ref[0]
    x1 = x_ref[0] + m[2:3] * _dot(cat, wout_ref[...])
    x1_ref[0] = x1
    h2 = _rms(x1) * g2_ref[...] * (1.0 + m[4:5]) + m[3:4]
    h2_ref[0] = _pack_bf16_pairs(h2)

    h_hi = h2.astype(BF16)
    h_lo = (h2 - h_hi.astype(F32)).astype(BF16)
    hi_part = _dot(h_hi, rw_ref[...])
    logits = (hi_part[:, :LANES] + hi_part[:, LANES:] + _dot(h_lo, rw_ref[:, :LANES])) + rb_ref[...]
    lane = lax.broadcasted_iota(jnp.int32, logits.shape, 1).astype(F32)
    vals, idxs = [], []
    for _k in range(TOP_K):
        mv = jnp.max(logits, axis=-1, keepdims=True)
        ix = jnp.min(jnp.where(logits == mv, lane, float(LANES)), axis=-1, keepdims=True)
        vals.append(mv)
        idxs.append(ix)
        logits = jnp.where(lane == ix, -jnp.inf, logits)
    es = [jnp.exp(v - vals[0]) for v in vals]
    den = es[0] + es[1] + es[2] + es[3]

    onehot = jnp.zeros(lane.shape, F32)
    for ix in idxs:
        onehot = onehot + jnp.where(lane == ix, 1.0, 0.0)
    row = lax.broadcasted_iota(jnp.int32, (ts, ts), 0)
    col = lax.broadcasted_iota(jnp.int32, (ts, ts), 1)
    tri = jnp.where(col < row, 1.0, 0.0).astype(BF16)
    before = _dot(tri, onehot.astype(BF16)) + base_ref[0:1, :]
    base_new = base_ref[0:1, :] + jnp.sum(onehot, axis=0, keepdims=True)
    base_ref[...] = jnp.broadcast_to(base_new, base_ref.shape)
    cnt_ref[...] = jnp.broadcast_to(base_new, cnt_ref.shape)

    gate_out = jnp.zeros(lane.shape, F32)
    ridx_out = jnp.zeros(lane.shape, F32)
    for kk in range(TOP_K):
        rank = jnp.sum(jnp.where(lane == idxs[kk], before, 0.0), axis=-1, keepdims=True)
        gate_out = jnp.where(lane == float(kk), es[kk] / den, gate_out)
        ridx_out = jnp.where(lane == float(kk), idxs[kk], ridx_out)
        ridx_out = jnp.where(lane == float(TOP_K + kk), rank, ridx_out)
    gate_ref[0] = gate_out
    ridx_ref[0] = ridx_out.astype(jnp.int32)


def _mix_call(att, u, x, mod, wpool, pscale, wout, g2, rw, rb):
    bsz, n, d = x.shape
    ts = MIX_TS
    nt = n // ts
    pool_w = u.shape[2]
    hb = ts // POOL_HALO
    n_halo_blocks = n // POOL_HALO
    const2 = lambda b, i: (0, 0)
    tok = lambda b, i: (b, i, 0)
    return pl.pallas_call(
        functools.partial(_mix_kernel, n_seq=n),
        grid=(bsz, nt),
        in_specs=[
            pl.BlockSpec((1, N_HEADS // 2, ts, LANES), lambda b, i: (b, 0, i, 0)),
            pl.BlockSpec((1, ts, pool_w), tok),
            pl.BlockSpec((1, POOL_HALO, pool_w), lambda b, i: (b, jnp.maximum(i * hb - 1, 0), 0)),
            pl.BlockSpec((1, POOL_HALO, pool_w),
                         lambda b, i: (b, jnp.minimum((i + 1) * hb, n_halo_blocks - 1), 0)),
            pl.BlockSpec((1, ts, d), tok),
            pl.BlockSpec((1, N_MOD, d), lambda b, i: (b, 0, 0)),
            pl.BlockSpec(wpool.shape, lambda b, i: (0, 0, 0)),
            pl.BlockSpec((1, pool_w), const2),
            pl.BlockSpec(wout.shape, const2),
            pl.BlockSpec((1, d), const2),
            pl.BlockSpec(rw.shape, const2),
            pl.BlockSpec((1, LANES), const2),
        ],
        out_specs=[
            pl.BlockSpec((1, ts, d), tok),
            pl.BlockSpec((1, ts, d // 2), tok),
            pl.BlockSpec((1, ts, LANES), tok),
            pl.BlockSpec((1, ts, LANES), tok),
            pl.BlockSpec((SUBLANES, LANES), const2),
        ],
        out_shape=[
            jax.ShapeDtypeStruct((bsz, n, d), F32),
            jax.ShapeDtypeStruct((bsz, n, d // 2), jnp.uint32),
            jax.ShapeDtypeStruct((bsz, n, LANES), F32),
            jax.ShapeDtypeStruct((bsz, n, LANES), jnp.int32),
            jax.ShapeDtypeStruct((SUBLANES, LANES), F32),
        ],
        scratch_shapes=[
            pltpu.VMEM((ts + 2 * POOL_HALO, pool_w), F32),
            pltpu.VMEM((SUBLANES, LANES), F32),
        ],
        compiler_params=pltpu.CompilerParams(
            dimension_semantics=("arbitrary", "arbitrary"), vmem_limit_bytes=VMEM_SMALL),
        name="mix",
    )(att, u, u, u, x, mod, wpool, pscale, wout, g2, rw, rb)


def _sc_mesh():
    return plsc.VectorSubcoreMesh(core_axis_name="core", subcore_axis_name="subcore")


def _sc_worker_id():
    info = plsc.get_sparse_core_info()
    return lax.axis_index("subcore") * info.num_cores + lax.axis_index("core")


def _sc_num_workers():
    info = plsc.get_sparse_core_info()
    return info.num_cores * info.num_subcores


def _scatter_rows(x, dest3, n_out):
    t, d = x.shape
    top_k, n_chunks, rows = dest3.shape
    assert n_chunks % _sc_num_workers() == 0 and n_chunks * rows == t
    per_worker = n_chunks // _sc_num_workers()

    @functools.partial(
        pl.kernel,
        out_type=jax.ShapeDtypeStruct((n_out, d), x.dtype),
        mesh=_sc_mesh(),
        scratch_types=[pltpu.VMEM((top_k, rows), jnp.int32), pltpu.VMEM((rows, d), x.dtype)],
        name="sc_scatter",
    )
    def scatter(x_hbm, i_hbm, o_hbm, idx_v, rows_v):
        wid = _sc_worker_id()

        @pl.loop(0, per_worker)
        def _(c):
            chunk = wid * per_worker + c
            for kk in range(top_k):
                pltpu.sync_copy(i_hbm.at[kk, chunk], idx_v.at[kk])
            pltpu.sync_copy(x_hbm.at[pl.ds(pl.multiple_of(chunk * rows, SUBLANES), rows)], rows_v)
            for kk in range(top_k):
                pltpu.sync_copy(rows_v, o_hbm.at[idx_v.at[kk]])

    return scatter(x, dest3)


def _gather_rows(y, idx):
    n = idx.shape[0]
    d = y.shape[1]
    rows = SC_ROWS
    assert n % (rows * _sc_num_workers()) == 0
    per_worker = n // _sc_num_workers()
    n_chunks = per_worker // rows

    @functools.partial(
        pl.kernel,
        out_type=jax.ShapeDtypeStruct((n, d), y.dtype),
        mesh=_sc_mesh(),
        scratch_types=[pltpu.VMEM((rows,), jnp.int32), pltpu.VMEM((rows, d), y.dtype)],
        name="sc_gather",
    )
    def gather(y_hbm, i_hbm, o_hbm, idx_v, rows_v):
        base = _sc_worker_id() * per_worker

        @pl.loop(0, n_chunks)
        def _(c):
            off = pl.multiple_of(base + c * rows, SUBLANES)
            pltpu.sync_copy(i_hbm.at[pl.ds(off, rows)], idx_v)
            pltpu.sync_copy(y_hbm.at[idx_v], rows_v)
            pltpu.sync_copy(rows_v, o_hbm.at[pl.ds(off, rows)])

    return gather(y, idx)


def _moe_kernel(blk_ref, exp_ref, lo_ref, hi_ref, new_ref, init_ref, next_ref, slot_ref, xs_ref,
                wgu_hbm, bgu_ref, wd_hbm, bd_ref, ys_ref, wgu_buf, wd_buf, sems):
    i = pl.program_id(0)
    slot = slot_ref[i]

    def weight_copies(e, s):
        return (pltpu.make_async_copy(wgu_hbm.at[e], wgu_buf.at[s], sems.at[s, 0]),
                pltpu.make_async_copy(wd_hbm.at[e], wd_buf.at[s], sems.at[s, 1]))

    @pl.when(i == 0)
    def _():
        for cp in weight_copies(exp_ref[0], slot):
            cp.start()

    @pl.when(new_ref[i] == 1)
    def _():
        for cp in weight_copies(exp_ref[i], slot):
            cp.wait()

        @pl.when(next_ref[i] >= 0)
        def _():
            for cp in weight_copies(next_ref[i], 1 - slot):
                cp.start()

    @pl.when(init_ref[i] == 1)
    def _():
        ys_ref[...] = jnp.zeros_like(ys_ref)

    lo = lo_ref[i]
    hi = hi_ref[i]

    def expert_rows(r0, n_rows):
        de = wd_buf.shape[1]
        x = _unpack_bf16_pairs(xs_ref[pl.ds(r0, n_rows), :])
        gu = _dot(x, wgu_buf[slot]) + bgu_ref[0]
        g = jnp.minimum(gu[:, :de], SWIGLU_LIMIT)
        lin = jnp.clip(gu[:, de:], -SWIGLU_LIMIT, SWIGLU_LIMIT)
        act = g / (1.0 + jnp.exp(-SWIGLU_ALPHA * g)) * (lin + 1.0)
        y = _dot(act, wd_buf[slot]) + bd_ref[0]
        row = r0 + lax.broadcasted_iota(jnp.int32, (n_rows, 1), 0)
        mine = jnp.logical_and(row >= lo, row < hi)
        ys_ref[pl.ds(r0, n_rows), :] = jnp.where(mine, _pack_bf16_pairs(y), ys_ref[pl.ds(r0, n_rows), :])

    first = lo // MOE_SUB
    pieces = jnp.where(hi > lo, (hi + MOE_SUB - 1) // MOE_SUB - first, 0)
    for cnt in range(1, MOE_BLOCK // MOE_SUB + 1):
        @pl.when(pieces == cnt)
        def _(cnt=cnt):
            r0 = 0 if cnt * MOE_SUB == MOE_BLOCK else pl.multiple_of(first * MOE_SUB, MOE_SUB)
            expert_rows(r0, cnt * MOE_SUB)


def _moe_call(sched, xs, wgu, bgu, wd, bd):
    n_items = sched[0].shape[0]
    _, d, de2 = wgu.shape
    de = de2 // 2
    wmap = lambda i, blk, exp, lo, hi, new, init, nxt, slot: (exp[i], 0, 0)
    xmap = lambda i, blk, exp, lo, hi, new, init, nxt, slot: (blk[i], 0)
    return pl.pallas_call(
        _moe_kernel,
        grid_spec=pltpu.PrefetchScalarGridSpec(
            num_scalar_prefetch=8,
            grid=(n_items,),
            in_specs=[
                pl.BlockSpec((MOE_BLOCK, d // 2), xmap),
                pl.BlockSpec(memory_space=pl.ANY),
                pl.BlockSpec((1, 1, de2), wmap),
                pl.BlockSpec(memory_space=pl.ANY),
                pl.BlockSpec((1, 1, d), wmap),
            ],
            out_specs=pl.BlockSpec((MOE_BLOCK, d // 2), xmap),
            scratch_shapes=[
                pltpu.VMEM((2, d, de2), F32), pltpu.VMEM((2, de, d), F32),
                pltpu.SemaphoreType.DMA((2, 2)),
            ],
        ),
        out_shape=jax.ShapeDtypeStruct(xs.shape, xs.dtype),
        compiler_params=pltpu.CompilerParams(
            dimension_semantics=("arbitrary",), vmem_limit_bytes=VMEM_LARGE),
        name="moe",
    )(*sched, xs, wgu, bgu, wd, bd)


def _moe_schedule(counts, n_rows):
    n_blocks = n_rows // MOE_BLOCK
    n_items = n_blocks + N_EXPERTS
    ends = jnp.cumsum(counts)
    starts = ends - counts
    first_blk = starts // MOE_BLOCK
    last_blk = (ends - 1) // MOE_BLOCK
    items_per = jnp.where(counts > 0, last_blk - first_blk + 1, 0)
    item_ends = jnp.cumsum(items_per)
    item_starts = item_ends - items_per
    total = item_ends[-1]
    it = jnp.arange(n_items, dtype=jnp.int32)
    live = it < total
    itc = jnp.minimum(it, total - 1)
    exp = jnp.sum((item_ends[None, :] <= itc[:, None]).astype(jnp.int32), axis=1)
    is_exp = exp[:, None] == jnp.arange(N_EXPERTS, dtype=jnp.int32)[None, :]
    pick = lambda table: jnp.sum(jnp.where(is_exp, table[None, :], 0), axis=1)
    blk = pick(first_blk) + itc - pick(item_starts)
    lo = jnp.clip(pick(starts) - blk * MOE_BLOCK, 0, MOE_BLOCK)
    hi = jnp.clip(pick(ends) - blk * MOE_BLOCK, 0, MOE_BLOCK)
    hi = jnp.where(live, hi, lo)
    prev_exp = jnp.concatenate([jnp.full((1,), -1, jnp.int32), exp[:-1]])
    prev_blk = jnp.concatenate([jnp.full((1,), -1, jnp.int32), blk[:-1]])
    new = jnp.logical_and(live, exp != prev_exp)
    init = jnp.logical_and(live, blk != prev_blk)
    slot = (jnp.cumsum(new.astype(jnp.int32)) - 1) % 2
    ar = jnp.arange(N_EXPERTS, dtype=jnp.int32)
    later = jnp.logical_and(counts[None, :] > 0, ar[None, :] > ar[:, None])
    next_exp = jnp.min(jnp.where(later, ar[None, :], N_EXPERTS), axis=1)
    next_exp = jnp.where(next_exp == N_EXPERTS, -1, next_exp)
    nxt = pick(next_exp)
    as_i32 = lambda a: a.astype(jnp.int32)
    return tuple(as_i32(a) for a in (blk, exp, lo, hi, new, init, nxt, slot)), starts


def _combine_kernel(*refs):
    yg_refs = refs[:TOP_K]
    x1_ref, gate_ref, mod_ref, fg_ref, o_ref = refs[TOP_K:]
    gates = gate_ref[...]
    y = gates[:, 0:1] * _unpack_bf16_pairs(yg_refs[0][...])
    for kk in range(1, TOP_K):
        y = y + gates[:, kk:kk + 1] * _unpack_bf16_pairs(yg_refs[kk][...])
    m = mod_ref[0]
    x2 = x1_ref[...] + m[5:6] * y
    o_ref[...] = _rms(x2) * fg_ref[...]


def _combine_call(yg, x1, gates, mod, fg, n_seq, part):
    t, d = x1.shape
    tt = COMBINE_TT
    tiles_per_seq = n_seq // tt
    n_tiles = yg.shape[0] // (TOP_K * tt)
    first = part * n_tiles
    tok = lambda i: (first + i, 0)
    slot_specs = [pl.BlockSpec((tt, d // 2), functools.partial(lambda kk, i: (kk * n_tiles + i, 0), kk))
                  for kk in range(TOP_K)]
    return pl.pallas_call(
        _combine_kernel,
        grid=(n_tiles,),
        in_specs=slot_specs + [
            pl.BlockSpec((tt, d), tok),
            pl.BlockSpec((tt, LANES), tok),
            pl.BlockSpec((1, N_MOD, d), lambda i: ((first + i) // tiles_per_seq, 0, 0)),
            pl.BlockSpec((1, d), lambda i: (0, 0)),
        ],
        out_specs=pl.BlockSpec((tt, d), tok),
        out_shape=jax.ShapeDtypeStruct((t, d), F32),
        input_output_aliases={TOP_K: 0},
        compiler_params=pltpu.CompilerParams(
            dimension_semantics=("arbitrary",), vmem_limit_bytes=VMEM_SMALL),
        name="combine",
    )(*([yg] * TOP_K), x1, gates, mod, fg)


def _pad_cols(a, width):
    return jnp.pad(a, ((0, 0), (0, width - a.shape[1])))


def _rope_slab(w_rope):
    return jnp.pad(w_rope, ((0, 0), (QK_NOPE, LANES - QK_NOPE - QK_ROPE)))


def _prep_w_in(w_in):
    kr0 = Q_LORA + KV_LORA
    w_kr = w_in[:, kr0:kr0 + QK_ROPE]
    return jnp.concatenate(
        [w_in[:, :kr0], _rope_slab(w_kr), w_in[:, kr0 + QK_ROPE:]], axis=1).astype(BF16)


def _prep_w_uq(w_uq):
    per = QK_NOPE + QK_ROPE
    rows = w_uq.shape[0]
    w = w_uq.reshape(rows, N_HEADS, per)
    slab = jnp.pad(w, ((0, 0), (0, 0), (0, LANES - per)))
    return slab.reshape(rows, -1).T.astype(BF16)


def _prep_w_ukv(w_ukv):
    per = QK_NOPE + V_DIM
    rows = w_ukv.shape[0]
    w = w_ukv.reshape(rows, N_HEADS, per)
    wk = jnp.pad(w[:, :, :QK_NOPE], ((0, 0), (0, 0), (0, LANES - QK_NOPE))).reshape(rows, -1)
    wvt = jnp.pad(jnp.transpose(w[:, :, QK_NOPE:], (1, 2, 0)), ((0, 0), (0, VT_ROWS - V_DIM), (0, 0)))
    return wk.astype(BF16), wvt.reshape(N_HEADS * VT_ROWS, rows).astype(BF16)


def _rope_tables(n_lat):
    rows = n_lat // GRID_W
    nf = QK_ROPE // 4
    row = jnp.repeat(jnp.arange(rows, dtype=F32), GRID_W)
    col = jnp.tile(jnp.arange(GRID_W, dtype=F32), rows)
    freqs = ROPE_BASE ** (-jnp.arange(nf, dtype=F32) / nf)
    ang = jnp.concatenate([row[:, None] * freqs, col[:, None] * freqs], axis=-1)
    cos, sin = jnp.cos(ang), jnp.sin(ang)
    ones = jnp.ones((n_lat, QK_NOPE), F32)
    zeros = jnp.zeros((n_lat, QK_NOPE), F32)
    cs = _pad_cols(jnp.concatenate([ones, cos, cos], axis=1), LANES)
    sn = _pad_cols(jnp.concatenate([zeros, -sin, sin], axis=1), LANES)
    return cs, sn


def kernel(x, c, ctx, c_ctx, w_mod, b_mod, norm1_g, w_in, q_norm_g, kv_norm_g, w_uq, w_ukv, w_pool,
           pool_scale, w_out, norm2_g, router_w, router_b, w_gate_up, b_gate_up, w_down, b_down,
           final_g):
    bsz, n, d = x.shape
    n_ctx = ctx.shape[1]
    t = bsz * n
    assert w_mod.shape[0] == 1, "single-layer block"
    l = 0
    tp = t // COMBINE_PARTS
    assert bsz + 1 <= SUBLANES and d % (2 * LANES) == 0 and n_ctx % ATTN_KC == 0
    assert n % GRID_W == 0 and n % FRONT_TS == 0 and n % MIX_TS == 0 and n % ATTN_KC == 0
    assert n % (ATTN_TQ * ATTN_TILES) == 0 and (t * TOP_K) % MOE_BLOCK == 0 and MOE_BLOCK % MOE_SUB == 0
    assert t % COMBINE_PARTS == 0 and tp % COMBINE_TT == 0 and n % COMBINE_TT == 0 and t % SC_ROWS == 0

    cc = jnp.concatenate([c, c_ctx[None, :], jnp.zeros((SUBLANES - bsz - 1, d), F32)], axis=0)
    mod = _mod_call(cc, w_mod[l], b_mod[l][None, :]).reshape(SUBLANES, N_MOD, d)
    mod_lat, mod_ctx = mod[:bsz], mod[bsz:bsz + 1]

    win = _prep_w_in(w_in[l])
    wuqt = _prep_w_uq(w_uq[l])
    wk, wvt = _prep_w_ukv(w_ukv[l])
    cs, sn = _rope_tables(n)
    cs_ctx = jnp.broadcast_to((jnp.arange(LANES) < QK_NOPE + QK_ROPE).astype(F32), (n_ctx, LANES))
    sn_ctx = jnp.zeros((n_ctx, LANES), F32)
    g1 = norm1_g[l][None, :]
    qg = q_norm_g[l][None, :]
    kvg = kv_norm_g[l][None, :]

    qt, k, vt, u = _front_call(x, mod_lat, True, g1, win, qg, kvg, wuqt, wk, wvt, cs, sn,
                               is_ctx=False, ts=FRONT_TS)
    kc, vct = _front_call(ctx, mod_ctx, False, g1, win[:, Q_LORA:Q_LORA + 2 * LANES], qg, kvg, wuqt,
                          wk, wvt, cs_ctx, sn_ctx, is_ctx=True, ts=n_ctx)
    att = _attn_call(qt, k, vt, kc, vct)

    rw = _pad_cols(router_w[l], LANES)
    rw_hi = rw.astype(BF16)
    rw = jnp.concatenate([rw_hi, (rw - rw_hi.astype(F32)).astype(BF16)], axis=1)
    rb = jnp.concatenate([router_b[l], jnp.full((LANES - N_EXPERTS,), -jnp.inf, F32)])[None, :]
    x1, h2, gates, ridx, cnt = _mix_call(
        att, u, x, mod_lat, w_pool[l].astype(BF16), pool_scale[l][None, :], w_out[l].astype(BF16),
        norm2_g[l][None, :], rw, rb)

    counts = cnt[0, :N_EXPERTS].astype(jnp.int32)
    sched, starts = _moe_schedule(counts, t * TOP_K)
    ridx = ridx.reshape(t, LANES)
    is_exp = ridx[:, :TOP_K, None] == jnp.arange(N_EXPERTS, dtype=jnp.int32)
    dest = jnp.sum(jnp.where(is_exp, starts, 0), axis=-1) + ridx[:, TOP_K:2 * TOP_K]
    dest_t = dest.T

    xs = _scatter_rows(h2.reshape(t, d // 2), dest_t.reshape(TOP_K, t // SC_ROWS, SC_ROWS), t * TOP_K)
    ys = _moe_call(sched, xs, w_gate_up[l], b_gate_up[l][:, None, :], w_down[l],
                   b_down[l][:, None, :])
    out = x1.reshape(t, d)
    gates = gates.reshape(t, LANES)
    for part in range(COMBINE_PARTS):
        idx = dest_t[:, part * tp:(part + 1) * tp].reshape(-1)
        yg = _gather_rows(ys, idx)
        out = _combine_call(yg, out, gates, mod_lat, final_g[None, :], n, part)
    return out.reshape(bsz, n, d)
```

```python
import functools
import math

import jax
import jax.numpy as jnp
from jax import lax
from jax.experimental import pallas as pl
from jax.experimental.pallas import tpu as pltpu
from jax.experimental.pallas import tpu_sc as plsc

F32 = jnp.float32
BF16 = jnp.bfloat16
HIGHEST = lax.Precision.HIGHEST

N_HEADS = 8
QK_NOPE = 64
QK_ROPE = 32
V_DIM = 64
Q_LORA = 256
KV_LORA = 128
GRID_W = 64
ROPE_BASE = 10000.0
POOL_WINDOWS = (2, 4, 8, 16)
POOL_CH = 128
N_EXPERTS = 32
TOP_K = 4
SWIGLU_LIMIT = 7.0
SWIGLU_ALPHA = 1.702
N_MOD = 6
EPS = 1e-6

LANES = 128
SUBLANES = 8
BF16_SUBLANES = 16
MIB = 1024 * 1024

POOL_HALO = max(POOL_WINDOWS) // 2
VT_ROWS = -(-(V_DIM + 1) // BF16_SUBLANES) * BF16_SUBLANES

FRONT_TS = 1024
ATTN_TQ = 512
ATTN_TILES = 4
ATTN_KC = 256
MIX_TS = 512
MOE_BLOCK = 512
MOE_SUB = 64
COMBINE_TT = 512
COMBINE_PARTS = 4
SC_ROWS = 128

VMEM_SMALL = 32 * MIB
VMEM_LARGE = 56 * MIB


def _dot(a, b, **kw):
    return jnp.dot(a, b, preferred_element_type=F32, **kw)


def _rms(x):
    return x * lax.rsqrt(jnp.mean(x * x, axis=-1, keepdims=True) + EPS)


def _pack_bf16_pairs(x):
    w = x.shape[1] // 2
    return pltpu.pack_elementwise([x[:, :w], x[:, w:]], packed_dtype=BF16)


def _unpack_bf16_pairs(words):
    halves = [pltpu.unpack_elementwise(words, index=h, packed_dtype=BF16, unpacked_dtype=F32)
              for h in range(2)]
    return jnp.concatenate(halves, axis=-1)


def _mod_kernel(c_ref, w_ref, b_ref, o_ref):
    c = c_ref[...]
    a = c / (1.0 + jnp.exp(-c))
    o_ref[...] = _dot(a, w_ref[...], precision=HIGHEST) + b_ref[...]


def _mod_call(cc, w_mod, b_mod):
    d = w_mod.shape[0]
    return pl.pallas_call(
        _mod_kernel,
        grid=(N_MOD,),
        in_specs=[
            pl.BlockSpec((SUBLANES, d), lambda j: (0, 0)),
            pl.BlockSpec((d, d), lambda j: (0, j)),
            pl.BlockSpec((1, d), lambda j: (0, j)),
        ],
        out_specs=pl.BlockSpec((SUBLANES, d), lambda j: (0, j)),
        out_shape=jax.ShapeDtypeStruct((SUBLANES, N_MOD * d), F32),
        compiler_params=pltpu.CompilerParams(
            dimension_semantics=("arbitrary",), vmem_limit_bytes=VMEM_SMALL),
        name="mod",
    )(cc, w_mod, b_mod)


def _front_kernel(x_ref, mod_ref, g1_ref, win_ref, qg_ref, kvg_ref, wuqt_ref, wk_ref, wvt_ref,
                  cs_ref, sn_ref, cst_ref, snt_ref, *out_refs, is_ctx, scale):
    x = x_ref[0]
    m = mod_ref[0]
    h = _rms(x) * g1_ref[...] * (1.0 + m[1:2]) + m[0:1]
    z = _dot(h.astype(BF16), win_ref[...])
    if is_ctx:
        k_ref, vt_ref = out_refs
        zkv = z
    else:
        qt_ref, k_ref, vt_ref, u_ref = out_refs
        zkv = z[:, Q_LORA:Q_LORA + 2 * LANES]
        u_ref[0] = z[:, Q_LORA + 2 * LANES:]

    ckv = _rms(zkv[:, :KV_LORA]) * kvg_ref[...]
    kk = _dot(ckv.astype(BF16), wk_ref[...])
    kra = zkv[:, LANES:2 * LANES]
    lane = lax.broadcasted_iota(jnp.int32, kra.shape, 1)
    half = QK_ROPE // 2
    partner = jnp.where(lane < QK_NOPE + half, pltpu.roll(kra, LANES - half, axis=1),
                        pltpu.roll(kra, half, axis=1))
    kr = kra * cs_ref[...] + partner * sn_ref[...]
    for hd in range(N_HEADS):
        k_ref[0, hd] = (kk[:, hd * LANES:(hd + 1) * LANES] + kr).astype(BF16)
    vt = _dot(wvt_ref[...], ckv.T.astype(BF16))
    row = lax.broadcasted_iota(jnp.int32, (VT_ROWS, 1), 0)
    ones = jnp.where(row == V_DIM, 1.0, 0.0)
    for hd in range(N_HEADS):
        vt_ref[0, hd] = (vt[hd * VT_ROWS:(hd + 1) * VT_ROWS] + ones).astype(BF16)

    if not is_ctx:
        cq = _rms(z[:, :Q_LORA]) * qg_ref[...]
        qqt = _dot(wuqt_ref[...], cq.T.astype(BF16))
        cst = cst_ref[...]
        snt = snt_ref[...]
        r1, r2, r3 = QK_NOPE, QK_NOPE + QK_ROPE // 2, QK_NOPE + QK_ROPE
        for hd in range(N_HEADS):
            qa = qqt[hd * LANES:(hd + 1) * LANES]
            qb = jnp.concatenate([qa[:r1], qa[r2:r3], qa[r1:r2], qa[r3:]], axis=0)
            qt_ref[0, hd] = ((qa * cst + qb * snt) * scale).astype(BF16)


def _front_call(xs, mod, mod_per_batch, g1, win, qg, kvg, wuqt, wk, wvt, cs, sn, *, is_ctx, ts):
    bsz, n, d = xs.shape
    nt = n // ts
    scale = math.log2(math.e) / math.sqrt(QK_NOPE + QK_ROPE)
    const = lambda b, i: (0, 0)
    mod_map = (lambda b, i: (b, 0, 0)) if mod_per_batch else (lambda b, i: (0, 0, 0))
    in_specs = [
        pl.BlockSpec((1, ts, d), lambda b, i: (b, i, 0)),
        pl.BlockSpec((1, N_MOD, d), mod_map),
        pl.BlockSpec((1, d), const),
        pl.BlockSpec(win.shape, const),
        pl.BlockSpec((1, Q_LORA), const),
        pl.BlockSpec((1, KV_LORA), const),
        pl.BlockSpec(wuqt.shape, const),
        pl.BlockSpec(wk.shape, const),
        pl.BlockSpec(wvt.shape, const),
        pl.BlockSpec((ts, LANES), lambda b, i: (i, 0)),
        pl.BlockSpec((ts, LANES), lambda b, i: (i, 0)),
        pl.BlockSpec((LANES, ts), lambda b, i: (0, i)),
        pl.BlockSpec((LANES, ts), lambda b, i: (0, i)),
    ]
    k_spec = pl.BlockSpec((1, N_HEADS, ts, LANES), lambda b, i: (b, 0, i, 0))
    k_shape = jax.ShapeDtypeStruct((bsz, N_HEADS, n, LANES), BF16)
    qt_spec = pl.BlockSpec((1, N_HEADS, LANES, ts), lambda b, i: (b, 0, 0, i))
    qt_shape = jax.ShapeDtypeStruct((bsz, N_HEADS, LANES, n), BF16)
    vt_spec = pl.BlockSpec((1, N_HEADS, VT_ROWS, ts), lambda b, i: (b, 0, 0, i))
    vt_shape = jax.ShapeDtypeStruct((bsz, N_HEADS, VT_ROWS, n), BF16)
    if is_ctx:
        out_specs = [k_spec, vt_spec]
        out_shape = [k_shape, vt_shape]
    else:
        pool_w = win.shape[1] - Q_LORA - 2 * LANES
        out_specs = [qt_spec, k_spec, vt_spec, pl.BlockSpec((1, ts, pool_w), lambda b, i: (b, i, 0))]
        out_shape = [qt_shape, k_shape, vt_shape, jax.ShapeDtypeStruct((bsz, n, pool_w), F32)]
    return pl.pallas_call(
        functools.partial(_front_kernel, is_ctx=is_ctx, scale=scale),
        grid=(bsz, nt),
        in_specs=in_specs,
        out_specs=out_specs,
        out_shape=out_shape,
        compiler_params=pltpu.CompilerParams(
            dimension_semantics=("arbitrary", "arbitrary"), vmem_limit_bytes=VMEM_LARGE),
        name="front_ctx" if is_ctx else "front",
    )(xs, mod, g1, win, qg, kvg, wuqt, wk, wvt, cs, sn, cs.T, sn.T)


def _attn_kernel(qt_ref, k_ref, vt_ref, kc_ref, vct_ref, o_ref, s_ref, mlc_ref, mxb_ref, oe_ref):
    n_ctx = kc_ref.shape[2]
    n_lat = k_ref.shape[2]
    chunks = [(None, 0, n_ctx)] + [(c * ATTN_KC, n_ctx + c * ATTN_KC, ATTN_KC)
                                   for c in range(n_lat // ATTN_KC)]
    sub = SUBLANES

    def q_cols(tile):
        if isinstance(tile, int):
            return pl.ds(tile * ATTN_TQ, ATTN_TQ)
        return pl.ds(pl.multiple_of(tile * ATTN_TQ, ATTN_TQ), ATTN_TQ)

    def score_chunk(tile, hd, buf, ci):
        off, soff, w = chunks[ci]
        keys = kc_ref[0, hd] if off is None else k_ref[0, hd, off:off + w, :]
        s = _dot(keys, qt_ref[0, hd, :, q_cols(tile)])
        s_ref[buf, soff:soff + w, :] = s
        mx = s[0:sub]
        for r in range(1, w // sub):
            mx = jnp.maximum(mx, s[r * sub:(r + 1) * sub])
        mlc_ref[buf, ci] = mx

    def row_max(buf):
        mx = mlc_ref[buf, 0]
        for ci in range(1, len(chunks)):
            mx = jnp.maximum(mx, mlc_ref[buf, ci])
        mxb_ref[...] = jnp.broadcast_to(jnp.max(mx, axis=0, keepdims=True), mx.shape)

    def weight_chunk(hd, buf, ci):
        off, soff, w = chunks[ci]
        p = jnp.exp2(s_ref[buf, soff:soff + w, :] - mxb_ref[0:1, :]).astype(BF16)
        vt = vct_ref[0, hd] if off is None else vt_ref[0, hd, :, off:off + w]
        return _dot(vt, p)

    def stage(hw, bw, scoring, bs):
        row_max(bw)
        acc = None
        for ci in range(len(chunks)):
            if scoring is not None:
                score_chunk(*scoring, bs, ci)
            part = weight_chunk(hw, bw, ci)
            acc = part if acc is None else acc + part
        return acc[0:V_DIM] / acc[V_DIM:V_DIM + 1]

    def write_pair(tile, j, ot_odd):
        pair_t = jnp.concatenate([oe_ref[...], ot_odd], axis=0)
        o_ref[0, j, q_cols(tile), :] = pair_t.T.astype(BF16)

    pairs_per_tile = N_HEADS // 2
    n_pairs = (qt_ref.shape[3] // ATTN_TQ) * pairs_per_tile

    def split(p):
        return p // pairs_per_tile, p % pairs_per_tile

    for ci in range(len(chunks)):
        score_chunk(0, 0, 0, ci)

    def head_pair(p, carry):
        tile, j = split(p)
        nxt_tile, nxt_j = split(p + 1)
        oe_ref[...] = stage(2 * j, 0, (tile, 2 * j + 1), 1)
        write_pair(tile, j, stage(2 * j + 1, 1, (nxt_tile, 2 * nxt_j), 0))
        return carry

    lax.fori_loop(0, n_pairs - 1, head_pair, 0)
    tile, j = split(n_pairs - 1)
    oe_ref[...] = stage(2 * j, 0, (tile, 2 * j + 1), 1)
    write_pair(tile, j, stage(2 * j + 1, 1, None, None))


def _attn_call(qt, k, vt, kc, vct):
    bsz, _, _, n = qt.shape
    n_ctx = kc.shape[2]
    tq = ATTN_TQ
    n_chunks = 1 + n // ATTN_KC
    per_batch = lambda b, i: (b, 0, 0, 0)
    resident = dict(pipeline_mode=pl.Buffered(1))
    rows_per_step = ATTN_TILES * tq
    return pl.pallas_call(
        _attn_kernel,
        grid=(bsz, n // rows_per_step),
        in_specs=[
            pl.BlockSpec((1, N_HEADS, LANES, rows_per_step), lambda b, i: (b, 0, 0, i)),
            pl.BlockSpec((1, N_HEADS, n, LANES), per_batch, **resident),
            pl.BlockSpec((1, N_HEADS, VT_ROWS, n), per_batch, **resident),
            pl.BlockSpec((1, N_HEADS, n_ctx, LANES), per_batch, **resident),
            pl.BlockSpec((1, N_HEADS, VT_ROWS, n_ctx), per_batch, **resident),
        ],
        out_specs=pl.BlockSpec((1, N_HEADS // 2, rows_per_step, LANES), lambda b, i: (b, 0, i, 0)),
        out_shape=jax.ShapeDtypeStruct((bsz, N_HEADS // 2, n, LANES), BF16),
        scratch_shapes=[
            pltpu.VMEM((2, n_ctx + n, tq), F32),
            pltpu.VMEM((2, n_chunks, SUBLANES, tq), F32),
            pltpu.VMEM((SUBLANES, tq), F32),
            pltpu.VMEM((V_DIM, tq), F32),
        ],
        compiler_params=pltpu.CompilerParams(
            dimension_semantics=("arbitrary", "arbitrary"), vmem_limit_bytes=VMEM_LARGE),
        name="attn",
    )(qt, k, vt, kc, vct)


def _mix_kernel(att_ref, u_ref, up_ref, un_ref, x_ref, mod_ref, wpool_ref, pscale_ref, wout_ref,
                g2_ref, rw_ref, rb_ref,
                x1_ref, h2_ref, gate_ref, ridx_ref, cnt_ref, ue_ref, base_ref, *, n_seq):
    b = pl.program_id(0)
    i = pl.program_id(1)
    n_tiles = pl.num_programs(1)
    ts = u_ref.shape[1]

    @pl.when(jnp.logical_and(b == 0, i == 0))
    def _():
        base_ref[...] = jnp.zeros_like(base_ref)

    u = u_ref[0]
    ue_ref[0:POOL_HALO] = jnp.where(i > 0, up_ref[0], 0.0)
    ue_ref[POOL_HALO:POOL_HALO + ts] = u
    ue_ref[POOL_HALO + ts:2 * POOL_HALO + ts] = jnp.where(i < n_tiles - 1, un_ref[0], 0.0)
    t = i * ts + lax.broadcasted_iota(jnp.int32, (ts, 1), 0)
    ys = []
    for g, w in enumerate(POOL_WINDOWS):
        half = w // 2
        lanes = slice(g * POOL_CH, (g + 1) * POOL_CH)
        ws = ue_ref[POOL_HALO - half:POOL_HALO - half + ts, lanes]
        for jj in range(-half + 1, half):
            ws = ws + ue_ref[POOL_HALO + jj:POOL_HALO + jj + ts, lanes]
        count = (jnp.minimum(t + half, n_seq) - jnp.maximum(t - half, 0)).astype(F32)
        mixed = (ws / count - u[:, lanes]).astype(BF16)
        ys.append(_dot(mixed, wpool_ref[g]))
    pool = jnp.concatenate(ys, axis=-1) * pscale_ref[...]

    cat = jnp.concatenate([att_ref[0, j] for j in range(N_HEADS // 2)] + [pool.astype(BF16)], axis=-1)
    m = mod_ref[0]
    x1 = x_ref[0] + m[2:3] * _dot(cat, wout_ref[...])
    x1_ref[0] = x1
    h2 = _rms(x1) * g2_ref[...] * (1.0 + m[4:5]) + m[3:4]
    h2_ref[0] = _pack_bf16_pairs(h2)

    h_hi = h2.astype(BF16)
    h_lo = (h2 - h_hi.astype(F32)).astype(BF16)
    hi_part = _dot(h_hi, rw_ref[...])
    logits = (hi_part[:, :LANES] + hi_part[:, LANES:] + _dot(h_lo, rw_ref[:, :LANES])) + rb_ref[...]
    lane = lax.broadcasted_iota(jnp.int32, logits.shape, 1).astype(F32)
    vals, idxs = [], []
    for _k in range(TOP_K):
        mv = jnp.max(logits, axis=-1, keepdims=True)
        ix = jnp.min(jnp.where(logits == mv, lane, float(LANES)), axis=-1, keepdims=True)
        vals.append(mv)
        idxs.append(ix)
        logits = jnp.where(lane == ix, -jnp.inf, logits)
    es = [jnp.exp(v - vals[0]) for v in vals]
    den = es[0] + es[1] + es[2] + es[3]

    onehot = jnp.zeros(lane.shape, F32)
    for ix in idxs:
        onehot = onehot + jnp.where(lane == ix, 1.0, 0.0)
    row = lax.broadcasted_iota(jnp.int32, (ts, ts), 0)
    col = lax.broadcasted_iota(jnp.int32, (ts, ts), 1)
    tri = jnp.where(col < row, 1.0, 0.0).astype(BF16)
    before = _dot(tri, onehot.astype(BF16)) + base_ref[0:1, :]
    base_new = base_ref[0:1, :] + jnp.sum(onehot, axis=0, keepdims=True)
    base_ref[...] = jnp.broadcast_to(base_new, base_ref.shape)
    cnt_ref[...] = jnp.broadcast_to(base_new, cnt_ref.shape)

    gate_out = jnp.zeros(lane.shape, F32)
    ridx_out = jnp.zeros(lane.shape, F32)
    for kk in range(TOP_K):
        rank = jnp.sum(jnp.where(lane == idxs[kk], before, 0.0), axis=-1, keepdims=True)
        gate_out = jnp.where(lane == float(kk), es[kk] / den, gate_out)
        ridx_out = jnp.where(lane == float(kk), idxs[kk], ridx_out)
        ridx_out = jnp.where(lane == float(TOP_K + kk), rank, ridx_out)
    gate_ref[0] = gate_out
    ridx_ref[...] = ridx_out.T[0:2 * TOP_K].astype(jnp.int32)


def _mix_call(att, u, x, mod, wpool, pscale, wout, g2, rw, rb):
    bsz, n, d = x.shape
    ts = MIX_TS
    nt = n // ts
    pool_w = u.shape[2]
    hb = ts // POOL_HALO
    n_halo_blocks = n // POOL_HALO
    const2 = lambda b, i: (0, 0)
    tok = lambda b, i: (b, i, 0)
    return pl.pallas_call(
        functools.partial(_mix_kernel, n_seq=n),
        grid=(bsz, nt),
        in_specs=[
            pl.BlockSpec((1, N_HEADS // 2, ts, LANES), lambda b, i: (b, 0, i, 0)),
            pl.BlockSpec((1, ts, pool_w), tok),
            pl.BlockSpec((1, POOL_HALO, pool_w), lambda b, i: (b, jnp.maximum(i * hb - 1, 0), 0)),
            pl.BlockSpec((1, POOL_HALO, pool_w),
                         lambda b, i: (b, jnp.minimum((i + 1) * hb, n_halo_blocks - 1), 0)),
            pl.BlockSpec((1, ts, d), tok),
            pl.BlockSpec((1, N_MOD, d), lambda b, i: (b, 0, 0)),
            pl.BlockSpec(wpool.shape, lambda b, i: (0, 0, 0)),
            pl.BlockSpec((1, pool_w), const2),
            pl.BlockSpec(wout.shape, const2),
            pl.BlockSpec((1, d), const2),
            pl.BlockSpec(rw.shape, const2),
            pl.BlockSpec((1, LANES), const2),
        ],
        out_specs=[
            pl.BlockSpec((1, ts, d), tok),
            pl.BlockSpec((1, ts, d // 2), tok),
            pl.BlockSpec((1, ts, LANES), tok),
            pl.BlockSpec((2 * TOP_K, ts), lambda b, i: (0, b * nt + i)),
            pl.BlockSpec((SUBLANES, LANES), const2),
        ],
        out_shape=[
            jax.ShapeDtypeStruct((bsz, n, d), F32),
            jax.ShapeDtypeStruct((bsz, n, d // 2), jnp.uint32),
            jax.ShapeDtypeStruct((bsz, n, LANES), F32),
            jax.ShapeDtypeStruct((2 * TOP_K, bsz * n), jnp.int32),
            jax.ShapeDtypeStruct((SUBLANES, LANES), F32),
        ],
        scratch_shapes=[
            pltpu.VMEM((ts + 2 * POOL_HALO, pool_w), F32),
            pltpu.VMEM((SUBLANES, LANES), F32),
        ],
        compiler_params=pltpu.CompilerParams(
            dimension_semantics=("arbitrary", "arbitrary"), vmem_limit_bytes=VMEM_SMALL),
        name="mix",
    )(att, u, u, u, x, mod, wpool, pscale, wout, g2, rw, rb)


def _sc_mesh():
    return plsc.VectorSubcoreMesh(core_axis_name="core", subcore_axis_name="subcore")


def _sc_worker_id():
    info = plsc.get_sparse_core_info()
    return lax.axis_index("subcore") * info.num_cores + lax.axis_index("core")


def _sc_num_workers():
    info = plsc.get_sparse_core_info()
    return info.num_cores * info.num_subcores


def _scatter_rows(x, dest3, n_out):
    t, d = x.shape
    top_k, n_chunks, rows = dest3.shape
    assert n_chunks % _sc_num_workers() == 0 and n_chunks * rows == t
    per_worker = n_chunks // _sc_num_workers()

    @functools.partial(
        pl.kernel,
        out_type=jax.ShapeDtypeStruct((n_out, d), x.dtype),
        mesh=_sc_mesh(),
        scratch_types=[pltpu.VMEM((top_k, rows), jnp.int32), pltpu.VMEM((rows, d), x.dtype)],
        name="sc_scatter",
    )
    def scatter(x_hbm, i_hbm, o_hbm, idx_v, rows_v):
        wid = _sc_worker_id()

        @pl.loop(0, per_worker)
        def _(c):
            chunk = wid * per_worker + c
            for kk in range(top_k):
                pltpu.sync_copy(i_hbm.at[kk, chunk], idx_v.at[kk])
            pltpu.sync_copy(x_hbm.at[pl.ds(pl.multiple_of(chunk * rows, SUBLANES), rows)], rows_v)
            for kk in range(top_k):
                pltpu.sync_copy(rows_v, o_hbm.at[idx_v.at[kk]])

    return scatter(x, dest3)


def _gather_rows(y, idx):
    n = idx.shape[0]
    d = y.shape[1]
    rows = SC_ROWS
    assert n % (rows * _sc_num_workers()) == 0
    per_worker = n // _sc_num_workers()
    n_chunks = per_worker // rows

    @functools.partial(
        pl.kernel,
        out_type=jax.ShapeDtypeStruct((n, d), y.dtype),
        mesh=_sc_mesh(),
        scratch_types=[pltpu.VMEM((rows,), jnp.int32), pltpu.VMEM((rows, d), y.dtype)],
        name="sc_gather",
    )
    def gather(y_hbm, i_hbm, o_hbm, idx_v, rows_v):
        base = _sc_worker_id() * per_worker

        @pl.loop(0, n_chunks)
        def _(c):
            off = pl.multiple_of(base + c * rows, SUBLANES)
            pltpu.sync_copy(i_hbm.at[pl.ds(off, rows)], idx_v)
            pltpu.sync_copy(y_hbm.at[idx_v], rows_v)
            pltpu.sync_copy(rows_v, o_hbm.at[pl.ds(off, rows)])

    return gather(y, idx)


def _moe_kernel(blk_ref, exp_ref, lo_ref, hi_ref, new_ref, init_ref, next_ref, slot_ref, xs_ref,
                wgu_hbm, bgu_ref, wd_hbm, bd_ref, ys_ref, wgu_buf, wd_buf, sems):
    i = pl.program_id(0)
    slot = slot_ref[i]

    def weight_copies(e, s):
        return (pltpu.make_async_copy(wgu_hbm.at[e], wgu_buf.at[s], sems.at[s, 0]),
                pltpu.make_async_copy(wd_hbm.at[e], wd_buf.at[s], sems.at[s, 1]))

    @pl.when(i == 0)
    def _():
        for cp in weight_copies(exp_ref[0], slot):
            cp.start()

    @pl.when(new_ref[i] == 1)
    def _():
        for cp in weight_copies(exp_ref[i], slot):
            cp.wait()

        @pl.when(next_ref[i] >= 0)
        def _():
            for cp in weight_copies(next_ref[i], 1 - slot):
                cp.start()

    @pl.when(init_ref[i] == 1)
    def _():
        ys_ref[...] = jnp.zeros_like(ys_ref)

    lo = lo_ref[i]
    hi = hi_ref[i]

    def expert_rows(r0, n_rows):
        de = wd_buf.shape[1]
        x = _unpack_bf16_pairs(xs_ref[pl.ds(r0, n_rows), :])
        gu = _dot(x, wgu_buf[slot]) + bgu_ref[0]
        g = jnp.minimum(gu[:, :de], SWIGLU_LIMIT)
        lin = jnp.clip(gu[:, de:], -SWIGLU_LIMIT, SWIGLU_LIMIT)
        act = g / (1.0 + jnp.exp(-SWIGLU_ALPHA * g)) * (lin + 1.0)
        y = _dot(act, wd_buf[slot]) + bd_ref[0]
        row = r0 + lax.broadcasted_iota(jnp.int32, (n_rows, 1), 0)
        mine = jnp.logical_and(row >= lo, row < hi)
        ys_ref[pl.ds(r0, n_rows), :] = jnp.where(mine, _pack_bf16_pairs(y), ys_ref[pl.ds(r0, n_rows), :])

    first = lo // MOE_SUB
    pieces = jnp.where(hi > lo, (hi + MOE_SUB - 1) // MOE_SUB - first, 0)
    for cnt in range(1, MOE_BLOCK // MOE_SUB + 1):
        @pl.when(pieces == cnt)
        def _(cnt=cnt):
            r0 = 0 if cnt * MOE_SUB == MOE_BLOCK else pl.multiple_of(first * MOE_SUB, MOE_SUB)
            expert_rows(r0, cnt * MOE_SUB)


def _moe_call(sched, xs, wgu, bgu, wd, bd):
    n_items = sched[0].shape[0]
    _, d, de2 = wgu.shape
    de = de2 // 2
    wmap = lambda i, blk, exp, lo, hi, new, init, nxt, slot: (exp[i], 0, 0)
    xmap = lambda i, blk, exp, lo, hi, new, init, nxt, slot: (blk[i], 0)
    return pl.pallas_call(
        _moe_kernel,
        grid_spec=pltpu.PrefetchScalarGridSpec(
            num_scalar_prefetch=8,
            grid=(n_items,),
            in_specs=[
                pl.BlockSpec((MOE_BLOCK, d // 2), xmap),
                pl.BlockSpec(memory_space=pl.ANY),
                pl.BlockSpec((1, 1, de2), wmap),
                pl.BlockSpec(memory_space=pl.ANY),
                pl.BlockSpec((1, 1, d), wmap),
            ],
            out_specs=pl.BlockSpec((MOE_BLOCK, d // 2), xmap),
            scratch_shapes=[
                pltpu.VMEM((2, d, de2), F32), pltpu.VMEM((2, de, d), F32),
                pltpu.SemaphoreType.DMA((2, 2)),
            ],
        ),
        out_shape=jax.ShapeDtypeStruct(xs.shape, xs.dtype),
        compiler_params=pltpu.CompilerParams(
            dimension_semantics=("arbitrary",), vmem_limit_bytes=VMEM_LARGE),
        name="moe",
    )(*sched, xs, wgu, bgu, wd, bd)


def _moe_schedule(counts, n_rows):
    n_blocks = n_rows // MOE_BLOCK
    n_items = n_blocks + N_EXPERTS
    ends = jnp.cumsum(counts)
    starts = ends - counts
    first_blk = starts // MOE_BLOCK
    last_blk = (ends - 1) // MOE_BLOCK
    items_per = jnp.where(counts > 0, last_blk - first_blk + 1, 0)
    item_ends = jnp.cumsum(items_per)
    item_starts = item_ends - items_per
    total = item_ends[-1]
    it = jnp.arange(n_items, dtype=jnp.int32)
    live = it < total
    itc = jnp.minimum(it, total - 1)
    exp = jnp.sum((item_ends[None, :] <= itc[:, None]).astype(jnp.int32), axis=1)
    is_exp = exp[:, None] == jnp.arange(N_EXPERTS, dtype=jnp.int32)[None, :]
    pick = lambda table: jnp.sum(jnp.where(is_exp, table[None, :], 0), axis=1)
    blk = pick(first_blk) + itc - pick(item_starts)
    lo = jnp.clip(pick(starts) - blk * MOE_BLOCK, 0, MOE_BLOCK)
    hi = jnp.clip(pick(ends) - blk * MOE_BLOCK, 0, MOE_BLOCK)
    hi = jnp.where(live, hi, lo)
    prev_exp = jnp.concatenate([jnp.full((1,), -1, jnp.int32), exp[:-1]])
    prev_blk = jnp.concatenate([jnp.full((1,), -1, jnp.int32), blk[:-1]])
    new = jnp.logical_and(live, exp != prev_exp)
    init = jnp.logical_and(live, blk != prev_blk)
    slot = (jnp.cumsum(new.astype(jnp.int32)) - 1) % 2
    ar = jnp.arange(N_EXPERTS, dtype=jnp.int32)
    later = jnp.logical_and(counts[None, :] > 0, ar[None, :] > ar[:, None])
    next_exp = jnp.min(jnp.where(later, ar[None, :], N_EXPERTS), axis=1)
    next_exp = jnp.where(next_exp == N_EXPERTS, -1, next_exp)
    nxt = pick(next_exp)
    as_i32 = lambda a: a.astype(jnp.int32)
    return tuple(as_i32(a) for a in (blk, exp, lo, hi, new, init, nxt, slot)), starts


def _combine_kernel(*refs):
    yg_refs = refs[:TOP_K]
    x1_ref, gate_ref, mod_ref, fg_ref, o_ref = refs[TOP_K:]
    gates = gate_ref[...]
    y = gates[:, 0:1] * _unpack_bf16_pairs(yg_refs[0][...])
    for kk in range(1, TOP_K):
        y = y + gates[:, kk:kk + 1] * _unpack_bf16_pairs(yg_refs[kk][...])
    m = mod_ref[0]
    x2 = x1_ref[...] + m[5:6] * y
    o_ref[...] = _rms(x2) * fg_ref[...]


def _combine_call(yg, x1, gates, mod, fg, n_seq, part):
    t, d = x1.shape
    tt = COMBINE_TT
    tiles_per_seq = n_seq // tt
    n_tiles = yg.shape[0] // (TOP_K * tt)
    first = part * n_tiles
    tok = lambda i: (first + i, 0)
    slot_specs = [pl.BlockSpec((tt, d // 2), functools.partial(lambda kk, i: (kk * n_tiles + i, 0), kk))
                  for kk in range(TOP_K)]
    return pl.pallas_call(
        _combine_kernel,
        grid=(n_tiles,),
        in_specs=slot_specs + [
            pl.BlockSpec((tt, d), tok),
            pl.BlockSpec((tt, LANES), tok),
            pl.BlockSpec((1, N_MOD, d), lambda i: ((first + i) // tiles_per_seq, 0, 0)),
            pl.BlockSpec((1, d), lambda i: (0, 0)),
        ],
        out_specs=pl.BlockSpec((tt, d), tok),
        out_shape=jax.ShapeDtypeStruct((t, d), F32),
        input_output_aliases={TOP_K: 0},
        compiler_params=pltpu.CompilerParams(
            dimension_semantics=("arbitrary",), vmem_limit_bytes=VMEM_SMALL),
        name="combine",
    )(*([yg] * TOP_K), x1, gates, mod, fg)


def _pad_cols(a, width):
    return jnp.pad(a, ((0, 0), (0, width - a.shape[1])))


def _rope_slab(w_rope):
    return jnp.pad(w_rope, ((0, 0), (QK_NOPE, LANES - QK_NOPE - QK_ROPE)))


def _prep_w_in(w_in):
    kr0 = Q_LORA + KV_LORA
    w_kr = w_in[:, kr0:kr0 + QK_ROPE]
    return jnp.concatenate(
        [w_in[:, :kr0], _rope_slab(w_kr), w_in[:, kr0 + QK_ROPE:]], axis=1).astype(BF16)


def _prep_w_uq(w_uq):
    per = QK_NOPE + QK_ROPE
    rows = w_uq.shape[0]
    w = w_uq.reshape(rows, N_HEADS, per)
    slab = jnp.pad(w, ((0, 0), (0, 0), (0, LANES - per)))
    return slab.reshape(rows, -1).T.astype(BF16)


def _prep_w_ukv(w_ukv):
    per = QK_NOPE + V_DIM
    rows = w_ukv.shape[0]
    w = w_ukv.reshape(rows, N_HEADS, per)
    wk = jnp.pad(w[:, :, :QK_NOPE], ((0, 0), (0, 0), (0, LANES - QK_NOPE))).reshape(rows, -1)
    wvt = jnp.pad(jnp.transpose(w[:, :, QK_NOPE:], (1, 2, 0)), ((0, 0), (0, VT_ROWS - V_DIM), (0, 0)))
    return wk.astype(BF16), wvt.reshape(N_HEADS * VT_ROWS, rows).astype(BF16)


def _rope_tables(n_lat):
    rows = n_lat // GRID_W
    nf = QK_ROPE // 4
    row = jnp.repeat(jnp.arange(rows, dtype=F32), GRID_W)
    col = jnp.tile(jnp.arange(GRID_W, dtype=F32), rows)
    freqs = ROPE_BASE ** (-jnp.arange(nf, dtype=F32) / nf)
    ang = jnp.concatenate([row[:, None] * freqs, col[:, None] * freqs], axis=-1)
    cos, sin = jnp.cos(ang), jnp.sin(ang)
    ones = jnp.ones((n_lat, QK_NOPE), F32)
    zeros = jnp.zeros((n_lat, QK_NOPE), F32)
    cs = _pad_cols(jnp.concatenate([ones, cos, cos], axis=1), LANES)
    sn = _pad_cols(jnp.concatenate([zeros, -sin, sin], axis=1), LANES)
    return cs, sn


def kernel(x, c, ctx, c_ctx, w_mod, b_mod, norm1_g, w_in, q_norm_g, kv_norm_g, w_uq, w_ukv, w_pool,
           pool_scale, w_out, norm2_g, router_w, router_b, w_gate_up, b_gate_up, w_down, b_down,
           final_g):
    bsz, n, d = x.shape
    n_ctx = ctx.shape[1]
    t = bsz * n
    assert w_mod.shape[0] == 1, "single-layer block"
    l = 0
    tp = t // COMBINE_PARTS
    assert bsz + 1 <= SUBLANES and d % (2 * LANES) == 0 and n_ctx % ATTN_KC == 0
    assert n % GRID_W == 0 and n % FRONT_TS == 0 and n % MIX_TS == 0 and n % ATTN_KC == 0
    assert n % (ATTN_TQ * ATTN_TILES) == 0 and (t * TOP_K) % MOE_BLOCK == 0 and MOE_BLOCK % MOE_SUB == 0
    assert t % COMBINE_PARTS == 0 and tp % COMBINE_TT == 0 and n % COMBINE_TT == 0 and t % SC_ROWS == 0

    cc = jnp.concatenate([c, c_ctx[None, :], jnp.zeros((SUBLANES - bsz - 1, d), F32)], axis=0)
    mod = _mod_call(cc, w_mod[l], b_mod[l][None, :]).reshape(SUBLANES, N_MOD, d)
    mod_lat, mod_ctx = mod[:bsz], mod[bsz:bsz + 1]

    win = _prep_w_in(w_in[l])
    wuqt = _prep_w_uq(w_uq[l])
    wk, wvt = _prep_w_ukv(w_ukv[l])
    cs, sn = _rope_tables(n)
    cs_ctx = jnp.broadcast_to((jnp.arange(LANES) < QK_NOPE + QK_ROPE).astype(F32), (n_ctx, LANES))
    sn_ctx = jnp.zeros((n_ctx, LANES), F32)
    g1 = norm1_g[l][None, :]
    qg = q_norm_g[l][None, :]
    kvg = kv_norm_g[l][None, :]

    qt, k, vt, u = _front_call(x, mod_lat, True, g1, win, qg, kvg, wuqt, wk, wvt, cs, sn,
                               is_ctx=False, ts=FRONT_TS)
    kc, vct = _front_call(ctx, mod_ctx, False, g1, win[:, Q_LORA:Q_LORA + 2 * LANES], qg, kvg, wuqt,
                          wk, wvt, cs_ctx, sn_ctx, is_ctx=True, ts=n_ctx)
    att = _attn_call(qt, k, vt, kc, vct)

    rw = _pad_cols(router_w[l], LANES)
    rw_hi = rw.astype(BF16)
    rw = jnp.concatenate([rw_hi, (rw - rw_hi.astype(F32)).astype(BF16)], axis=1)
    rb = jnp.concatenate([router_b[l], jnp.full((LANES - N_EXPERTS,), -jnp.inf, F32)])[None, :]
    x1, h2, gates, ridx, cnt = _mix_call(
        att, u, x, mod_lat, w_pool[l].astype(BF16), pool_scale[l][None, :], w_out[l].astype(BF16),
        norm2_g[l][None, :], rw, rb)

    counts = cnt[0, :N_EXPERTS].astype(jnp.int32)
    sched, starts = _moe_schedule(counts, t * TOP_K)
    is_exp = ridx[:TOP_K, None, :] == jnp.arange(N_EXPERTS, dtype=jnp.int32)[None, :, None]
    dest_t = jnp.sum(jnp.where(is_exp, starts[None, :, None], 0), axis=1) + ridx[TOP_K:]

    xs = _scatter_rows(h2.reshape(t, d // 2), dest_t.reshape(TOP_K, t // SC_ROWS, SC_ROWS), t * TOP_K)
    ys = _moe_call(sched, xs, w_gate_up[l], b_gate_up[l][:, None, :], w_down[l],
                   b_down[l][:, None, :])
    out = x1.reshape(t, d)
    gates = gates.reshape(t, LANES)
    for part in range(COMBINE_PARTS):
        idx = dest_t[:, part * tp:(part + 1) * tp].reshape(-1)
        yg = _gather_rows(ys, idx)
        out = _combine_call(yg, out, gates, mod_lat, final_g[None, :], n, part)
    return out.reshape(bsz, n, d)
```

```python
import functools
import math

import jax
import jax.numpy as jnp
from jax import lax
from jax.experimental import pallas as pl
from jax.experimental.pallas import tpu as pltpu
from jax.experimental.pallas import tpu_sc as plsc

F32 = jnp.float32
BF16 = jnp.bfloat16
HIGHEST = lax.Precision.HIGHEST

N_HEADS = 8
QK_NOPE = 64
QK_ROPE = 32
V_DIM = 64
Q_LORA = 256
KV_LORA = 128
GRID_W = 64
ROPE_BASE = 10000.0
POOL_WINDOWS = (2, 4, 8, 16)
POOL_CH = 128
N_EXPERTS = 32
TOP_K = 4
SWIGLU_LIMIT = 7.0
SWIGLU_ALPHA = 1.702
N_MOD = 6
EPS = 1e-6

LANES = 128
SUBLANES = 8
BF16_SUBLANES = 16
MIB = 1024 * 1024

POOL_HALO = max(POOL_WINDOWS) // 2
VT_ROWS = -(-(V_DIM + 1) // BF16_SUBLANES) * BF16_SUBLANES

FRONT_TS = 1024
ATTN_TQ = 512
ATTN_TILES = 4
ATTN_KC = 256
MIX_TS = 512
MOE_BLOCK = 512
MOE_SUB = 64
COMBINE_TT = 512
COMBINE_PARTS = 4
SC_ROWS = 128

VMEM_SMALL = 32 * MIB
VMEM_LARGE = 56 * MIB


def _dot(a, b, **kw):
    return jnp.dot(a, b, preferred_element_type=F32, **kw)


def _rms(x):
    return x * lax.rsqrt(jnp.mean(x * x, axis=-1, keepdims=True) + EPS)


def _pack_bf16_pairs(x):
    w = x.shape[1] // 2
    return pltpu.pack_elementwise([x[:, :w], x[:, w:]], packed_dtype=BF16)


def _unpack_bf16_pairs(words):
    halves = [pltpu.unpack_elementwise(words, index=h, packed_dtype=BF16, unpacked_dtype=F32)
              for h in range(2)]
    return jnp.concatenate(halves, axis=-1)


def _mod_kernel(c_ref, w_ref, b_ref, o_ref):
    c = c_ref[...]
    a = c / (1.0 + jnp.exp(-c))
    o_ref[...] = _dot(a, w_ref[...], precision=HIGHEST) + b_ref[...]


def _mod_call(cc, w_mod, b_mod):
    d = w_mod.shape[0]
    return pl.pallas_call(
        _mod_kernel,
        grid=(N_MOD,),
        in_specs=[
            pl.BlockSpec((SUBLANES, d), lambda j: (0, 0)),
            pl.BlockSpec((d, d), lambda j: (0, j)),
            pl.BlockSpec((1, d), lambda j: (0, j)),
        ],
        out_specs=pl.BlockSpec((SUBLANES, d), lambda j: (0, j)),
        out_shape=jax.ShapeDtypeStruct((SUBLANES, N_MOD * d), F32),
        compiler_params=pltpu.CompilerParams(
            dimension_semantics=("arbitrary",), vmem_limit_bytes=VMEM_SMALL),
        name="mod",
    )(cc, w_mod, b_mod)


def _front_kernel(x_ref, mod_ref, g1_ref, win_ref, qg_ref, kvg_ref, wuqt_ref, wk_ref, wvt_ref,
                  cs_ref, sn_ref, cst_ref, snt_ref, *out_refs, is_ctx, scale):
    x = x_ref[0]
    m = mod_ref[0]
    h = _rms(x) * g1_ref[...] * (1.0 + m[1:2]) + m[0:1]
    z = _dot(h.astype(BF16), win_ref[...])
    if is_ctx:
        k_ref, vt_ref = out_refs
        zkv = z
    else:
        qt_ref, k_ref, vt_ref, u_ref = out_refs
        zkv = z[:, Q_LORA:Q_LORA + 2 * LANES]
        u_ref[0] = z[:, Q_LORA + 2 * LANES:]

    ckv = _rms(zkv[:, :KV_LORA]) * kvg_ref[...]
    kk = _dot(ckv.astype(BF16), wk_ref[...])
    kra = zkv[:, LANES:2 * LANES]
    lane = lax.broadcasted_iota(jnp.int32, kra.shape, 1)
    half = QK_ROPE // 2
    partner = jnp.where(lane < QK_NOPE + half, pltpu.roll(kra, LANES - half, axis=1),
                        pltpu.roll(kra, half, axis=1))
    kr = kra * cs_ref[...] + partner * sn_ref[...]
    for hd in range(N_HEADS):
        k_ref[0, hd] = (kk[:, hd * LANES:(hd + 1) * LANES] + kr).astype(BF16)
    vt = _dot(wvt_ref[...], ckv.T.astype(BF16))
    row = lax.broadcasted_iota(jnp.int32, (VT_ROWS, 1), 0)
    ones = jnp.where(row == V_DIM, 1.0, 0.0)
    for hd in range(N_HEADS):
        vt_ref[0, hd] = (vt[hd * VT_ROWS:(hd + 1) * VT_ROWS] + ones).astype(BF16)

    if not is_ctx:
        cq = _rms(z[:, :Q_LORA]) * qg_ref[...]
        qqt = _dot(wuqt_ref[...], cq.T.astype(BF16))
        cst = cst_ref[...]
        snt = snt_ref[...]
        r1, r2, r3 = QK_NOPE, QK_NOPE + QK_ROPE // 2, QK_NOPE + QK_ROPE
        for hd in range(N_HEADS):
            qa = qqt[hd * LANES:(hd + 1) * LANES]
            qb = jnp.concatenate([qa[:r1], qa[r2:r3], qa[r1:r2], qa[r3:]], axis=0)
            qt_ref[0, hd] = ((qa * cst + qb * snt) * scale).astype(BF16)


def _front_call(xs, mod, mod_per_batch, g1, win, qg, kvg, wuqt, wk, wvt, cs, sn, *, is_ctx, ts):
    bsz, n, d = xs.shape
    nt = n // ts
    scale = math.log2(math.e) / math.sqrt(QK_NOPE + QK_ROPE)
    const = lambda b, i: (0, 0)
    mod_map = (lambda b, i: (b, 0, 0)) if mod_per_batch else (lambda b, i: (0, 0, 0))
    in_specs = [
        pl.BlockSpec((1, ts, d), lambda b, i: (b, i, 0)),
        pl.BlockSpec((1, N_MOD, d), mod_map),
        pl.BlockSpec((1, d), const),
        pl.BlockSpec(win.shape, const),
        pl.BlockSpec((1, Q_LORA), const),
        pl.BlockSpec((1, KV_LORA), const),
        pl.BlockSpec(wuqt.shape, const),
        pl.BlockSpec(wk.shape, const),
        pl.BlockSpec(wvt.shape, const),
        pl.BlockSpec((ts, LANES), lambda b, i: (i, 0)),
        pl.BlockSpec((ts, LANES), lambda b, i: (i, 0)),
        pl.BlockSpec((LANES, ts), lambda b, i: (0, i)),
        pl.BlockSpec((LANES, ts), lambda b, i: (0, i)),
    ]
    k_spec = pl.BlockSpec((1, N_HEADS, ts, LANES), lambda b, i: (b, 0, i, 0))
    k_shape = jax.ShapeDtypeStruct((bsz, N_HEADS, n, LANES), BF16)
    qt_spec = pl.BlockSpec((1, N_HEADS, LANES, ts), lambda b, i: (b, 0, 0, i))
    qt_shape = jax.ShapeDtypeStruct((bsz, N_HEADS, LANES, n), BF16)
    vt_spec = pl.BlockSpec((1, N_HEADS, VT_ROWS, ts), lambda b, i: (b, 0, 0, i))
    vt_shape = jax.ShapeDtypeStruct((bsz, N_HEADS, VT_ROWS, n), BF16)
    if is_ctx:
        out_specs = [k_spec, vt_spec]
        out_shape = [k_shape, vt_shape]
    else:
        pool_w = win.shape[1] - Q_LORA - 2 * LANES
        out_specs = [qt_spec, k_spec, vt_spec, pl.BlockSpec((1, ts, pool_w), lambda b, i: (b, i, 0))]
        out_shape = [qt_shape, k_shape, vt_shape, jax.ShapeDtypeStruct((bsz, n, pool_w), F32)]
    return pl.pallas_call(
        functools.partial(_front_kernel, is_ctx=is_ctx, scale=scale),
        grid=(bsz, nt),
        in_specs=in_specs,
        out_specs=out_specs,
        out_shape=out_shape,
        compiler_params=pltpu.CompilerParams(
            dimension_semantics=("arbitrary", "arbitrary"), vmem_limit_bytes=VMEM_LARGE),
        name="front_ctx" if is_ctx else "front",
    )(xs, mod, g1, win, qg, kvg, wuqt, wk, wvt, cs, sn, cs.T, sn.T)


def _attn_kernel(qt_ref, k_ref, vt_ref, kc_ref, vct_ref, o_ref, s_ref, mlc_ref, mxb_ref, oe_ref):
    n_ctx = kc_ref.shape[2]
    n_lat = k_ref.shape[2]
    chunks = [(None, 0, n_ctx)] + [(c * ATTN_KC, n_ctx + c * ATTN_KC, ATTN_KC)
                                   for c in range(n_lat // ATTN_KC)]
    sub = SUBLANES

    def q_cols(tile):
        if isinstance(tile, int):
            return pl.ds(tile * ATTN_TQ, ATTN_TQ)
        return pl.ds(pl.multiple_of(tile * ATTN_TQ, ATTN_TQ), ATTN_TQ)

    def score_chunk(tile, hd, buf, ci):
        off, soff, w = chunks[ci]
        keys = kc_ref[0, hd] if off is None else k_ref[0, hd, off:off + w, :]
        s = _dot(keys, qt_ref[0, hd, :, q_cols(tile)])
        s_ref[buf, soff:soff + w, :] = s
        mx = s[0:sub]
        for r in range(1, w // sub):
            mx = jnp.maximum(mx, s[r * sub:(r + 1) * sub])
        mlc_ref[buf, ci] = mx

    def row_max(buf):
        mx = mlc_ref[buf, 0]
        for ci in range(1, len(chunks)):
            mx = jnp.maximum(mx, mlc_ref[buf, ci])
        mxb_ref[...] = jnp.broadcast_to(jnp.max(mx, axis=0, keepdims=True), mx.shape)

    def weight_chunk(hd, buf, ci):
        off, soff, w = chunks[ci]
        p = jnp.exp2(s_ref[buf, soff:soff + w, :] - mxb_ref[0:1, :]).astype(BF16)
        vt = vct_ref[0, hd] if off is None else vt_ref[0, hd, :, off:off + w]
        return _dot(vt, p)

    def stage(hw, bw, scoring, bs):
        row_max(bw)
        acc = None
        for ci in range(len(chunks)):
            if scoring is not None:
                score_chunk(*scoring, bs, ci)
            part = weight_chunk(hw, bw, ci)
            acc = part if acc is None else acc + part
        return acc[0:V_DIM] / acc[V_DIM:V_DIM + 1]

    def write_pair(tile, j, ot_odd):
        pair_t = jnp.concatenate([oe_ref[...], ot_odd], axis=0)
        o_ref[0, j, q_cols(tile), :] = pair_t.T.astype(BF16)

    pairs_per_tile = N_HEADS // 2
    n_pairs = (qt_ref.shape[3] // ATTN_TQ) * pairs_per_tile

    def split(p):
        return p // pairs_per_tile, p % pairs_per_tile

    for ci in range(len(chunks)):
        score_chunk(0, 0, 0, ci)

    def head_pair(p, carry):
        tile, j = split(p)
        nxt_tile, nxt_j = split(p + 1)
        oe_ref[...] = stage(2 * j, 0, (tile, 2 * j + 1), 1)
        write_pair(tile, j, stage(2 * j + 1, 1, (nxt_tile, 2 * nxt_j), 0))
        return carry

    lax.fori_loop(0, n_pairs - 1, head_pair, 0)
    tile, j = split(n_pairs - 1)
    oe_ref[...] = stage(2 * j, 0, (tile, 2 * j + 1), 1)
    write_pair(tile, j, stage(2 * j + 1, 1, None, None))


def _attn_call(qt, k, vt, kc, vct):
    bsz, _, _, n = qt.shape
    n_ctx = kc.shape[2]
    tq = ATTN_TQ
    n_chunks = 1 + n // ATTN_KC
    per_batch = lambda b, i: (b, 0, 0, 0)
    resident = dict(pipeline_mode=pl.Buffered(1))
    rows_per_step = ATTN_TILES * tq
    return pl.pallas_call(
        _attn_kernel,
        grid=(bsz, n // rows_per_step),
        in_specs=[
            pl.BlockSpec((1, N_HEADS, LANES, rows_per_step), lambda b, i: (b, 0, 0, i)),
            pl.BlockSpec((1, N_HEADS, n, LANES), per_batch, **resident),
            pl.BlockSpec((1, N_HEADS, VT_ROWS, n), per_batch, **resident),
            pl.BlockSpec((1, N_HEADS, n_ctx, LANES), per_batch, **resident),
            pl.BlockSpec((1, N_HEADS, VT_ROWS, n_ctx), per_batch, **resident),
        ],
        out_specs=pl.BlockSpec((1, N_HEADS // 2, rows_per_step, LANES), lambda b, i: (b, 0, i, 0)),
        out_shape=jax.ShapeDtypeStruct((bsz, N_HEADS // 2, n, LANES), BF16),
        scratch_shapes=[
            pltpu.VMEM((2, n_ctx + n, tq), F32),
            pltpu.VMEM((2, n_chunks, SUBLANES, tq), F32),
            pltpu.VMEM((SUBLANES, tq), F32),
            pltpu.VMEM((V_DIM, tq), F32),
        ],
        compiler_params=pltpu.CompilerParams(
            dimension_semantics=("arbitrary", "arbitrary"), vmem_limit_bytes=VMEM_LARGE),
        name="attn",
    )(qt, k, vt, kc, vct)


def _mix_kernel(att_ref, u_ref, up_ref, un_ref, x_ref, mod_ref, wpool_ref, pscale_ref, wout_ref,
                g2_ref, rw_ref, rb_ref,
                x1_ref, h2_ref, gate_ref, ridx_ref, cnt_ref, ue_ref, base_ref, *, n_seq):
    b = pl.program_id(0)
    i = pl.program_id(1)
    n_tiles = pl.num_programs(1)
    ts = u_ref.shape[1]

    @pl.when(jnp.logical_and(b == 0, i == 0))
    def _():
        base_ref[...] = jnp.zeros_like(base_ref)

    u = u_ref[0]
    ue_ref[0:POOL_HALO] = jnp.where(i > 0, up_ref[0], 0.0)
    ue_ref[POOL_HALO:POOL_HALO + ts] = u
    ue_ref[POOL_HALO + ts:2 * POOL_HALO + ts] = jnp.where(i < n_tiles - 1, un_ref[0], 0.0)
    t = i * ts + lax.broadcasted_iota(jnp.int32, (ts, 1), 0)
    ys = []
    for g, w in enumerate(POOL_WINDOWS):
        half = w // 2
        lanes = slice(g * POOL_CH, (g + 1) * POOL_CH)
        ws = ue_ref[POOL_HALO - half:POOL_HALO - half + ts, lanes]
        for jj in range(-half + 1, half):
            ws = ws + ue_ref[POOL_HALO + jj:POOL_HALO + jj + ts, lanes]
        count = (jnp.minimum(t + half, n_seq) - jnp.maximum(t - half, 0)).astype(F32)
        mixed = (ws / count - u[:, lanes]).astype(BF16)
        ys.append(_dot(mixed, wpool_ref[g]))
    pool = jnp.concatenate(ys, axis=-1) * pscale_ref[...]

    cat = jnp.concatenate([att_ref[0, j] for j in range(N_HEADS // 2)] + [pool.astype(BF16)], axis=-1)
    m = mod_ref[0]
    x1 = x_ref[0] + m[2:3] * _dot(cat, wout_ref[...])
    x1_ref[0] = x1
    h2 = _rms(x1) * g2_ref[...] * (1.0 + m[4:5]) + m[3:4]
    h2_ref[0] = _pack_bf16_pairs(h2)

    h_hi = h2.astype(BF16)
    h_lo = (h2 - h_hi.astype(F32)).astype(BF16)
    hi_part = _dot(h_hi, rw_ref[...])
    logits = (hi_part[:, :LANES] + hi_part[:, LANES:] + _dot(h_lo, rw_ref[:, :LANES])) + rb_ref[...]
    lane = lax.broadcasted_iota(jnp.int32, logits.shape, 1).astype(F32)
    vals, idxs = [], []
    for _k in range(TOP_K):
        mv = jnp.max(logits, axis=-1, keepdims=True)
        ix = jnp.min(jnp.where(logits == mv, lane, float(LANES)), axis=-1, keepdims=True)
        vals.append(mv)
        idxs.append(ix)
        logits = jnp.where(lane == ix, -jnp.inf, logits)
    es = [jnp.exp(v - vals[0]) for v in vals]
    den = es[0] + es[1] + es[2] + es[3]

    onehot = jnp.zeros(lane.shape, F32)
    for ix in idxs:
        onehot = onehot + jnp.where(lane == ix, 1.0, 0.0)
    row = lax.broadcasted_iota(jnp.int32, (ts, ts), 0)
    col = lax.broadcasted_iota(jnp.int32, (ts, ts), 1)
    tri = jnp.where(col < row, 1.0, 0.0).astype(BF16)
    before = _dot(tri, onehot.astype(BF16)) + base_ref[0:1, :]
    base_new = base_ref[0:1, :] + jnp.sum(onehot, axis=0, keepdims=True)
    base_ref[...] = jnp.broadcast_to(base_new, base_ref.shape)
    cnt_ref[...] = jnp.broadcast_to(base_new, cnt_ref.shape)

    route = jnp.zeros(lane.shape, F32)
    for kk in range(TOP_K):
        rank = jnp.sum(jnp.where(lane == idxs[kk], before, 0.0), axis=-1, keepdims=True)
        route = jnp.where(lane == float(kk), idxs[kk], route)
        route = jnp.where(lane == float(TOP_K + kk), rank, route)
        route = jnp.where(lane == float(2 * TOP_K + kk), es[kk] / den, route)
    route_t = route.T
    ridx_ref[...] = route_t[0:2 * TOP_K].astype(jnp.int32)
    gate_ref[...] = route_t[2 * TOP_K:2 * TOP_K + SUBLANES]


def _mix_call(att, u, x, mod, wpool, pscale, wout, g2, rw, rb):
    bsz, n, d = x.shape
    ts = MIX_TS
    nt = n // ts
    pool_w = u.shape[2]
    hb = ts // POOL_HALO
    n_halo_blocks = n // POOL_HALO
    const2 = lambda b, i: (0, 0)
    tok = lambda b, i: (b, i, 0)
    return pl.pallas_call(
        functools.partial(_mix_kernel, n_seq=n),
        grid=(bsz, nt),
        in_specs=[
            pl.BlockSpec((1, N_HEADS // 2, ts, LANES), lambda b, i: (b, 0, i, 0)),
            pl.BlockSpec((1, ts, pool_w), tok),
            pl.BlockSpec((1, POOL_HALO, pool_w), lambda b, i: (b, jnp.maximum(i * hb - 1, 0), 0)),
            pl.BlockSpec((1, POOL_HALO, pool_w),
                         lambda b, i: (b, jnp.minimum((i + 1) * hb, n_halo_blocks - 1), 0)),
            pl.BlockSpec((1, ts, d), tok),
            pl.BlockSpec((1, N_MOD, d), lambda b, i: (b, 0, 0)),
            pl.BlockSpec(wpool.shape, lambda b, i: (0, 0, 0)),
            pl.BlockSpec((1, pool_w), const2),
            pl.BlockSpec(wout.shape, const2),
            pl.BlockSpec((1, d), const2),
            pl.BlockSpec(rw.shape, const2),
            pl.BlockSpec((1, LANES), const2),
        ],
        out_specs=[
            pl.BlockSpec((1, ts, d), tok),
            pl.BlockSpec((1, ts, d // 2), tok),
            pl.BlockSpec((SUBLANES, ts), lambda b, i: (0, b * nt + i)),
            pl.BlockSpec((2 * TOP_K, ts), lambda b, i: (0, b * nt + i)),
            pl.BlockSpec((SUBLANES, LANES), const2),
        ],
        out_shape=[
            jax.ShapeDtypeStruct((bsz, n, d), F32),
            jax.ShapeDtypeStruct((bsz, n, d // 2), jnp.uint32),
            jax.ShapeDtypeStruct((SUBLANES, bsz * n), F32),
            jax.ShapeDtypeStruct((2 * TOP_K, bsz * n), jnp.int32),
            jax.ShapeDtypeStruct((SUBLANES, LANES), F32),
        ],
        scratch_shapes=[
            pltpu.VMEM((ts + 2 * POOL_HALO, pool_w), F32),
            pltpu.VMEM((SUBLANES, LANES), F32),
        ],
        compiler_params=pltpu.CompilerParams(
            dimension_semantics=("arbitrary", "arbitrary"), vmem_limit_bytes=VMEM_SMALL),
        name="mix",
    )(att, u, u, u, x, mod, wpool, pscale, wout, g2, rw, rb)


def _sc_mesh():
    return plsc.VectorSubcoreMesh(core_axis_name="core", subcore_axis_name="subcore")


def _sc_worker_id():
    info = plsc.get_sparse_core_info()
    return lax.axis_index("subcore") * info.num_cores + lax.axis_index("core")


def _sc_num_workers():
    info = plsc.get_sparse_core_info()
    return info.num_cores * info.num_subcores


def _scatter_rows(x, dest3, n_out):
    t, d = x.shape
    top_k, n_chunks, rows = dest3.shape
    assert n_chunks % _sc_num_workers() == 0 and n_chunks * rows == t
    per_worker = n_chunks // _sc_num_workers()

    @functools.partial(
        pl.kernel,
        out_type=jax.ShapeDtypeStruct((n_out, d), x.dtype),
        mesh=_sc_mesh(),
        scratch_types=[pltpu.VMEM((top_k, rows), jnp.int32), pltpu.VMEM((rows, d), x.dtype)],
        name="sc_scatter",
    )
    def scatter(x_hbm, i_hbm, o_hbm, idx_v, rows_v):
        wid = _sc_worker_id()

        @pl.loop(0, per_worker)
        def _(c):
            chunk = wid * per_worker + c
            for kk in range(top_k):
                pltpu.sync_copy(i_hbm.at[kk, chunk], idx_v.at[kk])
            pltpu.sync_copy(x_hbm.at[pl.ds(pl.multiple_of(chunk * rows, SUBLANES), rows)], rows_v)
            for kk in range(top_k):
                pltpu.sync_copy(rows_v, o_hbm.at[idx_v.at[kk]])

    return scatter(x, dest3)


def _gather_rows(y, idx):
    n = idx.shape[0]
    d = y.shape[1]
    rows = SC_ROWS
    assert n % (rows * _sc_num_workers()) == 0
    per_worker = n // _sc_num_workers()
    n_chunks = per_worker // rows

    @functools.partial(
        pl.kernel,
        out_type=jax.ShapeDtypeStruct((n, d), y.dtype),
        mesh=_sc_mesh(),
        scratch_types=[pltpu.VMEM((rows,), jnp.int32), pltpu.VMEM((rows, d), y.dtype)],
        name="sc_gather",
    )
    def gather(y_hbm, i_hbm, o_hbm, idx_v, rows_v):
        base = _sc_worker_id() * per_worker

        @pl.loop(0, n_chunks)
        def _(c):
            off = pl.multiple_of(base + c * rows, SUBLANES)
            pltpu.sync_copy(i_hbm.at[pl.ds(off, rows)], idx_v)
            pltpu.sync_copy(y_hbm.at[idx_v], rows_v)
            pltpu.sync_copy(rows_v, o_hbm.at[pl.ds(off, rows)])

    return gather(y, idx)


def _moe_kernel(blk_ref, exp_ref, lo_ref, hi_ref, new_ref, init_ref, next_ref, slot_ref, xs_ref,
                wgu_hbm, bgu_ref, wd_hbm, bd_ref, ys_ref, wgu_buf, wd_buf, sems):
    i = pl.program_id(0)
    slot = slot_ref[i]

    def weight_copies(e, s):
        return (pltpu.make_async_copy(wgu_hbm.at[e], wgu_buf.at[s], sems.at[s, 0]),
                pltpu.make_async_copy(wd_hbm.at[e], wd_buf.at[s], sems.at[s, 1]))

    @pl.when(i == 0)
    def _():
        for cp in weight_copies(exp_ref[0], slot):
            cp.start()

    @pl.when(new_ref[i] == 1)
    def _():
        for cp in weight_copies(exp_ref[i], slot):
            cp.wait()

        @pl.when(next_ref[i] >= 0)
        def _():
            for cp in weight_copies(next_ref[i], 1 - slot):
                cp.start()

    @pl.when(init_ref[i] == 1)
    def _():
        ys_ref[...] = jnp.zeros_like(ys_ref)

    lo = lo_ref[i]
    hi = hi_ref[i]

    def expert_rows(r0, n_rows):
        de = wd_buf.shape[1]
        x = _unpack_bf16_pairs(xs_ref[pl.ds(r0, n_rows), :])
        gu = _dot(x, wgu_buf[slot]) + bgu_ref[0]
        g = jnp.minimum(gu[:, :de], SWIGLU_LIMIT)
        lin = jnp.clip(gu[:, de:], -SWIGLU_LIMIT, SWIGLU_LIMIT)
        act = g / (1.0 + jnp.exp(-SWIGLU_ALPHA * g)) * (lin + 1.0)
        y = _dot(act, wd_buf[slot]) + bd_ref[0]
        row = r0 + lax.broadcasted_iota(jnp.int32, (n_rows, 1), 0)
        mine = jnp.logical_and(row >= lo, row < hi)
        ys_ref[pl.ds(r0, n_rows), :] = jnp.where(mine, _pack_bf16_pairs(y), ys_ref[pl.ds(r0, n_rows), :])

    first = lo // MOE_SUB
    pieces = jnp.where(hi > lo, (hi + MOE_SUB - 1) // MOE_SUB - first, 0)
    for cnt in range(1, MOE_BLOCK // MOE_SUB + 1):
        @pl.when(pieces == cnt)
        def _(cnt=cnt):
            r0 = 0 if cnt * MOE_SUB == MOE_BLOCK else pl.multiple_of(first * MOE_SUB, MOE_SUB)
            expert_rows(r0, cnt * MOE_SUB)


def _moe_call(sched, xs, wgu, bgu, wd, bd):
    n_items = sched[0].shape[0]
    _, d, de2 = wgu.shape
    de = de2 // 2
    wmap = lambda i, blk, exp, lo, hi, new, init, nxt, slot: (exp[i], 0, 0)
    xmap = lambda i, blk, exp, lo, hi, new, init, nxt, slot: (blk[i], 0)
    return pl.pallas_call(
        _moe_kernel,
        grid_spec=pltpu.PrefetchScalarGridSpec(
            num_scalar_prefetch=8,
            grid=(n_items,),
            in_specs=[
                pl.BlockSpec((MOE_BLOCK, d // 2), xmap),
                pl.BlockSpec(memory_space=pl.ANY),
                pl.BlockSpec((1, 1, de2), wmap),
                pl.BlockSpec(memory_space=pl.ANY),
                pl.BlockSpec((1, 1, d), wmap),
            ],
            out_specs=pl.BlockSpec((MOE_BLOCK, d // 2), xmap),
            scratch_shapes=[
                pltpu.VMEM((2, d, de2), F32), pltpu.VMEM((2, de, d), F32),
                pltpu.SemaphoreType.DMA((2, 2)),
            ],
        ),
        out_shape=jax.ShapeDtypeStruct(xs.shape, xs.dtype),
        compiler_params=pltpu.CompilerParams(
            dimension_semantics=("arbitrary",), vmem_limit_bytes=VMEM_LARGE),
        name="moe",
    )(*sched, xs, wgu, bgu, wd, bd)


def _moe_schedule(counts, n_rows):
    n_blocks = n_rows // MOE_BLOCK
    n_items = n_blocks + N_EXPERTS
    ends = jnp.cumsum(counts)
    starts = ends - counts
    first_blk = starts // MOE_BLOCK
    last_blk = (ends - 1) // MOE_BLOCK
    items_per = jnp.where(counts > 0, last_blk - first_blk + 1, 0)
    item_ends = jnp.cumsum(items_per)
    item_starts = item_ends - items_per
    total = item_ends[-1]
    it = jnp.arange(n_items, dtype=jnp.int32)
    live = it < total
    itc = jnp.minimum(it, total - 1)
    exp = jnp.sum((item_ends[None, :] <= itc[:, None]).astype(jnp.int32), axis=1)
    is_exp = exp[:, None] == jnp.arange(N_EXPERTS, dtype=jnp.int32)[None, :]
    pick = lambda table: jnp.sum(jnp.where(is_exp, table[None, :], 0), axis=1)
    blk = pick(first_blk) + itc - pick(item_starts)
    lo = jnp.clip(pick(starts) - blk * MOE_BLOCK, 0, MOE_BLOCK)
    hi = jnp.clip(pick(ends) - blk * MOE_BLOCK, 0, MOE_BLOCK)
    hi = jnp.where(live, hi, lo)
    prev_exp = jnp.concatenate([jnp.full((1,), -1, jnp.int32), exp[:-1]])
    prev_blk = jnp.concatenate([jnp.full((1,), -1, jnp.int32), blk[:-1]])
    new = jnp.logical_and(live, exp != prev_exp)
    init = jnp.logical_and(live, blk != prev_blk)
    slot = (jnp.cumsum(new.astype(jnp.int32)) - 1) % 2
    ar = jnp.arange(N_EXPERTS, dtype=jnp.int32)
    later = jnp.logical_and(counts[None, :] > 0, ar[None, :] > ar[:, None])
    next_exp = jnp.min(jnp.where(later, ar[None, :], N_EXPERTS), axis=1)
    next_exp = jnp.where(next_exp == N_EXPERTS, -1, next_exp)
    nxt = pick(next_exp)
    as_i32 = lambda a: a.astype(jnp.int32)
    return tuple(as_i32(a) for a in (blk, exp, lo, hi, new, init, nxt, slot)), starts


def _combine_kernel(*refs):
    yg_refs = refs[:TOP_K]
    x1_ref, gate_ref, mod_ref, fg_ref, o_ref = refs[TOP_K:]
    gt = gate_ref[...]
    gates = jnp.concatenate([gt, jnp.zeros((LANES - gt.shape[0], gt.shape[1]), F32)], axis=0).T
    y = gates[:, 0:1] * _unpack_bf16_pairs(yg_refs[0][...])
    for kk in range(1, TOP_K):
        y = y + gates[:, kk:kk + 1] * _unpack_bf16_pairs(yg_refs[kk][...])
    m = mod_ref[0]
    x2 = x1_ref[...] + m[5:6] * y
    o_ref[...] = _rms(x2) * fg_ref[...]


def _combine_call(yg, x1, gates, mod, fg, n_seq, part):
    t, d = x1.shape
    tt = COMBINE_TT
    tiles_per_seq = n_seq // tt
    n_tiles = yg.shape[0] // (TOP_K * tt)
    first = part * n_tiles
    tok = lambda i: (first + i, 0)
    slot_specs = [pl.BlockSpec((tt, d // 2), functools.partial(lambda kk, i: (kk * n_tiles + i, 0), kk))
                  for kk in range(TOP_K)]
    return pl.pallas_call(
        _combine_kernel,
        grid=(n_tiles,),
        in_specs=slot_specs + [
            pl.BlockSpec((tt, d), tok),
            pl.BlockSpec((SUBLANES, tt), lambda i: (0, first + i)),
            pl.BlockSpec((1, N_MOD, d), lambda i: ((first + i) // tiles_per_seq, 0, 0)),
            pl.BlockSpec((1, d), lambda i: (0, 0)),
        ],
        out_specs=pl.BlockSpec((tt, d), tok),
        out_shape=jax.ShapeDtypeStruct((t, d), F32),
        input_output_aliases={TOP_K: 0},
        compiler_params=pltpu.CompilerParams(
            dimension_semantics=("arbitrary",), vmem_limit_bytes=VMEM_SMALL),
        name="combine",
    )(*([yg] * TOP_K), x1, gates, mod, fg)


def _pad_cols(a, width):
    return jnp.pad(a, ((0, 0), (0, width - a.shape[1])))


def _rope_slab(w_rope):
    return jnp.pad(w_rope, ((0, 0), (QK_NOPE, LANES - QK_NOPE - QK_ROPE)))


def _prep_w_in(w_in):
    kr0 = Q_LORA + KV_LORA
    w_kr = w_in[:, kr0:kr0 + QK_ROPE]
    return jnp.concatenate(
        [w_in[:, :kr0], _rope_slab(w_kr), w_in[:, kr0 + QK_ROPE:]], axis=1).astype(BF16)


def _prep_w_uq(w_uq):
    per = QK_NOPE + QK_ROPE
    rows = w_uq.shape[0]
    w = w_uq.reshape(rows, N_HEADS, per)
    slab = jnp.pad(w, ((0, 0), (0, 0), (0, LANES - per)))
    return slab.reshape(rows, -1).T.astype(BF16)


def _prep_w_ukv(w_ukv):
    per = QK_NOPE + V_DIM
    rows = w_ukv.shape[0]
    w = w_ukv.reshape(rows, N_HEADS, per)
    wk = jnp.pad(w[:, :, :QK_NOPE], ((0, 0), (0, 0), (0, LANES - QK_NOPE))).reshape(rows, -1)
    wvt = jnp.pad(jnp.transpose(w[:, :, QK_NOPE:], (1, 2, 0)), ((0, 0), (0, VT_ROWS - V_DIM), (0, 0)))
    return wk.astype(BF16), wvt.reshape(N_HEADS * VT_ROWS, rows).astype(BF16)


def _rope_tables(n_lat):
    rows = n_lat // GRID_W
    nf = QK_ROPE // 4
    row = jnp.repeat(jnp.arange(rows, dtype=F32), GRID_W)
    col = jnp.tile(jnp.arange(GRID_W, dtype=F32), rows)
    freqs = ROPE_BASE ** (-jnp.arange(nf, dtype=F32) / nf)
    ang = jnp.concatenate([row[:, None] * freqs, col[:, None] * freqs], axis=-1)
    cos, sin = jnp.cos(ang), jnp.sin(ang)
    ones = jnp.ones((n_lat, QK_NOPE), F32)
    zeros = jnp.zeros((n_lat, QK_NOPE), F32)
    cs = _pad_cols(jnp.concatenate([ones, cos, cos], axis=1), LANES)
    sn = _pad_cols(jnp.concatenate([zeros, -sin, sin], axis=1), LANES)
    return cs, sn


def kernel(x, c, ctx, c_ctx, w_mod, b_mod, norm1_g, w_in, q_norm_g, kv_norm_g, w_uq, w_ukv, w_pool,
           pool_scale, w_out, norm2_g, router_w, router_b, w_gate_up, b_gate_up, w_down, b_down,
           final_g):
    bsz, n, d = x.shape
    n_ctx = ctx.shape[1]
    t = bsz * n
    assert w_mod.shape[0] == 1, "single-layer block"
    l = 0
    tp = t // COMBINE_PARTS
    assert bsz + 1 <= SUBLANES and d % (2 * LANES) == 0 and n_ctx % ATTN_KC == 0
    assert n % GRID_W == 0 and n % FRONT_TS == 0 and n % MIX_TS == 0 and n % ATTN_KC == 0
    assert n % (ATTN_TQ * ATTN_TILES) == 0 and (t * TOP_K) % MOE_BLOCK == 0 and MOE_BLOCK % MOE_SUB == 0
    assert t % COMBINE_PARTS == 0 and tp % COMBINE_TT == 0 and n % COMBINE_TT == 0 and t % SC_ROWS == 0

    cc = jnp.concatenate([c, c_ctx[None, :], jnp.zeros((SUBLANES - bsz - 1, d), F32)], axis=0)
    mod = _mod_call(cc, w_mod[l], b_mod[l][None, :]).reshape(SUBLANES, N_MOD, d)
    mod_lat, mod_ctx = mod[:bsz], mod[bsz:bsz + 1]

    win = _prep_w_in(w_in[l])
    wuqt = _prep_w_uq(w_uq[l])
    wk, wvt = _prep_w_ukv(w_ukv[l])
    cs, sn = _rope_tables(n)
    cs_ctx = jnp.broadcast_to((jnp.arange(LANES) < QK_NOPE + QK_ROPE).astype(F32), (n_ctx, LANES))
    sn_ctx = jnp.zeros((n_ctx, LANES), F32)
    g1 = norm1_g[l][None, :]
    qg = q_norm_g[l][None, :]
    kvg = kv_norm_g[l][None, :]

    qt, k, vt, u = _front_call(x, mod_lat, True, g1, win, qg, kvg, wuqt, wk, wvt, cs, sn,
                               is_ctx=False, ts=FRONT_TS)
    kc, vct = _front_call(ctx, mod_ctx, False, g1, win[:, Q_LORA:Q_LORA + 2 * LANES], qg, kvg, wuqt,
                          wk, wvt, cs_ctx, sn_ctx, is_ctx=True, ts=n_ctx)
    att = _attn_call(qt, k, vt, kc, vct)

    rw = _pad_cols(router_w[l], LANES)
    rw_hi = rw.astype(BF16)
    rw = jnp.concatenate([rw_hi, (rw - rw_hi.astype(F32)).astype(BF16)], axis=1)
    rb = jnp.concatenate([router_b[l], jnp.full((LANES - N_EXPERTS,), -jnp.inf, F32)])[None, :]
    x1, h2, gates, ridx, cnt = _mix_call(
        att, u, x, mod_lat, w_pool[l].astype(BF16), pool_scale[l][None, :], w_out[l].astype(BF16),
        norm2_g[l][None, :], rw, rb)

    counts = cnt[0, :N_EXPERTS].astype(jnp.int32)
    sched, starts = _moe_schedule(counts, t * TOP_K)
    is_exp = ridx[:TOP_K, None, :] == jnp.arange(N_EXPERTS, dtype=jnp.int32)[None, :, None]
    dest_t = jnp.sum(jnp.where(is_exp, starts[None, :, None], 0), axis=1) + ridx[TOP_K:]

    xs = _scatter_rows(h2.reshape(t, d // 2), dest_t.reshape(TOP_K, t // SC_ROWS, SC_ROWS), t * TOP_K)
    ys = _moe_call(sched, xs, w_gate_up[l], b_gate_up[l][:, None, :], w_down[l],
                   b_down[l][:, None, :])
    out = x1.reshape(t, d)
    for part in range(COMBINE_PARTS):
        idx = dest_t[:, part * tp:(part + 1) * tp].reshape(-1)
        yg = _gather_rows(ys, idx)
        out = _combine_call(yg, out, gates, mod_lat, final_g[None, :], n, part)
    return out.reshape(bsz, n, d)
```

```python
import functools
import math

import jax
import jax.numpy as jnp
from jax import lax
from jax.experimental import pallas as pl
from jax.experimental.pallas import tpu as pltpu
from jax.experimental.pallas import tpu_sc as plsc

F32 = jnp.float32
BF16 = jnp.bfloat16
HIGHEST = lax.Precision.HIGHEST

N_HEADS = 8
QK_NOPE = 64
QK_ROPE = 32
V_DIM = 64
Q_LORA = 256
KV_LORA = 128
GRID_W = 64
ROPE_BASE = 10000.0
POOL_WINDOWS = (2, 4, 8, 16)
POOL_CH = 128
N_EXPERTS = 32
TOP_K = 4
SWIGLU_LIMIT = 7.0
SWIGLU_ALPHA = 1.702
N_MOD = 6
EPS = 1e-6

LANES = 128
SUBLANES = 8
BF16_SUBLANES = 16
MIB = 1024 * 1024

POOL_HALO = max(POOL_WINDOWS) // 2
VT_ROWS = -(-(V_DIM + 1) // BF16_SUBLANES) * BF16_SUBLANES

FRONT_TS = 1024
ATTN_TQ = 512
ATTN_TILES = 4
ATTN_KC = 256
MIX_TS = 512
MOE_BLOCK = 512
MOE_SUB = 64
COMBINE_TT = 512
COMBINE_PARTS = 8
SC_ROWS = 128

VMEM_SMALL = 32 * MIB
VMEM_LARGE = 56 * MIB


def _dot(a, b, **kw):
    return jnp.dot(a, b, preferred_element_type=F32, **kw)


def _rms(x):
    return x * lax.rsqrt(jnp.mean(x * x, axis=-1, keepdims=True) + EPS)


def _pack_bf16_pairs(x):
    w = x.shape[1] // 2
    return pltpu.pack_elementwise([x[:, :w], x[:, w:]], packed_dtype=BF16)


def _unpack_bf16_pairs(words):
    halves = [pltpu.unpack_elementwise(words, index=h, packed_dtype=BF16, unpacked_dtype=F32)
              for h in range(2)]
    return jnp.concatenate(halves, axis=-1)


def _mod_kernel(c_ref, w_ref, b_ref, o_ref):
    c = c_ref[...]
    a = c / (1.0 + jnp.exp(-c))
    o_ref[...] = _dot(a, w_ref[...], precision=HIGHEST) + b_ref[...]


def _mod_call(cc, w_mod, b_mod):
    d = w_mod.shape[0]
    return pl.pallas_call(
        _mod_kernel,
        grid=(N_MOD,),
        in_specs=[
            pl.BlockSpec((SUBLANES, d), lambda j: (0, 0)),
            pl.BlockSpec((d, d), lambda j: (0, j)),
            pl.BlockSpec((1, d), lambda j: (0, j)),
        ],
        out_specs=pl.BlockSpec((SUBLANES, d), lambda j: (0, j)),
        out_shape=jax.ShapeDtypeStruct((SUBLANES, N_MOD * d), F32),
        compiler_params=pltpu.CompilerParams(
            dimension_semantics=("arbitrary",), vmem_limit_bytes=VMEM_SMALL),
        name="mod",
    )(cc, w_mod, b_mod)


def _front_kernel(x_ref, mod_ref, g1_ref, win_ref, qg_ref, kvg_ref, wuqt_ref, wk_ref, wvt_ref,
                  cs_ref, sn_ref, cst_ref, snt_ref, *out_refs, is_ctx, scale):
    x = x_ref[0]
    m = mod_ref[0]
    h = _rms(x) * g1_ref[...] * (1.0 + m[1:2]) + m[0:1]
    z = _dot(h.astype(BF16), win_ref[...])
    if is_ctx:
        k_ref, vt_ref = out_refs
        zkv = z
    else:
        qt_ref, k_ref, vt_ref, u_ref = out_refs
        zkv = z[:, Q_LORA:Q_LORA + 2 * LANES]
        u_ref[0] = z[:, Q_LORA + 2 * LANES:]

    ckv = _rms(zkv[:, :KV_LORA]) * kvg_ref[...]
    kk = _dot(ckv.astype(BF16), wk_ref[...])
    kra = zkv[:, LANES:2 * LANES]
    lane = lax.broadcasted_iota(jnp.int32, kra.shape, 1)
    half = QK_ROPE // 2
    partner = jnp.where(lane < QK_NOPE + half, pltpu.roll(kra, LANES - half, axis=1),
                        pltpu.roll(kra, half, axis=1))
    kr = kra * cs_ref[...] + partner * sn_ref[...]
    for hd in range(N_HEADS):
        k_ref[0, hd] = (kk[:, hd * LANES:(hd + 1) * LANES] + kr).astype(BF16)
    vt = _dot(wvt_ref[...], ckv.T.astype(BF16))
    row = lax.broadcasted_iota(jnp.int32, (VT_ROWS, 1), 0)
    ones = jnp.where(row == V_DIM, 1.0, 0.0)
    for hd in range(N_HEADS):
        vt_ref[0, hd] = (vt[hd * VT_ROWS:(hd + 1) * VT_ROWS] + ones).astype(BF16)

    if not is_ctx:
        cq = _rms(z[:, :Q_LORA]) * qg_ref[...]
        qqt = _dot(wuqt_ref[...], cq.T.astype(BF16))
        cst = cst_ref[...]
        snt = snt_ref[...]
        r1, r2, r3 = QK_NOPE, QK_NOPE + QK_ROPE // 2, QK_NOPE + QK_ROPE
        for hd in range(N_HEADS):
            qa = qqt[hd * LANES:(hd + 1) * LANES]
            qb = jnp.concatenate([qa[:r1], qa[r2:r3], qa[r1:r2], qa[r3:]], axis=0)
            qt_ref[0, hd] = ((qa * cst + qb * snt) * scale).astype(BF16)


def _front_call(xs, mod, mod_per_batch, g1, win, qg, kvg, wuqt, wk, wvt, cs, sn, *, is_ctx, ts):
    bsz, n, d = xs.shape
    nt = n // ts
    scale = math.log2(math.e) / math.sqrt(QK_NOPE + QK_ROPE)
    const = lambda b, i: (0, 0)
    mod_map = (lambda b, i: (b, 0, 0)) if mod_per_batch else (lambda b, i: (0, 0, 0))
    in_specs = [
        pl.BlockSpec((1, ts, d), lambda b, i: (b, i, 0)),
        pl.BlockSpec((1, N_MOD, d), mod_map),
        pl.BlockSpec((1, d), const),
        pl.BlockSpec(win.shape, const),
        pl.BlockSpec((1, Q_LORA), const),
        pl.BlockSpec((1, KV_LORA), const),
        pl.BlockSpec(wuqt.shape, const),
        pl.BlockSpec(wk.shape, const),
        pl.BlockSpec(wvt.shape, const),
        pl.BlockSpec((ts, LANES), lambda b, i: (i, 0)),
        pl.BlockSpec((ts, LANES), lambda b, i: (i, 0)),
        pl.BlockSpec((LANES, ts), lambda b, i: (0, i)),
        pl.BlockSpec((LANES, ts), lambda b, i: (0, i)),
    ]
    k_spec = pl.BlockSpec((1, N_HEADS, ts, LANES), lambda b, i: (b, 0, i, 0))
    k_shape = jax.ShapeDtypeStruct((bsz, N_HEADS, n, LANES), BF16)
    qt_spec = pl.BlockSpec((1, N_HEADS, LANES, ts), lambda b, i: (b, 0, 0, i))
    qt_shape = jax.ShapeDtypeStruct((bsz, N_HEADS, LANES, n), BF16)
    vt_spec = pl.BlockSpec((1, N_HEADS, VT_ROWS, ts), lambda b, i: (b, 0, 0, i))
    vt_shape = jax.ShapeDtypeStruct((bsz, N_HEADS, VT_ROWS, n), BF16)
    if is_ctx:
        out_specs = [k_spec, vt_spec]
        out_shape = [k_shape, vt_shape]
    else:
        pool_w = win.shape[1] - Q_LORA - 2 * LANES
        out_specs = [qt_spec, k_spec, vt_spec, pl.BlockSpec((1, ts, pool_w), lambda b, i: (b, i, 0))]
        out_shape = [qt_shape, k_shape, vt_shape, jax.ShapeDtypeStruct((bsz, n, pool_w), F32)]
    return pl.pallas_call(
        functools.partial(_front_kernel, is_ctx=is_ctx, scale=scale),
        grid=(bsz, nt),
        in_specs=in_specs,
        out_specs=out_specs,
        out_shape=out_shape,
        compiler_params=pltpu.CompilerParams(
            dimension_semantics=("arbitrary", "arbitrary"), vmem_limit_bytes=VMEM_LARGE),
        name="front_ctx" if is_ctx else "front",
    )(xs, mod, g1, win, qg, kvg, wuqt, wk, wvt, cs, sn, cs.T, sn.T)


def _attn_kernel(qt_ref, k_ref, vt_ref, kc_ref, vct_ref, o_ref, s_ref, mlc_ref, mxb_ref, oe_ref):
    n_ctx = kc_ref.shape[2]
    n_lat = k_ref.shape[2]
    chunks = [(None, 0, n_ctx)] + [(c * ATTN_KC, n_ctx + c * ATTN_KC, ATTN_KC)
                                   for c in range(n_lat // ATTN_KC)]
    sub = SUBLANES

    def q_cols(tile):
        if isinstance(tile, int):
            return pl.ds(tile * ATTN_TQ, ATTN_TQ)
        return pl.ds(pl.multiple_of(tile * ATTN_TQ, ATTN_TQ), ATTN_TQ)

    def score_chunk(tile, hd, buf, ci):
        off, soff, w = chunks[ci]
        keys = kc_ref[0, hd] if off is None else k_ref[0, hd, off:off + w, :]
        s = _dot(keys, qt_ref[0, hd, :, q_cols(tile)])
        s_ref[buf, soff:soff + w, :] = s
        mx = s[0:sub]
        for r in range(1, w // sub):
            mx = jnp.maximum(mx, s[r * sub:(r + 1) * sub])
        mlc_ref[buf, ci] = mx

    def row_max(buf):
        mx = mlc_ref[buf, 0]
        for ci in range(1, len(chunks)):
            mx = jnp.maximum(mx, mlc_ref[buf, ci])
        mxb_ref[...] = jnp.broadcast_to(jnp.max(mx, axis=0, keepdims=True), mx.shape)

    def weight_chunk(hd, buf, ci):
        off, soff, w = chunks[ci]
        p = jnp.exp2(s_ref[buf, soff:soff + w, :] - mxb_ref[0:1, :]).astype(BF16)
        vt = vct_ref[0, hd] if off is None else vt_ref[0, hd, :, off:off + w]
        return _dot(vt, p)

    def stage(hw, bw, scoring, bs):
        row_max(bw)
        acc = None
        for ci in range(len(chunks)):
            if scoring is not None:
                score_chunk(*scoring, bs, ci)
            part = weight_chunk(hw, bw, ci)
            acc = part if acc is None else acc + part
        return acc[0:V_DIM] / acc[V_DIM:V_DIM + 1]

    def write_pair(tile, j, ot_odd):
        pair_t = jnp.concatenate([oe_ref[...], ot_odd], axis=0)
        o_ref[0, j, q_cols(tile), :] = pair_t.T.astype(BF16)

    pairs_per_tile = N_HEADS // 2
    n_pairs = (qt_ref.shape[3] // ATTN_TQ) * pairs_per_tile

    def split(p):
        return p // pairs_per_tile, p % pairs_per_tile

    for ci in range(len(chunks)):
        score_chunk(0, 0, 0, ci)

    def head_pair(p, carry):
        tile, j = split(p)
        nxt_tile, nxt_j = split(p + 1)
        oe_ref[...] = stage(2 * j, 0, (tile, 2 * j + 1), 1)
        write_pair(tile, j, stage(2 * j + 1, 1, (nxt_tile, 2 * nxt_j), 0))
        return carry

    lax.fori_loop(0, n_pairs - 1, head_pair, 0)
    tile, j = split(n_pairs - 1)
    oe_ref[...] = stage(2 * j, 0, (tile, 2 * j + 1), 1)
    write_pair(tile, j, stage(2 * j + 1, 1, None, None))


def _attn_call(qt, k, vt, kc, vct):
    bsz, _, _, n = qt.shape
    n_ctx = kc.shape[2]
    tq = ATTN_TQ
    n_chunks = 1 + n // ATTN_KC
    per_batch = lambda b, i: (b, 0, 0, 0)
    resident = dict(pipeline_mode=pl.Buffered(1))
    rows_per_step = ATTN_TILES * tq
    return pl.pallas_call(
        _attn_kernel,
        grid=(bsz, n // rows_per_step),
        in_specs=[
            pl.BlockSpec((1, N_HEADS, LANES, rows_per_step), lambda b, i: (b, 0, 0, i)),
            pl.BlockSpec((1, N_HEADS, n, LANES), per_batch, **resident),
            pl.BlockSpec((1, N_HEADS, VT_ROWS, n), per_batch, **resident),
            pl.BlockSpec((1, N_HEADS, n_ctx, LANES), per_batch, **resident),
            pl.BlockSpec((1, N_HEADS, VT_ROWS, n_ctx), per_batch, **resident),
        ],
        out_specs=pl.BlockSpec((1, N_HEADS // 2, rows_per_step, LANES), lambda b, i: (b, 0, i, 0)),
        out_shape=jax.ShapeDtypeStruct((bsz, N_HEADS // 2, n, LANES), BF16),
        scratch_shapes=[
            pltpu.VMEM((2, n_ctx + n, tq), F32),
            pltpu.VMEM((2, n_chunks, SUBLANES, tq), F32),
            pltpu.VMEM((SUBLANES, tq), F32),
            pltpu.VMEM((V_DIM, tq), F32),
        ],
        compiler_params=pltpu.CompilerParams(
            dimension_semantics=("arbitrary", "arbitrary"), vmem_limit_bytes=VMEM_LARGE),
        name="attn",
    )(qt, k, vt, kc, vct)


def _mix_kernel(att_ref, u_ref, up_ref, un_ref, x_ref, mod_ref, wpool_ref, pscale_ref, wout_ref,
                g2_ref, rw_ref, rb_ref,
                x1_ref, h2_ref, gate_ref, ridx_ref, cnt_ref, ue_ref, base_ref, *, n_seq):
    b = pl.program_id(0)
    i = pl.program_id(1)
    n_tiles = pl.num_programs(1)
    ts = u_ref.shape[1]

    @pl.when(jnp.logical_and(b == 0, i == 0))
    def _():
        base_ref[...] = jnp.zeros_like(base_ref)

    u = u_ref[0]
    ue_ref[0:POOL_HALO] = jnp.where(i > 0, up_ref[0], 0.0)
    ue_ref[POOL_HALO:POOL_HALO + ts] = u
    ue_ref[POOL_HALO + ts:2 * POOL_HALO + ts] = jnp.where(i < n_tiles - 1, un_ref[0], 0.0)
    t = i * ts + lax.broadcasted_iota(jnp.int32, (ts, 1), 0)
    ys = []
    for g, w in enumerate(POOL_WINDOWS):
        half = w // 2
        lanes = slice(g * POOL_CH, (g + 1) * POOL_CH)
        ws = ue_ref[POOL_HALO - half:POOL_HALO - half + ts, lanes]
        for jj in range(-half + 1, half):
            ws = ws + ue_ref[POOL_HALO + jj:POOL_HALO + jj + ts, lanes]
        count = (jnp.minimum(t + half, n_seq) - jnp.maximum(t - half, 0)).astype(F32)
        mixed = (ws / count - u[:, lanes]).astype(BF16)
        ys.append(_dot(mixed, wpool_ref[g]))
    pool = jnp.concatenate(ys, axis=-1) * pscale_ref[...]

    cat = jnp.concatenate([att_ref[0, j] for j in range(N_HEADS // 2)] + [pool.astype(BF16)], axis=-1)
    m = mod_ref[0]
    x1 = x_ref[0] + m[2:3] * _dot(cat, wout_ref[...])
    x1_ref[0] = x1
    h2 = _rms(x1) * g2_ref[...] * (1.0 + m[4:5]) + m[3:4]
    h2_ref[0] = _pack_bf16_pairs(h2)

    h_hi = h2.astype(BF16)
    h_lo = (h2 - h_hi.astype(F32)).astype(BF16)
    hi_part = _dot(h_hi, rw_ref[...])
    logits = (hi_part[:, :LANES] + hi_part[:, LANES:] + _dot(h_lo, rw_ref[:, :LANES])) + rb_ref[...]
    lane = lax.broadcasted_iota(jnp.int32, logits.shape, 1).astype(F32)
    vals, idxs = [], []
    for _k in range(TOP_K):
        mv = jnp.max(logits, axis=-1, keepdims=True)
        ix = jnp.min(jnp.where(logits == mv, lane, float(LANES)), axis=-1, keepdims=True)
        vals.append(mv)
        idxs.append(ix)
        logits = jnp.where(lane == ix, -jnp.inf, logits)
    es = [jnp.exp(v - vals[0]) for v in vals]
    den = es[0] + es[1] + es[2] + es[3]

    onehot = jnp.zeros(lane.shape, F32)
    for ix in idxs:
        onehot = onehot + jnp.where(lane == ix, 1.0, 0.0)
    row = lax.broadcasted_iota(jnp.int32, (ts, ts), 0)
    col = lax.broadcasted_iota(jnp.int32, (ts, ts), 1)
    tri = jnp.where(col < row, 1.0, 0.0).astype(BF16)
    before = _dot(tri, onehot.astype(BF16)) + base_ref[0:1, :]
    base_new = base_ref[0:1, :] + jnp.sum(onehot, axis=0, keepdims=True)
    base_ref[...] = jnp.broadcast_to(base_new, base_ref.shape)
    cnt_ref[...] = jnp.broadcast_to(base_new, cnt_ref.shape)

    route = jnp.zeros(lane.shape, F32)
    for kk in range(TOP_K):
        rank = jnp.sum(jnp.where(lane == idxs[kk], before, 0.0), axis=-1, keepdims=True)
        route = jnp.where(lane == float(kk), idxs[kk], route)
        route = jnp.where(lane == float(TOP_K + kk), rank, route)
        route = jnp.where(lane == float(2 * TOP_K + kk), es[kk] / den, route)
    route_t = route.T
    ridx_ref[...] = route_t[0:2 * TOP_K].astype(jnp.int32)
    gate_ref[...] = route_t[2 * TOP_K:2 * TOP_K + SUBLANES]


def _mix_call(att, u, x, mod, wpool, pscale, wout, g2, rw, rb):
    bsz, n, d = x.shape
    ts = MIX_TS
    nt = n // ts
    pool_w = u.shape[2]
    hb = ts // POOL_HALO
    n_halo_blocks = n // POOL_HALO
    const2 = lambda b, i: (0, 0)
    tok = lambda b, i: (b, i, 0)
    return pl.pallas_call(
        functools.partial(_mix_kernel, n_seq=n),
        grid=(bsz, nt),
        in_specs=[
            pl.BlockSpec((1, N_HEADS // 2, ts, LANES), lambda b, i: (b, 0, i, 0)),
            pl.BlockSpec((1, ts, pool_w), tok),
            pl.BlockSpec((1, POOL_HALO, pool_w), lambda b, i: (b, jnp.maximum(i * hb - 1, 0), 0)),
            pl.BlockSpec((1, POOL_HALO, pool_w),
                         lambda b, i: (b, jnp.minimum((i + 1) * hb, n_halo_blocks - 1), 0)),
            pl.BlockSpec((1, ts, d), tok),
            pl.BlockSpec((1, N_MOD, d), lambda b, i: (b, 0, 0)),
            pl.BlockSpec(wpool.shape, lambda b, i: (0, 0, 0)),
            pl.BlockSpec((1, pool_w), const2),
            pl.BlockSpec(wout.shape, const2),
            pl.BlockSpec((1, d), const2),
            pl.BlockSpec(rw.shape, const2),
            pl.BlockSpec((1, LANES), const2),
        ],
        out_specs=[
            pl.BlockSpec((1, ts, d), tok),
            pl.BlockSpec((1, ts, d // 2), tok),
            pl.BlockSpec((SUBLANES, ts), lambda b, i: (0, b * nt + i)),
            pl.BlockSpec((2 * TOP_K, ts), lambda b, i: (0, b * nt + i)),
            pl.BlockSpec((SUBLANES, LANES), const2),
        ],
        out_shape=[
            jax.ShapeDtypeStruct((bsz, n, d), F32),
            jax.ShapeDtypeStruct((bsz, n, d // 2), jnp.uint32),
            jax.ShapeDtypeStruct((SUBLANES, bsz * n), F32),
            jax.ShapeDtypeStruct((2 * TOP_K, bsz * n), jnp.int32),
            jax.ShapeDtypeStruct((SUBLANES, LANES), F32),
        ],
        scratch_shapes=[
            pltpu.VMEM((ts + 2 * POOL_HALO, pool_w), F32),
            pltpu.VMEM((SUBLANES, LANES), F32),
        ],
        compiler_params=pltpu.CompilerParams(
            dimension_semantics=("arbitrary", "arbitrary"), vmem_limit_bytes=VMEM_SMALL),
        name="mix",
    )(att, u, u, u, x, mod, wpool, pscale, wout, g2, rw, rb)


def _sc_mesh():
    return plsc.VectorSubcoreMesh(core_axis_name="core", subcore_axis_name="subcore")


def _sc_worker_id():
    info = plsc.get_sparse_core_info()
    return lax.axis_index("subcore") * info.num_cores + lax.axis_index("core")


def _sc_num_workers():
    info = plsc.get_sparse_core_info()
    return info.num_cores * info.num_subcores


def _scatter_rows(x, dest3, n_out):
    t, d = x.shape
    top_k, n_chunks, rows = dest3.shape
    assert n_chunks % _sc_num_workers() == 0 and n_chunks * rows == t
    per_worker = n_chunks // _sc_num_workers()

    @functools.partial(
        pl.kernel,
        out_type=jax.ShapeDtypeStruct((n_out, d), x.dtype),
        mesh=_sc_mesh(),
        scratch_types=[pltpu.VMEM((top_k, rows), jnp.int32), pltpu.VMEM((rows, d), x.dtype)],
        name="sc_scatter",
    )
    def scatter(x_hbm, i_hbm, o_hbm, idx_v, rows_v):
        wid = _sc_worker_id()

        @pl.loop(0, per_worker)
        def _(c):
            chunk = wid * per_worker + c
            for kk in range(top_k):
                pltpu.sync_copy(i_hbm.at[kk, chunk], idx_v.at[kk])
            pltpu.sync_copy(x_hbm.at[pl.ds(pl.multiple_of(chunk * rows, SUBLANES), rows)], rows_v)
            for kk in range(top_k):
                pltpu.sync_copy(rows_v, o_hbm.at[idx_v.at[kk]])

    return scatter(x, dest3)


def _gather_rows(y, idx):
    n = idx.shape[0]
    d = y.shape[1]
    rows = SC_ROWS
    assert n % (rows * _sc_num_workers()) == 0
    per_worker = n // _sc_num_workers()
    n_chunks = per_worker // rows

    @functools.partial(
        pl.kernel,
        out_type=jax.ShapeDtypeStruct((n, d), y.dtype),
        mesh=_sc_mesh(),
        scratch_types=[pltpu.VMEM((rows,), jnp.int32), pltpu.VMEM((rows, d), y.dtype)],
        name="sc_gather",
    )
    def gather(y_hbm, i_hbm, o_hbm, idx_v, rows_v):
        base = _sc_worker_id() * per_worker

        @pl.loop(0, n_chunks)
        def _(c):
            off = pl.multiple_of(base + c * rows, SUBLANES)
            pltpu.sync_copy(i_hbm.at[pl.ds(off, rows)], idx_v)
            pltpu.sync_copy(y_hbm.at[idx_v], rows_v)
            pltpu.sync_copy(rows_v, o_hbm.at[pl.ds(off, rows)])

    return gather(y, idx)


def _moe_kernel(blk_ref, exp_ref, lo_ref, hi_ref, new_ref, init_ref, next_ref, slot_ref, xs_ref,
                wgu_hbm, bgu_ref, wd_hbm, bd_ref, ys_ref, wgu_buf, wd_buf, sems):
    i = pl.program_id(0)
    slot = slot_ref[i]

    def weight_copies(e, s):
        return (pltpu.make_async_copy(wgu_hbm.at[e], wgu_buf.at[s], sems.at[s, 0]),
                pltpu.make_async_copy(wd_hbm.at[e], wd_buf.at[s], sems.at[s, 1]))

    @pl.when(i == 0)
    def _():
        for cp in weight_copies(exp_ref[0], slot):
            cp.start()

    @pl.when(new_ref[i] == 1)
    def _():
        for cp in weight_copies(exp_ref[i], slot):
            cp.wait()

        @pl.when(next_ref[i] >= 0)
        def _():
            for cp in weight_copies(next_ref[i], 1 - slot):
                cp.start()

    @pl.when(init_ref[i] == 1)
    def _():
        ys_ref[...] = jnp.zeros_like(ys_ref)

    lo = lo_ref[i]
    hi = hi_ref[i]

    def expert_rows(r0, n_rows):
        de = wd_buf.shape[1]
        x = _unpack_bf16_pairs(xs_ref[pl.ds(r0, n_rows), :])
        gu = _dot(x, wgu_buf[slot]) + bgu_ref[0]
        g = jnp.minimum(gu[:, :de], SWIGLU_LIMIT)
        lin = jnp.clip(gu[:, de:], -SWIGLU_LIMIT, SWIGLU_LIMIT)
        act = g / (1.0 + jnp.exp(-SWIGLU_ALPHA * g)) * (lin + 1.0)
        y = _dot(act, wd_buf[slot]) + bd_ref[0]
        row = r0 + lax.broadcasted_iota(jnp.int32, (n_rows, 1), 0)
        mine = jnp.logical_and(row >= lo, row < hi)
        ys_ref[pl.ds(r0, n_rows), :] = jnp.where(mine, _pack_bf16_pairs(y), ys_ref[pl.ds(r0, n_rows), :])

    first = lo // MOE_SUB
    pieces = jnp.where(hi > lo, (hi + MOE_SUB - 1) // MOE_SUB - first, 0)
    for cnt in range(1, MOE_BLOCK // MOE_SUB + 1):
        @pl.when(pieces == cnt)
        def _(cnt=cnt):
            r0 = 0 if cnt * MOE_SUB == MOE_BLOCK else pl.multiple_of(first * MOE_SUB, MOE_SUB)
            expert_rows(r0, cnt * MOE_SUB)


def _moe_call(sched, xs, wgu, bgu, wd, bd):
    n_items = sched[0].shape[0]
    _, d, de2 = wgu.shape
    de = de2 // 2
    wmap = lambda i, blk, exp, lo, hi, new, init, nxt, slot: (exp[i], 0, 0)
    xmap = lambda i, blk, exp, lo, hi, new, init, nxt, slot: (blk[i], 0)
    return pl.pallas_call(
        _moe_kernel,
        grid_spec=pltpu.PrefetchScalarGridSpec(
            num_scalar_prefetch=8,
            grid=(n_items,),
            in_specs=[
                pl.BlockSpec((MOE_BLOCK, d // 2), xmap),
                pl.BlockSpec(memory_space=pl.ANY),
                pl.BlockSpec((1, 1, de2), wmap),
                pl.BlockSpec(memory_space=pl.ANY),
                pl.BlockSpec((1, 1, d), wmap),
            ],
            out_specs=pl.BlockSpec((MOE_BLOCK, d // 2), xmap),
            scratch_shapes=[
                pltpu.VMEM((2, d, de2), F32), pltpu.VMEM((2, de, d), F32),
                pltpu.SemaphoreType.DMA((2, 2)),
            ],
        ),
        out_shape=jax.ShapeDtypeStruct(xs.shape, xs.dtype),
        compiler_params=pltpu.CompilerParams(
            dimension_semantics=("arbitrary",), vmem_limit_bytes=VMEM_LARGE),
        name="moe",
    )(*sched, xs, wgu, bgu, wd, bd)


def _moe_schedule(counts, n_rows):
    n_blocks = n_rows // MOE_BLOCK
    n_items = n_blocks + N_EXPERTS
    ends = jnp.cumsum(counts)
    starts = ends - counts
    first_blk = starts // MOE_BLOCK
    last_blk = (ends - 1) // MOE_BLOCK
    items_per = jnp.where(counts > 0, last_blk - first_blk + 1, 0)
    item_ends = jnp.cumsum(items_per)
    item_starts = item_ends - items_per
    total = item_ends[-1]
    it = jnp.arange(n_items, dtype=jnp.int32)
    live = it < total
    itc = jnp.minimum(it, total - 1)
    exp = jnp.sum((item_ends[None, :] <= itc[:, None]).astype(jnp.int32), axis=1)
    is_exp = exp[:, None] == jnp.arange(N_EXPERTS, dtype=jnp.int32)[None, :]
    pick = lambda table: jnp.sum(jnp.where(is_exp, table[None, :], 0), axis=1)
    blk = pick(first_blk) + itc - pick(item_starts)
    lo = jnp.clip(pick(starts) - blk * MOE_BLOCK, 0, MOE_BLOCK)
    hi = jnp.clip(pick(ends) - blk * MOE_BLOCK, 0, MOE_BLOCK)
    hi = jnp.where(live, hi, lo)
    prev_exp = jnp.concatenate([jnp.full((1,), -1, jnp.int32), exp[:-1]])
    prev_blk = jnp.concatenate([jnp.full((1,), -1, jnp.int32), blk[:-1]])
    new = jnp.logical_and(live, exp != prev_exp)
    init = jnp.logical_and(live, blk != prev_blk)
    slot = (jnp.cumsum(new.astype(jnp.int32)) - 1) % 2
    ar = jnp.arange(N_EXPERTS, dtype=jnp.int32)
    later = jnp.logical_and(counts[None, :] > 0, ar[None, :] > ar[:, None])
    next_exp = jnp.min(jnp.where(later, ar[None, :], N_EXPERTS), axis=1)
    next_exp = jnp.where(next_exp == N_EXPERTS, -1, next_exp)
    nxt = pick(next_exp)
    as_i32 = lambda a: a.astype(jnp.int32)
    return tuple(as_i32(a) for a in (blk, exp, lo, hi, new, init, nxt, slot)), starts


def _combine_kernel(*refs):
    yg_refs = refs[:TOP_K]
    x1_ref, gate_ref, mod_ref, fg_ref, o_ref = refs[TOP_K:]
    gt = gate_ref[...]
    gates = jnp.concatenate([gt, jnp.zeros((LANES - gt.shape[0], gt.shape[1]), F32)], axis=0).T
    y = gates[:, 0:1] * _unpack_bf16_pairs(yg_refs[0][...])
    for kk in range(1, TOP_K):
        y = y + gates[:, kk:kk + 1] * _unpack_bf16_pairs(yg_refs[kk][...])
    m = mod_ref[0]
    x2 = x1_ref[...] + m[5:6] * y
    o_ref[...] = _rms(x2) * fg_ref[...]


def _combine_call(yg, x1, gates, mod, fg, n_seq, part):
    t, d = x1.shape
    tt = COMBINE_TT
    tiles_per_seq = n_seq // tt
    n_tiles = yg.shape[0] // (TOP_K * tt)
    first = part * n_tiles
    tok = lambda i: (first + i, 0)
    slot_specs = [pl.BlockSpec((tt, d // 2), functools.partial(lambda kk, i: (kk * n_tiles + i, 0), kk))
                  for kk in range(TOP_K)]
    return pl.pallas_call(
        _combine_kernel,
        grid=(n_tiles,),
        in_specs=slot_specs + [
            pl.BlockSpec((tt, d), tok),
            pl.BlockSpec((SUBLANES, tt), lambda i: (0, first + i)),
            pl.BlockSpec((1, N_MOD, d), lambda i: ((first + i) // tiles_per_seq, 0, 0)),
            pl.BlockSpec((1, d), lambda i: (0, 0)),
        ],
        out_specs=pl.BlockSpec((tt, d), tok),
        out_shape=jax.ShapeDtypeStruct((t, d), F32),
        input_output_aliases={TOP_K: 0},
        compiler_params=pltpu.CompilerParams(
            dimension_semantics=("arbitrary",), vmem_limit_bytes=VMEM_SMALL),
        name="combine",
    )(*([yg] * TOP_K), x1, gates, mod, fg)


def _pad_cols(a, width):
    return jnp.pad(a, ((0, 0), (0, width - a.shape[1])))


def _rope_slab(w_rope):
    return jnp.pad(w_rope, ((0, 0), (QK_NOPE, LANES - QK_NOPE - QK_ROPE)))


def _prep_w_in(w_in):
    kr0 = Q_LORA + KV_LORA
    w_kr = w_in[:, kr0:kr0 + QK_ROPE]
    return jnp.concatenate(
        [w_in[:, :kr0], _rope_slab(w_kr), w_in[:, kr0 + QK_ROPE:]], axis=1).astype(BF16)


def _prep_w_uq(w_uq):
    per = QK_NOPE + QK_ROPE
    rows = w_uq.shape[0]
    w = w_uq.reshape(rows, N_HEADS, per)
    slab = jnp.pad(w, ((0, 0), (0, 0), (0, LANES - per)))
    return slab.reshape(rows, -1).T.astype(BF16)


def _prep_w_ukv(w_ukv):
    per = QK_NOPE + V_DIM
    rows = w_ukv.shape[0]
    w = w_ukv.reshape(rows, N_HEADS, per)
    wk = jnp.pad(w[:, :, :QK_NOPE], ((0, 0), (0, 0), (0, LANES - QK_NOPE))).reshape(rows, -1)
    wvt = jnp.pad(jnp.transpose(w[:, :, QK_NOPE:], (1, 2, 0)), ((0, 0), (0, VT_ROWS - V_DIM), (0, 0)))
    return wk.astype(BF16), wvt.reshape(N_HEADS * VT_ROWS, rows).astype(BF16)


def _rope_tables(n_lat):
    rows = n_lat // GRID_W
    nf = QK_ROPE // 4
    row = jnp.repeat(jnp.arange(rows, dtype=F32), GRID_W)
    col = jnp.tile(jnp.arange(GRID_W, dtype=F32), rows)
    freqs = ROPE_BASE ** (-jnp.arange(nf, dtype=F32) / nf)
    ang = jnp.concatenate([row[:, None] * freqs, col[:, None] * freqs], axis=-1)
    cos, sin = jnp.cos(ang), jnp.sin(ang)
    ones = jnp.ones((n_lat, QK_NOPE), F32)
    zeros = jnp.zeros((n_lat, QK_NOPE), F32)
    cs = _pad_cols(jnp.concatenate([ones, cos, cos], axis=1), LANES)
    sn = _pad_cols(jnp.concatenate([zeros, -sin, sin], axis=1), LANES)
    return cs, sn


def kernel(x, c, ctx, c_ctx, w_mod, b_mod, norm1_g, w_in, q_norm_g, kv_norm_g, w_uq, w_ukv, w_pool,
           pool_scale, w_out, norm2_g, router_w, router_b, w_gate_up, b_gate_up, w_down, b_down,
           final_g):
    bsz, n, d = x.shape
    n_ctx = ctx.shape[1]
    t = bsz * n
    assert w_mod.shape[0] == 1, "single-layer block"
    l = 0
    tp = t // COMBINE_PARTS
    assert bsz + 1 <= SUBLANES and d % (2 * LANES) == 0 and n_ctx % ATTN_KC == 0
    assert n % GRID_W == 0 and n % FRONT_TS == 0 and n % MIX_TS == 0 and n % ATTN_KC == 0
    assert n % (ATTN_TQ * ATTN_TILES) == 0 and (t * TOP_K) % MOE_BLOCK == 0 and MOE_BLOCK % MOE_SUB == 0
    assert t % COMBINE_PARTS == 0 and tp % COMBINE_TT == 0 and n % COMBINE_TT == 0 and t % SC_ROWS == 0

    cc = jnp.concatenate([c, c_ctx[None, :], jnp.zeros((SUBLANES - bsz - 1, d), F32)], axis=0)
    mod = _mod_call(cc, w_mod[l], b_mod[l][None, :]).reshape(SUBLANES, N_MOD, d)
    mod_lat, mod_ctx = mod[:bsz], mod[bsz:bsz + 1]

    win = _prep_w_in(w_in[l])
    wuqt = _prep_w_uq(w_uq[l])
    wk, wvt = _prep_w_ukv(w_ukv[l])
    cs, sn = _rope_tables(n)
    cs_ctx = jnp.broadcast_to((jnp.arange(LANES) < QK_NOPE + QK_ROPE).astype(F32), (n_ctx, LANES))
    sn_ctx = jnp.zeros((n_ctx, LANES), F32)
    g1 = norm1_g[l][None, :]
    qg = q_norm_g[l][None, :]
    kvg = kv_norm_g[l][None, :]

    qt, k, vt, u = _front_call(x, mod_lat, True, g1, win, qg, kvg, wuqt, wk, wvt, cs, sn,
                               is_ctx=False, ts=FRONT_TS)
    kc, vct = _front_call(ctx, mod_ctx, False, g1, win[:, Q_LORA:Q_LORA + 2 * LANES], qg, kvg, wuqt,
                          wk, wvt, cs_ctx, sn_ctx, is_ctx=True, ts=n_ctx)
    att = _attn_call(qt, k, vt, kc, vct)

    rw = _pad_cols(router_w[l], LANES)
    rw_hi = rw.astype(BF16)
    rw = jnp.concatenate([rw_hi, (rw - rw_hi.astype(F32)).astype(BF16)], axis=1)
    rb = jnp.concatenate([router_b[l], jnp.full((LANES - N_EXPERTS,), -jnp.inf, F32)])[None, :]
    x1, h2, gates, ridx, cnt = _mix_call(
        att, u, x, mod_lat, w_pool[l].astype(BF16), pool_scale[l][None, :], w_out[l].astype(BF16),
        norm2_g[l][None, :], rw, rb)

    counts = cnt[0, :N_EXPERTS].astype(jnp.int32)
    sched, starts = _moe_schedule(counts, t * TOP_K)
    is_exp = ridx[:TOP_K, None, :] == jnp.arange(N_EXPERTS, dtype=jnp.int32)[None, :, None]
    dest_t = jnp.sum(jnp.where(is_exp, starts[None, :, None], 0), axis=1) + ridx[TOP_K:]

    xs = _scatter_rows(h2.reshape(t, d // 2), dest_t.reshape(TOP_K, t // SC_ROWS, SC_ROWS), t * TOP_K)
    ys = _moe_call(sched, xs, w_gate_up[l], b_gate_up[l][:, None, :], w_down[l],
                   b_down[l][:, None, :])
    out = x1.reshape(t, d)
    for part in range(COMBINE_PARTS):
        idx = dest_t[:, part * tp:(part + 1) * tp].reshape(-1)
        yg = _gather_rows(ys, idx)
        out = _combine_call(yg, out, gates, mod_lat, final_g[None, :], n, part)
    return out.reshape(bsz, n, d)
```

```python
import functools
import math

import jax
import jax.numpy as jnp
from jax import lax
from jax.experimental import pallas as pl
from jax.experimental.pallas import tpu as pltpu
from jax.experimental.pallas import tpu_sc as plsc

F32 = jnp.float32
BF16 = jnp.bfloat16
HIGHEST = lax.Precision.HIGHEST

N_HEADS = 8
QK_NOPE = 64
QK_ROPE = 32
V_DIM = 64
Q_LORA = 256
KV_LORA = 128
GRID_W = 64
ROPE_BASE = 10000.0
POOL_WINDOWS = (2, 4, 8, 16)
POOL_CH = 128
N_EXPERTS = 32
TOP_K = 4
SWIGLU_LIMIT = 7.0
SWIGLU_ALPHA = 1.702
N_MOD = 6
EPS = 1e-6

LANES = 128
SUBLANES = 8
BF16_SUBLANES = 16
MIB = 1024 * 1024

POOL_HALO = max(POOL_WINDOWS) // 2
VT_ROWS = -(-(V_DIM + 1) // BF16_SUBLANES) * BF16_SUBLANES

FRONT_TS = 1024
ATTN_TQ = 512
ATTN_TILES = 4
ATTN_KC = 256
MIX_TS = 512
MOE_BLOCK = 512
MOE_SUB = 64
COMBINE_TT = 512
COMBINE_PARTS = 8
SC_ROWS = 128

VMEM_SMALL = 32 * MIB
VMEM_LARGE = 56 * MIB


def _dot(a, b, **kw):
    return jnp.dot(a, b, preferred_element_type=F32, **kw)


def _rms(x):
    return x * lax.rsqrt(jnp.mean(x * x, axis=-1, keepdims=True) + EPS)


def _pack_bf16_pairs(x):
    w = x.shape[1] // 2
    return pltpu.pack_elementwise([x[:, :w], x[:, w:]], packed_dtype=BF16)


def _unpack_bf16_pairs(words):
    halves = [pltpu.unpack_elementwise(words, index=h, packed_dtype=BF16, unpacked_dtype=F32)
              for h in range(2)]
    return jnp.concatenate(halves, axis=-1)


def _mod_kernel(c_ref, w_ref, b_ref, o_ref):
    c = c_ref[...]
    a = c / (1.0 + jnp.exp(-c))
    o_ref[...] = _dot(a, w_ref[...], precision=HIGHEST) + b_ref[...]


def _mod_call(cc, w_mod, b_mod):
    d = w_mod.shape[0]
    return pl.pallas_call(
        _mod_kernel,
        grid=(N_MOD,),
        in_specs=[
            pl.BlockSpec((SUBLANES, d), lambda j: (0, 0)),
            pl.BlockSpec((d, d), lambda j: (0, j)),
            pl.BlockSpec((1, d), lambda j: (0, j)),
        ],
        out_specs=pl.BlockSpec((SUBLANES, d), lambda j: (0, j)),
        out_shape=jax.ShapeDtypeStruct((SUBLANES, N_MOD * d), F32),
        compiler_params=pltpu.CompilerParams(
            dimension_semantics=("arbitrary",), vmem_limit_bytes=VMEM_SMALL),
        name="mod",
    )(cc, w_mod, b_mod)


def _front_kernel(x_ref, mod_ref, g1_ref, win_ref, qg_ref, kvg_ref, wuqt_ref, wk_ref, wvt_ref,
                  cs_ref, sn_ref, cst_ref, snt_ref, *out_refs, is_ctx, scale):
    x = x_ref[0]
    m = mod_ref[0]
    h = _rms(x) * g1_ref[...] * (1.0 + m[1:2]) + m[0:1]
    z = _dot(h.astype(BF16), win_ref[...])
    if is_ctx:
        k_ref, vt_ref = out_refs
        zkv = z
    else:
        qt_ref, k_ref, vt_ref, u_ref = out_refs
        zkv = z[:, Q_LORA:Q_LORA + 2 * LANES]
        u_ref[0] = z[:, Q_LORA + 2 * LANES:]

    ckv = _rms(zkv[:, :KV_LORA]) * kvg_ref[...]
    kk = _dot(ckv.astype(BF16), wk_ref[...])
    kra = zkv[:, LANES:2 * LANES]
    lane = lax.broadcasted_iota(jnp.int32, kra.shape, 1)
    half = QK_ROPE // 2
    partner = jnp.where(lane < QK_NOPE + half, pltpu.roll(kra, LANES - half, axis=1),
                        pltpu.roll(kra, half, axis=1))
    kr = kra * cs_ref[...] + partner * sn_ref[...]
    for hd in range(N_HEADS):
        k_ref[0, hd] = (kk[:, hd * LANES:(hd + 1) * LANES] + kr).astype(BF16)
    vt = _dot(wvt_ref[...], ckv.T.astype(BF16))
    row = lax.broadcasted_iota(jnp.int32, (VT_ROWS, 1), 0)
    ones = jnp.where(row == V_DIM, 1.0, 0.0)
    for hd in range(N_HEADS):
        vt_ref[0, hd] = (vt[hd * VT_ROWS:(hd + 1) * VT_ROWS] + ones).astype(BF16)

    if not is_ctx:
        cq = _rms(z[:, :Q_LORA]) * qg_ref[...]
        qqt = _dot(wuqt_ref[...], cq.T.astype(BF16))
        cst = cst_ref[...]
        snt = snt_ref[...]
        r1, r2, r3 = QK_NOPE, QK_NOPE + QK_ROPE // 2, QK_NOPE + QK_ROPE
        for hd in range(N_HEADS):
            qa = qqt[hd * LANES:(hd + 1) * LANES]
            qb = jnp.concatenate([qa[:r1], qa[r2:r3], qa[r1:r2], qa[r3:]], axis=0)
            qt_ref[0, hd] = ((qa * cst + qb * snt) * scale).astype(BF16)


def _front_call(xs, mod, mod_per_batch, g1, win, qg, kvg, wuqt, wk, wvt, cs, sn, *, is_ctx, ts):
    bsz, n, d = xs.shape
    nt = n // ts
    scale = math.log2(math.e) / math.sqrt(QK_NOPE + QK_ROPE)
    const = lambda b, i: (0, 0)
    mod_map = (lambda b, i: (b, 0, 0)) if mod_per_batch else (lambda b, i: (0, 0, 0))
    in_specs = [
        pl.BlockSpec((1, ts, d), lambda b, i: (b, i, 0)),
        pl.BlockSpec((1, N_MOD, d), mod_map),
        pl.BlockSpec((1, d), const),
        pl.BlockSpec(win.shape, const),
        pl.BlockSpec((1, Q_LORA), const),
        pl.BlockSpec((1, KV_LORA), const),
        pl.BlockSpec(wuqt.shape, const),
        pl.BlockSpec(wk.shape, const),
        pl.BlockSpec(wvt.shape, const),
        pl.BlockSpec((ts, LANES), lambda b, i: (i, 0)),
        pl.BlockSpec((ts, LANES), lambda b, i: (i, 0)),
        pl.BlockSpec((LANES, ts), lambda b, i: (0, i)),
        pl.BlockSpec((LANES, ts), lambda b, i: (0, i)),
    ]
    k_spec = pl.BlockSpec((1, N_HEADS, ts, LANES), lambda b, i: (b, 0, i, 0))
    k_shape = jax.ShapeDtypeStruct((bsz, N_HEADS, n, LANES), BF16)
    qt_spec = pl.BlockSpec((1, N_HEADS, LANES, ts), lambda b, i: (b, 0, 0, i))
    qt_shape = jax.ShapeDtypeStruct((bsz, N_HEADS, LANES, n), BF16)
    vt_spec = pl.BlockSpec((1, N_HEADS, VT_ROWS, ts), lambda b, i: (b, 0, 0, i))
    vt_shape = jax.ShapeDtypeStruct((bsz, N_HEADS, VT_ROWS, n), BF16)
    if is_ctx:
        out_specs = [k_spec, vt_spec]
        out_shape = [k_shape, vt_shape]
    else:
        pool_w = win.shape[1] - Q_LORA - 2 * LANES
        out_specs = [qt_spec, k_spec, vt_spec, pl.BlockSpec((1, ts, pool_w), lambda b, i: (b, i, 0))]
        out_shape = [qt_shape, k_shape, vt_shape, jax.ShapeDtypeStruct((bsz, n, pool_w), F32)]
    return pl.pallas_call(
        functools.partial(_front_kernel, is_ctx=is_ctx, scale=scale),
        grid=(bsz, nt),
        in_specs=in_specs,
        out_specs=out_specs,
        out_shape=out_shape,
        compiler_params=pltpu.CompilerParams(
            dimension_semantics=("arbitrary", "arbitrary"), vmem_limit_bytes=VMEM_LARGE),
        name="front_ctx" if is_ctx else "front",
    )(xs, mod, g1, win, qg, kvg, wuqt, wk, wvt, cs, sn, cs.T, sn.T)


def _attn_kernel(qt_ref, k_ref, vt_ref, kc_ref, vct_ref, o_ref, s_ref, mlc_ref, mxb_ref, oe_ref):
    n_ctx = kc_ref.shape[2]
    n_lat = k_ref.shape[2]
    chunks = [(None, 0, n_ctx)] + [(c * ATTN_KC, n_ctx + c * ATTN_KC, ATTN_KC)
                                   for c in range(n_lat // ATTN_KC)]
    sub = SUBLANES

    def q_cols(tile):
        if isinstance(tile, int):
            return pl.ds(tile * ATTN_TQ, ATTN_TQ)
        return pl.ds(pl.multiple_of(tile * ATTN_TQ, ATTN_TQ), ATTN_TQ)

    def score_chunk(tile, hd, buf, ci):
        off, soff, w = chunks[ci]
        keys = kc_ref[0, hd] if off is None else k_ref[0, hd, off:off + w, :]
        s = _dot(keys, qt_ref[0, hd, :, q_cols(tile)])
        s_ref[buf, soff:soff + w, :] = s
        mx = s[0:sub]
        for r in range(1, w // sub):
            mx = jnp.maximum(mx, s[r * sub:(r + 1) * sub])
        mlc_ref[buf, ci] = mx

    def row_max(buf):
        mx = mlc_ref[buf, 0]
        for ci in range(1, len(chunks)):
            mx = jnp.maximum(mx, mlc_ref[buf, ci])
        mxb_ref[...] = jnp.broadcast_to(jnp.max(mx, axis=0, keepdims=True), mx.shape)

    def weight_chunk(hd, buf, ci):
        off, soff, w = chunks[ci]
        p = jnp.exp2(s_ref[buf, soff:soff + w, :] - mxb_ref[0:1, :]).astype(BF16)
        vt = vct_ref[0, hd] if off is None else vt_ref[0, hd, :, off:off + w]
        return _dot(vt, p)

    def stage(hw, bw, scoring, bs):
        row_max(bw)
        acc = None
        for ci in range(len(chunks)):
            if scoring is not None:
                score_chunk(*scoring, bs, ci)
            part = weight_chunk(hw, bw, ci)
            acc = part if acc is None else acc + part
        return acc[0:V_DIM] / acc[V_DIM:V_DIM + 1]

    def write_pair(tile, j, ot_odd):
        pair_t = jnp.concatenate([oe_ref[...], ot_odd], axis=0)
        o_ref[0, j, q_cols(tile), :] = pair_t.T.astype(BF16)

    pairs_per_tile = N_HEADS // 2
    n_pairs = (qt_ref.shape[3] // ATTN_TQ) * pairs_per_tile

    def split(p):
        return p // pairs_per_tile, p % pairs_per_tile

    for ci in range(len(chunks)):
        score_chunk(0, 0, 0, ci)

    def head_pair(p, carry):
        tile, j = split(p)
        nxt_tile, nxt_j = split(p + 1)
        oe_ref[...] = stage(2 * j, 0, (tile, 2 * j + 1), 1)
        write_pair(tile, j, stage(2 * j + 1, 1, (nxt_tile, 2 * nxt_j), 0))
        return carry

    lax.fori_loop(0, n_pairs - 1, head_pair, 0)
    tile, j = split(n_pairs - 1)
    oe_ref[...] = stage(2 * j, 0, (tile, 2 * j + 1), 1)
    write_pair(tile, j, stage(2 * j + 1, 1, None, None))


def _attn_call(qt, k, vt, kc, vct):
    bsz, _, _, n = qt.shape
    n_ctx = kc.shape[2]
    tq = ATTN_TQ
    n_chunks = 1 + n // ATTN_KC
    per_batch = lambda b, i: (b, 0, 0, 0)
    resident = dict(pipeline_mode=pl.Buffered(1))
    rows_per_step = ATTN_TILES * tq
    return pl.pallas_call(
        _attn_kernel,
        grid=(bsz, n // rows_per_step),
        in_specs=[
            pl.BlockSpec((1, N_HEADS, LANES, rows_per_step), lambda b, i: (b, 0, 0, i)),
            pl.BlockSpec((1, N_HEADS, n, LANES), per_batch, **resident),
            pl.BlockSpec((1, N_HEADS, VT_ROWS, n), per_batch, **resident),
            pl.BlockSpec((1, N_HEADS, n_ctx, LANES), per_batch, **resident),
            pl.BlockSpec((1, N_HEADS, VT_ROWS, n_ctx), per_batch, **resident),
        ],
        out_specs=pl.BlockSpec((1, N_HEADS // 2, rows_per_step, LANES), lambda b, i: (b, 0, i, 0)),
        out_shape=jax.ShapeDtypeStruct((bsz, N_HEADS // 2, n, LANES), BF16),
        scratch_shapes=[
            pltpu.VMEM((2, n_ctx + n, tq), F32),
            pltpu.VMEM((2, n_chunks, SUBLANES, tq), F32),
            pltpu.VMEM((SUBLANES, tq), F32),
            pltpu.VMEM((V_DIM, tq), F32),
        ],
        compiler_params=pltpu.CompilerParams(
            dimension_semantics=("arbitrary", "arbitrary"), vmem_limit_bytes=VMEM_LARGE),
        name="attn",
    )(qt, k, vt, kc, vct)


def _mix_kernel(att_ref, u_ref, up_ref, un_ref, x_ref, mod_ref, wpool_ref, pscale_ref, wout_ref,
                g2_ref, rw_ref, rb_ref,
                x1_ref, h2_ref, gate_ref, ridx_ref, cnt_ref, ue_ref, base_ref, *, n_seq):
    b = pl.program_id(0)
    i = pl.program_id(1)
    n_tiles = pl.num_programs(1)
    ts = u_ref.shape[1]

    @pl.when(jnp.logical_and(b == 0, i == 0))
    def _():
        base_ref[...] = jnp.zeros_like(base_ref)

    u = u_ref[0]
    ue_ref[0:POOL_HALO] = jnp.where(i > 0, up_ref[0], 0.0)
    ue_ref[POOL_HALO:POOL_HALO + ts] = u
    ue_ref[POOL_HALO + ts:2 * POOL_HALO + ts] = jnp.where(i < n_tiles - 1, un_ref[0], 0.0)
    t = i * ts + lax.broadcasted_iota(jnp.int32, (ts, 1), 0)
    ys = []
    for g, w in enumerate(POOL_WINDOWS):
        half = w // 2
        lanes = slice(g * POOL_CH, (g + 1) * POOL_CH)
        ws = ue_ref[POOL_HALO - half:POOL_HALO - half + ts, lanes]
        for jj in range(-half + 1, half):
            ws = ws + ue_ref[POOL_HALO + jj:POOL_HALO + jj + ts, lanes]
        count = (jnp.minimum(t + half, n_seq) - jnp.maximum(t - half, 0)).astype(F32)
        mixed = (ws / count - u[:, lanes]).astype(BF16)
        ys.append(_dot(mixed, wpool_ref[g]))
    pool = jnp.concatenate(ys, axis=-1) * pscale_ref[...]

    cat = jnp.concatenate([att_ref[0, j] for j in range(N_HEADS // 2)] + [pool.astype(BF16)], axis=-1)
    m = mod_ref[0]
    x1 = x_ref[0] + m[2:3] * _dot(cat, wout_ref[...])
    x1_ref[0] = x1
    h2 = _rms(x1) * g2_ref[...] * (1.0 + m[4:5]) + m[3:4]
    h2_ref[0] = _pack_bf16_pairs(h2)

    h_hi = h2.astype(BF16)
    h_lo = (h2 - h_hi.astype(F32)).astype(BF16)
    hi_part = _dot(h_hi, rw_ref[...])
    logits = (hi_part[:, :LANES] + hi_part[:, LANES:] + _dot(h_lo, rw_ref[:, :LANES])) + rb_ref[...]
    lane = lax.broadcasted_iota(jnp.int32, logits.shape, 1).astype(F32)
    vals, idxs = [], []
    for _k in range(TOP_K):
        mv = jnp.max(logits, axis=-1, keepdims=True)
        ix = jnp.min(jnp.where(logits == mv, lane, float(LANES)), axis=-1, keepdims=True)
        vals.append(mv)
        idxs.append(ix)
        logits = jnp.where(lane == ix, -jnp.inf, logits)
    es = [jnp.exp(v - vals[0]) for v in vals]
    den = es[0] + es[1] + es[2] + es[3]

    onehot = jnp.zeros(lane.shape, F32)
    for ix in idxs:
        onehot = onehot + jnp.where(lane == ix, 1.0, 0.0)
    row = lax.broadcasted_iota(jnp.int32, (ts, ts), 0)
    col = lax.broadcasted_iota(jnp.int32, (ts, ts), 1)
    tri = jnp.where(col < row, 1.0, 0.0).astype(BF16)
    before = _dot(tri, onehot.astype(BF16)) + base_ref[0:1, :]
    base_new = base_ref[0:1, :] + jnp.sum(onehot, axis=0, keepdims=True)
    base_ref[...] = jnp.broadcast_to(base_new, base_ref.shape)
    cnt_ref[...] = jnp.broadcast_to(base_new, cnt_ref.shape)

    route = jnp.zeros(lane.shape, F32)
    for kk in range(TOP_K):
        rank = jnp.sum(jnp.where(lane == idxs[kk], before, 0.0), axis=-1, keepdims=True)
        route = jnp.where(lane == float(kk), idxs[kk], route)
        route = jnp.where(lane == float(TOP_K + kk), rank, route)
        route = jnp.where(lane == float(2 * TOP_K + kk), es[kk] / den, route)
    route_t = route.T
    ridx_ref[...] = route_t[0:2 * TOP_K].astype(jnp.int32)
    gate_ref[...] = route_t[2 * TOP_K:2 * TOP_K + SUBLANES]


def _mix_call(att, u, x, mod, wpool, pscale, wout, g2, rw, rb):
    bsz, n, d = x.shape
    ts = MIX_TS
    nt = n // ts
    pool_w = u.shape[2]
    hb = ts // POOL_HALO
    n_halo_blocks = n // POOL_HALO
    const2 = lambda b, i: (0, 0)
    tok = lambda b, i: (b, i, 0)
    return pl.pallas_call(
        functools.partial(_mix_kernel, n_seq=n),
        grid=(bsz, nt),
        in_specs=[
            pl.BlockSpec((1, N_HEADS // 2, ts, LANES), lambda b, i: (b, 0, i, 0)),
            pl.BlockSpec((1, ts, pool_w), tok),
            pl.BlockSpec((1, POOL_HALO, pool_w), lambda b, i: (b, jnp.maximum(i * hb - 1, 0), 0)),
            pl.BlockSpec((1, POOL_HALO, pool_w),
                         lambda b, i: (b, jnp.minimum((i + 1) * hb, n_halo_blocks - 1), 0)),
            pl.BlockSpec((1, ts, d), tok),
            pl.BlockSpec((1, N_MOD, d), lambda b, i: (b, 0, 0)),
            pl.BlockSpec(wpool.shape, lambda b, i: (0, 0, 0)),
            pl.BlockSpec((1, pool_w), const2),
            pl.BlockSpec(wout.shape, const2),
            pl.BlockSpec((1, d), const2),
            pl.BlockSpec(rw.shape, const2),
            pl.BlockSpec((1, LANES), const2),
        ],
        out_specs=[
            pl.BlockSpec((1, ts, d), tok),
            pl.BlockSpec((1, ts, d // 2), tok),
            pl.BlockSpec((SUBLANES, ts), lambda b, i: (0, b * nt + i)),
            pl.BlockSpec((2 * TOP_K, ts), lambda b, i: (0, b * nt + i)),
            pl.BlockSpec((SUBLANES, LANES), const2),
        ],
        out_shape=[
            jax.ShapeDtypeStruct((bsz, n, d), F32),
            jax.ShapeDtypeStruct((bsz, n, d // 2), jnp.uint32),
            jax.ShapeDtypeStruct((SUBLANES, bsz * n), F32),
            jax.ShapeDtypeStruct((2 * TOP_K, bsz * n), jnp.int32),
            jax.ShapeDtypeStruct((SUBLANES, LANES), F32),
        ],
        scratch_shapes=[
            pltpu.VMEM((ts + 2 * POOL_HALO, pool_w), F32),
            pltpu.VMEM((SUBLANES, LANES), F32),
        ],
        compiler_params=pltpu.CompilerParams(
            dimension_semantics=("arbitrary", "arbitrary"), vmem_limit_bytes=VMEM_SMALL),
        name="mix",
    )(att, u, u, u, x, mod, wpool, pscale, wout, g2, rw, rb)


def _sc_mesh():
    return plsc.VectorSubcoreMesh(core_axis_name="core", subcore_axis_name="subcore")


def _sc_worker_id():
    info = plsc.get_sparse_core_info()
    return lax.axis_index("subcore") * info.num_cores + lax.axis_index("core")


def _sc_num_workers():
    info = plsc.get_sparse_core_info()
    return info.num_cores * info.num_subcores


def _scatter_rows(x, dest3, n_out):
    t, d = x.shape
    top_k, n_chunks, rows = dest3.shape
    assert n_chunks % _sc_num_workers() == 0 and n_chunks * rows == t
    per_worker = n_chunks // _sc_num_workers()

    @functools.partial(
        pl.kernel,
        out_type=jax.ShapeDtypeStruct((n_out, d), x.dtype),
        mesh=_sc_mesh(),
        scratch_types=[pltpu.VMEM((top_k, rows), jnp.int32), pltpu.VMEM((rows, d), x.dtype)],
        name="sc_scatter",
    )
    def scatter(x_hbm, i_hbm, o_hbm, idx_v, rows_v):
        wid = _sc_worker_id()

        @pl.loop(0, per_worker)
        def _(c):
            chunk = wid * per_worker + c
            for kk in range(top_k):
                pltpu.sync_copy(i_hbm.at[kk, chunk], idx_v.at[kk])
            pltpu.sync_copy(x_hbm.at[pl.ds(pl.multiple_of(chunk * rows, SUBLANES), rows)], rows_v)
            for kk in range(top_k):
                pltpu.sync_copy(rows_v, o_hbm.at[idx_v.at[kk]])

    return scatter(x, dest3)


def _gather_rows(y, idx):
    n = idx.shape[0]
    d = y.shape[1]
    rows = SC_ROWS
    assert n % (rows * _sc_num_workers()) == 0
    per_worker = n // _sc_num_workers()
    n_chunks = per_worker // rows

    @functools.partial(
        pl.kernel,
        out_type=jax.ShapeDtypeStruct((n, d), y.dtype),
        mesh=_sc_mesh(),
        scratch_types=[pltpu.VMEM((rows,), jnp.int32), pltpu.VMEM((rows, d), y.dtype)],
        name="sc_gather",
    )
    def gather(y_hbm, i_hbm, o_hbm, idx_v, rows_v):
        base = _sc_worker_id() * per_worker

        @pl.loop(0, n_chunks)
        def _(c):
            off = pl.multiple_of(base + c * rows, SUBLANES)
            pltpu.sync_copy(i_hbm.at[pl.ds(off, rows)], idx_v)
            pltpu.sync_copy(y_hbm.at[idx_v], rows_v)
            pltpu.sync_copy(rows_v, o_hbm.at[pl.ds(off, rows)])

    return gather(y, idx)


def _moe_kernel(blk_ref, exp_ref, lo_ref, hi_ref, new_ref, init_ref, next_ref, slot_ref, xs_ref,
                wgu_hbm, bgu_ref, wd_hbm, bd_ref, ys_ref, wgu_buf, wd_buf, sems):
    i = pl.program_id(0)
    slot = slot_ref[i]

    def weight_copies(e, s):
        return (pltpu.make_async_copy(wgu_hbm.at[e], wgu_buf.at[s], sems.at[s, 0]),
                pltpu.make_async_copy(wd_hbm.at[e], wd_buf.at[s], sems.at[s, 1]))

    @pl.when(i == 0)
    def _():
        for cp in weight_copies(exp_ref[0], slot):
            cp.start()

    @pl.when(new_ref[i] == 1)
    def _():
        for cp in weight_copies(exp_ref[i], slot):
            cp.wait()

        @pl.when(next_ref[i] >= 0)
        def _():
            for cp in weight_copies(next_ref[i], 1 - slot):
                cp.start()

    @pl.when(init_ref[i] == 1)
    def _():
        ys_ref[...] = jnp.zeros_like(ys_ref)

    lo = lo_ref[i]
    hi = hi_ref[i]

    def expert_rows(r0, n_rows):
        de = wd_buf.shape[1]
        x = _unpack_bf16_pairs(xs_ref[pl.ds(r0, n_rows), :])
        gu = _dot(x, wgu_buf[slot]) + bgu_ref[0]
        g = jnp.minimum(gu[:, :de], SWIGLU_LIMIT)
        lin = jnp.clip(gu[:, de:], -SWIGLU_LIMIT, SWIGLU_LIMIT)
        act = g / (1.0 + jnp.exp(-SWIGLU_ALPHA * g)) * (lin + 1.0)
        y = _dot(act, wd_buf[slot]) + bd_ref[0]
        row = r0 + lax.broadcasted_iota(jnp.int32, (n_rows, 1), 0)
        mine = jnp.logical_and(row >= lo, row < hi)
        ys_ref[pl.ds(r0, n_rows), :] = jnp.where(mine, _pack_bf16_pairs(y), ys_ref[pl.ds(r0, n_rows), :])

    first = lo // MOE_SUB
    pieces = jnp.where(hi > lo, (hi + MOE_SUB - 1) // MOE_SUB - first, 0)
    for cnt in range(1, MOE_BLOCK // MOE_SUB + 1):
        @pl.when(pieces == cnt)
        def _(cnt=cnt):
            r0 = 0 if cnt * MOE_SUB == MOE_BLOCK else pl.multiple_of(first * MOE_SUB, MOE_SUB)
            expert_rows(r0, cnt * MOE_SUB)


def _moe_call(sched, xs, wgu, bgu, wd, bd):
    n_items = sched[0].shape[0]
    _, d, de2 = wgu.shape
    de = de2 // 2
    wmap = lambda i, blk, exp, lo, hi, new, init, nxt, slot: (exp[i], 0, 0)
    xmap = lambda i, blk, exp, lo, hi, new, init, nxt, slot: (blk[i], 0)
    return pl.pallas_call(
        _moe_kernel,
        grid_spec=pltpu.PrefetchScalarGridSpec(
            num_scalar_prefetch=8,
            grid=(n_items,),
            in_specs=[
                pl.BlockSpec((MOE_BLOCK, d // 2), xmap),
                pl.BlockSpec(memory_space=pl.ANY),
                pl.BlockSpec((1, 1, de2), wmap),
                pl.BlockSpec(memory_space=pl.ANY),
                pl.BlockSpec((1, 1, d), wmap),
            ],
            out_specs=pl.BlockSpec((MOE_BLOCK, d // 2), xmap),
            scratch_shapes=[
                pltpu.VMEM((2, d, de2), F32), pltpu.VMEM((2, de, d), F32),
                pltpu.SemaphoreType.DMA((2, 2)),
            ],
        ),
        out_shape=jax.ShapeDtypeStruct(xs.shape, xs.dtype),
        compiler_params=pltpu.CompilerParams(
            dimension_semantics=("arbitrary",), vmem_limit_bytes=VMEM_LARGE),
        name="moe",
    )(*sched, xs, wgu, bgu, wd, bd)


def _moe_schedule(counts, n_rows):
    n_blocks = n_rows // MOE_BLOCK
    n_items = n_blocks + N_EXPERTS
    ends = jnp.cumsum(counts)
    starts = ends - counts
    first_blk = starts // MOE_BLOCK
    last_blk = (ends - 1) // MOE_BLOCK
    items_per = jnp.where(counts > 0, last_blk - first_blk + 1, 0)
    item_ends = jnp.cumsum(items_per)
    item_starts = item_ends - items_per
    total = item_ends[-1]
    it = jnp.arange(n_items, dtype=jnp.int32)
    live = it < total
    itc = jnp.minimum(it, total - 1)
    exp = jnp.sum((item_ends[None, :] <= itc[:, None]).astype(jnp.int32), axis=1)
    is_exp = exp[:, None] == jnp.arange(N_EXPERTS, dtype=jnp.int32)[None, :]
    pick = lambda table: jnp.sum(jnp.where(is_exp, table[None, :], 0), axis=1)
    blk = pick(first_blk) + itc - pick(item_starts)
    lo = jnp.clip(pick(starts) - blk * MOE_BLOCK, 0, MOE_BLOCK)
    hi = jnp.clip(pick(ends) - blk * MOE_BLOCK, 0, MOE_BLOCK)
    hi = jnp.where(live, hi, lo)
    prev_exp = jnp.concatenate([jnp.full((1,), -1, jnp.int32), exp[:-1]])
    prev_blk = jnp.concatenate([jnp.full((1,), -1, jnp.int32), blk[:-1]])
    new = jnp.logical_and(live, exp != prev_exp)
    init = jnp.logical_and(live, blk != prev_blk)
    slot = (jnp.cumsum(new.astype(jnp.int32)) - 1) % 2
    ar = jnp.arange(N_EXPERTS, dtype=jnp.int32)
    later = jnp.logical_and(counts[None, :] > 0, ar[None, :] > ar[:, None])
    next_exp = jnp.min(jnp.where(later, ar[None, :], N_EXPERTS), axis=1)
    next_exp = jnp.where(next_exp == N_EXPERTS, -1, next_exp)
    nxt = pick(next_exp)
    as_i32 = lambda a: a.astype(jnp.int32)
    return tuple(as_i32(a) for a in (blk, exp, lo, hi, new, init, nxt, slot)), starts


def _combine_kernel(*refs):
    yg_refs = refs[:TOP_K]
    x1_ref, gate_ref, mod_ref, fg_ref, o_ref = refs[TOP_K:]
    gt = gate_ref[...]
    gates = jnp.concatenate([gt, jnp.zeros((LANES - gt.shape[0], gt.shape[1]), F32)], axis=0).T
    y = gates[:, 0:1] * _unpack_bf16_pairs(yg_refs[0][...])
    for kk in range(1, TOP_K):
        y = y + gates[:, kk:kk + 1] * _unpack_bf16_pairs(yg_refs[kk][...])
    m = mod_ref[0]
    x2 = x1_ref[...] + m[5:6] * y
    o_ref[...] = _rms(x2) * fg_ref[...]


def _combine_call(yg, x1, gates, mod, fg, n_seq, part):
    t, d = x1.shape
    tt = COMBINE_TT
    tiles_per_seq = n_seq // tt
    n_tiles = yg.shape[0] // (TOP_K * tt)
    first = part * n_tiles
    tok = lambda i: (first + i, 0)
    slot_specs = [pl.BlockSpec((tt, d // 2), functools.partial(lambda kk, i: (kk * n_tiles + i, 0), kk))
                  for kk in range(TOP_K)]
    return pl.pallas_call(
        _combine_kernel,
        grid=(n_tiles,),
        in_specs=slot_specs + [
            pl.BlockSpec((tt, d), tok),
            pl.BlockSpec((SUBLANES, tt), lambda i: (0, first + i)),
            pl.BlockSpec((1, N_MOD, d), lambda i: ((first + i) // tiles_per_seq, 0, 0)),
            pl.BlockSpec((1, d), lambda i: (0, 0)),
        ],
        out_specs=pl.BlockSpec((tt, d), tok),
        out_shape=jax.ShapeDtypeStruct((t, d), F32),
        input_output_aliases={TOP_K: 0},
        compiler_params=pltpu.CompilerParams(
            dimension_semantics=("arbitrary",), vmem_limit_bytes=VMEM_SMALL),
        name="combine",
    )(*([yg] * TOP_K), x1, gates, mod, fg)


def _pad_cols(a, width):
    return jnp.pad(a, ((0, 0), (0, width - a.shape[1])))


def _rope_slab(w_rope):
    return jnp.pad(w_rope, ((0, 0), (QK_NOPE, LANES - QK_NOPE - QK_ROPE)))


def _prep_w_in(w_in):
    kr0 = Q_LORA + KV_LORA
    w_kr = w_in[:, kr0:kr0 + QK_ROPE]
    return jnp.concatenate(
        [w_in[:, :kr0], _rope_slab(w_kr), w_in[:, kr0 + QK_ROPE:]], axis=1).astype(BF16)


def _prep_w_uq(w_uq):
    per = QK_NOPE + QK_ROPE
    rows = w_uq.shape[0]
    w = w_uq.reshape(rows, N_HEADS, per)
    slab = jnp.pad(w, ((0, 0), (0, 0), (0, LANES - per)))
    return slab.reshape(rows, -1).T.astype(BF16)


def _prep_w_ukv(w_ukv):
    per = QK_NOPE + V_DIM
    rows = w_ukv.shape[0]
    w = w_ukv.reshape(rows, N_HEADS, per)
    wk = jnp.pad(w[:, :, :QK_NOPE], ((0, 0), (0, 0), (0, LANES - QK_NOPE))).reshape(rows, -1)
    wvt = jnp.pad(jnp.transpose(w[:, :, QK_NOPE:], (1, 2, 0)), ((0, 0), (0, VT_ROWS - V_DIM), (0, 0)))
    return wk.astype(BF16), wvt.reshape(N_HEADS * VT_ROWS, rows).astype(BF16)


def _rope_tables(n_lat):
    rows = n_lat // GRID_W
    nf = QK_ROPE // 4
    row = jnp.repeat(jnp.arange(rows, dtype=F32), GRID_W)
    col = jnp.tile(jnp.arange(GRID_W, dtype=F32), rows)
    freqs = ROPE_BASE ** (-jnp.arange(nf, dtype=F32) / nf)
    ang = jnp.concatenate([row[:, None] * freqs, col[:, None] * freqs], axis=-1)
    cos, sin = jnp.cos(ang), jnp.sin(ang)
    ones = jnp.ones((n_lat, QK_NOPE), F32)
    zeros = jnp.zeros((n_lat, QK_NOPE), F32)
    cs = _pad_cols(jnp.concatenate([ones, cos, cos], axis=1), LANES)
    sn = _pad_cols(jnp.concatenate([zeros, -sin, sin], axis=1), LANES)
    return cs, sn


def kernel(x, c, ctx, c_ctx, w_mod, b_mod, norm1_g, w_in, q_norm_g, kv_norm_g, w_uq, w_ukv, w_pool,
           pool_scale, w_out, norm2_g, router_w, router_b, w_gate_up, b_gate_up, w_down, b_down,
           final_g):
    bsz, n, d = x.shape
    n_ctx = ctx.shape[1]
    t = bsz * n
    assert w_mod.shape[0] == 1, "single-layer block"
    l = 0
    tp = t // COMBINE_PARTS
    assert bsz + 1 <= SUBLANES and d % (2 * LANES) == 0 and n_ctx % ATTN_KC == 0
    assert POOL_HALO == SUBLANES and MIX_TS % POOL_HALO == 0
    assert n % GRID_W == 0 and n % FRONT_TS == 0 and n % MIX_TS == 0 and n % ATTN_KC == 0
    assert n % (ATTN_TQ * ATTN_TILES) == 0 and (t * TOP_K) % MOE_BLOCK == 0 and MOE_BLOCK % MOE_SUB == 0
    assert t % COMBINE_PARTS == 0 and tp % COMBINE_TT == 0 and n % COMBINE_TT == 0 and t % SC_ROWS == 0

    cc = jnp.concatenate([c, c_ctx[None, :], jnp.zeros((SUBLANES - bsz - 1, d), F32)], axis=0)
    mod = _mod_call(cc, w_mod[l], b_mod[l][None, :]).reshape(SUBLANES, N_MOD, d)
    mod_lat, mod_ctx = mod[:bsz], mod[bsz:bsz + 1]

    win = _prep_w_in(w_in[l])
    wuqt = _prep_w_uq(w_uq[l])
    wk, wvt = _prep_w_ukv(w_ukv[l])
    cs, sn = _rope_tables(n)
    cs_ctx = jnp.broadcast_to((jnp.arange(LANES) < QK_NOPE + QK_ROPE).astype(F32), (n_ctx, LANES))
    sn_ctx = jnp.zeros((n_ctx, LANES), F32)
    g1 = norm1_g[l][None, :]
    qg = q_norm_g[l][None, :]
    kvg = kv_norm_g[l][None, :]

    qt, k, vt, u = _front_call(x, mod_lat, True, g1, win, qg, kvg, wuqt, wk, wvt, cs, sn,
                               is_ctx=False, ts=FRONT_TS)
    kc, vct = _front_call(ctx, mod_ctx, False, g1, win[:, Q_LORA:Q_LORA + 2 * LANES], qg, kvg, wuqt,
                          wk, wvt, cs_ctx, sn_ctx, is_ctx=True, ts=n_ctx)
    att = _attn_call(qt, k, vt, kc, vct)

    rw = _pad_cols(router_w[l], LANES)
    rw_hi = rw.astype(BF16)
    rw = jnp.concatenate([rw_hi, (rw - rw_hi.astype(F32)).astype(BF16)], axis=1)
    rb = jnp.concatenate([router_b[l], jnp.full((LANES - N_EXPERTS,), -jnp.inf, F32)])[None, :]
    x1, h2, gates, ridx, cnt = _mix_call(
        att, u, x, mod_lat, w_pool[l].astype(BF16), pool_scale[l][None, :], w_out[l].astype(BF16),
        norm2_g[l][None, :], rw, rb)

    counts = cnt[0, :N_EXPERTS].astype(jnp.int32)
    sched, starts = _moe_schedule(counts, t * TOP_K)
    is_exp = ridx[:TOP_K, None, :] == jnp.arange(N_EXPERTS, dtype=jnp.int32)[None, :, None]
    dest_t = jnp.sum(jnp.where(is_exp, starts[None, :, None], 0), axis=1) + ridx[TOP_K:]

    xs = _scatter_rows(h2.reshape(t, d // 2), dest_t.reshape(TOP_K, t // SC_ROWS, SC_ROWS), t * TOP_K)
    ys = _moe_call(sched, xs, w_gate_up[l], b_gate_up[l][:, None, :], w_down[l],
                   b_down[l][:, None, :])
    out = x1.reshape(t, d)
    for part in range(COMBINE_PARTS):
        idx = dest_t[:, part * tp:(part + 1) * tp].reshape(-1)
        yg = _gather_rows(ys, idx)
        out = _combine_call(yg, out, gates, mod_lat, final_g[None, :], n, part)
    return out.reshape(bsz, n, d)
```

```python
import functools
import math

import jax
import jax.numpy as jnp
from jax import lax
from jax.experimental import pallas as pl
from jax.experimental.pallas import tpu as pltpu
from jax.experimental.pallas import tpu_sc as plsc

F32 = jnp.float32
BF16 = jnp.bfloat16
HIGHEST = lax.Precision.HIGHEST

N_HEADS = 8
QK_NOPE = 64
QK_ROPE = 32
V_DIM = 64
Q_LORA = 256
KV_LORA = 128
GRID_W = 64
ROPE_BASE = 10000.0
POOL_WINDOWS = (2, 4, 8, 16)
POOL_CH = 128
N_EXPERTS = 32
TOP_K = 4
SWIGLU_LIMIT = 7.0
SWIGLU_ALPHA = 1.702
N_MOD = 6
EPS = 1e-6

LANES = 128
SUBLANES = 8
BF16_SUBLANES = 16
MIB = 1024 * 1024

POOL_HALO = max(POOL_WINDOWS) // 2
VT_ROWS = -(-(V_DIM + 1) // BF16_SUBLANES) * BF16_SUBLANES

FRONT_TS = 1024
ATTN_TQ = 512
ATTN_TILES = 4
ATTN_KC = 256
MIX_TS = 512
MOE_BLOCK = 512
MOE_SUB = 64
COMBINE_TT = 512
COMBINE_PARTS = 16
SC_ROWS = 128

VMEM_SMALL = 32 * MIB
VMEM_LARGE = 56 * MIB


def _dot(a, b, **kw):
    return jnp.dot(a, b, preferred_element_type=F32, **kw)


def _rms(x):
    return x * lax.rsqrt(jnp.mean(x * x, axis=-1, keepdims=True) + EPS)


def _pack_bf16_pairs(x):
    w = x.shape[1] // 2
    return pltpu.pack_elementwise([x[:, :w], x[:, w:]], packed_dtype=BF16)


def _unpack_bf16_pairs(words):
    halves = [pltpu.unpack_elementwise(words, index=h, packed_dtype=BF16, unpacked_dtype=F32)
              for h in range(2)]
    return jnp.concatenate(halves, axis=-1)


def _mod_kernel(c_ref, w_ref, b_ref, o_ref):
    c = c_ref[...]
    a = c / (1.0 + jnp.exp(-c))
    o_ref[...] = _dot(a, w_ref[...], precision=HIGHEST) + b_ref[...]


def _mod_call(cc, w_mod, b_mod):
    d = w_mod.shape[0]
    return pl.pallas_call(
        _mod_kernel,
        grid=(N_MOD,),
        in_specs=[
            pl.BlockSpec((SUBLANES, d), lambda j: (0, 0)),
            pl.BlockSpec((d, d), lambda j: (0, j)),
            pl.BlockSpec((1, d), lambda j: (0, j)),
        ],
        out_specs=pl.BlockSpec((SUBLANES, d), lambda j: (0, j)),
        out_shape=jax.ShapeDtypeStruct((SUBLANES, N_MOD * d), F32),
        compiler_params=pltpu.CompilerParams(
            dimension_semantics=("arbitrary",), vmem_limit_bytes=VMEM_SMALL),
        name="mod",
    )(cc, w_mod, b_mod)


def _front_kernel(x_ref, mod_ref, g1_ref, win_ref, qg_ref, kvg_ref, wuqt_ref, wk_ref, wvt_ref,
                  cs_ref, sn_ref, cst_ref, snt_ref, *out_refs, is_ctx, scale):
    x = x_ref[0]
    m = mod_ref[0]
    h = _rms(x) * g1_ref[...] * (1.0 + m[1:2]) + m[0:1]
    z = _dot(h.astype(BF16), win_ref[...])
    if is_ctx:
        k_ref, vt_ref = out_refs
        zkv = z
    else:
        qt_ref, k_ref, vt_ref, u_ref = out_refs
        zkv = z[:, Q_LORA:Q_LORA + 2 * LANES]
        u_ref[0] = z[:, Q_LORA + 2 * LANES:]

    ckv = _rms(zkv[:, :KV_LORA]) * kvg_ref[...]
    kk = _dot(ckv.astype(BF16), wk_ref[...])
    kra = zkv[:, LANES:2 * LANES]
    lane = lax.broadcasted_iota(jnp.int32, kra.shape, 1)
    half = QK_ROPE // 2
    partner = jnp.where(lane < QK_NOPE + half, pltpu.roll(kra, LANES - half, axis=1),
                        pltpu.roll(kra, half, axis=1))
    kr = kra * cs_ref[...] + partner * sn_ref[...]
    for hd in range(N_HEADS):
        k_ref[0, hd] = (kk[:, hd * LANES:(hd + 1) * LANES] + kr).astype(BF16)
    vt = _dot(wvt_ref[...], ckv.T.astype(BF16))
    row = lax.broadcasted_iota(jnp.int32, (VT_ROWS, 1), 0)
    ones = jnp.where(row == V_DIM, 1.0, 0.0)
    for hd in range(N_HEADS):
        vt_ref[0, hd] = (vt[hd * VT_ROWS:(hd + 1) * VT_ROWS] + ones).astype(BF16)

    if not is_ctx:
        cq = _rms(z[:, :Q_LORA]) * qg_ref[...]
        qqt = _dot(wuqt_ref[...], cq.T.astype(BF16))
        cst = cst_ref[...]
        snt = snt_ref[...]
        r1, r2, r3 = QK_NOPE, QK_NOPE + QK_ROPE // 2, QK_NOPE + QK_ROPE
        for hd in range(N_HEADS):
            qa = qqt[hd * LANES:(hd + 1) * LANES]
            qb = jnp.concatenate([qa[:r1], qa[r2:r3], qa[r1:r2], qa[r3:]], axis=0)
            qt_ref[0, hd] = ((qa * cst + qb * snt) * scale).astype(BF16)


def _front_call(xs, mod, mod_per_batch, g1, win, qg, kvg, wuqt, wk, wvt, cs, sn, *, is_ctx, ts):
    bsz, n, d = xs.shape
    nt = n // ts
    scale = math.log2(math.e) / math.sqrt(QK_NOPE + QK_ROPE)
    const = lambda b, i: (0, 0)
    mod_map = (lambda b, i: (b, 0, 0)) if mod_per_batch else (lambda b, i: (0, 0, 0))
    in_specs = [
        pl.BlockSpec((1, ts, d), lambda b, i: (b, i, 0)),
        pl.BlockSpec((1, N_MOD, d), mod_map),
        pl.BlockSpec((1, d), const),
        pl.BlockSpec(win.shape, const),
        pl.BlockSpec((1, Q_LORA), const),
        pl.BlockSpec((1, KV_LORA), const),
        pl.BlockSpec(wuqt.shape, const),
        pl.BlockSpec(wk.shape, const),
        pl.BlockSpec(wvt.shape, const),
        pl.BlockSpec((ts, LANES), lambda b, i: (i, 0)),
        pl.BlockSpec((ts, LANES), lambda b, i: (i, 0)),
        pl.BlockSpec((LANES, ts), lambda b, i: (0, i)),
        pl.BlockSpec((LANES, ts), lambda b, i: (0, i)),
    ]
    k_spec = pl.BlockSpec((1, N_HEADS, ts, LANES), lambda b, i: (b, 0, i, 0))
    k_shape = jax.ShapeDtypeStruct((bsz, N_HEADS, n, LANES), BF16)
    qt_spec = pl.BlockSpec((1, N_HEADS, LANES, ts), lambda b, i: (b, 0, 0, i))
    qt_shape = jax.ShapeDtypeStruct((bsz, N_HEADS, LANES, n), BF16)
    vt_spec = pl.BlockSpec((1, N_HEADS, VT_ROWS, ts), lambda b, i: (b, 0, 0, i))
    vt_shape = jax.ShapeDtypeStruct((bsz, N_HEADS, VT_ROWS, n), BF16)
    if is_ctx:
        out_specs = [k_spec, vt_spec]
        out_shape = [k_shape, vt_shape]
    else:
        pool_w = win.shape[1] - Q_LORA - 2 * LANES
        out_specs = [qt_spec, k_spec, vt_spec, pl.BlockSpec((1, ts, pool_w), lambda b, i: (b, i, 0))]
        out_shape = [qt_shape, k_shape, vt_shape, jax.ShapeDtypeStruct((bsz, n, pool_w), F32)]
    return pl.pallas_call(
        functools.partial(_front_kernel, is_ctx=is_ctx, scale=scale),
        grid=(bsz, nt),
        in_specs=in_specs,
        out_specs=out_specs,
        out_shape=out_shape,
        compiler_params=pltpu.CompilerParams(
            dimension_semantics=("arbitrary", "arbitrary"), vmem_limit_bytes=VMEM_LARGE),
        name="front_ctx" if is_ctx else "front",
    )(xs, mod, g1, win, qg, kvg, wuqt, wk, wvt, cs, sn, cs.T, sn.T)


def _attn_kernel(qt_ref, k_ref, vt_ref, kc_ref, vct_ref, o_ref, s_ref, mlc_ref, mxb_ref, oe_ref):
    n_ctx = kc_ref.shape[2]
    n_lat = k_ref.shape[2]
    chunks = [(None, 0, n_ctx)] + [(c * ATTN_KC, n_ctx + c * ATTN_KC, ATTN_KC)
                                   for c in range(n_lat // ATTN_KC)]
    sub = SUBLANES

    def q_cols(tile):
        if isinstance(tile, int):
            return pl.ds(tile * ATTN_TQ, ATTN_TQ)
        return pl.ds(pl.multiple_of(tile * ATTN_TQ, ATTN_TQ), ATTN_TQ)

    def score_chunk(tile, hd, buf, ci):
        off, soff, w = chunks[ci]
        keys = kc_ref[0, hd] if off is None else k_ref[0, hd, off:off + w, :]
        s = _dot(keys, qt_ref[0, hd, :, q_cols(tile)])
        s_ref[buf, soff:soff + w, :] = s
        mx = s[0:sub]
        for r in range(1, w // sub):
            mx = jnp.maximum(mx, s[r * sub:(r + 1) * sub])
        mlc_ref[buf, ci] = mx

    def row_max(buf):
        mx = mlc_ref[buf, 0]
        for ci in range(1, len(chunks)):
            mx = jnp.maximum(mx, mlc_ref[buf, ci])
        mxb_ref[...] = jnp.broadcast_to(jnp.max(mx, axis=0, keepdims=True), mx.shape)

    def weight_chunk(hd, buf, ci):
        off, soff, w = chunks[ci]
        p = jnp.exp2(s_ref[buf, soff:soff + w, :] - mxb_ref[0:1, :]).astype(BF16)
        vt = vct_ref[0, hd] if off is None else vt_ref[0, hd, :, off:off + w]
        return _dot(vt, p)

    def stage(hw, bw, scoring, bs):
        row_max(bw)
        acc = None
        for ci in range(len(chunks)):
            if scoring is not None:
                score_chunk(*scoring, bs, ci)
            part = weight_chunk(hw, bw, ci)
            acc = part if acc is None else acc + part
        return acc[0:V_DIM] / acc[V_DIM:V_DIM + 1]

    def write_pair(tile, j, ot_odd):
        pair_t = jnp.concatenate([oe_ref[...], ot_odd], axis=0)
        o_ref[0, j, q_cols(tile), :] = pair_t.T.astype(BF16)

    pairs_per_tile = N_HEADS // 2
    n_pairs = (qt_ref.shape[3] // ATTN_TQ) * pairs_per_tile

    def split(p):
        return p // pairs_per_tile, p % pairs_per_tile

    for ci in range(len(chunks)):
        score_chunk(0, 0, 0, ci)

    def head_pair(p, carry):
        tile, j = split(p)
        nxt_tile, nxt_j = split(p + 1)
        oe_ref[...] = stage(2 * j, 0, (tile, 2 * j + 1), 1)
        write_pair(tile, j, stage(2 * j + 1, 1, (nxt_tile, 2 * nxt_j), 0))
        return carry

    lax.fori_loop(0, n_pairs - 1, head_pair, 0)
    tile, j = split(n_pairs - 1)
    oe_ref[...] = stage(2 * j, 0, (tile, 2 * j + 1), 1)
    write_pair(tile, j, stage(2 * j + 1, 1, None, None))


def _attn_call(qt, k, vt, kc, vct):
    bsz, _, _, n = qt.shape
    n_ctx = kc.shape[2]
    tq = ATTN_TQ
    n_chunks = 1 + n // ATTN_KC
    per_batch = lambda b, i: (b, 0, 0, 0)
    resident = dict(pipeline_mode=pl.Buffered(1))
    rows_per_step = ATTN_TILES * tq
    return pl.pallas_call(
        _attn_kernel,
        grid=(bsz, n // rows_per_step),
        in_specs=[
            pl.BlockSpec((1, N_HEADS, LANES, rows_per_step), lambda b, i: (b, 0, 0, i)),
            pl.BlockSpec((1, N_HEADS, n, LANES), per_batch, **resident),
            pl.BlockSpec((1, N_HEADS, VT_ROWS, n), per_batch, **resident),
            pl.BlockSpec((1, N_HEADS, n_ctx, LANES), per_batch, **resident),
            pl.BlockSpec((1, N_HEADS, VT_ROWS, n_ctx), per_batch, **resident),
        ],
        out_specs=pl.BlockSpec((1, N_HEADS // 2, rows_per_step, LANES), lambda b, i: (b, 0, i, 0)),
        out_shape=jax.ShapeDtypeStruct((bsz, N_HEADS // 2, n, LANES), BF16),
        scratch_shapes=[
            pltpu.VMEM((2, n_ctx + n, tq), F32),
            pltpu.VMEM((2, n_chunks, SUBLANES, tq), F32),
            pltpu.VMEM((SUBLANES, tq), F32),
            pltpu.VMEM((V_DIM, tq), F32),
        ],
        compiler_params=pltpu.CompilerParams(
            dimension_semantics=("arbitrary", "arbitrary"), vmem_limit_bytes=VMEM_LARGE),
        name="attn",
    )(qt, k, vt, kc, vct)


def _mix_kernel(att_ref, u_ref, up_ref, un_ref, x_ref, mod_ref, wpool_ref, pscale_ref, wout_ref,
                g2_ref, rw_ref, rb_ref,
                x1_ref, h2_ref, gate_ref, ridx_ref, cnt_ref, ue_ref, base_ref, *, n_seq):
    b = pl.program_id(0)
    i = pl.program_id(1)
    n_tiles = pl.num_programs(1)
    ts = u_ref.shape[1]

    @pl.when(jnp.logical_and(b == 0, i == 0))
    def _():
        base_ref[...] = jnp.zeros_like(base_ref)

    u = u_ref[0]
    ue_ref[0:POOL_HALO] = jnp.where(i > 0, up_ref[0], 0.0)
    ue_ref[POOL_HALO:POOL_HALO + ts] = u
    ue_ref[POOL_HALO + ts:2 * POOL_HALO + ts] = jnp.where(i < n_tiles - 1, un_ref[0], 0.0)
    t = i * ts + lax.broadcasted_iota(jnp.int32, (ts, 1), 0)
    ys = []
    for g, w in enumerate(POOL_WINDOWS):
        half = w // 2
        lanes = slice(g * POOL_CH, (g + 1) * POOL_CH)
        ws = ue_ref[POOL_HALO - half:POOL_HALO - half + ts, lanes]
        for jj in range(-half + 1, half):
            ws = ws + ue_ref[POOL_HALO + jj:POOL_HALO + jj + ts, lanes]
        count = (jnp.minimum(t + half, n_seq) - jnp.maximum(t - half, 0)).astype(F32)
        mixed = (ws / count - u[:, lanes]).astype(BF16)
        ys.append(_dot(mixed, wpool_ref[g]))
    pool = jnp.concatenate(ys, axis=-1) * pscale_ref[...]

    cat = jnp.concatenate([att_ref[0, j] for j in range(N_HEADS // 2)] + [pool.astype(BF16)], axis=-1)
    m = mod_ref[0]
    x1 = x_ref[0] + m[2:3] * _dot(cat, wout_ref[...])
    x1_ref[0] = x1
    h2 = _rms(x1) * g2_ref[...] * (1.0 + m[4:5]) + m[3:4]
    h2_ref[0] = _pack_bf16_pairs(h2)

    h_hi = h2.astype(BF16)
    h_lo = (h2 - h_hi.astype(F32)).astype(BF16)
    hi_part = _dot(h_hi, rw_ref[...])
    logits = (hi_part[:, :LANES] + hi_part[:, LANES:] + _dot(h_lo, rw_ref[:, :LANES])) + rb_ref[...]
    lane = lax.broadcasted_iota(jnp.int32, logits.shape, 1).astype(F32)
    vals, idxs = [], []
    for _k in range(TOP_K):
        mv = jnp.max(logits, axis=-1, keepdims=True)
        ix = jnp.min(jnp.where(logits == mv, lane, float(LANES)), axis=-1, keepdims=True)
        vals.append(mv)
        idxs.append(ix)
        logits = jnp.where(lane == ix, -jnp.inf, logits)
    es = [jnp.exp(v - vals[0]) for v in vals]
    den = es[0] + es[1] + es[2] + es[3]

    onehot = jnp.zeros(lane.shape, F32)
    for ix in idxs:
        onehot = onehot + jnp.where(lane == ix, 1.0, 0.0)
    row = lax.broadcasted_iota(jnp.int32, (ts, ts), 0)
    col = lax.broadcasted_iota(jnp.int32, (ts, ts), 1)
    tri = jnp.where(col < row, 1.0, 0.0).astype(BF16)
    before = _dot(tri, onehot.astype(BF16)) + base_ref[0:1, :]
    base_new = base_ref[0:1, :] + jnp.sum(onehot, axis=0, keepdims=True)
    base_ref[...] = jnp.broadcast_to(base_new, base_ref.shape)
    cnt_ref[...] = jnp.broadcast_to(base_new, cnt_ref.shape)

    route = jnp.zeros(lane.shape, F32)
    for kk in range(TOP_K):
        rank = jnp.sum(jnp.where(lane == idxs[kk], before, 0.0), axis=-1, keepdims=True)
        route = jnp.where(lane == float(kk), idxs[kk], route)
        route = jnp.where(lane == float(TOP_K + kk), rank, route)
        route = jnp.where(lane == float(2 * TOP_K + kk), es[kk] / den, route)
    route_t = route.T
    ridx_ref[...] = route_t[0:2 * TOP_K].astype(jnp.int32)
    gate_ref[...] = route_t[2 * TOP_K:2 * TOP_K + SUBLANES]


def _mix_call(att, u, x, mod, wpool, pscale, wout, g2, rw, rb):
    bsz, n, d = x.shape
    ts = MIX_TS
    nt = n // ts
    pool_w = u.shape[2]
    hb = ts // POOL_HALO
    n_halo_blocks = n // POOL_HALO
    const2 = lambda b, i: (0, 0)
    tok = lambda b, i: (b, i, 0)
    return pl.pallas_call(
        functools.partial(_mix_kernel, n_seq=n),
        grid=(bsz, nt),
        in_specs=[
            pl.BlockSpec((1, N_HEADS // 2, ts, LANES), lambda b, i: (b, 0, i, 0)),
            pl.BlockSpec((1, ts, pool_w), tok),
            pl.BlockSpec((1, POOL_HALO, pool_w), lambda b, i: (b, jnp.maximum(i * hb - 1, 0), 0)),
            pl.BlockSpec((1, POOL_HALO, pool_w),
                         lambda b, i: (b, jnp.minimum((i + 1) * hb, n_halo_blocks - 1), 0)),
            pl.BlockSpec((1, ts, d), tok),
            pl.BlockSpec((1, N_MOD, d), lambda b, i: (b, 0, 0)),
            pl.BlockSpec(wpool.shape, lambda b, i: (0, 0, 0)),
            pl.BlockSpec((1, pool_w), const2),
            pl.BlockSpec(wout.shape, const2),
            pl.BlockSpec((1, d), const2),
            pl.BlockSpec(rw.shape, const2),
            pl.BlockSpec((1, LANES), const2),
        ],
        out_specs=[
            pl.BlockSpec((1, ts, d), tok),
            pl.BlockSpec((1, ts, d // 2), tok),
            pl.BlockSpec((SUBLANES, ts), lambda b, i: (0, b * nt + i)),
            pl.BlockSpec((2 * TOP_K, ts), lambda b, i: (0, b * nt + i)),
            pl.BlockSpec((SUBLANES, LANES), const2),
        ],
        out_shape=[
            jax.ShapeDtypeStruct((bsz, n, d), F32),
            jax.ShapeDtypeStruct((bsz, n, d // 2), jnp.uint32),
            jax.ShapeDtypeStruct((SUBLANES, bsz * n), F32),
            jax.ShapeDtypeStruct((2 * TOP_K, bsz * n), jnp.int32),
            jax.ShapeDtypeStruct((SUBLANES, LANES), F32),
        ],
        scratch_shapes=[
            pltpu.VMEM((ts + 2 * POOL_HALO, pool_w), F32),
            pltpu.VMEM((SUBLANES, LANES), F32),
        ],
        compiler_params=pltpu.CompilerParams(
            dimension_semantics=("arbitrary", "arbitrary"), vmem_limit_bytes=VMEM_SMALL),
        name="mix",
    )(att, u, u, u, x, mod, wpool, pscale, wout, g2, rw, rb)


def _sc_mesh():
    return plsc.VectorSubcoreMesh(core_axis_name="core", subcore_axis_name="subcore")


def _sc_worker_id():
    info = plsc.get_sparse_core_info()
    return lax.axis_index("subcore") * info.num_cores + lax.axis_index("core")


def _sc_num_workers():
    info = plsc.get_sparse_core_info()
    return info.num_cores * info.num_subcores


def _scatter_rows(x, dest3, n_out):
    t, d = x.shape
    top_k, n_chunks, rows = dest3.shape
    assert n_chunks % _sc_num_workers() == 0 and n_chunks * rows == t
    per_worker = n_chunks // _sc_num_workers()

    @functools.partial(
        pl.kernel,
        out_type=jax.ShapeDtypeStruct((n_out, d), x.dtype),
        mesh=_sc_mesh(),
        scratch_types=[pltpu.VMEM((top_k, rows), jnp.int32), pltpu.VMEM((rows, d), x.dtype)],
        name="sc_scatter",
    )
    def scatter(x_hbm, i_hbm, o_hbm, idx_v, rows_v):
        wid = _sc_worker_id()

        @pl.loop(0, per_worker)
        def _(c):
            chunk = wid * per_worker + c
            for kk in range(top_k):
                pltpu.sync_copy(i_hbm.at[kk, chunk], idx_v.at[kk])
            pltpu.sync_copy(x_hbm.at[pl.ds(pl.multiple_of(chunk * rows, SUBLANES), rows)], rows_v)
            for kk in range(top_k):
                pltpu.sync_copy(rows_v, o_hbm.at[idx_v.at[kk]])

    return scatter(x, dest3)


def _gather_rows(y, idx):
    n = idx.shape[0]
    d = y.shape[1]
    rows = SC_ROWS
    assert n % (rows * _sc_num_workers()) == 0
    per_worker = n // _sc_num_workers()
    n_chunks = per_worker // rows

    @functools.partial(
        pl.kernel,
        out_type=jax.ShapeDtypeStruct((n, d), y.dtype),
        mesh=_sc_mesh(),
        scratch_types=[pltpu.VMEM((rows,), jnp.int32), pltpu.VMEM((rows, d), y.dtype)],
        name="sc_gather",
    )
    def gather(y_hbm, i_hbm, o_hbm, idx_v, rows_v):
        base = _sc_worker_id() * per_worker

        @pl.loop(0, n_chunks)
        def _(c):
            off = pl.multiple_of(base + c * rows, SUBLANES)
            pltpu.sync_copy(i_hbm.at[pl.ds(off, rows)], idx_v)
            pltpu.sync_copy(y_hbm.at[idx_v], rows_v)
            pltpu.sync_copy(rows_v, o_hbm.at[pl.ds(off, rows)])

    return gather(y, idx)


def _moe_kernel(blk_ref, exp_ref, lo_ref, hi_ref, new_ref, init_ref, next_ref, slot_ref, xs_ref,
                wgu_hbm, bgu_ref, wd_hbm, bd_ref, ys_ref, wgu_buf, wd_buf, sems):
    i = pl.program_id(0)
    slot = slot_ref[i]

    def weight_copies(e, s):
        return (pltpu.make_async_copy(wgu_hbm.at[e], wgu_buf.at[s], sems.at[s, 0]),
                pltpu.make_async_copy(wd_hbm.at[e], wd_buf.at[s], sems.at[s, 1]))

    @pl.when(i == 0)
    def _():
        for cp in weight_copies(exp_ref[0], slot):
            cp.start()

    @pl.when(new_ref[i] == 1)
    def _():
        for cp in weight_copies(exp_ref[i], slot):
            cp.wait()

        @pl.when(next_ref[i] >= 0)
        def _():
            for cp in weight_copies(next_ref[i], 1 - slot):
                cp.start()

    @pl.when(init_ref[i] == 1)
    def _():
        ys_ref[...] = jnp.zeros_like(ys_ref)

    lo = lo_ref[i]
    hi = hi_ref[i]

    def expert_rows(r0, n_rows):
        de = wd_buf.shape[1]
        x = _unpack_bf16_pairs(xs_ref[pl.ds(r0, n_rows), :])
        gu = _dot(x, wgu_buf[slot]) + bgu_ref[0]
        g = jnp.minimum(gu[:, :de], SWIGLU_LIMIT)
        lin = jnp.clip(gu[:, de:], -SWIGLU_LIMIT, SWIGLU_LIMIT)
        act = g / (1.0 + jnp.exp(-SWIGLU_ALPHA * g)) * (lin + 1.0)
        y = _dot(act, wd_buf[slot]) + bd_ref[0]
        row = r0 + lax.broadcasted_iota(jnp.int32, (n_rows, 1), 0)
        mine = jnp.logical_and(row >= lo, row < hi)
        ys_ref[pl.ds(r0, n_rows), :] = jnp.where(mine, _pack_bf16_pairs(y), ys_ref[pl.ds(r0, n_rows), :])

    first = lo // MOE_SUB
    pieces = jnp.where(hi > lo, (hi + MOE_SUB - 1) // MOE_SUB - first, 0)
    for cnt in range(1, MOE_BLOCK // MOE_SUB + 1):
        @pl.when(pieces == cnt)
        def _(cnt=cnt):
            r0 = 0 if cnt * MOE_SUB == MOE_BLOCK else pl.multiple_of(first * MOE_SUB, MOE_SUB)
            expert_rows(r0, cnt * MOE_SUB)


def _moe_call(sched, xs, wgu, bgu, wd, bd):
    n_items = sched[0].shape[0]
    _, d, de2 = wgu.shape
    de = de2 // 2
    wmap = lambda i, blk, exp, lo, hi, new, init, nxt, slot: (exp[i], 0, 0)
    xmap = lambda i, blk, exp, lo, hi, new, init, nxt, slot: (blk[i], 0)
    return pl.pallas_call(
        _moe_kernel,
        grid_spec=pltpu.PrefetchScalarGridSpec(
            num_scalar_prefetch=8,
            grid=(n_items,),
            in_specs=[
                pl.BlockSpec((MOE_BLOCK, d // 2), xmap),
                pl.BlockSpec(memory_space=pl.ANY),
                pl.BlockSpec((1, 1, de2), wmap),
                pl.BlockSpec(memory_space=pl.ANY),
                pl.BlockSpec((1, 1, d), wmap),
            ],
            out_specs=pl.BlockSpec((MOE_BLOCK, d // 2), xmap),
            scratch_shapes=[
                pltpu.VMEM((2, d, de2), F32), pltpu.VMEM((2, de, d), F32),
                pltpu.SemaphoreType.DMA((2, 2)),
            ],
        ),
        out_shape=jax.ShapeDtypeStruct(xs.shape, xs.dtype),
        compiler_params=pltpu.CompilerParams(
            dimension_semantics=("arbitrary",), vmem_limit_bytes=VMEM_LARGE),
        name="moe",
    )(*sched, xs, wgu, bgu, wd, bd)


def _moe_schedule(counts, n_rows):
    n_blocks = n_rows // MOE_BLOCK
    n_items = n_blocks + N_EXPERTS
    ends = jnp.cumsum(counts)
    starts = ends - counts
    first_blk = starts // MOE_BLOCK
    last_blk = (ends - 1) // MOE_BLOCK
    items_per = jnp.where(counts > 0, last_blk - first_blk + 1, 0)
    item_ends = jnp.cumsum(items_per)
    item_starts = item_ends - items_per
    total = item_ends[-1]
    it = jnp.arange(n_items, dtype=jnp.int32)
    live = it < total
    itc = jnp.minimum(it, total - 1)
    exp = jnp.sum((item_ends[None, :] <= itc[:, None]).astype(jnp.int32), axis=1)
    is_exp = exp[:, None] == jnp.arange(N_EXPERTS, dtype=jnp.int32)[None, :]
    pick = lambda table: jnp.sum(jnp.where(is_exp, table[None, :], 0), axis=1)
    blk = pick(first_blk) + itc - pick(item_starts)
    lo = jnp.clip(pick(starts) - blk * MOE_BLOCK, 0, MOE_BLOCK)
    hi = jnp.clip(pick(ends) - blk * MOE_BLOCK, 0, MOE_BLOCK)
    hi = jnp.where(live, hi, lo)
    prev_exp = jnp.concatenate([jnp.full((1,), -1, jnp.int32), exp[:-1]])
    prev_blk = jnp.concatenate([jnp.full((1,), -1, jnp.int32), blk[:-1]])
    new = jnp.logical_and(live, exp != prev_exp)
    init = jnp.logical_and(live, blk != prev_blk)
    slot = (jnp.cumsum(new.astype(jnp.int32)) - 1) % 2
    ar = jnp.arange(N_EXPERTS, dtype=jnp.int32)
    later = jnp.logical_and(counts[None, :] > 0, ar[None, :] > ar[:, None])
    next_exp = jnp.min(jnp.where(later, ar[None, :], N_EXPERTS), axis=1)
    next_exp = jnp.where(next_exp == N_EXPERTS, -1, next_exp)
    nxt = pick(next_exp)
    as_i32 = lambda a: a.astype(jnp.int32)
    return tuple(as_i32(a) for a in (blk, exp, lo, hi, new, init, nxt, slot)), starts


def _combine_kernel(*refs):
    yg_refs = refs[:TOP_K]
    x1_ref, gate_ref, mod_ref, fg_ref, o_ref = refs[TOP_K:]
    gt = gate_ref[...]
    gates = jnp.concatenate([gt, jnp.zeros((LANES - gt.shape[0], gt.shape[1]), F32)], axis=0).T
    y = gates[:, 0:1] * _unpack_bf16_pairs(yg_refs[0][...])
    for kk in range(1, TOP_K):
        y = y + gates[:, kk:kk + 1] * _unpack_bf16_pairs(yg_refs[kk][...])
    m = mod_ref[0]
    x2 = x1_ref[...] + m[5:6] * y
    o_ref[...] = _rms(x2) * fg_ref[...]


def _combine_call(yg, x1, gates, mod, fg, n_seq, part):
    t, d = x1.shape
    tt = COMBINE_TT
    tiles_per_seq = n_seq // tt
    n_tiles = yg.shape[0] // (TOP_K * tt)
    first = part * n_tiles
    tok = lambda i: (first + i, 0)
    slot_specs = [pl.BlockSpec((tt, d // 2), functools.partial(lambda kk, i: (kk * n_tiles + i, 0), kk))
                  for kk in range(TOP_K)]
    return pl.pallas_call(
        _combine_kernel,
        grid=(n_tiles,),
        in_specs=slot_specs + [
            pl.BlockSpec((tt, d), tok),
            pl.BlockSpec((SUBLANES, tt), lambda i: (0, first + i)),
            pl.BlockSpec((1, N_MOD, d), lambda i: ((first + i) // tiles_per_seq, 0, 0)),
            pl.BlockSpec((1, d), lambda i: (0, 0)),
        ],
        out_specs=pl.BlockSpec((tt, d), tok),
        out_shape=jax.ShapeDtypeStruct((t, d), F32),
        input_output_aliases={TOP_K: 0},
        compiler_params=pltpu.CompilerParams(
            dimension_semantics=("arbitrary",), vmem_limit_bytes=VMEM_SMALL),
        name="combine",
    )(*([yg] * TOP_K), x1, gates, mod, fg)


def _pad_cols(a, width):
    return jnp.pad(a, ((0, 0), (0, width - a.shape[1])))


def _rope_slab(w_rope):
    return jnp.pad(w_rope, ((0, 0), (QK_NOPE, LANES - QK_NOPE - QK_ROPE)))


def _prep_w_in(w_in):
    kr0 = Q_LORA + KV_LORA
    w_kr = w_in[:, kr0:kr0 + QK_ROPE]
    return jnp.concatenate(
        [w_in[:, :kr0], _rope_slab(w_kr), w_in[:, kr0 + QK_ROPE:]], axis=1).astype(BF16)


def _prep_w_uq(w_uq):
    per = QK_NOPE + QK_ROPE
    rows = w_uq.shape[0]
    w = w_uq.reshape(rows, N_HEADS, per)
    slab = jnp.pad(w, ((0, 0), (0, 0), (0, LANES - per)))
    return slab.reshape(rows, -1).T.astype(BF16)


def _prep_w_ukv(w_ukv):
    per = QK_NOPE + V_DIM
    rows = w_ukv.shape[0]
    w = w_ukv.reshape(rows, N_HEADS, per)
    wk = jnp.pad(w[:, :, :QK_NOPE], ((0, 0), (0, 0), (0, LANES - QK_NOPE))).reshape(rows, -1)
    wvt = jnp.pad(jnp.transpose(w[:, :, QK_NOPE:], (1, 2, 0)), ((0, 0), (0, VT_ROWS - V_DIM), (0, 0)))
    return wk.astype(BF16), wvt.reshape(N_HEADS * VT_ROWS, rows).astype(BF16)


def _rope_tables(n_lat):
    rows = n_lat // GRID_W
    nf = QK_ROPE // 4
    row = jnp.repeat(jnp.arange(rows, dtype=F32), GRID_W)
    col = jnp.tile(jnp.arange(GRID_W, dtype=F32), rows)
    freqs = ROPE_BASE ** (-jnp.arange(nf, dtype=F32) / nf)
    ang = jnp.concatenate([row[:, None] * freqs, col[:, None] * freqs], axis=-1)
    cos, sin = jnp.cos(ang), jnp.sin(ang)
    ones = jnp.ones((n_lat, QK_NOPE), F32)
    zeros = jnp.zeros((n_lat, QK_NOPE), F32)
    cs = _pad_cols(jnp.concatenate([ones, cos, cos], axis=1), LANES)
    sn = _pad_cols(jnp.concatenate([zeros, -sin, sin], axis=1), LANES)
    return cs, sn


def kernel(x, c, ctx, c_ctx, w_mod, b_mod, norm1_g, w_in, q_norm_g, kv_norm_g, w_uq, w_ukv, w_pool,
           pool_scale, w_out, norm2_g, router_w, router_b, w_gate_up, b_gate_up, w_down, b_down,
           final_g):
    bsz, n, d = x.shape
    n_ctx = ctx.shape[1]
    t = bsz * n
    assert w_mod.shape[0] == 1, "single-layer block"
    l = 0
    tp = t // COMBINE_PARTS
    assert bsz + 1 <= SUBLANES and d % (2 * LANES) == 0 and n_ctx % ATTN_KC == 0
    assert POOL_HALO == SUBLANES and MIX_TS % POOL_HALO == 0
    assert n % GRID_W == 0 and n % FRONT_TS == 0 and n % MIX_TS == 0 and n % ATTN_KC == 0
    assert n % (ATTN_TQ * ATTN_TILES) == 0 and (t * TOP_K) % MOE_BLOCK == 0 and MOE_BLOCK % MOE_SUB == 0
    assert t % COMBINE_PARTS == 0 and tp % COMBINE_TT == 0 and n % COMBINE_TT == 0 and t % SC_ROWS == 0

    cc = jnp.concatenate([c, c_ctx[None, :], jnp.zeros((SUBLANES - bsz - 1, d), F32)], axis=0)
    mod = _mod_call(cc, w_mod[l], b_mod[l][None, :]).reshape(SUBLANES, N_MOD, d)
    mod_lat, mod_ctx = mod[:bsz], mod[bsz:bsz + 1]

    win = _prep_w_in(w_in[l])
    wuqt = _prep_w_uq(w_uq[l])
    wk, wvt = _prep_w_ukv(w_ukv[l])
    cs, sn = _rope_tables(n)
    cs_ctx = jnp.broadcast_to((jnp.arange(LANES) < QK_NOPE + QK_ROPE).astype(F32), (n_ctx, LANES))
    sn_ctx = jnp.zeros((n_ctx, LANES), F32)
    g1 = norm1_g[l][None, :]
    qg = q_norm_g[l][None, :]
    kvg = kv_norm_g[l][None, :]

    qt, k, vt, u = _front_call(x, mod_lat, True, g1, win, qg, kvg, wuqt, wk, wvt, cs, sn,
                               is_ctx=False, ts=FRONT_TS)
    kc, vct = _front_call(ctx, mod_ctx, False, g1, win[:, Q_LORA:Q_LORA + 2 * LANES], qg, kvg, wuqt,
                          wk, wvt, cs_ctx, sn_ctx, is_ctx=True, ts=n_ctx)
    att = _attn_call(qt, k, vt, kc, vct)

    rw = _pad_cols(router_w[l], LANES)
    rw_hi = rw.astype(BF16)
    rw = jnp.concatenate([rw_hi, (rw - rw_hi.astype(F32)).astype(BF16)], axis=1)
    rb = jnp.concatenate([router_b[l], jnp.full((LANES - N_EXPERTS,), -jnp.inf, F32)])[None, :]
    x1, h2, gates, ridx, cnt = _mix_call(
        att, u, x, mod_lat, w_pool[l].astype(BF16), pool_scale[l][None, :], w_out[l].astype(BF16),
        norm2_g[l][None, :], rw, rb)

    counts = cnt[0, :N_EXPERTS].astype(jnp.int32)
    sched, starts = _moe_schedule(counts, t * TOP_K)
    is_exp = ridx[:TOP_K, None, :] == jnp.arange(N_EXPERTS, dtype=jnp.int32)[None, :, None]
    dest_t = jnp.sum(jnp.where(is_exp, starts[None, :, None], 0), axis=1) + ridx[TOP_K:]

    xs = _scatter_rows(h2.reshape(t, d // 2), dest_t.reshape(TOP_K, t // SC_ROWS, SC_ROWS), t * TOP_K)
    ys = _moe_call(sched, xs, w_gate_up[l], b_gate_up[l][:, None, :], w_down[l],
                   b_down[l][:, None, :])
    out = x1.reshape(t, d)
    for part in range(COMBINE_PARTS):
        idx = dest_t[:, part * tp:(part + 1) * tp].reshape(-1)
        yg = _gather_rows(ys, idx)
        out = _combine_call(yg, out, gates, mod_lat, final_g[None, :], n, part)
    return out.reshape(bsz, n, d)
```

```python
import functools
import math

import jax
import jax.numpy as jnp
from jax import lax
from jax.experimental import pallas as pl
from jax.experimental.pallas import tpu as pltpu
from jax.experimental.pallas import tpu_sc as plsc

F32 = jnp.float32
BF16 = jnp.bfloat16
HIGHEST = lax.Precision.HIGHEST

N_HEADS = 8
QK_NOPE = 64
QK_ROPE = 32
V_DIM = 64
Q_LORA = 256
KV_LORA = 128
GRID_W = 64
ROPE_BASE = 10000.0
POOL_WINDOWS = (2, 4, 8, 16)
POOL_CH = 128
N_EXPERTS = 32
TOP_K = 4
SWIGLU_LIMIT = 7.0
SWIGLU_ALPHA = 1.702
N_MOD = 6
EPS = 1e-6

LANES = 128
SUBLANES = 8
BF16_SUBLANES = 16
MIB = 1024 * 1024

POOL_HALO = max(POOL_WINDOWS) // 2
VT_ROWS = -(-(V_DIM + 1) // BF16_SUBLANES) * BF16_SUBLANES

FRONT_TS = 1024
ATTN_TQ = 512
ATTN_TILES = 4
ATTN_KC = 256
MIX_TS = 512
MOE_BLOCK = 512
MOE_SUB = 64
COMBINE_TT = 512
COMBINE_PARTS = 8
SC_ROWS = 128

VMEM_SMALL = 32 * MIB
VMEM_LARGE = 56 * MIB


def _dot(a, b, **kw):
    return jnp.dot(a, b, preferred_element_type=F32, **kw)


def _rms(x):
    return x * lax.rsqrt(jnp.mean(x * x, axis=-1, keepdims=True) + EPS)


def _pack_bf16_pairs(x):
    w = x.shape[1] // 2
    return pltpu.pack_elementwise([x[:, :w], x[:, w:]], packed_dtype=BF16)


def _unpack_bf16_pairs(words):
    halves = [pltpu.unpack_elementwise(words, index=h, packed_dtype=BF16, unpacked_dtype=F32)
              for h in range(2)]
    return jnp.concatenate(halves, axis=-1)


def _mod_kernel(c_ref, w_ref, b_ref, o_ref):
    c = c_ref[...]
    a = c / (1.0 + jnp.exp(-c))
    o_ref[...] = _dot(a, w_ref[...], precision=HIGHEST) + b_ref[...]


def _mod_call(cc, w_mod, b_mod):
    d = w_mod.shape[0]
    return pl.pallas_call(
        _mod_kernel,
        grid=(N_MOD,),
        in_specs=[
            pl.BlockSpec((SUBLANES, d), lambda j: (0, 0)),
            pl.BlockSpec((d, d), lambda j: (0, j)),
            pl.BlockSpec((1, d), lambda j: (0, j)),
        ],
        out_specs=pl.BlockSpec((SUBLANES, d), lambda j: (0, j)),
        out_shape=jax.ShapeDtypeStruct((SUBLANES, N_MOD * d), F32),
        compiler_params=pltpu.CompilerParams(
            dimension_semantics=("arbitrary",), vmem_limit_bytes=VMEM_SMALL),
        name="mod",
    )(cc, w_mod, b_mod)


def _front_kernel(x_ref, mod_ref, g1_ref, win_ref, qg_ref, kvg_ref, wuqt_ref, wk_ref, wvt_ref,
                  cs_ref, sn_ref, cst_ref, snt_ref, *out_refs, is_ctx, scale):
    x = x_ref[0]
    m = mod_ref[0]
    h = _rms(x) * g1_ref[...] * (1.0 + m[1:2]) + m[0:1]
    z = _dot(h.astype(BF16), win_ref[...])
    if is_ctx:
        k_ref, vt_ref = out_refs
        zkv = z
    else:
        qt_ref, k_ref, vt_ref, u_ref = out_refs
        zkv = z[:, Q_LORA:Q_LORA + 2 * LANES]
        u_ref[0] = z[:, Q_LORA + 2 * LANES:]

    ckv = _rms(zkv[:, :KV_LORA]) * kvg_ref[...]
    kk = _dot(ckv.astype(BF16), wk_ref[...])
    kra = zkv[:, LANES:2 * LANES]
    lane = lax.broadcasted_iota(jnp.int32, kra.shape, 1)
    half = QK_ROPE // 2
    partner = jnp.where(lane < QK_NOPE + half, pltpu.roll(kra, LANES - half, axis=1),
                        pltpu.roll(kra, half, axis=1))
    kr = kra * cs_ref[...] + partner * sn_ref[...]
    for hd in range(N_HEADS):
        k_ref[0, hd] = (kk[:, hd * LANES:(hd + 1) * LANES] + kr).astype(BF16)
    vt = _dot(wvt_ref[...], ckv.T.astype(BF16))
    row = lax.broadcasted_iota(jnp.int32, (VT_ROWS, 1), 0)
    ones = jnp.where(row == V_DIM, 1.0, 0.0)
    for hd in range(N_HEADS):
        vt_ref[0, hd] = (vt[hd * VT_ROWS:(hd + 1) * VT_ROWS] + ones).astype(BF16)

    if not is_ctx:
        cq = _rms(z[:, :Q_LORA]) * qg_ref[...]
        qqt = _dot(wuqt_ref[...], cq.T.astype(BF16))
        cst = cst_ref[...]
        snt = snt_ref[...]
        r1, r2, r3 = QK_NOPE, QK_NOPE + QK_ROPE // 2, QK_NOPE + QK_ROPE
        for hd in range(N_HEADS):
            qa = qqt[hd * LANES:(hd + 1) * LANES]
            qb = jnp.concatenate([qa[:r1], qa[r2:r3], qa[r1:r2], qa[r3:]], axis=0)
            qt_ref[0, hd] = ((qa * cst + qb * snt) * scale).astype(BF16)


def _front_call(xs, mod, mod_per_batch, g1, win, qg, kvg, wuqt, wk, wvt, cs, sn, *, is_ctx, ts):
    bsz, n, d = xs.shape
    nt = n // ts
    scale = math.log2(math.e) / math.sqrt(QK_NOPE + QK_ROPE)
    const = lambda b, i: (0, 0)
    mod_map = (lambda b, i: (b, 0, 0)) if mod_per_batch else (lambda b, i: (0, 0, 0))
    in_specs = [
        pl.BlockSpec((1, ts, d), lambda b, i: (b, i, 0)),
        pl.BlockSpec((1, N_MOD, d), mod_map),
        pl.BlockSpec((1, d), const),
        pl.BlockSpec(win.shape, const),
        pl.BlockSpec((1, Q_LORA), const),
        pl.BlockSpec((1, KV_LORA), const),
        pl.BlockSpec(wuqt.shape, const),
        pl.BlockSpec(wk.shape, const),
        pl.BlockSpec(wvt.shape, const),
        pl.BlockSpec((ts, LANES), lambda b, i: (i, 0)),
        pl.BlockSpec((ts, LANES), lambda b, i: (i, 0)),
        pl.BlockSpec((LANES, ts), lambda b, i: (0, i)),
        pl.BlockSpec((LANES, ts), lambda b, i: (0, i)),
    ]
    k_spec = pl.BlockSpec((1, N_HEADS, ts, LANES), lambda b, i: (b, 0, i, 0))
    k_shape = jax.ShapeDtypeStruct((bsz, N_HEADS, n, LANES), BF16)
    qt_spec = pl.BlockSpec((1, N_HEADS, LANES, ts), lambda b, i: (b, 0, 0, i))
    qt_shape = jax.ShapeDtypeStruct((bsz, N_HEADS, LANES, n), BF16)
    vt_spec = pl.BlockSpec((1, N_HEADS, VT_ROWS, ts), lambda b, i: (b, 0, 0, i))
    vt_shape = jax.ShapeDtypeStruct((bsz, N_HEADS, VT_ROWS, n), BF16)
    if is_ctx:
        out_specs = [k_spec, vt_spec]
        out_shape = [k_shape, vt_shape]
    else:
        pool_w = win.shape[1] - Q_LORA - 2 * LANES
        out_specs = [qt_spec, k_spec, vt_spec, pl.BlockSpec((1, ts, pool_w), lambda b, i: (b, i, 0))]
        out_shape = [qt_shape, k_shape, vt_shape, jax.ShapeDtypeStruct((bsz, n, pool_w), F32)]
    return pl.pallas_call(
        functools.partial(_front_kernel, is_ctx=is_ctx, scale=scale),
        grid=(bsz, nt),
        in_specs=in_specs,
        out_specs=out_specs,
        out_shape=out_shape,
        compiler_params=pltpu.CompilerParams(
            dimension_semantics=("arbitrary", "arbitrary"), vmem_limit_bytes=VMEM_LARGE),
        name="front_ctx" if is_ctx else "front",
    )(xs, mod, g1, win, qg, kvg, wuqt, wk, wvt, cs, sn, cs.T, sn.T)


def _attn_kernel(qt_ref, k_ref, vt_ref, kc_ref, vct_ref, o_ref, s_ref, mlc_ref, mxb_ref, oe_ref):
    n_ctx = kc_ref.shape[2]
    n_lat = k_ref.shape[2]
    chunks = [(None, 0, n_ctx)] + [(c * ATTN_KC, n_ctx + c * ATTN_KC, ATTN_KC)
                                   for c in range(n_lat // ATTN_KC)]
    sub = SUBLANES

    def q_cols(tile):
        if isinstance(tile, int):
            return pl.ds(tile * ATTN_TQ, ATTN_TQ)
        return pl.ds(pl.multiple_of(tile * ATTN_TQ, ATTN_TQ), ATTN_TQ)

    def score_chunk(tile, hd, buf, ci):
        off, soff, w = chunks[ci]
        keys = kc_ref[0, hd] if off is None else k_ref[0, hd, off:off + w, :]
        s = _dot(keys, qt_ref[0, hd, :, q_cols(tile)])
        s_ref[buf, soff:soff + w, :] = s
        mx = s[0:sub]
        for r in range(1, w // sub):
            mx = jnp.maximum(mx, s[r * sub:(r + 1) * sub])
        mlc_ref[buf, ci] = mx

    def row_max(buf):
        mx = mlc_ref[buf, 0]
        for ci in range(1, len(chunks)):
            mx = jnp.maximum(mx, mlc_ref[buf, ci])
        mxb_ref[...] = jnp.broadcast_to(jnp.max(mx, axis=0, keepdims=True), mx.shape)

    def weight_chunk(hd, buf, ci):
        off, soff, w = chunks[ci]
        p = jnp.exp2(s_ref[buf, soff:soff + w, :] - mxb_ref[0:1, :]).astype(BF16)
        vt = vct_ref[0, hd] if off is None else vt_ref[0, hd, :, off:off + w]
        return _dot(vt, p)

    def stage(hw, bw, scoring, bs):
        row_max(bw)
        acc = None
        for ci in range(len(chunks)):
            if scoring is not None:
                score_chunk(*scoring, bs, ci)
            part = weight_chunk(hw, bw, ci)
            acc = part if acc is None else acc + part
        return acc[0:V_DIM] / acc[V_DIM:V_DIM + 1]

    def write_pair(tile, j, ot_odd):
        pair_t = jnp.concatenate([oe_ref[...], ot_odd], axis=0)
        o_ref[0, j, q_cols(tile), :] = pair_t.T.astype(BF16)

    pairs_per_tile = N_HEADS // 2
    n_pairs = (qt_ref.shape[3] // ATTN_TQ) * pairs_per_tile

    def split(p):
        return p // pairs_per_tile, p % pairs_per_tile

    for ci in range(len(chunks)):
        score_chunk(0, 0, 0, ci)

    def head_pair(p, carry):
        tile, j = split(p)
        nxt_tile, nxt_j = split(p + 1)
        oe_ref[...] = stage(2 * j, 0, (tile, 2 * j + 1), 1)
        write_pair(tile, j, stage(2 * j + 1, 1, (nxt_tile, 2 * nxt_j), 0))
        return carry

    lax.fori_loop(0, n_pairs - 1, head_pair, 0)
    tile, j = split(n_pairs - 1)
    oe_ref[...] = stage(2 * j, 0, (tile, 2 * j + 1), 1)
    write_pair(tile, j, stage(2 * j + 1, 1, None, None))


def _attn_call(qt, k, vt, kc, vct):
    bsz, _, _, n = qt.shape
    n_ctx = kc.shape[2]
    tq = ATTN_TQ
    n_chunks = 1 + n // ATTN_KC
    per_batch = lambda b, i: (b, 0, 0, 0)
    resident = dict(pipeline_mode=pl.Buffered(1))
    rows_per_step = ATTN_TILES * tq
    return pl.pallas_call(
        _attn_kernel,
        grid=(bsz, n // rows_per_step),
        in_specs=[
            pl.BlockSpec((1, N_HEADS, LANES, rows_per_step), lambda b, i: (b, 0, 0, i)),
            pl.BlockSpec((1, N_HEADS, n, LANES), per_batch, **resident),
            pl.BlockSpec((1, N_HEADS, VT_ROWS, n), per_batch, **resident),
            pl.BlockSpec((1, N_HEADS, n_ctx, LANES), per_batch, **resident),
            pl.BlockSpec((1, N_HEADS, VT_ROWS, n_ctx), per_batch, **resident),
        ],
        out_specs=pl.BlockSpec((1, N_HEADS // 2, rows_per_step, LANES), lambda b, i: (b, 0, i, 0)),
        out_shape=jax.ShapeDtypeStruct((bsz, N_HEADS // 2, n, LANES), BF16),
        scratch_shapes=[
            pltpu.VMEM((2, n_ctx + n, tq), F32),
            pltpu.VMEM((2, n_chunks, SUBLANES, tq), F32),
            pltpu.VMEM((SUBLANES, tq), F32),
            pltpu.VMEM((V_DIM, tq), F32),
        ],
        compiler_params=pltpu.CompilerParams(
            dimension_semantics=("arbitrary", "arbitrary"), vmem_limit_bytes=VMEM_LARGE),
        name="attn",
    )(qt, k, vt, kc, vct)


def _mix_kernel(att_ref, u_ref, up_ref, un_ref, x_ref, mod_ref, wpool_ref, pscale_ref, wout_ref,
                g2_ref, rw_ref, rb_ref,
                x1_ref, h2_ref, gate_ref, ridx_ref, cnt_ref, ue_ref, base_ref, *, n_seq):
    b = pl.program_id(0)
    i = pl.program_id(1)
    n_tiles = pl.num_programs(1)
    ts = u_ref.shape[1]

    @pl.when(jnp.logical_and(b == 0, i == 0))
    def _():
        base_ref[...] = jnp.zeros_like(base_ref)

    u = u_ref[0]
    ue_ref[0:POOL_HALO] = jnp.where(i > 0, up_ref[0], 0.0)
    ue_ref[POOL_HALO:POOL_HALO + ts] = u
    ue_ref[POOL_HALO + ts:2 * POOL_HALO + ts] = jnp.where(i < n_tiles - 1, un_ref[0], 0.0)
    t = i * ts + lax.broadcasted_iota(jnp.int32, (ts, 1), 0)
    ys = []
    for g, w in enumerate(POOL_WINDOWS):
        half = w // 2
        lanes = slice(g * POOL_CH, (g + 1) * POOL_CH)
        ws = ue_ref[POOL_HALO - half:POOL_HALO - half + ts, lanes]
        for jj in range(-half + 1, half):
            ws = ws + ue_ref[POOL_HALO + jj:POOL_HALO + jj + ts, lanes]
        count = (jnp.minimum(t + half, n_seq) - jnp.maximum(t - half, 0)).astype(F32)
        mixed = (ws / count - u[:, lanes]).astype(BF16)
        ys.append(_dot(mixed, wpool_ref[g]))
    pool = jnp.concatenate(ys, axis=-1) * pscale_ref[...]

    cat = jnp.concatenate([att_ref[0, j] for j in range(N_HEADS // 2)] + [pool.astype(BF16)], axis=-1)
    m = mod_ref[0]
    x1 = x_ref[0] + m[2:3] * _dot(cat, wout_ref[...])
    x1_ref[0] = x1
    h2 = _rms(x1) * g2_ref[...] * (1.0 + m[4:5]) + m[3:4]
    h2_ref[0] = _pack_bf16_pairs(h2)

    h_hi = h2.astype(BF16)
    h_lo = (h2 - h_hi.astype(F32)).astype(BF16)
    hi_part = _dot(h_hi, rw_ref[...])
    logits = (hi_part[:, :LANES] + hi_part[:, LANES:] + _dot(h_lo, rw_ref[:, :LANES])) + rb_ref[...]
    lane = lax.broadcasted_iota(jnp.int32, logits.shape, 1).astype(F32)
    vals, idxs = [], []
    for _k in range(TOP_K):
        mv = jnp.max(logits, axis=-1, keepdims=True)
        ix = jnp.min(jnp.where(logits == mv, lane, float(LANES)), axis=-1, keepdims=True)
        vals.append(mv)
        idxs.append(ix)
        logits = jnp.where(lane == ix, -jnp.inf, logits)
    es = [jnp.exp(v - vals[0]) for v in vals]
    den = es[0] + es[1] + es[2] + es[3]

    onehot = jnp.zeros(lane.shape, F32)
    for ix in idxs:
        onehot = onehot + jnp.where(lane == ix, 1.0, 0.0)
    row = lax.broadcasted_iota(jnp.int32, (ts, ts), 0)
    col = lax.broadcasted_iota(jnp.int32, (ts, ts), 1)
    tri = jnp.where(col < row, 1.0, 0.0).astype(BF16)
    before = _dot(tri, onehot.astype(BF16)) + base_ref[0:1, :]
    base_new = base_ref[0:1, :] + jnp.sum(onehot, axis=0, keepdims=True)
    base_ref[...] = jnp.broadcast_to(base_new, base_ref.shape)
    cnt_ref[...] = jnp.broadcast_to(base_new, cnt_ref.shape)

    route = jnp.zeros(lane.shape, F32)
    for kk in range(TOP_K):
        rank = jnp.sum(jnp.where(lane == idxs[kk], before, 0.0), axis=-1, keepdims=True)
        route = jnp.where(lane == float(kk), idxs[kk], route)
        route = jnp.where(lane == float(TOP_K + kk), rank, route)
        route = jnp.where(lane == float(2 * TOP_K + kk), es[kk] / den, route)
    route_t = route.T
    ridx_ref[...] = route_t[0:2 * TOP_K].astype(jnp.int32)
    gate_ref[...] = route_t[2 * TOP_K:2 * TOP_K + SUBLANES]


def _mix_call(att, u, x, mod, wpool, pscale, wout, g2, rw, rb):
    bsz, n, d = x.shape
    ts = MIX_TS
    nt = n // ts
    pool_w = u.shape[2]
    hb = ts // POOL_HALO
    n_halo_blocks = n // POOL_HALO
    const2 = lambda b, i: (0, 0)
    tok = lambda b, i: (b, i, 0)
    return pl.pallas_call(
        functools.partial(_mix_kernel, n_seq=n),
        grid=(bsz, nt),
        in_specs=[
            pl.BlockSpec((1, N_HEADS // 2, ts, LANES), lambda b, i: (b, 0, i, 0)),
            pl.BlockSpec((1, ts, pool_w), tok),
            pl.BlockSpec((1, POOL_HALO, pool_w), lambda b, i: (b, jnp.maximum(i * hb - 1, 0), 0)),
            pl.BlockSpec((1, POOL_HALO, pool_w),
                         lambda b, i: (b, jnp.minimum((i + 1) * hb, n_halo_blocks - 1), 0)),
            pl.BlockSpec((1, ts, d), tok),
            pl.BlockSpec((1, N_MOD, d), lambda b, i: (b, 0, 0)),
            pl.BlockSpec(wpool.shape, lambda b, i: (0, 0, 0)),
            pl.BlockSpec((1, pool_w), const2),
            pl.BlockSpec(wout.shape, const2),
            pl.BlockSpec((1, d), const2),
            pl.BlockSpec(rw.shape, const2),
            pl.BlockSpec((1, LANES), const2),
        ],
        out_specs=[
            pl.BlockSpec((1, ts, d), tok),
            pl.BlockSpec((1, ts, d // 2), tok),
            pl.BlockSpec((SUBLANES, ts), lambda b, i: (0, b * nt + i)),
            pl.BlockSpec((2 * TOP_K, ts), lambda b, i: (0, b * nt + i)),
            pl.BlockSpec((SUBLANES, LANES), const2),
        ],
        out_shape=[
            jax.ShapeDtypeStruct((bsz, n, d), F32),
            jax.ShapeDtypeStruct((bsz, n, d // 2), jnp.uint32),
            jax.ShapeDtypeStruct((SUBLANES, bsz * n), F32),
            jax.ShapeDtypeStruct((2 * TOP_K, bsz * n), jnp.int32),
            jax.ShapeDtypeStruct((SUBLANES, LANES), F32),
        ],
        scratch_shapes=[
            pltpu.VMEM((ts + 2 * POOL_HALO, pool_w), F32),
            pltpu.VMEM((SUBLANES, LANES), F32),
        ],
        compiler_params=pltpu.CompilerParams(
            dimension_semantics=("arbitrary", "arbitrary"), vmem_limit_bytes=VMEM_SMALL),
        name="mix",
    )(att, u, u, u, x, mod, wpool, pscale, wout, g2, rw, rb)


def _sc_mesh():
    return plsc.VectorSubcoreMesh(core_axis_name="core", subcore_axis_name="subcore")


def _sc_worker_id():
    info = plsc.get_sparse_core_info()
    return lax.axis_index("subcore") * info.num_cores + lax.axis_index("core")


def _sc_num_workers():
    info = plsc.get_sparse_core_info()
    return info.num_cores * info.num_subcores


def _scatter_rows(x, dest3, n_out):
    t, d = x.shape
    top_k, n_chunks, rows = dest3.shape
    assert n_chunks % _sc_num_workers() == 0 and n_chunks * rows == t
    per_worker = n_chunks // _sc_num_workers()

    @functools.partial(
        pl.kernel,
        out_type=jax.ShapeDtypeStruct((n_out, d), x.dtype),
        mesh=_sc_mesh(),
        scratch_types=[pltpu.VMEM((top_k, rows), jnp.int32), pltpu.VMEM((rows, d), x.dtype)],
        name="sc_scatter",
    )
    def scatter(x_hbm, i_hbm, o_hbm, idx_v, rows_v):
        wid = _sc_worker_id()

        @pl.loop(0, per_worker)
        def _(c):
            chunk = wid * per_worker + c
            for kk in range(top_k):
                pltpu.sync_copy(i_hbm.at[kk, chunk], idx_v.at[kk])
            pltpu.sync_copy(x_hbm.at[pl.ds(pl.multiple_of(chunk * rows, SUBLANES), rows)], rows_v)
            for kk in range(top_k):
                pltpu.sync_copy(rows_v, o_hbm.at[idx_v.at[kk]])

    return scatter(x, dest3)


def _gather_rows(y, idx):
    n = idx.shape[0]
    d = y.shape[1]
    rows = SC_ROWS
    assert n % (rows * _sc_num_workers()) == 0
    per_worker = n // _sc_num_workers()
    n_chunks = per_worker // rows

    @functools.partial(
        pl.kernel,
        out_type=jax.ShapeDtypeStruct((n, d), y.dtype),
        mesh=_sc_mesh(),
        scratch_types=[pltpu.VMEM((rows,), jnp.int32), pltpu.VMEM((rows, d), y.dtype)],
        name="sc_gather",
    )
    def gather(y_hbm, i_hbm, o_hbm, idx_v, rows_v):
        base = _sc_worker_id() * per_worker

        @pl.loop(0, n_chunks)
        def _(c):
            off = pl.multiple_of(base + c * rows, SUBLANES)
            pltpu.sync_copy(i_hbm.at[pl.ds(off, rows)], idx_v)
            pltpu.sync_copy(y_hbm.at[idx_v], rows_v)
            pltpu.sync_copy(rows_v, o_hbm.at[pl.ds(off, rows)])

    return gather(y, idx)


def _moe_kernel(blk_ref, exp_ref, lo_ref, hi_ref, new_ref, init_ref, next_ref, slot_ref, xs_ref,
                wgu_hbm, bgu_ref, wd_hbm, bd_ref, ys_ref, wgu_buf, wd_buf, sems):
    i = pl.program_id(0)
    slot = slot_ref[i]

    de = wd_buf.shape[1]
    n_slabs = de // LANES

    def weight_copies(e, s):
        copies = [pltpu.make_async_copy(wd_hbm.at[e], wd_buf.at[s], sems.at[s, 1])]
        for j in range(n_slabs):
            for half in range(2):
                copies.append(pltpu.make_async_copy(
                    wgu_hbm.at[e, :, pl.ds(half * de + j * LANES, LANES)],
                    wgu_buf.at[s, :, pl.ds((2 * j + half) * LANES, LANES)], sems.at[s, 0]))
        return copies

    @pl.when(i == 0)
    def _():
        for cp in weight_copies(exp_ref[0], slot):
            cp.start()

    @pl.when(new_ref[i] == 1)
    def _():
        for cp in weight_copies(exp_ref[i], slot):
            cp.wait()

        @pl.when(next_ref[i] >= 0)
        def _():
            for cp in weight_copies(next_ref[i], 1 - slot):
                cp.start()

    @pl.when(init_ref[i] == 1)
    def _():
        ys_ref[...] = jnp.zeros_like(ys_ref)

    lo = lo_ref[i]
    hi = hi_ref[i]

    def expert_rows(r0, n_rows):
        x = _unpack_bf16_pairs(xs_ref[pl.ds(r0, n_rows), :])
        bgu = bgu_ref[0]
        tile = 2 * LANES
        y = None
        for pair in range(n_slabs // 2):
            acts = []
            for j in (2 * pair, 2 * pair + 1):
                gu = _dot(x, wgu_buf[slot, :, j * tile:(j + 1) * tile]) + bgu[:, j * tile:(j + 1) * tile]
                g = jnp.minimum(gu[:, :LANES], SWIGLU_LIMIT)
                lin = jnp.clip(gu[:, LANES:], -SWIGLU_LIMIT, SWIGLU_LIMIT)
                acts.append(g / (1.0 + jnp.exp(-SWIGLU_ALPHA * g)) * (lin + 1.0))
            part = _dot(jnp.concatenate(acts, axis=-1), wd_buf[slot, pair * tile:(pair + 1) * tile, :])
            y = part if y is None else y + part
        y = y + bd_ref[0]
        row = r0 + lax.broadcasted_iota(jnp.int32, (n_rows, 1), 0)
        mine = jnp.logical_and(row >= lo, row < hi)
        ys_ref[pl.ds(r0, n_rows), :] = jnp.where(mine, _pack_bf16_pairs(y), ys_ref[pl.ds(r0, n_rows), :])

    first = lo // MOE_SUB
    pieces = jnp.where(hi > lo, (hi + MOE_SUB - 1) // MOE_SUB - first, 0)
    for cnt in range(1, MOE_BLOCK // MOE_SUB + 1):
        @pl.when(pieces == cnt)
        def _(cnt=cnt):
            r0 = 0 if cnt * MOE_SUB == MOE_BLOCK else pl.multiple_of(first * MOE_SUB, MOE_SUB)
            expert_rows(r0, cnt * MOE_SUB)


def _moe_call(sched, xs, wgu, bgu, wd, bd):
    n_items = sched[0].shape[0]
    _, d, de2 = wgu.shape
    de = de2 // 2
    wmap = lambda i, blk, exp, lo, hi, new, init, nxt, slot: (exp[i], 0, 0)
    xmap = lambda i, blk, exp, lo, hi, new, init, nxt, slot: (blk[i], 0)
    return pl.pallas_call(
        _moe_kernel,
        grid_spec=pltpu.PrefetchScalarGridSpec(
            num_scalar_prefetch=8,
            grid=(n_items,),
            in_specs=[
                pl.BlockSpec((MOE_BLOCK, d // 2), xmap),
                pl.BlockSpec(memory_space=pl.ANY),
                pl.BlockSpec((1, 1, de2), wmap),
                pl.BlockSpec(memory_space=pl.ANY),
                pl.BlockSpec((1, 1, d), wmap),
            ],
            out_specs=pl.BlockSpec((MOE_BLOCK, d // 2), xmap),
            scratch_shapes=[
                pltpu.VMEM((2, d, de2), F32), pltpu.VMEM((2, de, d), F32),
                pltpu.SemaphoreType.DMA((2, 2)),
            ],
        ),
        out_shape=jax.ShapeDtypeStruct(xs.shape, xs.dtype),
        compiler_params=pltpu.CompilerParams(
            dimension_semantics=("arbitrary",), vmem_limit_bytes=VMEM_LARGE),
        name="moe",
    )(*sched, xs, wgu, bgu, wd, bd)


def _moe_schedule(counts, n_rows):
    n_blocks = n_rows // MOE_BLOCK
    n_items = n_blocks + N_EXPERTS
    ends = jnp.cumsum(counts)
    starts = ends - counts
    first_blk = starts // MOE_BLOCK
    last_blk = (ends - 1) // MOE_BLOCK
    items_per = jnp.where(counts > 0, last_blk - first_blk + 1, 0)
    item_ends = jnp.cumsum(items_per)
    item_starts = item_ends - items_per
    total = item_ends[-1]
    it = jnp.arange(n_items, dtype=jnp.int32)
    live = it < total
    itc = jnp.minimum(it, total - 1)
    exp = jnp.sum((item_ends[None, :] <= itc[:, None]).astype(jnp.int32), axis=1)
    is_exp = exp[:, None] == jnp.arange(N_EXPERTS, dtype=jnp.int32)[None, :]
    pick = lambda table: jnp.sum(jnp.where(is_exp, table[None, :], 0), axis=1)
    blk = pick(first_blk) + itc - pick(item_starts)
    lo = jnp.clip(pick(starts) - blk * MOE_BLOCK, 0, MOE_BLOCK)
    hi = jnp.clip(pick(ends) - blk * MOE_BLOCK, 0, MOE_BLOCK)
    hi = jnp.where(live, hi, lo)
    prev_exp = jnp.concatenate([jnp.full((1,), -1, jnp.int32), exp[:-1]])
    prev_blk = jnp.concatenate([jnp.full((1,), -1, jnp.int32), blk[:-1]])
    new = jnp.logical_and(live, exp != prev_exp)
    init = jnp.logical_and(live, blk != prev_blk)
    slot = (jnp.cumsum(new.astype(jnp.int32)) - 1) % 2
    ar = jnp.arange(N_EXPERTS, dtype=jnp.int32)
    later = jnp.logical_and(counts[None, :] > 0, ar[None, :] > ar[:, None])
    next_exp = jnp.min(jnp.where(later, ar[None, :], N_EXPERTS), axis=1)
    next_exp = jnp.where(next_exp == N_EXPERTS, -1, next_exp)
    nxt = pick(next_exp)
    as_i32 = lambda a: a.astype(jnp.int32)
    return tuple(as_i32(a) for a in (blk, exp, lo, hi, new, init, nxt, slot)), starts


def _combine_kernel(*refs):
    yg_refs = refs[:TOP_K]
    x1_ref, gate_ref, mod_ref, fg_ref, o_ref = refs[TOP_K:]
    gt = gate_ref[...]
    gates = jnp.concatenate([gt, jnp.zeros((LANES - gt.shape[0], gt.shape[1]), F32)], axis=0).T
    y = gates[:, 0:1] * _unpack_bf16_pairs(yg_refs[0][...])
    for kk in range(1, TOP_K):
        y = y + gates[:, kk:kk + 1] * _unpack_bf16_pairs(yg_refs[kk][...])
    m = mod_ref[0]
    x2 = x1_ref[...] + m[5:6] * y
    o_ref[...] = _rms(x2) * fg_ref[...]


def _combine_call(yg, x1, gates, mod, fg, n_seq, part):
    t, d = x1.shape
    tt = COMBINE_TT
    tiles_per_seq = n_seq // tt
    n_tiles = yg.shape[0] // (TOP_K * tt)
    first = part * n_tiles
    tok = lambda i: (first + i, 0)
    slot_specs = [pl.BlockSpec((tt, d // 2), functools.partial(lambda kk, i: (kk * n_tiles + i, 0), kk))
                  for kk in range(TOP_K)]
    return pl.pallas_call(
        _combine_kernel,
        grid=(n_tiles,),
        in_specs=slot_specs + [
            pl.BlockSpec((tt, d), tok),
            pl.BlockSpec((SUBLANES, tt), lambda i: (0, first + i)),
            pl.BlockSpec((1, N_MOD, d), lambda i: ((first + i) // tiles_per_seq, 0, 0)),
            pl.BlockSpec((1, d), lambda i: (0, 0)),
        ],
        out_specs=pl.BlockSpec((tt, d), tok),
        out_shape=jax.ShapeDtypeStruct((t, d), F32),
        input_output_aliases={TOP_K: 0},
        compiler_params=pltpu.CompilerParams(
            dimension_semantics=("arbitrary",), vmem_limit_bytes=VMEM_SMALL),
        name="combine",
    )(*([yg] * TOP_K), x1, gates, mod, fg)


def _pad_cols(a, width):
    return jnp.pad(a, ((0, 0), (0, width - a.shape[1])))


def _rope_slab(w_rope):
    return jnp.pad(w_rope, ((0, 0), (QK_NOPE, LANES - QK_NOPE - QK_ROPE)))


def _prep_w_in(w_in):
    kr0 = Q_LORA + KV_LORA
    w_kr = w_in[:, kr0:kr0 + QK_ROPE]
    return jnp.concatenate(
        [w_in[:, :kr0], _rope_slab(w_kr), w_in[:, kr0 + QK_ROPE:]], axis=1).astype(BF16)


def _prep_w_uq(w_uq):
    per = QK_NOPE + QK_ROPE
    rows = w_uq.shape[0]
    w = w_uq.reshape(rows, N_HEADS, per)
    slab = jnp.pad(w, ((0, 0), (0, 0), (0, LANES - per)))
    return slab.reshape(rows, -1).T.astype(BF16)


def _prep_w_ukv(w_ukv):
    per = QK_NOPE + V_DIM
    rows = w_ukv.shape[0]
    w = w_ukv.reshape(rows, N_HEADS, per)
    wk = jnp.pad(w[:, :, :QK_NOPE], ((0, 0), (0, 0), (0, LANES - QK_NOPE))).reshape(rows, -1)
    wvt = jnp.pad(jnp.transpose(w[:, :, QK_NOPE:], (1, 2, 0)), ((0, 0), (0, VT_ROWS - V_DIM), (0, 0)))
    return wk.astype(BF16), wvt.reshape(N_HEADS * VT_ROWS, rows).astype(BF16)


def _rope_tables(n_lat):
    rows = n_lat // GRID_W
    nf = QK_ROPE // 4
    row = jnp.repeat(jnp.arange(rows, dtype=F32), GRID_W)
    col = jnp.tile(jnp.arange(GRID_W, dtype=F32), rows)
    freqs = ROPE_BASE ** (-jnp.arange(nf, dtype=F32) / nf)
    ang = jnp.concatenate([row[:, None] * freqs, col[:, None] * freqs], axis=-1)
    cos, sin = jnp.cos(ang), jnp.sin(ang)
    ones = jnp.ones((n_lat, QK_NOPE), F32)
    zeros = jnp.zeros((n_lat, QK_NOPE), F32)
    cs = _pad_cols(jnp.concatenate([ones, cos, cos], axis=1), LANES)
    sn = _pad_cols(jnp.concatenate([zeros, -sin, sin], axis=1), LANES)
    return cs, sn


def kernel(x, c, ctx, c_ctx, w_mod, b_mod, norm1_g, w_in, q_norm_g, kv_norm_g, w_uq, w_ukv, w_pool,
           pool_scale, w_out, norm2_g, router_w, router_b, w_gate_up, b_gate_up, w_down, b_down,
           final_g):
    bsz, n, d = x.shape
    n_ctx = ctx.shape[1]
    t = bsz * n
    assert w_mod.shape[0] == 1, "single-layer block"
    l = 0
    tp = t // COMBINE_PARTS
    assert bsz + 1 <= SUBLANES and d % (2 * LANES) == 0 and n_ctx % ATTN_KC == 0
    assert POOL_HALO == SUBLANES and MIX_TS % POOL_HALO == 0
    assert n % GRID_W == 0 and n % FRONT_TS == 0 and n % MIX_TS == 0 and n % ATTN_KC == 0
    assert n % (ATTN_TQ * ATTN_TILES) == 0 and (t * TOP_K) % MOE_BLOCK == 0 and MOE_BLOCK % MOE_SUB == 0
    assert t % COMBINE_PARTS == 0 and tp % COMBINE_TT == 0 and n % COMBINE_TT == 0 and t % SC_ROWS == 0

    cc = jnp.concatenate([c, c_ctx[None, :], jnp.zeros((SUBLANES - bsz - 1, d), F32)], axis=0)
    mod = _mod_call(cc, w_mod[l], b_mod[l][None, :]).reshape(SUBLANES, N_MOD, d)
    mod_lat, mod_ctx = mod[:bsz], mod[bsz:bsz + 1]

    win = _prep_w_in(w_in[l])
    wuqt = _prep_w_uq(w_uq[l])
    wk, wvt = _prep_w_ukv(w_ukv[l])
    cs, sn = _rope_tables(n)
    cs_ctx = jnp.broadcast_to((jnp.arange(LANES) < QK_NOPE + QK_ROPE).astype(F32), (n_ctx, LANES))
    sn_ctx = jnp.zeros((n_ctx, LANES), F32)
    g1 = norm1_g[l][None, :]
    qg = q_norm_g[l][None, :]
    kvg = kv_norm_g[l][None, :]

    qt, k, vt, u = _front_call(x, mod_lat, True, g1, win, qg, kvg, wuqt, wk, wvt, cs, sn,
                               is_ctx=False, ts=FRONT_TS)
    kc, vct = _front_call(ctx, mod_ctx, False, g1, win[:, Q_LORA:Q_LORA + 2 * LANES], qg, kvg, wuqt,
                          wk, wvt, cs_ctx, sn_ctx, is_ctx=True, ts=n_ctx)
    att = _attn_call(qt, k, vt, kc, vct)

    rw = _pad_cols(router_w[l], LANES)
    rw_hi = rw.astype(BF16)
    rw = jnp.concatenate([rw_hi, (rw - rw_hi.astype(F32)).astype(BF16)], axis=1)
    rb = jnp.concatenate([router_b[l], jnp.full((LANES - N_EXPERTS,), -jnp.inf, F32)])[None, :]
    x1, h2, gates, ridx, cnt = _mix_call(
        att, u, x, mod_lat, w_pool[l].astype(BF16), pool_scale[l][None, :], w_out[l].astype(BF16),
        norm2_g[l][None, :], rw, rb)

    counts = cnt[0, :N_EXPERTS].astype(jnp.int32)
    sched, starts = _moe_schedule(counts, t * TOP_K)
    is_exp = ridx[:TOP_K, None, :] == jnp.arange(N_EXPERTS, dtype=jnp.int32)[None, :, None]
    dest_t = jnp.sum(jnp.where(is_exp, starts[None, :, None], 0), axis=1) + ridx[TOP_K:]

    xs = _scatter_rows(h2.reshape(t, d // 2), dest_t.reshape(TOP_K, t // SC_ROWS, SC_ROWS), t * TOP_K)
    de = w_down.shape[2]
    assert de % (2 * LANES) == 0
    bgu = b_gate_up[l].reshape(N_EXPERTS, 2, de // LANES, LANES).transpose(0, 2, 1, 3)
    ys = _moe_call(sched, xs, w_gate_up[l], bgu.reshape(N_EXPERTS, 1, 2 * de), w_down[l],
                   b_down[l][:, None, :])
    out = x1.reshape(t, d)
    for part in range(COMBINE_PARTS):
        idx = dest_t[:, part * tp:(part + 1) * tp].reshape(-1)
        yg = _gather_rows(ys, idx)
        out = _combine_call(yg, out, gates, mod_lat, final_g[None, :], n, part)
    return out.reshape(bsz, n, d)
```

```python
import functools
import math

import jax
import jax.numpy as jnp
from jax import lax
from jax.experimental import pallas as pl
from jax.experimental.pallas import tpu as pltpu
from jax.experimental.pallas import tpu_sc as plsc

F32 = jnp.float32
BF16 = jnp.bfloat16
HIGHEST = lax.Precision.HIGHEST

N_HEADS = 8
QK_NOPE = 64
QK_ROPE = 32
V_DIM = 64
Q_LORA = 256
KV_LORA = 128
GRID_W = 64
ROPE_BASE = 10000.0
POOL_WINDOWS = (2, 4, 8, 16)
POOL_CH = 128
N_EXPERTS = 32
TOP_K = 4
SWIGLU_LIMIT = 7.0
SWIGLU_ALPHA = 1.702
N_MOD = 6
EPS = 1e-6

LANES = 128
SUBLANES = 8
BF16_SUBLANES = 16
MIB = 1024 * 1024

POOL_HALO = max(POOL_WINDOWS) // 2
VT_ROWS = -(-(V_DIM + 1) // BF16_SUBLANES) * BF16_SUBLANES

FRONT_TS = 1024
ATTN_TQ = 512
ATTN_TILES = 4
ATTN_KC = 256
MIX_TS = 512
MOE_BLOCK = 512
MOE_SUB = 64
COMBINE_TT = 512
COMBINE_PARTS = 8
SC_ROWS = 128

VMEM_SMALL = 32 * MIB
VMEM_LARGE = 56 * MIB


def _dot(a, b, **kw):
    return jnp.dot(a, b, preferred_element_type=F32, **kw)


def _rms(x):
    return x * lax.rsqrt(jnp.mean(x * x, axis=-1, keepdims=True) + EPS)


def _pack_bf16_pairs(x):
    w = x.shape[1] // 2
    return pltpu.pack_elementwise([x[:, :w], x[:, w:]], packed_dtype=BF16)


def _unpack_bf16_pairs(words):
    halves = [pltpu.unpack_elementwise(words, index=h, packed_dtype=BF16, unpacked_dtype=F32)
              for h in range(2)]
    return jnp.concatenate(halves, axis=-1)


def _mod_kernel(c_ref, w_ref, b_ref, o_ref):
    c = c_ref[...]
    a = c / (1.0 + jnp.exp(-c))
    o_ref[...] = _dot(a, w_ref[...], precision=HIGHEST) + b_ref[...]


def _mod_call(cc, w_mod, b_mod):
    d = w_mod.shape[0]
    return pl.pallas_call(
        _mod_kernel,
        grid=(N_MOD,),
        in_specs=[
            pl.BlockSpec((SUBLANES, d), lambda j: (0, 0)),
            pl.BlockSpec((d, d), lambda j: (0, j)),
            pl.BlockSpec((1, d), lambda j: (0, j)),
        ],
        out_specs=pl.BlockSpec((SUBLANES, d), lambda j: (0, j)),
        out_shape=jax.ShapeDtypeStruct((SUBLANES, N_MOD * d), F32),
        compiler_params=pltpu.CompilerParams(
            dimension_semantics=("arbitrary",), vmem_limit_bytes=VMEM_SMALL),
        name="mod",
    )(cc, w_mod, b_mod)


def _front_kernel(x_ref, mod_ref, g1_ref, win_ref, qg_ref, kvg_ref, wuqt_ref, wk_ref, wvt_ref,
                  cs_ref, sn_ref, cst_ref, snt_ref, *out_refs, is_ctx, scale):
    x = x_ref[0]
    m = mod_ref[0]
    h = _rms(x) * g1_ref[...] * (1.0 + m[1:2]) + m[0:1]
    z = _dot(h.astype(BF16), win_ref[...])
    if is_ctx:
        k_ref, vt_ref = out_refs
        zkv = z
    else:
        qt_ref, k_ref, vt_ref, u_ref = out_refs
        zkv = z[:, Q_LORA:Q_LORA + 2 * LANES]
        u_ref[0] = z[:, Q_LORA + 2 * LANES:]

    ckv = _rms(zkv[:, :KV_LORA]) * kvg_ref[...]
    kk = _dot(ckv.astype(BF16), wk_ref[...])
    kra = zkv[:, LANES:2 * LANES]
    lane = lax.broadcasted_iota(jnp.int32, kra.shape, 1)
    half = QK_ROPE // 2
    partner = jnp.where(lane < QK_NOPE + half, pltpu.roll(kra, LANES - half, axis=1),
                        pltpu.roll(kra, half, axis=1))
    kr = kra * cs_ref[...] + partner * sn_ref[...]
    for hd in range(N_HEADS):
        k_ref[0, hd] = (kk[:, hd * LANES:(hd + 1) * LANES] + kr).astype(BF16)
    vt = _dot(wvt_ref[...], ckv.T.astype(BF16))
    row = lax.broadcasted_iota(jnp.int32, (VT_ROWS, 1), 0)
    ones = jnp.where(row == V_DIM, 1.0, 0.0)
    for hd in range(N_HEADS):
        vt_ref[0, hd] = (vt[hd * VT_ROWS:(hd + 1) * VT_ROWS] + ones).astype(BF16)

    if not is_ctx:
        cq = _rms(z[:, :Q_LORA]) * qg_ref[...]
        qqt = _dot(wuqt_ref[...], cq.T.astype(BF16))
        cst = cst_ref[...]
        snt = snt_ref[...]
        r1, r2, r3 = QK_NOPE, QK_NOPE + QK_ROPE // 2, QK_NOPE + QK_ROPE
        for hd in range(N_HEADS):
            qa = qqt[hd * LANES:(hd + 1) * LANES]
            qb = jnp.concatenate([qa[:r1], qa[r2:r3], qa[r1:r2], qa[r3:]], axis=0)
            qt_ref[0, hd] = ((qa * cst + qb * snt) * scale).astype(BF16)


def _front_call(xs, mod, mod_per_batch, g1, win, qg, kvg, wuqt, wk, wvt, cs, sn, *, is_ctx, ts):
    bsz, n, d = xs.shape
    nt = n // ts
    scale = math.log2(math.e) / math.sqrt(QK_NOPE + QK_ROPE)
    const = lambda b, i: (0, 0)
    mod_map = (lambda b, i: (b, 0, 0)) if mod_per_batch else (lambda b, i: (0, 0, 0))
    in_specs = [
        pl.BlockSpec((1, ts, d), lambda b, i: (b, i, 0)),
        pl.BlockSpec((1, N_MOD, d), mod_map),
        pl.BlockSpec((1, d), const),
        pl.BlockSpec(win.shape, const),
        pl.BlockSpec((1, Q_LORA), const),
        pl.BlockSpec((1, KV_LORA), const),
        pl.BlockSpec(wuqt.shape, const),
        pl.BlockSpec(wk.shape, const),
        pl.BlockSpec(wvt.shape, const),
        pl.BlockSpec((ts, LANES), lambda b, i: (i, 0)),
        pl.BlockSpec((ts, LANES), lambda b, i: (i, 0)),
        pl.BlockSpec((LANES, ts), lambda b, i: (0, i)),
        pl.BlockSpec((LANES, ts), lambda b, i: (0, i)),
    ]
    k_spec = pl.BlockSpec((1, N_HEADS, ts, LANES), lambda b, i: (b, 0, i, 0))
    k_shape = jax.ShapeDtypeStruct((bsz, N_HEADS, n, LANES), BF16)
    qt_spec = pl.BlockSpec((1, N_HEADS, LANES, ts), lambda b, i: (b, 0, 0, i))
    qt_shape = jax.ShapeDtypeStruct((bsz, N_HEADS, LANES, n), BF16)
    vt_spec = pl.BlockSpec((1, N_HEADS, VT_ROWS, ts), lambda b, i: (b, 0, 0, i))
    vt_shape = jax.ShapeDtypeStruct((bsz, N_HEADS, VT_ROWS, n), BF16)
    if is_ctx:
        out_specs = [k_spec, vt_spec]
        out_shape = [k_shape, vt_shape]
    else:
        pool_w = win.shape[1] - Q_LORA - 2 * LANES
        out_specs = [qt_spec, k_spec, vt_spec, pl.BlockSpec((1, ts, pool_w), lambda b, i: (b, i, 0))]
        out_shape = [qt_shape, k_shape, vt_shape, jax.ShapeDtypeStruct((bsz, n, pool_w), F32)]
    return pl.pallas_call(
        functools.partial(_front_kernel, is_ctx=is_ctx, scale=scale),
        grid=(bsz, nt),
        in_specs=in_specs,
        out_specs=out_specs,
        out_shape=out_shape,
        compiler_params=pltpu.CompilerParams(
            dimension_semantics=("arbitrary", "arbitrary"), vmem_limit_bytes=VMEM_LARGE),
        name="front_ctx" if is_ctx else "front",
    )(xs, mod, g1, win, qg, kvg, wuqt, wk, wvt, cs, sn, cs.T, sn.T)


def _attn_kernel(qt_ref, k_ref, vt_ref, kc_ref, vct_ref, o_ref, s_ref, mlc_ref, mxb_ref, oe_ref):
    n_ctx = kc_ref.shape[2]
    n_lat = k_ref.shape[2]
    chunks = [(None, 0, n_ctx)] + [(c * ATTN_KC, n_ctx + c * ATTN_KC, ATTN_KC)
                                   for c in range(n_lat // ATTN_KC)]
    sub = SUBLANES

    def q_cols(tile):
        if isinstance(tile, int):
            return pl.ds(tile * ATTN_TQ, ATTN_TQ)
        return pl.ds(pl.multiple_of(tile * ATTN_TQ, ATTN_TQ), ATTN_TQ)

    def score_chunk(tile, hd, buf, ci):
        off, soff, w = chunks[ci]
        keys = kc_ref[0, hd] if off is None else k_ref[0, hd, off:off + w, :]
        s = _dot(keys, qt_ref[0, hd, :, q_cols(tile)])
        s_ref[buf, soff:soff + w, :] = s
        mx = s[0:sub]
        for r in range(1, w // sub):
            mx = jnp.maximum(mx, s[r * sub:(r + 1) * sub])
        mlc_ref[buf, ci] = mx

    def row_max(buf):
        mx = mlc_ref[buf, 0]
        for ci in range(1, len(chunks)):
            mx = jnp.maximum(mx, mlc_ref[buf, ci])
        mxb_ref[...] = jnp.broadcast_to(jnp.max(mx, axis=0, keepdims=True), mx.shape)

    def weight_chunk(hd, buf, ci):
        off, soff, w = chunks[ci]
        p = jnp.exp2(s_ref[buf, soff:soff + w, :] - mxb_ref[0:1, :]).astype(BF16)
        vt = vct_ref[0, hd] if off is None else vt_ref[0, hd, :, off:off + w]
        return _dot(vt, p)

    def stage(hw, bw, scoring, bs):
        row_max(bw)
        acc = None
        for ci in range(len(chunks)):
            if scoring is not None:
                score_chunk(*scoring, bs, ci)
            part = weight_chunk(hw, bw, ci)
            acc = part if acc is None else acc + part
        return acc[0:V_DIM] / acc[V_DIM:V_DIM + 1]

    def write_pair(tile, j, ot_odd):
        pair_t = jnp.concatenate([oe_ref[...], ot_odd], axis=0)
        o_ref[0, j, q_cols(tile), :] = pair_t.T.astype(BF16)

    pairs_per_tile = N_HEADS // 2
    n_pairs = (qt_ref.shape[3] // ATTN_TQ) * pairs_per_tile

    def split(p):
        return p // pairs_per_tile, p % pairs_per_tile

    for ci in range(len(chunks)):
        score_chunk(0, 0, 0, ci)

    def head_pair(p, carry):
        tile, j = split(p)
        nxt_tile, nxt_j = split(p + 1)
        oe_ref[...] = stage(2 * j, 0, (tile, 2 * j + 1), 1)
        write_pair(tile, j, stage(2 * j + 1, 1, (nxt_tile, 2 * nxt_j), 0))
        return carry

    lax.fori_loop(0, n_pairs - 1, head_pair, 0)
    tile, j = split(n_pairs - 1)
    oe_ref[...] = stage(2 * j, 0, (tile, 2 * j + 1), 1)
    write_pair(tile, j, stage(2 * j + 1, 1, None, None))


def _attn_call(qt, k, vt, kc, vct):
    bsz, _, _, n = qt.shape
    n_ctx = kc.shape[2]
    tq = ATTN_TQ
    n_chunks = 1 + n // ATTN_KC
    per_batch = lambda b, i: (b, 0, 0, 0)
    resident = dict(pipeline_mode=pl.Buffered(1))
    rows_per_step = ATTN_TILES * tq
    return pl.pallas_call(
        _attn_kernel,
        grid=(bsz, n // rows_per_step),
        in_specs=[
            pl.BlockSpec((1, N_HEADS, LANES, rows_per_step), lambda b, i: (b, 0, 0, i)),
            pl.BlockSpec((1, N_HEADS, n, LANES), per_batch, **resident),
            pl.BlockSpec((1, N_HEADS, VT_ROWS, n), per_batch, **resident),
            pl.BlockSpec((1, N_HEADS, n_ctx, LANES), per_batch, **resident),
            pl.BlockSpec((1, N_HEADS, VT_ROWS, n_ctx), per_batch, **resident),
        ],
        out_specs=pl.BlockSpec((1, N_HEADS // 2, rows_per_step, LANES), lambda b, i: (b, 0, i, 0)),
        out_shape=jax.ShapeDtypeStruct((bsz, N_HEADS // 2, n, LANES), BF16),
        scratch_shapes=[
            pltpu.VMEM((2, n_ctx + n, tq), F32),
            pltpu.VMEM((2, n_chunks, SUBLANES, tq), F32),
            pltpu.VMEM((SUBLANES, tq), F32),
            pltpu.VMEM((V_DIM, tq), F32),
        ],
        compiler_params=pltpu.CompilerParams(
            dimension_semantics=("arbitrary", "arbitrary"), vmem_limit_bytes=VMEM_LARGE),
        name="attn",
    )(qt, k, vt, kc, vct)


def _mix_kernel(att_ref, u_ref, up_ref, un_ref, x_ref, mod_ref, wpool_ref, pscale_ref, wout_ref,
                g2_ref, rw_ref, rb_ref,
                x1_ref, h2_ref, gate_ref, ridx_ref, cnt_ref, ue_ref, base_ref, *, n_seq):
    b = pl.program_id(0)
    i = pl.program_id(1)
    n_tiles = pl.num_programs(1)
    ts = u_ref.shape[1]

    @pl.when(jnp.logical_and(b == 0, i == 0))
    def _():
        base_ref[...] = jnp.zeros_like(base_ref)

    u = u_ref[0]
    ue_ref[0:POOL_HALO] = jnp.where(i > 0, up_ref[0], 0.0)
    ue_ref[POOL_HALO:POOL_HALO + ts] = u
    ue_ref[POOL_HALO + ts:2 * POOL_HALO + ts] = jnp.where(i < n_tiles - 1, un_ref[0], 0.0)
    t = i * ts + lax.broadcasted_iota(jnp.int32, (ts, 1), 0)
    ys = []
    for g, w in enumerate(POOL_WINDOWS):
        half = w // 2
        lanes = slice(g * POOL_CH, (g + 1) * POOL_CH)
        ws = ue_ref[POOL_HALO - half:POOL_HALO - half + ts, lanes]
        for jj in range(-half + 1, half):
            ws = ws + ue_ref[POOL_HALO + jj:POOL_HALO + jj + ts, lanes]
        count = (jnp.minimum(t + half, n_seq) - jnp.maximum(t - half, 0)).astype(F32)
        mixed = (ws / count - u[:, lanes]).astype(BF16)
        ys.append(_dot(mixed, wpool_ref[g]))
    pool = jnp.concatenate(ys, axis=-1) * pscale_ref[...]

    cat = jnp.concatenate([att_ref[0, j] for j in range(N_HEADS // 2)] + [pool.astype(BF16)], axis=-1)
    m = mod_ref[0]
    x1 = x_ref[0] + m[2:3] * _dot(cat, wout_ref[...])
    x1_ref[0] = x1
    h2 = _rms(x1) * g2_ref[...] * (1.0 + m[4:5]) + m[3:4]
    h2_ref[0] = _pack_bf16_pairs(h2)

    h_hi = h2.astype(BF16)
    h_lo = (h2 - h_hi.astype(F32)).astype(BF16)
    hi_part = _dot(h_hi, rw_ref[...])
    logits = (hi_part[:, :LANES] + hi_part[:, LANES:] + _dot(h_lo, rw_ref[:, :LANES])) + rb_ref[...]
    lane = lax.broadcasted_iota(jnp.int32, logits.shape, 1).astype(F32)
    vals, idxs = [], []
    for _k in range(TOP_K):
        mv = jnp.max(logits, axis=-1, keepdims=True)
        ix = jnp.min(jnp.where(logits == mv, lane, float(LANES)), axis=-1, keepdims=True)
        vals.append(mv)
        idxs.append(ix)
        logits = jnp.where(lane == ix, -jnp.inf, logits)
    es = [jnp.exp(v - vals[0]) for v in vals]
    den = es[0] + es[1] + es[2] + es[3]

    onehot = jnp.zeros(lane.shape, F32)
    for ix in idxs:
        onehot = onehot + jnp.where(lane == ix, 1.0, 0.0)
    row = lax.broadcasted_iota(jnp.int32, (ts, ts), 0)
    col = lax.broadcasted_iota(jnp.int32, (ts, ts), 1)
    tri = jnp.where(col < row, 1.0, 0.0).astype(BF16)
    before = _dot(tri, onehot.astype(BF16)) + base_ref[0:1, :]
    base_new = base_ref[0:1, :] + jnp.sum(onehot, axis=0, keepdims=True)
    base_ref[...] = jnp.broadcast_to(base_new, base_ref.shape)
    cnt_ref[...] = jnp.broadcast_to(base_new, cnt_ref.shape)

    route = jnp.zeros(lane.shape, F32)
    for kk in range(TOP_K):
        rank = jnp.sum(jnp.where(lane == idxs[kk], before, 0.0), axis=-1, keepdims=True)
        route = jnp.where(lane == float(kk), idxs[kk], route)
        route = jnp.where(lane == float(TOP_K + kk), rank, route)
        route = jnp.where(lane == float(2 * TOP_K + kk), es[kk] / den, route)
    route_t = route.T
    ridx_ref[...] = route_t[0:2 * TOP_K].astype(jnp.int32)
    gate_ref[...] = route_t[2 * TOP_K:2 * TOP_K + SUBLANES]


def _mix_call(att, u, x, mod, wpool, pscale, wout, g2, rw, rb):
    bsz, n, d = x.shape
    ts = MIX_TS
    nt = n // ts
    pool_w = u.shape[2]
    hb = ts // POOL_HALO
    n_halo_blocks = n // POOL_HALO
    const2 = lambda b, i: (0, 0)
    tok = lambda b, i: (b, i, 0)
    return pl.pallas_call(
        functools.partial(_mix_kernel, n_seq=n),
        grid=(bsz, nt),
        in_specs=[
            pl.BlockSpec((1, N_HEADS // 2, ts, LANES), lambda b, i: (b, 0, i, 0)),
            pl.BlockSpec((1, ts, pool_w), tok),
            pl.BlockSpec((1, POOL_HALO, pool_w), lambda b, i: (b, jnp.maximum(i * hb - 1, 0), 0)),
            pl.BlockSpec((1, POOL_HALO, pool_w),
                         lambda b, i: (b, jnp.minimum((i + 1) * hb, n_halo_blocks - 1), 0)),
            pl.BlockSpec((1, ts, d), tok),
            pl.BlockSpec((1, N_MOD, d), lambda b, i: (b, 0, 0)),
            pl.BlockSpec(wpool.shape, lambda b, i: (0, 0, 0)),
            pl.BlockSpec((1, pool_w), const2),
            pl.BlockSpec(wout.shape, const2),
            pl.BlockSpec((1, d), const2),
            pl.BlockSpec(rw.shape, const2),
            pl.BlockSpec((1, LANES), const2),
        ],
        out_specs=[
            pl.BlockSpec((1, ts, d), tok),
            pl.BlockSpec((1, ts, d // 2), tok),
            pl.BlockSpec((SUBLANES, ts), lambda b, i: (0, b * nt + i)),
            pl.BlockSpec((2 * TOP_K, ts), lambda b, i: (0, b * nt + i)),
            pl.BlockSpec((SUBLANES, LANES), const2),
        ],
        out_shape=[
            jax.ShapeDtypeStruct((bsz, n, d), F32),
            jax.ShapeDtypeStruct((bsz, n, d // 2), jnp.uint32),
            jax.ShapeDtypeStruct((SUBLANES, bsz * n), F32),
            jax.ShapeDtypeStruct((2 * TOP_K, bsz * n), jnp.int32),
            jax.ShapeDtypeStruct((SUBLANES, LANES), F32),
        ],
        scratch_shapes=[
            pltpu.VMEM((ts + 2 * POOL_HALO, pool_w), F32),
            pltpu.VMEM((SUBLANES, LANES), F32),
        ],
        compiler_params=pltpu.CompilerParams(
            dimension_semantics=("arbitrary", "arbitrary"), vmem_limit_bytes=VMEM_SMALL),
        name="mix",
    )(att, u, u, u, x, mod, wpool, pscale, wout, g2, rw, rb)


def _sc_mesh():
    return plsc.VectorSubcoreMesh(core_axis_name="core", subcore_axis_name="subcore")


def _sc_worker_id():
    info = plsc.get_sparse_core_info()
    return lax.axis_index("subcore") * info.num_cores + lax.axis_index("core")


def _sc_num_workers():
    info = plsc.get_sparse_core_info()
    return info.num_cores * info.num_subcores


def _scatter_rows(x, dest3, n_out):
    t, d = x.shape
    top_k, n_chunks, rows = dest3.shape
    assert n_chunks % _sc_num_workers() == 0 and n_chunks * rows == t
    per_worker = n_chunks // _sc_num_workers()

    @functools.partial(
        pl.kernel,
        out_type=jax.ShapeDtypeStruct((n_out, d), x.dtype),
        mesh=_sc_mesh(),
        scratch_types=[pltpu.VMEM((top_k, rows), jnp.int32), pltpu.VMEM((rows, d), x.dtype)],
        name="sc_scatter",
    )
    def scatter(x_hbm, i_hbm, o_hbm, idx_v, rows_v):
        wid = _sc_worker_id()

        @pl.loop(0, per_worker)
        def _(c):
            chunk = wid * per_worker + c
            for kk in range(top_k):
                pltpu.sync_copy(i_hbm.at[kk, chunk], idx_v.at[kk])
            pltpu.sync_copy(x_hbm.at[pl.ds(pl.multiple_of(chunk * rows, SUBLANES), rows)], rows_v)
            for kk in range(top_k):
                pltpu.sync_copy(rows_v, o_hbm.at[idx_v.at[kk]])

    return scatter(x, dest3)


def _gather_rows(y, idx):
    n = idx.shape[0]
    d = y.shape[1]
    rows = SC_ROWS // 2
    assert n % (rows * _sc_num_workers()) == 0
    per_worker = n // _sc_num_workers()
    n_chunks = per_worker // rows

    @functools.partial(
        pl.kernel,
        out_type=jax.ShapeDtypeStruct((n, d), y.dtype),
        mesh=_sc_mesh(),
        scratch_types=[pltpu.VMEM((per_worker,), jnp.int32), pltpu.VMEM((2, rows, d), y.dtype),
                       pltpu.SemaphoreType.DMA((2,))],
        name="sc_gather",
    )
    def gather(y_hbm, i_hbm, o_hbm, idx_v, rows_v, sems):
        base = pl.multiple_of(_sc_worker_id() * per_worker, SUBLANES)
        pltpu.sync_copy(i_hbm.at[pl.ds(base, per_worker)], idx_v)

        def fetch(c):
            return pltpu.make_async_copy(y_hbm.at[idx_v.at[pl.ds(c * rows, rows)]], rows_v.at[c % 2],
                                         sems.at[c % 2])

        fetch(0).start()
        for c in range(n_chunks):
            if c + 1 < n_chunks:
                fetch(c + 1).start()
            fetch(c).wait()
            pltpu.sync_copy(rows_v.at[c % 2], o_hbm.at[pl.ds(base + c * rows, rows)])

    return gather(y, idx)


def _moe_kernel(blk_ref, exp_ref, lo_ref, hi_ref, new_ref, init_ref, next_ref, slot_ref, xs_ref,
                wgu_hbm, bgu_ref, wd_hbm, bd_ref, ys_ref, wgu_buf, wd_buf, sems):
    i = pl.program_id(0)
    slot = slot_ref[i]

    def weight_copies(e, s):
        return (pltpu.make_async_copy(wgu_hbm.at[e], wgu_buf.at[s], sems.at[s, 0]),
                pltpu.make_async_copy(wd_hbm.at[e], wd_buf.at[s], sems.at[s, 1]))

    @pl.when(i == 0)
    def _():
        for cp in weight_copies(exp_ref[0], slot):
            cp.start()

    @pl.when(new_ref[i] == 1)
    def _():
        for cp in weight_copies(exp_ref[i], slot):
            cp.wait()

        @pl.when(next_ref[i] >= 0)
        def _():
            for cp in weight_copies(next_ref[i], 1 - slot):
                cp.start()

    @pl.when(init_ref[i] == 1)
    def _():
        ys_ref[...] = jnp.zeros_like(ys_ref)

    lo = lo_ref[i]
    hi = hi_ref[i]

    def expert_rows(r0, n_rows):
        de = wd_buf.shape[1]
        x = _unpack_bf16_pairs(xs_ref[pl.ds(r0, n_rows), :])
        gu = _dot(x, wgu_buf[slot]) + bgu_ref[0]
        g = jnp.minimum(gu[:, :de], SWIGLU_LIMIT)
        lin = jnp.clip(gu[:, de:], -SWIGLU_LIMIT, SWIGLU_LIMIT)
        act = g / (1.0 + jnp.exp(-SWIGLU_ALPHA * g)) * (lin + 1.0)
        y = _dot(act, wd_buf[slot]) + bd_ref[0]
        row = r0 + lax.broadcasted_iota(jnp.int32, (n_rows, 1), 0)
        mine = jnp.logical_and(row >= lo, row < hi)
        ys_ref[pl.ds(r0, n_rows), :] = jnp.where(mine, _pack_bf16_pairs(y), ys_ref[pl.ds(r0, n_rows), :])

    first = lo // MOE_SUB
    pieces = jnp.where(hi > lo, (hi + MOE_SUB - 1) // MOE_SUB - first, 0)
    for cnt in range(1, MOE_BLOCK // MOE_SUB + 1):
        @pl.when(pieces == cnt)
        def _(cnt=cnt):
            r0 = 0 if cnt * MOE_SUB == MOE_BLOCK else pl.multiple_of(first * MOE_SUB, MOE_SUB)
            expert_rows(r0, cnt * MOE_SUB)


def _moe_call(sched, xs, wgu, bgu, wd, bd):
    n_items = sched[0].shape[0]
    _, d, de2 = wgu.shape
    de = de2 // 2
    wmap = lambda i, blk, exp, lo, hi, new, init, nxt, slot: (exp[i], 0, 0)
    xmap = lambda i, blk, exp, lo, hi, new, init, nxt, slot: (blk[i], 0)
    return pl.pallas_call(
        _moe_kernel,
        grid_spec=pltpu.PrefetchScalarGridSpec(
            num_scalar_prefetch=8,
            grid=(n_items,),
            in_specs=[
                pl.BlockSpec((MOE_BLOCK, d // 2), xmap),
                pl.BlockSpec(memory_space=pl.ANY),
                pl.BlockSpec((1, 1, de2), wmap),
                pl.BlockSpec(memory_space=pl.ANY),
                pl.BlockSpec((1, 1, d), wmap),
            ],
            out_specs=pl.BlockSpec((MOE_BLOCK, d // 2), xmap),
            scratch_shapes=[
                pltpu.VMEM((2, d, de2), F32), pltpu.VMEM((2, de, d), F32),
                pltpu.SemaphoreType.DMA((2, 2)),
            ],
        ),
        out_shape=jax.ShapeDtypeStruct(xs.shape, xs.dtype),
        compiler_params=pltpu.CompilerParams(
            dimension_semantics=("arbitrary",), vmem_limit_bytes=VMEM_LARGE),
        name="moe",
    )(*sched, xs, wgu, bgu, wd, bd)


def _moe_schedule(counts, n_rows):
    n_blocks = n_rows // MOE_BLOCK
    n_items = n_blocks + N_EXPERTS
    ends = jnp.cumsum(counts)
    starts = ends - counts
    first_blk = starts // MOE_BLOCK
    last_blk = (ends - 1) // MOE_BLOCK
    items_per = jnp.where(counts > 0, last_blk - first_blk + 1, 0)
    item_ends = jnp.cumsum(items_per)
    item_starts = item_ends - items_per
    total = item_ends[-1]
    it = jnp.arange(n_items, dtype=jnp.int32)
    live = it < total
    itc = jnp.minimum(it, total - 1)
    exp = jnp.sum((item_ends[None, :] <= itc[:, None]).astype(jnp.int32), axis=1)
    is_exp = exp[:, None] == jnp.arange(N_EXPERTS, dtype=jnp.int32)[None, :]
    pick = lambda table: jnp.sum(jnp.where(is_exp, table[None, :], 0), axis=1)
    blk = pick(first_blk) + itc - pick(item_starts)
    lo = jnp.clip(pick(starts) - blk * MOE_BLOCK, 0, MOE_BLOCK)
    hi = jnp.clip(pick(ends) - blk * MOE_BLOCK, 0, MOE_BLOCK)
    hi = jnp.where(live, hi, lo)
    prev_exp = jnp.concatenate([jnp.full((1,), -1, jnp.int32), exp[:-1]])
    prev_blk = jnp.concatenate([jnp.full((1,), -1, jnp.int32), blk[:-1]])
    new = jnp.logical_and(live, exp != prev_exp)
    init = jnp.logical_and(live, blk != prev_blk)
    slot = (jnp.cumsum(new.astype(jnp.int32)) - 1) % 2
    ar = jnp.arange(N_EXPERTS, dtype=jnp.int32)
    later = jnp.logical_and(counts[None, :] > 0, ar[None, :] > ar[:, None])
    next_exp = jnp.min(jnp.where(later, ar[None, :], N_EXPERTS), axis=1)
    next_exp = jnp.where(next_exp == N_EXPERTS, -1, next_exp)
    nxt = pick(next_exp)
    as_i32 = lambda a: a.astype(jnp.int32)
    return tuple(as_i32(a) for a in (blk, exp, lo, hi, new, init, nxt, slot)), starts


def _combine_kernel(*refs):
    yg_refs = refs[:TOP_K]
    x1_ref, gate_ref, mod_ref, fg_ref, o_ref = refs[TOP_K:]
    gt = gate_ref[...]
    gates = jnp.concatenate([gt, jnp.zeros((LANES - gt.shape[0], gt.shape[1]), F32)], axis=0).T
    y = gates[:, 0:1] * _unpack_bf16_pairs(yg_refs[0][...])
    for kk in range(1, TOP_K):
        y = y + gates[:, kk:kk + 1] * _unpack_bf16_pairs(yg_refs[kk][...])
    m = mod_ref[0]
    x2 = x1_ref[...] + m[5:6] * y
    o_ref[...] = _rms(x2) * fg_ref[...]


def _combine_call(yg, x1, gates, mod, fg, n_seq, part):
    t, d = x1.shape
    tt = COMBINE_TT
    tiles_per_seq = n_seq // tt
    n_tiles = yg.shape[0] // (TOP_K * tt)
    first = part * n_tiles
    tok = lambda i: (first + i, 0)
    slot_specs = [pl.BlockSpec((tt, d // 2), functools.partial(lambda kk, i: (kk * n_tiles + i, 0), kk))
                  for kk in range(TOP_K)]
    return pl.pallas_call(
        _combine_kernel,
        grid=(n_tiles,),
        in_specs=slot_specs + [
            pl.BlockSpec((tt, d), tok),
            pl.BlockSpec((SUBLANES, tt), lambda i: (0, first + i)),
            pl.BlockSpec((1, N_MOD, d), lambda i: ((first + i) // tiles_per_seq, 0, 0)),
            pl.BlockSpec((1, d), lambda i: (0, 0)),
        ],
        out_specs=pl.BlockSpec((tt, d), tok),
        out_shape=jax.ShapeDtypeStruct((t, d), F32),
        input_output_aliases={TOP_K: 0},
        compiler_params=pltpu.CompilerParams(
            dimension_semantics=("arbitrary",), vmem_limit_bytes=VMEM_SMALL),
        name="combine",
    )(*([yg] * TOP_K), x1, gates, mod, fg)


def _pad_cols(a, width):
    return jnp.pad(a, ((0, 0), (0, width - a.shape[1])))


def _rope_slab(w_rope):
    return jnp.pad(w_rope, ((0, 0), (QK_NOPE, LANES - QK_NOPE - QK_ROPE)))


def _prep_w_in(w_in):
    kr0 = Q_LORA + KV_LORA
    w_kr = w_in[:, kr0:kr0 + QK_ROPE]
    return jnp.concatenate(
        [w_in[:, :kr0], _rope_slab(w_kr), w_in[:, kr0 + QK_ROPE:]], axis=1).astype(BF16)


def _prep_w_uq(w_uq):
    per = QK_NOPE + QK_ROPE
    rows = w_uq.shape[0]
    w = w_uq.reshape(rows, N_HEADS, per)
    slab = jnp.pad(w, ((0, 0), (0, 0), (0, LANES - per)))
    return slab.reshape(rows, -1).T.astype(BF16)


def _prep_w_ukv(w_ukv):
    per = QK_NOPE + V_DIM
    rows = w_ukv.shape[0]
    w = w_ukv.reshape(rows, N_HEADS, per)
    wk = jnp.pad(w[:, :, :QK_NOPE], ((0, 0), (0, 0), (0, LANES - QK_NOPE))).reshape(rows, -1)
    wvt = jnp.pad(jnp.transpose(w[:, :, QK_NOPE:], (1, 2, 0)), ((0, 0), (0, VT_ROWS - V_DIM), (0, 0)))
    return wk.astype(BF16), wvt.reshape(N_HEADS * VT_ROWS, rows).astype(BF16)


def _rope_tables(n_lat):
    rows = n_lat // GRID_W
    nf = QK_ROPE // 4
    row = jnp.repeat(jnp.arange(rows, dtype=F32), GRID_W)
    col = jnp.tile(jnp.arange(GRID_W, dtype=F32), rows)
    freqs = ROPE_BASE ** (-jnp.arange(nf, dtype=F32) / nf)
    ang = jnp.concatenate([row[:, None] * freqs, col[:, None] * freqs], axis=-1)
    cos, sin = jnp.cos(ang), jnp.sin(ang)
    ones = jnp.ones((n_lat, QK_NOPE), F32)
    zeros = jnp.zeros((n_lat, QK_NOPE), F32)
    cs = _pad_cols(jnp.concatenate([ones, cos, cos], axis=1), LANES)
    sn = _pad_cols(jnp.concatenate([zeros, -sin, sin], axis=1), LANES)
    return cs, sn


def kernel(x, c, ctx, c_ctx, w_mod, b_mod, norm1_g, w_in, q_norm_g, kv_norm_g, w_uq, w_ukv, w_pool,
           pool_scale, w_out, norm2_g, router_w, router_b, w_gate_up, b_gate_up, w_down, b_down,
           final_g):
    bsz, n, d = x.shape
    n_ctx = ctx.shape[1]
    t = bsz * n
    assert w_mod.shape[0] == 1, "single-layer block"
    l = 0
    tp = t // COMBINE_PARTS
    assert bsz + 1 <= SUBLANES and d % (2 * LANES) == 0 and n_ctx % ATTN_KC == 0
    assert POOL_HALO == SUBLANES and MIX_TS % POOL_HALO == 0
    assert n % GRID_W == 0 and n % FRONT_TS == 0 and n % MIX_TS == 0 and n % ATTN_KC == 0
    assert n % (ATTN_TQ * ATTN_TILES) == 0 and (t * TOP_K) % MOE_BLOCK == 0 and MOE_BLOCK % MOE_SUB == 0
    assert t % COMBINE_PARTS == 0 and tp % COMBINE_TT == 0 and n % COMBINE_TT == 0 and t % SC_ROWS == 0

    cc = jnp.concatenate([c, c_ctx[None, :], jnp.zeros((SUBLANES - bsz - 1, d), F32)], axis=0)
    mod = _mod_call(cc, w_mod[l], b_mod[l][None, :]).reshape(SUBLANES, N_MOD, d)
    mod_lat, mod_ctx = mod[:bsz], mod[bsz:bsz + 1]

    win = _prep_w_in(w_in[l])
    wuqt = _prep_w_uq(w_uq[l])
    wk, wvt = _prep_w_ukv(w_ukv[l])
    cs, sn = _rope_tables(n)
    cs_ctx = jnp.broadcast_to((jnp.arange(LANES) < QK_NOPE + QK_ROPE).astype(F32), (n_ctx, LANES))
    sn_ctx = jnp.zeros((n_ctx, LANES), F32)
    g1 = norm1_g[l][None, :]
    qg = q_norm_g[l][None, :]
    kvg = kv_norm_g[l][None, :]

    qt, k, vt, u = _front_call(x, mod_lat, True, g1, win, qg, kvg, wuqt, wk, wvt, cs, sn,
                               is_ctx=False, ts=FRONT_TS)
    kc, vct = _front_call(ctx, mod_ctx, False, g1, win[:, Q_LORA:Q_LORA + 2 * LANES], qg, kvg, wuqt,
                          wk, wvt, cs_ctx, sn_ctx, is_ctx=True, ts=n_ctx)
    att = _attn_call(qt, k, vt, kc, vct)

    rw = _pad_cols(router_w[l], LANES)
    rw_hi = rw.astype(BF16)
    rw = jnp.concatenate([rw_hi, (rw - rw_hi.astype(F32)).astype(BF16)], axis=1)
    rb = jnp.concatenate([router_b[l], jnp.full((LANES - N_EXPERTS,), -jnp.inf, F32)])[None, :]
    x1, h2, gates, ridx, cnt = _mix_call(
        att, u, x, mod_lat, w_pool[l].astype(BF16), pool_scale[l][None, :], w_out[l].astype(BF16),
        norm2_g[l][None, :], rw, rb)

    counts = cnt[0, :N_EXPERTS].astype(jnp.int32)
    sched, starts = _moe_schedule(counts, t * TOP_K)
    is_exp = ridx[:TOP_K, None, :] == jnp.arange(N_EXPERTS, dtype=jnp.int32)[None, :, None]
    dest_t = jnp.sum(jnp.where(is_exp, starts[None, :, None], 0), axis=1) + ridx[TOP_K:]

    xs = _scatter_rows(h2.reshape(t, d // 2), dest_t.reshape(TOP_K, t // SC_ROWS, SC_ROWS), t * TOP_K)
    ys = _moe_call(sched, xs, w_gate_up[l], b_gate_up[l][:, None, :], w_down[l],
                   b_down[l][:, None, :])
    out = x1.reshape(t, d)
    for part in range(COMBINE_PARTS):
        idx = dest_t[:, part * tp:(part + 1) * tp].reshape(-1)
        yg = _gather_rows(ys, idx)
        out = _combine_call(yg, out, gates, mod_lat, final_g[None, :], n, part)
    return out.reshape(bsz, n, d)
```

```python
import functools
import math

import jax
import jax.numpy as jnp
from jax import lax
from jax.experimental import pallas as pl
from jax.experimental.pallas import tpu as pltpu
from jax.experimental.pallas import tpu_sc as plsc

F32 = jnp.float32
BF16 = jnp.bfloat16
HIGHEST = lax.Precision.HIGHEST

N_HEADS = 8
QK_NOPE = 64
QK_ROPE = 32
V_DIM = 64
Q_LORA = 256
KV_LORA = 128
GRID_W = 64
ROPE_BASE = 10000.0
POOL_WINDOWS = (2, 4, 8, 16)
POOL_CH = 128
N_EXPERTS = 32
TOP_K = 4
SWIGLU_LIMIT = 7.0
SWIGLU_ALPHA = 1.702
N_MOD = 6
EPS = 1e-6

LANES = 128
SUBLANES = 8
BF16_SUBLANES = 16
MIB = 1024 * 1024

POOL_HALO = max(POOL_WINDOWS) // 2
VT_ROWS = -(-(V_DIM + 1) // BF16_SUBLANES) * BF16_SUBLANES

FRONT_TS = 1024
ATTN_TQ = 512
ATTN_TILES = 4
ATTN_KC = 256
MIX_TS = 512
MOE_BLOCK = 512
MOE_SUB = 64
COMBINE_TT = 512
COMBINE_PARTS = 8
SC_ROWS = 128

VMEM_SMALL = 32 * MIB
VMEM_LARGE = 56 * MIB


def _dot(a, b, **kw):
    return jnp.dot(a, b, preferred_element_type=F32, **kw)


def _rms(x):
    return x * lax.rsqrt(jnp.mean(x * x, axis=-1, keepdims=True) + EPS)


def _pack_bf16_pairs(x):
    w = x.shape[1] // 2
    return pltpu.pack_elementwise([x[:, :w], x[:, w:]], packed_dtype=BF16)


def _unpack_bf16_pairs(words):
    halves = [pltpu.unpack_elementwise(words, index=h, packed_dtype=BF16, unpacked_dtype=F32)
              for h in range(2)]
    return jnp.concatenate(halves, axis=-1)


def _mod_kernel(c_ref, w_ref, b_ref, o_ref):
    c = c_ref[...]
    a = c / (1.0 + jnp.exp(-c))
    o_ref[...] = _dot(a, w_ref[...], precision=HIGHEST) + b_ref[...]


def _mod_call(cc, w_mod, b_mod):
    d = w_mod.shape[0]
    return pl.pallas_call(
        _mod_kernel,
        grid=(N_MOD,),
        in_specs=[
            pl.BlockSpec((SUBLANES, d), lambda j: (0, 0)),
            pl.BlockSpec((d, d), lambda j: (0, j)),
            pl.BlockSpec((1, d), lambda j: (0, j)),
        ],
        out_specs=pl.BlockSpec((SUBLANES, d), lambda j: (0, j)),
        out_shape=jax.ShapeDtypeStruct((SUBLANES, N_MOD * d), F32),
        compiler_params=pltpu.CompilerParams(
            dimension_semantics=("arbitrary",), vmem_limit_bytes=VMEM_SMALL),
        name="mod",
    )(cc, w_mod, b_mod)


def _front_kernel(x_ref, mod_ref, g1_ref, win_ref, qg_ref, kvg_ref, wuqt_ref, wk_ref, wvt_ref,
                  cs_ref, sn_ref, cst_ref, snt_ref, *out_refs, is_ctx, scale):
    x = x_ref[0]
    m = mod_ref[0]
    h = _rms(x) * g1_ref[...] * (1.0 + m[1:2]) + m[0:1]
    z = _dot(h.astype(BF16), win_ref[...])
    if is_ctx:
        k_ref, vt_ref = out_refs
        zkv = z
    else:
        qt_ref, k_ref, vt_ref, u_ref = out_refs
        zkv = z[:, Q_LORA:Q_LORA + 2 * LANES]
        u_ref[0] = z[:, Q_LORA + 2 * LANES:]

    ckv = _rms(zkv[:, :KV_LORA]) * kvg_ref[...]
    kk = _dot(ckv.astype(BF16), wk_ref[...])
    kra = zkv[:, LANES:2 * LANES]
    lane = lax.broadcasted_iota(jnp.int32, kra.shape, 1)
    half = QK_ROPE // 2
    partner = jnp.where(lane < QK_NOPE + half, pltpu.roll(kra, LANES - half, axis=1),
                        pltpu.roll(kra, half, axis=1))
    kr = kra * cs_ref[...] + partner * sn_ref[...]
    for hd in range(N_HEADS):
        k_ref[0, hd] = (kk[:, hd * LANES:(hd + 1) * LANES] + kr).astype(BF16)
    vt = _dot(wvt_ref[...], ckv.T.astype(BF16))
    row = lax.broadcasted_iota(jnp.int32, (VT_ROWS, 1), 0)
    ones = jnp.where(row == V_DIM, 1.0, 0.0)
    for hd in range(N_HEADS):
        vt_ref[0, hd] = (vt[hd * VT_ROWS:(hd + 1) * VT_ROWS] + ones).astype(BF16)

    if not is_ctx:
        cq = _rms(z[:, :Q_LORA]) * qg_ref[...]
        qqt = _dot(wuqt_ref[...], cq.T.astype(BF16))
        cst = cst_ref[...]
        snt = snt_ref[...]
        r1, r2, r3 = QK_NOPE, QK_NOPE + QK_ROPE // 2, QK_NOPE + QK_ROPE
        for hd in range(N_HEADS):
            qa = qqt[hd * LANES:(hd + 1) * LANES]
            qb = jnp.concatenate([qa[:r1], qa[r2:r3], qa[r1:r2], qa[r3:]], axis=0)
            qt_ref[0, hd] = ((qa * cst + qb * snt) * scale).astype(BF16)


def _front_call(xs, mod, mod_per_batch, g1, win, qg, kvg, wuqt, wk, wvt, cs, sn, *, is_ctx, ts):
    bsz, n, d = xs.shape
    nt = n // ts
    scale = math.log2(math.e) / math.sqrt(QK_NOPE + QK_ROPE)
    const = lambda b, i: (0, 0)
    mod_map = (lambda b, i: (b, 0, 0)) if mod_per_batch else (lambda b, i: (0, 0, 0))
    in_specs = [
        pl.BlockSpec((1, ts, d), lambda b, i: (b, i, 0)),
        pl.BlockSpec((1, N_MOD, d), mod_map),
        pl.BlockSpec((1, d), const),
        pl.BlockSpec(win.shape, const),
        pl.BlockSpec((1, Q_LORA), const),
        pl.BlockSpec((1, KV_LORA), const),
        pl.BlockSpec(wuqt.shape, const),
        pl.BlockSpec(wk.shape, const),
        pl.BlockSpec(wvt.shape, const),
        pl.BlockSpec((ts, LANES), lambda b, i: (i, 0)),
        pl.BlockSpec((ts, LANES), lambda b, i: (i, 0)),
        pl.BlockSpec((LANES, ts), lambda b, i: (0, i)),
        pl.BlockSpec((LANES, ts), lambda b, i: (0, i)),
    ]
    k_spec = pl.BlockSpec((1, N_HEADS, ts, LANES), lambda b, i: (b, 0, i, 0))
    k_shape = jax.ShapeDtypeStruct((bsz, N_HEADS, n, LANES), BF16)
    qt_spec = pl.BlockSpec((1, N_HEADS, LANES, ts), lambda b, i: (b, 0, 0, i))
    qt_shape = jax.ShapeDtypeStruct((bsz, N_HEADS, LANES, n), BF16)
    vt_spec = pl.BlockSpec((1, N_HEADS, VT_ROWS, ts), lambda b, i: (b, 0, 0, i))
    vt_shape = jax.ShapeDtypeStruct((bsz, N_HEADS, VT_ROWS, n), BF16)
    if is_ctx:
        out_specs = [k_spec, vt_spec]
        out_shape = [k_shape, vt_shape]
    else:
        pool_w = win.shape[1] - Q_LORA - 2 * LANES
        out_specs = [qt_spec, k_spec, vt_spec, pl.BlockSpec((1, ts, pool_w), lambda b, i: (b, i, 0))]
        out_shape = [qt_shape, k_shape, vt_shape, jax.ShapeDtypeStruct((bsz, n, pool_w), F32)]
    return pl.pallas_call(
        functools.partial(_front_kernel, is_ctx=is_ctx, scale=scale),
        grid=(bsz, nt),
        in_specs=in_specs,
        out_specs=out_specs,
        out_shape=out_shape,
        compiler_params=pltpu.CompilerParams(
            dimension_semantics=("arbitrary", "arbitrary"), vmem_limit_bytes=VMEM_LARGE),
        name="front_ctx" if is_ctx else "front",
    )(xs, mod, g1, win, qg, kvg, wuqt, wk, wvt, cs, sn, cs.T, sn.T)


def _attn_kernel(qt_ref, k_ref, vt_ref, kc_ref, vct_ref, o_ref, s_ref, mlc_ref, mxb_ref, oe_ref):
    n_ctx = kc_ref.shape[2]
    n_lat = k_ref.shape[2]
    chunks = [(None, 0, n_ctx)] + [(c * ATTN_KC, n_ctx + c * ATTN_KC, ATTN_KC)
                                   for c in range(n_lat // ATTN_KC)]
    sub = SUBLANES

    def q_cols(tile):
        if isinstance(tile, int):
            return pl.ds(tile * ATTN_TQ, ATTN_TQ)
        return pl.ds(pl.multiple_of(tile * ATTN_TQ, ATTN_TQ), ATTN_TQ)

    def score_chunk(tile, hd, buf, ci):
        off, soff, w = chunks[ci]
        keys = kc_ref[0, hd] if off is None else k_ref[0, hd, off:off + w, :]
        s = _dot(keys, qt_ref[0, hd, :, q_cols(tile)])
        s_ref[buf, soff:soff + w, :] = s
        mx = s[0:sub]
        for r in range(1, w // sub):
            mx = jnp.maximum(mx, s[r * sub:(r + 1) * sub])
        mlc_ref[buf, ci] = mx

    def row_max(buf):
        mx = mlc_ref[buf, 0]
        for ci in range(1, len(chunks)):
            mx = jnp.maximum(mx, mlc_ref[buf, ci])
        mxb_ref[...] = jnp.broadcast_to(jnp.max(mx, axis=0, keepdims=True), mx.shape)

    def weight_chunk(hd, buf, ci):
        off, soff, w = chunks[ci]
        p = jnp.exp2(s_ref[buf, soff:soff + w, :] - mxb_ref[0:1, :]).astype(BF16)
        vt = vct_ref[0, hd] if off is None else vt_ref[0, hd, :, off:off + w]
        return _dot(vt, p)

    def stage(hw, bw, scoring, bs):
        row_max(bw)
        acc = None
        for ci in range(len(chunks)):
            if scoring is not None:
                score_chunk(*scoring, bs, ci)
            part = weight_chunk(hw, bw, ci)
            acc = part if acc is None else acc + part
        return acc[0:V_DIM] / acc[V_DIM:V_DIM + 1]

    def write_pair(tile, j, ot_odd):
        pair_t = jnp.concatenate([oe_ref[...], ot_odd], axis=0)
        o_ref[0, j, q_cols(tile), :] = pair_t.T.astype(BF16)

    pairs_per_tile = N_HEADS // 2
    n_pairs = (qt_ref.shape[3] // ATTN_TQ) * pairs_per_tile

    def split(p):
        return p // pairs_per_tile, p % pairs_per_tile

    for ci in range(len(chunks)):
        score_chunk(0, 0, 0, ci)

    def head_pair(p, carry):
        tile, j = split(p)
        nxt_tile, nxt_j = split(p + 1)
        oe_ref[...] = stage(2 * j, 0, (tile, 2 * j + 1), 1)
        write_pair(tile, j, stage(2 * j + 1, 1, (nxt_tile, 2 * nxt_j), 0))
        return carry

    lax.fori_loop(0, n_pairs - 1, head_pair, 0)
    tile, j = split(n_pairs - 1)
    oe_ref[...] = stage(2 * j, 0, (tile, 2 * j + 1), 1)
    write_pair(tile, j, stage(2 * j + 1, 1, None, None))


def _attn_call(qt, k, vt, kc, vct):
    bsz, _, _, n = qt.shape
    n_ctx = kc.shape[2]
    tq = ATTN_TQ
    n_chunks = 1 + n // ATTN_KC
    per_batch = lambda b, i: (b, 0, 0, 0)
    resident = dict(pipeline_mode=pl.Buffered(1))
    rows_per_step = ATTN_TILES * tq
    return pl.pallas_call(
        _attn_kernel,
        grid=(bsz, n // rows_per_step),
        in_specs=[
            pl.BlockSpec((1, N_HEADS, LANES, rows_per_step), lambda b, i: (b, 0, 0, i)),
            pl.BlockSpec((1, N_HEADS, n, LANES), per_batch, **resident),
            pl.BlockSpec((1, N_HEADS, VT_ROWS, n), per_batch, **resident),
            pl.BlockSpec((1, N_HEADS, n_ctx, LANES), per_batch, **resident),
            pl.BlockSpec((1, N_HEADS, VT_ROWS, n_ctx), per_batch, **resident),
        ],
        out_specs=pl.BlockSpec((1, N_HEADS // 2, rows_per_step, LANES), lambda b, i: (b, 0, i, 0)),
        out_shape=jax.ShapeDtypeStruct((bsz, N_HEADS // 2, n, LANES), BF16),
        scratch_shapes=[
            pltpu.VMEM((2, n_ctx + n, tq), F32),
            pltpu.VMEM((2, n_chunks, SUBLANES, tq), F32),
            pltpu.VMEM((SUBLANES, tq), F32),
            pltpu.VMEM((V_DIM, tq), F32),
        ],
        compiler_params=pltpu.CompilerParams(
            dimension_semantics=("arbitrary", "arbitrary"), vmem_limit_bytes=VMEM_LARGE),
        name="attn",
    )(qt, k, vt, kc, vct)


def _mix_kernel(att_ref, u_ref, up_ref, un_ref, x_ref, mod_ref, wpool_ref, pscale_ref, wout_ref,
                g2_ref, rw_ref, rb_ref, tri_ref,
                x1_ref, h2_ref, gate_ref, ridx_ref, cnt_ref, ue_ref, base_ref, *, n_seq):
    b = pl.program_id(0)
    i = pl.program_id(1)
    n_tiles = pl.num_programs(1)
    ts = u_ref.shape[1]

    @pl.when(jnp.logical_and(b == 0, i == 0))
    def _():
        base_ref[...] = jnp.zeros_like(base_ref)

    u = u_ref[0]
    ue_ref[0:POOL_HALO] = jnp.where(i > 0, up_ref[0], 0.0)
    ue_ref[POOL_HALO:POOL_HALO + ts] = u
    ue_ref[POOL_HALO + ts:2 * POOL_HALO + ts] = jnp.where(i < n_tiles - 1, un_ref[0], 0.0)
    t = i * ts + lax.broadcasted_iota(jnp.int32, (ts, 1), 0)
    ys = []
    for g, w in enumerate(POOL_WINDOWS):
        half = w // 2
        lanes = slice(g * POOL_CH, (g + 1) * POOL_CH)
        ws = ue_ref[POOL_HALO - half:POOL_HALO - half + ts, lanes]
        for jj in range(-half + 1, half):
            ws = ws + ue_ref[POOL_HALO + jj:POOL_HALO + jj + ts, lanes]
        count = (jnp.minimum(t + half, n_seq) - jnp.maximum(t - half, 0)).astype(F32)
        mixed = (ws / count - u[:, lanes]).astype(BF16)
        ys.append(_dot(mixed, wpool_ref[g]))
    pool = jnp.concatenate(ys, axis=-1) * pscale_ref[...]

    cat = jnp.concatenate([att_ref[0, j] for j in range(N_HEADS // 2)] + [pool.astype(BF16)], axis=-1)
    m = mod_ref[0]
    x1 = x_ref[0] + m[2:3] * _dot(cat, wout_ref[...])
    x1_ref[0] = x1
    h2 = _rms(x1) * g2_ref[...] * (1.0 + m[4:5]) + m[3:4]
    h2_ref[0] = _pack_bf16_pairs(h2)

    h_hi = h2.astype(BF16)
    h_lo = (h2 - h_hi.astype(F32)).astype(BF16)
    hi_part = _dot(h_hi, rw_ref[...])
    logits = (hi_part[:, :LANES] + hi_part[:, LANES:] + _dot(h_lo, rw_ref[:, :LANES])) + rb_ref[...]
    lane = lax.broadcasted_iota(jnp.int32, logits.shape, 1).astype(F32)
    vals, idxs = [], []
    for _k in range(TOP_K):
        mv = jnp.max(logits, axis=-1, keepdims=True)
        ix = jnp.min(jnp.where(logits == mv, lane, float(LANES)), axis=-1, keepdims=True)
        vals.append(mv)
        idxs.append(ix)
        logits = jnp.where(lane == ix, -jnp.inf, logits)
    es = [jnp.exp(v - vals[0]) for v in vals]
    den = es[0] + es[1] + es[2] + es[3]

    onehot = jnp.zeros(lane.shape, F32)
    for ix in idxs:
        onehot = onehot + jnp.where(lane == ix, 1.0, 0.0)
    before = _dot(tri_ref[...], onehot.astype(BF16)) + base_ref[0:1, :]
    base_new = base_ref[0:1, :] + jnp.sum(onehot, axis=0, keepdims=True)
    base_ref[...] = jnp.broadcast_to(base_new, base_ref.shape)
    cnt_ref[...] = jnp.broadcast_to(base_new, cnt_ref.shape)

    route = jnp.zeros(lane.shape, F32)
    for kk in range(TOP_K):
        rank = jnp.sum(jnp.where(lane == idxs[kk], before, 0.0), axis=-1, keepdims=True)
        route = jnp.where(lane == float(kk), idxs[kk], route)
        route = jnp.where(lane == float(TOP_K + kk), rank, route)
        route = jnp.where(lane == float(2 * TOP_K + kk), es[kk] / den, route)
    route_t = route.T
    ridx_ref[...] = route_t[0:2 * TOP_K].astype(jnp.int32)
    gate_ref[...] = route_t[2 * TOP_K:2 * TOP_K + SUBLANES]


def _mix_call(att, u, x, mod, wpool, pscale, wout, g2, rw, rb):
    bsz, n, d = x.shape
    ts = MIX_TS
    nt = n // ts
    pool_w = u.shape[2]
    hb = ts // POOL_HALO
    n_halo_blocks = n // POOL_HALO
    const2 = lambda b, i: (0, 0)
    tok = lambda b, i: (b, i, 0)
    return pl.pallas_call(
        functools.partial(_mix_kernel, n_seq=n),
        grid=(bsz, nt),
        in_specs=[
            pl.BlockSpec((1, N_HEADS // 2, ts, LANES), lambda b, i: (b, 0, i, 0)),
            pl.BlockSpec((1, ts, pool_w), tok),
            pl.BlockSpec((1, POOL_HALO, pool_w), lambda b, i: (b, jnp.maximum(i * hb - 1, 0), 0)),
            pl.BlockSpec((1, POOL_HALO, pool_w),
                         lambda b, i: (b, jnp.minimum((i + 1) * hb, n_halo_blocks - 1), 0)),
            pl.BlockSpec((1, ts, d), tok),
            pl.BlockSpec((1, N_MOD, d), lambda b, i: (b, 0, 0)),
            pl.BlockSpec(wpool.shape, lambda b, i: (0, 0, 0)),
            pl.BlockSpec((1, pool_w), const2),
            pl.BlockSpec(wout.shape, const2),
            pl.BlockSpec((1, d), const2),
            pl.BlockSpec(rw.shape, const2),
            pl.BlockSpec((1, LANES), const2),
            pl.BlockSpec((ts, ts), const2),
        ],
        out_specs=[
            pl.BlockSpec((1, ts, d), tok),
            pl.BlockSpec((1, ts, d // 2), tok),
            pl.BlockSpec((SUBLANES, ts), lambda b, i: (0, b * nt + i)),
            pl.BlockSpec((2 * TOP_K, ts), lambda b, i: (0, b * nt + i)),
            pl.BlockSpec((SUBLANES, LANES), const2),
        ],
        out_shape=[
            jax.ShapeDtypeStruct((bsz, n, d), F32),
            jax.ShapeDtypeStruct((bsz, n, d // 2), jnp.uint32),
            jax.ShapeDtypeStruct((SUBLANES, bsz * n), F32),
            jax.ShapeDtypeStruct((2 * TOP_K, bsz * n), jnp.int32),
            jax.ShapeDtypeStruct((SUBLANES, LANES), F32),
        ],
        scratch_shapes=[
            pltpu.VMEM((ts + 2 * POOL_HALO, pool_w), F32),
            pltpu.VMEM((SUBLANES, LANES), F32),
        ],
        compiler_params=pltpu.CompilerParams(
            dimension_semantics=("arbitrary", "arbitrary"), vmem_limit_bytes=VMEM_SMALL),
        name="mix",
    )(att, u, u, u, x, mod, wpool, pscale, wout, g2, rw, rb, jnp.tri(ts, k=-1, dtype=BF16))


def _sc_mesh():
    return plsc.VectorSubcoreMesh(core_axis_name="core", subcore_axis_name="subcore")


def _sc_worker_id():
    info = plsc.get_sparse_core_info()
    return lax.axis_index("subcore") * info.num_cores + lax.axis_index("core")


def _sc_num_workers():
    info = plsc.get_sparse_core_info()
    return info.num_cores * info.num_subcores


def _scatter_rows(x, dest3, n_out):
    t, d = x.shape
    top_k, n_chunks, rows = dest3.shape
    assert n_chunks % _sc_num_workers() == 0 and n_chunks * rows == t
    per_worker = n_chunks // _sc_num_workers()

    @functools.partial(
        pl.kernel,
        out_type=jax.ShapeDtypeStruct((n_out, d), x.dtype),
        mesh=_sc_mesh(),
        scratch_types=[pltpu.VMEM((top_k, rows), jnp.int32), pltpu.VMEM((rows, d), x.dtype)],
        name="sc_scatter",
    )
    def scatter(x_hbm, i_hbm, o_hbm, idx_v, rows_v):
        wid = _sc_worker_id()

        @pl.loop(0, per_worker)
        def _(c):
            chunk = wid * per_worker + c
            for kk in range(top_k):
                pltpu.sync_copy(i_hbm.at[kk, chunk], idx_v.at[kk])
            pltpu.sync_copy(x_hbm.at[pl.ds(pl.multiple_of(chunk * rows, SUBLANES), rows)], rows_v)
            for kk in range(top_k):
                pltpu.sync_copy(rows_v, o_hbm.at[idx_v.at[kk]])

    return scatter(x, dest3)


def _gather_rows(y, idx):
    n = idx.shape[0]
    d = y.shape[1]
    rows = SC_ROWS
    assert n % (rows * _sc_num_workers()) == 0
    per_worker = n // _sc_num_workers()
    n_chunks = per_worker // rows

    @functools.partial(
        pl.kernel,
        out_type=jax.ShapeDtypeStruct((n, d), y.dtype),
        mesh=_sc_mesh(),
        scratch_types=[pltpu.VMEM((rows,), jnp.int32), pltpu.VMEM((rows, d), y.dtype)],
        name="sc_gather",
    )
    def gather(y_hbm, i_hbm, o_hbm, idx_v, rows_v):
        base = _sc_worker_id() * per_worker

        @pl.loop(0, n_chunks)
        def _(c):
            off = pl.multiple_of(base + c * rows, SUBLANES)
            pltpu.sync_copy(i_hbm.at[pl.ds(off, rows)], idx_v)
            pltpu.sync_copy(y_hbm.at[idx_v], rows_v)
            pltpu.sync_copy(rows_v, o_hbm.at[pl.ds(off, rows)])

    return gather(y, idx)


def _moe_kernel(blk_ref, exp_ref, lo_ref, hi_ref, new_ref, init_ref, next_ref, slot_ref, xs_ref,
                wgu_hbm, bgu_ref, wd_hbm, bd_ref, ys_ref, wgu_buf, wd_buf, sems):
    i = pl.program_id(0)
    slot = slot_ref[i]

    def weight_copies(e, s):
        return (pltpu.make_async_copy(wgu_hbm.at[e], wgu_buf.at[s], sems.at[s, 0]),
                pltpu.make_async_copy(wd_hbm.at[e], wd_buf.at[s], sems.at[s, 1]))

    @pl.when(i == 0)
    def _():
        for cp in weight_copies(exp_ref[0], slot):
            cp.start()

    @pl.when(new_ref[i] == 1)
    def _():
        for cp in weight_copies(exp_ref[i], slot):
            cp.wait()

        @pl.when(next_ref[i] >= 0)
        def _():
            for cp in weight_copies(next_ref[i], 1 - slot):
                cp.start()

    @pl.when(init_ref[i] == 1)
    def _():
        ys_ref[...] = jnp.zeros_like(ys_ref)

    lo = lo_ref[i]
    hi = hi_ref[i]

    def expert_rows(r0, n_rows):
        de = wd_buf.shape[1]
        x = _unpack_bf16_pairs(xs_ref[pl.ds(r0, n_rows), :])
        gu = _dot(x, wgu_buf[slot]) + bgu_ref[0]
        g = jnp.minimum(gu[:, :de], SWIGLU_LIMIT)
        lin = jnp.clip(gu[:, de:], -SWIGLU_LIMIT, SWIGLU_LIMIT)
        act = g / (1.0 + jnp.exp(-SWIGLU_ALPHA * g)) * (lin + 1.0)
        y = _dot(act, wd_buf[slot]) + bd_ref[0]
        row = r0 + lax.broadcasted_iota(jnp.int32, (n_rows, 1), 0)
        mine = jnp.logical_and(row >= lo, row < hi)
        ys_ref[pl.ds(r0, n_rows), :] = jnp.where(mine, _pack_bf16_pairs(y), ys_ref[pl.ds(r0, n_rows), :])

    first = lo // MOE_SUB
    pieces = jnp.where(hi > lo, (hi + MOE_SUB - 1) // MOE_SUB - first, 0)
    for cnt in range(1, MOE_BLOCK // MOE_SUB + 1):
        @pl.when(pieces == cnt)
        def _(cnt=cnt):
            r0 = 0 if cnt * MOE_SUB == MOE_BLOCK else pl.multiple_of(first * MOE_SUB, MOE_SUB)
            expert_rows(r0, cnt * MOE_SUB)


def _moe_call(sched, xs, wgu, bgu, wd, bd):
    n_items = sched[0].shape[0]
    _, d, de2 = wgu.shape
    de = de2 // 2
    wmap = lambda i, blk, exp, lo, hi, new, init, nxt, slot: (exp[i], 0, 0)
    xmap = lambda i, blk, exp, lo, hi, new, init, nxt, slot: (blk[i], 0)
    return pl.pallas_call(
        _moe_kernel,
        grid_spec=pltpu.PrefetchScalarGridSpec(
            num_scalar_prefetch=8,
            grid=(n_items,),
            in_specs=[
                pl.BlockSpec((MOE_BLOCK, d // 2), xmap),
                pl.BlockSpec(memory_space=pl.ANY),
                pl.BlockSpec((1, 1, de2), wmap),
                pl.BlockSpec(memory_space=pl.ANY),
                pl.BlockSpec((1, 1, d), wmap),
            ],
            out_specs=pl.BlockSpec((MOE_BLOCK, d // 2), xmap),
            scratch_shapes=[
                pltpu.VMEM((2, d, de2), F32), pltpu.VMEM((2, de, d), F32),
                pltpu.SemaphoreType.DMA((2, 2)),
            ],
        ),
        out_shape=jax.ShapeDtypeStruct(xs.shape, xs.dtype),
        compiler_params=pltpu.CompilerParams(
            dimension_semantics=("arbitrary",), vmem_limit_bytes=VMEM_LARGE),
        name="moe",
    )(*sched, xs, wgu, bgu, wd, bd)


def _moe_schedule(counts, n_rows):
    n_blocks = n_rows // MOE_BLOCK
    n_items = n_blocks + N_EXPERTS
    ends = jnp.cumsum(counts)
    starts = ends - counts
    first_blk = starts // MOE_BLOCK
    last_blk = (ends - 1) // MOE_BLOCK
    items_per = jnp.where(counts > 0, last_blk - first_blk + 1, 0)
    item_ends = jnp.cumsum(items_per)
    item_starts = item_ends - items_per
    total = item_ends[-1]
    it = jnp.arange(n_items, dtype=jnp.int32)
    live = it < total
    itc = jnp.minimum(it, total - 1)
    exp = jnp.sum((item_ends[None, :] <= itc[:, None]).astype(jnp.int32), axis=1)
    is_exp = exp[:, None] == jnp.arange(N_EXPERTS, dtype=jnp.int32)[None, :]
    pick = lambda table: jnp.sum(jnp.where(is_exp, table[None, :], 0), axis=1)
    blk = pick(first_blk) + itc - pick(item_starts)
    lo = jnp.clip(pick(starts) - blk * MOE_BLOCK, 0, MOE_BLOCK)
    hi = jnp.clip(pick(ends) - blk * MOE_BLOCK, 0, MOE_BLOCK)
    hi = jnp.where(live, hi, lo)
    prev_exp = jnp.concatenate([jnp.full((1,), -1, jnp.int32), exp[:-1]])
    prev_blk = jnp.concatenate([jnp.full((1,), -1, jnp.int32), blk[:-1]])
    new = jnp.logical_and(live, exp != prev_exp)
    init = jnp.logical_and(live, blk != prev_blk)
    slot = (jnp.cumsum(new.astype(jnp.int32)) - 1) % 2
    ar = jnp.arange(N_EXPERTS, dtype=jnp.int32)
    later = jnp.logical_and(counts[None, :] > 0, ar[None, :] > ar[:, None])
    next_exp = jnp.min(jnp.where(later, ar[None, :], N_EXPERTS), axis=1)
    next_exp = jnp.where(next_exp == N_EXPERTS, -1, next_exp)
    nxt = pick(next_exp)
    as_i32 = lambda a: a.astype(jnp.int32)
    return tuple(as_i32(a) for a in (blk, exp, lo, hi, new, init, nxt, slot)), starts


def _combine_kernel(*refs):
    yg_refs = refs[:TOP_K]
    x1_ref, gate_ref, mod_ref, fg_ref, o_ref = refs[TOP_K:]
    gt = gate_ref[...]
    gates = jnp.concatenate([gt, jnp.zeros((LANES - gt.shape[0], gt.shape[1]), F32)], axis=0).T
    y = gates[:, 0:1] * _unpack_bf16_pairs(yg_refs[0][...])
    for kk in range(1, TOP_K):
        y = y + gates[:, kk:kk + 1] * _unpack_bf16_pairs(yg_refs[kk][...])
    m = mod_ref[0]
    x2 = x1_ref[...] + m[5:6] * y
    o_ref[...] = _rms(x2) * fg_ref[...]


def _combine_call(yg, x1, gates, mod, fg, n_seq, part):
    t, d = x1.shape
    tt = COMBINE_TT
    tiles_per_seq = n_seq // tt
    n_tiles = yg.shape[0] // (TOP_K * tt)
    first = part * n_tiles
    tok = lambda i: (first + i, 0)
    slot_specs = [pl.BlockSpec((tt, d // 2), functools.partial(lambda kk, i: (kk * n_tiles + i, 0), kk))
                  for kk in range(TOP_K)]
    return pl.pallas_call(
        _combine_kernel,
        grid=(n_tiles,),
        in_specs=slot_specs + [
            pl.BlockSpec((tt, d), tok),
            pl.BlockSpec((SUBLANES, tt), lambda i: (0, first + i)),
            pl.BlockSpec((1, N_MOD, d), lambda i: ((first + i) // tiles_per_seq, 0, 0)),
            pl.BlockSpec((1, d), lambda i: (0, 0)),
        ],
        out_specs=pl.BlockSpec((tt, d), tok),
        out_shape=jax.ShapeDtypeStruct((t, d), F32),
        input_output_aliases={TOP_K: 0},
        compiler_params=pltpu.CompilerParams(
            dimension_semantics=("arbitrary",), vmem_limit_bytes=VMEM_SMALL),
        name="combine",
    )(*([yg] * TOP_K), x1, gates, mod, fg)


def _pad_cols(a, width):
    return jnp.pad(a, ((0, 0), (0, width - a.shape[1])))


def _rope_slab(w_rope):
    return jnp.pad(w_rope, ((0, 0), (QK_NOPE, LANES - QK_NOPE - QK_ROPE)))


def _prep_w_in(w_in):
    kr0 = Q_LORA + KV_LORA
    w_kr = w_in[:, kr0:kr0 + QK_ROPE]
    return jnp.concatenate(
        [w_in[:, :kr0], _rope_slab(w_kr), w_in[:, kr0 + QK_ROPE:]], axis=1).astype(BF16)


def _prep_w_uq(w_uq):
    per = QK_NOPE + QK_ROPE
    rows = w_uq.shape[0]
    w = w_uq.reshape(rows, N_HEADS, per)
    slab = jnp.pad(w, ((0, 0), (0, 0), (0, LANES - per)))
    return slab.reshape(rows, -1).T.astype(BF16)


def _prep_w_ukv(w_ukv):
    per = QK_NOPE + V_DIM
    rows = w_ukv.shape[0]
    w = w_ukv.reshape(rows, N_HEADS, per)
    wk = jnp.pad(w[:, :, :QK_NOPE], ((0, 0), (0, 0), (0, LANES - QK_NOPE))).reshape(rows, -1)
    wvt = jnp.pad(jnp.transpose(w[:, :, QK_NOPE:], (1, 2, 0)), ((0, 0), (0, VT_ROWS - V_DIM), (0, 0)))
    return wk.astype(BF16), wvt.reshape(N_HEADS * VT_ROWS, rows).astype(BF16)


def _rope_tables(n_lat):
    rows = n_lat // GRID_W
    nf = QK_ROPE // 4
    row = jnp.repeat(jnp.arange(rows, dtype=F32), GRID_W)
    col = jnp.tile(jnp.arange(GRID_W, dtype=F32), rows)
    freqs = ROPE_BASE ** (-jnp.arange(nf, dtype=F32) / nf)
    ang = jnp.concatenate([row[:, None] * freqs, col[:, None] * freqs], axis=-1)
    cos, sin = jnp.cos(ang), jnp.sin(ang)
    ones = jnp.ones((n_lat, QK_NOPE), F32)
    zeros = jnp.zeros((n_lat, QK_NOPE), F32)
    cs = _pad_cols(jnp.concatenate([ones, cos, cos], axis=1), LANES)
    sn = _pad_cols(jnp.concatenate([zeros, -sin, sin], axis=1), LANES)
    return cs, sn


def kernel(x, c, ctx, c_ctx, w_mod, b_mod, norm1_g, w_in, q_norm_g, kv_norm_g, w_uq, w_ukv, w_pool,
           pool_scale, w_out, norm2_g, router_w, router_b, w_gate_up, b_gate_up, w_down, b_down,
           final_g):
    bsz, n, d = x.shape
    n_ctx = ctx.shape[1]
    t = bsz * n
    assert w_mod.shape[0] == 1, "single-layer block"
    l = 0
    tp = t // COMBINE_PARTS
    assert bsz + 1 <= SUBLANES and d % (2 * LANES) == 0 and n_ctx % ATTN_KC == 0
    assert POOL_HALO == SUBLANES and MIX_TS % POOL_HALO == 0
    assert n % GRID_W == 0 and n % FRONT_TS == 0 and n % MIX_TS == 0 and n % ATTN_KC == 0
    assert n % (ATTN_TQ * ATTN_TILES) == 0 and (t * TOP_K) % MOE_BLOCK == 0 and MOE_BLOCK % MOE_SUB == 0
    assert t % COMBINE_PARTS == 0 and tp % COMBINE_TT == 0 and n % COMBINE_TT == 0 and t % SC_ROWS == 0

    cc = jnp.concatenate([c, c_ctx[None, :], jnp.zeros((SUBLANES - bsz - 1, d), F32)], axis=0)
    mod = _mod_call(cc, w_mod[l], b_mod[l][None, :]).reshape(SUBLANES, N_MOD, d)
    mod_lat, mod_ctx = mod[:bsz], mod[bsz:bsz + 1]

    win = _prep_w_in(w_in[l])
    wuqt = _prep_w_uq(w_uq[l])
    wk, wvt = _prep_w_ukv(w_ukv[l])
    cs, sn = _rope_tables(n)
    cs_ctx = jnp.broadcast_to((jnp.arange(LANES) < QK_NOPE + QK_ROPE).astype(F32), (n_ctx, LANES))
    sn_ctx = jnp.zeros((n_ctx, LANES), F32)
    g1 = norm1_g[l][None, :]
    qg = q_norm_g[l][None, :]
    kvg = kv_norm_g[l][None, :]

    qt, k, vt, u = _front_call(x, mod_lat, True, g1, win, qg, kvg, wuqt, wk, wvt, cs, sn,
                               is_ctx=False, ts=FRONT_TS)
    kc, vct = _front_call(ctx, mod_ctx, False, g1, win[:, Q_LORA:Q_LORA + 2 * LANES], qg, kvg, wuqt,
                          wk, wvt, cs_ctx, sn_ctx, is_ctx=True, ts=n_ctx)
    att = _attn_call(qt, k, vt, kc, vct)

    rw = _pad_cols(router_w[l], LANES)
    rw_hi = rw.astype(BF16)
    rw = jnp.concatenate([rw_hi, (rw - rw_hi.astype(F32)).astype(BF16)], axis=1)
    rb = jnp.concatenate([router_b[l], jnp.full((LANES - N_EXPERTS,), -jnp.inf, F32)])[None, :]
    x1, h2, gates, ridx, cnt = _mix_call(
        att, u, x, mod_lat, w_pool[l].astype(BF16), pool_scale[l][None, :], w_out[l].astype(BF16),
        norm2_g[l][None, :], rw, rb)

    counts = cnt[0, :N_EXPERTS].astype(jnp.int32)
    sched, starts = _moe_schedule(counts, t * TOP_K)
    is_exp = ridx[:TOP_K, None, :] == jnp.arange(N_EXPERTS, dtype=jnp.int32)[None, :, None]
    dest_t = jnp.sum(jnp.where(is_exp, starts[None, :, None], 0), axis=1) + ridx[TOP_K:]

    xs = _scatter_rows(h2.reshape(t, d // 2), dest_t.reshape(TOP_K, t // SC_ROWS, SC_ROWS), t * TOP_K)
    ys = _moe_call(sched, xs, w_gate_up[l], b_gate_up[l][:, None, :], w_down[l],
                   b_down[l][:, None, :])
    out = x1.reshape(t, d)
    for part in range(COMBINE_PARTS):
        idx = dest_t[:, part * tp:(part + 1) * tp].reshape(-1)
        yg = _gather_rows(ys, idx)
        out = _combine_call(yg, out, gates, mod_lat, final_g[None, :], n, part)
    return out.reshape(bsz, n, d)
```
